```python
import math
import jax, jax.numpy as jnp
from jax import lax
import numpy as np

D_MODEL = 2048
BATCH = 4
SEQ = 2048
DEPTH = 2
DEC_BATCH = 128
DEC_SEQ = 1
PAST_LEN = 16384
PAGE_SIZE = 128

GDN_HEADS = 8
GDN_DK = 128
GDN_DV = 128
GLA_HEADS = 4
GLA_DK = 128
GLA_DV = 256
GLA_RANK = 16
GLA_TAU = 16.0
SSD_HEADS = 32
SSD_HEADDIM = 64
SSD_GROUPS = 4
SSD_STATE = 128
SSD_INNER = SSD_HEADS * SSD_HEADDIM
SSD_NORM_GROUPS = SSD_GROUPS
CONV_WIDTH = 4
CHUNK = 64
D_FF = 5504
N_BRANCH = 3
N_MOD = 9
EPS = 1e-6

GDN_QK = GDN_HEADS * GDN_DK
GDN_V = GDN_HEADS * GDN_DV
GDN_CONV_DIM = 2 * GDN_QK + GDN_V
GLA_QK = GLA_HEADS * GLA_DK
GLA_V = GLA_HEADS * GLA_DV
SSD_BC = SSD_GROUPS * SSD_STATE
SSD_CONV_DIM = SSD_INNER + 2 * SSD_BC
IN_SPLITS = (GDN_CONV_DIM, GDN_V, GDN_HEADS, GDN_HEADS,
             GLA_QK, GLA_QK, GLA_V, GLA_RANK, GLA_V,
             SSD_INNER, SSD_CONV_DIM, SSD_HEADS,
             N_BRANCH * D_MODEL)
IN_TOTAL = sum(IN_SPLITS)

kernel_name = "hybrid_gdn_gla_ssd_macaron_adaln_step"


def split_cols(t, sizes):
    offs = [int(o) for o in np.cumsum(sizes)[:-1]]
    return jnp.split(t, offs, axis=-1)


def rms_norm(x, w):
    xf = x.astype(jnp.float32)
    y = xf * lax.rsqrt(jnp.mean(xf * xf, axis=-1, keepdims=True) + EPS)
    return (y * w.astype(jnp.float32)).astype(x.dtype)


def l2norm(x):
    return x * lax.rsqrt(jnp.sum(x * x, axis=-1, keepdims=True) + EPS)


def swiglu(h, wg, wu, wd):
    return (jax.nn.silu(h @ wg) * (h @ wu)) @ wd


def causal_conv(x, buf, w):
    L = x.shape[1]
    xp = jnp.concatenate([buf.astype(x.dtype), x], axis=1)
    y = xp[:, 0:L] * w[0]
    for j in range(1, CONV_WIDTH):
        y = y + xp[:, j:j + L] * w[j]
    return y, xp[:, xp.shape[1] - (CONV_WIDTH - 1):]


def to_chunks(t, chunk):
    b, l = t.shape[:2]
    return jnp.moveaxis(t.reshape((b, l // chunk, chunk) + t.shape[2:]), 1, 0)


def from_chunks(t):
    n, b, c = t.shape[:3]
    return jnp.moveaxis(t, 0, 1).reshape((b, n * c) + t.shape[3:])


def gated_delta_rule(q, k, v, beta, g, S0, chunk):
    tri = jnp.tril(jnp.ones((chunk, chunk), bool))
    strict = jnp.tril(jnp.ones((chunk, chunk), bool), -1)
    eye = jnp.eye(chunk, dtype=jnp.float32)
    dv = v.shape[-1]

    def step(S, inp):
        qc, kc, vc, bc, gc = inp
        gcum = jnp.cumsum(gc, axis=1).transpose(0, 2, 1)
        decay = jnp.exp(jnp.where(tri, gcum[..., :, None] - gcum[..., None, :], -jnp.inf))
        bt = bc.transpose(0, 2, 1)
        kk = jnp.einsum('bthd,bshd->bhts', kc, kc)
        a_mat = eye + jnp.where(strict, bt[..., :, None] * decay * kk, 0.0)
        rhs = jnp.concatenate([jnp.einsum('bshv,bhs->bhsv', vc, bt),
                               jnp.einsum('bshd,bhs->bhsd', kc, bt * jnp.exp(gcum))], axis=-1)
        sol = lax.linalg.triangular_solve(a_mat, rhs, left_side=True, lower=True, unit_diagonal=True)
        w = sol[..., :dv] - jnp.einsum('bhsd,bhdv->bhsv', sol[..., dv:], S)
        qk = jnp.einsum('bthd,bshd->bhts', qc, kc) * decay
        o = (jnp.einsum('bthd,bhdv->bhtv', qc, S) * jnp.exp(gcum)[..., None]
             + jnp.einsum('bhts,bhsv->bhtv', qk, w))
        g_last = gcum[..., -1:]
        S = S * jnp.exp(g_last)[..., None] + jnp.einsum('bshd,bhs,bhsv->bhdv', kc, jnp.exp(g_last - gcum), w)
        return S, o.transpose(0, 2, 1, 3)

    S, o = lax.scan(step, S0, tuple(to_chunks(t, chunk) for t in (q, k, v, beta, g)))
    return from_chunks(o), S


def gla_recurrence(q, k, v, log_a, S0, chunk):
    tri = jnp.tril(jnp.ones((chunk, chunk), bool))[None, :, :, None, None]

    def step(S, inp):
        qc, kc, vc, ac = inp
        b = jnp.cumsum(ac, axis=1)
        decay = jnp.exp(jnp.where(tri, b[:, :, None] - b[:, None, :], -jnp.inf))
        att = jnp.sum(qc[:, :, None] * kc[:, None, :] * decay, axis=-1)
        o = (jnp.einsum('btsh,bshv->bthv', att, vc)
             + jnp.einsum('bthd,bhdv->bthv', qc * jnp.exp(b), S))
        b_last = b[:, -1]
        S = S * jnp.exp(b_last)[..., None] + jnp.einsum('bshd,bshv->bhdv', kc * jnp.exp(b_last[:, None] - b), vc)
        return S, o

    S, o = lax.scan(step, S0, tuple(to_chunks(t, chunk) for t in (q, k, v, log_a)))
    return from_chunks(o), S


def ssd_recurrence(x, dt, A, Bm, Cm, h0, chunk):
    tri = jnp.tril(jnp.ones((chunk, chunk), bool))
    rep = x.shape[2] // Bm.shape[2]

    def step(h, inp):
        xc, dtc, bc, cc = inp
        acum = jnp.cumsum(dtc * A, axis=1).transpose(0, 2, 1)
        decay = jnp.exp(jnp.where(tri, acum[..., :, None] - acum[..., None, :], -jnp.inf))
        cb = jnp.repeat(jnp.einsum('btgn,bsgn->bgts', cc, bc), rep, axis=1)
        xdt = xc * dtc[..., None]
        ch = jnp.repeat(cc, rep, axis=2)
        bh = jnp.repeat(bc, rep, axis=2)
        y = (jnp.einsum('bhts,bshp->bthp', cb * decay, xdt)
             + jnp.einsum('bthn,bhpn->bthp', ch, h) * jnp.exp(acum).transpose(0, 2, 1)[..., None])
        a_last = acum[..., -1:]
        h = h * jnp.exp(a_last)[..., None] + jnp.einsum('bshn,bhs,bshp->bhpn', bh, jnp.exp(a_last - acum), xdt)
        return h, y

    h, y = lax.scan(step, h0, tuple(to_chunks(t, chunk) for t in (x, dt, Bm, Cm)))
    return from_chunks(y), h


def token_mixer(xn, p, conv_a_buf, s_a, s_b, conv_c_buf, s_c):
    f32 = jnp.float32
    bsz, L, _ = xn.shape
    (qkv_a, z_a, beta_a, dec_a, q_b, k_b, v_b, lr_b, r_b,
     z_c, xbc_c, dt_c, gates) = split_cols(xn @ p['w_in'], IN_SPLITS)

    qkv_a, conv_a_new = causal_conv(qkv_a, conv_a_buf, p['gdn_conv_w'])
    q_a, k_a, v_a = split_cols(jax.nn.silu(qkv_a.astype(f32)), (GDN_QK, GDN_QK, GDN_V))
    q_a = l2norm(q_a.reshape(bsz, L, GDN_HEADS, GDN_DK)) * (GDN_DK ** -0.5)
    k_a = l2norm(k_a.reshape(bsz, L, GDN_HEADS, GDN_DK))
    v_a = v_a.reshape(bsz, L, GDN_HEADS, GDN_DV)
    beta_a = jax.nn.sigmoid(beta_a.astype(f32))
    g_a = -jnp.exp(p['gdn_a_log'].astype(f32)) * jax.nn.softplus(dec_a.astype(f32) + p['gdn_dt_bias'].astype(f32))

    q_b = q_b.astype(f32).reshape(bsz, L, GLA_HEADS, GLA_DK) * (GLA_DK ** -0.5)
    k_b = k_b.astype(f32).reshape(bsz, L, GLA_HEADS, GLA_DK)
    v_b = v_b.astype(f32).reshape(bsz, L, GLA_HEADS, GLA_DV)
    la_b = (jax.nn.log_sigmoid((lr_b @ p['gla_w_gate'] + p['gla_b_gate']).astype(f32)) / GLA_TAU
            ).reshape(bsz, L, GLA_HEADS, GLA_DK)

    xbc_c, conv_c_new = causal_conv(xbc_c, conv_c_buf, p['ssd_conv_w'])
    xbc_c = jax.nn.silu((xbc_c + p['ssd_conv_b']).astype(f32))
    x_c, b_c, c_c = split_cols(xbc_c, (SSD_INNER, SSD_BC, SSD_BC))
    x_c = x_c.reshape(bsz, L, SSD_HEADS, SSD_HEADDIM)
    b_c = b_c.reshape(bsz, L, SSD_GROUPS, SSD_STATE)
    c_c = c_c.reshape(bsz, L, SSD_GROUPS, SSD_STATE)
    dt = jax.nn.softplus(dt_c.astype(f32) + p['ssd_dt_bias'].astype(f32))
    A = -jnp.exp(p['ssd_a_log'].astype(f32))

    chunk = min(CHUNK, L)
    Lp = -(-L // chunk) * chunk

    def pad(t):
        return jnp.pad(t, [(0, 0), (0, Lp - L)] + [(0, 0)] * (t.ndim - 2))

    o_a, s_a_new = gated_delta_rule(pad(q_a), pad(k_a), pad(v_a), pad(beta_a), pad(g_a), s_a.astype(f32), chunk)
    o_b, s_b_new = gla_recurrence(pad(q_b), pad(k_b), pad(v_b), pad(la_b), s_b.astype(f32), chunk)
    y_c, s_c_new = ssd_recurrence(pad(x_c), pad(dt), A, pad(b_c), pad(c_c), s_c.astype(f32), chunk)

    o_a = (rms_norm(o_a[:, :L], p['gdn_norm_w'])
           * jax.nn.silu(z_a.astype(f32).reshape(bsz, L, GDN_HEADS, GDN_DV))).reshape(bsz, L, GDN_V)
    o_b = (rms_norm(o_b[:, :L], p['gla_norm_w'])
           * jax.nn.silu(r_b.astype(f32).reshape(bsz, L, GLA_HEADS, GLA_DV))).reshape(bsz, L, GLA_V)
    y_c = (y_c[:, :L] + p['ssd_d'].astype(f32)[:, None] * x_c).reshape(bsz, L, SSD_INNER)
    y_c = y_c * jax.nn.silu(z_c.astype(f32))
    y_c = rms_norm(y_c.reshape(bsz, L, SSD_NORM_GROUPS, SSD_INNER // SSD_NORM_GROUPS),
                   p['ssd_norm_w'].reshape(SSD_NORM_GROUPS, SSD_INNER // SSD_NORM_GROUPS)).reshape(bsz, L, SSD_INNER)

    g_a_br, g_b_br, g_c_br = jnp.split(jax.nn.sigmoid(gates.astype(f32)), N_BRANCH, axis=-1)
    merged = (g_a_br * (o_a @ p['w_branch_gdn']) + g_b_br * (o_b @ p['w_branch_gla'])
              + g_c_br * (y_c @ p['w_branch_ssd']))
    out = (merged @ p['w_out']).astype(xn.dtype)
    new_state = (conv_a_new.astype(conv_a_buf.dtype), s_a_new.astype(s_a.dtype), s_b_new.astype(s_b.dtype),
                 conv_c_new.astype(conv_c_buf.dtype), s_c_new.astype(s_c.dtype))
    return out, new_state


def decoder_layer(x, c, p, st):
    mod = jax.nn.silu(c) @ p['w_ada'] + p['b_ada']
    sh1, sc1, gt1, sh2, sc2, gt2, sh3, sc3, gt3 = jnp.split(mod[:, None, :], N_MOD, axis=-1)
    h = rms_norm(x, p['norm1']) * (1.0 + sc1) + sh1
    x = x + 0.5 * gt1 * swiglu(h, p['ffn1_wg'], p['ffn1_wu'], p['ffn1_wd'])
    h = rms_norm(x, p['norm2']) * (1.0 + sc2) + sh2
    m, st = token_mixer(h, p, *st)
    x = x + gt2 * m
    h = rms_norm(x, p['norm3']) * (1.0 + sc3) + sh3
    x = x + 0.5 * gt3 * swiglu(h, p['ffn2_wg'], p['ffn2_wu'], p['ffn2_wd'])
    return x, st


def run_trunk(x, c, states, layer_params, final_norm):
    per_layer = []
    for l in range(DEPTH):
        x, st = decoder_layer(x, c, layer_params[l], tuple(s[l] for s in states))
        per_layer.append(st)
    new_states = tuple(jnp.stack([st[i] for st in per_layer]) for i in range(len(states)))
    return rms_norm(x, final_norm), new_states


def setup_inputs(seed: int = 0) -> dict:
    key = jax.random.key(seed)
    ks = jax.random.split(key, 48)
    f32 = jnp.float32
    L, D = DEPTH, D_MODEL

    def nrm(k, shape, scale):
        return scale * jax.random.normal(k, shape, f32)

    def gain(k, shape):
        return 1.0 + nrm(k, shape, 0.02)

    def dt_bias(k, shape):
        dt = jnp.exp(jax.random.uniform(k, shape, f32, math.log(1e-3), math.log(1e-1)))
        return dt + jnp.log(-jnp.expm1(-dt))

    def a_log(k, shape):
        return jnp.log(jax.random.uniform(k, shape, f32, 1.0, 16.0))

    return {
        "x_prompt": nrm(ks[0], (BATCH, SEQ, D), 1.0),
        "x_sample": nrm(ks[1], (DEC_BATCH, DEC_SEQ, D), 1.0),
        "state_gdn_conv": nrm(ks[2], (L, DEC_BATCH, CONV_WIDTH - 1, GDN_CONV_DIM), 1.0),
        "state_gdn": nrm(ks[3], (L, DEC_BATCH, GDN_HEADS, GDN_DK, GDN_DV), 0.1),
        "state_gla": nrm(ks[4], (L, DEC_BATCH, GLA_HEADS, GLA_DK, GLA_DV), 0.1),
        "state_ssd_conv": nrm(ks[5], (L, DEC_BATCH, CONV_WIDTH - 1, SSD_CONV_DIM), 1.0),
        "state_ssd": nrm(ks[6], (L, DEC_BATCH, SSD_HEADS, SSD_HEADDIM, SSD_STATE), 0.1),
        "c_prompt": nrm(ks[7], (BATCH, D), 1.0),
        "c_sample": nrm(ks[8], (DEC_BATCH, D), 1.0),
        "w_ada": nrm(ks[9], (L, D, N_MOD * D), 0.5 * D ** -0.5),
        "b_ada": nrm(ks[10], (L, N_MOD * D), 0.02),
        "norm1": gain(ks[11], (L, D)),
        "norm2": gain(ks[12], (L, D)),
        "norm3": gain(ks[13], (L, D)),
        "ffn1_wg": nrm(ks[14], (L, D, D_FF), D ** -0.5),
        "ffn1_wu": nrm(ks[15], (L, D, D_FF), D ** -0.5),
        "ffn1_wd": nrm(ks[16], (L, D_FF, D), D_FF ** -0.5),
        "ffn2_wg": nrm(ks[17], (L, D, D_FF), D ** -0.5),
        "ffn2_wu": nrm(ks[18], (L, D, D_FF), D ** -0.5),
        "ffn2_wd": nrm(ks[19], (L, D_FF, D), D_FF ** -0.5),
        "w_in": nrm(ks[20], (L, D, IN_TOTAL), D ** -0.5),
        "gdn_conv_w": nrm(ks[21], (L, CONV_WIDTH, GDN_CONV_DIM), CONV_WIDTH ** -0.5),
        "gdn_a_log": a_log(ks[22], (L, GDN_HEADS)),
        "gdn_dt_bias": dt_bias(ks[23], (L, GDN_HEADS)),
        "gdn_norm_w": gain(ks[24], (L, GDN_DV)),
        "gla_w_gate": nrm(ks[25], (L, GLA_RANK, GLA_QK), GLA_RANK ** -0.5),
        "gla_b_gate": nrm(ks[26], (L, GLA_QK), 0.02),
        "gla_norm_w": gain(ks[27], (L, GLA_DV)),
        "ssd_conv_w": nrm(ks[28], (L, CONV_WIDTH, SSD_CONV_DIM), CONV_WIDTH ** -0.5),
        "ssd_conv_b": nrm(ks[29], (L, SSD_CONV_DIM), 0.02),
        "ssd_a_log": a_log(ks[30], (L, SSD_HEADS)),
        "ssd_dt_bias": dt_bias(ks[31], (L, SSD_HEADS)),
        "ssd_d": gain(ks[32], (L, SSD_HEADS)),
        "ssd_norm_w": gain(ks[33], (L, SSD_INNER)),
        "w_branch_gdn": nrm(ks[34], (L, GDN_V, D), GDN_V ** -0.5),
        "w_branch_gla": nrm(ks[35], (L, GLA_V, D), GLA_V ** -0.5),
        "w_branch_ssd": nrm(ks[36], (L, SSD_INNER, D), SSD_INNER ** -0.5),
        "w_out": nrm(ks[37], (L, D, D), D ** -0.5),
        "final_norm": gain(ks[38], (D,)),
    }


def reference(x_prompt, x_sample, state_gdn_conv, state_gdn, state_gla, state_ssd_conv, state_ssd,
              c_prompt, c_sample, w_ada, b_ada, norm1, norm2, norm3,
              ffn1_wg, ffn1_wu, ffn1_wd, ffn2_wg, ffn2_wu, ffn2_wd,
              w_in, gdn_conv_w, gdn_a_log, gdn_dt_bias, gdn_norm_w,
              gla_w_gate, gla_b_gate, gla_norm_w,
              ssd_conv_w, ssd_conv_b, ssd_a_log, ssd_dt_bias, ssd_d, ssd_norm_w,
              w_branch_gdn, w_branch_gla, w_branch_ssd, w_out, final_norm):
    layer_params = [dict(w_ada=w_ada[l], b_ada=b_ada[l], norm1=norm1[l], norm2=norm2[l], norm3=norm3[l],
                         ffn1_wg=ffn1_wg[l], ffn1_wu=ffn1_wu[l], ffn1_wd=ffn1_wd[l],
                         ffn2_wg=ffn2_wg[l], ffn2_wu=ffn2_wu[l], ffn2_wd=ffn2_wd[l],
                         w_in=w_in[l], gdn_conv_w=gdn_conv_w[l], gdn_a_log=gdn_a_log[l],
                         gdn_dt_bias=gdn_dt_bias[l], gdn_norm_w=gdn_norm_w[l],
                         gla_w_gate=gla_w_gate[l], gla_b_gate=gla_b_gate[l], gla_norm_w=gla_norm_w[l],
                         ssd_conv_w=ssd_conv_w[l], ssd_conv_b=ssd_conv_b[l], ssd_a_log=ssd_a_log[l],
                         ssd_dt_bias=ssd_dt_bias[l], ssd_d=ssd_d[l], ssd_norm_w=ssd_norm_w[l],
                         w_branch_gdn=w_branch_gdn[l], w_branch_gla=w_branch_gla[l],
                         w_branch_ssd=w_branch_ssd[l], w_out=w_out[l])
                    for l in range(DEPTH)]
    sample_states = (state_gdn_conv, state_gdn, state_gla, state_ssd_conv, state_ssd)
    nb = x_prompt.shape[0]
    prompt_states = tuple(jnp.zeros((s.shape[0], nb) + s.shape[2:], x_prompt.dtype) for s in sample_states)
    y_prompt, (p_gdn_conv, p_gdn, p_gla, p_ssd_conv, p_ssd) = run_trunk(
        x_prompt, c_prompt, prompt_states, layer_params, final_norm)
    y_sample, (s_gdn_conv, s_gdn, s_gla, s_ssd_conv, s_ssd) = run_trunk(
        x_sample, c_sample, sample_states, layer_params, final_norm)
    return (y_prompt, y_sample, p_gdn_conv, p_gdn, p_gla, p_ssd_conv, p_ssd,
            s_gdn_conv, s_gdn, s_gla, s_ssd_conv, s_ssd)
```

```python
import functools

import jax
import jax.numpy as jnp
from jax import lax
from jax.experimental import pallas as pl
from jax.experimental.pallas import tpu as pltpu

f32 = jnp.float32
bf16 = jnp.bfloat16
HI = lax.Precision.HIGHEST

EPS = 1e-6
D_MODEL = 2048
N_MOD = 9
CHUNK = 64
CONV_W = 4
GDN_H, GDN_DK, GDN_DV = 8, 128, 128
GLA_H, GLA_DK, GLA_DV, GLA_RANK, GLA_TAU = 4, 128, 256, 16, 16.0
SSD_H, SSD_P, SSD_G, SSD_N = 32, 64, 4, 128
SSD_HG = SSD_H // SSD_G
GDN_QK = GDN_H * GDN_DK
GDN_V = GDN_H * GDN_DV
GDN_CONV = 2 * GDN_QK + GDN_V
GLA_QK = GLA_H * GLA_DK
GLA_V = GLA_H * GLA_DV
SSD_INNER = SSD_H * SSD_P
SSD_BC = SSD_G * SSD_N
SSD_CONV = SSD_INNER + 2 * SSD_BC

P_QKV_A = 0
P_Z_A = 3072
P_Q_B = 4096
P_K_B = 4608
P_V_B = 5120
P_R_B = 6144
P_Z_C = 7168
P_XBC = 9216
P_B_C = P_XBC + SSD_INNER
P_C_C = P_B_C + SSD_BC
P_GATES = 12288
P_SMALL = 18432
P_TOTAL = 18944
SM_BETA, SM_DEC, SM_LR, SM_DT = 0, 8, 16, 32
LANE = 128
FF_TILE = 512

VMEM_LIMIT = 56 * 1024 * 1024


def _cp(sem):
    return pltpu.CompilerParams(dimension_semantics=sem, vmem_limit_bytes=VMEM_LIMIT)


def _sigmoid(x):
    return jax.nn.sigmoid(x)


def _silu(x):
    return x * jax.nn.sigmoid(x)


def _rms(x, w):
    return x * lax.rsqrt(jnp.mean(x * x, axis=-1, keepdims=True) + EPS) * w


def _dot(a, b):
    return jnp.dot(a, b, preferred_element_type=f32)


def _dot_nt(a, b):
    return lax.dot_general(a, b, (((1,), (1,)), ((), ())), preferred_element_type=f32)


def _dot_tn(a, b):
    return lax.dot_general(a, b, (((0,), (0,)), ((), ())), preferred_element_type=f32)


def _dot_hi(a, b):
    return jnp.dot(a, b, precision=HI, preferred_element_type=f32)


def _tri_masks(c):
    row = lax.broadcasted_iota(jnp.int32, (c, c), 0)
    col = lax.broadcasted_iota(jnp.int32, (c, c), 1)
    return row >= col, row > col, row == col


def _lane_col(x, idx):
    lane = lax.broadcasted_iota(jnp.int32, x.shape, 1)
    return jnp.sum(jnp.where(lane == idx, x, 0.0), axis=1, keepdims=True)


def _scalar_vec(s):
    return jnp.full((1, 1), s, f32)


def _ada_kernel(c_ref, w_ref, b_ref, o_ref):
    s = _silu(c_ref[...]).astype(bf16)
    o_ref[...] = _dot(s, w_ref[...].astype(bf16)) + b_ref[...]


def _ada_mod(c_all, w_ada, b_ada):
    nl, dm, n = w_ada.shape
    r = c_all.shape[0]
    tn = 1024
    return pl.pallas_call(
        _ada_kernel,
        grid=(nl, n // tn),
        in_specs=[pl.BlockSpec((r, dm), lambda l, j: (0, 0)),
                  pl.BlockSpec((None, dm, tn), lambda l, j: (l, 0, j)),
                  pl.BlockSpec((None, 1, tn), lambda l, j: (l, 0, j))],
        out_specs=pl.BlockSpec((None, r, tn), lambda l, j: (l, 0, j)),
        out_shape=jax.ShapeDtypeStruct((nl, r, n), f32),
        compiler_params=_cp(("arbitrary", "arbitrary")),
        name="ada_mod",
    )(c_all, w_ada, b_ada.reshape(nl, 1, n))


def _mod_spec(per_row, tm, chunk):
    if per_row:
        return pl.BlockSpec((None, tm, D_MODEL), lambda b, i, j: (b, i, chunk))
    return pl.BlockSpec((None, 1, D_MODEL), lambda b, i, j: (b, 0, chunk))


def _ffn_kernel(x_ref, sh_ref, sc_ref, gt_ref, nw_ref, wg_ref, wu_ref, wd_ref, fw_ref, o_ref,
                h_ref, acc_ref, *, final):
    f = pl.program_id(2)

    @pl.when(f == 0)
    def _():
        y = _rms(x_ref[...], nw_ref[...])
        h_ref[...] = (y * (1.0 + sc_ref[...]) + sh_ref[...]).astype(bf16)
        acc_ref[...] = jnp.zeros_like(acc_ref)

    h = h_ref[...]
    g = _dot(h, wg_ref[...])
    u = _dot(h, wu_ref[...])
    a = (_silu(g) * u).astype(bf16)
    acc_ref[...] += _dot(a, wd_ref[...])

    @pl.when(f == pl.num_programs(2) - 1)
    def _():
        y = x_ref[...] + 0.5 * gt_ref[...] * acc_ref[...]
        if final:
            y = _rms(y, fw_ref[...])
        o_ref[...] = y


def _ffn(x, mod, k0, nw, wg, wu, wd, fw, per_row, tm, final):
    bx, tx, dm = x.shape
    fp = wg.shape[1]
    tf = FF_TILE
    return pl.pallas_call(
        functools.partial(_ffn_kernel, final=final),
        grid=(bx, tx // tm, fp // tf),
        in_specs=[pl.BlockSpec((None, tm, dm), lambda b, i, f: (b, i, 0)),
                  _mod_spec(per_row, tm, k0), _mod_spec(per_row, tm, k0 + 1), _mod_spec(per_row, tm, k0 + 2),
                  pl.BlockSpec((1, dm), lambda b, i, f: (0, 0)),
                  pl.BlockSpec((dm, tf), lambda b, i, f: (0, f)),
                  pl.BlockSpec((dm, tf), lambda b, i, f: (0, f)),
                  pl.BlockSpec((tf, dm), lambda b, i, f: (f, 0)),
                  pl.BlockSpec((1, dm), lambda b, i, f: (0, 0))],
        out_specs=pl.BlockSpec((None, tm, dm), lambda b, i, f: (b, i, 0)),
        out_shape=jax.ShapeDtypeStruct(x.shape, f32),
        scratch_shapes=[pltpu.VMEM((tm, dm), bf16), pltpu.VMEM((tm, dm), f32)],
        compiler_params=_cp(("parallel", "parallel", "arbitrary")),
        name="ffn",
    )(x, mod, mod, mod, nw, wg, wu, wd, fw)


def _inproj_kernel(x_ref, sh_ref, sc_ref, nw_ref, w_ref, o_ref, h_ref):
    @pl.when(pl.program_id(2) == 0)
    def _():
        y = _rms(x_ref[...], nw_ref[...])
        h_ref[...] = (y * (1.0 + sc_ref[...]) + sh_ref[...]).astype(bf16)

    o_ref[...] = _dot(h_ref[...], w_ref[...])


def _inproj(x, mod, nw, w, per_row, tm):
    bx, tx, dm = x.shape
    n = w.shape[1]
    tn = 512
    return pl.pallas_call(
        _inproj_kernel,
        grid=(bx, tx // tm, n // tn),
        in_specs=[pl.BlockSpec((None, tm, dm), lambda b, i, j: (b, i, 0)),
                  _mod_spec(per_row, tm, 3), _mod_spec(per_row, tm, 4),
                  pl.BlockSpec((1, dm), lambda b, i, j: (0, 0)),
                  pl.BlockSpec((dm, tn), lambda b, i, j: (0, j))],
        out_specs=pl.BlockSpec((None, tm, tn), lambda b, i, j: (b, i, j)),
        out_shape=jax.ShapeDtypeStruct((bx, tx, n), f32),
        scratch_shapes=[pltpu.VMEM((tm, dm), bf16)],
        compiler_params=_cp(("parallel", "parallel", "arbitrary")),
        name="inproj",
    )(x, mod, mod, nw, w)


def _conv_chunk(ext_ref, i, x_ref, w_ref):
    c = x_ref.shape[0]
    ext_ref[i, pl.ds(8, c), :] = x_ref[...]
    w = w_ref[...]
    y = ext_ref[i, pl.ds(5, c), :] * w[0:1]
    for j in range(1, CONV_W):
        y = y + ext_ref[i, pl.ds(5 + j, c), :] * w[j:j + 1]
    ext_ref[i, pl.ds(5, 3), :] = x_ref[pl.ds(c - 3, 3), :]
    return y


def _cumsum_rows(x, tri):
    return _dot_hi(jnp.where(tri, 1.0, 0.0).astype(f32), x)


def _gdn_kernel(alog_ref, dtb_ref, q_ref, k_ref, v_ref, z_ref, sm_ref, cq_ref, ck_ref, cv_ref,
                wq_ref, wk_ref, wv_ref, nw_ref, s0_ref, o_ref, s_ref, ext_ref):
    h = pl.program_id(1)
    c = q_ref.shape[0]

    @pl.when(pl.program_id(2) == 0)
    def _():
        s_ref[...] = s0_ref[...]
        ext_ref[0, pl.ds(5, 3), :] = cq_ref[...]
        ext_ref[1, pl.ds(5, 3), :] = ck_ref[...]
        ext_ref[2, pl.ds(5, 3), :] = cv_ref[...]

    q = _silu(_conv_chunk(ext_ref, 0, q_ref, wq_ref))
    k = _silu(_conv_chunk(ext_ref, 1, k_ref, wk_ref))
    v = _silu(_conv_chunk(ext_ref, 2, v_ref, wv_ref))
    q = q * lax.rsqrt(jnp.sum(q * q, axis=-1, keepdims=True) + EPS) * (GDN_DK ** -0.5)
    k = k * lax.rsqrt(jnp.sum(k * k, axis=-1, keepdims=True) + EPS)

    sm = sm_ref[...]
    beta = _sigmoid(_lane_col(sm, SM_BETA + h))
    dec = _lane_col(sm, SM_DEC + h)
    g = -jnp.exp(_scalar_vec(alog_ref[h])) * jax.nn.softplus(dec + dtb_ref[h])

    tri, strict, eye = _tri_masks(c)
    gc = _cumsum_rows(jnp.broadcast_to(g, (c, LANE)), tri)
    gc_row = gc.T[0:c, :]
    decay = jnp.exp(jnp.where(tri, gc[:, 0:c] - gc_row, -jnp.inf))
    qb = q.astype(bf16)
    kb = k.astype(bf16)
    kk = _dot_nt(kb, kb)
    nm = jnp.where(strict, beta * decay * kk, 0.0)
    inv = jnp.where(eye, 1.0, 0.0).astype(f32) - nm
    pw = _dot_hi(nm, nm)
    n_fac = max(1, (c - 1).bit_length() - 1)
    for i in range(n_fac):
        inv = inv + _dot_hi(inv, pw)
        if i + 1 < n_fac:
            pw = _dot_hi(pw, pw)
    eg = jnp.exp(gc)
    rhs = jnp.concatenate([v * beta, k * (beta * eg)], axis=1)
    sol = _dot_hi(inv, rhs)

    s = s_ref[...]
    sb = s.astype(bf16)
    w = sol[:, 0:GDN_DV] - _dot(sol[:, GDN_DV:].astype(bf16), sb)
    wb = w.astype(bf16)
    qk = _dot_nt(qb, kb) * decay
    o = _dot(qb, sb) * eg + _dot(qk.astype(bf16), wb)
    g_last = gc[c - 1:c, :]
    kd = (k * jnp.exp(g_last - gc)).astype(bf16)
    s_ref[...] = s * jnp.exp(g_last) + _dot_tn(kd, wb)

    o = _rms(o, nw_ref[...]) * _silu(z_ref[...])
    o_ref[...] = o.astype(bf16)


def _gdn_prompt(proj, conv_state, s0, conv_w, a_log, dt_bias, norm_w):
    b, t, _ = proj.shape
    c = CHUNK
    nt = t // c
    blk = lambda off: pl.BlockSpec((None, c, LANE), lambda bi, h, ti: (bi, ti, off // LANE + h))
    cst = lambda off: pl.BlockSpec((None, 3, LANE), lambda bi, h, ti: (bi, 0, off // LANE + h))
    cw = lambda off: pl.BlockSpec((CONV_W, LANE), lambda bi, h, ti: (0, off // LANE + h))
    smem = pl.BlockSpec(memory_space=pltpu.SMEM)
    st = pl.BlockSpec((None, None, GDN_DK, GDN_DV), lambda bi, h, ti: (bi, h, 0, 0))
    return pl.pallas_call(
        _gdn_kernel,
        grid=(b, GDN_H, nt),
        in_specs=[smem, smem,
                  blk(P_QKV_A), blk(P_QKV_A + GDN_QK), blk(P_QKV_A + 2 * GDN_QK), blk(P_Z_A),
                  pl.BlockSpec((None, c, LANE), lambda bi, h, ti: (bi, ti, P_SMALL // LANE)),
                  cst(0), cst(GDN_QK), cst(2 * GDN_QK),
                  cw(0), cw(GDN_QK), cw(2 * GDN_QK),
                  pl.BlockSpec((1, GDN_DV), lambda bi, h, ti: (0, 0)),
                  st],
        out_specs=[pl.BlockSpec((None, c, LANE), lambda bi, h, ti: (bi, ti, h)), st],
        out_shape=[jax.ShapeDtypeStruct((b, t, GDN_V), bf16),
                   jax.ShapeDtypeStruct((b, GDN_H, GDN_DK, GDN_DV), f32)],
        scratch_shapes=[pltpu.VMEM((3, 8 + c, LANE), f32)],
        compiler_params=_cp(("parallel", "parallel", "arbitrary")),
        name="gdn_prompt",
    )(a_log, dt_bias, proj, proj, proj, proj, proj, conv_state, conv_state, conv_state,
      conv_w, conv_w, conv_w, norm_w, s0)


GLA_SUB = 16


def _gla_kernel(q_ref, k_ref, v_ref, r_ref, sm_ref, wgate_ref, bgate_ref, nw_ref, s0_ref,
                o_ref, so_ref, st_ref):
    c = q_ref.shape[0]
    ti = pl.program_id(2)

    @pl.when(ti == 0)
    def _():
        st_ref[...] = s0_ref[...].T

    sm = sm_ref[...]
    lane = lax.broadcasted_iota(jnp.int32, sm.shape, 1)
    lr = jnp.where((lane >= SM_LR) & (lane < SM_LR + GLA_RANK), sm, 0.0).astype(bf16)
    la = jax.nn.log_sigmoid(_dot(lr, wgate_ref[...]) + bgate_ref[...]) / GLA_TAU

    q = q_ref[...] * (GLA_DK ** -0.5)
    k = k_ref[...]
    vb = v_ref[...].astype(bf16)
    tri, _, _ = _tri_masks(c)
    b = _cumsum_rows(la, tri)

    col = lax.broadcasted_iota(jnp.int32, (GLA_SUB, c), 1)
    row = lax.broadcasted_iota(jnp.int32, (GLA_SUB, c), 0)
    att_rows = []
    for i in range(c // GLA_SUB):
        lo = i * GLA_SUB
        b_i = b[lo:lo + GLA_SUB]
        q_i = q[lo:lo + GLA_SUB]
        b_top = b[lo:lo + 1]
        att = jnp.zeros((GLA_SUB, c), f32)
        if i > 0:
            q_t = (q_i * jnp.exp(b_i - b_top)).astype(bf16)
            k_t = (k * jnp.exp(jnp.minimum(b_top - b, 0.0))).astype(bf16)
            att = jnp.where(col < lo, _dot_nt(q_t, k_t), 0.0)
        for sl in range(GLA_SUB):
            sidx = lo + sl
            e = jnp.exp(jnp.minimum(b_i - b[sidx:sidx + 1], 0.0))
            p = jnp.sum(q_i * k[sidx:sidx + 1] * e, axis=1, keepdims=True)
            att = jnp.where((col == sidx) & (row >= sl), p, att)
        att_rows.append(att)
    att = jnp.concatenate(att_rows, axis=0)

    st = st_ref[...]
    o = _dot(att.astype(bf16), vb) + _dot_nt((q * jnp.exp(b)).astype(bf16), st.astype(bf16))
    b_last = b[c - 1:c]
    kd = (k * jnp.exp(b_last - b)).astype(bf16)
    st_new = st * jnp.exp(b_last) + _dot_tn(vb, kd)
    st_ref[...] = st_new

    @pl.when(ti == pl.num_programs(2) - 1)
    def _():
        so_ref[...] = st_new.T

    o = _rms(o, nw_ref[...]) * _silu(r_ref[...])
    o_ref[...] = o.astype(bf16)


def _gla_prompt(proj, s0, wgate_pad, bgate, norm_w):
    b, t, _ = proj.shape
    c = CHUNK
    nt = t // c
    st = pl.BlockSpec((None, None, GLA_DK, GLA_DV), lambda bi, h, ti: (bi, h, 0, 0))
    return pl.pallas_call(
        _gla_kernel,
        grid=(b, GLA_H, nt),
        in_specs=[pl.BlockSpec((None, c, GLA_DK), lambda bi, h, ti: (bi, ti, P_Q_B // GLA_DK + h)),
                  pl.BlockSpec((None, c, GLA_DK), lambda bi, h, ti: (bi, ti, P_K_B // GLA_DK + h)),
                  pl.BlockSpec((None, c, GLA_DV), lambda bi, h, ti: (bi, ti, P_V_B // GLA_DV + h)),
                  pl.BlockSpec((None, c, GLA_DV), lambda bi, h, ti: (bi, ti, P_R_B // GLA_DV + h)),
                  pl.BlockSpec((None, c, LANE), lambda bi, h, ti: (bi, ti, P_SMALL // LANE)),
                  pl.BlockSpec((LANE, GLA_DK), lambda bi, h, ti: (0, h)),
                  pl.BlockSpec((1, GLA_DK), lambda bi, h, ti: (0, h)),
                  pl.BlockSpec((1, GLA_DV), lambda bi, h, ti: (0, 0)),
                  st],
        out_specs=[pl.BlockSpec((None, c, GLA_DV), lambda bi, h, ti: (bi, ti, h)), st],
        out_shape=[jax.ShapeDtypeStruct((b, t, GLA_V), bf16),
                   jax.ShapeDtypeStruct((b, GLA_H, GLA_DK, GLA_DV), f32)],
        scratch_shapes=[pltpu.VMEM((GLA_DV, GLA_DK), f32)],
        compiler_params=_cp(("parallel", "parallel", "arbitrary")),
        name="gla_prompt",
    )(proj, proj, proj, proj, proj, wgate_pad, bgate, norm_w, s0)


def _ssd_kernel(alog_ref, dtb_ref, z_ref, x_ref, bm_ref, cm_ref, sm_ref, cx_ref, cb_ref, cc_ref,
                wx_ref, wb_ref, wc_ref, bx_ref, bb_ref, bc_ref, dvec_ref, nw_ref, h0_ref,
                o_ref, h_ref, extx_ref, extbc_ref, y_ref):
    g = pl.program_id(1)
    c = x_ref.shape[0]

    @pl.when(pl.program_id(2) == 0)
    def _():
        h_ref[...] = h0_ref[...]
        extx_ref[0, pl.ds(5, 3), :] = cx_ref[...]
        extbc_ref[0, pl.ds(5, 3), :] = cb_ref[...]
        extbc_ref[1, pl.ds(5, 3), :] = cc_ref[...]

    x = _silu(_conv_chunk(extx_ref, 0, x_ref, wx_ref) + bx_ref[...])
    bm = _silu(_conv_chunk(extbc_ref, 0, bm_ref, wb_ref) + bb_ref[...])
    cm = _silu(_conv_chunk(extbc_ref, 1, cm_ref, wc_ref) + bc_ref[...])
    bmb = bm.astype(bf16)
    cmb = cm.astype(bf16)

    sm = sm_ref[...]
    lane = lax.broadcasted_iota(jnp.int32, sm.shape, 1)
    tri, _, _ = _tri_masks(c)
    dt_all = jnp.zeros((c, LANE), f32)
    a_all = jnp.zeros((c, LANE), f32)
    for j in range(SSD_HG):
        hd = g * SSD_HG + j
        dt_j = jax.nn.softplus(_lane_col(sm, SM_DT + hd) + dtb_ref[hd])
        a_j = dt_j * (-jnp.exp(_scalar_vec(alog_ref[hd])))
        dt_all = jnp.where(lane == j, dt_j, dt_all)
        a_all = jnp.where(lane == j, a_j, a_all)
    ac = _cumsum_rows(a_all, tri)
    ac_t = ac.T
    cb = _dot_nt(cmb, bmb)

    for j in range(SSD_HG):
        ac_j = ac[:, j:j + 1]
        decay = jnp.exp(jnp.where(tri, ac_j - ac_t[j:j + 1, :], -jnp.inf))
        xdt = x[:, j * SSD_P:(j + 1) * SSD_P] * dt_all[:, j:j + 1]
        hj = h_ref[j]
        y = _dot((cb * decay).astype(bf16), xdt.astype(bf16)) + _dot_nt(cmb, hj.astype(bf16)) * jnp.exp(ac_j)
        a_last = ac[c - 1:c, j:j + 1]
        xs = (xdt * jnp.exp(a_last - ac_j)).astype(bf16)
        h_ref[j] = hj * jnp.exp(a_last) + _dot_tn(xs, bmb)
        y_ref[:, j * SSD_P:(j + 1) * SSD_P] = y

    y = (y_ref[...] + dvec_ref[...] * x) * _silu(z_ref[...])
    o_ref[...] = _rms(y, nw_ref[...]).astype(bf16)


def _ssd_prompt(proj, conv_state, h0, conv_w, conv_b, a_log, dt_bias, dvec, norm_w):
    b, t, _ = proj.shape
    c = CHUNK
    nt = t // c
    gw = SSD_HG * SSD_P
    smem = pl.BlockSpec(memory_space=pltpu.SMEM)
    st = pl.BlockSpec((None, SSD_HG, SSD_P, SSD_N), lambda bi, g, ti: (bi, g, 0, 0))
    nb = SSD_INNER // SSD_N
    return pl.pallas_call(
        _ssd_kernel,
        grid=(b, SSD_G, nt),
        in_specs=[smem, smem,
                  pl.BlockSpec((None, c, gw), lambda bi, g, ti: (bi, ti, P_Z_C // gw + g)),
                  pl.BlockSpec((None, c, gw), lambda bi, g, ti: (bi, ti, P_XBC // gw + g)),
                  pl.BlockSpec((None, c, SSD_N), lambda bi, g, ti: (bi, ti, P_B_C // SSD_N + g)),
                  pl.BlockSpec((None, c, SSD_N), lambda bi, g, ti: (bi, ti, P_C_C // SSD_N + g)),
                  pl.BlockSpec((None, c, LANE), lambda bi, g, ti: (bi, ti, P_SMALL // LANE)),
                  pl.BlockSpec((None, 3, gw), lambda bi, g, ti: (bi, 0, g)),
                  pl.BlockSpec((None, 3, SSD_N), lambda bi, g, ti: (bi, 0, nb + g)),
                  pl.BlockSpec((None, 3, SSD_N), lambda bi, g, ti: (bi, 0, nb + SSD_G + g)),
                  pl.BlockSpec((CONV_W, gw), lambda bi, g, ti: (0, g)),
                  pl.BlockSpec((CONV_W, SSD_N), lambda bi, g, ti: (0, nb + g)),
                  pl.BlockSpec((CONV_W, SSD_N), lambda bi, g, ti: (0, nb + SSD_G + g)),
                  pl.BlockSpec((1, gw), lambda bi, g, ti: (0, g)),
                  pl.BlockSpec((1, SSD_N), lambda bi, g, ti: (0, nb + g)),
                  pl.BlockSpec((1, SSD_N), lambda bi, g, ti: (0, nb + SSD_G + g)),
                  pl.BlockSpec((1, gw), lambda bi, g, ti: (0, g)),
                  pl.BlockSpec((1, gw), lambda bi, g, ti: (0, g)),
                  st],
        out_specs=[pl.BlockSpec((None, c, gw), lambda bi, g, ti: (bi, ti, g)), st],
        out_shape=[jax.ShapeDtypeStruct((b, t, SSD_INNER), bf16),
                   jax.ShapeDtypeStruct((b, SSD_H, SSD_P, SSD_N), f32)],
        scratch_shapes=[pltpu.VMEM((1, 8 + c, gw), f32), pltpu.VMEM((2, 8 + c, SSD_N), f32),
                        pltpu.VMEM((c, gw), f32)],
        compiler_params=_cp(("parallel", "parallel", "arbitrary")),
        name="ssd_prompt",
    )(a_log, dt_bias, proj, proj, proj, proj, proj, conv_state, conv_state, conv_state,
      conv_w, conv_w, conv_w, conv_b, conv_b, conv_b, dvec, norm_w, h0)


STEP_B = 8


def _conv_step(x_ref, c_ref, w_ref):
    w = w_ref[...]
    return c_ref[0] * w[0:1] + c_ref[1] * w[1:2] + c_ref[2] * w[2:3] + x_ref[...] * w[3:4]


def _gdn_step_kernel(alog_ref, dtb_ref, q_ref, k_ref, v_ref, z_ref, sm_ref, cq_ref, ck_ref, cv_ref,
                     wq_ref, wk_ref, wv_ref, nw_ref, s_ref, o_ref, so_ref):
    nb = q_ref.shape[0]
    qa = _silu(_conv_step(q_ref, cq_ref, wq_ref))
    ka = _silu(_conv_step(k_ref, ck_ref, wk_ref))
    va = _silu(_conv_step(v_ref, cv_ref, wv_ref))
    sm = sm_ref[...]
    z = z_ref[...]
    for h in range(GDN_H):
        sl = slice(h * GDN_DK, (h + 1) * GDN_DK)
        q = qa[:, sl]
        k = ka[:, sl]
        v = va[:, sl]
        q = q * lax.rsqrt(jnp.sum(q * q, axis=-1, keepdims=True) + EPS) * (GDN_DK ** -0.5)
        k = k * lax.rsqrt(jnp.sum(k * k, axis=-1, keepdims=True) + EPS)
        beta = _sigmoid(sm[:, SM_BETA + h:SM_BETA + h + 1])
        g = -jnp.exp(_scalar_vec(alog_ref[h])) * jax.nn.softplus(sm[:, SM_DEC + h:SM_DEC + h + 1] + dtb_ref[h])
        eg = jnp.exp(g)
        qk = jnp.sum(q * k, axis=-1, keepdims=True)
        q_t = q.T
        k_t = k.T
        rows = []
        for b in range(nb):
            s = s_ref[b, h]
            kcol = k_t[:, b:b + 1]
            ks = jnp.sum(s * kcol, axis=0, keepdims=True)
            qs = jnp.sum(s * q_t[:, b:b + 1], axis=0, keepdims=True)
            bt = beta[b:b + 1]
            e = eg[b:b + 1]
            w = bt * v[b:b + 1] - (bt * e) * ks
            rows.append(qs * e + qk[b:b + 1] * w)
            so_ref[b, h] = s * e + kcol * w
        o = jnp.concatenate(rows, axis=0)
        o = _rms(o, nw_ref[...]) * _silu(z[:, sl])
        o_ref[:, sl] = o.astype(bf16)


def _gdn_step(proj, conv_t, s0, conv_w, a_log, dt_bias, norm_w):
    n = proj.shape[0]
    nb = STEP_B
    blk = lambda off: pl.BlockSpec((nb, GDN_QK), lambda i: (i, off // GDN_QK))
    cst = lambda j: pl.BlockSpec((3, nb, GDN_QK), lambda i: (0, i, j))
    cw = lambda j: pl.BlockSpec((CONV_W, GDN_QK), lambda i: (0, j))
    smem = pl.BlockSpec(memory_space=pltpu.SMEM)
    st = pl.BlockSpec((nb, GDN_H, GDN_DK, GDN_DV), lambda i: (i, 0, 0, 0))
    return pl.pallas_call(
        _gdn_step_kernel,
        grid=(n // nb,),
        in_specs=[smem, smem,
                  blk(P_QKV_A), blk(P_QKV_A + GDN_QK), blk(P_QKV_A + 2 * GDN_QK), blk(P_Z_A),
                  pl.BlockSpec((nb, LANE), lambda i: (i, P_SMALL // LANE)),
                  cst(0), cst(1), cst(2), cw(0), cw(1), cw(2),
                  pl.BlockSpec((1, GDN_DV), lambda i: (0, 0)),
                  st],
        out_specs=[pl.BlockSpec((nb, GDN_V), lambda i: (i, 0)), st],
        out_shape=[jax.ShapeDtypeStruct((n, GDN_V), bf16),
                   jax.ShapeDtypeStruct(s0.shape, f32)],
        compiler_params=_cp(("parallel",)),
        name="gdn_step",
    )(a_log, dt_bias, proj, proj, proj, proj, proj, conv_t, conv_t, conv_t,
      conv_w, conv_w, conv_w, norm_w, s0)


def _gla_step_kernel(q_ref, k_ref, v_ref, r_ref, sm_ref, wgate_ref, bgate_ref, nw_ref, s_ref,
                     o_ref, so_ref):
    nb = q_ref.shape[0]
    sm = sm_ref[...]
    lane = lax.broadcasted_iota(jnp.int32, sm.shape, 1)
    lr = jnp.where((lane >= SM_LR) & (lane < SM_LR + GLA_RANK), sm, 0.0).astype(bf16)
    la_all = jax.nn.log_sigmoid(_dot(lr, wgate_ref[...]) + bgate_ref[...]) / GLA_TAU
    qa = q_ref[...] * (GLA_DK ** -0.5)
    ka = k_ref[...]
    va = v_ref[...]
    r = r_ref[...]
    for h in range(GLA_H):
        sk = slice(h * GLA_DK, (h + 1) * GLA_DK)
        sv = slice(h * GLA_DV, (h + 1) * GLA_DV)
        q = qa[:, sk]
        k = ka[:, sk]
        v = va[:, sv]
        e = jnp.exp(la_all[:, sk])
        qk = jnp.sum(q * k, axis=-1, keepdims=True)
        e_t = e.T
        k_t = k.T
        qe_t = (q * e).T
        rows = []
        for b in range(nb):
            s = s_ref[b, h]
            vrow = v[b:b + 1]
            rows.append(qk[b:b + 1] * vrow + jnp.sum(s * qe_t[:, b:b + 1], axis=0, keepdims=True))
            so_ref[b, h] = s * e_t[:, b:b + 1] + k_t[:, b:b + 1] * vrow
        o = jnp.concatenate(rows, axis=0)
        o = _rms(o, nw_ref[...]) * _silu(r[:, sv])
        o_ref[:, sv] = o.astype(bf16)


def _gla_step(proj, s0, wgate_pad, bgate, norm_w):
    n = proj.shape[0]
    nb = STEP_B
    st = pl.BlockSpec((nb, GLA_H, GLA_DK, GLA_DV), lambda i: (i, 0, 0, 0))
    return pl.pallas_call(
        _gla_step_kernel,
        grid=(n // nb,),
        in_specs=[pl.BlockSpec((nb, GLA_QK), lambda i: (i, P_Q_B // GLA_QK)),
                  pl.BlockSpec((nb, GLA_QK), lambda i: (i, P_K_B // GLA_QK)),
                  pl.BlockSpec((nb, GLA_V), lambda i: (i, P_V_B // GLA_V)),
                  pl.BlockSpec((nb, GLA_V), lambda i: (i, P_R_B // GLA_V)),
                  pl.BlockSpec((nb, LANE), lambda i: (i, P_SMALL // LANE)),
                  pl.BlockSpec((LANE, GLA_QK), lambda i: (0, 0)),
                  pl.BlockSpec((1, GLA_QK), lambda i: (0, 0)),
                  pl.BlockSpec((1, GLA_DV), lambda i: (0, 0)),
                  st],
        out_specs=[pl.BlockSpec((nb, GLA_V), lambda i: (i, 0)), st],
        out_shape=[jax.ShapeDtypeStruct((n, GLA_V), bf16),
                   jax.ShapeDtypeStruct(s0.shape, f32)],
        compiler_params=_cp(("parallel",)),
        name="gla_step",
    )(proj, proj, proj, proj, proj, wgate_pad, bgate, norm_w, s0)


def _ssd_step_kernel(alog_ref, dtb_ref, z_ref, x_ref, bm_ref, cm_ref, sm_ref, cx_ref, cb_ref, cc_ref,
                     wx_ref, wb_ref, wc_ref, bx_ref, bb_ref, bc_ref, dvec_ref, nw_ref, h_ref,
                     o_ref, ho_ref, y_ref):
    g = pl.program_id(1)
    nb = x_ref.shape[0]
    x = _silu(_conv_step(x_ref, cx_ref, wx_ref) + bx_ref[...])
    bm = _silu(_conv_step(bm_ref, cb_ref, wb_ref) + bb_ref[...])
    cm = _silu(_conv_step(cm_ref, cc_ref, wc_ref) + bc_ref[...])
    sm = sm_ref[...]
    cbdot = jnp.sum(cm * bm, axis=-1, keepdims=True)
    lane_b = lax.broadcasted_iota(jnp.int32, (SSD_P, nb), 1)
    for jp in range(SSD_HG // 2):
        x_t = x[:, jp * LANE:(jp + 1) * LANE].T
        halves = []
        for jj in range(2):
            j = 2 * jp + jj
            hd = g * SSD_HG + j
            dt = jax.nn.softplus(_lane_col(sm, SM_DT + hd) + dtb_ref[hd])
            ea = jnp.exp(dt * (-jnp.exp(_scalar_vec(alog_ref[hd]))))
            y_t = jnp.zeros((SSD_P, nb), f32)
            for b in range(nb):
                hh = h_ref[b, j]
                xdt = x_t[jj * SSD_P:(jj + 1) * SSD_P, b:b + 1] * dt[b:b + 1]
                eab = ea[b:b + 1]
                ycol = jnp.sum(hh * cm[b:b + 1], axis=-1, keepdims=True) * eab + cbdot[b:b + 1] * xdt
                ho_ref[b, j] = hh * eab + xdt * bm[b:b + 1]
                y_t = jnp.where(lane_b == b, ycol, y_t)
            halves.append(y_t)
        y_ref[:, jp * LANE:(jp + 1) * LANE] = jnp.concatenate(halves, axis=0).T
    y = (y_ref[...] + dvec_ref[...] * x) * _silu(z_ref[...])
    o_ref[...] = _rms(y, nw_ref[...]).astype(bf16)


def _ssd_step(proj, conv_t, h0, conv_w, conv_b, a_log, dt_bias, dvec, norm_w):
    n = proj.shape[0]
    nb = STEP_B
    gw = SSD_HG * SSD_P
    nbk = SSD_INNER // SSD_N
    smem = pl.BlockSpec(memory_space=pltpu.SMEM)
    st = pl.BlockSpec((nb, SSD_HG, SSD_P, SSD_N), lambda i, g: (i, g, 0, 0))
    return pl.pallas_call(
        _ssd_step_kernel,
        grid=(n // nb, SSD_G),
        in_specs=[smem, smem,
                  pl.BlockSpec((nb, gw), lambda i, g: (i, P_Z_C // gw + g)),
                  pl.BlockSpec((nb, gw), lambda i, g: (i, P_XBC // gw + g)),
                  pl.BlockSpec((nb, SSD_N), lambda i, g: (i, P_B_C // SSD_N + g)),
                  pl.BlockSpec((nb, SSD_N), lambda i, g: (i, P_C_C // SSD_N + g)),
                  pl.BlockSpec((nb, LANE), lambda i, g: (i, P_SMALL // LANE)),
                  pl.BlockSpec((3, nb, gw), lambda i, g: (0, i, g)),
                  pl.BlockSpec((3, nb, SSD_N), lambda i, g: (0, i, nbk + g)),
                  pl.BlockSpec((3, nb, SSD_N), lambda i, g: (0, i, nbk + SSD_G + g)),
                  pl.BlockSpec((CONV_W, gw), lambda i, g: (0, g)),
                  pl.BlockSpec((CONV_W, SSD_N), lambda i, g: (0, nbk + g)),
                  pl.BlockSpec((CONV_W, SSD_N), lambda i, g: (0, nbk + SSD_G + g)),
                  pl.BlockSpec((1, gw), lambda i, g: (0, g)),
                  pl.BlockSpec((1, SSD_N), lambda i, g: (0, nbk + g)),
                  pl.BlockSpec((1, SSD_N), lambda i, g: (0, nbk + SSD_G + g)),
                  pl.BlockSpec((1, gw), lambda i, g: (0, g)),
                  pl.BlockSpec((1, gw), lambda i, g: (0, g)),
                  st],
        out_specs=[pl.BlockSpec((nb, gw), lambda i, g: (i, g)), st],
        out_shape=[jax.ShapeDtypeStruct((n, SSD_INNER), bf16),
                   jax.ShapeDtypeStruct(h0.shape, f32)],
        scratch_shapes=[pltpu.VMEM((nb, gw), f32)],
        compiler_params=_cp(("parallel", "parallel")),
        name="ssd_step",
    )(a_log, dt_bias, proj, proj, proj, proj, proj, conv_t, conv_t, conv_t,
      conv_w, conv_w, conv_w, conv_b, conv_b, conv_b, dvec, norm_w, h0)


def _merge_kernel(oa_ref, ob_ref, oc_ref, ga_ref, gb_ref, gc_ref, wa_ref, wb_ref, wc_ref, o_ref):
    m = (_sigmoid(ga_ref[...]) * _dot(oa_ref[...], wa_ref[...])
         + _sigmoid(gb_ref[...]) * _dot(ob_ref[...], wb_ref[...])
         + _sigmoid(gc_ref[...]) * _dot(oc_ref[...], wc_ref[...]))
    o_ref[...] = m.astype(bf16)


def _merge(oa, ob, oc, proj, wa, wb, wc, tm):
    bx, tx, _ = oa.shape
    tn = 512
    gate = lambda k: pl.BlockSpec((None, tm, tn), lambda b, i, j: (b, i, (P_GATES + k * D_MODEL) // tn + j))
    act = lambda w: pl.BlockSpec((None, tm, w), lambda b, i, j: (b, i, 0))
    wsp = lambda w: pl.BlockSpec((w, tn), lambda b, i, j: (0, j))
    return pl.pallas_call(
        _merge_kernel,
        grid=(bx, tx // tm, D_MODEL // tn),
        in_specs=[act(GDN_V), act(GLA_V), act(SSD_INNER), gate(0), gate(1), gate(2),
                  wsp(GDN_V), wsp(GLA_V), wsp(SSD_INNER)],
        out_specs=pl.BlockSpec((None, tm, tn), lambda b, i, j: (b, i, j)),
        out_shape=jax.ShapeDtypeStruct((bx, tx, D_MODEL), bf16),
        compiler_params=_cp(("parallel", "parallel", "arbitrary")),
        name="merge",
    )(oa, ob, oc, proj, proj, proj, wa, wb, wc)


def _outproj_kernel(m_ref, x_ref, gt_ref, w_ref, o_ref):
    o_ref[...] = x_ref[...] + gt_ref[...] * _dot(m_ref[...], w_ref[...])


def _outproj(merged, x, mod, w, per_row, tm):
    bx, tx, dm = x.shape
    tn = 512
    nj = dm // tn
    if per_row:
        gspec = pl.BlockSpec((None, tm, tn), lambda b, i, j: (b, i, 5 * nj + j))
    else:
        gspec = pl.BlockSpec((None, 1, tn), lambda b, i, j: (b, 0, 5 * nj + j))
    return pl.pallas_call(
        _outproj_kernel,
        grid=(bx, tx // tm, nj),
        in_specs=[pl.BlockSpec((None, tm, dm), lambda b, i, j: (b, i, 0)),
                  pl.BlockSpec((None, tm, tn), lambda b, i, j: (b, i, j)),
                  gspec,
                  pl.BlockSpec((dm, tn), lambda b, i, j: (0, j))],
        out_specs=pl.BlockSpec((None, tm, tn), lambda b, i, j: (b, i, j)),
        out_shape=jax.ShapeDtypeStruct(x.shape, f32),
        compiler_params=_cp(("parallel", "parallel", "arbitrary")),
        name="outproj",
    )(merged, x, mod, w)


def _permute_w_in(w):
    d = w.shape[0]
    small = jnp.concatenate([w[:, 4096:4112], w[:, 6160:6176], w[:, 12320:12352]], axis=1)
    pad = jnp.zeros((d, P_TOTAL - P_SMALL - small.shape[1]), w.dtype)
    return jnp.concatenate([w[:, 0:4096], w[:, 4112:6160], w[:, 6176:12320], w[:, 12352:18496], small, pad],
                           axis=1).astype(bf16)


def _pad_ff(w, axis):
    f = w.shape[axis]
    fp = -(-f // FF_TILE) * FF_TILE
    widths = [(0, 0), (0, 0)]
    widths[axis] = (0, fp - f)
    return jnp.pad(w, widths).astype(bf16)


def _layer_params(l, p):
    row = lambda a: a[l].reshape(1, -1)
    wgate = jnp.zeros((LANE, GLA_QK), f32).at[SM_LR:SM_LR + GLA_RANK].set(p["gla_w_gate"][l]).astype(bf16)
    return dict(
        norm1=row(p["norm1"]), norm2=row(p["norm2"]), norm3=row(p["norm3"]),
        f1=(_pad_ff(p["ffn1_wg"][l], 1), _pad_ff(p["ffn1_wu"][l], 1), _pad_ff(p["ffn1_wd"][l], 0)),
        f2=(_pad_ff(p["ffn2_wg"][l], 1), _pad_ff(p["ffn2_wu"][l], 1), _pad_ff(p["ffn2_wd"][l], 0)),
        w_in=_permute_w_in(p["w_in"][l]),
        gdn_conv_w=p["gdn_conv_w"][l], gdn_a_log=p["gdn_a_log"][l], gdn_dt_bias=p["gdn_dt_bias"][l],
        gdn_norm_w=row(p["gdn_norm_w"]),
        gla_wgate=wgate, gla_bgate=row(p["gla_b_gate"]), gla_norm_w=row(p["gla_norm_w"]),
        ssd_conv_w=p["ssd_conv_w"][l], ssd_conv_b=row(p["ssd_conv_b"]), ssd_a_log=p["ssd_a_log"][l],
        ssd_dt_bias=p["ssd_dt_bias"][l], ssd_dvec=jnp.repeat(p["ssd_d"][l], SSD_P).reshape(1, -1),
        ssd_norm_w=row(p["ssd_norm_w"]),
        wa=p["w_branch_gdn"][l].astype(bf16), wb=p["w_branch_gla"][l].astype(bf16),
        wc=p["w_branch_ssd"][l].astype(bf16), w_out=p["w_out"][l].astype(bf16),
    )


def _new_conv_state(buf, raw):
    t = raw.shape[1]
    k = CONV_W - 1
    if t >= k:
        return raw[:, t - k:]
    return jnp.concatenate([buf[:, t:], raw], axis=1)


def _layer(x, mod, lp, st, per_row, tm, final_w):
    gdn_conv, s_gdn, s_gla, ssd_conv, s_ssd = st
    bx, tx, dm = x.shape
    x = _ffn(x, mod, 0, lp["norm1"], *lp["f1"], lp["norm1"], per_row, tm, False)
    proj = _inproj(x, mod, lp["norm2"], lp["w_in"], per_row, tm)
    if per_row:
        p2 = proj.reshape(tx, P_TOTAL)
        gct = jnp.swapaxes(gdn_conv, 0, 1)
        sct = jnp.swapaxes(ssd_conv, 0, 1)
        oa, s_gdn_n = _gdn_step(p2, gct, s_gdn, lp["gdn_conv_w"], lp["gdn_a_log"], lp["gdn_dt_bias"],
                                lp["gdn_norm_w"])
        ob, s_gla_n = _gla_step(p2, s_gla, lp["gla_wgate"], lp["gla_bgate"], lp["gla_norm_w"])
        oc, s_ssd_n = _ssd_step(p2, sct, s_ssd, lp["ssd_conv_w"], lp["ssd_conv_b"], lp["ssd_a_log"],
                                lp["ssd_dt_bias"], lp["ssd_dvec"], lp["ssd_norm_w"])
        oa, ob, oc = (o.reshape(1, tx, -1) for o in (oa, ob, oc))
        raw = p2.reshape(tx, 1, P_TOTAL)
        gdn_conv_n = _new_conv_state(gdn_conv, raw[:, :, P_QKV_A:P_QKV_A + GDN_CONV])
        ssd_conv_n = _new_conv_state(ssd_conv, raw[:, :, P_XBC:P_XBC + SSD_CONV])
    else:
        oa, s_gdn_n = _gdn_prompt(proj, gdn_conv, s_gdn, lp["gdn_conv_w"], lp["gdn_a_log"], lp["gdn_dt_bias"],
                                  lp["gdn_norm_w"])
        ob, s_gla_n = _gla_prompt(proj, s_gla, lp["gla_wgate"], lp["gla_bgate"], lp["gla_norm_w"])
        oc, s_ssd_n = _ssd_prompt(proj, ssd_conv, s_ssd, lp["ssd_conv_w"], lp["ssd_conv_b"], lp["ssd_a_log"],
                                  lp["ssd_dt_bias"], lp["ssd_dvec"], lp["ssd_norm_w"])
        gdn_conv_n = _new_conv_state(gdn_conv, proj[:, :, P_QKV_A:P_QKV_A + GDN_CONV])
        ssd_conv_n = _new_conv_state(ssd_conv, proj[:, :, P_XBC:P_XBC + SSD_CONV])
    merged = _merge(oa, ob, oc, proj, lp["wa"], lp["wb"], lp["wc"], tm)
    x = _outproj(merged, x, mod, lp["w_out"], per_row, tm)
    fw = lp["norm3"] if final_w is None else final_w
    x = _ffn(x, mod, 6, lp["norm3"], *lp["f2"], fw, per_row, tm, final_w is not None)
    return x, (gdn_conv_n, s_gdn_n, s_gla_n, ssd_conv_n, s_ssd_n)


def _trunk(x, mods, lps, states, per_row, tm, final_w):
    per_layer = []
    nl = len(lps)
    for l in range(nl):
        x, st = _layer(x, mods[l], lps[l], tuple(s[l] for s in states), per_row, tm,
                       final_w if l == nl - 1 else None)
        per_layer.append(st)
    return x, tuple(jnp.stack([st[i] for st in per_layer]) for i in range(len(states)))


def kernel(x_prompt, x_sample, state_gdn_conv, state_gdn, state_gla, state_ssd_conv, state_ssd, c_prompt, c_sample, w_ada, b_ada, norm1, norm2, norm3, ffn1_wg, ffn1_wu, ffn1_wd, ffn2_wg, ffn2_wu, ffn2_wd, w_in, gdn_conv_w, gdn_a_log, gdn_dt_bias, gdn_norm_w, gla_w_gate, gla_b_gate, gla_norm_w, ssd_conv_w, ssd_conv_b, ssd_a_log, ssd_dt_bias, ssd_d, ssd_norm_w, w_branch_gdn, w_branch_gla, w_branch_ssd, w_out, final_norm):
    p = dict(norm1=norm1, norm2=norm2, norm3=norm3,
             ffn1_wg=ffn1_wg, ffn1_wu=ffn1_wu, ffn1_wd=ffn1_wd, ffn2_wg=ffn2_wg, ffn2_wu=ffn2_wu, ffn2_wd=ffn2_wd,
             w_in=w_in, gdn_conv_w=gdn_conv_w, gdn_a_log=gdn_a_log, gdn_dt_bias=gdn_dt_bias, gdn_norm_w=gdn_norm_w,
             gla_w_gate=gla_w_gate, gla_b_gate=gla_b_gate, gla_norm_w=gla_norm_w,
             ssd_conv_w=ssd_conv_w, ssd_conv_b=ssd_conv_b, ssd_a_log=ssd_a_log, ssd_dt_bias=ssd_dt_bias,
             ssd_d=ssd_d, ssd_norm_w=ssd_norm_w,
             w_branch_gdn=w_branch_gdn, w_branch_gla=w_branch_gla, w_branch_ssd=w_branch_ssd, w_out=w_out)
    nl = w_ada.shape[0]
    bp, tp, dm = x_prompt.shape
    bs = x_sample.shape[0]
    assert x_sample.shape[1] == 1 and tp % CHUNK == 0 and bs % STEP_B == 0 and dm == D_MODEL
    lps = [_layer_params(l, p) for l in range(nl)]
    fw = final_norm.reshape(1, dm)

    rows = bp + bs
    rpad = -(-rows // 8) * 8
    c_all = jnp.concatenate([c_prompt, c_sample, jnp.zeros((rpad - rows, dm), f32)], axis=0)
    mod = _ada_mod(c_all, w_ada, b_ada)
    mod_p = [mod[l, :bp].reshape(bp, 1, N_MOD * dm) for l in range(nl)]
    mod_s = [mod[l, bp:rows].reshape(1, bs, N_MOD * dm) for l in range(nl)]

    sample_states = (state_gdn_conv, state_gdn, state_gla, state_ssd_conv, state_ssd)
    prompt_states = tuple(jnp.zeros((s.shape[0], bp) + s.shape[2:], x_prompt.dtype) for s in sample_states)
    tm_p = 512 if tp % 512 == 0 else CHUNK
    y_p, st_p = _trunk(x_prompt, mod_p, lps, prompt_states, False, tm_p, fw)
    y_s, st_s = _trunk(x_sample.reshape(1, bs, dm), mod_s, lps, sample_states, True, bs, fw)
    return (y_p, y_s.reshape(bs, 1, dm)) + st_p + st_s
```

```python
import functools

import jax
import jax.numpy as jnp
from jax import lax
from jax.experimental import pallas as pl
from jax.experimental.pallas import tpu as pltpu

f32 = jnp.float32
bf16 = jnp.bfloat16
HI = lax.Precision.HIGHEST

EPS = 1e-6
D_MODEL = 2048
N_MOD = 9
CHUNK = 64
CONV_W = 4
GDN_H, GDN_DK, GDN_DV = 8, 128, 128
GLA_H, GLA_DK, GLA_DV, GLA_RANK, GLA_TAU = 4, 128, 256, 16, 16.0
SSD_H, SSD_P, SSD_G, SSD_N = 32, 64, 4, 128
SSD_HG = SSD_H // SSD_G
GDN_QK = GDN_H * GDN_DK
GDN_V = GDN_H * GDN_DV
GDN_CONV = 2 * GDN_QK + GDN_V
GLA_QK = GLA_H * GLA_DK
GLA_V = GLA_H * GLA_DV
SSD_INNER = SSD_H * SSD_P
SSD_BC = SSD_G * SSD_N
SSD_CONV = SSD_INNER + 2 * SSD_BC
SSD_GW = SSD_HG * SSD_P

P_QKV_A = 0
P_XBC = 3072
P_Z_C = 6144
P_GATES = 8192
P_Z_A = 14336
P_Q_B = 15360
P_K_B = 15872
P_V_B = 16384
P_R_B = 17408
P_SMALL = 18432
P_TOTAL = 18944
SM_BETA, SM_DEC, SM_LR, SM_DT = 0, 8, 16, 32
LANE = 128
FF_TILE = 512

VMEM_LIMIT = 56 * 1024 * 1024


def _cp(sem):
    return pltpu.CompilerParams(dimension_semantics=sem, vmem_limit_bytes=VMEM_LIMIT)


def _sigmoid(x):
    return jax.nn.sigmoid(x)


def _silu(x):
    return x * jax.nn.sigmoid(x)


def _rms(x, w):
    return x * lax.rsqrt(jnp.mean(x * x, axis=-1, keepdims=True) + EPS) * w


def _dot(a, b):
    return jnp.dot(a, b, preferred_element_type=f32)


def _dot_nt(a, b):
    return lax.dot_general(a, b, (((1,), (1,)), ((), ())), preferred_element_type=f32)


def _dot_tn(a, b):
    return lax.dot_general(a, b, (((0,), (0,)), ((), ())), preferred_element_type=f32)


def _dot_hi(a, b):
    return jnp.dot(a, b, precision=HI, preferred_element_type=f32)


def _tri_masks(c):
    row = lax.broadcasted_iota(jnp.int32, (c, c), 0)
    col = lax.broadcasted_iota(jnp.int32, (c, c), 1)
    return row >= col, row > col, row == col


def _lane_col(x, idx):
    lane = lax.broadcasted_iota(jnp.int32, x.shape, 1)
    return jnp.sum(jnp.where(lane == idx, x, 0.0), axis=1, keepdims=True)


def _scalar_vec(s):
    return jnp.full((1, 1), s, f32)


def _ada_kernel(c_ref, w_ref, b_ref, o_ref):
    s = _silu(c_ref[...]).astype(bf16)
    o_ref[...] = _dot(s, w_ref[...].astype(bf16)) + b_ref[...]


def _ada_mod(c_all, w_ada, b_ada):
    nl, dm, n = w_ada.shape
    r = c_all.shape[0]
    tn = 1024
    return pl.pallas_call(
        _ada_kernel,
        grid=(nl, n // tn),
        in_specs=[pl.BlockSpec((r, dm), lambda l, j: (0, 0)),
                  pl.BlockSpec((None, dm, tn), lambda l, j: (l, 0, j)),
                  pl.BlockSpec((None, 1, tn), lambda l, j: (l, 0, j))],
        out_specs=pl.BlockSpec((None, r, tn), lambda l, j: (l, 0, j)),
        out_shape=jax.ShapeDtypeStruct((nl, r, n), f32),
        compiler_params=_cp(("arbitrary", "arbitrary")),
        name="ada_mod",
    )(c_all, w_ada, b_ada.reshape(nl, 1, n))


def _mod_spec(per_row, tm, chunk):
    if per_row:
        return pl.BlockSpec((None, tm, D_MODEL), lambda b, i, j: (b, i, chunk))
    return pl.BlockSpec((None, 1, D_MODEL), lambda b, i, j: (b, 0, chunk))


def _ffn_kernel(x_ref, sh_ref, sc_ref, gt_ref, nw_ref, wg_ref, wu_ref, wd_ref, fw_ref, o_ref,
                h_ref, acc_ref, *, final):
    f = pl.program_id(2)

    @pl.when(f == 0)
    def _():
        y = _rms(x_ref[...], nw_ref[...])
        h_ref[...] = (y * (1.0 + sc_ref[...]) + sh_ref[...]).astype(bf16)
        acc_ref[...] = jnp.zeros_like(acc_ref)

    h = h_ref[...]
    g = _dot(h, wg_ref[...])
    u = _dot(h, wu_ref[...])
    a = (_silu(g) * u).astype(bf16)
    acc_ref[...] += _dot(a, wd_ref[...])

    @pl.when(f == pl.num_programs(2) - 1)
    def _():
        y = x_ref[...] + 0.5 * gt_ref[...] * acc_ref[...]
        if final:
            y = _rms(y, fw_ref[...])
        o_ref[...] = y


def _ffn(x, mod, k0, nw, wg, wu, wd, fw, per_row, tm, final):
    bx, tx, dm = x.shape
    fp = wg.shape[1]
    tf = FF_TILE
    return pl.pallas_call(
        functools.partial(_ffn_kernel, final=final),
        grid=(bx, tx // tm, fp // tf),
        in_specs=[pl.BlockSpec((None, tm, dm), lambda b, i, f: (b, i, 0)),
                  _mod_spec(per_row, tm, k0), _mod_spec(per_row, tm, k0 + 1), _mod_spec(per_row, tm, k0 + 2),
                  pl.BlockSpec((1, dm), lambda b, i, f: (0, 0)),
                  pl.BlockSpec((dm, tf), lambda b, i, f: (0, f)),
                  pl.BlockSpec((dm, tf), lambda b, i, f: (0, f)),
                  pl.BlockSpec((tf, dm), lambda b, i, f: (f, 0)),
                  pl.BlockSpec((1, dm), lambda b, i, f: (0, 0))],
        out_specs=pl.BlockSpec((None, tm, dm), lambda b, i, f: (b, i, 0)),
        out_shape=jax.ShapeDtypeStruct(x.shape, f32),
        scratch_shapes=[pltpu.VMEM((tm, dm), bf16), pltpu.VMEM((tm, dm), f32)],
        compiler_params=_cp(("parallel", "parallel", "arbitrary")),
        name="ffn",
    )(x, mod, mod, mod, nw, wg, wu, wd, fw)


def _inproj_kernel(x_ref, sh_ref, sc_ref, nw_ref, w_ref, o_ref, h_ref):
    @pl.when(pl.program_id(2) == 0)
    def _():
        y = _rms(x_ref[...], nw_ref[...])
        h_ref[...] = (y * (1.0 + sc_ref[...]) + sh_ref[...]).astype(bf16)

    o_ref[...] = _dot(h_ref[...], w_ref[...])


def _inproj(x, mod, nw, w, per_row, tm):
    bx, tx, dm = x.shape
    n = w.shape[1]
    tn = 512
    return pl.pallas_call(
        _inproj_kernel,
        grid=(bx, tx // tm, n // tn),
        in_specs=[pl.BlockSpec((None, tm, dm), lambda b, i, j: (b, i, 0)),
                  _mod_spec(per_row, tm, 3), _mod_spec(per_row, tm, 4),
                  pl.BlockSpec((1, dm), lambda b, i, j: (0, 0)),
                  pl.BlockSpec((dm, tn), lambda b, i, j: (0, j))],
        out_specs=pl.BlockSpec((None, tm, tn), lambda b, i, j: (b, i, j)),
        out_shape=jax.ShapeDtypeStruct((bx, tx, n), f32),
        scratch_shapes=[pltpu.VMEM((tm, dm), bf16)],
        compiler_params=_cp(("parallel", "parallel", "arbitrary")),
        name="inproj",
    )(x, mod, mod, nw, w)


def _conv_load(ext_ref, x_ref, cst_ref, first):
    c = x_ref.shape[0]

    @pl.when(first)
    def _():
        ext_ref[pl.ds(5, CONV_W - 1), :] = cst_ref[...]

    ext_ref[pl.ds(8, c), :] = x_ref[...]


def _conv_cols(ext_ref, w_ref, b_ref, lo, width, c):
    sl = pl.ds(lo, width)
    w = w_ref[:, sl]
    y = ext_ref[pl.ds(5, c), sl] * w[0:1]
    for j in range(1, CONV_W):
        y = y + ext_ref[pl.ds(5 + j, c), sl] * w[j:j + 1]
    if b_ref is not None:
        y = y + b_ref[:, sl]
    return _silu(y)


def _conv_carry(ext_ref, x_ref):
    c = x_ref.shape[0]
    ext_ref[pl.ds(5, CONV_W - 1), :] = x_ref[pl.ds(c - (CONV_W - 1), CONV_W - 1), :]


def _cumsum_rows(x, tri):
    return _dot_hi(jnp.where(tri, 1.0, 0.0).astype(f32), x)


def _gdn_kernel(qkv_ref, z_ref, sm_ref, cst_ref, cw_ref, alog_ref, dtb_ref, nw_ref, s0_ref,
                o_ref, s_ref, ext_ref):
    c = qkv_ref.shape[0]
    first = pl.program_id(1) == 0

    @pl.when(first)
    def _():
        s_ref[...] = s0_ref[...]

    _conv_load(ext_ref, qkv_ref, cst_ref, first)

    sm = sm_ref[...]
    beta_l = _sigmoid(sm)
    g_l = -jnp.exp(alog_ref[...]) * jax.nn.softplus(sm + dtb_ref[...])
    tri, strict, eye = _tri_masks(c)
    gc_all = _cumsum_rows(g_l, tri)
    gc_t = gc_all.T
    eye_f = jnp.where(eye, 1.0, 0.0).astype(f32)
    n_fac = max(1, (c - 1).bit_length() - 1)

    hs = range(GDN_H)
    qb, kb, ks, decay, eg, gcs, nm, rhs = [], [], [], [], [], [], [], []
    for h in hs:
        q = _conv_cols(ext_ref, cw_ref, None, h * GDN_DK, GDN_DK, c)
        k = _conv_cols(ext_ref, cw_ref, None, GDN_QK + h * GDN_DK, GDN_DK, c)
        v = _conv_cols(ext_ref, cw_ref, None, 2 * GDN_QK + h * GDN_DV, GDN_DV, c)
        q = q * lax.rsqrt(jnp.sum(q * q, axis=-1, keepdims=True) + EPS) * (GDN_DK ** -0.5)
        k = k * lax.rsqrt(jnp.sum(k * k, axis=-1, keepdims=True) + EPS)
        beta = beta_l[:, SM_BETA + h:SM_BETA + h + 1]
        gc = gc_all[:, SM_DEC + h:SM_DEC + h + 1]
        d = jnp.exp(jnp.where(tri, gc - gc_t[SM_DEC + h:SM_DEC + h + 1, :], -jnp.inf))
        e = jnp.exp(gc)
        qb.append(q.astype(bf16))
        kb.append(k.astype(bf16))
        ks.append(k)
        decay.append(d)
        eg.append(e)
        gcs.append(gc)
        nm.append(jnp.where(strict, beta * d * _dot_nt(kb[h], kb[h]), 0.0))
        rhs.append(jnp.concatenate([v * beta, k * (beta * e)], axis=1))

    inv = [eye_f - nm[h] for h in hs]
    pw = [_dot_hi(nm[h], nm[h]) for h in hs]
    for i in range(n_fac):
        inv = [inv[h] + _dot_hi(inv[h], pw[h]) for h in hs]
        if i + 1 < n_fac:
            pw = [_dot_hi(pw[h], pw[h]) for h in hs]
    sol = [_dot_hi(inv[h], rhs[h]) for h in hs]

    s = [s_ref[h] for h in hs]
    sb = [s[h].astype(bf16) for h in hs]
    wb = [(sol[h][:, 0:GDN_DV] - _dot(sol[h][:, GDN_DV:].astype(bf16), sb[h])).astype(bf16) for h in hs]
    qk = [(_dot_nt(qb[h], kb[h]) * decay[h]).astype(bf16) for h in hs]
    o = [_dot(qb[h], sb[h]) * eg[h] + _dot(qk[h], wb[h]) for h in hs]
    for h in hs:
        g_last = gcs[h][c - 1:c]
        kd = (ks[h] * jnp.exp(g_last - gcs[h])).astype(bf16)
        s_ref[h] = s[h] * jnp.exp(g_last) + _dot_tn(kd, wb[h])
    for h in hs:
        sl = slice(h * GDN_DV, (h + 1) * GDN_DV)
        o_ref[:, sl] = (_rms(o[h], nw_ref[...]) * _silu(z_ref[:, sl])).astype(bf16)

    _conv_carry(ext_ref, qkv_ref)


def _gdn_prompt(proj, conv_state, s0, conv_w, alog_l, dtb_l, norm_w):
    b, t, _ = proj.shape
    c = CHUNK
    const = lambda shape: pl.BlockSpec(shape, lambda bi, ti: (0,) * len(shape))
    st = pl.BlockSpec((None, GDN_H, GDN_DK, GDN_DV), lambda bi, ti: (bi, 0, 0, 0))
    return pl.pallas_call(
        _gdn_kernel,
        grid=(b, t // c),
        in_specs=[pl.BlockSpec((None, c, GDN_CONV), lambda bi, ti: (bi, ti, P_QKV_A // GDN_CONV)),
                  pl.BlockSpec((None, c, GDN_V), lambda bi, ti: (bi, ti, P_Z_A // GDN_V)),
                  pl.BlockSpec((None, c, LANE), lambda bi, ti: (bi, ti, P_SMALL // LANE)),
                  pl.BlockSpec((None, CONV_W - 1, GDN_CONV), lambda bi, ti: (bi, 0, 0)),
                  const((CONV_W, GDN_CONV)), const((1, LANE)), const((1, LANE)), const((1, GDN_DV)),
                  st],
        out_specs=[pl.BlockSpec((None, c, GDN_V), lambda bi, ti: (bi, ti, 0)), st],
        out_shape=[jax.ShapeDtypeStruct((b, t, GDN_V), bf16),
                   jax.ShapeDtypeStruct((b, GDN_H, GDN_DK, GDN_DV), f32)],
        scratch_shapes=[pltpu.VMEM((8 + c, GDN_CONV), f32)],
        compiler_params=_cp(("parallel", "arbitrary")),
        name="gdn_prompt",
    )(proj, proj, proj, conv_state, conv_w, alog_l, dtb_l, norm_w, s0)


GLA_SUB = 16


def _gla_kernel(q_ref, k_ref, v_ref, r_ref, sm_ref, wgate_ref, bgate_ref, nw_ref, s0_ref,
                o_ref, so_ref, st_ref):
    c = q_ref.shape[0]
    ti = pl.program_id(1)

    @pl.when(ti == 0)
    def _():
        for h in range(GLA_H):
            st_ref[h] = s0_ref[h].T

    sm = sm_ref[...]
    lane = lax.broadcasted_iota(jnp.int32, sm.shape, 1)
    lr = jnp.where((lane >= SM_LR) & (lane < SM_LR + GLA_RANK), sm, 0.0).astype(bf16)
    la = jax.nn.log_sigmoid(_dot(lr, wgate_ref[...]) + bgate_ref[...]) / GLA_TAU
    tri, _, _ = _tri_masks(c)
    b_all = _cumsum_rows(la, tri)

    col = lax.broadcasted_iota(jnp.int32, (GLA_SUB, c), 1)
    row = lax.broadcasted_iota(jnp.int32, (GLA_SUB, c), 0)
    hs = range(GLA_H)
    qs, ks, vbs, bs, atts = [], [], [], [], []
    for h in hs:
        sk = slice(h * GLA_DK, (h + 1) * GLA_DK)
        q = q_ref[:, sk] * (GLA_DK ** -0.5)
        k = k_ref[:, sk]
        b = b_all[:, sk]
        qs.append(q)
        ks.append(k)
        bs.append(b)
        vbs.append(v_ref[:, h * GLA_DV:(h + 1) * GLA_DV].astype(bf16))
        att_rows = []
        for i in range(c // GLA_SUB):
            lo = i * GLA_SUB
            b_i = b[lo:lo + GLA_SUB]
            q_i = q[lo:lo + GLA_SUB]
            b_top = b[lo:lo + 1]
            att = jnp.zeros((GLA_SUB, c), f32)
            if i > 0:
                q_t = (q_i * jnp.exp(b_i - b_top)).astype(bf16)
                k_t = (k * jnp.exp(jnp.minimum(b_top - b, 0.0))).astype(bf16)
                att = jnp.where(col < lo, _dot_nt(q_t, k_t), 0.0)
            for sl in range(GLA_SUB):
                sidx = lo + sl
                e = jnp.exp(jnp.minimum(b_i - b[sidx:sidx + 1], 0.0))
                p = jnp.sum(q_i * k[sidx:sidx + 1] * e, axis=1, keepdims=True)
                att = jnp.where((col == sidx) & (row >= sl), p, att)
            att_rows.append(att)
        atts.append(jnp.concatenate(att_rows, axis=0).astype(bf16))

    sts = [st_ref[h] for h in hs]
    os_ = [_dot(atts[h], vbs[h]) + _dot_nt((qs[h] * jnp.exp(bs[h])).astype(bf16), sts[h].astype(bf16))
           for h in hs]
    st_new = []
    for h in hs:
        b_last = bs[h][c - 1:c]
        kd = (ks[h] * jnp.exp(b_last - bs[h])).astype(bf16)
        st_new.append(sts[h] * jnp.exp(b_last) + _dot_tn(vbs[h], kd))
        st_ref[h] = st_new[h]

    @pl.when(ti == pl.num_programs(1) - 1)
    def _():
        for h in hs:
            so_ref[h] = st_new[h].T

    for h in hs:
        sv = slice(h * GLA_DV, (h + 1) * GLA_DV)
        o_ref[:, sv] = (_rms(os_[h], nw_ref[...]) * _silu(r_ref[:, sv])).astype(bf16)


def _gla_prompt(proj, s0, wgate_pad, bgate, norm_w):
    b, t, _ = proj.shape
    c = CHUNK
    const = lambda shape: pl.BlockSpec(shape, lambda bi, ti: (0,) * len(shape))
    st = pl.BlockSpec((None, GLA_H, GLA_DK, GLA_DV), lambda bi, ti: (bi, 0, 0, 0))
    return pl.pallas_call(
        _gla_kernel,
        grid=(b, t // c),
        in_specs=[pl.BlockSpec((None, c, GLA_QK), lambda bi, ti: (bi, ti, P_Q_B // GLA_QK)),
                  pl.BlockSpec((None, c, GLA_QK), lambda bi, ti: (bi, ti, P_K_B // GLA_QK)),
                  pl.BlockSpec((None, c, GLA_V), lambda bi, ti: (bi, ti, P_V_B // GLA_V)),
                  pl.BlockSpec((None, c, GLA_V), lambda bi, ti: (bi, ti, P_R_B // GLA_V)),
                  pl.BlockSpec((None, c, LANE), lambda bi, ti: (bi, ti, P_SMALL // LANE)),
                  const((LANE, GLA_QK)), const((1, GLA_QK)), const((1, GLA_DV)),
                  st],
        out_specs=[pl.BlockSpec((None, c, GLA_V), lambda bi, ti: (bi, ti, 0)), st],
        out_shape=[jax.ShapeDtypeStruct((b, t, GLA_V), bf16),
                   jax.ShapeDtypeStruct((b, GLA_H, GLA_DK, GLA_DV), f32)],
        scratch_shapes=[pltpu.VMEM((GLA_H, GLA_DV, GLA_DK), f32)],
        compiler_params=_cp(("parallel", "arbitrary")),
        name="gla_prompt",
    )(proj, proj, proj, proj, proj, wgate_pad, bgate, norm_w, s0)


def _ssd_kernel(z_ref, xbc_ref, sm_ref, cst_ref, cw_ref, cb_ref, avec_ref, dtb_ref, dvec_ref, nw_ref, h0_ref,
                o_ref, h_ref, ext_ref):
    c = xbc_ref.shape[0]
    first = pl.program_id(1) == 0

    @pl.when(first)
    def _():
        h_ref[...] = h0_ref[...]

    _conv_load(ext_ref, xbc_ref, cst_ref, first)

    dt_l = jax.nn.softplus(sm_ref[...] + dtb_ref[...])
    tri, _, _ = _tri_masks(c)
    ac_all = _cumsum_rows(dt_l * avec_ref[...], tri)
    ac_t2 = jnp.concatenate([ac_all, ac_all], axis=0).T

    lane = lax.broadcasted_iota(jnp.int32, (c, 2 * SSD_P), 1)
    rowi = lax.broadcasted_iota(jnp.int32, (c, 2 * SSD_P), 0)
    lo_half = lane < SSD_P
    tri2 = rowi >= jnp.where(lo_half, lane, lane - SSD_P)
    lane1 = lane[0:1]
    gs_ = range(SSD_G)
    ps_ = range(SSD_HG // 2)

    xs_, bmb, hg, cb2, ch = [], [], [], [], []
    for g in gs_:
        xs_.append(_conv_cols(ext_ref, cw_ref, cb_ref, g * SSD_GW, SSD_GW, c))
        bmb.append(_conv_cols(ext_ref, cw_ref, cb_ref, SSD_INNER + g * SSD_N, SSD_N, c).astype(bf16))
        cmb = _conv_cols(ext_ref, cw_ref, cb_ref, SSD_INNER + SSD_BC + g * SSD_N, SSD_N, c).astype(bf16)
        hg.append(h_ref[pl.ds(g * SSD_HG, SSD_HG)].reshape(SSD_GW, SSD_N))
        cb2.append(_dot_nt(cmb, jnp.concatenate([bmb[g], bmb[g]], axis=0)))
        ch.append(_dot_nt(cmb, hg[g].astype(bf16)))

    m2, rhs, e_col, xsc = {}, {}, {}, {}
    for g in gs_:
        for p in ps_:
            l0 = SM_DT + g * SSD_HG + 2 * p
            ps = slice(p * 2 * SSD_P, (p + 1) * 2 * SSD_P)
            ac_col = jnp.where(lo_half, ac_all[:, l0:l0 + 1], ac_all[:, l0 + 1:l0 + 2])
            ac_row = jnp.where(lane1 < SSD_P, ac_t2[l0:l0 + 1, :], ac_t2[l0 + 1:l0 + 2, :])
            decay = jnp.exp(jnp.where(tri2, ac_col - ac_row, -jnp.inf))
            m2[g, p] = (cb2[g] * decay).astype(bf16)
            dt2 = jnp.where(lo_half, dt_l[:, l0:l0 + 1], dt_l[:, l0 + 1:l0 + 2])
            xdt = xs_[g][:, ps] * dt2
            rhs[g, p] = jnp.concatenate([jnp.where(lo_half, xdt, 0.0), jnp.where(lo_half, 0.0, xdt)],
                                        axis=0).astype(bf16)
            e_col[g, p] = jnp.exp(ac_col)
            xsc[g, p] = xdt * jnp.exp(ac_col[c - 1:c] - ac_col)
    y2 = {gp: _dot(m2[gp], rhs[gp]) for gp in m2}

    for g in gs_:
        xsg = jnp.concatenate([xsc[g, p] for p in ps_], axis=1).astype(bf16)
        dh = _dot_tn(xsg, bmb[g])
        for j in range(SSD_HG):
            ln = SM_DT + g * SSD_HG + j
            h_ref[g * SSD_HG + j] = (hg[g][j * SSD_P:(j + 1) * SSD_P] * jnp.exp(ac_all[c - 1:c, ln:ln + 1])
                                     + dh[j * SSD_P:(j + 1) * SSD_P])
    for g in gs_:
        gs = slice(g * SSD_GW, (g + 1) * SSD_GW)
        y = jnp.concatenate([y2[g, p] + ch[g][:, p * 2 * SSD_P:(p + 1) * 2 * SSD_P] * e_col[g, p] for p in ps_],
                            axis=1)
        y = (y + dvec_ref[:, gs] * xs_[g]) * _silu(z_ref[:, gs])
        o_ref[:, gs] = _rms(y, nw_ref[:, gs]).astype(bf16)

    _conv_carry(ext_ref, xbc_ref)


def _ssd_prompt(proj, conv_state, h0, conv_w, conv_b, avec_l, dtb_l, dvec, norm_w):
    b, t, _ = proj.shape
    c = CHUNK
    const = lambda shape: pl.BlockSpec(shape, lambda bi, ti: (0,) * len(shape))
    st = pl.BlockSpec((None, SSD_H, SSD_P, SSD_N), lambda bi, ti: (bi, 0, 0, 0))
    return pl.pallas_call(
        _ssd_kernel,
        grid=(b, t // c),
        in_specs=[pl.BlockSpec((None, c, SSD_INNER), lambda bi, ti: (bi, ti, P_Z_C // SSD_INNER)),
                  pl.BlockSpec((None, c, SSD_CONV), lambda bi, ti: (bi, ti, P_XBC // SSD_CONV)),
                  pl.BlockSpec((None, c, LANE), lambda bi, ti: (bi, ti, P_SMALL // LANE)),
                  pl.BlockSpec((None, CONV_W - 1, SSD_CONV), lambda bi, ti: (bi, 0, 0)),
                  const((CONV_W, SSD_CONV)), const((1, SSD_CONV)), const((1, LANE)), const((1, LANE)),
                  const((1, SSD_INNER)), const((1, SSD_INNER)),
                  st],
        out_specs=[pl.BlockSpec((None, c, SSD_INNER), lambda bi, ti: (bi, ti, 0)), st],
        out_shape=[jax.ShapeDtypeStruct((b, t, SSD_INNER), bf16),
                   jax.ShapeDtypeStruct((b, SSD_H, SSD_P, SSD_N), f32)],
        scratch_shapes=[pltpu.VMEM((8 + c, SSD_CONV), f32)],
        compiler_params=_cp(("parallel", "arbitrary")),
        name="ssd_prompt",
    )(proj, proj, proj, conv_state, conv_w, conv_b, avec_l, dtb_l, dvec, norm_w, h0)


STEP_B = 8


def _conv_step(x_ref, c_ref, w_ref):
    w = w_ref[...]
    return c_ref[0] * w[0:1] + c_ref[1] * w[1:2] + c_ref[2] * w[2:3] + x_ref[...] * w[3:4]


def _state_call(kernel, grid, in_specs, out_specs, out_shape, scratch, sem, name, args, prev_state):
    aliases = {}
    if prev_state is not None:
        in_specs = in_specs + [pl.BlockSpec(memory_space=pl.ANY)]
        args = args + (prev_state,)
        aliases = {len(args) - 1: len(out_shape) - 1}
        kernel = functools.partial(_drop_alias_ref, kernel, len(args) - 1)
    return pl.pallas_call(kernel, grid=grid, in_specs=in_specs, out_specs=out_specs, out_shape=out_shape,
                          scratch_shapes=scratch, input_output_aliases=aliases,
                          compiler_params=_cp(sem), name=name)(*args)


def _drop_alias_ref(kernel, idx, *refs):
    return kernel(*refs[:idx], *refs[idx + 1:])


def _gdn_step_kernel(alog_ref, dtb_ref, q_ref, k_ref, v_ref, z_ref, sm_ref, cq_ref, ck_ref, cv_ref,
                     wq_ref, wk_ref, wv_ref, nw_ref, s_ref, o_ref, so_ref):
    nb = q_ref.shape[0]
    qa = _silu(_conv_step(q_ref, cq_ref, wq_ref))
    ka = _silu(_conv_step(k_ref, ck_ref, wk_ref))
    va = _silu(_conv_step(v_ref, cv_ref, wv_ref))
    sm = sm_ref[...]
    z = z_ref[...]
    for h in range(GDN_H):
        sl = slice(h * GDN_DK, (h + 1) * GDN_DK)
        q = qa[:, sl]
        k = ka[:, sl]
        v = va[:, sl]
        q = q * lax.rsqrt(jnp.sum(q * q, axis=-1, keepdims=True) + EPS) * (GDN_DK ** -0.5)
        k = k * lax.rsqrt(jnp.sum(k * k, axis=-1, keepdims=True) + EPS)
        beta = _sigmoid(sm[:, SM_BETA + h:SM_BETA + h + 1])
        g = -jnp.exp(_scalar_vec(alog_ref[h])) * jax.nn.softplus(sm[:, SM_DEC + h:SM_DEC + h + 1] + dtb_ref[h])
        eg = jnp.exp(g)
        qk = jnp.sum(q * k, axis=-1, keepdims=True)
        q_t = q.T
        k_t = k.T
        rows = []
        for b in range(nb):
            s = s_ref[b, h]
            kcol = k_t[:, b:b + 1]
            ks = jnp.sum(s * kcol, axis=0, keepdims=True)
            qs = jnp.sum(s * q_t[:, b:b + 1], axis=0, keepdims=True)
            bt = beta[b:b + 1]
            e = eg[b:b + 1]
            w = bt * v[b:b + 1] - (bt * e) * ks
            rows.append(qs * e + qk[b:b + 1] * w)
            so_ref[b, h] = s * e + kcol * w
        o = jnp.concatenate(rows, axis=0)
        o = _rms(o, nw_ref[...]) * _silu(z[:, sl])
        o_ref[:, sl] = o.astype(bf16)


def _gdn_step(proj, conv_t, states, prev, l, conv_w, a_log, dt_bias, norm_w):
    n = proj.shape[0]
    nb = STEP_B
    blk = lambda off: pl.BlockSpec((nb, GDN_QK), lambda i: (i, off // GDN_QK))
    cst = lambda j: pl.BlockSpec((CONV_W - 1, nb, GDN_QK), lambda i: (0, i, j))
    cw = lambda j: pl.BlockSpec((CONV_W, GDN_QK), lambda i: (0, j))
    smem = pl.BlockSpec(memory_space=pltpu.SMEM)
    st = pl.BlockSpec((None, nb, GDN_H, GDN_DK, GDN_DV), lambda i: (l, i, 0, 0, 0))
    return _state_call(
        _gdn_step_kernel, (n // nb,),
        [smem, smem, blk(P_QKV_A), blk(P_QKV_A + GDN_QK), blk(P_QKV_A + 2 * GDN_QK), blk(P_Z_A),
         pl.BlockSpec((nb, LANE), lambda i: (i, P_SMALL // LANE)),
         cst(0), cst(1), cst(2), cw(0), cw(1), cw(2),
         pl.BlockSpec((1, GDN_DV), lambda i: (0, 0)), st],
        [pl.BlockSpec((nb, GDN_V), lambda i: (i, 0)), st],
        [jax.ShapeDtypeStruct((n, GDN_V), bf16), jax.ShapeDtypeStruct(states.shape, f32)],
        [], ("parallel",), "gdn_step",
        (a_log, dt_bias, proj, proj, proj, proj, proj, conv_t, conv_t, conv_t, conv_w, conv_w, conv_w, norm_w, states),
        prev)


def _gla_step_kernel(q_ref, k_ref, v_ref, r_ref, sm_ref, wgate_ref, bgate_ref, nw_ref, s_ref,
                     o_ref, so_ref):
    nb = q_ref.shape[0]
    sm = sm_ref[...]
    lane = lax.broadcasted_iota(jnp.int32, sm.shape, 1)
    lr = jnp.where((lane >= SM_LR) & (lane < SM_LR + GLA_RANK), sm, 0.0).astype(bf16)
    la_all = jax.nn.log_sigmoid(_dot(lr, wgate_ref[...]) + bgate_ref[...]) / GLA_TAU
    qa = q_ref[...] * (GLA_DK ** -0.5)
    ka = k_ref[...]
    va = v_ref[...]
    r = r_ref[...]
    for h in range(GLA_H):
        sk = slice(h * GLA_DK, (h + 1) * GLA_DK)
        sv = slice(h * GLA_DV, (h + 1) * GLA_DV)
        q = qa[:, sk]
        k = ka[:, sk]
        v = va[:, sv]
        e = jnp.exp(la_all[:, sk])
        qk = jnp.sum(q * k, axis=-1, keepdims=True)
        e_t = e.T
        k_t = k.T
        qe_t = (q * e).T
        rows = []
        for b in range(nb):
            s = s_ref[b, h]
            vrow = v[b:b + 1]
            rows.append(qk[b:b + 1] * vrow + jnp.sum(s * qe_t[:, b:b + 1], axis=0, keepdims=True))
            so_ref[b, h] = s * e_t[:, b:b + 1] + k_t[:, b:b + 1] * vrow
        o = jnp.concatenate(rows, axis=0)
        o = _rms(o, nw_ref[...]) * _silu(r[:, sv])
        o_ref[:, sv] = o.astype(bf16)


def _gla_step(proj, states, prev, l, wgate_pad, bgate, norm_w):
    n = proj.shape[0]
    nb = STEP_B
    st = pl.BlockSpec((None, nb, GLA_H, GLA_DK, GLA_DV), lambda i: (l, i, 0, 0, 0))
    return _state_call(
        _gla_step_kernel, (n // nb,),
        [pl.BlockSpec((nb, GLA_QK), lambda i: (i, P_Q_B // GLA_QK)),
         pl.BlockSpec((nb, GLA_QK), lambda i: (i, P_K_B // GLA_QK)),
         pl.BlockSpec((nb, GLA_V), lambda i: (i, P_V_B // GLA_V)),
         pl.BlockSpec((nb, GLA_V), lambda i: (i, P_R_B // GLA_V)),
         pl.BlockSpec((nb, LANE), lambda i: (i, P_SMALL // LANE)),
         pl.BlockSpec((LANE, GLA_QK), lambda i: (0, 0)),
         pl.BlockSpec((1, GLA_QK), lambda i: (0, 0)),
         pl.BlockSpec((1, GLA_DV), lambda i: (0, 0)), st],
        [pl.BlockSpec((nb, GLA_V), lambda i: (i, 0)), st],
        [jax.ShapeDtypeStruct((n, GLA_V), bf16), jax.ShapeDtypeStruct(states.shape, f32)],
        [], ("parallel",), "gla_step",
        (proj, proj, proj, proj, proj, wgate_pad, bgate, norm_w, states),
        prev)


def _ssd_step_kernel(alog_ref, dtb_ref, z_ref, x_ref, bm_ref, cm_ref, sm_ref, cx_ref, cb_ref, cc_ref,
                     wx_ref, wb_ref, wc_ref, bx_ref, bb_ref, bc_ref, dvec_ref, nw_ref, h_ref,
                     o_ref, ho_ref, y_ref):
    g = pl.program_id(1)
    nb = x_ref.shape[0]
    x = _silu(_conv_step(x_ref, cx_ref, wx_ref) + bx_ref[...])
    bm = _silu(_conv_step(bm_ref, cb_ref, wb_ref) + bb_ref[...])
    cm = _silu(_conv_step(cm_ref, cc_ref, wc_ref) + bc_ref[...])
    sm = sm_ref[...]
    cbdot = jnp.sum(cm * bm, axis=-1, keepdims=True)
    lane_b = lax.broadcasted_iota(jnp.int32, (SSD_P, nb), 1)
    for jp in range(SSD_HG // 2):
        x_t = x[:, jp * LANE:(jp + 1) * LANE].T
        halves = []
        for jj in range(2):
            j = 2 * jp + jj
            hd = g * SSD_HG + j
            dt = jax.nn.softplus(_lane_col(sm, SM_DT + hd) + dtb_ref[hd])
            ea = jnp.exp(dt * (-jnp.exp(_scalar_vec(alog_ref[hd]))))
            y_t = jnp.zeros((SSD_P, nb), f32)
            for b in range(nb):
                hh = h_ref[b, j]
                xdt = x_t[jj * SSD_P:(jj + 1) * SSD_P, b:b + 1] * dt[b:b + 1]
                eab = ea[b:b + 1]
                ycol = jnp.sum(hh * cm[b:b + 1], axis=-1, keepdims=True) * eab + cbdot[b:b + 1] * xdt
                ho_ref[b, j] = hh * eab + xdt * bm[b:b + 1]
                y_t = jnp.where(lane_b == b, ycol, y_t)
            halves.append(y_t)
        y_ref[:, jp * LANE:(jp + 1) * LANE] = jnp.concatenate(halves, axis=0).T
    y = (y_ref[...] + dvec_ref[...] * x) * _silu(z_ref[...])
    o_ref[...] = _rms(y, nw_ref[...]).astype(bf16)


def _ssd_step(proj, conv_t, states, prev, l, conv_w, conv_b, a_log, dt_bias, dvec, norm_w):
    n = proj.shape[0]
    nb = STEP_B
    gw = SSD_GW
    nbk = SSD_INNER // SSD_N
    smem = pl.BlockSpec(memory_space=pltpu.SMEM)
    st = pl.BlockSpec((None, nb, SSD_HG, SSD_P, SSD_N), lambda i, g: (l, i, g, 0, 0))
    return _state_call(
        _ssd_step_kernel, (n // nb, SSD_G),
        [smem, smem,
         pl.BlockSpec((nb, gw), lambda i, g: (i, P_Z_C // gw + g)),
         pl.BlockSpec((nb, gw), lambda i, g: (i, P_XBC // gw + g)),
         pl.BlockSpec((nb, SSD_N), lambda i, g: (i, (P_XBC + SSD_INNER) // SSD_N + g)),
         pl.BlockSpec((nb, SSD_N), lambda i, g: (i, (P_XBC + SSD_INNER + SSD_BC) // SSD_N + g)),
         pl.BlockSpec((nb, LANE), lambda i, g: (i, P_SMALL // LANE)),
         pl.BlockSpec((CONV_W - 1, nb, gw), lambda i, g: (0, i, g)),
         pl.BlockSpec((CONV_W - 1, nb, SSD_N), lambda i, g: (0, i, nbk + g)),
         pl.BlockSpec((CONV_W - 1, nb, SSD_N), lambda i, g: (0, i, nbk + SSD_G + g)),
         pl.BlockSpec((CONV_W, gw), lambda i, g: (0, g)),
         pl.BlockSpec((CONV_W, SSD_N), lambda i, g: (0, nbk + g)),
         pl.BlockSpec((CONV_W, SSD_N), lambda i, g: (0, nbk + SSD_G + g)),
         pl.BlockSpec((1, gw), lambda i, g: (0, g)),
         pl.BlockSpec((1, SSD_N), lambda i, g: (0, nbk + g)),
         pl.BlockSpec((1, SSD_N), lambda i, g: (0, nbk + SSD_G + g)),
         pl.BlockSpec((1, gw), lambda i, g: (0, g)),
         pl.BlockSpec((1, gw), lambda i, g: (0, g)), st],
        [pl.BlockSpec((nb, gw), lambda i, g: (i, g)), st],
        [jax.ShapeDtypeStruct((n, SSD_INNER), bf16), jax.ShapeDtypeStruct(states.shape, f32)],
        [pltpu.VMEM((nb, gw), f32)], ("parallel", "parallel"), "ssd_step",
        (a_log, dt_bias, proj, proj, proj, proj, proj, conv_t, conv_t, conv_t,
         conv_w, conv_w, conv_w, conv_b, conv_b, conv_b, dvec, norm_w, states),
        prev)


def _merge_kernel(oa_ref, ob_ref, oc_ref, ga_ref, gb_ref, gc_ref, wa_ref, wb_ref, wc_ref, o_ref):
    m = (_sigmoid(ga_ref[...]) * _dot(oa_ref[...], wa_ref[...])
         + _sigmoid(gb_ref[...]) * _dot(ob_ref[...], wb_ref[...])
         + _sigmoid(gc_ref[...]) * _dot(oc_ref[...], wc_ref[...]))
    o_ref[...] = m.astype(bf16)


def _merge(oa, ob, oc, proj, wa, wb, wc, tm):
    bx, tx, _ = oa.shape
    tn = 512
    gate = lambda k: pl.BlockSpec((None, tm, tn), lambda b, i, j: (b, i, (P_GATES + k * D_MODEL) // tn + j))
    act = lambda w: pl.BlockSpec((None, tm, w), lambda b, i, j: (b, i, 0))
    wsp = lambda w: pl.BlockSpec((w, tn), lambda b, i, j: (0, j))
    return pl.pallas_call(
        _merge_kernel,
        grid=(bx, tx // tm, D_MODEL // tn),
        in_specs=[act(GDN_V), act(GLA_V), act(SSD_INNER), gate(0), gate(1), gate(2),
                  wsp(GDN_V), wsp(GLA_V), wsp(SSD_INNER)],
        out_specs=pl.BlockSpec((None, tm, tn), lambda b, i, j: (b, i, j)),
        out_shape=jax.ShapeDtypeStruct((bx, tx, D_MODEL), bf16),
        compiler_params=_cp(("parallel", "parallel", "arbitrary")),
        name="merge",
    )(oa, ob, oc, proj, proj, proj, wa, wb, wc)


def _outproj_kernel(m_ref, x_ref, gt_ref, w_ref, o_ref):
    o_ref[...] = x_ref[...] + gt_ref[...] * _dot(m_ref[...], w_ref[...])


def _outproj(merged, x, mod, w, per_row, tm):
    bx, tx, dm = x.shape
    tn = 512
    nj = dm // tn
    if per_row:
        gspec = pl.BlockSpec((None, tm, tn), lambda b, i, j: (b, i, 5 * nj + j))
    else:
        gspec = pl.BlockSpec((None, 1, tn), lambda b, i, j: (b, 0, 5 * nj + j))
    return pl.pallas_call(
        _outproj_kernel,
        grid=(bx, tx // tm, nj),
        in_specs=[pl.BlockSpec((None, tm, dm), lambda b, i, j: (b, i, 0)),
                  pl.BlockSpec((None, tm, tn), lambda b, i, j: (b, i, j)),
                  gspec,
                  pl.BlockSpec((dm, tn), lambda b, i, j: (0, j))],
        out_specs=pl.BlockSpec((None, tm, tn), lambda b, i, j: (b, i, j)),
        out_shape=jax.ShapeDtypeStruct(x.shape, f32),
        compiler_params=_cp(("parallel", "parallel", "arbitrary")),
        name="outproj",
    )(merged, x, mod, w)


def _permute_w_in(w):
    d = w.shape[0]
    small = jnp.concatenate([w[:, 4096:4112], w[:, 6160:6176], w[:, 12320:12352]], axis=1)
    pad = jnp.zeros((d, P_TOTAL - P_SMALL - small.shape[1]), w.dtype)
    return jnp.concatenate([w[:, 0:3072], w[:, 9248:12320], w[:, 7200:9248], w[:, 12352:18496],
                            w[:, 3072:4096], w[:, 4112:6160], w[:, 6176:7200], small, pad],
                           axis=1).astype(bf16)


def _pad_ff(w, axis):
    f = w.shape[axis]
    fp = -(-f // FF_TILE) * FF_TILE
    widths = [(0, 0), (0, 0)]
    widths[axis] = (0, fp - f)
    return jnp.pad(w, widths).astype(bf16)


def _lane_vec(v, lo):
    return jnp.zeros((1, LANE), f32).at[0, lo:lo + v.shape[0]].set(v)


def _layer_params(l, p):
    row = lambda a: a[l].reshape(1, -1)
    wgate = jnp.zeros((LANE, GLA_QK), f32).at[SM_LR:SM_LR + GLA_RANK].set(p["gla_w_gate"][l]).astype(bf16)
    return dict(
        norm1=row(p["norm1"]), norm2=row(p["norm2"]), norm3=row(p["norm3"]),
        f1=(_pad_ff(p["ffn1_wg"][l], 1), _pad_ff(p["ffn1_wu"][l], 1), _pad_ff(p["ffn1_wd"][l], 0)),
        f2=(_pad_ff(p["ffn2_wg"][l], 1), _pad_ff(p["ffn2_wu"][l], 1), _pad_ff(p["ffn2_wd"][l], 0)),
        w_in=_permute_w_in(p["w_in"][l]),
        gdn_conv_w=p["gdn_conv_w"][l], gdn_a_log=p["gdn_a_log"][l], gdn_dt_bias=p["gdn_dt_bias"][l],
        gdn_alog_l=_lane_vec(p["gdn_a_log"][l], SM_DEC), gdn_dtb_l=_lane_vec(p["gdn_dt_bias"][l], SM_DEC),
        gdn_norm_w=row(p["gdn_norm_w"]),
        gla_wgate=wgate, gla_bgate=row(p["gla_b_gate"]), gla_norm_w=row(p["gla_norm_w"]),
        ssd_conv_w=p["ssd_conv_w"][l], ssd_conv_b=row(p["ssd_conv_b"]), ssd_a_log=p["ssd_a_log"][l],
        ssd_dt_bias=p["ssd_dt_bias"][l],
        ssd_avec_l=_lane_vec(-jnp.exp(p["ssd_a_log"][l]), SM_DT), ssd_dtb_l=_lane_vec(p["ssd_dt_bias"][l], SM_DT),
        ssd_dvec=jnp.repeat(p["ssd_d"][l], SSD_P).reshape(1, -1),
        ssd_norm_w=row(p["ssd_norm_w"]),
        wa=p["w_branch_gdn"][l].astype(bf16), wb=p["w_branch_gla"][l].astype(bf16),
        wc=p["w_branch_ssd"][l].astype(bf16), w_out=p["w_out"][l].astype(bf16),
    )


def _new_conv_state(buf, raw):
    t = raw.shape[1]
    k = CONV_W - 1
    if t >= k:
        return raw[:, t - k:]
    return jnp.concatenate([buf[:, t:], raw], axis=1)


def _mixer_prompt(proj, lp, st):
    gdn_conv, s_gdn, s_gla, ssd_conv, s_ssd = st
    oa, s_gdn_n = _gdn_prompt(proj, gdn_conv, s_gdn, lp["gdn_conv_w"], lp["gdn_alog_l"], lp["gdn_dtb_l"],
                              lp["gdn_norm_w"])
    ob, s_gla_n = _gla_prompt(proj, s_gla, lp["gla_wgate"], lp["gla_bgate"], lp["gla_norm_w"])
    oc, s_ssd_n = _ssd_prompt(proj, ssd_conv, s_ssd, lp["ssd_conv_w"], lp["ssd_conv_b"], lp["ssd_avec_l"],
                              lp["ssd_dtb_l"], lp["ssd_dvec"], lp["ssd_norm_w"])
    gdn_conv_n = _new_conv_state(gdn_conv, proj[:, :, P_QKV_A:P_QKV_A + GDN_CONV])
    ssd_conv_n = _new_conv_state(ssd_conv, proj[:, :, P_XBC:P_XBC + SSD_CONV])
    return (oa, ob, oc), (gdn_conv_n, s_gdn_n, s_gla_n, ssd_conv_n, s_ssd_n)


def _mixer_sample(proj, lp, l, states, prev):
    n = proj.shape[1]
    p2 = proj.reshape(n, P_TOTAL)
    gdn_conv, ssd_conv = states[0][l], states[3][l]
    gct = jnp.swapaxes(gdn_conv, 0, 1)
    sct = jnp.swapaxes(ssd_conv, 0, 1)
    pv = (None,) * 5 if prev is None else prev
    oa, s_gdn_n = _gdn_step(p2, gct, states[1], pv[1], l, lp["gdn_conv_w"], lp["gdn_a_log"], lp["gdn_dt_bias"],
                            lp["gdn_norm_w"])
    ob, s_gla_n = _gla_step(p2, states[2], pv[2], l, lp["gla_wgate"], lp["gla_bgate"], lp["gla_norm_w"])
    oc, s_ssd_n = _ssd_step(p2, sct, states[4], pv[4], l, lp["ssd_conv_w"], lp["ssd_conv_b"], lp["ssd_a_log"],
                            lp["ssd_dt_bias"], lp["ssd_dvec"], lp["ssd_norm_w"])
    raw = p2.reshape(n, 1, P_TOTAL)
    gdn_conv_n = _new_conv_state(gdn_conv, raw[:, :, P_QKV_A:P_QKV_A + GDN_CONV])
    ssd_conv_n = _new_conv_state(ssd_conv, raw[:, :, P_XBC:P_XBC + SSD_CONV])
    outs = tuple(o.reshape(1, n, -1) for o in (oa, ob, oc))
    return outs, (gdn_conv_n, s_gdn_n, s_gla_n, ssd_conv_n, s_ssd_n)


def _trunk(x, mods, lps, states, per_row, tm, final_w):
    nl = len(lps)
    per_layer = []
    prev = None
    for l in range(nl):
        lp, mod = lps[l], mods[l]
        last = l == nl - 1
        x = _ffn(x, mod, 0, lp["norm1"], *lp["f1"], lp["norm1"], per_row, tm, False)
        proj = _inproj(x, mod, lp["norm2"], lp["w_in"], per_row, tm)
        if per_row:
            (oa, ob, oc), st = _mixer_sample(proj, lp, l, states, prev)
            prev = st
        else:
            (oa, ob, oc), st = _mixer_prompt(proj, lp, tuple(s[l] for s in states))
        per_layer.append(st)
        merged = _merge(oa, ob, oc, proj, lp["wa"], lp["wb"], lp["wc"], tm)
        x = _outproj(merged, x, mod, lp["w_out"], per_row, tm)
        x = _ffn(x, mod, 6, lp["norm3"], *lp["f2"], final_w if last else lp["norm3"], per_row, tm, last)
    stack = lambda i: jnp.stack([st[i] for st in per_layer])
    if per_row:
        new_states = (stack(0), prev[1], prev[2], stack(3), prev[4])
    else:
        new_states = tuple(stack(i) for i in range(5))
    return x, new_states


def kernel(x_prompt, x_sample, state_gdn_conv, state_gdn, state_gla, state_ssd_conv, state_ssd, c_prompt, c_sample, w_ada, b_ada, norm1, norm2, norm3, ffn1_wg, ffn1_wu, ffn1_wd, ffn2_wg, ffn2_wu, ffn2_wd, w_in, gdn_conv_w, gdn_a_log, gdn_dt_bias, gdn_norm_w, gla_w_gate, gla_b_gate, gla_norm_w, ssd_conv_w, ssd_conv_b, ssd_a_log, ssd_dt_bias, ssd_d, ssd_norm_w, w_branch_gdn, w_branch_gla, w_branch_ssd, w_out, final_norm):
    p = dict(norm1=norm1, norm2=norm2, norm3=norm3,
             ffn1_wg=ffn1_wg, ffn1_wu=ffn1_wu, ffn1_wd=ffn1_wd, ffn2_wg=ffn2_wg, ffn2_wu=ffn2_wu, ffn2_wd=ffn2_wd,
             w_in=w_in, gdn_conv_w=gdn_conv_w, gdn_a_log=gdn_a_log, gdn_dt_bias=gdn_dt_bias, gdn_norm_w=gdn_norm_w,
             gla_w_gate=gla_w_gate, gla_b_gate=gla_b_gate, gla_norm_w=gla_norm_w,
             ssd_conv_w=ssd_conv_w, ssd_conv_b=ssd_conv_b, ssd_a_log=ssd_a_log, ssd_dt_bias=ssd_dt_bias,
             ssd_d=ssd_d, ssd_norm_w=ssd_norm_w,
             w_branch_gdn=w_branch_gdn, w_branch_gla=w_branch_gla, w_branch_ssd=w_branch_ssd, w_out=w_out)
    nl = w_ada.shape[0]
    bp, tp, dm = x_prompt.shape
    bs = x_sample.shape[0]
    assert x_sample.shape[1] == 1 and tp % CHUNK == 0 and bs % STEP_B == 0 and dm == D_MODEL
    lps = [_layer_params(l, p) for l in range(nl)]
    fw = final_norm.reshape(1, dm)

    rows = bp + bs
    rpad = -(-rows // 8) * 8
    c_all = jnp.concatenate([c_prompt, c_sample, jnp.zeros((rpad - rows, dm), f32)], axis=0)
    mod = _ada_mod(c_all, w_ada, b_ada)
    mod_p = [mod[l, :bp].reshape(bp, 1, N_MOD * dm) for l in range(nl)]
    mod_s = [mod[l, bp:rows].reshape(1, bs, N_MOD * dm) for l in range(nl)]

    sample_states = (state_gdn_conv, state_gdn, state_gla, state_ssd_conv, state_ssd)
    prompt_states = tuple(jnp.zeros((s.shape[0], bp) + s.shape[2:], x_prompt.dtype) for s in sample_states)
    tm_p = 512 if tp % 512 == 0 else CHUNK
    y_p, st_p = _trunk(x_prompt, mod_p, lps, prompt_states, False, tm_p, fw)
    y_s, st_s = _trunk(x_sample.reshape(1, bs, dm), mod_s, lps, sample_states, True, bs, fw)
    return (y_p, y_s.reshape(bs, 1, dm)) + st_p + st_s
```

```python
import functools

import jax
import jax.numpy as jnp
from jax import lax
from jax.experimental import pallas as pl
from jax.experimental.pallas import tpu as pltpu

f32 = jnp.float32
bf16 = jnp.bfloat16
HI = lax.Precision.HIGHEST

EPS = 1e-6
D_MODEL = 2048
N_MOD = 9
CHUNK = 64
CONV_W = 4
GDN_H, GDN_DK, GDN_DV = 8, 128, 128
GLA_H, GLA_DK, GLA_DV, GLA_RANK, GLA_TAU = 4, 128, 256, 16, 16.0
SSD_H, SSD_P, SSD_G, SSD_N = 32, 64, 4, 128
SSD_HG = SSD_H // SSD_G
GDN_QK = GDN_H * GDN_DK
GDN_V = GDN_H * GDN_DV
GDN_CONV = 2 * GDN_QK + GDN_V
GLA_QK = GLA_H * GLA_DK
GLA_V = GLA_H * GLA_DV
SSD_INNER = SSD_H * SSD_P
SSD_BC = SSD_G * SSD_N
SSD_CONV = SSD_INNER + 2 * SSD_BC
SSD_GW = SSD_HG * SSD_P

LANE = 128
FF_TILE = 512

IN_SPLITS = (("qkv_a", GDN_CONV), ("z_a", GDN_V), ("beta", GDN_H), ("dec", GDN_H),
             ("q_b", GLA_QK), ("k_b", GLA_QK), ("v_b", GLA_V), ("lr", GLA_RANK), ("r_b", GLA_V),
             ("z_c", SSD_INNER), ("xbc", SSD_CONV), ("dt", SSD_H), ("gates", 3 * D_MODEL))
MAIN_ORDER = ("qkv_a", "xbc", "z_c", "gates", "z_a", "q_b", "k_b", "v_b", "r_b")
SMALL_ORDER = ("beta", "dec", "lr", "dt")


def _layout():
    src, off = {}, 0
    for name, w in IN_SPLITS:
        src[name] = (off, off + w)
        off += w
    main, small, d = {}, {}, 0
    for name in MAIN_ORDER:
        w = src[name][1] - src[name][0]
        main[name] = d
        d += w
    total = d
    d = 0
    for name in SMALL_ORDER:
        small[name] = d
        d += src[name][1] - src[name][0]
    assert d <= LANE
    return src, main, small, total


W_IN_SRC, P_MAIN, P_SM, P_TOTAL = _layout()
P_QKV_A, P_XBC, P_Z_C, P_GATES, P_Z_A = (P_MAIN[k] for k in ("qkv_a", "xbc", "z_c", "gates", "z_a"))
P_Q_B, P_K_B, P_V_B, P_R_B = (P_MAIN[k] for k in ("q_b", "k_b", "v_b", "r_b"))
SM_BETA, SM_DEC, SM_LR, SM_DT = (P_SM[k] for k in SMALL_ORDER)

VMEM_LIMIT = 56 * 1024 * 1024


def _cp(sem):
    return pltpu.CompilerParams(dimension_semantics=sem, vmem_limit_bytes=VMEM_LIMIT)


def _blk(off, width):
    assert off % width == 0, (off, width)
    return off // width


def _sigmoid(x):
    return jax.nn.sigmoid(x)


def _silu(x):
    return x * jax.nn.sigmoid(x)


def _rms(x, w):
    return x * lax.rsqrt(jnp.mean(x * x, axis=-1, keepdims=True) + EPS) * w


def _dot(a, b):
    return jnp.dot(a, b, preferred_element_type=f32)


def _dot_nt(a, b):
    return lax.dot_general(a, b, (((1,), (1,)), ((), ())), preferred_element_type=f32)


def _dot_tn(a, b):
    return lax.dot_general(a, b, (((0,), (0,)), ((), ())), preferred_element_type=f32)


def _dot_hi(a, b):
    return jnp.dot(a, b, precision=HI, preferred_element_type=f32)


def _tri_masks(c):
    row = lax.broadcasted_iota(jnp.int32, (c, c), 0)
    col = lax.broadcasted_iota(jnp.int32, (c, c), 1)
    return row >= col, row > col, row == col


def _lane_col(x, idx):
    lane = lax.broadcasted_iota(jnp.int32, x.shape, 1)
    return jnp.sum(jnp.where(lane == idx, x, 0.0), axis=1, keepdims=True)


def _scalar_vec(s):
    return jnp.full((1, 1), s, f32)


def _ada_kernel(c_ref, w_ref, b_ref, o_ref):
    s = _silu(c_ref[...]).astype(bf16)
    o_ref[...] = _dot(s, w_ref[...].astype(bf16)) + b_ref[...]


def _ada_mod(c_all, w_ada, b_ada):
    nl, dm, n = w_ada.shape
    r = c_all.shape[0]
    tn = 1024
    return pl.pallas_call(
        _ada_kernel,
        grid=(nl, n // tn),
        in_specs=[pl.BlockSpec((r, dm), lambda l, j: (0, 0)),
                  pl.BlockSpec((None, dm, tn), lambda l, j: (l, 0, j)),
                  pl.BlockSpec((None, 1, tn), lambda l, j: (l, 0, j))],
        out_specs=pl.BlockSpec((None, r, tn), lambda l, j: (l, 0, j)),
        out_shape=jax.ShapeDtypeStruct((nl, r, n), f32),
        compiler_params=_cp(("arbitrary", "arbitrary")),
        name="ada_mod",
    )(c_all, w_ada, b_ada.reshape(nl, 1, n))


def _mod_spec(per_row, tm, chunk):
    if per_row:
        return pl.BlockSpec((None, tm, D_MODEL), lambda b, i, j: (b, i, chunk))
    return pl.BlockSpec((None, 1, D_MODEL), lambda b, i, j: (b, 0, chunk))


def _ffn_kernel(x_ref, sh_ref, sc_ref, gt_ref, nw_ref, wg_ref, wu_ref, wd_ref, fw_ref, o_ref,
                h_ref, acc_ref, *, final):
    f = pl.program_id(2)

    @pl.when(f == 0)
    def _():
        y = _rms(x_ref[...], nw_ref[...])
        h_ref[...] = (y * (1.0 + sc_ref[...]) + sh_ref[...]).astype(bf16)
        acc_ref[...] = jnp.zeros_like(acc_ref)

    h = h_ref[...]
    g = _dot(h, wg_ref[...])
    u = _dot(h, wu_ref[...])
    a = (_silu(g) * u).astype(bf16)
    acc_ref[...] += _dot(a, wd_ref[...])

    @pl.when(f == pl.num_programs(2) - 1)
    def _():
        y = x_ref[...] + 0.5 * gt_ref[...] * acc_ref[...]
        if final:
            y = _rms(y, fw_ref[...])
        o_ref[...] = y


def _ffn(x, mod, k0, nw, wg, wu, wd, l, fw, per_row, tm, final):
    bx, tx, dm = x.shape
    fp = wg.shape[2]
    tf = FF_TILE
    return pl.pallas_call(
        functools.partial(_ffn_kernel, final=final),
        grid=(bx, tx // tm, fp // tf),
        in_specs=[pl.BlockSpec((None, tm, dm), lambda b, i, f: (b, i, 0)),
                  _mod_spec(per_row, tm, k0), _mod_spec(per_row, tm, k0 + 1), _mod_spec(per_row, tm, k0 + 2),
                  pl.BlockSpec((1, dm), lambda b, i, f: (0, 0)),
                  pl.BlockSpec((None, dm, tf), lambda b, i, f: (l, 0, f)),
                  pl.BlockSpec((None, dm, tf), lambda b, i, f: (l, 0, f)),
                  pl.BlockSpec((None, tf, dm), lambda b, i, f: (l, f, 0)),
                  pl.BlockSpec((1, dm), lambda b, i, f: (0, 0))],
        out_specs=pl.BlockSpec((None, tm, dm), lambda b, i, f: (b, i, 0)),
        out_shape=jax.ShapeDtypeStruct(x.shape, f32),
        scratch_shapes=[pltpu.VMEM((tm, dm), bf16), pltpu.VMEM((tm, dm), f32)],
        compiler_params=_cp(("parallel", "parallel", "arbitrary")),
        name="ffn",
    )(x, mod, mod, mod, nw, wg, wu, wd, fw)


def _inproj_kernel(x_ref, sh_ref, sc_ref, nw_ref, w_ref, ws_ref, o_ref, os_ref, h_ref):
    @pl.when(pl.program_id(2) == 0)
    def _():
        y = _rms(x_ref[...], nw_ref[...])
        h = (y * (1.0 + sc_ref[...]) + sh_ref[...]).astype(bf16)
        h_ref[...] = h
        os_ref[...] = _dot(h, ws_ref[...])

    o_ref[...] = _dot(h_ref[...], w_ref[...]).astype(o_ref.dtype)


def _inproj(x, mod, nw, w, ws, l, per_row, tm, out_dtype):
    bx, tx, dm = x.shape
    n = w.shape[2]
    tn = 1024
    return pl.pallas_call(
        _inproj_kernel,
        grid=(bx, tx // tm, n // tn),
        in_specs=[pl.BlockSpec((None, tm, dm), lambda b, i, j: (b, i, 0)),
                  _mod_spec(per_row, tm, 3), _mod_spec(per_row, tm, 4),
                  pl.BlockSpec((1, dm), lambda b, i, j: (0, 0)),
                  pl.BlockSpec((None, dm, tn), lambda b, i, j: (l, 0, j)),
                  pl.BlockSpec((None, dm, LANE), lambda b, i, j: (l, 0, 0))],
        out_specs=[pl.BlockSpec((None, tm, tn), lambda b, i, j: (b, i, j)),
                   pl.BlockSpec((None, tm, LANE), lambda b, i, j: (b, i, 0))],
        out_shape=[jax.ShapeDtypeStruct((bx, tx, n), out_dtype),
                   jax.ShapeDtypeStruct((bx, tx, LANE), f32)],
        scratch_shapes=[pltpu.VMEM((tm, dm), bf16)],
        compiler_params=_cp(("parallel", "parallel", "arbitrary")),
        name="inproj",
    )(x, mod, mod, nw, w, ws)


def _conv_load(ext_ref, x_ref, cst_ref, first):
    c = x_ref.shape[0]

    @pl.when(first)
    def _():
        ext_ref[pl.ds(5, CONV_W - 1), :] = cst_ref[...]

    ext_ref[pl.ds(8, c), :] = x_ref[...].astype(f32)


def _conv_cols(ext_ref, w_ref, b_ref, lo, width, c):
    sl = pl.ds(lo, width)
    w = w_ref[:, sl]
    y = ext_ref[pl.ds(5, c), sl] * w[0:1]
    for j in range(1, CONV_W):
        y = y + ext_ref[pl.ds(5 + j, c), sl] * w[j:j + 1]
    if b_ref is not None:
        y = y + b_ref[:, sl]
    return _silu(y)


def _conv_carry(ext_ref, c):
    ext_ref[pl.ds(5, CONV_W - 1), :] = ext_ref[pl.ds(8 + c - (CONV_W - 1), CONV_W - 1), :]


def _cumsum_rows(x, tri):
    return _dot_hi(jnp.where(tri, 1.0, 0.0).astype(f32), x)


def _gdn_kernel(qkv_ref, z_ref, sm_ref, cst_ref, cw_ref, alog_ref, dtb_ref, nw_ref, s0_ref,
                o_ref, s_ref, ext_ref):
    c = qkv_ref.shape[0]
    first = pl.program_id(1) == 0

    @pl.when(first)
    def _():
        s_ref[...] = s0_ref[...]

    _conv_load(ext_ref, qkv_ref, cst_ref, first)

    sm = sm_ref[...]
    beta_l = _sigmoid(sm)
    g_l = -jnp.exp(alog_ref[...]) * jax.nn.softplus(sm + dtb_ref[...])
    tri, strict, eye = _tri_masks(c)
    gc_all = _cumsum_rows(g_l, tri)
    gc_t = gc_all.T
    eye_f = jnp.where(eye, 1.0, 0.0).astype(f32)
    n_fac = max(1, (c - 1).bit_length() - 1)

    hs = range(GDN_H)
    qb, kb, ks, decay, eg, gcs, nm, rhs = [], [], [], [], [], [], [], []
    for h in hs:
        q = _conv_cols(ext_ref, cw_ref, None, h * GDN_DK, GDN_DK, c)
        k = _conv_cols(ext_ref, cw_ref, None, GDN_QK + h * GDN_DK, GDN_DK, c)
        v = _conv_cols(ext_ref, cw_ref, None, 2 * GDN_QK + h * GDN_DV, GDN_DV, c)
        q = q * lax.rsqrt(jnp.sum(q * q, axis=-1, keepdims=True) + EPS) * (GDN_DK ** -0.5)
        k = k * lax.rsqrt(jnp.sum(k * k, axis=-1, keepdims=True) + EPS)
        beta = beta_l[:, SM_BETA + h:SM_BETA + h + 1]
        gc = gc_all[:, SM_DEC + h:SM_DEC + h + 1]
        d = jnp.exp(jnp.where(tri, gc - gc_t[SM_DEC + h:SM_DEC + h + 1, :], -jnp.inf))
        e = jnp.exp(gc)
        qb.append(q.astype(bf16))
        kb.append(k.astype(bf16))
        ks.append(k)
        decay.append(d)
        eg.append(e)
        gcs.append(gc)
        nm.append(jnp.where(strict, beta * d * _dot_nt(kb[h], kb[h]), 0.0))
        rhs.append(jnp.concatenate([v * beta, k * (beta * e)], axis=1))

    inv = [eye_f - nm[h] for h in hs]
    pw = [_dot_hi(nm[h], nm[h]) for h in hs]
    for i in range(n_fac):
        inv = [inv[h] + _dot_hi(inv[h], pw[h]) for h in hs]
        if i + 1 < n_fac:
            pw = [_dot_hi(pw[h], pw[h]) for h in hs]
    sol = [_dot_hi(inv[h], rhs[h]) for h in hs]

    s = [s_ref[h] for h in hs]
    sb = [s[h].astype(bf16) for h in hs]
    wb = [(sol[h][:, 0:GDN_DV] - _dot(sol[h][:, GDN_DV:].astype(bf16), sb[h])).astype(bf16) for h in hs]
    qk = [(_dot_nt(qb[h], kb[h]) * decay[h]).astype(bf16) for h in hs]
    o = [_dot(qb[h], sb[h]) * eg[h] + _dot(qk[h], wb[h]) for h in hs]
    for h in hs:
        g_last = gcs[h][c - 1:c]
        kd = (ks[h] * jnp.exp(g_last - gcs[h])).astype(bf16)
        s_ref[h] = s[h] * jnp.exp(g_last) + _dot_tn(kd, wb[h])
    for h in hs:
        sl = slice(h * GDN_DV, (h + 1) * GDN_DV)
        o_ref[:, sl] = (_rms(o[h], nw_ref[...]) * _silu(z_ref[:, sl].astype(f32))).astype(bf16)

    _conv_carry(ext_ref, c)


def _gdn_prompt(proj, small, conv_state, s0, conv_w, alog_l, dtb_l, norm_w):
    b, t, _ = proj.shape
    c = CHUNK
    const = lambda shape: pl.BlockSpec(shape, lambda bi, ti: (0,) * len(shape))
    st = pl.BlockSpec((None, GDN_H, GDN_DK, GDN_DV), lambda bi, ti: (bi, 0, 0, 0))
    return pl.pallas_call(
        _gdn_kernel,
        grid=(b, t // c),
        in_specs=[pl.BlockSpec((None, c, GDN_CONV), lambda bi, ti: (bi, ti, _blk(P_QKV_A, GDN_CONV))),
                  pl.BlockSpec((None, c, GDN_V), lambda bi, ti: (bi, ti, _blk(P_Z_A, GDN_V))),
                  pl.BlockSpec((None, c, LANE), lambda bi, ti: (bi, ti, 0)),
                  pl.BlockSpec((None, CONV_W - 1, GDN_CONV), lambda bi, ti: (bi, 0, 0)),
                  const((CONV_W, GDN_CONV)), const((1, LANE)), const((1, LANE)), const((1, GDN_DV)),
                  st],
        out_specs=[pl.BlockSpec((None, c, GDN_V), lambda bi, ti: (bi, ti, 0)), st],
        out_shape=[jax.ShapeDtypeStruct((b, t, GDN_V), bf16),
                   jax.ShapeDtypeStruct((b, GDN_H, GDN_DK, GDN_DV), f32)],
        scratch_shapes=[pltpu.VMEM((8 + c, GDN_CONV), f32)],
        compiler_params=_cp(("parallel", "arbitrary")),
        name="gdn_prompt",
    )(proj, proj, small, conv_state, conv_w, alog_l, dtb_l, norm_w, s0)


GLA_SUB = 16


def _gla_kernel(q_ref, k_ref, v_ref, r_ref, sm_ref, wgate_ref, bgate_ref, nw_ref, s0_ref,
                o_ref, so_ref, st_ref):
    c = q_ref.shape[0]
    ti = pl.program_id(1)

    @pl.when(ti == 0)
    def _():
        for h in range(GLA_H):
            st_ref[h] = s0_ref[h].T

    sm = sm_ref[...]
    lane = lax.broadcasted_iota(jnp.int32, sm.shape, 1)
    lr = jnp.where((lane >= SM_LR) & (lane < SM_LR + GLA_RANK), sm, 0.0).astype(bf16)
    la = jax.nn.log_sigmoid(_dot(lr, wgate_ref[...]) + bgate_ref[...]) / GLA_TAU
    tri, _, _ = _tri_masks(c)
    b_all = _cumsum_rows(la, tri)

    col = lax.broadcasted_iota(jnp.int32, (GLA_SUB, c), 1)
    row = lax.broadcasted_iota(jnp.int32, (GLA_SUB, c), 0)
    hs = range(GLA_H)
    qs, ks, vbs, bs, atts = [], [], [], [], []
    for h in hs:
        sk = slice(h * GLA_DK, (h + 1) * GLA_DK)
        q = q_ref[:, sk].astype(f32) * (GLA_DK ** -0.5)
        k = k_ref[:, sk].astype(f32)
        b = b_all[:, sk]
        qs.append(q)
        ks.append(k)
        bs.append(b)
        vbs.append(v_ref[:, h * GLA_DV:(h + 1) * GLA_DV].astype(bf16))
        att_rows = []
        for i in range(c // GLA_SUB):
            lo = i * GLA_SUB
            b_i = b[lo:lo + GLA_SUB]
            q_i = q[lo:lo + GLA_SUB]
            b_top = b[lo:lo + 1]
            att = jnp.zeros((GLA_SUB, c), f32)
            if i > 0:
                q_t = (q_i * jnp.exp(b_i - b_top)).astype(bf16)
                k_t = (k * jnp.exp(jnp.minimum(b_top - b, 0.0))).astype(bf16)
                att = jnp.where(col < lo, _dot_nt(q_t, k_t), 0.0)
            for sl in range(GLA_SUB):
                sidx = lo + sl
                e = jnp.exp(jnp.minimum(b_i - b[sidx:sidx + 1], 0.0))
                p = jnp.sum(q_i * k[sidx:sidx + 1] * e, axis=1, keepdims=True)
                att = jnp.where((col == sidx) & (row >= sl), p, att)
            att_rows.append(att)
        atts.append(jnp.concatenate(att_rows, axis=0).astype(bf16))

    sts = [st_ref[h] for h in hs]
    os_ = [_dot(atts[h], vbs[h]) + _dot_nt((qs[h] * jnp.exp(bs[h])).astype(bf16), sts[h].astype(bf16))
           for h in hs]
    st_new = []
    for h in hs:
        b_last = bs[h][c - 1:c]
        kd = (ks[h] * jnp.exp(b_last - bs[h])).astype(bf16)
        st_new.append(sts[h] * jnp.exp(b_last) + _dot_tn(vbs[h], kd))
        st_ref[h] = st_new[h]

    @pl.when(ti == pl.num_programs(1) - 1)
    def _():
        for h in hs:
            so_ref[h] = st_new[h].T

    for h in hs:
        sv = slice(h * GLA_DV, (h + 1) * GLA_DV)
        o_ref[:, sv] = (_rms(os_[h], nw_ref[...]) * _silu(r_ref[:, sv].astype(f32))).astype(bf16)


def _gla_prompt(proj, small, s0, wgate_pad, bgate, norm_w):
    b, t, _ = proj.shape
    c = CHUNK
    const = lambda shape: pl.BlockSpec(shape, lambda bi, ti: (0,) * len(shape))
    st = pl.BlockSpec((None, GLA_H, GLA_DK, GLA_DV), lambda bi, ti: (bi, 0, 0, 0))
    return pl.pallas_call(
        _gla_kernel,
        grid=(b, t // c),
        in_specs=[pl.BlockSpec((None, c, GLA_QK), lambda bi, ti: (bi, ti, _blk(P_Q_B, GLA_QK))),
                  pl.BlockSpec((None, c, GLA_QK), lambda bi, ti: (bi, ti, _blk(P_K_B, GLA_QK))),
                  pl.BlockSpec((None, c, GLA_V), lambda bi, ti: (bi, ti, _blk(P_V_B, GLA_V))),
                  pl.BlockSpec((None, c, GLA_V), lambda bi, ti: (bi, ti, _blk(P_R_B, GLA_V))),
                  pl.BlockSpec((None, c, LANE), lambda bi, ti: (bi, ti, 0)),
                  const((LANE, GLA_QK)), const((1, GLA_QK)), const((1, GLA_DV)),
                  st],
        out_specs=[pl.BlockSpec((None, c, GLA_V), lambda bi, ti: (bi, ti, 0)), st],
        out_shape=[jax.ShapeDtypeStruct((b, t, GLA_V), bf16),
                   jax.ShapeDtypeStruct((b, GLA_H, GLA_DK, GLA_DV), f32)],
        scratch_shapes=[pltpu.VMEM((GLA_H, GLA_DV, GLA_DK), f32)],
        compiler_params=_cp(("parallel", "arbitrary")),
        name="gla_prompt",
    )(proj, proj, proj, proj, small, wgate_pad, bgate, norm_w, s0)


def _ssd_kernel(z_ref, xbc_ref, sm_ref, cst_ref, cw_ref, cb_ref, avec_ref, dtb_ref, dvec_ref, nw_ref, h0_ref,
                o_ref, h_ref, ext_ref):
    c = xbc_ref.shape[0]
    first = pl.program_id(1) == 0

    @pl.when(first)
    def _():
        h_ref[...] = h0_ref[...]

    _conv_load(ext_ref, xbc_ref, cst_ref, first)

    dt_l = jax.nn.softplus(sm_ref[...] + dtb_ref[...])
    tri, _, _ = _tri_masks(c)
    ac_all = _cumsum_rows(dt_l * avec_ref[...], tri)
    ac_t2 = jnp.concatenate([ac_all, ac_all], axis=0).T

    lane = lax.broadcasted_iota(jnp.int32, (c, 2 * SSD_P), 1)
    rowi = lax.broadcasted_iota(jnp.int32, (c, 2 * SSD_P), 0)
    lo_half = lane < SSD_P
    tri2 = rowi >= jnp.where(lo_half, lane, lane - SSD_P)
    lane1 = lane[0:1]
    gs_ = range(SSD_G)
    ps_ = range(SSD_HG // 2)

    xs_, bmb, hg, cb2, ch = [], [], [], [], []
    for g in gs_:
        xs_.append(_conv_cols(ext_ref, cw_ref, cb_ref, g * SSD_GW, SSD_GW, c))
        bmb.append(_conv_cols(ext_ref, cw_ref, cb_ref, SSD_INNER + g * SSD_N, SSD_N, c).astype(bf16))
        cmb = _conv_cols(ext_ref, cw_ref, cb_ref, SSD_INNER + SSD_BC + g * SSD_N, SSD_N, c).astype(bf16)
        hg.append(h_ref[pl.ds(g * SSD_HG, SSD_HG)].reshape(SSD_GW, SSD_N))
        cb2.append(_dot_nt(cmb, jnp.concatenate([bmb[g], bmb[g]], axis=0)))
        ch.append(_dot_nt(cmb, hg[g].astype(bf16)))

    m2, rhs, e_col, xsc = {}, {}, {}, {}
    for g in gs_:
        for p in ps_:
            l0 = SM_DT + g * SSD_HG + 2 * p
            ps = slice(p * 2 * SSD_P, (p + 1) * 2 * SSD_P)
            ac_col = jnp.where(lo_half, ac_all[:, l0:l0 + 1], ac_all[:, l0 + 1:l0 + 2])
            ac_row = jnp.where(lane1 < SSD_P, ac_t2[l0:l0 + 1, :], ac_t2[l0 + 1:l0 + 2, :])
            decay = jnp.exp(jnp.where(tri2, ac_col - ac_row, -jnp.inf))
            m2[g, p] = (cb2[g] * decay).astype(bf16)
            dt2 = jnp.where(lo_half, dt_l[:, l0:l0 + 1], dt_l[:, l0 + 1:l0 + 2])
            xdt = xs_[g][:, ps] * dt2
            rhs[g, p] = jnp.concatenate([jnp.where(lo_half, xdt, 0.0), jnp.where(lo_half, 0.0, xdt)],
                                        axis=0).astype(bf16)
            e_col[g, p] = jnp.exp(ac_col)
            xsc[g, p] = xdt * jnp.exp(ac_col[c - 1:c] - ac_col)
    y2 = {gp: _dot(m2[gp], rhs[gp]) for gp in m2}

    for g in gs_:
        xsg = jnp.concatenate([xsc[g, p] for p in ps_], axis=1).astype(bf16)
        dh = _dot_tn(xsg, bmb[g])
        for j in range(SSD_HG):
            ln = SM_DT + g * SSD_HG + j
            h_ref[g * SSD_HG + j] = (hg[g][j * SSD_P:(j + 1) * SSD_P] * jnp.exp(ac_all[c - 1:c, ln:ln + 1])
                                     + dh[j * SSD_P:(j + 1) * SSD_P])
    for g in gs_:
        gs = slice(g * SSD_GW, (g + 1) * SSD_GW)
        y = jnp.concatenate([y2[g, p] + ch[g][:, p * 2 * SSD_P:(p + 1) * 2 * SSD_P] * e_col[g, p] for p in ps_],
                            axis=1)
        y = (y + dvec_ref[:, gs] * xs_[g]) * _silu(z_ref[:, gs].astype(f32))
        o_ref[:, gs] = _rms(y, nw_ref[:, gs]).astype(bf16)

    _conv_carry(ext_ref, c)


def _ssd_prompt(proj, small, conv_state, h0, conv_w, conv_b, avec_l, dtb_l, dvec, norm_w):
    b, t, _ = proj.shape
    c = CHUNK
    const = lambda shape: pl.BlockSpec(shape, lambda bi, ti: (0,) * len(shape))
    st = pl.BlockSpec((None, SSD_H, SSD_P, SSD_N), lambda bi, ti: (bi, 0, 0, 0))
    return pl.pallas_call(
        _ssd_kernel,
        grid=(b, t // c),
        in_specs=[pl.BlockSpec((None, c, SSD_INNER), lambda bi, ti: (bi, ti, _blk(P_Z_C, SSD_INNER))),
                  pl.BlockSpec((None, c, SSD_CONV), lambda bi, ti: (bi, ti, _blk(P_XBC, SSD_CONV))),
                  pl.BlockSpec((None, c, LANE), lambda bi, ti: (bi, ti, 0)),
                  pl.BlockSpec((None, CONV_W - 1, SSD_CONV), lambda bi, ti: (bi, 0, 0)),
                  const((CONV_W, SSD_CONV)), const((1, SSD_CONV)), const((1, LANE)), const((1, LANE)),
                  const((1, SSD_INNER)), const((1, SSD_INNER)),
                  st],
        out_specs=[pl.BlockSpec((None, c, SSD_INNER), lambda bi, ti: (bi, ti, 0)), st],
        out_shape=[jax.ShapeDtypeStruct((b, t, SSD_INNER), bf16),
                   jax.ShapeDtypeStruct((b, SSD_H, SSD_P, SSD_N), f32)],
        scratch_shapes=[pltpu.VMEM((8 + c, SSD_CONV), f32)],
        compiler_params=_cp(("parallel", "arbitrary")),
        name="ssd_prompt",
    )(proj, proj, small, conv_state, conv_w, conv_b, avec_l, dtb_l, dvec, norm_w, h0)


STEP_B = 8


def _conv_step(x_ref, c_ref, w_ref):
    w = w_ref[...]
    return c_ref[0] * w[0:1] + c_ref[1] * w[1:2] + c_ref[2] * w[2:3] + x_ref[...] * w[3:4]


def _state_call(kernel, grid, in_specs, out_specs, out_shape, scratch, sem, name, args, prev_state):
    aliases = {}
    if prev_state is not None:
        in_specs = in_specs + [pl.BlockSpec(memory_space=pl.ANY)]
        args = args + (prev_state,)
        aliases = {len(args) - 1: len(out_shape) - 1}
        kernel = functools.partial(_drop_alias_ref, kernel, len(args) - 1)
    return pl.pallas_call(kernel, grid=grid, in_specs=in_specs, out_specs=out_specs, out_shape=out_shape,
                          scratch_shapes=scratch, input_output_aliases=aliases,
                          compiler_params=_cp(sem), name=name)(*args)


def _drop_alias_ref(kernel, idx, *refs):
    return kernel(*refs[:idx], *refs[idx + 1:])


def _gdn_step_kernel(alog_ref, dtb_ref, q_ref, k_ref, v_ref, z_ref, sm_ref, cq_ref, ck_ref, cv_ref,
                     wq_ref, wk_ref, wv_ref, nw_ref, s_ref, o_ref, so_ref):
    nb = q_ref.shape[0]
    qa = _silu(_conv_step(q_ref, cq_ref, wq_ref))
    ka = _silu(_conv_step(k_ref, ck_ref, wk_ref))
    va = _silu(_conv_step(v_ref, cv_ref, wv_ref))
    sm = sm_ref[...]
    z = z_ref[...]
    for h in range(GDN_H):
        sl = slice(h * GDN_DK, (h + 1) * GDN_DK)
        q = qa[:, sl]
        k = ka[:, sl]
        v = va[:, sl]
        q = q * lax.rsqrt(jnp.sum(q * q, axis=-1, keepdims=True) + EPS) * (GDN_DK ** -0.5)
        k = k * lax.rsqrt(jnp.sum(k * k, axis=-1, keepdims=True) + EPS)
        beta = _sigmoid(sm[:, SM_BETA + h:SM_BETA + h + 1])
        g = -jnp.exp(_scalar_vec(alog_ref[h])) * jax.nn.softplus(sm[:, SM_DEC + h:SM_DEC + h + 1] + dtb_ref[h])
        eg = jnp.exp(g)
        qk = jnp.sum(q * k, axis=-1, keepdims=True)
        q_t = q.T
        k_t = k.T
        rows = []
        for b in range(nb):
            s = s_ref[b, h]
            kcol = k_t[:, b:b + 1]
            ks = jnp.sum(s * kcol, axis=0, keepdims=True)
            qs = jnp.sum(s * q_t[:, b:b + 1], axis=0, keepdims=True)
            bt = beta[b:b + 1]
            e = eg[b:b + 1]
            w = bt * v[b:b + 1] - (bt * e) * ks
            rows.append(qs * e + qk[b:b + 1] * w)
            so_ref[b, h] = s * e + kcol * w
        o = jnp.concatenate(rows, axis=0)
        o = _rms(o, nw_ref[...]) * _silu(z[:, sl])
        o_ref[:, sl] = o.astype(bf16)


def _gdn_step(proj, small, conv_t, states, prev, l, conv_w, a_log, dt_bias, norm_w):
    n = proj.shape[0]
    nb = STEP_B
    blk = lambda off: pl.BlockSpec((nb, GDN_QK), lambda i: (i, _blk(off, GDN_QK)))
    cst = lambda j: pl.BlockSpec((CONV_W - 1, nb, GDN_QK), lambda i: (0, i, j))
    cw = lambda j: pl.BlockSpec((CONV_W, GDN_QK), lambda i: (0, j))
    smem = pl.BlockSpec(memory_space=pltpu.SMEM)
    st = pl.BlockSpec((None, nb, GDN_H, GDN_DK, GDN_DV), lambda i: (l, i, 0, 0, 0))
    return _state_call(
        _gdn_step_kernel, (n // nb,),
        [smem, smem, blk(P_QKV_A), blk(P_QKV_A + GDN_QK), blk(P_QKV_A + 2 * GDN_QK), blk(P_Z_A),
         pl.BlockSpec((nb, LANE), lambda i: (i, 0)),
         cst(0), cst(1), cst(2), cw(0), cw(1), cw(2),
         pl.BlockSpec((1, GDN_DV), lambda i: (0, 0)), st],
        [pl.BlockSpec((nb, GDN_V), lambda i: (i, 0)), st],
        [jax.ShapeDtypeStruct((n, GDN_V), bf16), jax.ShapeDtypeStruct(states.shape, f32)],
        [], ("parallel",), "gdn_step",
        (a_log, dt_bias, proj, proj, proj, proj, small, conv_t, conv_t, conv_t, conv_w, conv_w, conv_w, norm_w, states),
        prev)


def _gla_step_kernel(q_ref, k_ref, v_ref, r_ref, sm_ref, wgate_ref, bgate_ref, nw_ref, s_ref,
                     o_ref, so_ref):
    nb = q_ref.shape[0]
    sm = sm_ref[...]
    lane = lax.broadcasted_iota(jnp.int32, sm.shape, 1)
    lr = jnp.where((lane >= SM_LR) & (lane < SM_LR + GLA_RANK), sm, 0.0).astype(bf16)
    la_all = jax.nn.log_sigmoid(_dot(lr, wgate_ref[...]) + bgate_ref[...]) / GLA_TAU
    qa = q_ref[...] * (GLA_DK ** -0.5)
    ka = k_ref[...]
    va = v_ref[...]
    r = r_ref[...]
    for h in range(GLA_H):
        sk = slice(h * GLA_DK, (h + 1) * GLA_DK)
        sv = slice(h * GLA_DV, (h + 1) * GLA_DV)
        q = qa[:, sk]
        k = ka[:, sk]
        v = va[:, sv]
        e = jnp.exp(la_all[:, sk])
        qk = jnp.sum(q * k, axis=-1, keepdims=True)
        e_t = e.T
        k_t = k.T
        qe_t = (q * e).T
        rows = []
        for b in range(nb):
            s = s_ref[b, h]
            vrow = v[b:b + 1]
            rows.append(qk[b:b + 1] * vrow + jnp.sum(s * qe_t[:, b:b + 1], axis=0, keepdims=True))
            so_ref[b, h] = s * e_t[:, b:b + 1] + k_t[:, b:b + 1] * vrow
        o = jnp.concatenate(rows, axis=0)
        o = _rms(o, nw_ref[...]) * _silu(r[:, sv])
        o_ref[:, sv] = o.astype(bf16)


def _gla_step(proj, small, states, prev, l, wgate_pad, bgate, norm_w):
    n = proj.shape[0]
    nb = STEP_B
    st = pl.BlockSpec((None, nb, GLA_H, GLA_DK, GLA_DV), lambda i: (l, i, 0, 0, 0))
    return _state_call(
        _gla_step_kernel, (n // nb,),
        [pl.BlockSpec((nb, GLA_QK), lambda i: (i, _blk(P_Q_B, GLA_QK))),
         pl.BlockSpec((nb, GLA_QK), lambda i: (i, _blk(P_K_B, GLA_QK))),
         pl.BlockSpec((nb, GLA_V), lambda i: (i, _blk(P_V_B, GLA_V))),
         pl.BlockSpec((nb, GLA_V), lambda i: (i, _blk(P_R_B, GLA_V))),
         pl.BlockSpec((nb, LANE), lambda i: (i, 0)),
         pl.BlockSpec((LANE, GLA_QK), lambda i: (0, 0)),
         pl.BlockSpec((1, GLA_QK), lambda i: (0, 0)),
         pl.BlockSpec((1, GLA_DV), lambda i: (0, 0)), st],
        [pl.BlockSpec((nb, GLA_V), lambda i: (i, 0)), st],
        [jax.ShapeDtypeStruct((n, GLA_V), bf16), jax.ShapeDtypeStruct(states.shape, f32)],
        [], ("parallel",), "gla_step",
        (proj, proj, proj, proj, small, wgate_pad, bgate, norm_w, states),
        prev)


def _ssd_step_kernel(alog_ref, dtb_ref, z_ref, x_ref, bm_ref, cm_ref, sm_ref, cx_ref, cb_ref, cc_ref,
                     wx_ref, wb_ref, wc_ref, bx_ref, bb_ref, bc_ref, dvec_ref, nw_ref, h_ref,
                     o_ref, ho_ref, y_ref):
    g = pl.program_id(1)
    nb = x_ref.shape[0]
    x = _silu(_conv_step(x_ref, cx_ref, wx_ref) + bx_ref[...])
    bm = _silu(_conv_step(bm_ref, cb_ref, wb_ref) + bb_ref[...])
    cm = _silu(_conv_step(cm_ref, cc_ref, wc_ref) + bc_ref[...])
    sm = sm_ref[...]
    cbdot = jnp.sum(cm * bm, axis=-1, keepdims=True)
    lane_b = lax.broadcasted_iota(jnp.int32, (SSD_P, nb), 1)
    for jp in range(SSD_HG // 2):
        x_t = x[:, jp * LANE:(jp + 1) * LANE].T
        halves = []
        for jj in range(2):
            j = 2 * jp + jj
            hd = g * SSD_HG + j
            dt = jax.nn.softplus(_lane_col(sm, SM_DT + hd) + dtb_ref[hd])
            ea = jnp.exp(dt * (-jnp.exp(_scalar_vec(alog_ref[hd]))))
            y_t = jnp.zeros((SSD_P, nb), f32)
            for b in range(nb):
                hh = h_ref[b, j]
                xdt = x_t[jj * SSD_P:(jj + 1) * SSD_P, b:b + 1] * dt[b:b + 1]
                eab = ea[b:b + 1]
                ycol = jnp.sum(hh * cm[b:b + 1], axis=-1, keepdims=True) * eab + cbdot[b:b + 1] * xdt
                ho_ref[b, j] = hh * eab + xdt * bm[b:b + 1]
                y_t = jnp.where(lane_b == b, ycol, y_t)
            halves.append(y_t)
        y_ref[:, jp * LANE:(jp + 1) * LANE] = jnp.concatenate(halves, axis=0).T
    y = (y_ref[...] + dvec_ref[...] * x) * _silu(z_ref[...])
    o_ref[...] = _rms(y, nw_ref[...]).astype(bf16)


def _ssd_step(proj, small, conv_t, states, prev, l, conv_w, conv_b, a_log, dt_bias, dvec, norm_w):
    n = proj.shape[0]
    nb = STEP_B
    gw = SSD_GW
    nbk = SSD_INNER // SSD_N
    smem = pl.BlockSpec(memory_space=pltpu.SMEM)
    st = pl.BlockSpec((None, nb, SSD_HG, SSD_P, SSD_N), lambda i, g: (l, i, g, 0, 0))
    return _state_call(
        _ssd_step_kernel, (n // nb, SSD_G),
        [smem, smem,
         pl.BlockSpec((nb, gw), lambda i, g: (i, _blk(P_Z_C, gw) + g)),
         pl.BlockSpec((nb, gw), lambda i, g: (i, _blk(P_XBC, gw) + g)),
         pl.BlockSpec((nb, SSD_N), lambda i, g: (i, _blk(P_XBC + SSD_INNER, SSD_N) + g)),
         pl.BlockSpec((nb, SSD_N), lambda i, g: (i, _blk(P_XBC + SSD_INNER + SSD_BC, SSD_N) + g)),
         pl.BlockSpec((nb, LANE), lambda i, g: (i, 0)),
         pl.BlockSpec((CONV_W - 1, nb, gw), lambda i, g: (0, i, g)),
         pl.BlockSpec((CONV_W - 1, nb, SSD_N), lambda i, g: (0, i, nbk + g)),
         pl.BlockSpec((CONV_W - 1, nb, SSD_N), lambda i, g: (0, i, nbk + SSD_G + g)),
         pl.BlockSpec((CONV_W, gw), lambda i, g: (0, g)),
         pl.BlockSpec((CONV_W, SSD_N), lambda i, g: (0, nbk + g)),
         pl.BlockSpec((CONV_W, SSD_N), lambda i, g: (0, nbk + SSD_G + g)),
         pl.BlockSpec((1, gw), lambda i, g: (0, g)),
         pl.BlockSpec((1, SSD_N), lambda i, g: (0, nbk + g)),
         pl.BlockSpec((1, SSD_N), lambda i, g: (0, nbk + SSD_G + g)),
         pl.BlockSpec((1, gw), lambda i, g: (0, g)),
         pl.BlockSpec((1, gw), lambda i, g: (0, g)), st],
        [pl.BlockSpec((nb, gw), lambda i, g: (i, g)), st],
        [jax.ShapeDtypeStruct((n, SSD_INNER), bf16), jax.ShapeDtypeStruct(states.shape, f32)],
        [pltpu.VMEM((nb, gw), f32)], ("parallel", "parallel"), "ssd_step",
        (a_log, dt_bias, proj, proj, proj, proj, small, conv_t, conv_t, conv_t,
         conv_w, conv_w, conv_w, conv_b, conv_b, conv_b, dvec, norm_w, states),
        prev)


def _merge_kernel(oa_ref, ob_ref, oc_ref, ga_ref, gb_ref, gc_ref, wa_ref, wb_ref, wc_ref, o_ref):
    m = (_sigmoid(ga_ref[...].astype(f32)) * _dot(oa_ref[...], wa_ref[...])
         + _sigmoid(gb_ref[...].astype(f32)) * _dot(ob_ref[...], wb_ref[...])
         + _sigmoid(gc_ref[...].astype(f32)) * _dot(oc_ref[...], wc_ref[...]))
    o_ref[...] = m.astype(bf16)


def _merge(oa, ob, oc, proj, wa, wb, wc, l, tm):
    bx, tx, _ = oa.shape
    tn = 512
    gate = lambda k: pl.BlockSpec((None, tm, tn), lambda b, i, j: (b, i, _blk(P_GATES + k * D_MODEL, tn) + j))
    act = lambda w: pl.BlockSpec((None, tm, w), lambda b, i, j: (b, i, 0))
    wsp = lambda w: pl.BlockSpec((None, w, tn), lambda b, i, j: (l, 0, j))
    return pl.pallas_call(
        _merge_kernel,
        grid=(bx, tx // tm, D_MODEL // tn),
        in_specs=[act(GDN_V), act(GLA_V), act(SSD_INNER), gate(0), gate(1), gate(2),
                  wsp(GDN_V), wsp(GLA_V), wsp(SSD_INNER)],
        out_specs=pl.BlockSpec((None, tm, tn), lambda b, i, j: (b, i, j)),
        out_shape=jax.ShapeDtypeStruct((bx, tx, D_MODEL), bf16),
        compiler_params=_cp(("parallel", "parallel", "arbitrary")),
        name="merge",
    )(oa, ob, oc, proj, proj, proj, wa, wb, wc)


def _outproj_kernel(m_ref, x_ref, gt_ref, w_ref, o_ref):
    o_ref[...] = x_ref[...] + gt_ref[...] * _dot(m_ref[...], w_ref[...])


def _outproj(merged, x, mod, w, l, per_row, tm):
    bx, tx, dm = x.shape
    tn = 512
    nj = dm // tn
    if per_row:
        gspec = pl.BlockSpec((None, tm, tn), lambda b, i, j: (b, i, 5 * nj + j))
    else:
        gspec = pl.BlockSpec((None, 1, tn), lambda b, i, j: (b, 0, 5 * nj + j))
    return pl.pallas_call(
        _outproj_kernel,
        grid=(bx, tx // tm, nj),
        in_specs=[pl.BlockSpec((None, tm, dm), lambda b, i, j: (b, i, 0)),
                  pl.BlockSpec((None, tm, tn), lambda b, i, j: (b, i, j)),
                  gspec,
                  pl.BlockSpec((None, dm, tn), lambda b, i, j: (l, 0, j))],
        out_specs=pl.BlockSpec((None, tm, tn), lambda b, i, j: (b, i, j)),
        out_shape=jax.ShapeDtypeStruct(x.shape, f32),
        compiler_params=_cp(("parallel", "parallel", "arbitrary")),
        name="outproj",
    )(merged, x, mod, w)


def _permute_kernel(w_ref, o_ref, s_ref):
    for name in MAIN_ORDER:
        a, b = W_IN_SRC[name]
        o_ref[:, P_MAIN[name]:P_MAIN[name] + b - a] = w_ref[:, a:b].astype(bf16)
    s_ref[...] = jnp.zeros_like(s_ref)
    for name in SMALL_ORDER:
        a, b = W_IN_SRC[name]
        s_ref[:, P_SM[name]:P_SM[name] + b - a] = w_ref[:, a:b].astype(bf16)


def _permute_w_in(w):
    nl, d, n = w.shape
    tk = 128
    return pl.pallas_call(
        _permute_kernel,
        grid=(nl, d // tk),
        in_specs=[pl.BlockSpec((None, tk, n), lambda l, i: (l, i, 0))],
        out_specs=[pl.BlockSpec((None, tk, P_TOTAL), lambda l, i: (l, i, 0)),
                   pl.BlockSpec((None, tk, LANE), lambda l, i: (l, i, 0))],
        out_shape=[jax.ShapeDtypeStruct((nl, d, P_TOTAL), bf16), jax.ShapeDtypeStruct((nl, d, LANE), bf16)],
        compiler_params=_cp(("parallel", "parallel")),
        name="permute_w_in",
    )(w)


def _pad_ff(w, axis):
    f = w.shape[axis]
    fp = -(-f // FF_TILE) * FF_TILE
    widths = [(0, 0)] * w.ndim
    widths[axis] = (0, fp - f)
    return jnp.pad(w.astype(bf16), widths)


def _lane_vec(v, lo):
    return jnp.zeros((1, LANE), f32).at[0, lo:lo + v.shape[0]].set(v)


def _layer_params(l, p):
    row = lambda a: a[l].reshape(1, -1)
    wgate = jnp.zeros((LANE, GLA_QK), f32).at[SM_LR:SM_LR + GLA_RANK].set(p["gla_w_gate"][l]).astype(bf16)
    return dict(
        norm1=row(p["norm1"]), norm2=row(p["norm2"]), norm3=row(p["norm3"]),
        gdn_conv_w=p["gdn_conv_w"][l], gdn_a_log=p["gdn_a_log"][l], gdn_dt_bias=p["gdn_dt_bias"][l],
        gdn_alog_l=_lane_vec(p["gdn_a_log"][l], SM_DEC), gdn_dtb_l=_lane_vec(p["gdn_dt_bias"][l], SM_DEC),
        gdn_norm_w=row(p["gdn_norm_w"]),
        gla_wgate=wgate, gla_bgate=row(p["gla_b_gate"]), gla_norm_w=row(p["gla_norm_w"]),
        ssd_conv_w=p["ssd_conv_w"][l], ssd_conv_b=row(p["ssd_conv_b"]), ssd_a_log=p["ssd_a_log"][l],
        ssd_dt_bias=p["ssd_dt_bias"][l],
        ssd_avec_l=_lane_vec(-jnp.exp(p["ssd_a_log"][l]), SM_DT), ssd_dtb_l=_lane_vec(p["ssd_dt_bias"][l], SM_DT),
        ssd_dvec=jnp.repeat(p["ssd_d"][l], SSD_P).reshape(1, -1),
        ssd_norm_w=row(p["ssd_norm_w"]),
    )


def _stacked_weights(p):
    w_in, w_in_small = _permute_w_in(p["w_in"])
    return dict(
        f1=(_pad_ff(p["ffn1_wg"], 2), _pad_ff(p["ffn1_wu"], 2), _pad_ff(p["ffn1_wd"], 1)),
        f2=(_pad_ff(p["ffn2_wg"], 2), _pad_ff(p["ffn2_wu"], 2), _pad_ff(p["ffn2_wd"], 1)),
        w_in=w_in, w_in_small=w_in_small,
        wa=p["w_branch_gdn"].astype(bf16), wb=p["w_branch_gla"].astype(bf16),
        wc=p["w_branch_ssd"].astype(bf16), w_out=p["w_out"].astype(bf16),
    )


def _new_conv_state(buf, raw):
    t = raw.shape[1]
    k = CONV_W - 1
    if t >= k:
        return raw[:, t - k:]
    return jnp.concatenate([buf[:, t:], raw], axis=1)


def _mixer_prompt(proj, small, lp, st):
    gdn_conv, s_gdn, s_gla, ssd_conv, s_ssd = st
    oa, s_gdn_n = _gdn_prompt(proj, small, gdn_conv, s_gdn, lp["gdn_conv_w"], lp["gdn_alog_l"], lp["gdn_dtb_l"],
                              lp["gdn_norm_w"])
    ob, s_gla_n = _gla_prompt(proj, small, s_gla, lp["gla_wgate"], lp["gla_bgate"], lp["gla_norm_w"])
    oc, s_ssd_n = _ssd_prompt(proj, small, ssd_conv, s_ssd, lp["ssd_conv_w"], lp["ssd_conv_b"], lp["ssd_avec_l"],
                              lp["ssd_dtb_l"], lp["ssd_dvec"], lp["ssd_norm_w"])
    gdn_conv_n = _new_conv_state(gdn_conv, proj[:, :, P_QKV_A:P_QKV_A + GDN_CONV]).astype(f32)
    ssd_conv_n = _new_conv_state(ssd_conv, proj[:, :, P_XBC:P_XBC + SSD_CONV]).astype(f32)
    return (oa, ob, oc), (gdn_conv_n, s_gdn_n, s_gla_n, ssd_conv_n, s_ssd_n)


def _mixer_sample(proj, small, lp, l, states, prev):
    n = proj.shape[1]
    p2 = proj.reshape(n, P_TOTAL)
    s2 = small.reshape(n, LANE)
    gdn_conv, ssd_conv = states[0][l], states[3][l]
    gct = jnp.swapaxes(gdn_conv, 0, 1)
    sct = jnp.swapaxes(ssd_conv, 0, 1)
    pv = (None,) * 5 if prev is None else prev
    oa, s_gdn_n = _gdn_step(p2, s2, gct, states[1], pv[1], l, lp["gdn_conv_w"], lp["gdn_a_log"],
                            lp["gdn_dt_bias"], lp["gdn_norm_w"])
    ob, s_gla_n = _gla_step(p2, s2, states[2], pv[2], l, lp["gla_wgate"], lp["gla_bgate"], lp["gla_norm_w"])
    oc, s_ssd_n = _ssd_step(p2, s2, sct, states[4], pv[4], l, lp["ssd_conv_w"], lp["ssd_conv_b"], lp["ssd_a_log"],
                            lp["ssd_dt_bias"], lp["ssd_dvec"], lp["ssd_norm_w"])
    raw = p2.reshape(n, 1, P_TOTAL)
    gdn_conv_n = _new_conv_state(gdn_conv, raw[:, :, P_QKV_A:P_QKV_A + GDN_CONV])
    ssd_conv_n = _new_conv_state(ssd_conv, raw[:, :, P_XBC:P_XBC + SSD_CONV])
    outs = tuple(o.reshape(1, n, -1) for o in (oa, ob, oc))
    return outs, (gdn_conv_n, s_gdn_n, s_gla_n, ssd_conv_n, s_ssd_n)


def _trunk(x, mods, lps, sw, states, per_row, tm, tm_in, final_w):
    nl = len(lps)
    per_layer = []
    prev = None
    for l in range(nl):
        lp, mod = lps[l], mods[l]
        last = l == nl - 1
        x = _ffn(x, mod, 0, lp["norm1"], *sw["f1"], l, lp["norm1"], per_row, tm, False)
        proj, small = _inproj(x, mod, lp["norm2"], sw["w_in"], sw["w_in_small"], l, per_row, tm_in,
                              f32 if per_row else bf16)
        if per_row:
            (oa, ob, oc), st = _mixer_sample(proj, small, lp, l, states, prev)
            prev = st
        else:
            (oa, ob, oc), st = _mixer_prompt(proj, small, lp, tuple(s[l] for s in states))
        per_layer.append(st)
        merged = _merge(oa, ob, oc, proj, sw["wa"], sw["wb"], sw["wc"], l, tm)
        x = _outproj(merged, x, mod, sw["w_out"], l, per_row, tm)
        x = _ffn(x, mod, 6, lp["norm3"], *sw["f2"], l, final_w if last else lp["norm3"], per_row, tm, last)
    stack = lambda i: jnp.stack([st[i] for st in per_layer])
    if per_row:
        new_states = (stack(0), prev[1], prev[2], stack(3), prev[4])
    else:
        new_states = tuple(stack(i) for i in range(5))
    return x, new_states


def kernel(x_prompt, x_sample, state_gdn_conv, state_gdn, state_gla, state_ssd_conv, state_ssd, c_prompt, c_sample, w_ada, b_ada, norm1, norm2, norm3, ffn1_wg, ffn1_wu, ffn1_wd, ffn2_wg, ffn2_wu, ffn2_wd, w_in, gdn_conv_w, gdn_a_log, gdn_dt_bias, gdn_norm_w, gla_w_gate, gla_b_gate, gla_norm_w, ssd_conv_w, ssd_conv_b, ssd_a_log, ssd_dt_bias, ssd_d, ssd_norm_w, w_branch_gdn, w_branch_gla, w_branch_ssd, w_out, final_norm):
    p = dict(norm1=norm1, norm2=norm2, norm3=norm3,
             ffn1_wg=ffn1_wg, ffn1_wu=ffn1_wu, ffn1_wd=ffn1_wd, ffn2_wg=ffn2_wg, ffn2_wu=ffn2_wu, ffn2_wd=ffn2_wd,
             w_in=w_in, gdn_conv_w=gdn_conv_w, gdn_a_log=gdn_a_log, gdn_dt_bias=gdn_dt_bias, gdn_norm_w=gdn_norm_w,
             gla_w_gate=gla_w_gate, gla_b_gate=gla_b_gate, gla_norm_w=gla_norm_w,
             ssd_conv_w=ssd_conv_w, ssd_conv_b=ssd_conv_b, ssd_a_log=ssd_a_log, ssd_dt_bias=ssd_dt_bias,
             ssd_d=ssd_d, ssd_norm_w=ssd_norm_w,
             w_branch_gdn=w_branch_gdn, w_branch_gla=w_branch_gla, w_branch_ssd=w_branch_ssd, w_out=w_out)
    nl = w_ada.shape[0]
    bp, tp, dm = x_prompt.shape
    bs = x_sample.shape[0]
    assert x_sample.shape[1] == 1 and tp % CHUNK == 0 and bs % STEP_B == 0 and dm == D_MODEL
    lps = [_layer_params(l, p) for l in range(nl)]
    sw = _stacked_weights(p)
    fw = final_norm.reshape(1, dm)

    rows = bp + bs
    rpad = -(-rows // 8) * 8
    c_all = jnp.concatenate([c_prompt, c_sample, jnp.zeros((rpad - rows, dm), f32)], axis=0)
    mod = _ada_mod(c_all, w_ada, b_ada)
    mod_p = [mod[l, :bp].reshape(bp, 1, N_MOD * dm) for l in range(nl)]
    mod_s = [mod[l, bp:rows].reshape(1, bs, N_MOD * dm) for l in range(nl)]

    sample_states = (state_gdn_conv, state_gdn, state_gla, state_ssd_conv, state_ssd)
    prompt_states = tuple(jnp.zeros((s.shape[0], bp) + s.shape[2:], x_prompt.dtype) for s in sample_states)
    tm_p = 512 if tp % 512 == 0 else CHUNK
    tm_in = 1024 if tp % 1024 == 0 else tm_p
    y_p, st_p = _trunk(x_prompt, mod_p, lps, sw, prompt_states, False, tm_p, tm_in, fw)
    y_s, st_s = _trunk(x_sample.reshape(1, bs, dm), mod_s, lps, sw, sample_states, True, bs, bs, fw)
    return (y_p, y_s.reshape(bs, 1, dm)) + st_p + st_s
```

```python
import functools

import jax
import jax.numpy as jnp
from jax import lax
from jax.experimental import pallas as pl
from jax.experimental.pallas import tpu as pltpu

f32 = jnp.float32
bf16 = jnp.bfloat16
HI = lax.Precision.HIGHEST

EPS = 1e-6
D_MODEL = 2048
N_MOD = 9
CHUNK = 64
CONV_W = 4
GDN_H, GDN_DK, GDN_DV = 8, 128, 128
GLA_H, GLA_DK, GLA_DV, GLA_RANK, GLA_TAU = 4, 128, 256, 16, 16.0
SSD_H, SSD_P, SSD_G, SSD_N = 32, 64, 4, 128
SSD_HG = SSD_H // SSD_G
GDN_QK = GDN_H * GDN_DK
GDN_V = GDN_H * GDN_DV
GDN_CONV = 2 * GDN_QK + GDN_V
GLA_QK = GLA_H * GLA_DK
GLA_V = GLA_H * GLA_DV
SSD_INNER = SSD_H * SSD_P
SSD_BC = SSD_G * SSD_N
SSD_CONV = SSD_INNER + 2 * SSD_BC
SSD_GW = SSD_HG * SSD_P

LANE = 128
FF_TILE = 512

IN_SPLITS = (("qkv_a", GDN_CONV), ("z_a", GDN_V), ("beta", GDN_H), ("dec", GDN_H),
             ("q_b", GLA_QK), ("k_b", GLA_QK), ("v_b", GLA_V), ("lr", GLA_RANK), ("r_b", GLA_V),
             ("z_c", SSD_INNER), ("xbc", SSD_CONV), ("dt", SSD_H), ("gates", 3 * D_MODEL))
MAIN_ORDER = ("qkv_a", "xbc", "z_c", "gates", "z_a", "q_b", "k_b", "v_b", "r_b")
SMALL_ORDER = ("beta", "dec", "lr", "dt")


def _layout():
    src, off = {}, 0
    for name, w in IN_SPLITS:
        src[name] = (off, off + w)
        off += w
    main, small, d = {}, {}, 0
    for name in MAIN_ORDER:
        w = src[name][1] - src[name][0]
        main[name] = d
        d += w
    total = d
    d = 0
    for name in SMALL_ORDER:
        small[name] = d
        d += src[name][1] - src[name][0]
    assert d <= LANE
    return src, main, small, total


W_IN_SRC, P_MAIN, P_SM, P_TOTAL = _layout()
P_QKV_A, P_XBC, P_Z_C, P_GATES, P_Z_A = (P_MAIN[k] for k in ("qkv_a", "xbc", "z_c", "gates", "z_a"))
P_Q_B, P_K_B, P_V_B, P_R_B = (P_MAIN[k] for k in ("q_b", "k_b", "v_b", "r_b"))
SM_BETA, SM_DEC, SM_LR, SM_DT = (P_SM[k] for k in SMALL_ORDER)

VMEM_LIMIT = 56 * 1024 * 1024


def _cp(sem):
    return pltpu.CompilerParams(dimension_semantics=sem, vmem_limit_bytes=VMEM_LIMIT)


def _blk(off, width):
    assert off % width == 0, (off, width)
    return off // width


def _sigmoid(x):
    return jax.nn.sigmoid(x)


def _silu(x):
    return x * jax.nn.sigmoid(x)


def _rms(x, w):
    return x * lax.rsqrt(jnp.mean(x * x, axis=-1, keepdims=True) + EPS) * w


def _dot(a, b):
    return jnp.dot(a, b, preferred_element_type=f32)


def _dot_nt(a, b):
    return lax.dot_general(a, b, (((1,), (1,)), ((), ())), preferred_element_type=f32)


def _dot_tn(a, b):
    return lax.dot_general(a, b, (((0,), (0,)), ((), ())), preferred_element_type=f32)


def _split(x):
    hi = x.astype(bf16)
    return hi, (x - hi.astype(f32)).astype(bf16)


def _dot3(a, b):
    return _dot(a[0], b[0]) + (_dot(a[0], b[1]) + _dot(a[1], b[0]))


def _dot_hi(a, b):
    return jnp.dot(a, b, precision=HI, preferred_element_type=f32)


def _tri_masks(c):
    row = lax.broadcasted_iota(jnp.int32, (c, c), 0)
    col = lax.broadcasted_iota(jnp.int32, (c, c), 1)
    return row >= col, row > col, row == col


def _lane_col(x, idx):
    lane = lax.broadcasted_iota(jnp.int32, x.shape, 1)
    return jnp.sum(jnp.where(lane == idx, x, 0.0), axis=1, keepdims=True)


def _scalar_vec(s):
    return jnp.full((1, 1), s, f32)


def _ada_kernel(c_ref, w_ref, b_ref, o_ref):
    s = _silu(c_ref[...]).astype(bf16)
    o_ref[...] = _dot(s, w_ref[...].astype(bf16)) + b_ref[...]


def _ada_mod(c_all, w_ada, b_ada):
    nl, dm, n = w_ada.shape
    r = c_all.shape[0]
    tn = 1024
    return pl.pallas_call(
        _ada_kernel,
        grid=(nl, n // tn),
        in_specs=[pl.BlockSpec((r, dm), lambda l, j: (0, 0)),
                  pl.BlockSpec((None, dm, tn), lambda l, j: (l, 0, j)),
                  pl.BlockSpec((None, 1, tn), lambda l, j: (l, 0, j))],
        out_specs=pl.BlockSpec((None, r, tn), lambda l, j: (l, 0, j)),
        out_shape=jax.ShapeDtypeStruct((nl, r, n), f32),
        compiler_params=_cp(("arbitrary", "arbitrary")),
        name="ada_mod",
    )(c_all, w_ada, b_ada.reshape(nl, 1, n))


def _mod_spec(per_row, tm, chunk):
    if per_row:
        return pl.BlockSpec((None, tm, D_MODEL), lambda b, i, j: (b, i, chunk))
    return pl.BlockSpec((None, 1, D_MODEL), lambda b, i, j: (b, 0, chunk))


def _ffn_kernel(x_ref, sh_ref, sc_ref, gt_ref, nw_ref, fw_ref, wg_ref, wu_ref, wd_ref, *rest, final, has_tail):
    if has_tail:
        wgt_ref, wut_ref, wdt_ref, o_ref, h_ref, acc_ref = rest
    else:
        o_ref, h_ref, acc_ref = rest
    f = pl.program_id(2)
    last = pl.num_programs(2) - 1

    @pl.when(f == 0)
    def _():
        y = _rms(x_ref[...], nw_ref[...])
        h_ref[...] = (y * (1.0 + sc_ref[...]) + sh_ref[...]).astype(bf16)
        acc_ref[...] = jnp.zeros_like(acc_ref)

    def accumulate(wg, wu, wd):
        h = h_ref[...]
        a = (_silu(_dot(h, wg[...])) * _dot(h, wu[...])).astype(bf16)
        acc_ref[...] += _dot(a, wd[...])

    if has_tail:
        pl.when(f < last)(lambda: accumulate(wg_ref, wu_ref, wd_ref))
        pl.when(f == last)(lambda: accumulate(wgt_ref, wut_ref, wdt_ref))
    else:
        accumulate(wg_ref, wu_ref, wd_ref)

    @pl.when(f == last)
    def _():
        y = x_ref[...] + 0.5 * gt_ref[...] * acc_ref[...]
        if final:
            y = _rms(y, fw_ref[...])
        o_ref[...] = y


def _ffn(x, mod, k0, nw, w, l, fw, per_row, tm, final):
    bx, tx, dm = x.shape
    tf = FF_TILE
    n_full = w[0].shape[2] // tf
    has_tail = len(w) > 3
    full = lambda f: jnp.minimum(f, n_full - 1)
    w_specs = [pl.BlockSpec((None, dm, tf), lambda b, i, f: (l, 0, full(f))),
               pl.BlockSpec((None, dm, tf), lambda b, i, f: (l, 0, full(f))),
               pl.BlockSpec((None, tf, dm), lambda b, i, f: (l, full(f), 0))]
    if has_tail:
        ft = w[3].shape[2]
        w_specs += [pl.BlockSpec((None, dm, ft), lambda b, i, f: (l, 0, 0)),
                    pl.BlockSpec((None, dm, ft), lambda b, i, f: (l, 0, 0)),
                    pl.BlockSpec((None, ft, dm), lambda b, i, f: (l, 0, 0))]
    return pl.pallas_call(
        functools.partial(_ffn_kernel, final=final, has_tail=has_tail),
        grid=(bx, tx // tm, n_full + int(has_tail)),
        in_specs=[pl.BlockSpec((None, tm, dm), lambda b, i, f: (b, i, 0)),
                  _mod_spec(per_row, tm, k0), _mod_spec(per_row, tm, k0 + 1), _mod_spec(per_row, tm, k0 + 2),
                  pl.BlockSpec((1, dm), lambda b, i, f: (0, 0)),
                  pl.BlockSpec((1, dm), lambda b, i, f: (0, 0))] + w_specs,
        out_specs=pl.BlockSpec((None, tm, dm), lambda b, i, f: (b, i, 0)),
        out_shape=jax.ShapeDtypeStruct(x.shape, f32),
        scratch_shapes=[pltpu.VMEM((tm, dm), bf16), pltpu.VMEM((tm, dm), f32)],
        compiler_params=_cp(("parallel", "parallel", "arbitrary")),
        name="ffn",
    )(x, mod, mod, mod, nw, fw, *w)


def _inproj_kernel(x_ref, sh_ref, sc_ref, nw_ref, w_ref, ws_ref, o_ref, os_ref, h_ref):
    @pl.when(pl.program_id(2) == 0)
    def _():
        y = _rms(x_ref[...], nw_ref[...])
        h = (y * (1.0 + sc_ref[...]) + sh_ref[...]).astype(bf16)
        h_ref[...] = h
        os_ref[...] = _dot_nt(h, ws_ref[...])

    o_ref[...] = _dot_nt(h_ref[...], w_ref[...]).astype(o_ref.dtype)


def _inproj(x, mod, nw, w, ws, l, per_row, tm, out_dtype):
    bx, tx, dm = x.shape
    n = w.shape[1]
    tn = 1024
    return pl.pallas_call(
        _inproj_kernel,
        grid=(bx, tx // tm, n // tn),
        in_specs=[pl.BlockSpec((None, tm, dm), lambda b, i, j: (b, i, 0)),
                  _mod_spec(per_row, tm, 3), _mod_spec(per_row, tm, 4),
                  pl.BlockSpec((1, dm), lambda b, i, j: (0, 0)),
                  pl.BlockSpec((None, tn, dm), lambda b, i, j: (l, j, 0)),
                  pl.BlockSpec((None, LANE, dm), lambda b, i, j: (l, 0, 0))],
        out_specs=[pl.BlockSpec((None, tm, tn), lambda b, i, j: (b, i, j)),
                   pl.BlockSpec((None, tm, LANE), lambda b, i, j: (b, i, 0))],
        out_shape=[jax.ShapeDtypeStruct((bx, tx, n), out_dtype),
                   jax.ShapeDtypeStruct((bx, tx, LANE), f32)],
        scratch_shapes=[pltpu.VMEM((tm, dm), bf16)],
        compiler_params=_cp(("parallel", "parallel", "arbitrary")),
        name="inproj",
    )(x, mod, mod, nw, w, ws)


def _conv_load(ext_ref, x_ref, cst_ref, first):
    c = x_ref.shape[0]

    @pl.when(first)
    def _():
        ext_ref[pl.ds(5, CONV_W - 1), :] = cst_ref[...]

    ext_ref[pl.ds(8, c), :] = x_ref[...].astype(f32)


def _conv_cols(ext_ref, w_ref, b_ref, lo, width, c):
    sl = pl.ds(lo, width)
    w = w_ref[:, sl]
    y = ext_ref[pl.ds(5, c), sl] * w[0:1]
    for j in range(1, CONV_W):
        y = y + ext_ref[pl.ds(5 + j, c), sl] * w[j:j + 1]
    if b_ref is not None:
        y = y + b_ref[:, sl]
    return _silu(y)


def _conv_carry(ext_ref, c):
    ext_ref[pl.ds(5, CONV_W - 1), :] = ext_ref[pl.ds(8 + c - (CONV_W - 1), CONV_W - 1), :]


def _cumsum_rows(x, tri):
    return _dot_hi(jnp.where(tri, 1.0, 0.0).astype(f32), x)


def _gdn_kernel(qkv_ref, z_ref, sm_ref, cst_ref, cw_ref, alog_ref, dtb_ref, nw_ref, s0_ref,
                o_ref, s_ref, ext_ref):
    c = qkv_ref.shape[0]
    first = pl.program_id(1) == 0

    @pl.when(first)
    def _():
        s_ref[...] = s0_ref[...]

    _conv_load(ext_ref, qkv_ref, cst_ref, first)

    sm = sm_ref[...]
    beta_l = _sigmoid(sm)
    g_l = -jnp.exp(alog_ref[...]) * jax.nn.softplus(sm + dtb_ref[...])
    tri, strict, eye = _tri_masks(c)
    gc_all = _cumsum_rows(g_l, tri)
    gc_t = gc_all.T
    eye_f = jnp.where(eye, 1.0, 0.0).astype(f32)
    n_fac = max(1, (c - 1).bit_length() - 1)

    hs = range(GDN_H)
    qb, kb, ks, decay, eg, gcs, nm, rhs = [], [], [], [], [], [], [], []
    for h in hs:
        q = _conv_cols(ext_ref, cw_ref, None, h * GDN_DK, GDN_DK, c)
        k = _conv_cols(ext_ref, cw_ref, None, GDN_QK + h * GDN_DK, GDN_DK, c)
        v = _conv_cols(ext_ref, cw_ref, None, 2 * GDN_QK + h * GDN_DV, GDN_DV, c)
        q = q * lax.rsqrt(jnp.sum(q * q, axis=-1, keepdims=True) + EPS) * (GDN_DK ** -0.5)
        k = k * lax.rsqrt(jnp.sum(k * k, axis=-1, keepdims=True) + EPS)
        beta = beta_l[:, SM_BETA + h:SM_BETA + h + 1]
        gc = gc_all[:, SM_DEC + h:SM_DEC + h + 1]
        d = jnp.exp(jnp.where(tri, gc - gc_t[SM_DEC + h:SM_DEC + h + 1, :], -jnp.inf))
        e = jnp.exp(gc)
        qb.append(q.astype(bf16))
        kb.append(k.astype(bf16))
        ks.append(k)
        decay.append(d)
        eg.append(e)
        gcs.append(gc)
        nm.append(jnp.where(strict, beta * d * _dot_nt(kb[h], kb[h]), 0.0))
        rhs.append(jnp.concatenate([v * beta, k * (beta * e)], axis=1))

    inv = [eye_f - nm[h] for h in hs]
    pw_s = [_split(nm[h]) for h in hs]
    pw_s = [_split(_dot3(pw_s[h], pw_s[h])) for h in hs]
    for i in range(n_fac):
        inv = [inv[h] + _dot3(_split(inv[h]), pw_s[h]) for h in hs]
        if i + 1 < n_fac:
            pw_s = [_split(_dot3(pw_s[h], pw_s[h])) for h in hs]
    sol = [_dot3(_split(inv[h]), _split(rhs[h])) for h in hs]

    s = [s_ref[h] for h in hs]
    sb = [s[h].astype(bf16) for h in hs]
    wb = [(sol[h][:, 0:GDN_DV] - _dot(sol[h][:, GDN_DV:].astype(bf16), sb[h])).astype(bf16) for h in hs]
    qk = [(_dot_nt(qb[h], kb[h]) * decay[h]).astype(bf16) for h in hs]
    o = [_dot(qb[h], sb[h]) * eg[h] + _dot(qk[h], wb[h]) for h in hs]
    for h in hs:
        g_last = gcs[h][c - 1:c]
        kd = (ks[h] * jnp.exp(g_last - gcs[h])).astype(bf16)
        s_ref[h] = s[h] * jnp.exp(g_last) + _dot_tn(kd, wb[h])
    for h in hs:
        sl = slice(h * GDN_DV, (h + 1) * GDN_DV)
        o_ref[:, sl] = (_rms(o[h], nw_ref[...]) * _silu(z_ref[:, sl].astype(f32))).astype(bf16)

    _conv_carry(ext_ref, c)


def _gdn_prompt(proj, small, conv_state, s0, conv_w, alog_l, dtb_l, norm_w):
    b, t, _ = proj.shape
    c = CHUNK
    const = lambda shape: pl.BlockSpec(shape, lambda bi, ti: (0,) * len(shape))
    st = pl.BlockSpec((None, GDN_H, GDN_DK, GDN_DV), lambda bi, ti: (bi, 0, 0, 0))
    return pl.pallas_call(
        _gdn_kernel,
        grid=(b, t // c),
        in_specs=[pl.BlockSpec((None, c, GDN_CONV), lambda bi, ti: (bi, ti, _blk(P_QKV_A, GDN_CONV))),
                  pl.BlockSpec((None, c, GDN_V), lambda bi, ti: (bi, ti, _blk(P_Z_A, GDN_V))),
                  pl.BlockSpec((None, c, LANE), lambda bi, ti: (bi, ti, 0)),
                  pl.BlockSpec((None, CONV_W - 1, GDN_CONV), lambda bi, ti: (bi, 0, 0)),
                  const((CONV_W, GDN_CONV)), const((1, LANE)), const((1, LANE)), const((1, GDN_DV)),
                  st],
        out_specs=[pl.BlockSpec((None, c, GDN_V), lambda bi, ti: (bi, ti, 0)), st],
        out_shape=[jax.ShapeDtypeStruct((b, t, GDN_V), bf16),
                   jax.ShapeDtypeStruct((b, GDN_H, GDN_DK, GDN_DV), f32)],
        scratch_shapes=[pltpu.VMEM((8 + c, GDN_CONV), f32)],
        compiler_params=_cp(("parallel", "arbitrary")),
        name="gdn_prompt",
    )(proj, proj, small, conv_state, conv_w, alog_l, dtb_l, norm_w, s0)


GLA_SUB = 16


def _gla_kernel(q_ref, k_ref, v_ref, r_ref, sm_ref, wgate_ref, bgate_ref, nw_ref, s0_ref,
                o_ref, so_ref, st_ref):
    c = q_ref.shape[0]
    ti = pl.program_id(1)

    @pl.when(ti == 0)
    def _():
        for h in range(GLA_H):
            st_ref[h] = s0_ref[h].T

    sm = sm_ref[...]
    lane = lax.broadcasted_iota(jnp.int32, sm.shape, 1)
    lr = jnp.where((lane >= SM_LR) & (lane < SM_LR + GLA_RANK), sm, 0.0).astype(bf16)
    la = jax.nn.log_sigmoid(_dot(lr, wgate_ref[...]) + bgate_ref[...]) / GLA_TAU
    tri, _, _ = _tri_masks(c)
    b_all = _cumsum_rows(la, tri)

    col = lax.broadcasted_iota(jnp.int32, (GLA_SUB, c), 1)
    row = lax.broadcasted_iota(jnp.int32, (GLA_SUB, c), 0)
    hs = range(GLA_H)
    qs, ks, vbs, bs, atts = [], [], [], [], []
    for h in hs:
        sk = slice(h * GLA_DK, (h + 1) * GLA_DK)
        q = q_ref[:, sk].astype(f32) * (GLA_DK ** -0.5)
        k = k_ref[:, sk].astype(f32)
        b = b_all[:, sk]
        qs.append(q)
        ks.append(k)
        bs.append(b)
        vbs.append(v_ref[:, h * GLA_DV:(h + 1) * GLA_DV].astype(bf16))
        att_rows = []
        for i in range(c // GLA_SUB):
            lo = i * GLA_SUB
            b_i = b[lo:lo + GLA_SUB]
            q_i = q[lo:lo + GLA_SUB]
            b_top = b[lo:lo + 1]
            att = jnp.zeros((GLA_SUB, c), f32)
            if i > 0:
                q_t = (q_i * jnp.exp(b_i - b_top)).astype(bf16)
                k_t = (k * jnp.exp(jnp.minimum(b_top - b, 0.0))).astype(bf16)
                att = jnp.where(col < lo, _dot_nt(q_t, k_t), 0.0)
            for sl in range(GLA_SUB):
                sidx = lo + sl
                e = jnp.exp(jnp.minimum(b_i - b[sidx:sidx + 1], 0.0))
                p = jnp.sum(q_i * k[sidx:sidx + 1] * e, axis=1, keepdims=True)
                att = jnp.where((col == sidx) & (row >= sl), p, att)
            att_rows.append(att)
        atts.append(jnp.concatenate(att_rows, axis=0).astype(bf16))

    sts = [st_ref[h] for h in hs]
    os_ = [_dot(atts[h], vbs[h]) + _dot_nt((qs[h] * jnp.exp(bs[h])).astype(bf16), sts[h].astype(bf16))
           for h in hs]
    st_new = []
    for h in hs:
        b_last = bs[h][c - 1:c]
        kd = (ks[h] * jnp.exp(b_last - bs[h])).astype(bf16)
        st_new.append(sts[h] * jnp.exp(b_last) + _dot_tn(vbs[h], kd))
        st_ref[h] = st_new[h]

    @pl.when(ti == pl.num_programs(1) - 1)
    def _():
        for h in hs:
            so_ref[h] = st_new[h].T

    for h in hs:
        sv = slice(h * GLA_DV, (h + 1) * GLA_DV)
        o_ref[:, sv] = (_rms(os_[h], nw_ref[...]) * _silu(r_ref[:, sv].astype(f32))).astype(bf16)


def _gla_prompt(proj, small, s0, wgate_pad, bgate, norm_w):
    b, t, _ = proj.shape
    c = CHUNK
    const = lambda shape: pl.BlockSpec(shape, lambda bi, ti: (0,) * len(shape))
    st = pl.BlockSpec((None, GLA_H, GLA_DK, GLA_DV), lambda bi, ti: (bi, 0, 0, 0))
    return pl.pallas_call(
        _gla_kernel,
        grid=(b, t // c),
        in_specs=[pl.BlockSpec((None, c, GLA_QK), lambda bi, ti: (bi, ti, _blk(P_Q_B, GLA_QK))),
                  pl.BlockSpec((None, c, GLA_QK), lambda bi, ti: (bi, ti, _blk(P_K_B, GLA_QK))),
                  pl.BlockSpec((None, c, GLA_V), lambda bi, ti: (bi, ti, _blk(P_V_B, GLA_V))),
                  pl.BlockSpec((None, c, GLA_V), lambda bi, ti: (bi, ti, _blk(P_R_B, GLA_V))),
                  pl.BlockSpec((None, c, LANE), lambda bi, ti: (bi, ti, 0)),
                  const((LANE, GLA_QK)), const((1, GLA_QK)), const((1, GLA_DV)),
                  st],
        out_specs=[pl.BlockSpec((None, c, GLA_V), lambda bi, ti: (bi, ti, 0)), st],
        out_shape=[jax.ShapeDtypeStruct((b, t, GLA_V), bf16),
                   jax.ShapeDtypeStruct((b, GLA_H, GLA_DK, GLA_DV), f32)],
        scratch_shapes=[pltpu.VMEM((GLA_H, GLA_DV, GLA_DK), f32)],
        compiler_params=_cp(("parallel", "arbitrary")),
        name="gla_prompt",
    )(proj, proj, proj, proj, small, wgate_pad, bgate, norm_w, s0)


def _ssd_kernel(z_ref, xbc_ref, sm_ref, cst_ref, cw_ref, cb_ref, avec_ref, dtb_ref, dvec_ref, nw_ref, h0_ref,
                o_ref, h_ref, ext_ref):
    c = xbc_ref.shape[0]
    first = pl.program_id(1) == 0

    @pl.when(first)
    def _():
        h_ref[...] = h0_ref[...]

    _conv_load(ext_ref, xbc_ref, cst_ref, first)

    dt_l = jax.nn.softplus(sm_ref[...] + dtb_ref[...])
    tri, _, _ = _tri_masks(c)
    ac_all = _cumsum_rows(dt_l * avec_ref[...], tri)
    ac_t2 = jnp.concatenate([ac_all, ac_all], axis=0).T

    lane = lax.broadcasted_iota(jnp.int32, (c, 2 * SSD_P), 1)
    rowi = lax.broadcasted_iota(jnp.int32, (c, 2 * SSD_P), 0)
    lo_half = lane < SSD_P
    tri2 = rowi >= jnp.where(lo_half, lane, lane - SSD_P)
    lane1 = lane[0:1]
    gs_ = range(SSD_G)
    ps_ = range(SSD_HG // 2)

    xs_, bmb, hg, cb2, ch = [], [], [], [], []
    for g in gs_:
        xs_.append(_conv_cols(ext_ref, cw_ref, cb_ref, g * SSD_GW, SSD_GW, c))
        bmb.append(_conv_cols(ext_ref, cw_ref, cb_ref, SSD_INNER + g * SSD_N, SSD_N, c).astype(bf16))
        cmb = _conv_cols(ext_ref, cw_ref, cb_ref, SSD_INNER + SSD_BC + g * SSD_N, SSD_N, c).astype(bf16)
        hg.append(h_ref[pl.ds(g * SSD_HG, SSD_HG)].reshape(SSD_GW, SSD_N))
        cb2.append(_dot_nt(cmb, jnp.concatenate([bmb[g], bmb[g]], axis=0)))
        ch.append(_dot_nt(cmb, hg[g].astype(bf16)))

    m2, rhs, e_col, xsc = {}, {}, {}, {}
    for g in gs_:
        for p in ps_:
            l0 = SM_DT + g * SSD_HG + 2 * p
            ps = slice(p * 2 * SSD_P, (p + 1) * 2 * SSD_P)
            ac_col = jnp.where(lo_half, ac_all[:, l0:l0 + 1], ac_all[:, l0 + 1:l0 + 2])
            ac_row = jnp.where(lane1 < SSD_P, ac_t2[l0:l0 + 1, :], ac_t2[l0 + 1:l0 + 2, :])
            decay = jnp.exp(jnp.where(tri2, ac_col - ac_row, -jnp.inf))
            m2[g, p] = (cb2[g] * decay).astype(bf16)
            dt2 = jnp.where(lo_half, dt_l[:, l0:l0 + 1], dt_l[:, l0 + 1:l0 + 2])
            xdt = xs_[g][:, ps] * dt2
            rhs[g, p] = jnp.concatenate([jnp.where(lo_half, xdt, 0.0), jnp.where(lo_half, 0.0, xdt)],
                                        axis=0).astype(bf16)
            e_col[g, p] = jnp.exp(ac_col)
            xsc[g, p] = xdt * jnp.exp(ac_col[c - 1:c] - ac_col)
    y2 = {gp: _dot(m2[gp], rhs[gp]) for gp in m2}

    for g in gs_:
        xsg = jnp.concatenate([xsc[g, p] for p in ps_], axis=1).astype(bf16)
        dh = _dot_tn(xsg, bmb[g])
        for j in range(SSD_HG):
            ln = SM_DT + g * SSD_HG + j
            h_ref[g * SSD_HG + j] = (hg[g][j * SSD_P:(j + 1) * SSD_P] * jnp.exp(ac_all[c - 1:c, ln:ln + 1])
                                     + dh[j * SSD_P:(j + 1) * SSD_P])
    for g in gs_:
        gs = slice(g * SSD_GW, (g + 1) * SSD_GW)
        y = jnp.concatenate([y2[g, p] + ch[g][:, p * 2 * SSD_P:(p + 1) * 2 * SSD_P] * e_col[g, p] for p in ps_],
                            axis=1)
        y = (y + dvec_ref[:, gs] * xs_[g]) * _silu(z_ref[:, gs].astype(f32))
        o_ref[:, gs] = _rms(y, nw_ref[:, gs]).astype(bf16)

    _conv_carry(ext_ref, c)


def _ssd_prompt(proj, small, conv_state, h0, conv_w, conv_b, avec_l, dtb_l, dvec, norm_w):
    b, t, _ = proj.shape
    c = CHUNK
    const = lambda shape: pl.BlockSpec(shape, lambda bi, ti: (0,) * len(shape))
    st = pl.BlockSpec((None, SSD_H, SSD_P, SSD_N), lambda bi, ti: (bi, 0, 0, 0))
    return pl.pallas_call(
        _ssd_kernel,
        grid=(b, t // c),
        in_specs=[pl.BlockSpec((None, c, SSD_INNER), lambda bi, ti: (bi, ti, _blk(P_Z_C, SSD_INNER))),
                  pl.BlockSpec((None, c, SSD_CONV), lambda bi, ti: (bi, ti, _blk(P_XBC, SSD_CONV))),
                  pl.BlockSpec((None, c, LANE), lambda bi, ti: (bi, ti, 0)),
                  pl.BlockSpec((None, CONV_W - 1, SSD_CONV), lambda bi, ti: (bi, 0, 0)),
                  const((CONV_W, SSD_CONV)), const((1, SSD_CONV)), const((1, LANE)), const((1, LANE)),
                  const((1, SSD_INNER)), const((1, SSD_INNER)),
                  st],
        out_specs=[pl.BlockSpec((None, c, SSD_INNER), lambda bi, ti: (bi, ti, 0)), st],
        out_shape=[jax.ShapeDtypeStruct((b, t, SSD_INNER), bf16),
                   jax.ShapeDtypeStruct((b, SSD_H, SSD_P, SSD_N), f32)],
        scratch_shapes=[pltpu.VMEM((8 + c, SSD_CONV), f32)],
        compiler_params=_cp(("parallel", "arbitrary")),
        name="ssd_prompt",
    )(proj, proj, small, conv_state, conv_w, conv_b, avec_l, dtb_l, dvec, norm_w, h0)


STEP_B = 8


def _conv_step(x_ref, c_ref, w_ref):
    w = w_ref[...]
    return c_ref[0] * w[0:1] + c_ref[1] * w[1:2] + c_ref[2] * w[2:3] + x_ref[...] * w[3:4]


def _state_call(kernel, grid, in_specs, out_specs, out_shape, scratch, sem, name, args, prev_state):
    aliases = {}
    if prev_state is not None:
        in_specs = in_specs + [pl.BlockSpec(memory_space=pl.ANY)]
        args = args + (prev_state,)
        aliases = {len(args) - 1: len(out_shape) - 1}
        kernel = functools.partial(_drop_alias_ref, kernel, len(args) - 1)
    return pl.pallas_call(kernel, grid=grid, in_specs=in_specs, out_specs=out_specs, out_shape=out_shape,
                          scratch_shapes=scratch, input_output_aliases=aliases,
                          compiler_params=_cp(sem), name=name)(*args)


def _drop_alias_ref(kernel, idx, *refs):
    return kernel(*refs[:idx], *refs[idx + 1:])


def _gdn_step_kernel(alog_ref, dtb_ref, q_ref, k_ref, v_ref, z_ref, sm_ref, cq_ref, ck_ref, cv_ref,
                     wq_ref, wk_ref, wv_ref, nw_ref, s_ref, o_ref, so_ref):
    nb = q_ref.shape[0]
    qa = _silu(_conv_step(q_ref, cq_ref, wq_ref))
    ka = _silu(_conv_step(k_ref, ck_ref, wk_ref))
    va = _silu(_conv_step(v_ref, cv_ref, wv_ref))
    sm = sm_ref[...]
    z = z_ref[...]
    for h in range(GDN_H):
        sl = slice(h * GDN_DK, (h + 1) * GDN_DK)
        q = qa[:, sl]
        k = ka[:, sl]
        v = va[:, sl]
        q = q * lax.rsqrt(jnp.sum(q * q, axis=-1, keepdims=True) + EPS) * (GDN_DK ** -0.5)
        k = k * lax.rsqrt(jnp.sum(k * k, axis=-1, keepdims=True) + EPS)
        beta = _sigmoid(sm[:, SM_BETA + h:SM_BETA + h + 1])
        g = -jnp.exp(_scalar_vec(alog_ref[h])) * jax.nn.softplus(sm[:, SM_DEC + h:SM_DEC + h + 1] + dtb_ref[h])
        eg = jnp.exp(g)
        qk = jnp.sum(q * k, axis=-1, keepdims=True)
        q_t = q.T
        k_t = k.T
        rows = []
        for b in range(nb):
            s = s_ref[b, h]
            kcol = k_t[:, b:b + 1]
            ks = jnp.sum(s * kcol, axis=0, keepdims=True)
            qs = jnp.sum(s * q_t[:, b:b + 1], axis=0, keepdims=True)
            bt = beta[b:b + 1]
            e = eg[b:b + 1]
            w = bt * v[b:b + 1] - (bt * e) * ks
            rows.append(qs * e + qk[b:b + 1] * w)
            so_ref[b, h] = s * e + kcol * w
        o = jnp.concatenate(rows, axis=0)
        o = _rms(o, nw_ref[...]) * _silu(z[:, sl])
        o_ref[:, sl] = o.astype(bf16)


def _gdn_step(proj, small, conv_t, states, prev, l, conv_w, a_log, dt_bias, norm_w):
    n = proj.shape[0]
    nb = STEP_B
    blk = lambda off: pl.BlockSpec((nb, GDN_QK), lambda i: (i, _blk(off, GDN_QK)))
    cst = lambda j: pl.BlockSpec((CONV_W - 1, nb, GDN_QK), lambda i: (0, i, j))
    cw = lambda j: pl.BlockSpec((CONV_W, GDN_QK), lambda i: (0, j))
    smem = pl.BlockSpec(memory_space=pltpu.SMEM)
    st = pl.BlockSpec((None, nb, GDN_H, GDN_DK, GDN_DV), lambda i: (l, i, 0, 0, 0))
    return _state_call(
        _gdn_step_kernel, (n // nb,),
        [smem, smem, blk(P_QKV_A), blk(P_QKV_A + GDN_QK), blk(P_QKV_A + 2 * GDN_QK), blk(P_Z_A),
         pl.BlockSpec((nb, LANE), lambda i: (i, 0)),
         cst(0), cst(1), cst(2), cw(0), cw(1), cw(2),
         pl.BlockSpec((1, GDN_DV), lambda i: (0, 0)), st],
        [pl.BlockSpec((nb, GDN_V), lambda i: (i, 0)), st],
        [jax.ShapeDtypeStruct((n, GDN_V), bf16), jax.ShapeDtypeStruct(states.shape, f32)],
        [], ("parallel",), "gdn_step",
        (a_log, dt_bias, proj, proj, proj, proj, small, conv_t, conv_t, conv_t, conv_w, conv_w, conv_w, norm_w, states),
        prev)


def _gla_step_kernel(q_ref, k_ref, v_ref, r_ref, sm_ref, wgate_ref, bgate_ref, nw_ref, s_ref,
                     o_ref, so_ref):
    nb = q_ref.shape[0]
    sm = sm_ref[...]
    lane = lax.broadcasted_iota(jnp.int32, sm.shape, 1)
    lr = jnp.where((lane >= SM_LR) & (lane < SM_LR + GLA_RANK), sm, 0.0).astype(bf16)
    la_all = jax.nn.log_sigmoid(_dot(lr, wgate_ref[...]) + bgate_ref[...]) / GLA_TAU
    qa = q_ref[...] * (GLA_DK ** -0.5)
    ka = k_ref[...]
    va = v_ref[...]
    r = r_ref[...]
    for h in range(GLA_H):
        sk = slice(h * GLA_DK, (h + 1) * GLA_DK)
        sv = slice(h * GLA_DV, (h + 1) * GLA_DV)
        q = qa[:, sk]
        k = ka[:, sk]
        v = va[:, sv]
        e = jnp.exp(la_all[:, sk])
        qk = jnp.sum(q * k, axis=-1, keepdims=True)
        e_t = e.T
        k_t = k.T
        qe_t = (q * e).T
        rows = []
        for b in range(nb):
            s = s_ref[b, h]
            vrow = v[b:b + 1]
            rows.append(qk[b:b + 1] * vrow + jnp.sum(s * qe_t[:, b:b + 1], axis=0, keepdims=True))
            so_ref[b, h] = s * e_t[:, b:b + 1] + k_t[:, b:b + 1] * vrow
        o = jnp.concatenate(rows, axis=0)
        o = _rms(o, nw_ref[...]) * _silu(r[:, sv])
        o_ref[:, sv] = o.astype(bf16)


def _gla_step(proj, small, states, prev, l, wgate_pad, bgate, norm_w):
    n = proj.shape[0]
    nb = STEP_B
    st = pl.BlockSpec((None, nb, GLA_H, GLA_DK, GLA_DV), lambda i: (l, i, 0, 0, 0))
    return _state_call(
        _gla_step_kernel, (n // nb,),
        [pl.BlockSpec((nb, GLA_QK), lambda i: (i, _blk(P_Q_B, GLA_QK))),
         pl.BlockSpec((nb, GLA_QK), lambda i: (i, _blk(P_K_B, GLA_QK))),
         pl.BlockSpec((nb, GLA_V), lambda i: (i, _blk(P_V_B, GLA_V))),
         pl.BlockSpec((nb, GLA_V), lambda i: (i, _blk(P_R_B, GLA_V))),
         pl.BlockSpec((nb, LANE), lambda i: (i, 0)),
         pl.BlockSpec((LANE, GLA_QK), lambda i: (0, 0)),
         pl.BlockSpec((1, GLA_QK), lambda i: (0, 0)),
         pl.BlockSpec((1, GLA_DV), lambda i: (0, 0)), st],
        [pl.BlockSpec((nb, GLA_V), lambda i: (i, 0)), st],
        [jax.ShapeDtypeStruct((n, GLA_V), bf16), jax.ShapeDtypeStruct(states.shape, f32)],
        [], ("parallel",), "gla_step",
        (proj, proj, proj, proj, small, wgate_pad, bgate, norm_w, states),
        prev)


def _ssd_step_kernel(alog_ref, dtb_ref, z_ref, x_ref, bm_ref, cm_ref, sm_ref, cx_ref, cb_ref, cc_ref,
                     wx_ref, wb_ref, wc_ref, bx_ref, bb_ref, bc_ref, dvec_ref, nw_ref, h_ref,
                     o_ref, ho_ref, y_ref):
    g = pl.program_id(1)
    nb = x_ref.shape[0]
    x = _silu(_conv_step(x_ref, cx_ref, wx_ref) + bx_ref[...])
    bm = _silu(_conv_step(bm_ref, cb_ref, wb_ref) + bb_ref[...])
    cm = _silu(_conv_step(cm_ref, cc_ref, wc_ref) + bc_ref[...])
    sm = sm_ref[...]
    cbdot = jnp.sum(cm * bm, axis=-1, keepdims=True)
    lane_b = lax.broadcasted_iota(jnp.int32, (SSD_P, nb), 1)
    for jp in range(SSD_HG // 2):
        x_t = x[:, jp * LANE:(jp + 1) * LANE].T
        halves = []
        for jj in range(2):
            j = 2 * jp + jj
            hd = g * SSD_HG + j
            dt = jax.nn.softplus(_lane_col(sm, SM_DT + hd) + dtb_ref[hd])
            ea = jnp.exp(dt * (-jnp.exp(_scalar_vec(alog_ref[hd]))))
            y_t = jnp.zeros((SSD_P, nb), f32)
            for b in range(nb):
                hh = h_ref[b, j]
                xdt = x_t[jj * SSD_P:(jj + 1) * SSD_P, b:b + 1] * dt[b:b + 1]
                eab = ea[b:b + 1]
                ycol = jnp.sum(hh * cm[b:b + 1], axis=-1, keepdims=True) * eab + cbdot[b:b + 1] * xdt
                ho_ref[b, j] = hh * eab + xdt * bm[b:b + 1]
                y_t = jnp.where(lane_b == b, ycol, y_t)
            halves.append(y_t)
        y_ref[:, jp * LANE:(jp + 1) * LANE] = jnp.concatenate(halves, axis=0).T
    y = (y_ref[...] + dvec_ref[...] * x) * _silu(z_ref[...])
    o_ref[...] = _rms(y, nw_ref[...]).astype(bf16)


def _ssd_step(proj, small, conv_t, states, prev, l, conv_w, conv_b, a_log, dt_bias, dvec, norm_w):
    n = proj.shape[0]
    nb = STEP_B
    gw = SSD_GW
    nbk = SSD_INNER // SSD_N
    smem = pl.BlockSpec(memory_space=pltpu.SMEM)
    st = pl.BlockSpec((None, nb, SSD_HG, SSD_P, SSD_N), lambda i, g: (l, i, g, 0, 0))
    return _state_call(
        _ssd_step_kernel, (n // nb, SSD_G),
        [smem, smem,
         pl.BlockSpec((nb, gw), lambda i, g: (i, _blk(P_Z_C, gw) + g)),
         pl.BlockSpec((nb, gw), lambda i, g: (i, _blk(P_XBC, gw) + g)),
         pl.BlockSpec((nb, SSD_N), lambda i, g: (i, _blk(P_XBC + SSD_INNER, SSD_N) + g)),
         pl.BlockSpec((nb, SSD_N), lambda i, g: (i, _blk(P_XBC + SSD_INNER + SSD_BC, SSD_N) + g)),
         pl.BlockSpec((nb, LANE), lambda i, g: (i, 0)),
         pl.BlockSpec((CONV_W - 1, nb, gw), lambda i, g: (0, i, g)),
         pl.BlockSpec((CONV_W - 1, nb, SSD_N), lambda i, g: (0, i, nbk + g)),
         pl.BlockSpec((CONV_W - 1, nb, SSD_N), lambda i, g: (0, i, nbk + SSD_G + g)),
         pl.BlockSpec((CONV_W, gw), lambda i, g: (0, g)),
         pl.BlockSpec((CONV_W, SSD_N), lambda i, g: (0, nbk + g)),
         pl.BlockSpec((CONV_W, SSD_N), lambda i, g: (0, nbk + SSD_G + g)),
         pl.BlockSpec((1, gw), lambda i, g: (0, g)),
         pl.BlockSpec((1, SSD_N), lambda i, g: (0, nbk + g)),
         pl.BlockSpec((1, SSD_N), lambda i, g: (0, nbk + SSD_G + g)),
         pl.BlockSpec((1, gw), lambda i, g: (0, g)),
         pl.BlockSpec((1, gw), lambda i, g: (0, g)), st],
        [pl.BlockSpec((nb, gw), lambda i, g: (i, g)), st],
        [jax.ShapeDtypeStruct((n, SSD_INNER), bf16), jax.ShapeDtypeStruct(states.shape, f32)],
        [pltpu.VMEM((nb, gw), f32)], ("parallel", "parallel"), "ssd_step",
        (a_log, dt_bias, proj, proj, proj, proj, small, conv_t, conv_t, conv_t,
         conv_w, conv_w, conv_w, conv_b, conv_b, conv_b, dvec, norm_w, states),
        prev)


def _merge_kernel(oa_ref, ob_ref, oc_ref, ga_ref, gb_ref, gc_ref, wa_ref, wb_ref, wc_ref, o_ref):
    m = (_sigmoid(ga_ref[...].astype(f32)) * _dot(oa_ref[...], wa_ref[...])
         + _sigmoid(gb_ref[...].astype(f32)) * _dot(ob_ref[...], wb_ref[...])
         + _sigmoid(gc_ref[...].astype(f32)) * _dot(oc_ref[...], wc_ref[...]))
    o_ref[...] = m.astype(bf16)


def _merge(oa, ob, oc, proj, wa, wb, wc, l, tm):
    bx, tx, _ = oa.shape
    tn = 512
    gate = lambda k: pl.BlockSpec((None, tm, tn), lambda b, i, j: (b, i, _blk(P_GATES + k * D_MODEL, tn) + j))
    act = lambda w: pl.BlockSpec((None, tm, w), lambda b, i, j: (b, i, 0))
    wsp = lambda w: pl.BlockSpec((None, w, tn), lambda b, i, j: (l, 0, j))
    return pl.pallas_call(
        _merge_kernel,
        grid=(bx, tx // tm, D_MODEL // tn),
        in_specs=[act(GDN_V), act(GLA_V), act(SSD_INNER), gate(0), gate(1), gate(2),
                  wsp(GDN_V), wsp(GLA_V), wsp(SSD_INNER)],
        out_specs=pl.BlockSpec((None, tm, tn), lambda b, i, j: (b, i, j)),
        out_shape=jax.ShapeDtypeStruct((bx, tx, D_MODEL), bf16),
        compiler_params=_cp(("parallel", "parallel", "arbitrary")),
        name="merge",
    )(oa, ob, oc, proj, proj, proj, wa, wb, wc)


def _outproj_kernel(m_ref, x_ref, gt_ref, w_ref, o_ref):
    o_ref[...] = x_ref[...] + gt_ref[...] * _dot(m_ref[...], w_ref[...])


def _outproj(merged, x, mod, w, l, per_row, tm):
    bx, tx, dm = x.shape
    tn = 512
    nj = dm // tn
    if per_row:
        gspec = pl.BlockSpec((None, tm, tn), lambda b, i, j: (b, i, 5 * nj + j))
    else:
        gspec = pl.BlockSpec((None, 1, tn), lambda b, i, j: (b, 0, 5 * nj + j))
    return pl.pallas_call(
        _outproj_kernel,
        grid=(bx, tx // tm, nj),
        in_specs=[pl.BlockSpec((None, tm, dm), lambda b, i, j: (b, i, 0)),
                  pl.BlockSpec((None, tm, tn), lambda b, i, j: (b, i, j)),
                  gspec,
                  pl.BlockSpec((None, dm, tn), lambda b, i, j: (l, 0, j))],
        out_specs=pl.BlockSpec((None, tm, tn), lambda b, i, j: (b, i, j)),
        out_shape=jax.ShapeDtypeStruct(x.shape, f32),
        compiler_params=_cp(("parallel", "parallel", "arbitrary")),
        name="outproj",
    )(merged, x, mod, w)


def _permute_w_in(w):
    nl, d, _ = w.shape
    wt = jnp.swapaxes(w, 1, 2)
    rows = lambda name: wt[:, W_IN_SRC[name][0]:W_IN_SRC[name][1]]
    main = jnp.concatenate([rows(name) for name in MAIN_ORDER], axis=1).astype(bf16)
    small = jnp.concatenate([rows(name) for name in SMALL_ORDER], axis=1)
    small = jnp.pad(small, ((0, 0), (0, LANE - small.shape[1]), (0, 0))).astype(bf16)
    return main, small


def _ffn_weights(wg, wu, wd):
    f = wg.shape[2]
    cut = f - f % FF_TILE
    w = tuple(a.astype(bf16) for a in (wg, wu, wd))
    if cut == f:
        return w
    return w + (w[0][:, :, cut:], w[1][:, :, cut:], w[2][:, cut:, :])


def _lane_vec(v, lo):
    return jnp.zeros((1, LANE), f32).at[0, lo:lo + v.shape[0]].set(v)


def _layer_params(l, p):
    row = lambda a: a[l].reshape(1, -1)
    wgate = jnp.zeros((LANE, GLA_QK), f32).at[SM_LR:SM_LR + GLA_RANK].set(p["gla_w_gate"][l]).astype(bf16)
    return dict(
        norm1=row(p["norm1"]), norm2=row(p["norm2"]), norm3=row(p["norm3"]),
        gdn_conv_w=p["gdn_conv_w"][l], gdn_a_log=p["gdn_a_log"][l], gdn_dt_bias=p["gdn_dt_bias"][l],
        gdn_alog_l=_lane_vec(p["gdn_a_log"][l], SM_DEC), gdn_dtb_l=_lane_vec(p["gdn_dt_bias"][l], SM_DEC),
        gdn_norm_w=row(p["gdn_norm_w"]),
        gla_wgate=wgate, gla_bgate=row(p["gla_b_gate"]), gla_norm_w=row(p["gla_norm_w"]),
        ssd_conv_w=p["ssd_conv_w"][l], ssd_conv_b=row(p["ssd_conv_b"]), ssd_a_log=p["ssd_a_log"][l],
        ssd_dt_bias=p["ssd_dt_bias"][l],
        ssd_avec_l=_lane_vec(-jnp.exp(p["ssd_a_log"][l]), SM_DT), ssd_dtb_l=_lane_vec(p["ssd_dt_bias"][l], SM_DT),
        ssd_dvec=jnp.repeat(p["ssd_d"][l], SSD_P).reshape(1, -1),
        ssd_norm_w=row(p["ssd_norm_w"]),
    )


def _stacked_weights(p):
    w_in, w_in_small = _permute_w_in(p["w_in"])
    return dict(
        f1=_ffn_weights(p["ffn1_wg"], p["ffn1_wu"], p["ffn1_wd"]),
        f2=_ffn_weights(p["ffn2_wg"], p["ffn2_wu"], p["ffn2_wd"]),
        w_in=w_in, w_in_small=w_in_small,
        wa=p["w_branch_gdn"].astype(bf16), wb=p["w_branch_gla"].astype(bf16),
        wc=p["w_branch_ssd"].astype(bf16), w_out=p["w_out"].astype(bf16),
    )


def _new_conv_state(buf, raw):
    t = raw.shape[1]
    k = CONV_W - 1
    if t >= k:
        return raw[:, t - k:]
    return jnp.concatenate([buf[:, t:], raw], axis=1)


def _mixer_prompt(proj, small, lp, st):
    gdn_conv, s_gdn, s_gla, ssd_conv, s_ssd = st
    oa, s_gdn_n = _gdn_prompt(proj, small, gdn_conv, s_gdn, lp["gdn_conv_w"], lp["gdn_alog_l"], lp["gdn_dtb_l"],
                              lp["gdn_norm_w"])
    ob, s_gla_n = _gla_prompt(proj, small, s_gla, lp["gla_wgate"], lp["gla_bgate"], lp["gla_norm_w"])
    oc, s_ssd_n = _ssd_prompt(proj, small, ssd_conv, s_ssd, lp["ssd_conv_w"], lp["ssd_conv_b"], lp["ssd_avec_l"],
                              lp["ssd_dtb_l"], lp["ssd_dvec"], lp["ssd_norm_w"])
    gdn_conv_n = _new_conv_state(gdn_conv, proj[:, :, P_QKV_A:P_QKV_A + GDN_CONV]).astype(f32)
    ssd_conv_n = _new_conv_state(ssd_conv, proj[:, :, P_XBC:P_XBC + SSD_CONV]).astype(f32)
    return (oa, ob, oc), (gdn_conv_n, s_gdn_n, s_gla_n, ssd_conv_n, s_ssd_n)


def _mixer_sample(proj, small, lp, l, states, prev):
    n = proj.shape[1]
    p2 = proj.reshape(n, P_TOTAL)
    s2 = small.reshape(n, LANE)
    gdn_conv, ssd_conv = states[0][l], states[3][l]
    gct = jnp.swapaxes(gdn_conv, 0, 1)
    sct = jnp.swapaxes(ssd_conv, 0, 1)
    pv = (None,) * 5 if prev is None else prev
    oa, s_gdn_n = _gdn_step(p2, s2, gct, states[1], pv[1], l, lp["gdn_conv_w"], lp["gdn_a_log"],
                            lp["gdn_dt_bias"], lp["gdn_norm_w"])
    ob, s_gla_n = _gla_step(p2, s2, states[2], pv[2], l, lp["gla_wgate"], lp["gla_bgate"], lp["gla_norm_w"])
    oc, s_ssd_n = _ssd_step(p2, s2, sct, states[4], pv[4], l, lp["ssd_conv_w"], lp["ssd_conv_b"], lp["ssd_a_log"],
                            lp["ssd_dt_bias"], lp["ssd_dvec"], lp["ssd_norm_w"])
    raw = p2.reshape(n, 1, P_TOTAL)
    gdn_conv_n = _new_conv_state(gdn_conv, raw[:, :, P_QKV_A:P_QKV_A + GDN_CONV])
    ssd_conv_n = _new_conv_state(ssd_conv, raw[:, :, P_XBC:P_XBC + SSD_CONV])
    outs = tuple(o.reshape(1, n, -1) for o in (oa, ob, oc))
    return outs, (gdn_conv_n, s_gdn_n, s_gla_n, ssd_conv_n, s_ssd_n)


def _trunk(x, mods, lps, sw, states, per_row, tm, tm_in, final_w):
    nl = len(lps)
    per_layer = []
    prev = None
    for l in range(nl):
        lp, mod = lps[l], mods[l]
        last = l == nl - 1
        x = _ffn(x, mod, 0, lp["norm1"], sw["f1"], l, lp["norm1"], per_row, tm, False)
        proj, small = _inproj(x, mod, lp["norm2"], sw["w_in"], sw["w_in_small"], l, per_row, tm_in,
                              f32 if per_row else bf16)
        if per_row:
            (oa, ob, oc), st = _mixer_sample(proj, small, lp, l, states, prev)
            prev = st
        else:
            (oa, ob, oc), st = _mixer_prompt(proj, small, lp, tuple(s[l] for s in states))
        per_layer.append(st)
        merged = _merge(oa, ob, oc, proj, sw["wa"], sw["wb"], sw["wc"], l, tm)
        x = _outproj(merged, x, mod, sw["w_out"], l, per_row, tm)
        x = _ffn(x, mod, 6, lp["norm3"], sw["f2"], l, final_w if last else lp["norm3"], per_row, tm, last)
    stack = lambda i: jnp.stack([st[i] for st in per_layer])
    if per_row:
        new_states = (stack(0), prev[1], prev[2], stack(3), prev[4])
    else:
        new_states = tuple(stack(i) for i in range(5))
    return x, new_states


def kernel(x_prompt, x_sample, state_gdn_conv, state_gdn, state_gla, state_ssd_conv, state_ssd, c_prompt, c_sample, w_ada, b_ada, norm1, norm2, norm3, ffn1_wg, ffn1_wu, ffn1_wd, ffn2_wg, ffn2_wu, ffn2_wd, w_in, gdn_conv_w, gdn_a_log, gdn_dt_bias, gdn_norm_w, gla_w_gate, gla_b_gate, gla_norm_w, ssd_conv_w, ssd_conv_b, ssd_a_log, ssd_dt_bias, ssd_d, ssd_norm_w, w_branch_gdn, w_branch_gla, w_branch_ssd, w_out, final_norm):
    p = dict(norm1=norm1, norm2=norm2, norm3=norm3,
             ffn1_wg=ffn1_wg, ffn1_wu=ffn1_wu, ffn1_wd=ffn1_wd, ffn2_wg=ffn2_wg, ffn2_wu=ffn2_wu, ffn2_wd=ffn2_wd,
             w_in=w_in, gdn_conv_w=gdn_conv_w, gdn_a_log=gdn_a_log, gdn_dt_bias=gdn_dt_bias, gdn_norm_w=gdn_norm_w,
             gla_w_gate=gla_w_gate, gla_b_gate=gla_b_gate, gla_norm_w=gla_norm_w,
             ssd_conv_w=ssd_conv_w, ssd_conv_b=ssd_conv_b, ssd_a_log=ssd_a_log, ssd_dt_bias=ssd_dt_bias,
             ssd_d=ssd_d, ssd_norm_w=ssd_norm_w,
             w_branch_gdn=w_branch_gdn, w_branch_gla=w_branch_gla, w_branch_ssd=w_branch_ssd, w_out=w_out)
    nl = w_ada.shape[0]
    bp, tp, dm = x_prompt.shape
    bs = x_sample.shape[0]
    assert x_sample.shape[1] == 1 and tp % CHUNK == 0 and bs % STEP_B == 0 and dm == D_MODEL
    lps = [_layer_params(l, p) for l in range(nl)]
    sw = _stacked_weights(p)
    fw = final_norm.reshape(1, dm)

    rows = bp + bs
    rpad = -(-rows // 8) * 8
    c_all = jnp.concatenate([c_prompt, c_sample, jnp.zeros((rpad - rows, dm), f32)], axis=0)
    mod = _ada_mod(c_all, w_ada, b_ada)
    mod_p = [mod[l, :bp].reshape(bp, 1, N_MOD * dm) for l in range(nl)]
    mod_s = [mod[l, bp:rows].reshape(1, bs, N_MOD * dm) for l in range(nl)]

    sample_states = (state_gdn_conv, state_gdn, state_gla, state_ssd_conv, state_ssd)
    prompt_states = tuple(jnp.zeros((s.shape[0], bp) + s.shape[2:], x_prompt.dtype) for s in sample_states)
    tm_p = 512 if tp % 512 == 0 else CHUNK
    tm_in = 1024 if tp % 1024 == 0 else tm_p
    y_p, st_p = _trunk(x_prompt, mod_p, lps, sw, prompt_states, False, tm_p, tm_in, fw)
    y_s, st_s = _trunk(x_sample.reshape(1, bs, dm), mod_s, lps, sw, sample_states, True, bs, bs, fw)
    return (y_p, y_s.reshape(bs, 1, dm)) + st_p + st_s
```

```python
import functools

import jax
import jax.numpy as jnp
from jax import lax
from jax.experimental import pallas as pl
from jax.experimental.pallas import tpu as pltpu

f32 = jnp.float32
bf16 = jnp.bfloat16
HI = lax.Precision.HIGHEST

EPS = 1e-6
D_MODEL = 2048
N_MOD = 9
CHUNK = 64
CONV_W = 4
GDN_H, GDN_DK, GDN_DV = 8, 128, 128
GLA_H, GLA_DK, GLA_DV, GLA_RANK, GLA_TAU = 4, 128, 256, 16, 16.0
SSD_H, SSD_P, SSD_G, SSD_N = 32, 64, 4, 128
SSD_HG = SSD_H // SSD_G
GDN_QK = GDN_H * GDN_DK
GDN_V = GDN_H * GDN_DV
GDN_CONV = 2 * GDN_QK + GDN_V
GLA_QK = GLA_H * GLA_DK
GLA_V = GLA_H * GLA_DV
SSD_INNER = SSD_H * SSD_P
SSD_BC = SSD_G * SSD_N
SSD_CONV = SSD_INNER + 2 * SSD_BC
SSD_GW = SSD_HG * SSD_P

LANE = 128
FF_TILE = 512

IN_SPLITS = (("qkv_a", GDN_CONV), ("z_a", GDN_V), ("beta", GDN_H), ("dec", GDN_H),
             ("q_b", GLA_QK), ("k_b", GLA_QK), ("v_b", GLA_V), ("lr", GLA_RANK), ("r_b", GLA_V),
             ("z_c", SSD_INNER), ("xbc", SSD_CONV), ("dt", SSD_H), ("gates", 3 * D_MODEL))
MAIN_ORDER = ("qkv_a", "xbc", "z_c", "gates", "z_a", "q_b", "k_b", "v_b", "r_b")
SMALL_ORDER = ("beta", "dec", "lr", "dt")


def _layout():
    src, off = {}, 0
    for name, w in IN_SPLITS:
        src[name] = (off, off + w)
        off += w
    main, small, d = {}, {}, 0
    for name in MAIN_ORDER:
        w = src[name][1] - src[name][0]
        main[name] = d
        d += w
    total = d
    d = 0
    for name in SMALL_ORDER:
        small[name] = d
        d += src[name][1] - src[name][0]
    assert d <= LANE
    return src, main, small, total


W_IN_SRC, P_MAIN, P_SM, P_TOTAL = _layout()
P_QKV_A, P_XBC, P_Z_C, P_GATES, P_Z_A = (P_MAIN[k] for k in ("qkv_a", "xbc", "z_c", "gates", "z_a"))
P_Q_B, P_K_B, P_V_B, P_R_B = (P_MAIN[k] for k in ("q_b", "k_b", "v_b", "r_b"))
SM_BETA, SM_DEC, SM_LR, SM_DT = (P_SM[k] for k in SMALL_ORDER)

VMEM_LIMIT = 56 * 1024 * 1024


def _cp(sem):
    return pltpu.CompilerParams(dimension_semantics=sem, vmem_limit_bytes=VMEM_LIMIT)


def _blk(off, width):
    assert off % width == 0, (off, width)
    return off // width


def _sigmoid(x):
    return jax.nn.sigmoid(x)


def _silu(x):
    return x * jax.nn.sigmoid(x)


def _rms(x, w):
    return x * lax.rsqrt(jnp.mean(x * x, axis=-1, keepdims=True) + EPS) * w


def _dot(a, b):
    return jnp.dot(a, b, preferred_element_type=f32)


def _dot_nt(a, b):
    return lax.dot_general(a, b, (((1,), (1,)), ((), ())), preferred_element_type=f32)


def _dot_tn(a, b):
    return lax.dot_general(a, b, (((0,), (0,)), ((), ())), preferred_element_type=f32)


def _split(x):
    hi = x.astype(bf16)
    return hi, (x - hi.astype(f32)).astype(bf16)


def _dot3(a, b):
    return _dot(a[0], b[0]) + (_dot(a[0], b[1]) + _dot(a[1], b[0]))


def _dot_hi(a, b):
    return jnp.dot(a, b, precision=HI, preferred_element_type=f32)


def _tri_masks(c):
    row = lax.broadcasted_iota(jnp.int32, (c, c), 0)
    col = lax.broadcasted_iota(jnp.int32, (c, c), 1)
    return row >= col, row > col, row == col


def _lane_col(x, idx):
    lane = lax.broadcasted_iota(jnp.int32, x.shape, 1)
    return jnp.sum(jnp.where(lane == idx, x, 0.0), axis=1, keepdims=True)


def _scalar_vec(s):
    return jnp.full((1, 1), s, f32)


def _ada_kernel(c_ref, w_ref, b_ref, o_ref):
    s = _silu(c_ref[...]).astype(bf16)
    o_ref[...] = _dot(s, w_ref[...].astype(bf16)) + b_ref[...]


def _ada_mod(c_all, w_ada, b_ada):
    nl, dm, n = w_ada.shape
    r = c_all.shape[0]
    tn = 1024
    return pl.pallas_call(
        _ada_kernel,
        grid=(nl, n // tn),
        in_specs=[pl.BlockSpec((r, dm), lambda l, j: (0, 0)),
                  pl.BlockSpec((None, dm, tn), lambda l, j: (l, 0, j)),
                  pl.BlockSpec((None, 1, tn), lambda l, j: (l, 0, j))],
        out_specs=pl.BlockSpec((None, r, tn), lambda l, j: (l, 0, j)),
        out_shape=jax.ShapeDtypeStruct((nl, r, n), f32),
        compiler_params=_cp(("arbitrary", "arbitrary")),
        name="ada_mod",
    )(c_all, w_ada, b_ada.reshape(nl, 1, n))


def _mod_spec(per_row, tm, chunk):
    if per_row:
        return pl.BlockSpec((None, tm, D_MODEL), lambda b, i, j: (b, i, chunk))
    return pl.BlockSpec((None, 1, D_MODEL), lambda b, i, j: (b, 0, chunk))


def _ffn_kernel(x_ref, sh_ref, sc_ref, gt_ref, nw_ref, fw_ref, wg_ref, wu_ref, wd_ref, *rest, final, has_tail):
    if has_tail:
        wgt_ref, wut_ref, wdt_ref, o_ref, h_ref, acc_ref = rest
    else:
        o_ref, h_ref, acc_ref = rest
    f = pl.program_id(2)
    last = pl.num_programs(2) - 1

    @pl.when(f == 0)
    def _():
        y = _rms(x_ref[...], nw_ref[...])
        h_ref[...] = (y * (1.0 + sc_ref[...]) + sh_ref[...]).astype(bf16)
        acc_ref[...] = jnp.zeros_like(acc_ref)

    def accumulate(wg, wu, wd):
        h = h_ref[...]
        a = (_silu(_dot(h, wg[...])) * _dot(h, wu[...])).astype(bf16)
        acc_ref[...] += _dot(a, wd[...])

    if has_tail:
        pl.when(f < last)(lambda: accumulate(wg_ref, wu_ref, wd_ref))
        pl.when(f == last)(lambda: accumulate(wgt_ref, wut_ref, wdt_ref))
    else:
        accumulate(wg_ref, wu_ref, wd_ref)

    @pl.when(f == last)
    def _():
        y = x_ref[...] + 0.5 * gt_ref[...] * acc_ref[...]
        if final:
            y = _rms(y, fw_ref[...])
        o_ref[...] = y


def _ffn(x, mod, k0, nw, w, l, fw, per_row, tm, final):
    bx, tx, dm = x.shape
    tf = FF_TILE
    n_full = w[0].shape[2] // tf
    has_tail = len(w) > 3
    full = lambda f: jnp.minimum(f, n_full - 1)
    w_specs = [pl.BlockSpec((None, dm, tf), lambda b, i, f: (l, 0, full(f))),
               pl.BlockSpec((None, dm, tf), lambda b, i, f: (l, 0, full(f))),
               pl.BlockSpec((None, tf, dm), lambda b, i, f: (l, full(f), 0))]
    if has_tail:
        ft = w[3].shape[2]
        w_specs += [pl.BlockSpec((None, dm, ft), lambda b, i, f: (l, 0, 0)),
                    pl.BlockSpec((None, dm, ft), lambda b, i, f: (l, 0, 0)),
                    pl.BlockSpec((None, ft, dm), lambda b, i, f: (l, 0, 0))]
    return pl.pallas_call(
        functools.partial(_ffn_kernel, final=final, has_tail=has_tail),
        grid=(bx, tx // tm, n_full + int(has_tail)),
        in_specs=[pl.BlockSpec((None, tm, dm), lambda b, i, f: (b, i, 0)),
                  _mod_spec(per_row, tm, k0), _mod_spec(per_row, tm, k0 + 1), _mod_spec(per_row, tm, k0 + 2),
                  pl.BlockSpec((1, dm), lambda b, i, f: (0, 0)),
                  pl.BlockSpec((1, dm), lambda b, i, f: (0, 0))] + w_specs,
        out_specs=pl.BlockSpec((None, tm, dm), lambda b, i, f: (b, i, 0)),
        out_shape=jax.ShapeDtypeStruct(x.shape, f32),
        scratch_shapes=[pltpu.VMEM((tm, dm), bf16), pltpu.VMEM((tm, dm), f32)],
        compiler_params=_cp(("parallel", "parallel", "arbitrary")),
        name="ffn",
    )(x, mod, mod, mod, nw, fw, *w)


def _inproj_kernel(x_ref, sh_ref, sc_ref, nw_ref, w_ref, ws_ref, o_ref, os_ref, h_ref):
    @pl.when(pl.program_id(2) == 0)
    def _():
        y = _rms(x_ref[...], nw_ref[...])
        h = (y * (1.0 + sc_ref[...]) + sh_ref[...]).astype(bf16)
        h_ref[...] = h
        os_ref[...] = _dot_nt(h, ws_ref[...])

    o_ref[...] = _dot_nt(h_ref[...], w_ref[...]).astype(o_ref.dtype)


def _inproj(x, mod, nw, w, ws, l, per_row, tm, out_dtype):
    bx, tx, dm = x.shape
    n = w.shape[1]
    tn = 1024
    return pl.pallas_call(
        _inproj_kernel,
        grid=(bx, tx // tm, n // tn),
        in_specs=[pl.BlockSpec((None, tm, dm), lambda b, i, j: (b, i, 0)),
                  _mod_spec(per_row, tm, 3), _mod_spec(per_row, tm, 4),
                  pl.BlockSpec((1, dm), lambda b, i, j: (0, 0)),
                  pl.BlockSpec((None, tn, dm), lambda b, i, j: (l, j, 0)),
                  pl.BlockSpec((None, LANE, dm), lambda b, i, j: (l, 0, 0))],
        out_specs=[pl.BlockSpec((None, tm, tn), lambda b, i, j: (b, i, j)),
                   pl.BlockSpec((None, tm, LANE), lambda b, i, j: (b, i, 0))],
        out_shape=[jax.ShapeDtypeStruct((bx, tx, n), out_dtype),
                   jax.ShapeDtypeStruct((bx, tx, LANE), f32)],
        scratch_shapes=[pltpu.VMEM((tm, dm), bf16)],
        compiler_params=_cp(("parallel", "parallel", "arbitrary")),
        name="inproj",
    )(x, mod, mod, nw, w, ws)


def _conv_load(ext_ref, x_ref, cst_ref, first):
    c = x_ref.shape[0]

    @pl.when(first)
    def _():
        ext_ref[pl.ds(5, CONV_W - 1), :] = cst_ref[...]

    ext_ref[pl.ds(8, c), :] = x_ref[...].astype(f32)


def _conv_cols(ext_ref, w_ref, b_ref, lo, width, c):
    sl = pl.ds(lo, width)
    w = w_ref[:, sl]
    y = ext_ref[pl.ds(5, c), sl] * w[0:1]
    for j in range(1, CONV_W):
        y = y + ext_ref[pl.ds(5 + j, c), sl] * w[j:j + 1]
    if b_ref is not None:
        y = y + b_ref[:, sl]
    return _silu(y)


def _conv_carry(ext_ref, c):
    ext_ref[pl.ds(5, CONV_W - 1), :] = ext_ref[pl.ds(8 + c - (CONV_W - 1), CONV_W - 1), :]


def _cumsum_rows(x, tri):
    return _dot_hi(jnp.where(tri, 1.0, 0.0).astype(f32), x)


def _gdn_kernel(qkv_ref, z_ref, sm_ref, cst_ref, cw_ref, alog_ref, dtb_ref, nw_ref, s0_ref,
                o_ref, s_ref, ext_ref):
    c = qkv_ref.shape[0]
    first = pl.program_id(1) == 0

    @pl.when(first)
    def _():
        s_ref[...] = s0_ref[...]

    _conv_load(ext_ref, qkv_ref, cst_ref, first)

    sm = sm_ref[...]
    beta_l = _sigmoid(sm)
    g_l = -jnp.exp(alog_ref[...]) * jax.nn.softplus(sm + dtb_ref[...])
    tri, strict, eye = _tri_masks(c)
    gc_all = _cumsum_rows(g_l, tri)
    gc_t = gc_all.T
    eye_f = jnp.where(eye, 1.0, 0.0).astype(f32)
    n_fac = max(1, (c - 1).bit_length() - 1)

    hs = range(GDN_H)
    qb, kb, ks, decay, eg, gcs, nm, rhs = [], [], [], [], [], [], [], []
    for h in hs:
        q = _conv_cols(ext_ref, cw_ref, None, h * GDN_DK, GDN_DK, c)
        k = _conv_cols(ext_ref, cw_ref, None, GDN_QK + h * GDN_DK, GDN_DK, c)
        v = _conv_cols(ext_ref, cw_ref, None, 2 * GDN_QK + h * GDN_DV, GDN_DV, c)
        q = q * lax.rsqrt(jnp.sum(q * q, axis=-1, keepdims=True) + EPS) * (GDN_DK ** -0.5)
        k = k * lax.rsqrt(jnp.sum(k * k, axis=-1, keepdims=True) + EPS)
        beta = beta_l[:, SM_BETA + h:SM_BETA + h + 1]
        gc = gc_all[:, SM_DEC + h:SM_DEC + h + 1]
        d = jnp.exp(jnp.where(tri, gc - gc_t[SM_DEC + h:SM_DEC + h + 1, :], -jnp.inf))
        e = jnp.exp(gc)
        qb.append(q.astype(bf16))
        kb.append(k.astype(bf16))
        ks.append(k)
        decay.append(d)
        eg.append(e)
        gcs.append(gc)
        nm.append(jnp.where(strict, beta * d * _dot_nt(kb[h], kb[h]), 0.0))
        rhs.append(jnp.concatenate([v * beta, k * (beta * e)], axis=1))

    inv = [eye_f - nm[h] for h in hs]
    pw_s = [_split(nm[h]) for h in hs]
    pw_s = [_split(_dot3(pw_s[h], pw_s[h])) for h in hs]
    for i in range(n_fac):
        inv = [inv[h] + _dot3(_split(inv[h]), pw_s[h]) for h in hs]
        if i + 1 < n_fac:
            pw_s = [_split(_dot3(pw_s[h], pw_s[h])) for h in hs]
    sol = [_dot3(_split(inv[h]), _split(rhs[h])) for h in hs]

    s = [s_ref[h] for h in hs]
    sb = [s[h].astype(bf16) for h in hs]
    wb = [(sol[h][:, 0:GDN_DV] - _dot(sol[h][:, GDN_DV:].astype(bf16), sb[h])).astype(bf16) for h in hs]
    qk = [(_dot_nt(qb[h], kb[h]) * decay[h]).astype(bf16) for h in hs]
    o = [_dot(qb[h], sb[h]) * eg[h] + _dot(qk[h], wb[h]) for h in hs]
    for h in hs:
        g_last = gcs[h][c - 1:c]
        kd = (ks[h] * jnp.exp(g_last - gcs[h])).astype(bf16)
        s_ref[h] = s[h] * jnp.exp(g_last) + _dot_tn(kd, wb[h])
    for h in hs:
        sl = slice(h * GDN_DV, (h + 1) * GDN_DV)
        o_ref[:, sl] = (_rms(o[h], nw_ref[...]) * _silu(z_ref[:, sl].astype(f32))).astype(bf16)

    _conv_carry(ext_ref, c)


def _gdn_prompt(proj, small, conv_state, s0, conv_w, alog_l, dtb_l, norm_w):
    b, t, _ = proj.shape
    c = CHUNK
    const = lambda shape: pl.BlockSpec(shape, lambda bi, ti: (0,) * len(shape))
    st = pl.BlockSpec((None, GDN_H, GDN_DK, GDN_DV), lambda bi, ti: (bi, 0, 0, 0))
    return pl.pallas_call(
        _gdn_kernel,
        grid=(b, t // c),
        in_specs=[pl.BlockSpec((None, c, GDN_CONV), lambda bi, ti: (bi, ti, _blk(P_QKV_A, GDN_CONV))),
                  pl.BlockSpec((None, c, GDN_V), lambda bi, ti: (bi, ti, _blk(P_Z_A, GDN_V))),
                  pl.BlockSpec((None, c, LANE), lambda bi, ti: (bi, ti, 0)),
                  pl.BlockSpec((None, CONV_W - 1, GDN_CONV), lambda bi, ti: (bi, 0, 0)),
                  const((CONV_W, GDN_CONV)), const((1, LANE)), const((1, LANE)), const((1, GDN_DV)),
                  st],
        out_specs=[pl.BlockSpec((None, c, GDN_V), lambda bi, ti: (bi, ti, 0)), st],
        out_shape=[jax.ShapeDtypeStruct((b, t, GDN_V), bf16),
                   jax.ShapeDtypeStruct((b, GDN_H, GDN_DK, GDN_DV), f32)],
        scratch_shapes=[pltpu.VMEM((8 + c, GDN_CONV), f32)],
        compiler_params=_cp(("parallel", "arbitrary")),
        name="gdn_prompt",
    )(proj, proj, small, conv_state, conv_w, alog_l, dtb_l, norm_w, s0)


GLA_SUB = 16


def _gla_kernel(q_ref, k_ref, v_ref, r_ref, sm_ref, wgate_ref, bgate_ref, nw_ref, s0_ref,
                o_ref, so_ref, st_ref):
    c = q_ref.shape[0]
    ti = pl.program_id(1)

    @pl.when(ti == 0)
    def _():
        for h in range(GLA_H):
            st_ref[h] = s0_ref[h].T

    sm = sm_ref[...]
    lane = lax.broadcasted_iota(jnp.int32, sm.shape, 1)
    lr = jnp.where((lane >= SM_LR) & (lane < SM_LR + GLA_RANK), sm, 0.0).astype(bf16)
    la = jax.nn.log_sigmoid(_dot(lr, wgate_ref[...]) + bgate_ref[...]) / GLA_TAU
    tri, _, _ = _tri_masks(c)
    b_all = _cumsum_rows(la, tri)

    col = lax.broadcasted_iota(jnp.int32, (GLA_SUB, c), 1)
    row = lax.broadcasted_iota(jnp.int32, (GLA_SUB, c), 0)
    hs = range(GLA_H)
    qs, ks, vbs, bs, atts = [], [], [], [], []
    for h in hs:
        sk = slice(h * GLA_DK, (h + 1) * GLA_DK)
        q = q_ref[:, sk].astype(f32) * (GLA_DK ** -0.5)
        k = k_ref[:, sk].astype(f32)
        b = b_all[:, sk]
        qs.append(q)
        ks.append(k)
        bs.append(b)
        vbs.append(v_ref[:, h * GLA_DV:(h + 1) * GLA_DV].astype(bf16))
        att_rows = []
        for i in range(c // GLA_SUB):
            lo = i * GLA_SUB
            b_i = b[lo:lo + GLA_SUB]
            q_i = q[lo:lo + GLA_SUB]
            b_top = b[lo:lo + 1]
            att = jnp.zeros((GLA_SUB, c), f32)
            if i > 0:
                q_t = (q_i * jnp.exp(b_i - b_top)).astype(bf16)
                k_t = (k * jnp.exp(jnp.minimum(b_top - b, 0.0))).astype(bf16)
                att = jnp.where(col < lo, _dot_nt(q_t, k_t), 0.0)
            for sl in range(GLA_SUB):
                sidx = lo + sl
                e = jnp.exp(jnp.minimum(b_i - b[sidx:sidx + 1], 0.0))
                p = jnp.sum(q_i * k[sidx:sidx + 1] * e, axis=1, keepdims=True)
                att = jnp.where((col == sidx) & (row >= sl), p, att)
            att_rows.append(att)
        atts.append(jnp.concatenate(att_rows, axis=0).astype(bf16))

    sts = [st_ref[h] for h in hs]
    os_ = [_dot(atts[h], vbs[h]) + _dot_nt((qs[h] * jnp.exp(bs[h])).astype(bf16), sts[h].astype(bf16))
           for h in hs]
    st_new = []
    for h in hs:
        b_last = bs[h][c - 1:c]
        kd = (ks[h] * jnp.exp(b_last - bs[h])).astype(bf16)
        st_new.append(sts[h] * jnp.exp(b_last) + _dot_tn(vbs[h], kd))
        st_ref[h] = st_new[h]

    @pl.when(ti == pl.num_programs(1) - 1)
    def _():
        for h in hs:
            so_ref[h] = st_new[h].T

    for h in hs:
        sv = slice(h * GLA_DV, (h + 1) * GLA_DV)
        o_ref[:, sv] = (_rms(os_[h], nw_ref[...]) * _silu(r_ref[:, sv].astype(f32))).astype(bf16)


def _gla_prompt(proj, small, s0, wgate_pad, bgate, norm_w):
    b, t, _ = proj.shape
    c = CHUNK
    const = lambda shape: pl.BlockSpec(shape, lambda bi, ti: (0,) * len(shape))
    st = pl.BlockSpec((None, GLA_H, GLA_DK, GLA_DV), lambda bi, ti: (bi, 0, 0, 0))
    return pl.pallas_call(
        _gla_kernel,
        grid=(b, t // c),
        in_specs=[pl.BlockSpec((None, c, GLA_QK), lambda bi, ti: (bi, ti, _blk(P_Q_B, GLA_QK))),
                  pl.BlockSpec((None, c, GLA_QK), lambda bi, ti: (bi, ti, _blk(P_K_B, GLA_QK))),
                  pl.BlockSpec((None, c, GLA_V), lambda bi, ti: (bi, ti, _blk(P_V_B, GLA_V))),
                  pl.BlockSpec((None, c, GLA_V), lambda bi, ti: (bi, ti, _blk(P_R_B, GLA_V))),
                  pl.BlockSpec((None, c, LANE), lambda bi, ti: (bi, ti, 0)),
                  const((LANE, GLA_QK)), const((1, GLA_QK)), const((1, GLA_DV)),
                  st],
        out_specs=[pl.BlockSpec((None, c, GLA_V), lambda bi, ti: (bi, ti, 0)), st],
        out_shape=[jax.ShapeDtypeStruct((b, t, GLA_V), bf16),
                   jax.ShapeDtypeStruct((b, GLA_H, GLA_DK, GLA_DV), f32)],
        scratch_shapes=[pltpu.VMEM((GLA_H, GLA_DV, GLA_DK), f32)],
        compiler_params=_cp(("parallel", "arbitrary")),
        name="gla_prompt",
    )(proj, proj, proj, proj, small, wgate_pad, bgate, norm_w, s0)


def _ssd_kernel(z_ref, xbc_ref, sm_ref, cst_ref, cw_ref, cb_ref, avec_ref, dtb_ref, dvec_ref, nw_ref, h0_ref,
                o_ref, h_ref, ext_ref):
    c = xbc_ref.shape[0]
    first = pl.program_id(1) == 0

    @pl.when(first)
    def _():
        h_ref[...] = h0_ref[...]

    _conv_load(ext_ref, xbc_ref, cst_ref, first)

    dt_l = jax.nn.softplus(sm_ref[...] + dtb_ref[...])
    tri, _, _ = _tri_masks(c)
    ac_all = _cumsum_rows(dt_l * avec_ref[...], tri)
    ac_t2 = jnp.concatenate([ac_all, ac_all], axis=0).T

    lane = lax.broadcasted_iota(jnp.int32, (c, 2 * SSD_P), 1)
    rowi = lax.broadcasted_iota(jnp.int32, (c, 2 * SSD_P), 0)
    lo_half = lane < SSD_P
    tri2 = rowi >= jnp.where(lo_half, lane, lane - SSD_P)
    lane1 = lane[0:1]
    gs_ = range(SSD_G)
    ps_ = range(SSD_HG // 2)

    xs_, bmb, hg, cb2, ch = [], [], [], [], []
    for g in gs_:
        xs_.append(_conv_cols(ext_ref, cw_ref, cb_ref, g * SSD_GW, SSD_GW, c))
        bmb.append(_conv_cols(ext_ref, cw_ref, cb_ref, SSD_INNER + g * SSD_N, SSD_N, c).astype(bf16))
        cmb = _conv_cols(ext_ref, cw_ref, cb_ref, SSD_INNER + SSD_BC + g * SSD_N, SSD_N, c).astype(bf16)
        hg.append(h_ref[pl.ds(g * SSD_HG, SSD_HG)].reshape(SSD_GW, SSD_N))
        cb2.append(_dot_nt(cmb, jnp.concatenate([bmb[g], bmb[g]], axis=0)))
        ch.append(_dot_nt(cmb, hg[g].astype(bf16)))

    m2, rhs, e_col, xsc = {}, {}, {}, {}
    for g in gs_:
        for p in ps_:
            l0 = SM_DT + g * SSD_HG + 2 * p
            ps = slice(p * 2 * SSD_P, (p + 1) * 2 * SSD_P)
            ac_col = jnp.where(lo_half, ac_all[:, l0:l0 + 1], ac_all[:, l0 + 1:l0 + 2])
            ac_row = jnp.where(lane1 < SSD_P, ac_t2[l0:l0 + 1, :], ac_t2[l0 + 1:l0 + 2, :])
            decay = jnp.exp(jnp.where(tri2, ac_col - ac_row, -jnp.inf))
            m2[g, p] = (cb2[g] * decay).astype(bf16)
            dt2 = jnp.where(lo_half, dt_l[:, l0:l0 + 1], dt_l[:, l0 + 1:l0 + 2])
            xdt = xs_[g][:, ps] * dt2
            rhs[g, p] = jnp.concatenate([jnp.where(lo_half, xdt, 0.0), jnp.where(lo_half, 0.0, xdt)],
                                        axis=0).astype(bf16)
            e_col[g, p] = jnp.exp(ac_col)
            xsc[g, p] = xdt * jnp.exp(ac_col[c - 1:c] - ac_col)
    y2 = {gp: _dot(m2[gp], rhs[gp]) for gp in m2}

    for g in gs_:
        xsg = jnp.concatenate([xsc[g, p] for p in ps_], axis=1).astype(bf16)
        dh = _dot_tn(xsg, bmb[g])
        for j in range(SSD_HG):
            ln = SM_DT + g * SSD_HG + j
            h_ref[g * SSD_HG + j] = (hg[g][j * SSD_P:(j + 1) * SSD_P] * jnp.exp(ac_all[c - 1:c, ln:ln + 1])
                                     + dh[j * SSD_P:(j + 1) * SSD_P])
    for g in gs_:
        gs = slice(g * SSD_GW, (g + 1) * SSD_GW)
        y = jnp.concatenate([y2[g, p] + ch[g][:, p * 2 * SSD_P:(p + 1) * 2 * SSD_P] * e_col[g, p] for p in ps_],
                            axis=1)
        y = (y + dvec_ref[:, gs] * xs_[g]) * _silu(z_ref[:, gs].astype(f32))
        o_ref[:, gs] = _rms(y, nw_ref[:, gs]).astype(bf16)

    _conv_carry(ext_ref, c)


def _ssd_prompt(proj, small, conv_state, h0, conv_w, conv_b, avec_l, dtb_l, dvec, norm_w):
    b, t, _ = proj.shape
    c = CHUNK
    const = lambda shape: pl.BlockSpec(shape, lambda bi, ti: (0,) * len(shape))
    st = pl.BlockSpec((None, SSD_H, SSD_P, SSD_N), lambda bi, ti: (bi, 0, 0, 0))
    return pl.pallas_call(
        _ssd_kernel,
        grid=(b, t // c),
        in_specs=[pl.BlockSpec((None, c, SSD_INNER), lambda bi, ti: (bi, ti, _blk(P_Z_C, SSD_INNER))),
                  pl.BlockSpec((None, c, SSD_CONV), lambda bi, ti: (bi, ti, _blk(P_XBC, SSD_CONV))),
                  pl.BlockSpec((None, c, LANE), lambda bi, ti: (bi, ti, 0)),
                  pl.BlockSpec((None, CONV_W - 1, SSD_CONV), lambda bi, ti: (bi, 0, 0)),
                  const((CONV_W, SSD_CONV)), const((1, SSD_CONV)), const((1, LANE)), const((1, LANE)),
                  const((1, SSD_INNER)), const((1, SSD_INNER)),
                  st],
        out_specs=[pl.BlockSpec((None, c, SSD_INNER), lambda bi, ti: (bi, ti, 0)), st],
        out_shape=[jax.ShapeDtypeStruct((b, t, SSD_INNER), bf16),
                   jax.ShapeDtypeStruct((b, SSD_H, SSD_P, SSD_N), f32)],
        scratch_shapes=[pltpu.VMEM((8 + c, SSD_CONV), f32)],
        compiler_params=_cp(("parallel", "arbitrary")),
        name="ssd_prompt",
    )(proj, proj, small, conv_state, conv_w, conv_b, avec_l, dtb_l, dvec, norm_w, h0)


STEP_B = 8


def _conv_step(x_ref, c_ref, w_ref):
    w = w_ref[...]
    return c_ref[0] * w[0:1] + c_ref[1] * w[1:2] + c_ref[2] * w[2:3] + x_ref[...] * w[3:4]


def _expand_matrix(nb, width):
    r = lax.broadcasted_iota(jnp.int32, (nb, nb * width), 0)
    c = lax.broadcasted_iota(jnp.int32, (nb, nb * width), 1)
    return jnp.where((c >= r * width) & (c < (r + 1) * width), 1.0, 0.0).astype(bf16)


def _bcast_cols(x, e_mat):
    hi, lo = _split(x)
    return _dot_tn(hi, e_mat) + _dot_tn(lo, e_mat)


def _state_call(kernel, grid, in_specs, out_specs, out_shape, scratch, sem, name, args, prev_state):
    aliases = {}
    if prev_state is not None:
        in_specs = in_specs + [pl.BlockSpec(memory_space=pl.ANY)]
        args = args + (prev_state,)
        aliases = {len(args) - 1: len(out_shape) - 1}
        kernel = functools.partial(_drop_alias_ref, kernel, len(args) - 1)
    return pl.pallas_call(kernel, grid=grid, in_specs=in_specs, out_specs=out_specs, out_shape=out_shape,
                          scratch_shapes=scratch, input_output_aliases=aliases,
                          compiler_params=_cp(sem), name=name)(*args)


def _drop_alias_ref(kernel, idx, *refs):
    return kernel(*refs[:idx], *refs[idx + 1:])


def _gdn_step_kernel(alog_ref, dtb_ref, q_ref, k_ref, v_ref, z_ref, sm_ref, cq_ref, ck_ref, cv_ref,
                     wq_ref, wk_ref, wv_ref, nw_ref, s_ref, o_ref, so_ref):
    nb = q_ref.shape[0]
    qa = _silu(_conv_step(q_ref, cq_ref, wq_ref))
    ka = _silu(_conv_step(k_ref, ck_ref, wk_ref))
    va = _silu(_conv_step(v_ref, cv_ref, wv_ref))
    sm = sm_ref[...]
    z = z_ref[...]
    e_mat = _expand_matrix(nb, GDN_DV)
    for h in range(GDN_H):
        sl = slice(h * GDN_DK, (h + 1) * GDN_DK)
        q = qa[:, sl]
        k = ka[:, sl]
        v = va[:, sl]
        q = q * lax.rsqrt(jnp.sum(q * q, axis=-1, keepdims=True) + EPS) * (GDN_DK ** -0.5)
        k = k * lax.rsqrt(jnp.sum(k * k, axis=-1, keepdims=True) + EPS)
        beta = _sigmoid(sm[:, SM_BETA + h:SM_BETA + h + 1])
        g = -jnp.exp(_scalar_vec(alog_ref[h])) * jax.nn.softplus(sm[:, SM_DEC + h:SM_DEC + h + 1] + dtb_ref[h])
        eg = jnp.exp(g)
        qk = jnp.sum(q * k, axis=-1, keepdims=True)
        kb = _bcast_cols(k, e_mat)
        qb = _bcast_cols(q, e_mat)
        blk = lambda m, b: m[:, b * GDN_DV:(b + 1) * GDN_DV]
        ks = jnp.concatenate([jnp.sum(s_ref[b, h] * blk(kb, b), axis=0, keepdims=True) for b in range(nb)], axis=0)
        qs = jnp.concatenate([jnp.sum(s_ref[b, h] * blk(qb, b), axis=0, keepdims=True) for b in range(nb)], axis=0)
        w = beta * v - (beta * eg) * ks
        o = qs * eg + qk * w
        for b in range(nb):
            so_ref[b, h] = s_ref[b, h] * eg[b:b + 1] + blk(kb, b) * w[b:b + 1]
        o = _rms(o, nw_ref[...]) * _silu(z[:, sl])
        o_ref[:, sl] = o.astype(bf16)


def _gdn_step(proj, small, conv_t, states, prev, l, conv_w, a_log, dt_bias, norm_w):
    n = proj.shape[0]
    nb = STEP_B
    blk = lambda off: pl.BlockSpec((nb, GDN_QK), lambda i: (i, _blk(off, GDN_QK)))
    cst = lambda j: pl.BlockSpec((CONV_W - 1, nb, GDN_QK), lambda i: (0, i, j))
    cw = lambda j: pl.BlockSpec((CONV_W, GDN_QK), lambda i: (0, j))
    smem = pl.BlockSpec(memory_space=pltpu.SMEM)
    st = pl.BlockSpec((None, nb, GDN_H, GDN_DK, GDN_DV), lambda i: (l, i, 0, 0, 0))
    return _state_call(
        _gdn_step_kernel, (n // nb,),
        [smem, smem, blk(P_QKV_A), blk(P_QKV_A + GDN_QK), blk(P_QKV_A + 2 * GDN_QK), blk(P_Z_A),
         pl.BlockSpec((nb, LANE), lambda i: (i, 0)),
         cst(0), cst(1), cst(2), cw(0), cw(1), cw(2),
         pl.BlockSpec((1, GDN_DV), lambda i: (0, 0)), st],
        [pl.BlockSpec((nb, GDN_V), lambda i: (i, 0)), st],
        [jax.ShapeDtypeStruct((n, GDN_V), bf16), jax.ShapeDtypeStruct(states.shape, f32)],
        [], ("parallel",), "gdn_step",
        (a_log, dt_bias, proj, proj, proj, proj, small, conv_t, conv_t, conv_t, conv_w, conv_w, conv_w, norm_w, states),
        prev)


def _gla_step_kernel(q_ref, k_ref, v_ref, r_ref, sm_ref, wgate_ref, bgate_ref, nw_ref, s_ref,
                     o_ref, so_ref):
    nb = q_ref.shape[0]
    sm = sm_ref[...]
    lane = lax.broadcasted_iota(jnp.int32, sm.shape, 1)
    lr = jnp.where((lane >= SM_LR) & (lane < SM_LR + GLA_RANK), sm, 0.0).astype(bf16)
    la_all = jax.nn.log_sigmoid(_dot(lr, wgate_ref[...]) + bgate_ref[...]) / GLA_TAU
    qa = q_ref[...] * (GLA_DK ** -0.5)
    ka = k_ref[...]
    va = v_ref[...]
    r = r_ref[...]
    for h in range(GLA_H):
        sk = slice(h * GLA_DK, (h + 1) * GLA_DK)
        sv = slice(h * GLA_DV, (h + 1) * GLA_DV)
        q = qa[:, sk]
        k = ka[:, sk]
        v = va[:, sv]
        e = jnp.exp(la_all[:, sk])
        qk = jnp.sum(q * k, axis=-1, keepdims=True)
        e_t = e.T
        k_t = k.T
        qe_t = (q * e).T
        rows = []
        for b in range(nb):
            s = s_ref[b, h]
            vrow = v[b:b + 1]
            rows.append(qk[b:b + 1] * vrow + jnp.sum(s * qe_t[:, b:b + 1], axis=0, keepdims=True))
            so_ref[b, h] = s * e_t[:, b:b + 1] + k_t[:, b:b + 1] * vrow
        o = jnp.concatenate(rows, axis=0)
        o = _rms(o, nw_ref[...]) * _silu(r[:, sv])
        o_ref[:, sv] = o.astype(bf16)


def _gla_step(proj, small, states, prev, l, wgate_pad, bgate, norm_w):
    n = proj.shape[0]
    nb = STEP_B
    st = pl.BlockSpec((None, nb, GLA_H, GLA_DK, GLA_DV), lambda i: (l, i, 0, 0, 0))
    return _state_call(
        _gla_step_kernel, (n // nb,),
        [pl.BlockSpec((nb, GLA_QK), lambda i: (i, _blk(P_Q_B, GLA_QK))),
         pl.BlockSpec((nb, GLA_QK), lambda i: (i, _blk(P_K_B, GLA_QK))),
         pl.BlockSpec((nb, GLA_V), lambda i: (i, _blk(P_V_B, GLA_V))),
         pl.BlockSpec((nb, GLA_V), lambda i: (i, _blk(P_R_B, GLA_V))),
         pl.BlockSpec((nb, LANE), lambda i: (i, 0)),
         pl.BlockSpec((LANE, GLA_QK), lambda i: (0, 0)),
         pl.BlockSpec((1, GLA_QK), lambda i: (0, 0)),
         pl.BlockSpec((1, GLA_DV), lambda i: (0, 0)), st],
        [pl.BlockSpec((nb, GLA_V), lambda i: (i, 0)), st],
        [jax.ShapeDtypeStruct((n, GLA_V), bf16), jax.ShapeDtypeStruct(states.shape, f32)],
        [], ("parallel",), "gla_step",
        (proj, proj, proj, proj, small, wgate_pad, bgate, norm_w, states),
        prev)


def _ssd_step_kernel(alog_ref, dtb_ref, z_ref, xbc_ref, sm_ref, cst_ref, cw_ref, cb_ref, dvec_ref, nw_ref, h_ref,
                     o_ref, ho_ref):
    nb = xbc_ref.shape[0]
    xbc = _silu(_conv_step(xbc_ref, cst_ref, cw_ref) + cb_ref[...])
    sm_t = sm_ref[...].T
    e_mat = _expand_matrix(nb, SSD_N)
    lane_b = lax.broadcasted_iota(jnp.int32, (SSD_P, nb), 1)
    for g in range(SSD_G):
        gs = slice(g * SSD_GW, (g + 1) * SSD_GW)
        bm = xbc[:, SSD_INNER + g * SSD_N:SSD_INNER + (g + 1) * SSD_N]
        cm_t = xbc[:, SSD_INNER + SSD_BC + g * SSD_N:SSD_INNER + SSD_BC + (g + 1) * SSD_N].T
        cb_row = jnp.sum(cm_t * bm.T, axis=0, keepdims=True)
        hg = h_ref[:, pl.ds(g * SSD_HG, SSD_HG)].reshape(nb * SSD_GW, SSD_N)
        ch = _dot(hg.astype(bf16), cm_t.astype(bf16))
        pairs = []
        for jp in range(SSD_HG // 2):
            x_t = xbc[:, g * SSD_GW + jp * LANE:g * SSD_GW + (jp + 1) * LANE].T
            halves = []
            for jj in range(2):
                j = 2 * jp + jj
                hd = g * SSD_HG + j
                dt_row = jax.nn.softplus(sm_t[SM_DT + hd:SM_DT + hd + 1, :] + dtb_ref[hd])
                ea_row = jnp.exp(dt_row * (-jnp.exp(_scalar_vec(alog_ref[hd]))))
                xdt_t = x_t[jj * SSD_P:(jj + 1) * SSD_P] * dt_row
                hi, lo = _split(xdt_t)
                xb = _dot(hi, e_mat) + _dot(lo, e_mat)
                yh = jnp.zeros((SSD_P, nb), f32)
                for b in range(nb):
                    r0 = (b * SSD_HG + j) * SSD_P
                    yh = jnp.where(lane_b == b, ch[r0:r0 + SSD_P], yh)
                    ho_ref[b, hd] = (h_ref[b, hd] * ea_row[:, b:b + 1]
                                     + xb[:, b * SSD_N:(b + 1) * SSD_N] * bm[b:b + 1])
                halves.append(yh * ea_row + cb_row * xdt_t)
            pairs.append(jnp.concatenate(halves, axis=0).T)
        y = jnp.concatenate(pairs, axis=1)
        y = (y + dvec_ref[:, gs] * xbc[:, gs]) * _silu(z_ref[:, gs])
        o_ref[:, gs] = _rms(y, nw_ref[:, gs]).astype(bf16)


def _ssd_step(proj, small, conv_t, states, prev, l, conv_w, conv_b, a_log, dt_bias, dvec, norm_w):
    n = proj.shape[0]
    nb = STEP_B
    smem = pl.BlockSpec(memory_space=pltpu.SMEM)
    const = lambda shape: pl.BlockSpec(shape, lambda i: (0,) * len(shape))
    st = pl.BlockSpec((None, nb, SSD_H, SSD_P, SSD_N), lambda i: (l, i, 0, 0, 0))
    return _state_call(
        _ssd_step_kernel, (n // nb,),
        [smem, smem,
         pl.BlockSpec((nb, SSD_INNER), lambda i: (i, _blk(P_Z_C, SSD_INNER))),
         pl.BlockSpec((nb, SSD_CONV), lambda i: (i, _blk(P_XBC, SSD_CONV))),
         pl.BlockSpec((nb, LANE), lambda i: (i, 0)),
         pl.BlockSpec((CONV_W - 1, nb, SSD_CONV), lambda i: (0, i, 0)),
         const((CONV_W, SSD_CONV)), const((1, SSD_CONV)), const((1, SSD_INNER)), const((1, SSD_INNER)), st],
        [pl.BlockSpec((nb, SSD_INNER), lambda i: (i, 0)), st],
        [jax.ShapeDtypeStruct((n, SSD_INNER), bf16), jax.ShapeDtypeStruct(states.shape, f32)],
        [], ("parallel",), "ssd_step",
        (a_log, dt_bias, proj, proj, small, conv_t, conv_w, conv_b, dvec, norm_w, states),
        prev)


def _mixout_kernel(oa_ref, ob_ref, oc_ref, ga_ref, gb_ref, gc_ref, wa_ref, wb_ref, wc_ref, wo_ref,
                   x_ref, gt_ref, o_ref, acc_ref):
    j = pl.program_id(2)
    m = (_sigmoid(ga_ref[...].astype(f32)) * _dot(oa_ref[...], wa_ref[...])
         + _sigmoid(gb_ref[...].astype(f32)) * _dot(ob_ref[...], wb_ref[...])
         + _sigmoid(gc_ref[...].astype(f32)) * _dot(oc_ref[...], wc_ref[...]))
    part = _dot(m.astype(bf16), wo_ref[...])

    @pl.when(j == 0)
    def _():
        acc_ref[...] = part

    @pl.when(j > 0)
    def _():
        acc_ref[...] += part

    @pl.when(j == pl.num_programs(2) - 1)
    def _():
        o_ref[...] = x_ref[...] + gt_ref[...] * acc_ref[...]


def _mixout(oa, ob, oc, proj, x, mod, wa, wb, wc, wo, l, per_row, tm):
    bx, tx, dm = x.shape
    tn = 512
    gate = lambda k: pl.BlockSpec((None, tm, tn), lambda b, i, j: (b, i, _blk(P_GATES + k * D_MODEL, tn) + j))
    act = lambda w: pl.BlockSpec((None, tm, w), lambda b, i, j: (b, i, 0))
    wsp = lambda w: pl.BlockSpec((None, w, tn), lambda b, i, j: (l, 0, j))
    return pl.pallas_call(
        _mixout_kernel,
        grid=(bx, tx // tm, dm // tn),
        in_specs=[act(GDN_V), act(GLA_V), act(SSD_INNER), gate(0), gate(1), gate(2),
                  wsp(GDN_V), wsp(GLA_V), wsp(SSD_INNER),
                  pl.BlockSpec((None, tn, dm), lambda b, i, j: (l, j, 0)),
                  pl.BlockSpec((None, tm, dm), lambda b, i, j: (b, i, 0)),
                  _mod_spec(per_row, tm, 5)],
        out_specs=pl.BlockSpec((None, tm, dm), lambda b, i, j: (b, i, 0)),
        out_shape=jax.ShapeDtypeStruct(x.shape, f32),
        scratch_shapes=[pltpu.VMEM((tm, dm), f32)],
        compiler_params=_cp(("parallel", "parallel", "arbitrary")),
        name="mixout",
    )(oa, ob, oc, proj, proj, proj, wa, wb, wc, wo, x, mod)


def _permute_w_in(w):
    nl, d, _ = w.shape
    wt = jnp.swapaxes(w, 1, 2)
    rows = lambda name: wt[:, W_IN_SRC[name][0]:W_IN_SRC[name][1]]
    main = jnp.concatenate([rows(name) for name in MAIN_ORDER], axis=1).astype(bf16)
    small = jnp.concatenate([rows(name) for name in SMALL_ORDER], axis=1)
    small = jnp.pad(small, ((0, 0), (0, LANE - small.shape[1]), (0, 0))).astype(bf16)
    return main, small


def _ffn_weights(wg, wu, wd):
    f = wg.shape[2]
    cut = f - f % FF_TILE
    w = tuple(a.astype(bf16) for a in (wg, wu, wd))
    if cut == f:
        return w
    return w + (w[0][:, :, cut:], w[1][:, :, cut:], w[2][:, cut:, :])


def _lane_vec(v, lo):
    return jnp.zeros((1, LANE), f32).at[0, lo:lo + v.shape[0]].set(v)


def _layer_params(l, p):
    row = lambda a: a[l].reshape(1, -1)
    wgate = jnp.zeros((LANE, GLA_QK), f32).at[SM_LR:SM_LR + GLA_RANK].set(p["gla_w_gate"][l]).astype(bf16)
    return dict(
        norm1=row(p["norm1"]), norm2=row(p["norm2"]), norm3=row(p["norm3"]),
        gdn_conv_w=p["gdn_conv_w"][l], gdn_a_log=p["gdn_a_log"][l], gdn_dt_bias=p["gdn_dt_bias"][l],
        gdn_alog_l=_lane_vec(p["gdn_a_log"][l], SM_DEC), gdn_dtb_l=_lane_vec(p["gdn_dt_bias"][l], SM_DEC),
        gdn_norm_w=row(p["gdn_norm_w"]),
        gla_wgate=wgate, gla_bgate=row(p["gla_b_gate"]), gla_norm_w=row(p["gla_norm_w"]),
        ssd_conv_w=p["ssd_conv_w"][l], ssd_conv_b=row(p["ssd_conv_b"]), ssd_a_log=p["ssd_a_log"][l],
        ssd_dt_bias=p["ssd_dt_bias"][l],
        ssd_avec_l=_lane_vec(-jnp.exp(p["ssd_a_log"][l]), SM_DT), ssd_dtb_l=_lane_vec(p["ssd_dt_bias"][l], SM_DT),
        ssd_dvec=jnp.repeat(p["ssd_d"][l], SSD_P).reshape(1, -1),
        ssd_norm_w=row(p["ssd_norm_w"]),
    )


def _stacked_weights(p):
    w_in, w_in_small = _permute_w_in(p["w_in"])
    return dict(
        f1=_ffn_weights(p["ffn1_wg"], p["ffn1_wu"], p["ffn1_wd"]),
        f2=_ffn_weights(p["ffn2_wg"], p["ffn2_wu"], p["ffn2_wd"]),
        w_in=w_in, w_in_small=w_in_small,
        wa=p["w_branch_gdn"].astype(bf16), wb=p["w_branch_gla"].astype(bf16),
        wc=p["w_branch_ssd"].astype(bf16), w_out=p["w_out"].astype(bf16),
    )


def _new_conv_state(buf, raw):
    t = raw.shape[1]
    k = CONV_W - 1
    if t >= k:
        return raw[:, t - k:]
    return jnp.concatenate([buf[:, t:], raw], axis=1)


def _mixer_prompt(proj, small, lp, st):
    gdn_conv, s_gdn, s_gla, ssd_conv, s_ssd = st
    oa, s_gdn_n = _gdn_prompt(proj, small, gdn_conv, s_gdn, lp["gdn_conv_w"], lp["gdn_alog_l"], lp["gdn_dtb_l"],
                              lp["gdn_norm_w"])
    ob, s_gla_n = _gla_prompt(proj, small, s_gla, lp["gla_wgate"], lp["gla_bgate"], lp["gla_norm_w"])
    oc, s_ssd_n = _ssd_prompt(proj, small, ssd_conv, s_ssd, lp["ssd_conv_w"], lp["ssd_conv_b"], lp["ssd_avec_l"],
                              lp["ssd_dtb_l"], lp["ssd_dvec"], lp["ssd_norm_w"])
    gdn_conv_n = _new_conv_state(gdn_conv, proj[:, :, P_QKV_A:P_QKV_A + GDN_CONV]).astype(f32)
    ssd_conv_n = _new_conv_state(ssd_conv, proj[:, :, P_XBC:P_XBC + SSD_CONV]).astype(f32)
    return (oa, ob, oc), (gdn_conv_n, s_gdn_n, s_gla_n, ssd_conv_n, s_ssd_n)


def _mixer_sample(proj, small, lp, l, states, prev):
    n = proj.shape[1]
    p2 = proj.reshape(n, P_TOTAL)
    s2 = small.reshape(n, LANE)
    gdn_conv, ssd_conv = states[0][l], states[3][l]
    gct = jnp.swapaxes(gdn_conv, 0, 1)
    sct = jnp.swapaxes(ssd_conv, 0, 1)
    pv = (None,) * 5 if prev is None else prev
    oa, s_gdn_n = _gdn_step(p2, s2, gct, states[1], pv[1], l, lp["gdn_conv_w"], lp["gdn_a_log"],
                            lp["gdn_dt_bias"], lp["gdn_norm_w"])
    ob, s_gla_n = _gla_step(p2, s2, states[2], pv[2], l, lp["gla_wgate"], lp["gla_bgate"], lp["gla_norm_w"])
    oc, s_ssd_n = _ssd_step(p2, s2, sct, states[4], pv[4], l, lp["ssd_conv_w"], lp["ssd_conv_b"], lp["ssd_a_log"],
                            lp["ssd_dt_bias"], lp["ssd_dvec"], lp["ssd_norm_w"])
    raw = p2.reshape(n, 1, P_TOTAL)
    gdn_conv_n = _new_conv_state(gdn_conv, raw[:, :, P_QKV_A:P_QKV_A + GDN_CONV])
    ssd_conv_n = _new_conv_state(ssd_conv, raw[:, :, P_XBC:P_XBC + SSD_CONV])
    outs = tuple(o.reshape(1, n, -1) for o in (oa, ob, oc))
    return outs, (gdn_conv_n, s_gdn_n, s_gla_n, ssd_conv_n, s_ssd_n)


def _trunk(x, mods, lps, sw, states, per_row, tm, tm_in, final_w):
    nl = len(lps)
    per_layer = []
    prev = None
    for l in range(nl):
        lp, mod = lps[l], mods[l]
        last = l == nl - 1
        x = _ffn(x, mod, 0, lp["norm1"], sw["f1"], l, lp["norm1"], per_row, tm, False)
        proj, small = _inproj(x, mod, lp["norm2"], sw["w_in"], sw["w_in_small"], l, per_row, tm_in,
                              f32 if per_row else bf16)
        if per_row:
            (oa, ob, oc), st = _mixer_sample(proj, small, lp, l, states, prev)
            prev = st
        else:
            (oa, ob, oc), st = _mixer_prompt(proj, small, lp, tuple(s[l] for s in states))
        per_layer.append(st)
        x = _mixout(oa, ob, oc, proj, x, mod, sw["wa"], sw["wb"], sw["wc"], sw["w_out"], l, per_row, tm)
        x = _ffn(x, mod, 6, lp["norm3"], sw["f2"], l, final_w if last else lp["norm3"], per_row, tm, last)
    stack = lambda i: jnp.stack([st[i] for st in per_layer])
    if per_row:
        new_states = (stack(0), prev[1], prev[2], stack(3), prev[4])
    else:
        new_states = tuple(stack(i) for i in range(5))
    return x, new_states


def kernel(x_prompt, x_sample, state_gdn_conv, state_gdn, state_gla, state_ssd_conv, state_ssd, c_prompt, c_sample, w_ada, b_ada, norm1, norm2, norm3, ffn1_wg, ffn1_wu, ffn1_wd, ffn2_wg, ffn2_wu, ffn2_wd, w_in, gdn_conv_w, gdn_a_log, gdn_dt_bias, gdn_norm_w, gla_w_gate, gla_b_gate, gla_norm_w, ssd_conv_w, ssd_conv_b, ssd_a_log, ssd_dt_bias, ssd_d, ssd_norm_w, w_branch_gdn, w_branch_gla, w_branch_ssd, w_out, final_norm):
    p = dict(norm1=norm1, norm2=norm2, norm3=norm3,
             ffn1_wg=ffn1_wg, ffn1_wu=ffn1_wu, ffn1_wd=ffn1_wd, ffn2_wg=ffn2_wg, ffn2_wu=ffn2_wu, ffn2_wd=ffn2_wd,
             w_in=w_in, gdn_conv_w=gdn_conv_w, gdn_a_log=gdn_a_log, gdn_dt_bias=gdn_dt_bias, gdn_norm_w=gdn_norm_w,
             gla_w_gate=gla_w_gate, gla_b_gate=gla_b_gate, gla_norm_w=gla_norm_w,
             ssd_conv_w=ssd_conv_w, ssd_conv_b=ssd_conv_b, ssd_a_log=ssd_a_log, ssd_dt_bias=ssd_dt_bias,
             ssd_d=ssd_d, ssd_norm_w=ssd_norm_w,
             w_branch_gdn=w_branch_gdn, w_branch_gla=w_branch_gla, w_branch_ssd=w_branch_ssd, w_out=w_out)
    nl = w_ada.shape[0]
    bp, tp, dm = x_prompt.shape
    bs = x_sample.shape[0]
    assert x_sample.shape[1] == 1 and tp % CHUNK == 0 and bs % STEP_B == 0 and dm == D_MODEL
    lps = [_layer_params(l, p) for l in range(nl)]
    sw = _stacked_weights(p)
    fw = final_norm.reshape(1, dm)

    rows = bp + bs
    rpad = -(-rows // 8) * 8
    c_all = jnp.concatenate([c_prompt, c_sample, jnp.zeros((rpad - rows, dm), f32)], axis=0)
    mod = _ada_mod(c_all, w_ada, b_ada)
    mod_p = [mod[l, :bp].reshape(bp, 1, N_MOD * dm) for l in range(nl)]
    mod_s = [mod[l, bp:rows].reshape(1, bs, N_MOD * dm) for l in range(nl)]

    sample_states = (state_gdn_conv, state_gdn, state_gla, state_ssd_conv, state_ssd)
    prompt_states = tuple(jnp.zeros((s.shape[0], bp) + s.shape[2:], x_prompt.dtype) for s in sample_states)
    tm_p = 512 if tp % 512 == 0 else CHUNK
    tm_in = 1024 if tp % 1024 == 0 else tm_p
    y_p, st_p = _trunk(x_prompt, mod_p, lps, sw, prompt_states, False, tm_p, tm_in, fw)
    y_s, st_s = _trunk(x_sample.reshape(1, bs, dm), mod_s, lps, sw, sample_states, True, bs, bs, fw)
    return (y_p, y_s.reshape(bs, 1, dm)) + st_p + st_s
```

```python
import functools

import jax
import jax.numpy as jnp
from jax import lax
from jax.experimental import pallas as pl
from jax.experimental.pallas import tpu as pltpu

f32 = jnp.float32
bf16 = jnp.bfloat16
HI = lax.Precision.HIGHEST

EPS = 1e-6
D_MODEL = 2048
N_MOD = 9
CHUNK = 64
CONV_W = 4
GDN_H, GDN_DK, GDN_DV = 8, 128, 128
GLA_H, GLA_DK, GLA_DV, GLA_RANK, GLA_TAU = 4, 128, 256, 16, 16.0
SSD_H, SSD_P, SSD_G, SSD_N = 32, 64, 4, 128
SSD_HG = SSD_H // SSD_G
GDN_QK = GDN_H * GDN_DK
GDN_V = GDN_H * GDN_DV
GDN_CONV = 2 * GDN_QK + GDN_V
GLA_QK = GLA_H * GLA_DK
GLA_V = GLA_H * GLA_DV
SSD_INNER = SSD_H * SSD_P
SSD_BC = SSD_G * SSD_N
SSD_CONV = SSD_INNER + 2 * SSD_BC
SSD_GW = SSD_HG * SSD_P

LANE = 128
FF_TILE = 512

IN_SPLITS = (("qkv_a", GDN_CONV), ("z_a", GDN_V), ("beta", GDN_H), ("dec", GDN_H),
             ("q_b", GLA_QK), ("k_b", GLA_QK), ("v_b", GLA_V), ("lr", GLA_RANK), ("r_b", GLA_V),
             ("z_c", SSD_INNER), ("xbc", SSD_CONV), ("dt", SSD_H), ("gates", 3 * D_MODEL))
MAIN_ORDER = ("qkv_a", "xbc", "z_c", "gates", "z_a", "q_b", "k_b", "v_b", "r_b")
SMALL_ORDER = ("beta", "dec", "lr", "dt")


def _layout():
    src, off = {}, 0
    for name, w in IN_SPLITS:
        src[name] = (off, off + w)
        off += w
    main, small, d = {}, {}, 0
    for name in MAIN_ORDER:
        w = src[name][1] - src[name][0]
        main[name] = d
        d += w
    total = d
    d = 0
    for name in SMALL_ORDER:
        small[name] = d
        d += src[name][1] - src[name][0]
    assert d <= LANE
    return src, main, small, total


W_IN_SRC, P_MAIN, P_SM, P_TOTAL = _layout()
P_QKV_A, P_XBC, P_Z_C, P_GATES, P_Z_A = (P_MAIN[k] for k in ("qkv_a", "xbc", "z_c", "gates", "z_a"))
P_Q_B, P_K_B, P_V_B, P_R_B = (P_MAIN[k] for k in ("q_b", "k_b", "v_b", "r_b"))
SM_BETA, SM_DEC, SM_LR, SM_DT = (P_SM[k] for k in SMALL_ORDER)

VMEM_LIMIT = 56 * 1024 * 1024


def _cp(sem):
    return pltpu.CompilerParams(dimension_semantics=sem, vmem_limit_bytes=VMEM_LIMIT)


def _blk(off, width):
    assert off % width == 0, (off, width)
    return off // width


def _sigmoid(x):
    return jax.nn.sigmoid(x)


def _silu(x):
    return x * jax.nn.sigmoid(x)


def _rms(x, w):
    return x * lax.rsqrt(jnp.mean(x * x, axis=-1, keepdims=True) + EPS) * w


def _dot(a, b):
    return jnp.dot(a, b, preferred_element_type=f32)


def _dot_nt(a, b):
    return lax.dot_general(a, b, (((1,), (1,)), ((), ())), preferred_element_type=f32)


def _dot_tn(a, b):
    return lax.dot_general(a, b, (((0,), (0,)), ((), ())), preferred_element_type=f32)


def _split(x):
    hi = x.astype(bf16)
    return hi, (x - hi.astype(f32)).astype(bf16)


def _dot3(a, b):
    return _dot(a[0], b[0]) + (_dot(a[0], b[1]) + _dot(a[1], b[0]))


def _dot_hi(a, b):
    return jnp.dot(a, b, precision=HI, preferred_element_type=f32)


def _tri_masks(c):
    row = lax.broadcasted_iota(jnp.int32, (c, c), 0)
    col = lax.broadcasted_iota(jnp.int32, (c, c), 1)
    return row >= col, row > col, row == col


def _lane_col(x, idx):
    lane = lax.broadcasted_iota(jnp.int32, x.shape, 1)
    return jnp.sum(jnp.where(lane == idx, x, 0.0), axis=1, keepdims=True)


def _scalar_vec(s):
    return jnp.full((1, 1), s, f32)


def _ada_kernel(c_ref, w_ref, b_ref, o_ref):
    s = _silu(c_ref[...]).astype(bf16)
    o_ref[...] = _dot(s, w_ref[...].astype(bf16)) + b_ref[...]


def _ada_mod(c_all, w_ada, b_ada):
    nl, dm, n = w_ada.shape
    r = c_all.shape[0]
    tn = 1024
    return pl.pallas_call(
        _ada_kernel,
        grid=(nl, n // tn),
        in_specs=[pl.BlockSpec((r, dm), lambda l, j: (0, 0)),
                  pl.BlockSpec((None, dm, tn), lambda l, j: (l, 0, j)),
                  pl.BlockSpec((None, 1, tn), lambda l, j: (l, 0, j))],
        out_specs=pl.BlockSpec((None, r, tn), lambda l, j: (l, 0, j)),
        out_shape=jax.ShapeDtypeStruct((nl, r, n), f32),
        compiler_params=_cp(("arbitrary", "arbitrary")),
        name="ada_mod",
    )(c_all, w_ada, b_ada.reshape(nl, 1, n))


def _mod_spec(per_row, tm, chunk):
    if per_row:
        return pl.BlockSpec((None, tm, D_MODEL), lambda b, i, j: (b, i, chunk))
    return pl.BlockSpec((None, 1, D_MODEL), lambda b, i, j: (b, 0, chunk))


def _ffn_kernel(x_ref, sh_ref, sc_ref, gt_ref, nw_ref, fw_ref, wg_ref, wu_ref, wd_ref, *rest, final, has_tail):
    if has_tail:
        wgt_ref, wut_ref, wdt_ref, o_ref, h_ref, acc_ref = rest
    else:
        o_ref, h_ref, acc_ref = rest
    f = pl.program_id(2)
    last = pl.num_programs(2) - 1

    @pl.when(f == 0)
    def _():
        y = _rms(x_ref[...], nw_ref[...])
        h_ref[...] = (y * (1.0 + sc_ref[...]) + sh_ref[...]).astype(bf16)
        acc_ref[...] = jnp.zeros_like(acc_ref)

    def accumulate(wg, wu, wd):
        h = h_ref[...]
        a = (_silu(_dot(h, wg[...])) * _dot(h, wu[...])).astype(bf16)
        acc_ref[...] += _dot(a, wd[...])

    if has_tail:
        pl.when(f < last)(lambda: accumulate(wg_ref, wu_ref, wd_ref))
        pl.when(f == last)(lambda: accumulate(wgt_ref, wut_ref, wdt_ref))
    else:
        accumulate(wg_ref, wu_ref, wd_ref)

    @pl.when(f == last)
    def _():
        y = x_ref[...] + 0.5 * gt_ref[...] * acc_ref[...]
        if final:
            y = _rms(y, fw_ref[...])
        o_ref[...] = y


def _ffn(x, mod, k0, nw, w, l, fw, per_row, tm, final):
    bx, tx, dm = x.shape
    tf = FF_TILE
    n_full = w[0].shape[2] // tf
    has_tail = len(w) > 3
    full = lambda f: jnp.minimum(f, n_full - 1)
    w_specs = [pl.BlockSpec((None, dm, tf), lambda b, i, f: (l, 0, full(f))),
               pl.BlockSpec((None, dm, tf), lambda b, i, f: (l, 0, full(f))),
               pl.BlockSpec((None, tf, dm), lambda b, i, f: (l, full(f), 0))]
    if has_tail:
        ft = w[3].shape[2]
        w_specs += [pl.BlockSpec((None, dm, ft), lambda b, i, f: (l, 0, 0)),
                    pl.BlockSpec((None, dm, ft), lambda b, i, f: (l, 0, 0)),
                    pl.BlockSpec((None, ft, dm), lambda b, i, f: (l, 0, 0))]
    return pl.pallas_call(
        functools.partial(_ffn_kernel, final=final, has_tail=has_tail),
        grid=(bx, tx // tm, n_full + int(has_tail)),
        in_specs=[pl.BlockSpec((None, tm, dm), lambda b, i, f: (b, i, 0)),
                  _mod_spec(per_row, tm, k0), _mod_spec(per_row, tm, k0 + 1), _mod_spec(per_row, tm, k0 + 2),
                  pl.BlockSpec((1, dm), lambda b, i, f: (0, 0)),
                  pl.BlockSpec((1, dm), lambda b, i, f: (0, 0))] + w_specs,
        out_specs=pl.BlockSpec((None, tm, dm), lambda b, i, f: (b, i, 0)),
        out_shape=jax.ShapeDtypeStruct(x.shape, f32),
        scratch_shapes=[pltpu.VMEM((tm, dm), bf16), pltpu.VMEM((tm, dm), f32)],
        compiler_params=_cp(("parallel", "parallel", "arbitrary")),
        name="ffn",
    )(x, mod, mod, mod, nw, fw, *w)


def _inproj_kernel(x_ref, sh_ref, sc_ref, nw_ref, w_ref, ws_ref, o_ref, os_ref, h_ref):
    @pl.when(pl.program_id(2) == 0)
    def _():
        y = _rms(x_ref[...], nw_ref[...])
        h = (y * (1.0 + sc_ref[...]) + sh_ref[...]).astype(bf16)
        h_ref[...] = h
        os_ref[...] = _dot_nt(h, ws_ref[...])

    o_ref[...] = _dot_nt(h_ref[...], w_ref[...]).astype(o_ref.dtype)


def _inproj(x, mod, nw, w, ws, l, per_row, tm, out_dtype):
    bx, tx, dm = x.shape
    n = w.shape[1]
    tn = 1024
    return pl.pallas_call(
        _inproj_kernel,
        grid=(bx, tx // tm, n // tn),
        in_specs=[pl.BlockSpec((None, tm, dm), lambda b, i, j: (b, i, 0)),
                  _mod_spec(per_row, tm, 3), _mod_spec(per_row, tm, 4),
                  pl.BlockSpec((1, dm), lambda b, i, j: (0, 0)),
                  pl.BlockSpec((None, tn, dm), lambda b, i, j: (l, j, 0)),
                  pl.BlockSpec((None, LANE, dm), lambda b, i, j: (l, 0, 0))],
        out_specs=[pl.BlockSpec((None, tm, tn), lambda b, i, j: (b, i, j)),
                   pl.BlockSpec((None, tm, LANE), lambda b, i, j: (b, i, 0))],
        out_shape=[jax.ShapeDtypeStruct((bx, tx, n), out_dtype),
                   jax.ShapeDtypeStruct((bx, tx, LANE), f32)],
        scratch_shapes=[pltpu.VMEM((tm, dm), bf16)],
        compiler_params=_cp(("parallel", "parallel", "arbitrary")),
        name="inproj",
    )(x, mod, mod, nw, w, ws)


def _conv_load(ext_ref, x_ref, cst_ref, first):
    c = x_ref.shape[0]

    @pl.when(first)
    def _():
        ext_ref[pl.ds(5, CONV_W - 1), :] = cst_ref[...]

    ext_ref[pl.ds(8, c), :] = x_ref[...].astype(f32)


def _conv_cols(ext_ref, w_ref, b_ref, lo, width, c):
    sl = pl.ds(lo, width)
    w = w_ref[:, sl]
    y = ext_ref[pl.ds(5, c), sl] * w[0:1]
    for j in range(1, CONV_W):
        y = y + ext_ref[pl.ds(5 + j, c), sl] * w[j:j + 1]
    if b_ref is not None:
        y = y + b_ref[:, sl]
    return _silu(y)


def _conv_carry(ext_ref, c):
    ext_ref[pl.ds(5, CONV_W - 1), :] = ext_ref[pl.ds(8 + c - (CONV_W - 1), CONV_W - 1), :]


def _cumsum_rows(x, tri):
    return _dot_hi(jnp.where(tri, 1.0, 0.0).astype(f32), x)


def _gdn_kernel(qkv_ref, z_ref, sm_ref, cst_ref, cw_ref, alog_ref, dtb_ref, nw_ref, s0_ref,
                o_ref, s_ref, ext_ref):
    c = qkv_ref.shape[0]
    first = pl.program_id(1) == 0

    @pl.when(first)
    def _():
        s_ref[...] = s0_ref[...]

    _conv_load(ext_ref, qkv_ref, cst_ref, first)

    sm = sm_ref[...]
    beta_l = _sigmoid(sm)
    g_l = -jnp.exp(alog_ref[...]) * jax.nn.softplus(sm + dtb_ref[...])
    tri, strict, eye = _tri_masks(c)
    gc_all = _cumsum_rows(g_l, tri)
    gc_t = gc_all.T
    eye_f = jnp.where(eye, 1.0, 0.0).astype(f32)
    n_fac = max(1, (c - 1).bit_length() - 1)

    hs = range(GDN_H)
    qb, kb, ks, decay, eg, gcs, nm, rhs = [], [], [], [], [], [], [], []
    for h in hs:
        q = _conv_cols(ext_ref, cw_ref, None, h * GDN_DK, GDN_DK, c)
        k = _conv_cols(ext_ref, cw_ref, None, GDN_QK + h * GDN_DK, GDN_DK, c)
        v = _conv_cols(ext_ref, cw_ref, None, 2 * GDN_QK + h * GDN_DV, GDN_DV, c)
        q = q * lax.rsqrt(jnp.sum(q * q, axis=-1, keepdims=True) + EPS) * (GDN_DK ** -0.5)
        k = k * lax.rsqrt(jnp.sum(k * k, axis=-1, keepdims=True) + EPS)
        beta = beta_l[:, SM_BETA + h:SM_BETA + h + 1]
        gc = gc_all[:, SM_DEC + h:SM_DEC + h + 1]
        d = jnp.exp(jnp.where(tri, gc - gc_t[SM_DEC + h:SM_DEC + h + 1, :], -jnp.inf))
        e = jnp.exp(gc)
        qb.append(q.astype(bf16))
        kb.append(k.astype(bf16))
        ks.append(k)
        decay.append(d)
        eg.append(e)
        gcs.append(gc)
        nm.append(jnp.where(strict, beta * d * _dot_nt(kb[h], kb[h]), 0.0))
        rhs.append(jnp.concatenate([v * beta, k * (beta * e)], axis=1))

    inv = [eye_f - nm[h] for h in hs]
    pw_s = [_split(nm[h]) for h in hs]
    pw_s = [_split(_dot3(pw_s[h], pw_s[h])) for h in hs]
    for i in range(n_fac):
        inv = [inv[h] + _dot3(_split(inv[h]), pw_s[h]) for h in hs]
        if i + 1 < n_fac:
            pw_s = [_split(_dot3(pw_s[h], pw_s[h])) for h in hs]
    sol = []
    for h in hs:
        inv_hi, inv_lo = _split(inv[h])
        rb = rhs[h].astype(bf16)
        sol.append(_dot(inv_hi, rb) + _dot(inv_lo, rb))

    s = [s_ref[h] for h in hs]
    sb = [s[h].astype(bf16) for h in hs]
    wb = [(sol[h][:, 0:GDN_DV] - _dot(sol[h][:, GDN_DV:].astype(bf16), sb[h])).astype(bf16) for h in hs]
    qk = [(_dot_nt(qb[h], kb[h]) * decay[h]).astype(bf16) for h in hs]
    o = [_dot(qb[h], sb[h]) * eg[h] + _dot(qk[h], wb[h]) for h in hs]
    for h in hs:
        g_last = gcs[h][c - 1:c]
        kd = (ks[h] * jnp.exp(g_last - gcs[h])).astype(bf16)
        s_ref[h] = s[h] * jnp.exp(g_last) + _dot_tn(kd, wb[h])
    for h in hs:
        sl = slice(h * GDN_DV, (h + 1) * GDN_DV)
        o_ref[:, sl] = (_rms(o[h], nw_ref[...]) * _silu(z_ref[:, sl].astype(f32))).astype(bf16)

    _conv_carry(ext_ref, c)


def _tok(width, off):
    return pl.BlockSpec((None, CHUNK, width), lambda bi, ti: (bi, ti, _blk(off, width)))


def _per_seq(*shape):
    return pl.BlockSpec((None,) + shape, lambda bi, ti: (bi,) + (0,) * len(shape))


def _const(*shape):
    return pl.BlockSpec(shape, lambda bi, ti: (0,) * len(shape))


N_GDN_IN, N_GLA_IN, N_SSD_IN = 9, 9, 11


def _mixers_kernel(*refs):
    i0, i1, i2 = N_GDN_IN, N_GDN_IN + N_GLA_IN, N_GDN_IN + N_GLA_IN + N_SSD_IN
    gdn_in, gla_in, ssd_in = refs[:i0], refs[i0:i1], refs[i1:i2]
    oa_ref, sa_ref, ob_ref, sb_ref, oc_ref, sc_ref, ext_a_ref, st_b_ref, ext_c_ref = refs[i2:]
    _gdn_kernel(*gdn_in, oa_ref, sa_ref, ext_a_ref)
    _gla_kernel(*gla_in, ob_ref, sb_ref, st_b_ref)
    _ssd_kernel(*ssd_in, oc_ref, sc_ref, ext_c_ref)


GLA_SUB = 16


def _gla_kernel(q_ref, k_ref, v_ref, r_ref, sm_ref, wgate_ref, bgate_ref, nw_ref, s0_ref,
                o_ref, so_ref, st_ref):
    c = q_ref.shape[0]
    ti = pl.program_id(1)

    @pl.when(ti == 0)
    def _():
        for h in range(GLA_H):
            st_ref[h] = s0_ref[h].T

    sm = sm_ref[...]
    lane = lax.broadcasted_iota(jnp.int32, sm.shape, 1)
    lr = jnp.where((lane >= SM_LR) & (lane < SM_LR + GLA_RANK), sm, 0.0).astype(bf16)
    la = jax.nn.log_sigmoid(_dot(lr, wgate_ref[...]) + bgate_ref[...]) / GLA_TAU
    tri, _, _ = _tri_masks(c)
    b_all = _cumsum_rows(la, tri)

    col = lax.broadcasted_iota(jnp.int32, (GLA_SUB, c), 1)
    row = lax.broadcasted_iota(jnp.int32, (GLA_SUB, c), 0)
    hs = range(GLA_H)
    qs, ks, vbs, bs, atts = [], [], [], [], []
    for h in hs:
        sk = slice(h * GLA_DK, (h + 1) * GLA_DK)
        q = q_ref[:, sk].astype(f32) * (GLA_DK ** -0.5)
        k = k_ref[:, sk].astype(f32)
        b = b_all[:, sk]
        qs.append(q)
        ks.append(k)
        bs.append(b)
        vbs.append(v_ref[:, h * GLA_DV:(h + 1) * GLA_DV].astype(bf16))
        att_rows = []
        for i in range(c // GLA_SUB):
            lo = i * GLA_SUB
            b_i = b[lo:lo + GLA_SUB]
            q_i = q[lo:lo + GLA_SUB]
            b_top = b[lo:lo + 1]
            att = jnp.zeros((GLA_SUB, c), f32)
            if i > 0:
                q_t = (q_i * jnp.exp(b_i - b_top)).astype(bf16)
                k_t = (k * jnp.exp(jnp.minimum(b_top - b, 0.0))).astype(bf16)
                att = jnp.where(col < lo, _dot_nt(q_t, k_t), 0.0)
            for sl in range(GLA_SUB):
                sidx = lo + sl
                e = jnp.exp(jnp.minimum(b_i - b[sidx:sidx + 1], 0.0))
                p = jnp.sum(q_i * k[sidx:sidx + 1] * e, axis=1, keepdims=True)
                att = jnp.where((col == sidx) & (row >= sl), p, att)
            att_rows.append(att)
        atts.append(jnp.concatenate(att_rows, axis=0).astype(bf16))

    sts = [st_ref[h] for h in hs]
    os_ = [_dot(atts[h], vbs[h]) + _dot_nt((qs[h] * jnp.exp(bs[h])).astype(bf16), sts[h].astype(bf16))
           for h in hs]
    st_new = []
    for h in hs:
        b_last = bs[h][c - 1:c]
        kd = (ks[h] * jnp.exp(b_last - bs[h])).astype(bf16)
        st_new.append(sts[h] * jnp.exp(b_last) + _dot_tn(vbs[h], kd))
        st_ref[h] = st_new[h]

    @pl.when(ti == pl.num_programs(1) - 1)
    def _():
        for h in hs:
            so_ref[h] = st_new[h].T

    for h in hs:
        sv = slice(h * GLA_DV, (h + 1) * GLA_DV)
        o_ref[:, sv] = (_rms(os_[h], nw_ref[...]) * _silu(r_ref[:, sv].astype(f32))).astype(bf16)


def _ssd_kernel(z_ref, xbc_ref, sm_ref, cst_ref, cw_ref, cb_ref, avec_ref, dtb_ref, dvec_ref, nw_ref, h0_ref,
                o_ref, h_ref, ext_ref):
    c = xbc_ref.shape[0]
    first = pl.program_id(1) == 0

    @pl.when(first)
    def _():
        h_ref[...] = h0_ref[...]

    _conv_load(ext_ref, xbc_ref, cst_ref, first)

    dt_l = jax.nn.softplus(sm_ref[...] + dtb_ref[...])
    tri, _, _ = _tri_masks(c)
    ac_all = _cumsum_rows(dt_l * avec_ref[...], tri)
    ac_t2 = jnp.concatenate([ac_all, ac_all], axis=0).T

    lane = lax.broadcasted_iota(jnp.int32, (c, 2 * SSD_P), 1)
    rowi = lax.broadcasted_iota(jnp.int32, (c, 2 * SSD_P), 0)
    lo_half = lane < SSD_P
    tri2 = rowi >= jnp.where(lo_half, lane, lane - SSD_P)
    lane1 = lane[0:1]
    gs_ = range(SSD_G)
    ps_ = range(SSD_HG // 2)

    xs_, bmb, hg, cb2, ch = [], [], [], [], []
    for g in gs_:
        xs_.append(_conv_cols(ext_ref, cw_ref, cb_ref, g * SSD_GW, SSD_GW, c))
        bmb.append(_conv_cols(ext_ref, cw_ref, cb_ref, SSD_INNER + g * SSD_N, SSD_N, c).astype(bf16))
        cmb = _conv_cols(ext_ref, cw_ref, cb_ref, SSD_INNER + SSD_BC + g * SSD_N, SSD_N, c).astype(bf16)
        hg.append(h_ref[pl.ds(g * SSD_HG, SSD_HG)].reshape(SSD_GW, SSD_N))
        cb2.append(_dot_nt(cmb, jnp.concatenate([bmb[g], bmb[g]], axis=0)))
        ch.append(_dot_nt(cmb, hg[g].astype(bf16)))

    m2, rhs, e_col, xsc = {}, {}, {}, {}
    for g in gs_:
        for p in ps_:
            l0 = SM_DT + g * SSD_HG + 2 * p
            ps = slice(p * 2 * SSD_P, (p + 1) * 2 * SSD_P)
            ac_col = jnp.where(lo_half, ac_all[:, l0:l0 + 1], ac_all[:, l0 + 1:l0 + 2])
            ac_row = jnp.where(lane1 < SSD_P, ac_t2[l0:l0 + 1, :], ac_t2[l0 + 1:l0 + 2, :])
            decay = jnp.exp(jnp.where(tri2, ac_col - ac_row, -jnp.inf))
            m2[g, p] = (cb2[g] * decay).astype(bf16)
            dt2 = jnp.where(lo_half, dt_l[:, l0:l0 + 1], dt_l[:, l0 + 1:l0 + 2])
            xdt = xs_[g][:, ps] * dt2
            rhs[g, p] = jnp.concatenate([jnp.where(lo_half, xdt, 0.0), jnp.where(lo_half, 0.0, xdt)],
                                        axis=0).astype(bf16)
            e_col[g, p] = jnp.exp(ac_col)
            xsc[g, p] = xdt * jnp.exp(ac_col[c - 1:c] - ac_col)
    y2 = {gp: _dot(m2[gp], rhs[gp]) for gp in m2}

    for g in gs_:
        xsg = jnp.concatenate([xsc[g, p] for p in ps_], axis=1).astype(bf16)
        dh = _dot_tn(xsg, bmb[g])
        for j in range(SSD_HG):
            ln = SM_DT + g * SSD_HG + j
            h_ref[g * SSD_HG + j] = (hg[g][j * SSD_P:(j + 1) * SSD_P] * jnp.exp(ac_all[c - 1:c, ln:ln + 1])
                                     + dh[j * SSD_P:(j + 1) * SSD_P])
    for g in gs_:
        gs = slice(g * SSD_GW, (g + 1) * SSD_GW)
        y = jnp.concatenate([y2[g, p] + ch[g][:, p * 2 * SSD_P:(p + 1) * 2 * SSD_P] * e_col[g, p] for p in ps_],
                            axis=1)
        y = (y + dvec_ref[:, gs] * xs_[g]) * _silu(z_ref[:, gs].astype(f32))
        o_ref[:, gs] = _rms(y, nw_ref[:, gs]).astype(bf16)

    _conv_carry(ext_ref, c)


STEP_B = 8


def _conv_step(x_ref, c_ref, w_ref):
    w = w_ref[...]
    return c_ref[0] * w[0:1] + c_ref[1] * w[1:2] + c_ref[2] * w[2:3] + x_ref[...] * w[3:4]


def _expand_matrix(nb, width):
    r = lax.broadcasted_iota(jnp.int32, (nb, nb * width), 0)
    c = lax.broadcasted_iota(jnp.int32, (nb, nb * width), 1)
    return jnp.where((c >= r * width) & (c < (r + 1) * width), 1.0, 0.0).astype(bf16)


def _bcast_cols(x, e_mat):
    hi, lo = _split(x)
    return _dot_tn(hi, e_mat) + _dot_tn(lo, e_mat)


def _state_call(kernel, grid, in_specs, out_specs, out_shape, scratch, sem, name, args, prev_state):
    aliases = {}
    if prev_state is not None:
        in_specs = in_specs + [pl.BlockSpec(memory_space=pl.ANY)]
        args = args + (prev_state,)
        aliases = {len(args) - 1: len(out_shape) - 1}
        kernel = functools.partial(_drop_alias_ref, kernel, len(args) - 1)
    return pl.pallas_call(kernel, grid=grid, in_specs=in_specs, out_specs=out_specs, out_shape=out_shape,
                          scratch_shapes=scratch, input_output_aliases=aliases,
                          compiler_params=_cp(sem), name=name)(*args)


def _drop_alias_ref(kernel, idx, *refs):
    return kernel(*refs[:idx], *refs[idx + 1:])


def _gdn_step_kernel(alog_ref, dtb_ref, q_ref, k_ref, v_ref, z_ref, sm_ref, cq_ref, ck_ref, cv_ref,
                     wq_ref, wk_ref, wv_ref, nw_ref, s_ref, o_ref, so_ref):
    nb = q_ref.shape[0]
    qa = _silu(_conv_step(q_ref, cq_ref, wq_ref))
    ka = _silu(_conv_step(k_ref, ck_ref, wk_ref))
    va = _silu(_conv_step(v_ref, cv_ref, wv_ref))
    sm = sm_ref[...]
    z = z_ref[...]
    e_mat = _expand_matrix(nb, GDN_DV)
    for h in range(GDN_H):
        sl = slice(h * GDN_DK, (h + 1) * GDN_DK)
        q = qa[:, sl]
        k = ka[:, sl]
        v = va[:, sl]
        q = q * lax.rsqrt(jnp.sum(q * q, axis=-1, keepdims=True) + EPS) * (GDN_DK ** -0.5)
        k = k * lax.rsqrt(jnp.sum(k * k, axis=-1, keepdims=True) + EPS)
        beta = _sigmoid(sm[:, SM_BETA + h:SM_BETA + h + 1])
        g = -jnp.exp(_scalar_vec(alog_ref[h])) * jax.nn.softplus(sm[:, SM_DEC + h:SM_DEC + h + 1] + dtb_ref[h])
        eg = jnp.exp(g)
        qk = jnp.sum(q * k, axis=-1, keepdims=True)
        kb = _bcast_cols(k, e_mat)
        qb = _bcast_cols(q, e_mat)
        blk = lambda m, b: m[:, b * GDN_DV:(b + 1) * GDN_DV]
        ks = jnp.concatenate([jnp.sum(s_ref[b, h] * blk(kb, b), axis=0, keepdims=True) for b in range(nb)], axis=0)
        qs = jnp.concatenate([jnp.sum(s_ref[b, h] * blk(qb, b), axis=0, keepdims=True) for b in range(nb)], axis=0)
        w = beta * v - (beta * eg) * ks
        o = qs * eg + qk * w
        for b in range(nb):
            so_ref[b, h] = s_ref[b, h] * eg[b:b + 1] + blk(kb, b) * w[b:b + 1]
        o = _rms(o, nw_ref[...]) * _silu(z[:, sl])
        o_ref[:, sl] = o.astype(bf16)


def _gdn_step(proj, small, conv_t, states, prev, l, conv_w, a_log, dt_bias, norm_w):
    n = proj.shape[0]
    nb = STEP_B
    blk = lambda off: pl.BlockSpec((nb, GDN_QK), lambda i: (i, _blk(off, GDN_QK)))
    cst = lambda j: pl.BlockSpec((CONV_W - 1, nb, GDN_QK), lambda i: (0, i, j))
    cw = lambda j: pl.BlockSpec((CONV_W, GDN_QK), lambda i: (0, j))
    smem = pl.BlockSpec(memory_space=pltpu.SMEM)
    st = pl.BlockSpec((None, nb, GDN_H, GDN_DK, GDN_DV), lambda i: (l, i, 0, 0, 0))
    return _state_call(
        _gdn_step_kernel, (n // nb,),
        [smem, smem, blk(P_QKV_A), blk(P_QKV_A + GDN_QK), blk(P_QKV_A + 2 * GDN_QK), blk(P_Z_A),
         pl.BlockSpec((nb, LANE), lambda i: (i, 0)),
         cst(0), cst(1), cst(2), cw(0), cw(1), cw(2),
         pl.BlockSpec((1, GDN_DV), lambda i: (0, 0)), st],
        [pl.BlockSpec((nb, GDN_V), lambda i: (i, 0)), st],
        [jax.ShapeDtypeStruct((n, GDN_V), bf16), jax.ShapeDtypeStruct(states.shape, f32)],
        [], ("parallel",), "gdn_step",
        (a_log, dt_bias, proj, proj, proj, proj, small, conv_t, conv_t, conv_t, conv_w, conv_w, conv_w, norm_w, states),
        prev)


def _gla_step_kernel(q_ref, k_ref, v_ref, r_ref, sm_ref, wgate_ref, bgate_ref, nw_ref, s_ref,
                     o_ref, so_ref):
    nb = q_ref.shape[0]
    sm = sm_ref[...]
    lane = lax.broadcasted_iota(jnp.int32, sm.shape, 1)
    lr = jnp.where((lane >= SM_LR) & (lane < SM_LR + GLA_RANK), sm, 0.0).astype(bf16)
    la_all = jax.nn.log_sigmoid(_dot(lr, wgate_ref[...]) + bgate_ref[...]) / GLA_TAU
    qa = q_ref[...] * (GLA_DK ** -0.5)
    ka = k_ref[...]
    va = v_ref[...]
    r = r_ref[...]
    for h in range(GLA_H):
        sk = slice(h * GLA_DK, (h + 1) * GLA_DK)
        sv = slice(h * GLA_DV, (h + 1) * GLA_DV)
        q = qa[:, sk]
        k = ka[:, sk]
        v = va[:, sv]
        e = jnp.exp(la_all[:, sk])
        qk = jnp.sum(q * k, axis=-1, keepdims=True)
        e_t = e.T
        k_t = k.T
        qe_t = (q * e).T
        rows = []
        for b in range(nb):
            s = s_ref[b, h]
            vrow = v[b:b + 1]
            rows.append(qk[b:b + 1] * vrow + jnp.sum(s * qe_t[:, b:b + 1], axis=0, keepdims=True))
            so_ref[b, h] = s * e_t[:, b:b + 1] + k_t[:, b:b + 1] * vrow
        o = jnp.concatenate(rows, axis=0)
        o = _rms(o, nw_ref[...]) * _silu(r[:, sv])
        o_ref[:, sv] = o.astype(bf16)


def _gla_step(proj, small, states, prev, l, wgate_pad, bgate, norm_w):
    n = proj.shape[0]
    nb = STEP_B
    st = pl.BlockSpec((None, nb, GLA_H, GLA_DK, GLA_DV), lambda i: (l, i, 0, 0, 0))
    return _state_call(
        _gla_step_kernel, (n // nb,),
        [pl.BlockSpec((nb, GLA_QK), lambda i: (i, _blk(P_Q_B, GLA_QK))),
         pl.BlockSpec((nb, GLA_QK), lambda i: (i, _blk(P_K_B, GLA_QK))),
         pl.BlockSpec((nb, GLA_V), lambda i: (i, _blk(P_V_B, GLA_V))),
         pl.BlockSpec((nb, GLA_V), lambda i: (i, _blk(P_R_B, GLA_V))),
         pl.BlockSpec((nb, LANE), lambda i: (i, 0)),
         pl.BlockSpec((LANE, GLA_QK), lambda i: (0, 0)),
         pl.BlockSpec((1, GLA_QK), lambda i: (0, 0)),
         pl.BlockSpec((1, GLA_DV), lambda i: (0, 0)), st],
        [pl.BlockSpec((nb, GLA_V), lambda i: (i, 0)), st],
        [jax.ShapeDtypeStruct((n, GLA_V), bf16), jax.ShapeDtypeStruct(states.shape, f32)],
        [], ("parallel",), "gla_step",
        (proj, proj, proj, proj, small, wgate_pad, bgate, norm_w, states),
        prev)


def _ssd_step_kernel(alog_ref, dtb_ref, z_ref, xbc_ref, sm_ref, cst_ref, cw_ref, cb_ref, dvec_ref, nw_ref, h_ref,
                     o_ref, ho_ref):
    nb = xbc_ref.shape[0]
    xbc = _silu(_conv_step(xbc_ref, cst_ref, cw_ref) + cb_ref[...])
    sm_t = sm_ref[...].T
    e_mat = _expand_matrix(nb, SSD_N)
    lane_b = lax.broadcasted_iota(jnp.int32, (SSD_P, nb), 1)
    for g in range(SSD_G):
        gs = slice(g * SSD_GW, (g + 1) * SSD_GW)
        bm = xbc[:, SSD_INNER + g * SSD_N:SSD_INNER + (g + 1) * SSD_N]
        cm_t = xbc[:, SSD_INNER + SSD_BC + g * SSD_N:SSD_INNER + SSD_BC + (g + 1) * SSD_N].T
        cb_row = jnp.sum(cm_t * bm.T, axis=0, keepdims=True)
        hg = h_ref[:, pl.ds(g * SSD_HG, SSD_HG)].reshape(nb * SSD_GW, SSD_N)
        ch = _dot(hg.astype(bf16), cm_t.astype(bf16))
        pairs = []
        for jp in range(SSD_HG // 2):
            x_t = xbc[:, g * SSD_GW + jp * LANE:g * SSD_GW + (jp + 1) * LANE].T
            halves = []
            for jj in range(2):
                j = 2 * jp + jj
                hd = g * SSD_HG + j
                dt_row = jax.nn.softplus(sm_t[SM_DT + hd:SM_DT + hd + 1, :] + dtb_ref[hd])
                ea_row = jnp.exp(dt_row * (-jnp.exp(_scalar_vec(alog_ref[hd]))))
                xdt_t = x_t[jj * SSD_P:(jj + 1) * SSD_P] * dt_row
                hi, lo = _split(xdt_t)
                xb = _dot(hi, e_mat) + _dot(lo, e_mat)
                yh = jnp.zeros((SSD_P, nb), f32)
                for b in range(nb):
                    r0 = (b * SSD_HG + j) * SSD_P
                    yh = jnp.where(lane_b == b, ch[r0:r0 + SSD_P], yh)
                    ho_ref[b, hd] = (h_ref[b, hd] * ea_row[:, b:b + 1]
                                     + xb[:, b * SSD_N:(b + 1) * SSD_N] * bm[b:b + 1])
                halves.append(yh * ea_row + cb_row * xdt_t)
            pairs.append(jnp.concatenate(halves, axis=0).T)
        y = jnp.concatenate(pairs, axis=1)
        y = (y + dvec_ref[:, gs] * xbc[:, gs]) * _silu(z_ref[:, gs])
        o_ref[:, gs] = _rms(y, nw_ref[:, gs]).astype(bf16)


def _ssd_step(proj, small, conv_t, states, prev, l, conv_w, conv_b, a_log, dt_bias, dvec, norm_w):
    n = proj.shape[0]
    nb = STEP_B
    smem = pl.BlockSpec(memory_space=pltpu.SMEM)
    const = lambda shape: pl.BlockSpec(shape, lambda i: (0,) * len(shape))
    st = pl.BlockSpec((None, nb, SSD_H, SSD_P, SSD_N), lambda i: (l, i, 0, 0, 0))
    return _state_call(
        _ssd_step_kernel, (n // nb,),
        [smem, smem,
         pl.BlockSpec((nb, SSD_INNER), lambda i: (i, _blk(P_Z_C, SSD_INNER))),
         pl.BlockSpec((nb, SSD_CONV), lambda i: (i, _blk(P_XBC, SSD_CONV))),
         pl.BlockSpec((nb, LANE), lambda i: (i, 0)),
         pl.BlockSpec((CONV_W - 1, nb, SSD_CONV), lambda i: (0, i, 0)),
         const((CONV_W, SSD_CONV)), const((1, SSD_CONV)), const((1, SSD_INNER)), const((1, SSD_INNER)), st],
        [pl.BlockSpec((nb, SSD_INNER), lambda i: (i, 0)), st],
        [jax.ShapeDtypeStruct((n, SSD_INNER), bf16), jax.ShapeDtypeStruct(states.shape, f32)],
        [], ("parallel",), "ssd_step",
        (a_log, dt_bias, proj, proj, small, conv_t, conv_w, conv_b, dvec, norm_w, states),
        prev)


def _mixout_kernel(oa_ref, ob_ref, oc_ref, ga_ref, gb_ref, gc_ref, wa_ref, wb_ref, wc_ref, wo_ref,
                   x_ref, gt_ref, o_ref, acc_ref):
    j = pl.program_id(2)
    m = (_sigmoid(ga_ref[...].astype(f32)) * _dot(oa_ref[...], wa_ref[...])
         + _sigmoid(gb_ref[...].astype(f32)) * _dot(ob_ref[...], wb_ref[...])
         + _sigmoid(gc_ref[...].astype(f32)) * _dot(oc_ref[...], wc_ref[...]))
    part = _dot(m.astype(bf16), wo_ref[...])

    @pl.when(j == 0)
    def _():
        acc_ref[...] = part

    @pl.when(j > 0)
    def _():
        acc_ref[...] += part

    @pl.when(j == pl.num_programs(2) - 1)
    def _():
        o_ref[...] = x_ref[...] + gt_ref[...] * acc_ref[...]


def _mixout(oa, ob, oc, proj, x, mod, wa, wb, wc, wo, l, per_row, tm):
    bx, tx, dm = x.shape
    tn = 512
    gate = lambda k: pl.BlockSpec((None, tm, tn), lambda b, i, j: (b, i, _blk(P_GATES + k * D_MODEL, tn) + j))
    act = lambda w: pl.BlockSpec((None, tm, w), lambda b, i, j: (b, i, 0))
    wsp = lambda w: pl.BlockSpec((None, w, tn), lambda b, i, j: (l, 0, j))
    return pl.pallas_call(
        _mixout_kernel,
        grid=(bx, tx // tm, dm // tn),
        in_specs=[act(GDN_V), act(GLA_V), act(SSD_INNER), gate(0), gate(1), gate(2),
                  wsp(GDN_V), wsp(GLA_V), wsp(SSD_INNER),
                  pl.BlockSpec((None, tn, dm), lambda b, i, j: (l, j, 0)),
                  pl.BlockSpec((None, tm, dm), lambda b, i, j: (b, i, 0)),
                  _mod_spec(per_row, tm, 5)],
        out_specs=pl.BlockSpec((None, tm, dm), lambda b, i, j: (b, i, 0)),
        out_shape=jax.ShapeDtypeStruct(x.shape, f32),
        scratch_shapes=[pltpu.VMEM((tm, dm), f32)],
        compiler_params=_cp(("parallel", "parallel", "arbitrary")),
        name="mixout",
    )(oa, ob, oc, proj, proj, proj, wa, wb, wc, wo, x, mod)


def _permute_w_in(w):
    nl, d, _ = w.shape
    wt = jnp.swapaxes(w, 1, 2)
    rows = lambda name: wt[:, W_IN_SRC[name][0]:W_IN_SRC[name][1]]
    main = jnp.concatenate([rows(name) for name in MAIN_ORDER], axis=1).astype(bf16)
    small = jnp.concatenate([rows(name) for name in SMALL_ORDER], axis=1)
    small = jnp.pad(small, ((0, 0), (0, LANE - small.shape[1]), (0, 0))).astype(bf16)
    return main, small


def _ffn_weights(wg, wu, wd):
    f = wg.shape[2]
    cut = f - f % FF_TILE
    w = tuple(a.astype(bf16) for a in (wg, wu, wd))
    if cut == f:
        return w
    return w + (w[0][:, :, cut:], w[1][:, :, cut:], w[2][:, cut:, :])


def _lane_vec(v, lo):
    return jnp.zeros((1, LANE), f32).at[0, lo:lo + v.shape[0]].set(v)


def _layer_params(l, p):
    row = lambda a: a[l].reshape(1, -1)
    wgate = jnp.zeros((LANE, GLA_QK), f32).at[SM_LR:SM_LR + GLA_RANK].set(p["gla_w_gate"][l]).astype(bf16)
    return dict(
        norm1=row(p["norm1"]), norm2=row(p["norm2"]), norm3=row(p["norm3"]),
        gdn_conv_w=p["gdn_conv_w"][l], gdn_a_log=p["gdn_a_log"][l], gdn_dt_bias=p["gdn_dt_bias"][l],
        gdn_alog_l=_lane_vec(p["gdn_a_log"][l], SM_DEC), gdn_dtb_l=_lane_vec(p["gdn_dt_bias"][l], SM_DEC),
        gdn_norm_w=row(p["gdn_norm_w"]),
        gla_wgate=wgate, gla_bgate=row(p["gla_b_gate"]), gla_norm_w=row(p["gla_norm_w"]),
        ssd_conv_w=p["ssd_conv_w"][l], ssd_conv_b=row(p["ssd_conv_b"]), ssd_a_log=p["ssd_a_log"][l],
        ssd_dt_bias=p["ssd_dt_bias"][l],
        ssd_avec_l=_lane_vec(-jnp.exp(p["ssd_a_log"][l]), SM_DT), ssd_dtb_l=_lane_vec(p["ssd_dt_bias"][l], SM_DT),
        ssd_dvec=jnp.repeat(p["ssd_d"][l], SSD_P).reshape(1, -1),
        ssd_norm_w=row(p["ssd_norm_w"]),
    )


def _stacked_weights(p):
    w_in, w_in_small = _permute_w_in(p["w_in"])
    return dict(
        f1=_ffn_weights(p["ffn1_wg"], p["ffn1_wu"], p["ffn1_wd"]),
        f2=_ffn_weights(p["ffn2_wg"], p["ffn2_wu"], p["ffn2_wd"]),
        w_in=w_in, w_in_small=w_in_small,
        wa=p["w_branch_gdn"].astype(bf16), wb=p["w_branch_gla"].astype(bf16),
        wc=p["w_branch_ssd"].astype(bf16), w_out=p["w_out"].astype(bf16),
    )


def _new_conv_state(buf, raw):
    t = raw.shape[1]
    k = CONV_W - 1
    if t >= k:
        return raw[:, t - k:]
    return jnp.concatenate([buf[:, t:], raw], axis=1)


def _mixer_prompt(proj, small, lp, st):
    gdn_conv, s_gdn, s_gla, ssd_conv, s_ssd = st
    b, t, _ = proj.shape
    c = CHUNK
    sm_spec = pl.BlockSpec((None, c, LANE), lambda bi, ti: (bi, ti, 0))
    st_a, st_b, st_c = _per_seq(GDN_H, GDN_DK, GDN_DV), _per_seq(GLA_H, GLA_DK, GLA_DV), _per_seq(SSD_H, SSD_P, SSD_N)
    gdn_specs = [_tok(GDN_CONV, P_QKV_A), _tok(GDN_V, P_Z_A), sm_spec, _per_seq(CONV_W - 1, GDN_CONV),
                 _const(CONV_W, GDN_CONV), _const(1, LANE), _const(1, LANE), _const(1, GDN_DV), st_a]
    gdn_args = (proj, proj, small, gdn_conv, lp["gdn_conv_w"], lp["gdn_alog_l"], lp["gdn_dtb_l"],
                lp["gdn_norm_w"], s_gdn)
    gla_specs = [_tok(GLA_QK, P_Q_B), _tok(GLA_QK, P_K_B), _tok(GLA_V, P_V_B), _tok(GLA_V, P_R_B), sm_spec,
                 _const(LANE, GLA_QK), _const(1, GLA_QK), _const(1, GLA_DV), st_b]
    gla_args = (proj, proj, proj, proj, small, lp["gla_wgate"], lp["gla_bgate"], lp["gla_norm_w"], s_gla)
    ssd_specs = [_tok(SSD_INNER, P_Z_C), _tok(SSD_CONV, P_XBC), sm_spec, _per_seq(CONV_W - 1, SSD_CONV),
                 _const(CONV_W, SSD_CONV), _const(1, SSD_CONV), _const(1, LANE), _const(1, LANE),
                 _const(1, SSD_INNER), _const(1, SSD_INNER), st_c]
    ssd_args = (proj, proj, small, ssd_conv, lp["ssd_conv_w"], lp["ssd_conv_b"], lp["ssd_avec_l"],
                lp["ssd_dtb_l"], lp["ssd_dvec"], lp["ssd_norm_w"], s_ssd)
    assert (len(gdn_specs), len(gla_specs), len(ssd_specs)) == (N_GDN_IN, N_GLA_IN, N_SSD_IN)
    out_tok = lambda width: pl.BlockSpec((None, c, width), lambda bi, ti: (bi, ti, 0))
    oa, s_gdn_n, ob, s_gla_n, oc, s_ssd_n = pl.pallas_call(
        _mixers_kernel,
        grid=(b, t // c),
        in_specs=gdn_specs + gla_specs + ssd_specs,
        out_specs=[out_tok(GDN_V), st_a, out_tok(GLA_V), st_b, out_tok(SSD_INNER), st_c],
        out_shape=[jax.ShapeDtypeStruct((b, t, GDN_V), bf16), jax.ShapeDtypeStruct(s_gdn.shape, f32),
                   jax.ShapeDtypeStruct((b, t, GLA_V), bf16), jax.ShapeDtypeStruct(s_gla.shape, f32),
                   jax.ShapeDtypeStruct((b, t, SSD_INNER), bf16), jax.ShapeDtypeStruct(s_ssd.shape, f32)],
        scratch_shapes=[pltpu.VMEM((8 + c, GDN_CONV), f32), pltpu.VMEM((GLA_H, GLA_DV, GLA_DK), f32),
                        pltpu.VMEM((8 + c, SSD_CONV), f32)],
        compiler_params=_cp(("parallel", "arbitrary")),
        name="mixers_prompt",
    )(*gdn_args, *gla_args, *ssd_args)
    gdn_conv_n = _new_conv_state(gdn_conv, proj[:, :, P_QKV_A:P_QKV_A + GDN_CONV]).astype(f32)
    ssd_conv_n = _new_conv_state(ssd_conv, proj[:, :, P_XBC:P_XBC + SSD_CONV]).astype(f32)
    return (oa, ob, oc), (gdn_conv_n, s_gdn_n, s_gla_n, ssd_conv_n, s_ssd_n)


def _mixer_sample(proj, small, lp, l, states, prev):
    n = proj.shape[1]
    p2 = proj.reshape(n, P_TOTAL)
    s2 = small.reshape(n, LANE)
    gdn_conv, ssd_conv = states[0][l], states[3][l]
    gct = jnp.swapaxes(gdn_conv, 0, 1)
    sct = jnp.swapaxes(ssd_conv, 0, 1)
    pv = (None,) * 5 if prev is None else prev
    oa, s_gdn_n = _gdn_step(p2, s2, gct, states[1], pv[1], l, lp["gdn_conv_w"], lp["gdn_a_log"],
                            lp["gdn_dt_bias"], lp["gdn_norm_w"])
    ob, s_gla_n = _gla_step(p2, s2, states[2], pv[2], l, lp["gla_wgate"], lp["gla_bgate"], lp["gla_norm_w"])
    oc, s_ssd_n = _ssd_step(p2, s2, sct, states[4], pv[4], l, lp["ssd_conv_w"], lp["ssd_conv_b"], lp["ssd_a_log"],
                            lp["ssd_dt_bias"], lp["ssd_dvec"], lp["ssd_norm_w"])
    raw = p2.reshape(n, 1, P_TOTAL)
    gdn_conv_n = _new_conv_state(gdn_conv, raw[:, :, P_QKV_A:P_QKV_A + GDN_CONV])
    ssd_conv_n = _new_conv_state(ssd_conv, raw[:, :, P_XBC:P_XBC + SSD_CONV])
    outs = tuple(o.reshape(1, n, -1) for o in (oa, ob, oc))
    return outs, (gdn_conv_n, s_gdn_n, s_gla_n, ssd_conv_n, s_ssd_n)


def _trunk(x, mods, lps, sw, states, per_row, tm, tm_in, final_w):
    nl = len(lps)
    per_layer = []
    prev = None
    for l in range(nl):
        lp, mod = lps[l], mods[l]
        last = l == nl - 1
        x = _ffn(x, mod, 0, lp["norm1"], sw["f1"], l, lp["norm1"], per_row, tm, False)
        proj, small = _inproj(x, mod, lp["norm2"], sw["w_in"], sw["w_in_small"], l, per_row, tm_in,
                              f32 if per_row else bf16)
        if per_row:
            (oa, ob, oc), st = _mixer_sample(proj, small, lp, l, states, prev)
            prev = st
        else:
            (oa, ob, oc), st = _mixer_prompt(proj, small, lp, tuple(s[l] for s in states))
        per_layer.append(st)
        x = _mixout(oa, ob, oc, proj, x, mod, sw["wa"], sw["wb"], sw["wc"], sw["w_out"], l, per_row, tm)
        x = _ffn(x, mod, 6, lp["norm3"], sw["f2"], l, final_w if last else lp["norm3"], per_row, tm, last)
    stack = lambda i: jnp.stack([st[i] for st in per_layer])
    if per_row:
        new_states = (stack(0), prev[1], prev[2], stack(3), prev[4])
    else:
        new_states = tuple(stack(i) for i in range(5))
    return x, new_states


def kernel(x_prompt, x_sample, state_gdn_conv, state_gdn, state_gla, state_ssd_conv, state_ssd, c_prompt, c_sample, w_ada, b_ada, norm1, norm2, norm3, ffn1_wg, ffn1_wu, ffn1_wd, ffn2_wg, ffn2_wu, ffn2_wd, w_in, gdn_conv_w, gdn_a_log, gdn_dt_bias, gdn_norm_w, gla_w_gate, gla_b_gate, gla_norm_w, ssd_conv_w, ssd_conv_b, ssd_a_log, ssd_dt_bias, ssd_d, ssd_norm_w, w_branch_gdn, w_branch_gla, w_branch_ssd, w_out, final_norm):
    p = dict(norm1=norm1, norm2=norm2, norm3=norm3,
             ffn1_wg=ffn1_wg, ffn1_wu=ffn1_wu, ffn1_wd=ffn1_wd, ffn2_wg=ffn2_wg, ffn2_wu=ffn2_wu, ffn2_wd=ffn2_wd,
             w_in=w_in, gdn_conv_w=gdn_conv_w, gdn_a_log=gdn_a_log, gdn_dt_bias=gdn_dt_bias, gdn_norm_w=gdn_norm_w,
             gla_w_gate=gla_w_gate, gla_b_gate=gla_b_gate, gla_norm_w=gla_norm_w,
             ssd_conv_w=ssd_conv_w, ssd_conv_b=ssd_conv_b, ssd_a_log=ssd_a_log, ssd_dt_bias=ssd_dt_bias,
             ssd_d=ssd_d, ssd_norm_w=ssd_norm_w,
             w_branch_gdn=w_branch_gdn, w_branch_gla=w_branch_gla, w_branch_ssd=w_branch_ssd, w_out=w_out)
    nl = w_ada.shape[0]
    bp, tp, dm = x_prompt.shape
    bs = x_sample.shape[0]
    assert x_sample.shape[1] == 1 and tp % CHUNK == 0 and bs % STEP_B == 0 and dm == D_MODEL
    lps = [_layer_params(l, p) for l in range(nl)]
    sw = _stacked_weights(p)
    fw = final_norm.reshape(1, dm)

    rows = bp + bs
    rpad = -(-rows // 8) * 8
    c_all = jnp.concatenate([c_prompt, c_sample, jnp.zeros((rpad - rows, dm), f32)], axis=0)
    mod = _ada_mod(c_all, w_ada, b_ada)
    mod_p = [mod[l, :bp].reshape(bp, 1, N_MOD * dm) for l in range(nl)]
    mod_s = [mod[l, bp:rows].reshape(1, bs, N_MOD * dm) for l in range(nl)]

    sample_states = (state_gdn_conv, state_gdn, state_gla, state_ssd_conv, state_ssd)
    prompt_states = tuple(jnp.zeros((s.shape[0], bp) + s.shape[2:], x_prompt.dtype) for s in sample_states)
    tm_p = 512 if tp % 512 == 0 else CHUNK
    tm_in = 1024 if tp % 1024 == 0 else tm_p
    y_p, st_p = _trunk(x_prompt, mod_p, lps, sw, prompt_states, False, tm_p, tm_in, fw)
    y_s, st_s = _trunk(x_sample.reshape(1, bs, dm), mod_s, lps, sw, sample_states, True, bs, bs, fw)
    return (y_p, y_s.reshape(bs, 1, dm)) + st_p + st_s
```

```python
import functools

import jax
import jax.numpy as jnp
from jax import lax
from jax.experimental import pallas as pl
from jax.experimental.pallas import tpu as pltpu

f32 = jnp.float32
bf16 = jnp.bfloat16
HI = lax.Precision.HIGHEST

EPS = 1e-6
D_MODEL = 2048
N_MOD = 9
CHUNK = 64
CONV_W = 4
GDN_H, GDN_DK, GDN_DV = 8, 128, 128
GLA_H, GLA_DK, GLA_DV, GLA_RANK, GLA_TAU = 4, 128, 256, 16, 16.0
SSD_H, SSD_P, SSD_G, SSD_N = 32, 64, 4, 128
SSD_HG = SSD_H // SSD_G
GDN_QK = GDN_H * GDN_DK
GDN_V = GDN_H * GDN_DV
GDN_CONV = 2 * GDN_QK + GDN_V
GLA_QK = GLA_H * GLA_DK
GLA_V = GLA_H * GLA_DV
SSD_INNER = SSD_H * SSD_P
SSD_BC = SSD_G * SSD_N
SSD_CONV = SSD_INNER + 2 * SSD_BC
SSD_GW = SSD_HG * SSD_P

LANE = 128
FF_TILE = 512
FF_SUB = 256

IN_SPLITS = (("qkv_a", GDN_CONV), ("z_a", GDN_V), ("beta", GDN_H), ("dec", GDN_H),
             ("q_b", GLA_QK), ("k_b", GLA_QK), ("v_b", GLA_V), ("lr", GLA_RANK), ("r_b", GLA_V),
             ("z_c", SSD_INNER), ("xbc", SSD_CONV), ("dt", SSD_H), ("gates", 3 * D_MODEL))
MAIN_ORDER = ("qkv_a", "xbc", "z_c", "gates", "z_a", "q_b", "k_b", "v_b", "r_b")
SMALL_ORDER = ("beta", "dec", "lr", "dt")


def _layout():
    src, off = {}, 0
    for name, w in IN_SPLITS:
        src[name] = (off, off + w)
        off += w
    main, small, d = {}, {}, 0
    for name in MAIN_ORDER:
        w = src[name][1] - src[name][0]
        main[name] = d
        d += w
    total = d
    d = 0
    for name in SMALL_ORDER:
        small[name] = d
        d += src[name][1] - src[name][0]
    assert d <= LANE
    return src, main, small, total


W_IN_SRC, P_MAIN, P_SM, P_TOTAL = _layout()
P_QKV_A, P_XBC, P_Z_C, P_GATES, P_Z_A = (P_MAIN[k] for k in ("qkv_a", "xbc", "z_c", "gates", "z_a"))
P_Q_B, P_K_B, P_V_B, P_R_B = (P_MAIN[k] for k in ("q_b", "k_b", "v_b", "r_b"))
SM_BETA, SM_DEC, SM_LR, SM_DT = (P_SM[k] for k in SMALL_ORDER)

VMEM_LIMIT = 56 * 1024 * 1024


def _cp(sem):
    return pltpu.CompilerParams(dimension_semantics=sem, vmem_limit_bytes=VMEM_LIMIT)


def _blk(off, width):
    assert off % width == 0, (off, width)
    return off // width


def _sigmoid(x):
    return jax.nn.sigmoid(x)


def _silu(x):
    return x * jax.nn.sigmoid(x)


def _rms(x, w):
    return x * lax.rsqrt(jnp.mean(x * x, axis=-1, keepdims=True) + EPS) * w


def _dot(a, b):
    return jnp.dot(a, b, preferred_element_type=f32)


def _dot_nt(a, b):
    return lax.dot_general(a, b, (((1,), (1,)), ((), ())), preferred_element_type=f32)


def _dot_tn(a, b):
    return lax.dot_general(a, b, (((0,), (0,)), ((), ())), preferred_element_type=f32)


def _split(x):
    hi = x.astype(bf16)
    return hi, (x - hi.astype(f32)).astype(bf16)


def _dot3(a, b):
    return _dot(a[0], b[0]) + (_dot(a[0], b[1]) + _dot(a[1], b[0]))


def _dot_hi(a, b):
    return jnp.dot(a, b, precision=HI, preferred_element_type=f32)


def _tri_masks(c):
    row = lax.broadcasted_iota(jnp.int32, (c, c), 0)
    col = lax.broadcasted_iota(jnp.int32, (c, c), 1)
    return row >= col, row > col, row == col


def _lane_col(x, idx):
    lane = lax.broadcasted_iota(jnp.int32, x.shape, 1)
    return jnp.sum(jnp.where(lane == idx, x, 0.0), axis=1, keepdims=True)


def _scalar_vec(s):
    return jnp.full((1, 1), s, f32)


def _ada_kernel(c_ref, w_ref, b_ref, o_ref):
    s = _silu(c_ref[...]).astype(bf16)
    o_ref[...] = _dot(s, w_ref[...].astype(bf16)) + b_ref[...]


def _ada_mod(c_all, w_ada, b_ada):
    nl, dm, n = w_ada.shape
    r = c_all.shape[0]
    tn = 1024
    return pl.pallas_call(
        _ada_kernel,
        grid=(nl, n // tn),
        in_specs=[pl.BlockSpec((r, dm), lambda l, j: (0, 0)),
                  pl.BlockSpec((None, dm, tn), lambda l, j: (l, 0, j)),
                  pl.BlockSpec((None, 1, tn), lambda l, j: (l, 0, j))],
        out_specs=pl.BlockSpec((None, r, tn), lambda l, j: (l, 0, j)),
        out_shape=jax.ShapeDtypeStruct((nl, r, n), f32),
        compiler_params=_cp(("arbitrary", "arbitrary")),
        name="ada_mod",
    )(c_all, w_ada, b_ada.reshape(nl, 1, n))


def _mod_spec(per_row, tm, chunk):
    if per_row:
        return pl.BlockSpec((None, tm, D_MODEL), lambda b, i, j: (b, i, chunk))
    return pl.BlockSpec((None, 1, D_MODEL), lambda b, i, j: (b, 0, chunk))


def _ffn_kernel(x_ref, sh_ref, sc_ref, gt_ref, nw_ref, fw_ref, wg_ref, wu_ref, wd_ref, *rest, final, has_tail):
    if has_tail:
        wgt_ref, wut_ref, wdt_ref, o_ref, h_ref, acc_ref = rest
    else:
        o_ref, h_ref, acc_ref = rest
    f = pl.program_id(2)
    last = pl.num_programs(2) - 1

    @pl.when(f == 0)
    def _():
        y = _rms(x_ref[...], nw_ref[...])
        h_ref[...] = (y * (1.0 + sc_ref[...]) + sh_ref[...]).astype(bf16)
        acc_ref[...] = jnp.zeros_like(acc_ref)

    def accumulate(wg, wu, wd):
        h = h_ref[...]
        width = wg.shape[1]
        sub = FF_SUB if width % FF_SUB == 0 else width
        parts = [(_dot(h, wg[:, s:s + sub]), _dot(h, wu[:, s:s + sub])) for s in range(0, width, sub)]
        acc = acc_ref[...]
        for i, (g, u) in enumerate(parts):
            acc = acc + _dot((_silu(g) * u).astype(bf16), wd[i * sub:(i + 1) * sub, :])
        acc_ref[...] = acc

    if has_tail:
        pl.when(f < last)(lambda: accumulate(wg_ref, wu_ref, wd_ref))
        pl.when(f == last)(lambda: accumulate(wgt_ref, wut_ref, wdt_ref))
    else:
        accumulate(wg_ref, wu_ref, wd_ref)

    @pl.when(f == last)
    def _():
        y = x_ref[...] + 0.5 * gt_ref[...] * acc_ref[...]
        if final:
            y = _rms(y, fw_ref[...])
        o_ref[...] = y


def _ffn(x, mod, k0, nw, w, l, fw, per_row, tm, final):
    bx, tx, dm = x.shape
    tf = FF_TILE
    n_full = w[0].shape[2] // tf
    has_tail = len(w) > 3
    full = lambda f: jnp.minimum(f, n_full - 1)
    w_specs = [pl.BlockSpec((None, dm, tf), lambda b, i, f: (l, 0, full(f))),
               pl.BlockSpec((None, dm, tf), lambda b, i, f: (l, 0, full(f))),
               pl.BlockSpec((None, tf, dm), lambda b, i, f: (l, full(f), 0))]
    if has_tail:
        ft = w[3].shape[2]
        w_specs += [pl.BlockSpec((None, dm, ft), lambda b, i, f: (l, 0, 0)),
                    pl.BlockSpec((None, dm, ft), lambda b, i, f: (l, 0, 0)),
                    pl.BlockSpec((None, ft, dm), lambda b, i, f: (l, 0, 0))]
    return pl.pallas_call(
        functools.partial(_ffn_kernel, final=final, has_tail=has_tail),
        grid=(bx, tx // tm, n_full + int(has_tail)),
        in_specs=[pl.BlockSpec((None, tm, dm), lambda b, i, f: (b, i, 0)),
                  _mod_spec(per_row, tm, k0), _mod_spec(per_row, tm, k0 + 1), _mod_spec(per_row, tm, k0 + 2),
                  pl.BlockSpec((1, dm), lambda b, i, f: (0, 0)),
                  pl.BlockSpec((1, dm), lambda b, i, f: (0, 0))] + w_specs,
        out_specs=pl.BlockSpec((None, tm, dm), lambda b, i, f: (b, i, 0)),
        out_shape=jax.ShapeDtypeStruct(x.shape, f32),
        scratch_shapes=[pltpu.VMEM((tm, dm), bf16), pltpu.VMEM((tm, dm), f32)],
        compiler_params=_cp(("parallel", "parallel", "arbitrary")),
        name="ffn",
    )(x, mod, mod, mod, nw, fw, *w)


def _inproj_kernel(x_ref, sh_ref, sc_ref, nw_ref, w_ref, ws_ref, o_ref, os_ref, h_ref):
    @pl.when(pl.program_id(2) == 0)
    def _():
        y = _rms(x_ref[...], nw_ref[...])
        h = (y * (1.0 + sc_ref[...]) + sh_ref[...]).astype(bf16)
        h_ref[...] = h
        os_ref[...] = _dot_nt(h, ws_ref[...])

    o_ref[...] = _dot_nt(h_ref[...], w_ref[...]).astype(o_ref.dtype)


def _inproj(x, mod, nw, w, ws, l, per_row, tm, out_dtype):
    bx, tx, dm = x.shape
    n = w.shape[1]
    tn = 1024
    return pl.pallas_call(
        _inproj_kernel,
        grid=(bx, tx // tm, n // tn),
        in_specs=[pl.BlockSpec((None, tm, dm), lambda b, i, j: (b, i, 0)),
                  _mod_spec(per_row, tm, 3), _mod_spec(per_row, tm, 4),
                  pl.BlockSpec((1, dm), lambda b, i, j: (0, 0)),
                  pl.BlockSpec((None, tn, dm), lambda b, i, j: (l, j, 0)),
                  pl.BlockSpec((None, LANE, dm), lambda b, i, j: (l, 0, 0))],
        out_specs=[pl.BlockSpec((None, tm, tn), lambda b, i, j: (b, i, j)),
                   pl.BlockSpec((None, tm, LANE), lambda b, i, j: (b, i, 0))],
        out_shape=[jax.ShapeDtypeStruct((bx, tx, n), out_dtype),
                   jax.ShapeDtypeStruct((bx, tx, LANE), f32)],
        scratch_shapes=[pltpu.VMEM((tm, dm), bf16)],
        compiler_params=_cp(("parallel", "parallel", "arbitrary")),
        name="inproj",
    )(x, mod, mod, nw, w, ws)


def _conv_load(ext_ref, x_ref, cst_ref, first):
    c = x_ref.shape[0]

    @pl.when(first)
    def _():
        ext_ref[pl.ds(5, CONV_W - 1), :] = cst_ref[...]

    ext_ref[pl.ds(8, c), :] = x_ref[...].astype(f32)


def _conv_cols(ext_ref, w_ref, b_ref, lo, width, c):
    sl = pl.ds(lo, width)
    w = w_ref[:, sl]
    y = ext_ref[pl.ds(5, c), sl] * w[0:1]
    for j in range(1, CONV_W):
        y = y + ext_ref[pl.ds(5 + j, c), sl] * w[j:j + 1]
    if b_ref is not None:
        y = y + b_ref[:, sl]
    return _silu(y)


def _conv_carry(ext_ref, c):
    ext_ref[pl.ds(5, CONV_W - 1), :] = ext_ref[pl.ds(8 + c - (CONV_W - 1), CONV_W - 1), :]


def _cumsum_rows(x, tri):
    return _dot_hi(jnp.where(tri, 1.0, 0.0).astype(f32), x)


def _gdn_kernel(qkv_ref, z_ref, sm_ref, cst_ref, cw_ref, alog_ref, dtb_ref, nw_ref, s0_ref,
                o_ref, s_ref, ext_ref):
    c = qkv_ref.shape[0]
    first = pl.program_id(1) == 0

    @pl.when(first)
    def _():
        s_ref[...] = s0_ref[...]

    _conv_load(ext_ref, qkv_ref, cst_ref, first)

    sm = sm_ref[...]
    beta_l = _sigmoid(sm)
    g_l = -jnp.exp(alog_ref[...]) * jax.nn.softplus(sm + dtb_ref[...])
    tri, strict, eye = _tri_masks(c)
    gc_all = _cumsum_rows(g_l, tri)
    gc_t = gc_all.T
    eye_f = jnp.where(eye, 1.0, 0.0).astype(f32)
    n_fac = max(1, (c - 1).bit_length() - 1)

    hs = range(GDN_H)
    qb, kb, ks, decay, eg, gcs, nm, rhs = [], [], [], [], [], [], [], []
    for h in hs:
        q = _conv_cols(ext_ref, cw_ref, None, h * GDN_DK, GDN_DK, c)
        k = _conv_cols(ext_ref, cw_ref, None, GDN_QK + h * GDN_DK, GDN_DK, c)
        v = _conv_cols(ext_ref, cw_ref, None, 2 * GDN_QK + h * GDN_DV, GDN_DV, c)
        q = q * lax.rsqrt(jnp.sum(q * q, axis=-1, keepdims=True) + EPS) * (GDN_DK ** -0.5)
        k = k * lax.rsqrt(jnp.sum(k * k, axis=-1, keepdims=True) + EPS)
        beta = beta_l[:, SM_BETA + h:SM_BETA + h + 1]
        gc = gc_all[:, SM_DEC + h:SM_DEC + h + 1]
        d = jnp.exp(jnp.where(tri, gc - gc_t[SM_DEC + h:SM_DEC + h + 1, :], -jnp.inf))
        e = jnp.exp(gc)
        qb.append(q.astype(bf16))
        kb.append(k.astype(bf16))
        ks.append(k)
        decay.append(d)
        eg.append(e)
        gcs.append(gc)
        nm.append(jnp.where(strict, beta * d * _dot_nt(kb[h], kb[h]), 0.0))
        rhs.append(jnp.concatenate([v * beta, k * (beta * e)], axis=1))
        yield

    inv = [eye_f - nm[h] for h in hs]
    pw_s = [_split(nm[h]) for h in hs]
    pw_s = [_split(_dot3(pw_s[h], pw_s[h])) for h in hs]
    yield
    for i in range(n_fac):
        inv = [inv[h] + _dot3(_split(inv[h]), pw_s[h]) for h in hs]
        yield
        if i + 1 < n_fac:
            pw_s = [_split(_dot3(pw_s[h], pw_s[h])) for h in hs]
            yield
    sol = []
    for h in hs:
        inv_hi, inv_lo = _split(inv[h])
        rb = rhs[h].astype(bf16)
        sol.append(_dot(inv_hi, rb) + _dot(inv_lo, rb))
    yield

    s = [s_ref[h] for h in hs]
    sb = [s[h].astype(bf16) for h in hs]
    wb = [(sol[h][:, 0:GDN_DV] - _dot(sol[h][:, GDN_DV:].astype(bf16), sb[h])).astype(bf16) for h in hs]
    yield
    qk = [(_dot_nt(qb[h], kb[h]) * decay[h]).astype(bf16) for h in hs]
    yield
    o = [_dot(qb[h], sb[h]) * eg[h] + _dot(qk[h], wb[h]) for h in hs]
    yield
    for h in hs:
        g_last = gcs[h][c - 1:c]
        kd = (ks[h] * jnp.exp(g_last - gcs[h])).astype(bf16)
        s_ref[h] = s[h] * jnp.exp(g_last) + _dot_tn(kd, wb[h])
    yield
    for h in hs:
        sl = slice(h * GDN_DV, (h + 1) * GDN_DV)
        o_ref[:, sl] = (_rms(o[h], nw_ref[...]) * _silu(z_ref[:, sl].astype(f32))).astype(bf16)
        yield

    _conv_carry(ext_ref, c)


def _tok(width, off):
    return pl.BlockSpec((None, CHUNK, width), lambda bi, ti: (bi, ti, _blk(off, width)))


def _per_seq(*shape):
    return pl.BlockSpec((None,) + shape, lambda bi, ti: (bi,) + (0,) * len(shape))


def _const(*shape):
    return pl.BlockSpec(shape, lambda bi, ti: (0,) * len(shape))


N_GDN_IN, N_GLA_IN, N_SSD_IN = 9, 9, 11
_DONE = object()


def _mixers_kernel(*refs):
    i0, i1, i2 = N_GDN_IN, N_GDN_IN + N_GLA_IN, N_GDN_IN + N_GLA_IN + N_SSD_IN
    gdn_in, gla_in, ssd_in = refs[:i0], refs[i0:i1], refs[i1:i2]
    oa_ref, sa_ref, ob_ref, sb_ref, oc_ref, sc_ref, ext_a_ref, st_b_ref, ext_c_ref = refs[i2:]
    live = [_gdn_kernel(*gdn_in, oa_ref, sa_ref, ext_a_ref),
            _gla_kernel(*gla_in, ob_ref, sb_ref, st_b_ref),
            _ssd_kernel(*ssd_in, oc_ref, sc_ref, ext_c_ref)]
    while live:
        for gen in list(live):
            if next(gen, _DONE) is _DONE:
                live.remove(gen)


GLA_SUB = 16


def _gla_kernel(q_ref, k_ref, v_ref, r_ref, sm_ref, wgate_ref, bgate_ref, nw_ref, s0_ref,
                o_ref, so_ref, st_ref):
    c = q_ref.shape[0]
    ti = pl.program_id(1)

    @pl.when(ti == 0)
    def _():
        for h in range(GLA_H):
            st_ref[h] = s0_ref[h].T

    sm = sm_ref[...]
    lane = lax.broadcasted_iota(jnp.int32, sm.shape, 1)
    lr = jnp.where((lane >= SM_LR) & (lane < SM_LR + GLA_RANK), sm, 0.0).astype(bf16)
    la = jax.nn.log_sigmoid(_dot(lr, wgate_ref[...]) + bgate_ref[...]) / GLA_TAU
    tri, _, _ = _tri_masks(c)
    b_all = _cumsum_rows(la, tri)

    col = lax.broadcasted_iota(jnp.int32, (GLA_SUB, c), 1)
    row = lax.broadcasted_iota(jnp.int32, (GLA_SUB, c), 0)
    hs = range(GLA_H)
    qs, ks, vbs, bs, atts = [], [], [], [], []
    for h in hs:
        sk = slice(h * GLA_DK, (h + 1) * GLA_DK)
        q = q_ref[:, sk].astype(f32) * (GLA_DK ** -0.5)
        k = k_ref[:, sk].astype(f32)
        b = b_all[:, sk]
        qs.append(q)
        ks.append(k)
        bs.append(b)
        vbs.append(v_ref[:, h * GLA_DV:(h + 1) * GLA_DV].astype(bf16))
        att_rows = []
        for i in range(c // GLA_SUB):
            lo = i * GLA_SUB
            b_i = b[lo:lo + GLA_SUB]
            q_i = q[lo:lo + GLA_SUB]
            b_top = b[lo:lo + 1]
            att = jnp.zeros((GLA_SUB, c), f32)
            if i > 0:
                q_t = (q_i * jnp.exp(b_i - b_top)).astype(bf16)
                k_t = (k * jnp.exp(jnp.minimum(b_top - b, 0.0))).astype(bf16)
                att = jnp.where(col < lo, _dot_nt(q_t, k_t), 0.0)
            for sl in range(GLA_SUB):
                sidx = lo + sl
                e = jnp.exp(jnp.minimum(b_i - b[sidx:sidx + 1], 0.0))
                p = jnp.sum(q_i * k[sidx:sidx + 1] * e, axis=1, keepdims=True)
                att = jnp.where((col == sidx) & (row >= sl), p, att)
            att_rows.append(att)
            yield
        atts.append(jnp.concatenate(att_rows, axis=0).astype(bf16))

    sts = [st_ref[h] for h in hs]
    os_ = [_dot(atts[h], vbs[h]) + _dot_nt((qs[h] * jnp.exp(bs[h])).astype(bf16), sts[h].astype(bf16))
           for h in hs]
    yield
    for h in hs:
        b_last = bs[h][c - 1:c]
        kd = (ks[h] * jnp.exp(b_last - bs[h])).astype(bf16)
        st_ref[h] = sts[h] * jnp.exp(b_last) + _dot_tn(vbs[h], kd)
        yield
    for h in hs:
        sv = slice(h * GLA_DV, (h + 1) * GLA_DV)
        o_ref[:, sv] = (_rms(os_[h], nw_ref[...]) * _silu(r_ref[:, sv].astype(f32))).astype(bf16)
        yield

    @pl.when(ti == pl.num_programs(1) - 1)
    def _():
        for h in hs:
            so_ref[h] = st_ref[h].T


def _ssd_kernel(z_ref, xbc_ref, sm_ref, cst_ref, cw_ref, cb_ref, avec_ref, dtb_ref, dvec_ref, nw_ref, h0_ref,
                o_ref, h_ref, ext_ref):
    c = xbc_ref.shape[0]
    first = pl.program_id(1) == 0

    @pl.when(first)
    def _():
        h_ref[...] = h0_ref[...]

    _conv_load(ext_ref, xbc_ref, cst_ref, first)

    dt_l = jax.nn.softplus(sm_ref[...] + dtb_ref[...])
    tri, _, _ = _tri_masks(c)
    ac_all = _cumsum_rows(dt_l * avec_ref[...], tri)
    ac_t2 = jnp.concatenate([ac_all, ac_all], axis=0).T

    lane = lax.broadcasted_iota(jnp.int32, (c, 2 * SSD_P), 1)
    rowi = lax.broadcasted_iota(jnp.int32, (c, 2 * SSD_P), 0)
    lo_half = lane < SSD_P
    tri2 = rowi >= jnp.where(lo_half, lane, lane - SSD_P)
    lane1 = lane[0:1]
    gs_ = range(SSD_G)
    ps_ = range(SSD_HG // 2)

    xs_, bmb, hg, cb2, ch = [], [], [], [], []
    for g in gs_:
        xs_.append(_conv_cols(ext_ref, cw_ref, cb_ref, g * SSD_GW, SSD_GW, c))
        bmb.append(_conv_cols(ext_ref, cw_ref, cb_ref, SSD_INNER + g * SSD_N, SSD_N, c).astype(bf16))
        cmb = _conv_cols(ext_ref, cw_ref, cb_ref, SSD_INNER + SSD_BC + g * SSD_N, SSD_N, c).astype(bf16)
        hg.append(h_ref[pl.ds(g * SSD_HG, SSD_HG)].reshape(SSD_GW, SSD_N))
        cb2.append(_dot_nt(cmb, jnp.concatenate([bmb[g], bmb[g]], axis=0)))
        ch.append(_dot_nt(cmb, hg[g].astype(bf16)))
        yield

    m2, rhs, e_col, xsc = {}, {}, {}, {}
    for g in gs_:
        for p in ps_:
            l0 = SM_DT + g * SSD_HG + 2 * p
            ps = slice(p * 2 * SSD_P, (p + 1) * 2 * SSD_P)
            ac_col = jnp.where(lo_half, ac_all[:, l0:l0 + 1], ac_all[:, l0 + 1:l0 + 2])
            ac_row = jnp.where(lane1 < SSD_P, ac_t2[l0:l0 + 1, :], ac_t2[l0 + 1:l0 + 2, :])
            decay = jnp.exp(jnp.where(tri2, ac_col - ac_row, -jnp.inf))
            m2[g, p] = (cb2[g] * decay).astype(bf16)
            dt2 = jnp.where(lo_half, dt_l[:, l0:l0 + 1], dt_l[:, l0 + 1:l0 + 2])
            xdt = xs_[g][:, ps] * dt2
            rhs[g, p] = jnp.concatenate([jnp.where(lo_half, xdt, 0.0), jnp.where(lo_half, 0.0, xdt)],
                                        axis=0).astype(bf16)
            e_col[g, p] = jnp.exp(ac_col)
            xsc[g, p] = xdt * jnp.exp(ac_col[c - 1:c] - ac_col)
            yield
    y2 = {gp: _dot(m2[gp], rhs[gp]) for gp in m2}
    yield

    for g in gs_:
        xsg = jnp.concatenate([xsc[g, p] for p in ps_], axis=1).astype(bf16)
        dh = _dot_tn(xsg, bmb[g])
        for j in range(SSD_HG):
            ln = SM_DT + g * SSD_HG + j
            h_ref[g * SSD_HG + j] = (hg[g][j * SSD_P:(j + 1) * SSD_P] * jnp.exp(ac_all[c - 1:c, ln:ln + 1])
                                     + dh[j * SSD_P:(j + 1) * SSD_P])
        yield
    for g in gs_:
        gs = slice(g * SSD_GW, (g + 1) * SSD_GW)
        y = jnp.concatenate([y2[g, p] + ch[g][:, p * 2 * SSD_P:(p + 1) * 2 * SSD_P] * e_col[g, p] for p in ps_],
                            axis=1)
        y = (y + dvec_ref[:, gs] * xs_[g]) * _silu(z_ref[:, gs].astype(f32))
        o_ref[:, gs] = _rms(y, nw_ref[:, gs]).astype(bf16)
        yield

    _conv_carry(ext_ref, c)


STEP_B = 8


def _conv_step(x_ref, c_ref, w_ref):
    w = w_ref[...]
    return c_ref[0] * w[0:1] + c_ref[1] * w[1:2] + c_ref[2] * w[2:3] + x_ref[...] * w[3:4]


def _expand_matrix(nb, width):
    r = lax.broadcasted_iota(jnp.int32, (nb, nb * width), 0)
    c = lax.broadcasted_iota(jnp.int32, (nb, nb * width), 1)
    return jnp.where((c >= r * width) & (c < (r + 1) * width), 1.0, 0.0).astype(bf16)


def _bcast_cols(x, e_mat):
    hi, lo = _split(x)
    return _dot_tn(hi, e_mat) + _dot_tn(lo, e_mat)


def _state_call(kernel, grid, in_specs, out_specs, out_shape, scratch, sem, name, args, prev_state):
    aliases = {}
    if prev_state is not None:
        in_specs = in_specs + [pl.BlockSpec(memory_space=pl.ANY)]
        args = args + (prev_state,)
        aliases = {len(args) - 1: len(out_shape) - 1}
        kernel = functools.partial(_drop_alias_ref, kernel, len(args) - 1)
    return pl.pallas_call(kernel, grid=grid, in_specs=in_specs, out_specs=out_specs, out_shape=out_shape,
                          scratch_shapes=scratch, input_output_aliases=aliases,
                          compiler_params=_cp(sem), name=name)(*args)


def _drop_alias_ref(kernel, idx, *refs):
    return kernel(*refs[:idx], *refs[idx + 1:])


def _gdn_step_kernel(alog_ref, dtb_ref, q_ref, k_ref, v_ref, z_ref, sm_ref, cq_ref, ck_ref, cv_ref,
                     wq_ref, wk_ref, wv_ref, nw_ref, s_ref, o_ref, so_ref):
    nb = q_ref.shape[0]
    qa = _silu(_conv_step(q_ref, cq_ref, wq_ref))
    ka = _silu(_conv_step(k_ref, ck_ref, wk_ref))
    va = _silu(_conv_step(v_ref, cv_ref, wv_ref))
    sm = sm_ref[...]
    z = z_ref[...]
    e_mat = _expand_matrix(nb, GDN_DV)
    for h in range(GDN_H):
        sl = slice(h * GDN_DK, (h + 1) * GDN_DK)
        q = qa[:, sl]
        k = ka[:, sl]
        v = va[:, sl]
        q = q * lax.rsqrt(jnp.sum(q * q, axis=-1, keepdims=True) + EPS) * (GDN_DK ** -0.5)
        k = k * lax.rsqrt(jnp.sum(k * k, axis=-1, keepdims=True) + EPS)
        beta = _sigmoid(sm[:, SM_BETA + h:SM_BETA + h + 1])
        g = -jnp.exp(_scalar_vec(alog_ref[h])) * jax.nn.softplus(sm[:, SM_DEC + h:SM_DEC + h + 1] + dtb_ref[h])
        eg = jnp.exp(g)
        qk = jnp.sum(q * k, axis=-1, keepdims=True)
        kb = _bcast_cols(k, e_mat)
        qb = _bcast_cols(q, e_mat)
        blk = lambda m, b: m[:, b * GDN_DV:(b + 1) * GDN_DV]
        ks = jnp.concatenate([jnp.sum(s_ref[b, h] * blk(kb, b), axis=0, keepdims=True) for b in range(nb)], axis=0)
        qs = jnp.concatenate([jnp.sum(s_ref[b, h] * blk(qb, b), axis=0, keepdims=True) for b in range(nb)], axis=0)
        w = beta * v - (beta * eg) * ks
        o = qs * eg + qk * w
        for b in range(nb):
            so_ref[b, h] = s_ref[b, h] * eg[b:b + 1] + blk(kb, b) * w[b:b + 1]
        o = _rms(o, nw_ref[...]) * _silu(z[:, sl])
        o_ref[:, sl] = o.astype(bf16)


def _gdn_step(proj, small, conv_t, states, prev, l, conv_w, a_log, dt_bias, norm_w):
    n = proj.shape[0]
    nb = STEP_B
    blk = lambda off: pl.BlockSpec((nb, GDN_QK), lambda i: (i, _blk(off, GDN_QK)))
    cst = lambda j: pl.BlockSpec((CONV_W - 1, nb, GDN_QK), lambda i: (0, i, j))
    cw = lambda j: pl.BlockSpec((CONV_W, GDN_QK), lambda i: (0, j))
    smem = pl.BlockSpec(memory_space=pltpu.SMEM)
    st = pl.BlockSpec((None, nb, GDN_H, GDN_DK, GDN_DV), lambda i: (l, i, 0, 0, 0))
    return _state_call(
        _gdn_step_kernel, (n // nb,),
        [smem, smem, blk(P_QKV_A), blk(P_QKV_A + GDN_QK), blk(P_QKV_A + 2 * GDN_QK), blk(P_Z_A),
         pl.BlockSpec((nb, LANE), lambda i: (i, 0)),
         cst(0), cst(1), cst(2), cw(0), cw(1), cw(2),
         pl.BlockSpec((1, GDN_DV), lambda i: (0, 0)), st],
        [pl.BlockSpec((nb, GDN_V), lambda i: (i, 0)), st],
        [jax.ShapeDtypeStruct((n, GDN_V), bf16), jax.ShapeDtypeStruct(states.shape, f32)],
        [], ("parallel",), "gdn_step",
        (a_log, dt_bias, proj, proj, proj, proj, small, conv_t, conv_t, conv_t, conv_w, conv_w, conv_w, norm_w, states),
        prev)


def _gla_step_kernel(q_ref, k_ref, v_ref, r_ref, sm_ref, wgate_ref, bgate_ref, nw_ref, s_ref,
                     o_ref, so_ref):
    nb = q_ref.shape[0]
    sm = sm_ref[...]
    lane = lax.broadcasted_iota(jnp.int32, sm.shape, 1)
    lr = jnp.where((lane >= SM_LR) & (lane < SM_LR + GLA_RANK), sm, 0.0).astype(bf16)
    la_all = jax.nn.log_sigmoid(_dot(lr, wgate_ref[...]) + bgate_ref[...]) / GLA_TAU
    qa = q_ref[...] * (GLA_DK ** -0.5)
    ka = k_ref[...]
    va = v_ref[...]
    r = r_ref[...]
    for h in range(GLA_H):
        sk = slice(h * GLA_DK, (h + 1) * GLA_DK)
        sv = slice(h * GLA_DV, (h + 1) * GLA_DV)
        q = qa[:, sk]
        k = ka[:, sk]
        v = va[:, sv]
        e = jnp.exp(la_all[:, sk])
        qk = jnp.sum(q * k, axis=-1, keepdims=True)
        e_t = e.T
        k_t = k.T
        qe_t = (q * e).T
        rows = []
        for b in range(nb):
            s = s_ref[b, h]
            vrow = v[b:b + 1]
            rows.append(qk[b:b + 1] * vrow + jnp.sum(s * qe_t[:, b:b + 1], axis=0, keepdims=True))
            so_ref[b, h] = s * e_t[:, b:b + 1] + k_t[:, b:b + 1] * vrow
        o = jnp.concatenate(rows, axis=0)
        o = _rms(o, nw_ref[...]) * _silu(r[:, sv])
        o_ref[:, sv] = o.astype(bf16)


def _gla_step(proj, small, states, prev, l, wgate_pad, bgate, norm_w):
    n = proj.shape[0]
    nb = STEP_B
    st = pl.BlockSpec((None, nb, GLA_H, GLA_DK, GLA_DV), lambda i: (l, i, 0, 0, 0))
    return _state_call(
        _gla_step_kernel, (n // nb,),
        [pl.BlockSpec((nb, GLA_QK), lambda i: (i, _blk(P_Q_B, GLA_QK))),
         pl.BlockSpec((nb, GLA_QK), lambda i: (i, _blk(P_K_B, GLA_QK))),
         pl.BlockSpec((nb, GLA_V), lambda i: (i, _blk(P_V_B, GLA_V))),
         pl.BlockSpec((nb, GLA_V), lambda i: (i, _blk(P_R_B, GLA_V))),
         pl.BlockSpec((nb, LANE), lambda i: (i, 0)),
         pl.BlockSpec((LANE, GLA_QK), lambda i: (0, 0)),
         pl.BlockSpec((1, GLA_QK), lambda i: (0, 0)),
         pl.BlockSpec((1, GLA_DV), lambda i: (0, 0)), st],
        [pl.BlockSpec((nb, GLA_V), lambda i: (i, 0)), st],
        [jax.ShapeDtypeStruct((n, GLA_V), bf16), jax.ShapeDtypeStruct(states.shape, f32)],
        [], ("parallel",), "gla_step",
        (proj, proj, proj, proj, small, wgate_pad, bgate, norm_w, states),
        prev)


def _ssd_step_kernel(alog_ref, dtb_ref, z_ref, xbc_ref, sm_ref, cst_ref, cw_ref, cb_ref, dvec_ref, nw_ref, h_ref,
                     o_ref, ho_ref):
    nb = xbc_ref.shape[0]
    xbc = _silu(_conv_step(xbc_ref, cst_ref, cw_ref) + cb_ref[...])
    sm_t = sm_ref[...].T
    e_mat = _expand_matrix(nb, SSD_N)
    lane_b = lax.broadcasted_iota(jnp.int32, (SSD_P, nb), 1)
    for g in range(SSD_G):
        gs = slice(g * SSD_GW, (g + 1) * SSD_GW)
        bm = xbc[:, SSD_INNER + g * SSD_N:SSD_INNER + (g + 1) * SSD_N]
        cm_t = xbc[:, SSD_INNER + SSD_BC + g * SSD_N:SSD_INNER + SSD_BC + (g + 1) * SSD_N].T
        cb_row = jnp.sum(cm_t * bm.T, axis=0, keepdims=True)
        hg = h_ref[:, pl.ds(g * SSD_HG, SSD_HG)].reshape(nb * SSD_GW, SSD_N)
        ch = _dot(hg.astype(bf16), cm_t.astype(bf16))
        pairs = []
        for jp in range(SSD_HG // 2):
            x_t = xbc[:, g * SSD_GW + jp * LANE:g * SSD_GW + (jp + 1) * LANE].T
            halves = []
            for jj in range(2):
                j = 2 * jp + jj
                hd = g * SSD_HG + j
                dt_row = jax.nn.softplus(sm_t[SM_DT + hd:SM_DT + hd + 1, :] + dtb_ref[hd])
                ea_row = jnp.exp(dt_row * (-jnp.exp(_scalar_vec(alog_ref[hd]))))
                xdt_t = x_t[jj * SSD_P:(jj + 1) * SSD_P] * dt_row
                hi, lo = _split(xdt_t)
                xb = _dot(hi, e_mat) + _dot(lo, e_mat)
                yh = jnp.zeros((SSD_P, nb), f32)
                for b in range(nb):
                    r0 = (b * SSD_HG + j) * SSD_P
                    yh = jnp.where(lane_b == b, ch[r0:r0 + SSD_P], yh)
                    ho_ref[b, hd] = (h_ref[b, hd] * ea_row[:, b:b + 1]
                                     + xb[:, b * SSD_N:(b + 1) * SSD_N] * bm[b:b + 1])
                halves.append(yh * ea_row + cb_row * xdt_t)
            pairs.append(jnp.concatenate(halves, axis=0).T)
        y = jnp.concatenate(pairs, axis=1)
        y = (y + dvec_ref[:, gs] * xbc[:, gs]) * _silu(z_ref[:, gs])
        o_ref[:, gs] = _rms(y, nw_ref[:, gs]).astype(bf16)


def _ssd_step(proj, small, conv_t, states, prev, l, conv_w, conv_b, a_log, dt_bias, dvec, norm_w):
    n = proj.shape[0]
    nb = STEP_B
    smem = pl.BlockSpec(memory_space=pltpu.SMEM)
    const = lambda shape: pl.BlockSpec(shape, lambda i: (0,) * len(shape))
    st = pl.BlockSpec((None, nb, SSD_H, SSD_P, SSD_N), lambda i: (l, i, 0, 0, 0))
    return _state_call(
        _ssd_step_kernel, (n // nb,),
        [smem, smem,
         pl.BlockSpec((nb, SSD_INNER), lambda i: (i, _blk(P_Z_C, SSD_INNER))),
         pl.BlockSpec((nb, SSD_CONV), lambda i: (i, _blk(P_XBC, SSD_CONV))),
         pl.BlockSpec((nb, LANE), lambda i: (i, 0)),
         pl.BlockSpec((CONV_W - 1, nb, SSD_CONV), lambda i: (0, i, 0)),
         const((CONV_W, SSD_CONV)), const((1, SSD_CONV)), const((1, SSD_INNER)), const((1, SSD_INNER)), st],
        [pl.BlockSpec((nb, SSD_INNER), lambda i: (i, 0)), st],
        [jax.ShapeDtypeStruct((n, SSD_INNER), bf16), jax.ShapeDtypeStruct(states.shape, f32)],
        [], ("parallel",), "ssd_step",
        (a_log, dt_bias, proj, proj, small, conv_t, conv_w, conv_b, dvec, norm_w, states),
        prev)


def _mixout_kernel(oa_ref, ob_ref, oc_ref, ga_ref, gb_ref, gc_ref, wa_ref, wb_ref, wc_ref, wo_ref,
                   x_ref, gt_ref, o_ref, acc_ref):
    j = pl.program_id(2)
    m = (_sigmoid(ga_ref[...].astype(f32)) * _dot(oa_ref[...], wa_ref[...])
         + _sigmoid(gb_ref[...].astype(f32)) * _dot(ob_ref[...], wb_ref[...])
         + _sigmoid(gc_ref[...].astype(f32)) * _dot(oc_ref[...], wc_ref[...]))
    part = _dot(m.astype(bf16), wo_ref[...])

    @pl.when(j == 0)
    def _():
        acc_ref[...] = part

    @pl.when(j > 0)
    def _():
        acc_ref[...] += part

    @pl.when(j == pl.num_programs(2) - 1)
    def _():
        o_ref[...] = x_ref[...] + gt_ref[...] * acc_ref[...]


def _mixout(oa, ob, oc, proj, x, mod, wa, wb, wc, wo, l, per_row, tm):
    bx, tx, dm = x.shape
    tn = 512
    gate = lambda k: pl.BlockSpec((None, tm, tn), lambda b, i, j: (b, i, _blk(P_GATES + k * D_MODEL, tn) + j))
    act = lambda w: pl.BlockSpec((None, tm, w), lambda b, i, j: (b, i, 0))
    wsp = lambda w: pl.BlockSpec((None, w, tn), lambda b, i, j: (l, 0, j))
    return pl.pallas_call(
        _mixout_kernel,
        grid=(bx, tx // tm, dm // tn),
        in_specs=[act(GDN_V), act(GLA_V), act(SSD_INNER), gate(0), gate(1), gate(2),
                  wsp(GDN_V), wsp(GLA_V), wsp(SSD_INNER),
                  pl.BlockSpec((None, tn, dm), lambda b, i, j: (l, j, 0)),
                  pl.BlockSpec((None, tm, dm), lambda b, i, j: (b, i, 0)),
                  _mod_spec(per_row, tm, 5)],
        out_specs=pl.BlockSpec((None, tm, dm), lambda b, i, j: (b, i, 0)),
        out_shape=jax.ShapeDtypeStruct(x.shape, f32),
        scratch_shapes=[pltpu.VMEM((tm, dm), f32)],
        compiler_params=_cp(("parallel", "parallel", "arbitrary")),
        name="mixout",
    )(oa, ob, oc, proj, proj, proj, wa, wb, wc, wo, x, mod)


def _permute_w_in(w):
    nl, d, _ = w.shape
    wt = jnp.swapaxes(w, 1, 2)
    rows = lambda name: wt[:, W_IN_SRC[name][0]:W_IN_SRC[name][1]]
    main = jnp.concatenate([rows(name) for name in MAIN_ORDER], axis=1).astype(bf16)
    small = jnp.concatenate([rows(name) for name in SMALL_ORDER], axis=1)
    small = jnp.pad(small, ((0, 0), (0, LANE - small.shape[1]), (0, 0))).astype(bf16)
    return main, small


def _ffn_weights(wg, wu, wd):
    f = wg.shape[2]
    cut = f - f % FF_TILE
    w = tuple(a.astype(bf16) for a in (wg, wu, wd))
    if cut == f:
        return w
    return w + (w[0][:, :, cut:], w[1][:, :, cut:], w[2][:, cut:, :])


def _lane_vec(v, lo):
    return jnp.zeros((1, LANE), f32).at[0, lo:lo + v.shape[0]].set(v)


def _layer_params(l, p):
    row = lambda a: a[l].reshape(1, -1)
    wgate = jnp.zeros((LANE, GLA_QK), f32).at[SM_LR:SM_LR + GLA_RANK].set(p["gla_w_gate"][l]).astype(bf16)
    return dict(
        norm1=row(p["norm1"]), norm2=row(p["norm2"]), norm3=row(p["norm3"]),
        gdn_conv_w=p["gdn_conv_w"][l], gdn_a_log=p["gdn_a_log"][l], gdn_dt_bias=p["gdn_dt_bias"][l],
        gdn_alog_l=_lane_vec(p["gdn_a_log"][l], SM_DEC), gdn_dtb_l=_lane_vec(p["gdn_dt_bias"][l], SM_DEC),
        gdn_norm_w=row(p["gdn_norm_w"]),
        gla_wgate=wgate, gla_bgate=row(p["gla_b_gate"]), gla_norm_w=row(p["gla_norm_w"]),
        ssd_conv_w=p["ssd_conv_w"][l], ssd_conv_b=row(p["ssd_conv_b"]), ssd_a_log=p["ssd_a_log"][l],
        ssd_dt_bias=p["ssd_dt_bias"][l],
        ssd_avec_l=_lane_vec(-jnp.exp(p["ssd_a_log"][l]), SM_DT), ssd_dtb_l=_lane_vec(p["ssd_dt_bias"][l], SM_DT),
        ssd_dvec=jnp.repeat(p["ssd_d"][l], SSD_P).reshape(1, -1),
        ssd_norm_w=row(p["ssd_norm_w"]),
    )


def _stacked_weights(p):
    w_in, w_in_small = _permute_w_in(p["w_in"])
    return dict(
        f1=_ffn_weights(p["ffn1_wg"], p["ffn1_wu"], p["ffn1_wd"]),
        f2=_ffn_weights(p["ffn2_wg"], p["ffn2_wu"], p["ffn2_wd"]),
        w_in=w_in, w_in_small=w_in_small,
        wa=p["w_branch_gdn"].astype(bf16), wb=p["w_branch_gla"].astype(bf16),
        wc=p["w_branch_ssd"].astype(bf16), w_out=p["w_out"].astype(bf16),
    )


def _new_conv_state(buf, raw):
    t = raw.shape[1]
    k = CONV_W - 1
    if t >= k:
        return raw[:, t - k:]
    return jnp.concatenate([buf[:, t:], raw], axis=1)


def _mixer_prompt(proj, small, lp, st):
    gdn_conv, s_gdn, s_gla, ssd_conv, s_ssd = st
    b, t, _ = proj.shape
    c = CHUNK
    sm_spec = pl.BlockSpec((None, c, LANE), lambda bi, ti: (bi, ti, 0))
    st_a, st_b, st_c = _per_seq(GDN_H, GDN_DK, GDN_DV), _per_seq(GLA_H, GLA_DK, GLA_DV), _per_seq(SSD_H, SSD_P, SSD_N)
    gdn_specs = [_tok(GDN_CONV, P_QKV_A), _tok(GDN_V, P_Z_A), sm_spec, _per_seq(CONV_W - 1, GDN_CONV),
                 _const(CONV_W, GDN_CONV), _const(1, LANE), _const(1, LANE), _const(1, GDN_DV), st_a]
    gdn_args = (proj, proj, small, gdn_conv, lp["gdn_conv_w"], lp["gdn_alog_l"], lp["gdn_dtb_l"],
                lp["gdn_norm_w"], s_gdn)
    gla_specs = [_tok(GLA_QK, P_Q_B), _tok(GLA_QK, P_K_B), _tok(GLA_V, P_V_B), _tok(GLA_V, P_R_B), sm_spec,
                 _const(LANE, GLA_QK), _const(1, GLA_QK), _const(1, GLA_DV), st_b]
    gla_args = (proj, proj, proj, proj, small, lp["gla_wgate"], lp["gla_bgate"], lp["gla_norm_w"], s_gla)
    ssd_specs = [_tok(SSD_INNER, P_Z_C), _tok(SSD_CONV, P_XBC), sm_spec, _per_seq(CONV_W - 1, SSD_CONV),
                 _const(CONV_W, SSD_CONV), _const(1, SSD_CONV), _const(1, LANE), _const(1, LANE),
                 _const(1, SSD_INNER), _const(1, SSD_INNER), st_c]
    ssd_args = (proj, proj, small, ssd_conv, lp["ssd_conv_w"], lp["ssd_conv_b"], lp["ssd_avec_l"],
                lp["ssd_dtb_l"], lp["ssd_dvec"], lp["ssd_norm_w"], s_ssd)
    assert (len(gdn_specs), len(gla_specs), len(ssd_specs)) == (N_GDN_IN, N_GLA_IN, N_SSD_IN)
    out_tok = lambda width: pl.BlockSpec((None, c, width), lambda bi, ti: (bi, ti, 0))
    oa, s_gdn_n, ob, s_gla_n, oc, s_ssd_n = pl.pallas_call(
        _mixers_kernel,
        grid=(b, t // c),
        in_specs=gdn_specs + gla_specs + ssd_specs,
        out_specs=[out_tok(GDN_V), st_a, out_tok(GLA_V), st_b, out_tok(SSD_INNER), st_c],
        out_shape=[jax.ShapeDtypeStruct((b, t, GDN_V), bf16), jax.ShapeDtypeStruct(s_gdn.shape, f32),
                   jax.ShapeDtypeStruct((b, t, GLA_V), bf16), jax.ShapeDtypeStruct(s_gla.shape, f32),
                   jax.ShapeDtypeStruct((b, t, SSD_INNER), bf16), jax.ShapeDtypeStruct(s_ssd.shape, f32)],
        scratch_shapes=[pltpu.VMEM((8 + c, GDN_CONV), f32), pltpu.VMEM((GLA_H, GLA_DV, GLA_DK), f32),
                        pltpu.VMEM((8 + c, SSD_CONV), f32)],
        compiler_params=_cp(("parallel", "arbitrary")),
        name="mixers_prompt",
    )(*gdn_args, *gla_args, *ssd_args)
    gdn_conv_n = _new_conv_state(gdn_conv, proj[:, :, P_QKV_A:P_QKV_A + GDN_CONV]).astype(f32)
    ssd_conv_n = _new_conv_state(ssd_conv, proj[:, :, P_XBC:P_XBC + SSD_CONV]).astype(f32)
    return (oa, ob, oc), (gdn_conv_n, s_gdn_n, s_gla_n, ssd_conv_n, s_ssd_n)


def _mixer_sample(proj, small, lp, l, states, prev):
    n = proj.shape[1]
    p2 = proj.reshape(n, P_TOTAL)
    s2 = small.reshape(n, LANE)
    gdn_conv, ssd_conv = states[0][l], states[3][l]
    gct = jnp.swapaxes(gdn_conv, 0, 1)
    sct = jnp.swapaxes(ssd_conv, 0, 1)
    pv = (None,) * 5 if prev is None else prev
    oa, s_gdn_n = _gdn_step(p2, s2, gct, states[1], pv[1], l, lp["gdn_conv_w"], lp["gdn_a_log"],
                            lp["gdn_dt_bias"], lp["gdn_norm_w"])
    ob, s_gla_n = _gla_step(p2, s2, states[2], pv[2], l, lp["gla_wgate"], lp["gla_bgate"], lp["gla_norm_w"])
    oc, s_ssd_n = _ssd_step(p2, s2, sct, states[4], pv[4], l, lp["ssd_conv_w"], lp["ssd_conv_b"], lp["ssd_a_log"],
                            lp["ssd_dt_bias"], lp["ssd_dvec"], lp["ssd_norm_w"])
    raw = p2.reshape(n, 1, P_TOTAL)
    gdn_conv_n = _new_conv_state(gdn_conv, raw[:, :, P_QKV_A:P_QKV_A + GDN_CONV])
    ssd_conv_n = _new_conv_state(ssd_conv, raw[:, :, P_XBC:P_XBC + SSD_CONV])
    outs = tuple(o.reshape(1, n, -1) for o in (oa, ob, oc))
    return outs, (gdn_conv_n, s_gdn_n, s_gla_n, ssd_conv_n, s_ssd_n)


def _trunk(x, mods, lps, sw, states, per_row, tm, tm_in, final_w):
    nl = len(lps)
    per_layer = []
    prev = None
    for l in range(nl):
        lp, mod = lps[l], mods[l]
        last = l == nl - 1
        x = _ffn(x, mod, 0, lp["norm1"], sw["f1"], l, lp["norm1"], per_row, tm, False)
        proj, small = _inproj(x, mod, lp["norm2"], sw["w_in"], sw["w_in_small"], l, per_row, tm_in,
                              f32 if per_row else bf16)
        if per_row:
            (oa, ob, oc), st = _mixer_sample(proj, small, lp, l, states, prev)
            prev = st
        else:
            (oa, ob, oc), st = _mixer_prompt(proj, small, lp, tuple(s[l] for s in states))
        per_layer.append(st)
        x = _mixout(oa, ob, oc, proj, x, mod, sw["wa"], sw["wb"], sw["wc"], sw["w_out"], l, per_row, tm)
        x = _ffn(x, mod, 6, lp["norm3"], sw["f2"], l, final_w if last else lp["norm3"], per_row, tm, last)
    stack = lambda i: jnp.stack([st[i] for st in per_layer])
    if per_row:
        new_states = (stack(0), prev[1], prev[2], stack(3), prev[4])
    else:
        new_states = tuple(stack(i) for i in range(5))
    return x, new_states


def kernel(x_prompt, x_sample, state_gdn_conv, state_gdn, state_gla, state_ssd_conv, state_ssd, c_prompt, c_sample, w_ada, b_ada, norm1, norm2, norm3, ffn1_wg, ffn1_wu, ffn1_wd, ffn2_wg, ffn2_wu, ffn2_wd, w_in, gdn_conv_w, gdn_a_log, gdn_dt_bias, gdn_norm_w, gla_w_gate, gla_b_gate, gla_norm_w, ssd_conv_w, ssd_conv_b, ssd_a_log, ssd_dt_bias, ssd_d, ssd_norm_w, w_branch_gdn, w_branch_gla, w_branch_ssd, w_out, final_norm):
    p = dict(norm1=norm1, norm2=norm2, norm3=norm3,
             ffn1_wg=ffn1_wg, ffn1_wu=ffn1_wu, ffn1_wd=ffn1_wd, ffn2_wg=ffn2_wg, ffn2_wu=ffn2_wu, ffn2_wd=ffn2_wd,
             w_in=w_in, gdn_conv_w=gdn_conv_w, gdn_a_log=gdn_a_log, gdn_dt_bias=gdn_dt_bias, gdn_norm_w=gdn_norm_w,
             gla_w_gate=gla_w_gate, gla_b_gate=gla_b_gate, gla_norm_w=gla_norm_w,
             ssd_conv_w=ssd_conv_w, ssd_conv_b=ssd_conv_b, ssd_a_log=ssd_a_log, ssd_dt_bias=ssd_dt_bias,
             ssd_d=ssd_d, ssd_norm_w=ssd_norm_w,
             w_branch_gdn=w_branch_gdn, w_branch_gla=w_branch_gla, w_branch_ssd=w_branch_ssd, w_out=w_out)
    nl = w_ada.shape[0]
    bp, tp, dm = x_prompt.shape
    bs = x_sample.shape[0]
    assert x_sample.shape[1] == 1 and tp % CHUNK == 0 and bs % STEP_B == 0 and dm == D_MODEL
    lps = [_layer_params(l, p) for l in range(nl)]
    sw = _stacked_weights(p)
    fw = final_norm.reshape(1, dm)

    rows = bp + bs
    rpad = -(-rows // 8) * 8
    c_all = jnp.concatenate([c_prompt, c_sample, jnp.zeros((rpad - rows, dm), f32)], axis=0)
    mod = _ada_mod(c_all, w_ada, b_ada)
    mod_p = [mod[l, :bp].reshape(bp, 1, N_MOD * dm) for l in range(nl)]
    mod_s = [mod[l, bp:rows].reshape(1, bs, N_MOD * dm) for l in range(nl)]

    sample_states = (state_gdn_conv, state_gdn, state_gla, state_ssd_conv, state_ssd)
    prompt_states = tuple(jnp.zeros((s.shape[0], bp) + s.shape[2:], x_prompt.dtype) for s in sample_states)
    tm_p = 512 if tp % 512 == 0 else CHUNK
    tm_in = 1024 if tp % 1024 == 0 else tm_p
    y_p, st_p = _trunk(x_prompt, mod_p, lps, sw, prompt_states, False, tm_p, tm_in, fw)
    y_s, st_s = _trunk(x_sample.reshape(1, bs, dm), mod_s, lps, sw, sample_states, True, bs, bs, fw)
    return (y_p, y_s.reshape(bs, 1, dm)) + st_p + st_s
```

```python
import functools

import jax
import jax.numpy as jnp
from jax import lax
from jax.experimental import pallas as pl
from jax.experimental.pallas import tpu as pltpu

f32 = jnp.float32
bf16 = jnp.bfloat16
HI = lax.Precision.HIGHEST

EPS = 1e-6
D_MODEL = 2048
N_MOD = 9
CHUNK = 64
CONV_W = 4
GDN_H, GDN_DK, GDN_DV = 8, 128, 128
GLA_H, GLA_DK, GLA_DV, GLA_RANK, GLA_TAU = 4, 128, 256, 16, 16.0
SSD_H, SSD_P, SSD_G, SSD_N = 32, 64, 4, 128
SSD_HG = SSD_H // SSD_G
GDN_QK = GDN_H * GDN_DK
GDN_V = GDN_H * GDN_DV
GDN_CONV = 2 * GDN_QK + GDN_V
GLA_QK = GLA_H * GLA_DK
GLA_V = GLA_H * GLA_DV
SSD_INNER = SSD_H * SSD_P
SSD_BC = SSD_G * SSD_N
SSD_CONV = SSD_INNER + 2 * SSD_BC
SSD_GW = SSD_HG * SSD_P

LANE = 128
FF_TILE = 512
FF_SUB = 256

IN_SPLITS = (("qkv_a", GDN_CONV), ("z_a", GDN_V), ("beta", GDN_H), ("dec", GDN_H),
             ("q_b", GLA_QK), ("k_b", GLA_QK), ("v_b", GLA_V), ("lr", GLA_RANK), ("r_b", GLA_V),
             ("z_c", SSD_INNER), ("xbc", SSD_CONV), ("dt", SSD_H), ("gates", 3 * D_MODEL))
MAIN_ORDER = ("qkv_a", "xbc", "z_c", "gates", "z_a", "q_b", "k_b", "v_b", "r_b")
SMALL_ORDER = ("beta", "dec", "lr", "dt")


def _layout():
    src, off = {}, 0
    for name, w in IN_SPLITS:
        src[name] = (off, off + w)
        off += w
    main, small, d = {}, {}, 0
    for name in MAIN_ORDER:
        w = src[name][1] - src[name][0]
        main[name] = d
        d += w
    total = d
    d = 0
    for name in SMALL_ORDER:
        small[name] = d
        d += src[name][1] - src[name][0]
    assert d <= LANE
    return src, main, small, total


W_IN_SRC, P_MAIN, P_SM, P_TOTAL = _layout()
P_QKV_A, P_XBC, P_Z_C, P_GATES, P_Z_A = (P_MAIN[k] for k in ("qkv_a", "xbc", "z_c", "gates", "z_a"))
P_Q_B, P_K_B, P_V_B, P_R_B = (P_MAIN[k] for k in ("q_b", "k_b", "v_b", "r_b"))
SM_BETA, SM_DEC, SM_LR, SM_DT = (P_SM[k] for k in SMALL_ORDER)

VMEM_LIMIT = 56 * 1024 * 1024


def _cp(sem):
    return pltpu.CompilerParams(dimension_semantics=sem, vmem_limit_bytes=VMEM_LIMIT)


def _blk(off, width):
    assert off % width == 0, (off, width)
    return off // width


def _sigmoid(x):
    return 0.5 + 0.5 * jnp.tanh(0.5 * x)


def _silu(x):
    h = 0.5 * x
    return h + h * jnp.tanh(h)


def _rms(x, w):
    return x * lax.rsqrt(jnp.mean(x * x, axis=-1, keepdims=True) + EPS) * w


def _dot(a, b):
    return jnp.dot(a, b, preferred_element_type=f32)


def _dot_nt(a, b):
    return lax.dot_general(a, b, (((1,), (1,)), ((), ())), preferred_element_type=f32)


def _dot_tn(a, b):
    return lax.dot_general(a, b, (((0,), (0,)), ((), ())), preferred_element_type=f32)


def _split(x):
    hi = x.astype(bf16)
    return hi, (x - hi.astype(f32)).astype(bf16)


def _dot3(a, b):
    return _dot(a[0], b[0]) + (_dot(a[0], b[1]) + _dot(a[1], b[0]))


def _dot_hi(a, b):
    return jnp.dot(a, b, precision=HI, preferred_element_type=f32)


def _tri_masks(c):
    row = lax.broadcasted_iota(jnp.int32, (c, c), 0)
    col = lax.broadcasted_iota(jnp.int32, (c, c), 1)
    return row >= col, row > col, row == col


SUB = 8


def _row_time(idx, c):
    n = c // SUB
    assert n & (n - 1) == 0, "chunk / SUB must be a power of two"
    sh = n.bit_length() - 1
    return ((idx & (n - 1)) << 3) | (idx >> sh)


def _tri_masks_residue(c, cols=None):
    cols = c if cols is None else cols
    row = _row_time(lax.broadcasted_iota(jnp.int32, (c, cols), 0), c)
    col = _row_time(lax.broadcasted_iota(jnp.int32, (c, cols), 1) & (c - 1), c)
    return row >= col, row > col, row == col


def _load_residue(ref, c, sl=slice(None)):
    n = c // SUB
    return jnp.concatenate([ref[pl.ds(r, n, stride=SUB), sl] for r in range(SUB)], axis=0)


def _store_natural(nat_ref, x, c, lo):
    n = c // SUB
    for j in range(x.shape[1] // LANE):
        for r in range(SUB):
            nat_ref[lo // LANE + j, pl.ds(r, n, stride=SUB), :] = x[r * n:(r + 1) * n, j * LANE:(j + 1) * LANE]


def _read_natural(nat_ref, lo, width):
    return jnp.concatenate([nat_ref[lo // LANE + j] for j in range(width // LANE)], axis=1)


def _lane_col(x, idx):
    lane = lax.broadcasted_iota(jnp.int32, x.shape, 1)
    return jnp.sum(jnp.where(lane == idx, x, 0.0), axis=1, keepdims=True)


def _scalar_vec(s):
    return jnp.full((1, 1), s, f32)


def _ada_kernel(c_ref, w_ref, b_ref, o_ref):
    s = _silu(c_ref[...]).astype(bf16)
    o_ref[...] = _dot(s, w_ref[...].astype(bf16)) + b_ref[...]


def _ada_mod(c_all, w_ada, b_ada):
    nl, dm, n = w_ada.shape
    r = c_all.shape[0]
    tn = 1024
    return pl.pallas_call(
        _ada_kernel,
        grid=(nl, n // tn),
        in_specs=[pl.BlockSpec((r, dm), lambda l, j: (0, 0)),
                  pl.BlockSpec((None, dm, tn), lambda l, j: (l, 0, j)),
                  pl.BlockSpec((None, 1, tn), lambda l, j: (l, 0, j))],
        out_specs=pl.BlockSpec((None, r, tn), lambda l, j: (l, 0, j)),
        out_shape=jax.ShapeDtypeStruct((nl, r, n), f32),
        compiler_params=_cp(("arbitrary", "arbitrary")),
        name="ada_mod",
    )(c_all, w_ada, b_ada.reshape(nl, 1, n))


def _mod_spec(per_row, tm, chunk):
    if per_row:
        return pl.BlockSpec((None, tm, D_MODEL), lambda b, i, j: (b, i, chunk))
    return pl.BlockSpec((None, 1, D_MODEL), lambda b, i, j: (b, 0, chunk))


def _ffn_kernel(x_ref, sh_ref, sc_ref, gt_ref, nw_ref, fw_ref, wg_ref, wu_ref, wd_ref, *rest, final, has_tail):
    if has_tail:
        wgt_ref, wut_ref, wdt_ref, o_ref, h_ref, acc_ref = rest
    else:
        o_ref, h_ref, acc_ref = rest
    f = pl.program_id(2)
    last = pl.num_programs(2) - 1

    @pl.when(f == 0)
    def _():
        y = _rms(x_ref[...], nw_ref[...])
        h_ref[...] = (y * (1.0 + sc_ref[...]) + sh_ref[...]).astype(bf16)
        acc_ref[...] = jnp.zeros_like(acc_ref)

    def accumulate(wg, wu, wd):
        h = h_ref[...]
        width = wg.shape[1]
        sub = FF_SUB if width % FF_SUB == 0 else width
        parts = [(_dot(h, wg[:, s:s + sub]), _dot(h, wu[:, s:s + sub])) for s in range(0, width, sub)]
        acc = acc_ref[...]
        for i, (g, u) in enumerate(parts):
            acc = acc + _dot((_silu(g) * u).astype(bf16), wd[i * sub:(i + 1) * sub, :])
        acc_ref[...] = acc

    if has_tail:
        pl.when(f < last)(lambda: accumulate(wg_ref, wu_ref, wd_ref))
        pl.when(f == last)(lambda: accumulate(wgt_ref, wut_ref, wdt_ref))
    else:
        accumulate(wg_ref, wu_ref, wd_ref)

    @pl.when(f == last)
    def _():
        y = x_ref[...] + 0.5 * gt_ref[...] * acc_ref[...]
        if final:
            y = _rms(y, fw_ref[...])
        o_ref[...] = y


def _ffn(x, mod, k0, nw, w, l, fw, per_row, tm, final):
    bx, tx, dm = x.shape
    tf = FF_TILE
    n_full = w[0].shape[2] // tf
    has_tail = len(w) > 3
    full = lambda f: jnp.minimum(f, n_full - 1)
    w_specs = [pl.BlockSpec((None, dm, tf), lambda b, i, f: (l, 0, full(f))),
               pl.BlockSpec((None, dm, tf), lambda b, i, f: (l, 0, full(f))),
               pl.BlockSpec((None, tf, dm), lambda b, i, f: (l, full(f), 0))]
    if has_tail:
        ft = w[3].shape[2]
        w_specs += [pl.BlockSpec((None, dm, ft), lambda b, i, f: (l, 0, 0)),
                    pl.BlockSpec((None, dm, ft), lambda b, i, f: (l, 0, 0)),
                    pl.BlockSpec((None, ft, dm), lambda b, i, f: (l, 0, 0))]
    return pl.pallas_call(
        functools.partial(_ffn_kernel, final=final, has_tail=has_tail),
        grid=(bx, tx // tm, n_full + int(has_tail)),
        in_specs=[pl.BlockSpec((None, tm, dm), lambda b, i, f: (b, i, 0)),
                  _mod_spec(per_row, tm, k0), _mod_spec(per_row, tm, k0 + 1), _mod_spec(per_row, tm, k0 + 2),
                  pl.BlockSpec((1, dm), lambda b, i, f: (0, 0)),
                  pl.BlockSpec((1, dm), lambda b, i, f: (0, 0))] + w_specs,
        out_specs=pl.BlockSpec((None, tm, dm), lambda b, i, f: (b, i, 0)),
        out_shape=jax.ShapeDtypeStruct(x.shape, f32),
        scratch_shapes=[pltpu.VMEM((tm, dm), bf16), pltpu.VMEM((tm, dm), f32)],
        compiler_params=_cp(("parallel", "parallel", "arbitrary")),
        name="ffn",
    )(x, mod, mod, mod, nw, fw, *w)


def _inproj_kernel(x_ref, sh_ref, sc_ref, nw_ref, w_ref, ws_ref, o_ref, os_ref, h_ref):
    @pl.when(pl.program_id(2) == 0)
    def _():
        y = _rms(x_ref[...], nw_ref[...])
        h = (y * (1.0 + sc_ref[...]) + sh_ref[...]).astype(bf16)
        h_ref[...] = h
        os_ref[...] = _dot_nt(h, ws_ref[...])

    o_ref[...] = _dot_nt(h_ref[...], w_ref[...]).astype(o_ref.dtype)


def _inproj(x, mod, nw, w, ws, l, per_row, tm, out_dtype):
    bx, tx, dm = x.shape
    n = w.shape[1]
    tn = 1024
    return pl.pallas_call(
        _inproj_kernel,
        grid=(bx, tx // tm, n // tn),
        in_specs=[pl.BlockSpec((None, tm, dm), lambda b, i, j: (b, i, 0)),
                  _mod_spec(per_row, tm, 3), _mod_spec(per_row, tm, 4),
                  pl.BlockSpec((1, dm), lambda b, i, j: (0, 0)),
                  pl.BlockSpec((None, tn, dm), lambda b, i, j: (l, j, 0)),
                  pl.BlockSpec((None, LANE, dm), lambda b, i, j: (l, 0, 0))],
        out_specs=[pl.BlockSpec((None, tm, tn), lambda b, i, j: (b, i, j)),
                   pl.BlockSpec((None, tm, LANE), lambda b, i, j: (b, i, 0))],
        out_shape=[jax.ShapeDtypeStruct((bx, tx, n), out_dtype),
                   jax.ShapeDtypeStruct((bx, tx, LANE), f32)],
        scratch_shapes=[pltpu.VMEM((tm, dm), bf16)],
        compiler_params=_cp(("parallel", "parallel", "arbitrary")),
        name="inproj",
    )(x, mod, mod, nw, w, ws)


def _conv_load(ext_ref, x_ref, cst_ref, first):
    c = x_ref.shape[0]
    nblk = ext_ref.shape[0]

    @pl.when(first)
    def _():
        for j in range(nblk):
            ext_ref[j, pl.ds(5, CONV_W - 1), :] = cst_ref[:, j * LANE:(j + 1) * LANE]

    for j in range(nblk):
        ext_ref[j, pl.ds(8, c), :] = x_ref[:, j * LANE:(j + 1) * LANE].astype(f32)


def _conv_cols(ext_ref, w_ref, b_ref, lo, width, c):
    n = c // SUB
    cols = []
    for blk in range(lo // LANE, (lo + width) // LANE):
        sl = pl.ds(blk * LANE, LANE)
        w = w_ref[:, sl]
        taps = [ext_ref[blk, pl.ds(8 - (CONV_W - 1) + k, n, stride=SUB), :] for k in range(SUB + CONV_W - 1)]
        ys = []
        for r in range(SUB):
            y = taps[r] * w[0:1]
            for j in range(1, CONV_W):
                y = y + taps[r + j] * w[j:j + 1]
            ys.append(y)
        y = jnp.concatenate(ys, axis=0)
        if b_ref is not None:
            y = y + b_ref[:, sl]
        cols.append(_silu(y))
    return cols[0] if len(cols) == 1 else jnp.concatenate(cols, axis=1)


def _conv_carry(ext_ref, c):
    for j in range(ext_ref.shape[0]):
        ext_ref[j, pl.ds(5, CONV_W - 1), :] = ext_ref[j, pl.ds(8 + c - (CONV_W - 1), CONV_W - 1), :]


def _cumsum_rows(x, tri):
    return _dot_hi(jnp.where(tri, 1.0, 0.0).astype(f32), x)


def _gdn_kernel(qkv_ref, z_ref, sm_ref, cst_ref, cw_ref, alog_ref, dtb_ref, nw_ref, s0_ref,
                o_ref, s_ref, ext_ref, nat_ref):
    c = qkv_ref.shape[0]
    first = pl.program_id(1) == 0

    @pl.when(first)
    def _():
        s_ref[...] = s0_ref[...]

    _conv_load(ext_ref, qkv_ref, cst_ref, first)

    sm = _load_residue(sm_ref, c)
    beta_l = _sigmoid(sm)
    g_l = -jnp.exp(alog_ref[...]) * jax.nn.softplus(sm + dtb_ref[...])
    tri, strict, eye = _tri_masks_residue(c)
    gc_all = _cumsum_rows(g_l, tri)
    gc_t = gc_all.T
    eye_f = jnp.where(eye, 1.0, 0.0).astype(f32)
    n_fac = max(1, (c - 1).bit_length() - 1)

    hs = range(GDN_H)
    qb, kb, ks, decay, eg, gcs, nm, rhs = [], [], [], [], [], [], [], []
    for h in hs:
        q = _conv_cols(ext_ref, cw_ref, None, h * GDN_DK, GDN_DK, c)
        k = _conv_cols(ext_ref, cw_ref, None, GDN_QK + h * GDN_DK, GDN_DK, c)
        v = _conv_cols(ext_ref, cw_ref, None, 2 * GDN_QK + h * GDN_DV, GDN_DV, c)
        q = q * lax.rsqrt(jnp.sum(q * q, axis=-1, keepdims=True) + EPS) * (GDN_DK ** -0.5)
        k = k * lax.rsqrt(jnp.sum(k * k, axis=-1, keepdims=True) + EPS)
        beta = beta_l[:, SM_BETA + h:SM_BETA + h + 1]
        gc = gc_all[:, SM_DEC + h:SM_DEC + h + 1]
        d = jnp.exp(jnp.where(tri, gc - gc_t[SM_DEC + h:SM_DEC + h + 1, :], -jnp.inf))
        e = jnp.exp(gc)
        qb.append(q.astype(bf16))
        kb.append(k.astype(bf16))
        ks.append(k)
        decay.append(d)
        eg.append(e)
        gcs.append(gc)
        nm.append(jnp.where(strict, beta * d * _dot_nt(kb[h], kb[h]), 0.0))
        rhs.append(jnp.concatenate([v * beta, k * (beta * e)], axis=1))
        yield

    inv = [eye_f - nm[h] for h in hs]
    pw_s = [_split(nm[h]) for h in hs]
    pw_s = [_split(_dot3(pw_s[h], pw_s[h])) for h in hs]
    yield
    for i in range(n_fac):
        inv = [inv[h] + _dot3(_split(inv[h]), pw_s[h]) for h in hs]
        yield
        if i + 1 < n_fac:
            pw_s = [_split(_dot3(pw_s[h], pw_s[h])) for h in hs]
            yield
    sol = []
    for h in hs:
        inv_hi, inv_lo = _split(inv[h])
        rb = rhs[h].astype(bf16)
        sol.append(_dot(inv_hi, rb) + _dot(inv_lo, rb))
    yield

    s = [s_ref[h] for h in hs]
    sb = [s[h].astype(bf16) for h in hs]
    wb = [(sol[h][:, 0:GDN_DV] - _dot(sol[h][:, GDN_DV:].astype(bf16), sb[h])).astype(bf16) for h in hs]
    yield
    qk = [(_dot_nt(qb[h], kb[h]) * decay[h]).astype(bf16) for h in hs]
    yield
    o = [_dot(qb[h], sb[h]) * eg[h] + _dot(qk[h], wb[h]) for h in hs]
    yield
    for h in hs:
        g_last = gcs[h][c - 1:c]
        kd = (ks[h] * jnp.exp(g_last - gcs[h])).astype(bf16)
        s_ref[h] = s[h] * jnp.exp(g_last) + _dot_tn(kd, wb[h])
    yield
    for h in hs:
        sl = slice(h * GDN_DV, (h + 1) * GDN_DV)
        _store_natural(nat_ref, _rms(o[h], nw_ref[...]), c, h * GDN_DV)
        o_ref[:, sl] = (_read_natural(nat_ref, h * GDN_DV, GDN_DV) * _silu(z_ref[:, sl].astype(f32))).astype(bf16)
        yield

    _conv_carry(ext_ref, c)


def _tok(width, off):
    return pl.BlockSpec((None, CHUNK, width), lambda bi, ti: (bi, ti, _blk(off, width)))


def _per_seq(*shape):
    return pl.BlockSpec((None,) + shape, lambda bi, ti: (bi,) + (0,) * len(shape))


def _const(*shape):
    return pl.BlockSpec(shape, lambda bi, ti: (0,) * len(shape))


N_GDN_IN, N_GLA_IN, N_SSD_IN = 9, 9, 11
_DONE = object()


def _mixers_kernel(*refs):
    i0, i1, i2 = N_GDN_IN, N_GDN_IN + N_GLA_IN, N_GDN_IN + N_GLA_IN + N_SSD_IN
    gdn_in, gla_in, ssd_in = refs[:i0], refs[i0:i1], refs[i1:i2]
    oa_ref, sa_ref, ob_ref, sb_ref, oc_ref, sc_ref, ext_a_ref, st_b_ref, ext_c_ref, nat_a_ref, nat_c_ref = refs[i2:]
    live = [_gdn_kernel(*gdn_in, oa_ref, sa_ref, ext_a_ref, nat_a_ref),
            _gla_kernel(*gla_in, ob_ref, sb_ref, st_b_ref),
            _ssd_kernel(*ssd_in, oc_ref, sc_ref, ext_c_ref, nat_c_ref)]
    while live:
        for gen in list(live):
            if next(gen, _DONE) is _DONE:
                live.remove(gen)


GLA_SUB = 16


def _gla_kernel(q_ref, k_ref, v_ref, r_ref, sm_ref, wgate_ref, bgate_ref, nw_ref, s0_ref,
                o_ref, so_ref, st_ref):
    c = q_ref.shape[0]
    ti = pl.program_id(1)

    @pl.when(ti == 0)
    def _():
        for h in range(GLA_H):
            st_ref[h] = s0_ref[h].T

    sm = sm_ref[...]
    lane = lax.broadcasted_iota(jnp.int32, sm.shape, 1)
    lr = jnp.where((lane >= SM_LR) & (lane < SM_LR + GLA_RANK), sm, 0.0).astype(bf16)
    la = jax.nn.log_sigmoid(_dot(lr, wgate_ref[...]) + bgate_ref[...]) / GLA_TAU
    tri, _, _ = _tri_masks(c)
    b_all = _cumsum_rows(la, tri)

    col = lax.broadcasted_iota(jnp.int32, (GLA_SUB, c), 1)
    row = lax.broadcasted_iota(jnp.int32, (GLA_SUB, c), 0)
    hs = range(GLA_H)
    qs, ks, vbs, bs, atts = [], [], [], [], []
    for h in hs:
        sk = slice(h * GLA_DK, (h + 1) * GLA_DK)
        q = q_ref[:, sk].astype(f32) * (GLA_DK ** -0.5)
        k = k_ref[:, sk].astype(f32)
        b = b_all[:, sk]
        qs.append(q)
        ks.append(k)
        bs.append(b)
        vbs.append(v_ref[:, h * GLA_DV:(h + 1) * GLA_DV].astype(bf16))
        att_rows = []
        for i in range(c // GLA_SUB):
            lo = i * GLA_SUB
            b_i = b[lo:lo + GLA_SUB]
            q_i = q[lo:lo + GLA_SUB]
            b_top = b[lo:lo + 1]
            att = jnp.zeros((GLA_SUB, c), f32)
            if i > 0:
                q_t = (q_i * jnp.exp(b_i - b_top)).astype(bf16)
                k_t = (k * jnp.exp(jnp.minimum(b_top - b, 0.0))).astype(bf16)
                att = jnp.where(col < lo, _dot_nt(q_t, k_t), 0.0)
            for sl in range(GLA_SUB):
                sidx = lo + sl
                e = jnp.exp(jnp.minimum(b_i - b[sidx:sidx + 1], 0.0))
                p = jnp.sum(q_i * k[sidx:sidx + 1] * e, axis=1, keepdims=True)
                att = jnp.where((col == sidx) & (row >= sl), p, att)
            att_rows.append(att)
            yield
        atts.append(jnp.concatenate(att_rows, axis=0).astype(bf16))

    sts = [st_ref[h] for h in hs]
    os_ = [_dot(atts[h], vbs[h]) + _dot_nt((qs[h] * jnp.exp(bs[h])).astype(bf16), sts[h].astype(bf16))
           for h in hs]
    yield
    for h in hs:
        b_last = bs[h][c - 1:c]
        kd = (ks[h] * jnp.exp(b_last - bs[h])).astype(bf16)
        st_ref[h] = sts[h] * jnp.exp(b_last) + _dot_tn(vbs[h], kd)
        yield
    for h in hs:
        sv = slice(h * GLA_DV, (h + 1) * GLA_DV)
        o_ref[:, sv] = (_rms(os_[h], nw_ref[...]) * _silu(r_ref[:, sv].astype(f32))).astype(bf16)
        yield

    @pl.when(ti == pl.num_programs(1) - 1)
    def _():
        for h in hs:
            so_ref[h] = st_ref[h].T


def _ssd_kernel(z_ref, xbc_ref, sm_ref, cst_ref, cw_ref, cb_ref, avec_ref, dtb_ref, dvec_ref, nw_ref, h0_ref,
                o_ref, h_ref, ext_ref, nat_ref):
    c = xbc_ref.shape[0]
    assert c == SSD_P, "head pairs share a lane tile: the chunk must be as wide as a head"
    first = pl.program_id(1) == 0

    @pl.when(first)
    def _():
        h_ref[...] = h0_ref[...]

    _conv_load(ext_ref, xbc_ref, cst_ref, first)

    dt_l = jax.nn.softplus(_load_residue(sm_ref, c) + dtb_ref[...])
    tri, _, _ = _tri_masks_residue(c)
    ac_all = _cumsum_rows(dt_l * avec_ref[...], tri)
    ac_t2 = jnp.concatenate([ac_all, ac_all], axis=0).T

    lane = lax.broadcasted_iota(jnp.int32, (c, 2 * SSD_P), 1)
    lo_half = lane < SSD_P
    tri2, _, _ = _tri_masks_residue(c, 2 * SSD_P)
    lane1 = lane[0:1]
    gs_ = range(SSD_G)
    ps_ = range(SSD_HG // 2)

    xs_, bmb, hg, cb2, ch = [], [], [], [], []
    for g in gs_:
        xs_.append(_conv_cols(ext_ref, cw_ref, cb_ref, g * SSD_GW, SSD_GW, c))
        bmb.append(_conv_cols(ext_ref, cw_ref, cb_ref, SSD_INNER + g * SSD_N, SSD_N, c).astype(bf16))
        cmb = _conv_cols(ext_ref, cw_ref, cb_ref, SSD_INNER + SSD_BC + g * SSD_N, SSD_N, c).astype(bf16)
        hg.append(h_ref[pl.ds(g * SSD_HG, SSD_HG)].reshape(SSD_GW, SSD_N))
        cb2.append(_dot_nt(cmb, jnp.concatenate([bmb[g], bmb[g]], axis=0)))
        ch.append(_dot_nt(cmb, hg[g].astype(bf16)))
        yield

    m2, rhs, e_col, xsc = {}, {}, {}, {}
    for g in gs_:
        for p in ps_:
            l0 = SM_DT + g * SSD_HG + 2 * p
            ps = slice(p * 2 * SSD_P, (p + 1) * 2 * SSD_P)
            ac_col = jnp.where(lo_half, ac_all[:, l0:l0 + 1], ac_all[:, l0 + 1:l0 + 2])
            ac_row = jnp.where(lane1 < SSD_P, ac_t2[l0:l0 + 1, :], ac_t2[l0 + 1:l0 + 2, :])
            decay = jnp.exp(jnp.where(tri2, ac_col - ac_row, -jnp.inf))
            m2[g, p] = (cb2[g] * decay).astype(bf16)
            dt2 = jnp.where(lo_half, dt_l[:, l0:l0 + 1], dt_l[:, l0 + 1:l0 + 2])
            xdt = xs_[g][:, ps] * dt2
            rhs[g, p] = jnp.concatenate([jnp.where(lo_half, xdt, 0.0), jnp.where(lo_half, 0.0, xdt)],
                                        axis=0).astype(bf16)
            e_col[g, p] = jnp.exp(ac_col)
            xsc[g, p] = xdt * jnp.exp(ac_col[c - 1:c] - ac_col)
            yield
    y2 = {gp: _dot(m2[gp], rhs[gp]) for gp in m2}
    yield

    for g in gs_:
        xsg = jnp.concatenate([xsc[g, p] for p in ps_], axis=1).astype(bf16)
        dh = _dot_tn(xsg, bmb[g])
        for j in range(SSD_HG):
            ln = SM_DT + g * SSD_HG + j
            h_ref[g * SSD_HG + j] = (hg[g][j * SSD_P:(j + 1) * SSD_P] * jnp.exp(ac_all[c - 1:c, ln:ln + 1])
                                     + dh[j * SSD_P:(j + 1) * SSD_P])
        yield
    for g in gs_:
        gs = slice(g * SSD_GW, (g + 1) * SSD_GW)
        y = jnp.concatenate([y2[g, p] + ch[g][:, p * 2 * SSD_P:(p + 1) * 2 * SSD_P] * e_col[g, p] for p in ps_],
                            axis=1)
        _store_natural(nat_ref, y + dvec_ref[:, gs] * xs_[g], c, g * SSD_GW)
        y = _read_natural(nat_ref, g * SSD_GW, SSD_GW) * _silu(z_ref[:, gs].astype(f32))
        o_ref[:, gs] = _rms(y, nw_ref[:, gs]).astype(bf16)
        yield

    _conv_carry(ext_ref, c)


STEP_B = 8


def _conv_step(x_ref, c_ref, w_ref):
    w = w_ref[...]
    return c_ref[0] * w[0:1] + c_ref[1] * w[1:2] + c_ref[2] * w[2:3] + x_ref[...] * w[3:4]


def _expand_matrix(nb, width):
    r = lax.broadcasted_iota(jnp.int32, (nb, nb * width), 0)
    c = lax.broadcasted_iota(jnp.int32, (nb, nb * width), 1)
    return jnp.where((c >= r * width) & (c < (r + 1) * width), 1.0, 0.0).astype(bf16)


def _bcast_cols(x, e_mat):
    hi, lo = _split(x)
    return _dot_tn(hi, e_mat) + _dot_tn(lo, e_mat)


def _state_call(kernel, grid, in_specs, out_specs, out_shape, scratch, sem, name, args, prev_state):
    aliases = {}
    if prev_state is not None:
        in_specs = in_specs + [pl.BlockSpec(memory_space=pl.ANY)]
        args = args + (prev_state,)
        aliases = {len(args) - 1: len(out_shape) - 1}
        kernel = functools.partial(_drop_alias_ref, kernel, len(args) - 1)
    return pl.pallas_call(kernel, grid=grid, in_specs=in_specs, out_specs=out_specs, out_shape=out_shape,
                          scratch_shapes=scratch, input_output_aliases=aliases,
                          compiler_params=_cp(sem), name=name)(*args)


def _drop_alias_ref(kernel, idx, *refs):
    return kernel(*refs[:idx], *refs[idx + 1:])


def _gdn_step_kernel(alog_ref, dtb_ref, q_ref, k_ref, v_ref, z_ref, sm_ref, cq_ref, ck_ref, cv_ref,
                     wq_ref, wk_ref, wv_ref, nw_ref, s_ref, o_ref, so_ref):
    nb = q_ref.shape[0]
    qa = _silu(_conv_step(q_ref, cq_ref, wq_ref))
    ka = _silu(_conv_step(k_ref, ck_ref, wk_ref))
    va = _silu(_conv_step(v_ref, cv_ref, wv_ref))
    sm = sm_ref[...]
    z = z_ref[...]
    e_mat = _expand_matrix(nb, GDN_DV)
    for h in range(GDN_H):
        sl = slice(h * GDN_DK, (h + 1) * GDN_DK)
        q = qa[:, sl]
        k = ka[:, sl]
        v = va[:, sl]
        q = q * lax.rsqrt(jnp.sum(q * q, axis=-1, keepdims=True) + EPS) * (GDN_DK ** -0.5)
        k = k * lax.rsqrt(jnp.sum(k * k, axis=-1, keepdims=True) + EPS)
        beta = _sigmoid(sm[:, SM_BETA + h:SM_BETA + h + 1])
        g = -jnp.exp(_scalar_vec(alog_ref[h])) * jax.nn.softplus(sm[:, SM_DEC + h:SM_DEC + h + 1] + dtb_ref[h])
        eg = jnp.exp(g)
        qk = jnp.sum(q * k, axis=-1, keepdims=True)
        kb = _bcast_cols(k, e_mat)
        qb = _bcast_cols(q, e_mat)
        blk = lambda m, b: m[:, b * GDN_DV:(b + 1) * GDN_DV]
        ks = jnp.concatenate([jnp.sum(s_ref[b, h] * blk(kb, b), axis=0, keepdims=True) for b in range(nb)], axis=0)
        qs = jnp.concatenate([jnp.sum(s_ref[b, h] * blk(qb, b), axis=0, keepdims=True) for b in range(nb)], axis=0)
        w = beta * v - (beta * eg) * ks
        o = qs * eg + qk * w
        for b in range(nb):
            so_ref[b, h] = s_ref[b, h] * eg[b:b + 1] + blk(kb, b) * w[b:b + 1]
        o = _rms(o, nw_ref[...]) * _silu(z[:, sl])
        o_ref[:, sl] = o.astype(bf16)


def _gdn_step(proj, small, conv_t, states, prev, l, conv_w, a_log, dt_bias, norm_w):
    n = proj.shape[0]
    nb = STEP_B
    blk = lambda off: pl.BlockSpec((nb, GDN_QK), lambda i: (i, _blk(off, GDN_QK)))
    cst = lambda j: pl.BlockSpec((CONV_W - 1, nb, GDN_QK), lambda i: (0, i, j))
    cw = lambda j: pl.BlockSpec((CONV_W, GDN_QK), lambda i: (0, j))
    smem = pl.BlockSpec(memory_space=pltpu.SMEM)
    st = pl.BlockSpec((None, nb, GDN_H, GDN_DK, GDN_DV), lambda i: (l, i, 0, 0, 0))
    return _state_call(
        _gdn_step_kernel, (n // nb,),
        [smem, smem, blk(P_QKV_A), blk(P_QKV_A + GDN_QK), blk(P_QKV_A + 2 * GDN_QK), blk(P_Z_A),
         pl.BlockSpec((nb, LANE), lambda i: (i, 0)),
         cst(0), cst(1), cst(2), cw(0), cw(1), cw(2),
         pl.BlockSpec((1, GDN_DV), lambda i: (0, 0)), st],
        [pl.BlockSpec((nb, GDN_V), lambda i: (i, 0)), st],
        [jax.ShapeDtypeStruct((n, GDN_V), bf16), jax.ShapeDtypeStruct(states.shape, f32)],
        [], ("parallel",), "gdn_step",
        (a_log, dt_bias, proj, proj, proj, proj, small, conv_t, conv_t, conv_t, conv_w, conv_w, conv_w, norm_w, states),
        prev)


def _gla_step_kernel(q_ref, k_ref, v_ref, r_ref, sm_ref, wgate_ref, bgate_ref, nw_ref, s_ref,
                     o_ref, so_ref):
    nb = q_ref.shape[0]
    sm = sm_ref[...]
    lane = lax.broadcasted_iota(jnp.int32, sm.shape, 1)
    lr = jnp.where((lane >= SM_LR) & (lane < SM_LR + GLA_RANK), sm, 0.0).astype(bf16)
    la_all = jax.nn.log_sigmoid(_dot(lr, wgate_ref[...]) + bgate_ref[...]) / GLA_TAU
    qa = q_ref[...] * (GLA_DK ** -0.5)
    ka = k_ref[...]
    va = v_ref[...]
    r = r_ref[...]
    for h in range(GLA_H):
        sk = slice(h * GLA_DK, (h + 1) * GLA_DK)
        sv = slice(h * GLA_DV, (h + 1) * GLA_DV)
        q = qa[:, sk]
        k = ka[:, sk]
        v = va[:, sv]
        e = jnp.exp(la_all[:, sk])
        qk = jnp.sum(q * k, axis=-1, keepdims=True)
        e_t = e.T
        k_t = k.T
        qe_t = (q * e).T
        rows = []
        for b in range(nb):
            s = s_ref[b, h]
            vrow = v[b:b + 1]
            rows.append(qk[b:b + 1] * vrow + jnp.sum(s * qe_t[:, b:b + 1], axis=0, keepdims=True))
            so_ref[b, h] = s * e_t[:, b:b + 1] + k_t[:, b:b + 1] * vrow
        o = jnp.concatenate(rows, axis=0)
        o = _rms(o, nw_ref[...]) * _silu(r[:, sv])
        o_ref[:, sv] = o.astype(bf16)


def _gla_step(proj, small, states, prev, l, wgate_pad, bgate, norm_w):
    n = proj.shape[0]
    nb = STEP_B
    st = pl.BlockSpec((None, nb, GLA_H, GLA_DK, GLA_DV), lambda i: (l, i, 0, 0, 0))
    return _state_call(
        _gla_step_kernel, (n // nb,),
        [pl.BlockSpec((nb, GLA_QK), lambda i: (i, _blk(P_Q_B, GLA_QK))),
         pl.BlockSpec((nb, GLA_QK), lambda i: (i, _blk(P_K_B, GLA_QK))),
         pl.BlockSpec((nb, GLA_V), lambda i: (i, _blk(P_V_B, GLA_V))),
         pl.BlockSpec((nb, GLA_V), lambda i: (i, _blk(P_R_B, GLA_V))),
         pl.BlockSpec((nb, LANE), lambda i: (i, 0)),
         pl.BlockSpec((LANE, GLA_QK), lambda i: (0, 0)),
         pl.BlockSpec((1, GLA_QK), lambda i: (0, 0)),
         pl.BlockSpec((1, GLA_DV), lambda i: (0, 0)), st],
        [pl.BlockSpec((nb, GLA_V), lambda i: (i, 0)), st],
        [jax.ShapeDtypeStruct((n, GLA_V), bf16), jax.ShapeDtypeStruct(states.shape, f32)],
        [], ("parallel",), "gla_step",
        (proj, proj, proj, proj, small, wgate_pad, bgate, norm_w, states),
        prev)


def _ssd_step_kernel(alog_ref, dtb_ref, z_ref, xbc_ref, sm_ref, cst_ref, cw_ref, cb_ref, dvec_ref, nw_ref, h_ref,
                     o_ref, ho_ref):
    nb = xbc_ref.shape[0]
    xbc = _silu(_conv_step(xbc_ref, cst_ref, cw_ref) + cb_ref[...])
    sm_t = sm_ref[...].T
    e_mat = _expand_matrix(nb, SSD_N)
    lane_b = lax.broadcasted_iota(jnp.int32, (SSD_P, nb), 1)
    for g in range(SSD_G):
        gs = slice(g * SSD_GW, (g + 1) * SSD_GW)
        bm = xbc[:, SSD_INNER + g * SSD_N:SSD_INNER + (g + 1) * SSD_N]
        cm_t = xbc[:, SSD_INNER + SSD_BC + g * SSD_N:SSD_INNER + SSD_BC + (g + 1) * SSD_N].T
        cb_row = jnp.sum(cm_t * bm.T, axis=0, keepdims=True)
        hg = h_ref[:, pl.ds(g * SSD_HG, SSD_HG)].reshape(nb * SSD_GW, SSD_N)
        ch = _dot(hg.astype(bf16), cm_t.astype(bf16))
        pairs = []
        for jp in range(SSD_HG // 2):
            x_t = xbc[:, g * SSD_GW + jp * LANE:g * SSD_GW + (jp + 1) * LANE].T
            halves = []
            for jj in range(2):
                j = 2 * jp + jj
                hd = g * SSD_HG + j
                dt_row = jax.nn.softplus(sm_t[SM_DT + hd:SM_DT + hd + 1, :] + dtb_ref[hd])
                ea_row = jnp.exp(dt_row * (-jnp.exp(_scalar_vec(alog_ref[hd]))))
                xdt_t = x_t[jj * SSD_P:(jj + 1) * SSD_P] * dt_row
                hi, lo = _split(xdt_t)
                xb = _dot(hi, e_mat) + _dot(lo, e_mat)
                yh = jnp.zeros((SSD_P, nb), f32)
                for b in range(nb):
                    r0 = (b * SSD_HG + j) * SSD_P
                    yh = jnp.where(lane_b == b, ch[r0:r0 + SSD_P], yh)
                    ho_ref[b, hd] = (h_ref[b, hd] * ea_row[:, b:b + 1]
                                     + xb[:, b * SSD_N:(b + 1) * SSD_N] * bm[b:b + 1])
                halves.append(yh * ea_row + cb_row * xdt_t)
            pairs.append(jnp.concatenate(halves, axis=0).T)
        y = jnp.concatenate(pairs, axis=1)
        y = (y + dvec_ref[:, gs] * xbc[:, gs]) * _silu(z_ref[:, gs])
        o_ref[:, gs] = _rms(y, nw_ref[:, gs]).astype(bf16)


def _ssd_step(proj, small, conv_t, states, prev, l, conv_w, conv_b, a_log, dt_bias, dvec, norm_w):
    n = proj.shape[0]
    nb = STEP_B
    smem = pl.BlockSpec(memory_space=pltpu.SMEM)
    const = lambda shape: pl.BlockSpec(shape, lambda i: (0,) * len(shape))
    st = pl.BlockSpec((None, nb, SSD_H, SSD_P, SSD_N), lambda i: (l, i, 0, 0, 0))
    return _state_call(
        _ssd_step_kernel, (n // nb,),
        [smem, smem,
         pl.BlockSpec((nb, SSD_INNER), lambda i: (i, _blk(P_Z_C, SSD_INNER))),
         pl.BlockSpec((nb, SSD_CONV), lambda i: (i, _blk(P_XBC, SSD_CONV))),
         pl.BlockSpec((nb, LANE), lambda i: (i, 0)),
         pl.BlockSpec((CONV_W - 1, nb, SSD_CONV), lambda i: (0, i, 0)),
         const((CONV_W, SSD_CONV)), const((1, SSD_CONV)), const((1, SSD_INNER)), const((1, SSD_INNER)), st],
        [pl.BlockSpec((nb, SSD_INNER), lambda i: (i, 0)), st],
        [jax.ShapeDtypeStruct((n, SSD_INNER), bf16), jax.ShapeDtypeStruct(states.shape, f32)],
        [], ("parallel",), "ssd_step",
        (a_log, dt_bias, proj, proj, small, conv_t, conv_w, conv_b, dvec, norm_w, states),
        prev)


def _mixout_kernel(oa_ref, ob_ref, oc_ref, ga_ref, gb_ref, gc_ref, wa_ref, wb_ref, wc_ref, wo_ref,
                   x_ref, gt_ref, o_ref, acc_ref):
    j = pl.program_id(2)

    @pl.when(j == 0)
    def _():
        acc_ref[...] = jnp.zeros_like(acc_ref)

    m = (_sigmoid(ga_ref[...].astype(f32)) * _dot(oa_ref[...], wa_ref[...])
         + _sigmoid(gb_ref[...].astype(f32)) * _dot(ob_ref[...], wb_ref[...])
         + _sigmoid(gc_ref[...].astype(f32)) * _dot(oc_ref[...], wc_ref[...]))
    acc_ref[...] += _dot(m.astype(bf16), wo_ref[...])

    @pl.when(j == pl.num_programs(2) - 1)
    def _():
        o_ref[...] = x_ref[...] + gt_ref[...] * acc_ref[...]


def _mixout(oa, ob, oc, proj, x, mod, wa, wb, wc, wo, l, per_row, tm):
    bx, tx, dm = x.shape
    tn = 512
    gate = lambda k: pl.BlockSpec((None, tm, tn), lambda b, i, j: (b, i, _blk(P_GATES + k * D_MODEL, tn) + j))
    act = lambda w: pl.BlockSpec((None, tm, w), lambda b, i, j: (b, i, 0))
    wsp = lambda w: pl.BlockSpec((None, w, tn), lambda b, i, j: (l, 0, j))
    return pl.pallas_call(
        _mixout_kernel,
        grid=(bx, tx // tm, dm // tn),
        in_specs=[act(GDN_V), act(GLA_V), act(SSD_INNER), gate(0), gate(1), gate(2),
                  wsp(GDN_V), wsp(GLA_V), wsp(SSD_INNER),
                  pl.BlockSpec((None, tn, dm), lambda b, i, j: (l, j, 0)),
                  pl.BlockSpec((None, tm, dm), lambda b, i, j: (b, i, 0)),
                  _mod_spec(per_row, tm, 5)],
        out_specs=pl.BlockSpec((None, tm, dm), lambda b, i, j: (b, i, 0)),
        out_shape=jax.ShapeDtypeStruct(x.shape, f32),
        scratch_shapes=[pltpu.VMEM((tm, dm), f32)],
        compiler_params=_cp(("parallel", "parallel", "arbitrary")),
        name="mixout",
    )(oa, ob, oc, proj, proj, proj, wa, wb, wc, wo, x, mod)


def _permute_w_in(w):
    nl, d, _ = w.shape
    wt = jnp.swapaxes(w, 1, 2)
    rows = lambda name: wt[:, W_IN_SRC[name][0]:W_IN_SRC[name][1]]
    main = jnp.concatenate([rows(name) for name in MAIN_ORDER], axis=1).astype(bf16)
    small = jnp.concatenate([rows(name) for name in SMALL_ORDER], axis=1)
    small = jnp.pad(small, ((0, 0), (0, LANE - small.shape[1]), (0, 0))).astype(bf16)
    return main, small


def _ffn_weights(wg, wu, wd):
    f = wg.shape[2]
    cut = f - f % FF_TILE
    w = tuple(a.astype(bf16) for a in (wg, wu, wd))
    if cut == f:
        return w
    return w + (w[0][:, :, cut:], w[1][:, :, cut:], w[2][:, cut:, :])


def _lane_vec(v, lo):
    return jnp.zeros((1, LANE), f32).at[0, lo:lo + v.shape[0]].set(v)


def _layer_params(l, p):
    row = lambda a: a[l].reshape(1, -1)
    wgate = jnp.zeros((LANE, GLA_QK), f32).at[SM_LR:SM_LR + GLA_RANK].set(p["gla_w_gate"][l]).astype(bf16)
    return dict(
        norm1=row(p["norm1"]), norm2=row(p["norm2"]), norm3=row(p["norm3"]),
        gdn_conv_w=p["gdn_conv_w"][l], gdn_a_log=p["gdn_a_log"][l], gdn_dt_bias=p["gdn_dt_bias"][l],
        gdn_alog_l=_lane_vec(p["gdn_a_log"][l], SM_DEC), gdn_dtb_l=_lane_vec(p["gdn_dt_bias"][l], SM_DEC),
        gdn_norm_w=row(p["gdn_norm_w"]),
        gla_wgate=wgate, gla_bgate=row(p["gla_b_gate"]), gla_norm_w=row(p["gla_norm_w"]),
        ssd_conv_w=p["ssd_conv_w"][l], ssd_conv_b=row(p["ssd_conv_b"]), ssd_a_log=p["ssd_a_log"][l],
        ssd_dt_bias=p["ssd_dt_bias"][l],
        ssd_avec_l=_lane_vec(-jnp.exp(p["ssd_a_log"][l]), SM_DT), ssd_dtb_l=_lane_vec(p["ssd_dt_bias"][l], SM_DT),
        ssd_dvec=jnp.repeat(p["ssd_d"][l], SSD_P).reshape(1, -1),
        ssd_norm_w=row(p["ssd_norm_w"]),
    )


def _stacked_weights(p):
    w_in, w_in_small = _permute_w_in(p["w_in"])
    return dict(
        f1=_ffn_weights(p["ffn1_wg"], p["ffn1_wu"], p["ffn1_wd"]),
        f2=_ffn_weights(p["ffn2_wg"], p["ffn2_wu"], p["ffn2_wd"]),
        w_in=w_in, w_in_small=w_in_small,
        wa=p["w_branch_gdn"].astype(bf16), wb=p["w_branch_gla"].astype(bf16),
        wc=p["w_branch_ssd"].astype(bf16), w_out=p["w_out"].astype(bf16),
    )


def _new_conv_state(buf, raw):
    t = raw.shape[1]
    k = CONV_W - 1
    if t >= k:
        return raw[:, t - k:]
    return jnp.concatenate([buf[:, t:], raw], axis=1)


def _mixer_prompt(proj, small, lp, st):
    gdn_conv, s_gdn, s_gla, ssd_conv, s_ssd = st
    b, t, _ = proj.shape
    c = CHUNK
    sm_spec = pl.BlockSpec((None, c, LANE), lambda bi, ti: (bi, ti, 0))
    st_a, st_b, st_c = _per_seq(GDN_H, GDN_DK, GDN_DV), _per_seq(GLA_H, GLA_DK, GLA_DV), _per_seq(SSD_H, SSD_P, SSD_N)
    gdn_specs = [_tok(GDN_CONV, P_QKV_A), _tok(GDN_V, P_Z_A), sm_spec, _per_seq(CONV_W - 1, GDN_CONV),
                 _const(CONV_W, GDN_CONV), _const(1, LANE), _const(1, LANE), _const(1, GDN_DV), st_a]
    gdn_args = (proj, proj, small, gdn_conv, lp["gdn_conv_w"], lp["gdn_alog_l"], lp["gdn_dtb_l"],
                lp["gdn_norm_w"], s_gdn)
    gla_specs = [_tok(GLA_QK, P_Q_B), _tok(GLA_QK, P_K_B), _tok(GLA_V, P_V_B), _tok(GLA_V, P_R_B), sm_spec,
                 _const(LANE, GLA_QK), _const(1, GLA_QK), _const(1, GLA_DV), st_b]
    gla_args = (proj, proj, proj, proj, small, lp["gla_wgate"], lp["gla_bgate"], lp["gla_norm_w"], s_gla)
    ssd_specs = [_tok(SSD_INNER, P_Z_C), _tok(SSD_CONV, P_XBC), sm_spec, _per_seq(CONV_W - 1, SSD_CONV),
                 _const(CONV_W, SSD_CONV), _const(1, SSD_CONV), _const(1, LANE), _const(1, LANE),
                 _const(1, SSD_INNER), _const(1, SSD_INNER), st_c]
    ssd_args = (proj, proj, small, ssd_conv, lp["ssd_conv_w"], lp["ssd_conv_b"], lp["ssd_avec_l"],
                lp["ssd_dtb_l"], lp["ssd_dvec"], lp["ssd_norm_w"], s_ssd)
    assert (len(gdn_specs), len(gla_specs), len(ssd_specs)) == (N_GDN_IN, N_GLA_IN, N_SSD_IN)
    out_tok = lambda width: pl.BlockSpec((None, c, width), lambda bi, ti: (bi, ti, 0))
    oa, s_gdn_n, ob, s_gla_n, oc, s_ssd_n = pl.pallas_call(
        _mixers_kernel,
        grid=(b, t // c),
        in_specs=gdn_specs + gla_specs + ssd_specs,
        out_specs=[out_tok(GDN_V), st_a, out_tok(GLA_V), st_b, out_tok(SSD_INNER), st_c],
        out_shape=[jax.ShapeDtypeStruct((b, t, GDN_V), bf16), jax.ShapeDtypeStruct(s_gdn.shape, f32),
                   jax.ShapeDtypeStruct((b, t, GLA_V), bf16), jax.ShapeDtypeStruct(s_gla.shape, f32),
                   jax.ShapeDtypeStruct((b, t, SSD_INNER), bf16), jax.ShapeDtypeStruct(s_ssd.shape, f32)],
        scratch_shapes=[pltpu.VMEM((GDN_CONV // LANE, 8 + c, LANE), f32), pltpu.VMEM((GLA_H, GLA_DV, GLA_DK), f32),
                        pltpu.VMEM((SSD_CONV // LANE, 8 + c, LANE), f32),
                        pltpu.VMEM((GDN_V // LANE, c, LANE), f32), pltpu.VMEM((SSD_INNER // LANE, c, LANE), f32)],
        compiler_params=_cp(("parallel", "arbitrary")),
        name="mixers_prompt",
    )(*gdn_args, *gla_args, *ssd_args)
    gdn_conv_n = _new_conv_state(gdn_conv, proj[:, :, P_QKV_A:P_QKV_A + GDN_CONV]).astype(f32)
    ssd_conv_n = _new_conv_state(ssd_conv, proj[:, :, P_XBC:P_XBC + SSD_CONV]).astype(f32)
    return (oa, ob, oc), (gdn_conv_n, s_gdn_n, s_gla_n, ssd_conv_n, s_ssd_n)


def _mixer_sample(proj, small, lp, l, states, prev):
    n = proj.shape[1]
    p2 = proj.reshape(n, P_TOTAL)
    s2 = small.reshape(n, LANE)
    gdn_conv, ssd_conv = states[0][l], states[3][l]
    gct = jnp.swapaxes(gdn_conv, 0, 1)
    sct = jnp.swapaxes(ssd_conv, 0, 1)
    pv = (None,) * 5 if prev is None else prev
    oa, s_gdn_n = _gdn_step(p2, s2, gct, states[1], pv[1], l, lp["gdn_conv_w"], lp["gdn_a_log"],
                            lp["gdn_dt_bias"], lp["gdn_norm_w"])
    ob, s_gla_n = _gla_step(p2, s2, states[2], pv[2], l, lp["gla_wgate"], lp["gla_bgate"], lp["gla_norm_w"])
    oc, s_ssd_n = _ssd_step(p2, s2, sct, states[4], pv[4], l, lp["ssd_conv_w"], lp["ssd_conv_b"], lp["ssd_a_log"],
                            lp["ssd_dt_bias"], lp["ssd_dvec"], lp["ssd_norm_w"])
    raw = p2.reshape(n, 1, P_TOTAL)
    gdn_conv_n = _new_conv_state(gdn_conv, raw[:, :, P_QKV_A:P_QKV_A + GDN_CONV])
    ssd_conv_n = _new_conv_state(ssd_conv, raw[:, :, P_XBC:P_XBC + SSD_CONV])
    outs = tuple(o.reshape(1, n, -1) for o in (oa, ob, oc))
    return outs, (gdn_conv_n, s_gdn_n, s_gla_n, ssd_conv_n, s_ssd_n)


def _trunk(x, mods, lps, sw, states, per_row, tm, tm_in, final_w):
    nl = len(lps)
    per_layer = []
    prev = None
    for l in range(nl):
        lp, mod = lps[l], mods[l]
        last = l == nl - 1
        x = _ffn(x, mod, 0, lp["norm1"], sw["f1"], l, lp["norm1"], per_row, tm, False)
        proj, small = _inproj(x, mod, lp["norm2"], sw["w_in"], sw["w_in_small"], l, per_row, tm_in,
                              f32 if per_row else bf16)
        if per_row:
            (oa, ob, oc), st = _mixer_sample(proj, small, lp, l, states, prev)
            prev = st
        else:
            (oa, ob, oc), st = _mixer_prompt(proj, small, lp, tuple(s[l] for s in states))
        per_layer.append(st)
        x = _mixout(oa, ob, oc, proj, x, mod, sw["wa"], sw["wb"], sw["wc"], sw["w_out"], l, per_row, tm)
        x = _ffn(x, mod, 6, lp["norm3"], sw["f2"], l, final_w if last else lp["norm3"], per_row, tm, last)
    stack = lambda i: jnp.stack([st[i] for st in per_layer])
    if per_row:
        new_states = (stack(0), prev[1], prev[2], stack(3), prev[4])
    else:
        new_states = tuple(stack(i) for i in range(5))
    return x, new_states


def kernel(x_prompt, x_sample, state_gdn_conv, state_gdn, state_gla, state_ssd_conv, state_ssd, c_prompt, c_sample, w_ada, b_ada, norm1, norm2, norm3, ffn1_wg, ffn1_wu, ffn1_wd, ffn2_wg, ffn2_wu, ffn2_wd, w_in, gdn_conv_w, gdn_a_log, gdn_dt_bias, gdn_norm_w, gla_w_gate, gla_b_gate, gla_norm_w, ssd_conv_w, ssd_conv_b, ssd_a_log, ssd_dt_bias, ssd_d, ssd_norm_w, w_branch_gdn, w_branch_gla, w_branch_ssd, w_out, final_norm):
    p = dict(norm1=norm1, norm2=norm2, norm3=norm3,
             ffn1_wg=ffn1_wg, ffn1_wu=ffn1_wu, ffn1_wd=ffn1_wd, ffn2_wg=ffn2_wg, ffn2_wu=ffn2_wu, ffn2_wd=ffn2_wd,
             w_in=w_in, gdn_conv_w=gdn_conv_w, gdn_a_log=gdn_a_log, gdn_dt_bias=gdn_dt_bias, gdn_norm_w=gdn_norm_w,
             gla_w_gate=gla_w_gate, gla_b_gate=gla_b_gate, gla_norm_w=gla_norm_w,
             ssd_conv_w=ssd_conv_w, ssd_conv_b=ssd_conv_b, ssd_a_log=ssd_a_log, ssd_dt_bias=ssd_dt_bias,
             ssd_d=ssd_d, ssd_norm_w=ssd_norm_w,
             w_branch_gdn=w_branch_gdn, w_branch_gla=w_branch_gla, w_branch_ssd=w_branch_ssd, w_out=w_out)
    nl = w_ada.shape[0]
    bp, tp, dm = x_prompt.shape
    bs = x_sample.shape[0]
    assert x_sample.shape[1] == 1 and tp % CHUNK == 0 and bs % STEP_B == 0 and dm == D_MODEL
    lps = [_layer_params(l, p) for l in range(nl)]
    sw = _stacked_weights(p)
    fw = final_norm.reshape(1, dm)

    rows = bp + bs
    rpad = -(-rows // 8) * 8
    c_all = jnp.concatenate([c_prompt, c_sample, jnp.zeros((rpad - rows, dm), f32)], axis=0)
    mod = _ada_mod(c_all, w_ada, b_ada)
    mod_p = [mod[l, :bp].reshape(bp, 1, N_MOD * dm) for l in range(nl)]
    mod_s = [mod[l, bp:rows].reshape(1, bs, N_MOD * dm) for l in range(nl)]

    sample_states = (state_gdn_conv, state_gdn, state_gla, state_ssd_conv, state_ssd)
    prompt_states = tuple(jnp.zeros((s.shape[0], bp) + s.shape[2:], x_prompt.dtype) for s in sample_states)
    tm_p = 512 if tp % 512 == 0 else CHUNK
    tm_in = 1024 if tp % 1024 == 0 else tm_p
    y_p, st_p = _trunk(x_prompt, mod_p, lps, sw, prompt_states, False, tm_p, tm_in, fw)
    y_s, st_s = _trunk(x_sample.reshape(1, bs, dm), mod_s, lps, sw, sample_states, True, bs, bs, fw)
    return (y_p, y_s.reshape(bs, 1, dm)) + st_p + st_s
```

```python
import functools

import jax
import jax.numpy as jnp
from jax import lax
from jax.experimental import pallas as pl
from jax.experimental.pallas import tpu as pltpu

f32 = jnp.float32
bf16 = jnp.bfloat16
HI = lax.Precision.HIGHEST

EPS = 1e-6
D_MODEL = 2048
N_MOD = 9
CHUNK = 64
CONV_W = 4
GDN_H, GDN_DK, GDN_DV = 8, 128, 128
GLA_H, GLA_DK, GLA_DV, GLA_RANK, GLA_TAU = 4, 128, 256, 16, 16.0
SSD_H, SSD_P, SSD_G, SSD_N = 32, 64, 4, 128
SSD_HG = SSD_H // SSD_G
GDN_QK = GDN_H * GDN_DK
GDN_V = GDN_H * GDN_DV
GDN_CONV = 2 * GDN_QK + GDN_V
GLA_QK = GLA_H * GLA_DK
GLA_V = GLA_H * GLA_DV
SSD_INNER = SSD_H * SSD_P
SSD_BC = SSD_G * SSD_N
SSD_CONV = SSD_INNER + 2 * SSD_BC
SSD_GW = SSD_HG * SSD_P

LANE = 128
FF_TILE = 512
FF_SUB = 256

IN_SPLITS = (("qkv_a", GDN_CONV), ("z_a", GDN_V), ("beta", GDN_H), ("dec", GDN_H),
             ("q_b", GLA_QK), ("k_b", GLA_QK), ("v_b", GLA_V), ("lr", GLA_RANK), ("r_b", GLA_V),
             ("z_c", SSD_INNER), ("xbc", SSD_CONV), ("dt", SSD_H), ("gates", 3 * D_MODEL))
MAIN_ORDER = ("qkv_a", "xbc", "z_c", "gates", "z_a", "q_b", "k_b", "v_b", "r_b")
SMALL_ORDER = ("beta", "dec", "lr", "dt")


def _layout():
    src, off = {}, 0
    for name, w in IN_SPLITS:
        src[name] = (off, off + w)
        off += w
    main, small, d = {}, {}, 0
    for name in MAIN_ORDER:
        w = src[name][1] - src[name][0]
        main[name] = d
        d += w
    total = d
    d = 0
    for name in SMALL_ORDER:
        small[name] = d
        d += src[name][1] - src[name][0]
    assert d <= LANE
    return src, main, small, total


W_IN_SRC, P_MAIN, P_SM, P_TOTAL = _layout()
P_QKV_A, P_XBC, P_Z_C, P_GATES, P_Z_A = (P_MAIN[k] for k in ("qkv_a", "xbc", "z_c", "gates", "z_a"))
P_Q_B, P_K_B, P_V_B, P_R_B = (P_MAIN[k] for k in ("q_b", "k_b", "v_b", "r_b"))
SM_BETA, SM_DEC, SM_LR, SM_DT = (P_SM[k] for k in SMALL_ORDER)

VMEM_LIMIT = 56 * 1024 * 1024


def _cp(sem):
    return pltpu.CompilerParams(dimension_semantics=sem, vmem_limit_bytes=VMEM_LIMIT)


def _blk(off, width):
    assert off % width == 0, (off, width)
    return off // width


def _sigmoid(x):
    return 0.5 + 0.5 * jnp.tanh(0.5 * x)


def _silu(x):
    h = 0.5 * x
    return h + h * jnp.tanh(h)


def _rms(x, w):
    return x * lax.rsqrt(jnp.mean(x * x, axis=-1, keepdims=True) + EPS) * w


def _dot(a, b):
    return jnp.dot(a, b, preferred_element_type=f32)


def _dot_nt(a, b):
    return lax.dot_general(a, b, (((1,), (1,)), ((), ())), preferred_element_type=f32)


def _dot_tn(a, b):
    return lax.dot_general(a, b, (((0,), (0,)), ((), ())), preferred_element_type=f32)


def _split(x):
    hi = x.astype(bf16)
    return hi, (x - hi.astype(f32)).astype(bf16)


def _dot3(a, b):
    return _dot(a[0], b[0]) + (_dot(a[0], b[1]) + _dot(a[1], b[0]))


def _dot_hi(a, b):
    return jnp.dot(a, b, precision=HI, preferred_element_type=f32)


def _tri_masks(c):
    row = lax.broadcasted_iota(jnp.int32, (c, c), 0)
    col = lax.broadcasted_iota(jnp.int32, (c, c), 1)
    return row >= col, row > col, row == col


SUB = 8


def _row_time(idx, c):
    n = c // SUB
    assert n & (n - 1) == 0, "chunk / SUB must be a power of two"
    sh = n.bit_length() - 1
    return ((idx & (n - 1)) << 3) | (idx >> sh)


def _tri_masks_residue(c, cols=None):
    cols = c if cols is None else cols
    row = _row_time(lax.broadcasted_iota(jnp.int32, (c, cols), 0), c)
    col = _row_time(lax.broadcasted_iota(jnp.int32, (c, cols), 1) & (c - 1), c)
    return row >= col, row > col, row == col


def _load_residue(ref, c, sl=slice(None)):
    n = c // SUB
    return jnp.concatenate([ref[pl.ds(r, n, stride=SUB), sl] for r in range(SUB)], axis=0)


def _store_natural(nat_ref, x, c, lo):
    n = c // SUB
    for j in range(x.shape[1] // LANE):
        for r in range(SUB):
            nat_ref[lo // LANE + j, pl.ds(r, n, stride=SUB), :] = x[r * n:(r + 1) * n, j * LANE:(j + 1) * LANE]


def _read_natural(nat_ref, lo, width):
    return jnp.concatenate([nat_ref[lo // LANE + j] for j in range(width // LANE)], axis=1)


def _lane_col(x, idx):
    lane = lax.broadcasted_iota(jnp.int32, x.shape, 1)
    return jnp.sum(jnp.where(lane == idx, x, 0.0), axis=1, keepdims=True)


def _scalar_vec(s):
    return jnp.full((1, 1), s, f32)


def _ada_kernel(c_ref, w_ref, b_ref, o_ref):
    s = _silu(c_ref[...]).astype(bf16)
    o_ref[...] = _dot(s, w_ref[...].astype(bf16)) + b_ref[...]


def _ada_mod(c_all, w_ada, b_ada):
    nl, dm, n = w_ada.shape
    r = c_all.shape[0]
    tn = 1024
    return pl.pallas_call(
        _ada_kernel,
        grid=(nl, n // tn),
        in_specs=[pl.BlockSpec((r, dm), lambda l, j: (0, 0)),
                  pl.BlockSpec((None, dm, tn), lambda l, j: (l, 0, j)),
                  pl.BlockSpec((None, 1, tn), lambda l, j: (l, 0, j))],
        out_specs=pl.BlockSpec((None, r, tn), lambda l, j: (l, 0, j)),
        out_shape=jax.ShapeDtypeStruct((nl, r, n), f32),
        compiler_params=_cp(("arbitrary", "arbitrary")),
        name="ada_mod",
    )(c_all, w_ada, b_ada.reshape(nl, 1, n))


def _mod_spec(per_row, tm, chunk):
    if per_row:
        return pl.BlockSpec((None, tm, D_MODEL), lambda b, i, j: (b, i, chunk))
    return pl.BlockSpec((None, 1, D_MODEL), lambda b, i, j: (b, 0, chunk))


def _ffn_kernel(x_ref, sh_ref, sc_ref, gt_ref, nw_ref, fw_ref, wg_ref, wu_ref, wd_ref, *rest, final, has_tail):
    if has_tail:
        wgt_ref, wut_ref, wdt_ref, o_ref, h_ref, acc_ref = rest
    else:
        o_ref, h_ref, acc_ref = rest
    f = pl.program_id(2)
    last = pl.num_programs(2) - 1

    @pl.when(f == 0)
    def _():
        y = _rms(x_ref[...], nw_ref[...])
        h_ref[...] = (y * (1.0 + sc_ref[...]) + sh_ref[...]).astype(bf16)
        acc_ref[...] = jnp.zeros_like(acc_ref)

    def accumulate(wg, wu, wd):
        h = h_ref[...]
        width = wg.shape[1]
        sub = FF_SUB if width % FF_SUB == 0 else width
        parts = [(_dot(h, wg[:, s:s + sub]), _dot(h, wu[:, s:s + sub])) for s in range(0, width, sub)]
        acc = acc_ref[...]
        for i, (g, u) in enumerate(parts):
            acc = acc + _dot((_silu(g) * u).astype(bf16), wd[i * sub:(i + 1) * sub, :])
        acc_ref[...] = acc

    if has_tail:
        pl.when(f < last)(lambda: accumulate(wg_ref, wu_ref, wd_ref))
        pl.when(f == last)(lambda: accumulate(wgt_ref, wut_ref, wdt_ref))
    else:
        accumulate(wg_ref, wu_ref, wd_ref)

    @pl.when(f == last)
    def _():
        y = x_ref[...] + 0.5 * gt_ref[...] * acc_ref[...]
        if final:
            y = _rms(y, fw_ref[...])
        o_ref[...] = y


def _ffn(x, mod, k0, nw, w, l, fw, per_row, tm, final):
    bx, tx, dm = x.shape
    tf = FF_TILE
    n_full = w[0].shape[2] // tf
    has_tail = len(w) > 3
    full = lambda f: jnp.minimum(f, n_full - 1)
    w_specs = [pl.BlockSpec((None, dm, tf), lambda b, i, f: (l, 0, full(f))),
               pl.BlockSpec((None, dm, tf), lambda b, i, f: (l, 0, full(f))),
               pl.BlockSpec((None, tf, dm), lambda b, i, f: (l, full(f), 0))]
    if has_tail:
        ft = w[3].shape[2]
        w_specs += [pl.BlockSpec((None, dm, ft), lambda b, i, f: (l, 0, 0)),
                    pl.BlockSpec((None, dm, ft), lambda b, i, f: (l, 0, 0)),
                    pl.BlockSpec((None, ft, dm), lambda b, i, f: (l, 0, 0))]
    return pl.pallas_call(
        functools.partial(_ffn_kernel, final=final, has_tail=has_tail),
        grid=(bx, tx // tm, n_full + int(has_tail)),
        in_specs=[pl.BlockSpec((None, tm, dm), lambda b, i, f: (b, i, 0)),
                  _mod_spec(per_row, tm, k0), _mod_spec(per_row, tm, k0 + 1), _mod_spec(per_row, tm, k0 + 2),
                  pl.BlockSpec((1, dm), lambda b, i, f: (0, 0)),
                  pl.BlockSpec((1, dm), lambda b, i, f: (0, 0))] + w_specs,
        out_specs=pl.BlockSpec((None, tm, dm), lambda b, i, f: (b, i, 0)),
        out_shape=jax.ShapeDtypeStruct(x.shape, f32),
        scratch_shapes=[pltpu.VMEM((tm, dm), bf16), pltpu.VMEM((tm, dm), f32)],
        compiler_params=_cp(("parallel", "parallel", "arbitrary")),
        name="ffn",
    )(x, mod, mod, mod, nw, fw, *w)


def _inproj_kernel(x_ref, sh_ref, sc_ref, nw_ref, w_ref, ws_ref, o_ref, os_ref, h_ref):
    @pl.when(pl.program_id(2) == 0)
    def _():
        y = _rms(x_ref[...], nw_ref[...])
        h = (y * (1.0 + sc_ref[...]) + sh_ref[...]).astype(bf16)
        h_ref[...] = h
        os_ref[...] = _dot_nt(h, ws_ref[...])

    o_ref[...] = _dot_nt(h_ref[...], w_ref[...]).astype(o_ref.dtype)


def _inproj(x, mod, nw, w, ws, l, per_row, tm, out_dtype):
    bx, tx, dm = x.shape
    n = w.shape[1]
    tn = 1024
    return pl.pallas_call(
        _inproj_kernel,
        grid=(bx, tx // tm, n // tn),
        in_specs=[pl.BlockSpec((None, tm, dm), lambda b, i, j: (b, i, 0)),
                  _mod_spec(per_row, tm, 3), _mod_spec(per_row, tm, 4),
                  pl.BlockSpec((1, dm), lambda b, i, j: (0, 0)),
                  pl.BlockSpec((None, tn, dm), lambda b, i, j: (l, j, 0)),
                  pl.BlockSpec((None, LANE, dm), lambda b, i, j: (l, 0, 0))],
        out_specs=[pl.BlockSpec((None, tm, tn), lambda b, i, j: (b, i, j)),
                   pl.BlockSpec((None, tm, LANE), lambda b, i, j: (b, i, 0))],
        out_shape=[jax.ShapeDtypeStruct((bx, tx, n), out_dtype),
                   jax.ShapeDtypeStruct((bx, tx, LANE), f32)],
        scratch_shapes=[pltpu.VMEM((tm, dm), bf16)],
        compiler_params=_cp(("parallel", "parallel", "arbitrary")),
        name="inproj",
    )(x, mod, mod, nw, w, ws)


def _conv_load(ext_ref, x_ref, cst_ref, first):
    c = x_ref.shape[0]
    nblk = ext_ref.shape[0]

    @pl.when(first)
    def _():
        for j in range(nblk):
            ext_ref[j, pl.ds(5, CONV_W - 1), :] = cst_ref[:, j * LANE:(j + 1) * LANE]

    for j in range(nblk):
        ext_ref[j, pl.ds(8, c), :] = x_ref[:, j * LANE:(j + 1) * LANE].astype(f32)


def _conv_cols(ext_ref, w_ref, b_ref, lo, width, c):
    n = c // SUB
    cols = []
    for blk in range(lo // LANE, (lo + width) // LANE):
        sl = pl.ds(blk * LANE, LANE)
        w = w_ref[:, sl]
        taps = [ext_ref[blk, pl.ds(8 - (CONV_W - 1) + k, n, stride=SUB), :] for k in range(SUB + CONV_W - 1)]
        ys = []
        for r in range(SUB):
            y = taps[r] * w[0:1]
            for j in range(1, CONV_W):
                y = y + taps[r + j] * w[j:j + 1]
            ys.append(y)
        y = jnp.concatenate(ys, axis=0)
        if b_ref is not None:
            y = y + b_ref[:, sl]
        cols.append(_silu(y))
    return cols[0] if len(cols) == 1 else jnp.concatenate(cols, axis=1)


def _conv_carry(ext_ref, c):
    for j in range(ext_ref.shape[0]):
        ext_ref[j, pl.ds(5, CONV_W - 1), :] = ext_ref[j, pl.ds(8 + c - (CONV_W - 1), CONV_W - 1), :]


def _cumsum_rows(x, tri):
    return _dot_hi(jnp.where(tri, 1.0, 0.0).astype(f32), x)


def _gdn_kernel(qkv_ref, z_ref, sm_ref, cst_ref, cw_ref, alog_ref, dtb_ref, nw_ref, s0_ref,
                o_ref, s_ref, ext_ref, nat_ref):
    c = qkv_ref.shape[0]
    first = pl.program_id(1) == 0

    @pl.when(first)
    def _():
        s_ref[...] = s0_ref[...]

    _conv_load(ext_ref, qkv_ref, cst_ref, first)

    sm = _load_residue(sm_ref, c)
    beta_l = _sigmoid(sm)
    g_l = -jnp.exp(alog_ref[...]) * jax.nn.softplus(sm + dtb_ref[...])
    tri, strict, eye = _tri_masks_residue(c)
    gc_all = _cumsum_rows(g_l, tri)
    gc_t = gc_all.T
    eye_f = jnp.where(eye, 1.0, 0.0).astype(f32)
    n_fac = max(1, (c - 1).bit_length() - 1)

    hs = range(GDN_H)
    qb, kb, ks, decay, eg, gcs, nm, rhs = [], [], [], [], [], [], [], []
    for h in hs:
        q = _conv_cols(ext_ref, cw_ref, None, h * GDN_DK, GDN_DK, c)
        k = _conv_cols(ext_ref, cw_ref, None, GDN_QK + h * GDN_DK, GDN_DK, c)
        v = _conv_cols(ext_ref, cw_ref, None, 2 * GDN_QK + h * GDN_DV, GDN_DV, c)
        q = q * lax.rsqrt(jnp.sum(q * q, axis=-1, keepdims=True) + EPS) * (GDN_DK ** -0.5)
        k = k * lax.rsqrt(jnp.sum(k * k, axis=-1, keepdims=True) + EPS)
        beta = beta_l[:, SM_BETA + h:SM_BETA + h + 1]
        gc = gc_all[:, SM_DEC + h:SM_DEC + h + 1]
        d = jnp.exp(jnp.where(tri, gc - gc_t[SM_DEC + h:SM_DEC + h + 1, :], -jnp.inf))
        e = jnp.exp(gc)
        qb.append(q.astype(bf16))
        kb.append(k.astype(bf16))
        ks.append(k)
        decay.append(d)
        eg.append(e)
        gcs.append(gc)
        nm.append(jnp.where(strict, beta * d * _dot_nt(kb[h], kb[h]), 0.0))
        rhs.append(jnp.concatenate([v * beta, k * (beta * e)], axis=1))
        yield

    inv = [eye_f - nm[h] for h in hs]
    pw_s = [_split(nm[h]) for h in hs]
    pw_s = [_split(_dot3(pw_s[h], pw_s[h])) for h in hs]
    yield
    for i in range(n_fac):
        inv = [inv[h] + _dot3(_split(inv[h]), pw_s[h]) for h in hs]
        yield
        if i + 1 < n_fac:
            pw_s = [_split(_dot3(pw_s[h], pw_s[h])) for h in hs]
            yield
    sol = []
    for h in hs:
        inv_hi, inv_lo = _split(inv[h])
        rb = rhs[h].astype(bf16)
        sol.append(_dot(inv_hi, rb) + _dot(inv_lo, rb))
    yield

    s = [s_ref[h] for h in hs]
    sb = [s[h].astype(bf16) for h in hs]
    wb = [(sol[h][:, 0:GDN_DV] - _dot(sol[h][:, GDN_DV:].astype(bf16), sb[h])).astype(bf16) for h in hs]
    yield
    qk = [(_dot_nt(qb[h], kb[h]) * decay[h]).astype(bf16) for h in hs]
    yield
    o = [_dot(qb[h], sb[h]) * eg[h] + _dot(qk[h], wb[h]) for h in hs]
    yield
    for h in hs:
        g_last = gcs[h][c - 1:c]
        kd = (ks[h] * jnp.exp(g_last - gcs[h])).astype(bf16)
        s_ref[h] = s[h] * jnp.exp(g_last) + _dot_tn(kd, wb[h])
    yield
    for h in hs:
        sl = slice(h * GDN_DV, (h + 1) * GDN_DV)
        _store_natural(nat_ref, _rms(o[h], nw_ref[...]), c, h * GDN_DV)
        o_ref[:, sl] = (_read_natural(nat_ref, h * GDN_DV, GDN_DV) * _silu(z_ref[:, sl].astype(f32))).astype(bf16)
        yield

    _conv_carry(ext_ref, c)


def _tok(width, off):
    return pl.BlockSpec((None, CHUNK, width), lambda bi, ti: (bi, ti, _blk(off, width)))


def _per_seq(*shape):
    return pl.BlockSpec((None,) + shape, lambda bi, ti: (bi,) + (0,) * len(shape))


def _const(*shape):
    return pl.BlockSpec(shape, lambda bi, ti: (0,) * len(shape))


N_GDN_IN, N_GLA_IN, N_SSD_IN = 9, 9, 11
_DONE = object()


def _mixers_kernel(*refs):
    i0, i1, i2 = N_GDN_IN, N_GDN_IN + N_GLA_IN, N_GDN_IN + N_GLA_IN + N_SSD_IN
    gdn_in, gla_in, ssd_in = refs[:i0], refs[i0:i1], refs[i1:i2]
    oa_ref, sa_ref, ob_ref, sb_ref, oc_ref, sc_ref, ext_a_ref, st_b_ref, ext_c_ref, nat_a_ref, nat_c_ref = refs[i2:]
    live = [_gdn_kernel(*gdn_in, oa_ref, sa_ref, ext_a_ref, nat_a_ref),
            _gla_kernel(*gla_in, ob_ref, sb_ref, st_b_ref),
            _ssd_kernel(*ssd_in, oc_ref, sc_ref, ext_c_ref, nat_c_ref)]
    while live:
        for gen in list(live):
            if next(gen, _DONE) is _DONE:
                live.remove(gen)


GLA_SUB = 16


def _gla_kernel(q_ref, k_ref, v_ref, r_ref, sm_ref, wgate_ref, bgate_ref, nw_ref, s0_ref,
                o_ref, so_ref, st_ref):
    c = q_ref.shape[0]
    ti = pl.program_id(1)

    @pl.when(ti == 0)
    def _():
        for h in range(GLA_H):
            st_ref[h] = s0_ref[h].T

    sm = sm_ref[...]
    lane = lax.broadcasted_iota(jnp.int32, sm.shape, 1)
    lr = jnp.where((lane >= SM_LR) & (lane < SM_LR + GLA_RANK), sm, 0.0).astype(bf16)
    la = jax.nn.log_sigmoid(_dot(lr, wgate_ref[...]) + bgate_ref[...]) / GLA_TAU
    tri, _, _ = _tri_masks(c)
    b_all = _cumsum_rows(la, tri)

    col = lax.broadcasted_iota(jnp.int32, (GLA_SUB, c), 1)
    row = lax.broadcasted_iota(jnp.int32, (GLA_SUB, c), 0)
    hs = range(GLA_H)
    qs, ks, vbs, bs, atts = [], [], [], [], []
    for h in hs:
        sk = slice(h * GLA_DK, (h + 1) * GLA_DK)
        q = q_ref[:, sk].astype(f32) * (GLA_DK ** -0.5)
        k = k_ref[:, sk].astype(f32)
        b = b_all[:, sk]
        qs.append(q)
        ks.append(k)
        bs.append(b)
        vbs.append(v_ref[:, h * GLA_DV:(h + 1) * GLA_DV].astype(bf16))
        att_rows = []
        for i in range(c // GLA_SUB):
            lo = i * GLA_SUB
            b_i = b[lo:lo + GLA_SUB]
            q_i = q[lo:lo + GLA_SUB]
            b_top = b[lo:lo + 1]
            att = jnp.zeros((GLA_SUB, c), f32)
            if i > 0:
                q_t = (q_i * jnp.exp(b_i - b_top)).astype(bf16)
                k_t = (k * jnp.exp(jnp.minimum(b_top - b, 0.0))).astype(bf16)
                att = jnp.where(col < lo, _dot_nt(q_t, k_t), 0.0)
            for sl in range(GLA_SUB):
                sidx = lo + sl
                e = jnp.exp(jnp.minimum(b_i - b[sidx:sidx + 1], 0.0))
                p = jnp.sum(q_i * k[sidx:sidx + 1] * e, axis=1, keepdims=True)
                att = jnp.where((col == sidx) & (row >= sl), p, att)
            att_rows.append(att)
            yield
        atts.append(jnp.concatenate(att_rows, axis=0).astype(bf16))

    sts = [st_ref[h] for h in hs]
    os_ = [_dot(atts[h], vbs[h]) + _dot_nt((qs[h] * jnp.exp(bs[h])).astype(bf16), sts[h].astype(bf16))
           for h in hs]
    yield
    for h in hs:
        b_last = bs[h][c - 1:c]
        kd = (ks[h] * jnp.exp(b_last - bs[h])).astype(bf16)
        st_ref[h] = sts[h] * jnp.exp(b_last) + _dot_tn(vbs[h], kd)
        yield
    for h in hs:
        sv = slice(h * GLA_DV, (h + 1) * GLA_DV)
        o_ref[:, sv] = (_rms(os_[h], nw_ref[...]) * _silu(r_ref[:, sv].astype(f32))).astype(bf16)
        yield

    @pl.when(ti == pl.num_programs(1) - 1)
    def _():
        for h in hs:
            so_ref[h] = st_ref[h].T


def _ssd_kernel(z_ref, xbc_ref, sm_ref, cst_ref, cw_ref, cb_ref, avec_ref, dtb_ref, dvec_ref, nw_ref, h0_ref,
                o_ref, h_ref, ext_ref, nat_ref):
    c = xbc_ref.shape[0]
    assert c == SSD_P, "head pairs share a lane tile: the chunk must be as wide as a head"
    first = pl.program_id(1) == 0

    @pl.when(first)
    def _():
        h_ref[...] = h0_ref[...]

    _conv_load(ext_ref, xbc_ref, cst_ref, first)

    dt_l = jax.nn.softplus(_load_residue(sm_ref, c) + dtb_ref[...])
    tri, _, _ = _tri_masks_residue(c)
    ac_all = _cumsum_rows(dt_l * avec_ref[...], tri)
    ac_t2 = jnp.concatenate([ac_all, ac_all], axis=0).T

    lane = lax.broadcasted_iota(jnp.int32, (c, 2 * SSD_P), 1)
    lo_half = lane < SSD_P
    tri2, _, _ = _tri_masks_residue(c, 2 * SSD_P)
    lane1 = lane[0:1]
    gs_ = range(SSD_G)
    ps_ = range(SSD_HG // 2)

    xs_, bmb, hg, cb2, ch = [], [], [], [], []
    for g in gs_:
        xs_.append(_conv_cols(ext_ref, cw_ref, cb_ref, g * SSD_GW, SSD_GW, c))
        bmb.append(_conv_cols(ext_ref, cw_ref, cb_ref, SSD_INNER + g * SSD_N, SSD_N, c).astype(bf16))
        cmb = _conv_cols(ext_ref, cw_ref, cb_ref, SSD_INNER + SSD_BC + g * SSD_N, SSD_N, c).astype(bf16)
        hg.append(h_ref[pl.ds(g * SSD_HG, SSD_HG)].reshape(SSD_GW, SSD_N))
        cb2.append(_dot_nt(cmb, jnp.concatenate([bmb[g], bmb[g]], axis=0)))
        ch.append(_dot_nt(cmb, hg[g].astype(bf16)))
        yield

    m2, rhs, e_col, xsc = {}, {}, {}, {}
    for g in gs_:
        for p in ps_:
            l0 = SM_DT + g * SSD_HG + 2 * p
            ps = slice(p * 2 * SSD_P, (p + 1) * 2 * SSD_P)
            ac_col = jnp.where(lo_half, ac_all[:, l0:l0 + 1], ac_all[:, l0 + 1:l0 + 2])
            ac_row = jnp.where(lane1 < SSD_P, ac_t2[l0:l0 + 1, :], ac_t2[l0 + 1:l0 + 2, :])
            decay = jnp.exp(jnp.where(tri2, ac_col - ac_row, -jnp.inf))
            m2[g, p] = (cb2[g] * decay).astype(bf16)
            dt2 = jnp.where(lo_half, dt_l[:, l0:l0 + 1], dt_l[:, l0 + 1:l0 + 2])
            xdt = xs_[g][:, ps] * dt2
            rhs[g, p] = jnp.concatenate([jnp.where(lo_half, xdt, 0.0), jnp.where(lo_half, 0.0, xdt)],
                                        axis=0).astype(bf16)
            e_col[g, p] = jnp.exp(ac_col)
            xsc[g, p] = xdt * jnp.exp(ac_col[c - 1:c] - ac_col)
            yield
    y2 = {gp: _dot(m2[gp], rhs[gp]) for gp in m2}
    yield

    for g in gs_:
        xsg = jnp.concatenate([xsc[g, p] for p in ps_], axis=1).astype(bf16)
        dh = _dot_tn(xsg, bmb[g])
        for j in range(SSD_HG):
            ln = SM_DT + g * SSD_HG + j
            h_ref[g * SSD_HG + j] = (hg[g][j * SSD_P:(j + 1) * SSD_P] * jnp.exp(ac_all[c - 1:c, ln:ln + 1])
                                     + dh[j * SSD_P:(j + 1) * SSD_P])
        yield
    for g in gs_:
        gs = slice(g * SSD_GW, (g + 1) * SSD_GW)
        y = jnp.concatenate([y2[g, p] + ch[g][:, p * 2 * SSD_P:(p + 1) * 2 * SSD_P] * e_col[g, p] for p in ps_],
                            axis=1)
        _store_natural(nat_ref, y + dvec_ref[:, gs] * xs_[g], c, g * SSD_GW)
        y = _read_natural(nat_ref, g * SSD_GW, SSD_GW) * _silu(z_ref[:, gs].astype(f32))
        o_ref[:, gs] = _rms(y, nw_ref[:, gs]).astype(bf16)
        yield

    _conv_carry(ext_ref, c)


STEP_B = 8


def _conv_step(x_ref, c_ref, w_ref):
    w = w_ref[...]
    return c_ref[0] * w[0:1] + c_ref[1] * w[1:2] + c_ref[2] * w[2:3] + x_ref[...] * w[3:4]


def _expand_matrix(nb, width):
    r = lax.broadcasted_iota(jnp.int32, (nb, nb * width), 0)
    c = lax.broadcasted_iota(jnp.int32, (nb, nb * width), 1)
    return jnp.where((c >= r * width) & (c < (r + 1) * width), 1.0, 0.0).astype(bf16)


def _bcast_cols(x, e_mat):
    hi, lo = _split(x)
    return _dot_tn(hi, e_mat) + _dot_tn(lo, e_mat)


def _state_call(kernel, grid, in_specs, out_specs, out_shape, scratch, sem, name, args, prev_state):
    aliases = {}
    if prev_state is not None:
        in_specs = in_specs + [pl.BlockSpec(memory_space=pl.ANY)]
        args = args + (prev_state,)
        aliases = {len(args) - 1: len(out_shape) - 1}
        kernel = functools.partial(_drop_alias_ref, kernel, len(args) - 1)
    return pl.pallas_call(kernel, grid=grid, in_specs=in_specs, out_specs=out_specs, out_shape=out_shape,
                          scratch_shapes=scratch, input_output_aliases=aliases,
                          compiler_params=_cp(sem), name=name)(*args)


def _drop_alias_ref(kernel, idx, *refs):
    return kernel(*refs[:idx], *refs[idx + 1:])


def _gdn_step_kernel(alog_ref, dtb_ref, q_ref, k_ref, v_ref, z_ref, sm_ref, cq_ref, ck_ref, cv_ref,
                     wq_ref, wk_ref, wv_ref, nw_ref, s_ref, o_ref, so_ref):
    nb = q_ref.shape[0]
    qa = _silu(_conv_step(q_ref, cq_ref, wq_ref))
    ka = _silu(_conv_step(k_ref, ck_ref, wk_ref))
    va = _silu(_conv_step(v_ref, cv_ref, wv_ref))
    sm = sm_ref[...]
    z = z_ref[...]
    e_mat = _expand_matrix(nb, GDN_DV)
    for h in range(GDN_H):
        sl = slice(h * GDN_DK, (h + 1) * GDN_DK)
        q = qa[:, sl]
        k = ka[:, sl]
        v = va[:, sl]
        q = q * lax.rsqrt(jnp.sum(q * q, axis=-1, keepdims=True) + EPS) * (GDN_DK ** -0.5)
        k = k * lax.rsqrt(jnp.sum(k * k, axis=-1, keepdims=True) + EPS)
        beta = _sigmoid(sm[:, SM_BETA + h:SM_BETA + h + 1])
        g = -jnp.exp(_scalar_vec(alog_ref[h])) * jax.nn.softplus(sm[:, SM_DEC + h:SM_DEC + h + 1] + dtb_ref[h])
        eg = jnp.exp(g)
        qk = jnp.sum(q * k, axis=-1, keepdims=True)
        kb = _bcast_cols(k, e_mat)
        qb = _bcast_cols(q, e_mat)
        blk = lambda m, b: m[:, b * GDN_DV:(b + 1) * GDN_DV]
        ks = jnp.concatenate([jnp.sum(s_ref[b, h] * blk(kb, b), axis=0, keepdims=True) for b in range(nb)], axis=0)
        qs = jnp.concatenate([jnp.sum(s_ref[b, h] * blk(qb, b), axis=0, keepdims=True) for b in range(nb)], axis=0)
        w = beta * v - (beta * eg) * ks
        o = qs * eg + qk * w
        for b in range(nb):
            so_ref[b, h] = s_ref[b, h] * eg[b:b + 1] + blk(kb, b) * w[b:b + 1]
        o = _rms(o, nw_ref[...]) * _silu(z[:, sl])
        o_ref[:, sl] = o.astype(bf16)


def _gdn_step(proj, small, conv_t, states, prev, l, conv_w, a_log, dt_bias, norm_w):
    n = proj.shape[0]
    nb = STEP_B
    blk = lambda off: pl.BlockSpec((nb, GDN_QK), lambda i: (i, _blk(off, GDN_QK)))
    cst = lambda j: pl.BlockSpec((CONV_W - 1, nb, GDN_QK), lambda i: (0, i, j))
    cw = lambda j: pl.BlockSpec((CONV_W, GDN_QK), lambda i: (0, j))
    smem = pl.BlockSpec(memory_space=pltpu.SMEM)
    st = pl.BlockSpec((None, nb, GDN_H, GDN_DK, GDN_DV), lambda i: (l, i, 0, 0, 0))
    return _state_call(
        _gdn_step_kernel, (n // nb,),
        [smem, smem, blk(P_QKV_A), blk(P_QKV_A + GDN_QK), blk(P_QKV_A + 2 * GDN_QK), blk(P_Z_A),
         pl.BlockSpec((nb, LANE), lambda i: (i, 0)),
         cst(0), cst(1), cst(2), cw(0), cw(1), cw(2),
         pl.BlockSpec((1, GDN_DV), lambda i: (0, 0)), st],
        [pl.BlockSpec((nb, GDN_V), lambda i: (i, 0)), st],
        [jax.ShapeDtypeStruct((n, GDN_V), bf16), jax.ShapeDtypeStruct(states.shape, f32)],
        [], ("parallel",), "gdn_step",
        (a_log, dt_bias, proj, proj, proj, proj, small, conv_t, conv_t, conv_t, conv_w, conv_w, conv_w, norm_w, states),
        prev)


def _gla_step_kernel(q_ref, k_ref, v_ref, r_ref, sm_ref, wgate_ref, bgate_ref, nw_ref, s_ref,
                     o_ref, so_ref):
    nb = q_ref.shape[0]
    sm = sm_ref[...]
    lane = lax.broadcasted_iota(jnp.int32, sm.shape, 1)
    lr = jnp.where((lane >= SM_LR) & (lane < SM_LR + GLA_RANK), sm, 0.0).astype(bf16)
    la_all = jax.nn.log_sigmoid(_dot(lr, wgate_ref[...]) + bgate_ref[...]) / GLA_TAU
    qa = q_ref[...] * (GLA_DK ** -0.5)
    ka = k_ref[...]
    va = v_ref[...]
    r = r_ref[...]
    for h in range(GLA_H):
        sk = slice(h * GLA_DK, (h + 1) * GLA_DK)
        sv = slice(h * GLA_DV, (h + 1) * GLA_DV)
        q = qa[:, sk]
        k = ka[:, sk]
        v = va[:, sv]
        e = jnp.exp(la_all[:, sk])
        qk = jnp.sum(q * k, axis=-1, keepdims=True)
        e_t = e.T
        k_t = k.T
        qe_t = (q * e).T
        rows = []
        for b in range(nb):
            s = s_ref[b, h]
            vrow = v[b:b + 1]
            rows.append(qk[b:b + 1] * vrow + jnp.sum(s * qe_t[:, b:b + 1], axis=0, keepdims=True))
            so_ref[b, h] = s * e_t[:, b:b + 1] + k_t[:, b:b + 1] * vrow
        o = jnp.concatenate(rows, axis=0)
        o = _rms(o, nw_ref[...]) * _silu(r[:, sv])
        o_ref[:, sv] = o.astype(bf16)


def _gla_step(proj, small, states, prev, l, wgate_pad, bgate, norm_w):
    n = proj.shape[0]
    nb = STEP_B
    st = pl.BlockSpec((None, nb, GLA_H, GLA_DK, GLA_DV), lambda i: (l, i, 0, 0, 0))
    return _state_call(
        _gla_step_kernel, (n // nb,),
        [pl.BlockSpec((nb, GLA_QK), lambda i: (i, _blk(P_Q_B, GLA_QK))),
         pl.BlockSpec((nb, GLA_QK), lambda i: (i, _blk(P_K_B, GLA_QK))),
         pl.BlockSpec((nb, GLA_V), lambda i: (i, _blk(P_V_B, GLA_V))),
         pl.BlockSpec((nb, GLA_V), lambda i: (i, _blk(P_R_B, GLA_V))),
         pl.BlockSpec((nb, LANE), lambda i: (i, 0)),
         pl.BlockSpec((LANE, GLA_QK), lambda i: (0, 0)),
         pl.BlockSpec((1, GLA_QK), lambda i: (0, 0)),
         pl.BlockSpec((1, GLA_DV), lambda i: (0, 0)), st],
        [pl.BlockSpec((nb, GLA_V), lambda i: (i, 0)), st],
        [jax.ShapeDtypeStruct((n, GLA_V), bf16), jax.ShapeDtypeStruct(states.shape, f32)],
        [], ("parallel",), "gla_step",
        (proj, proj, proj, proj, small, wgate_pad, bgate, norm_w, states),
        prev)


def _ssd_step_kernel(alog_ref, dtb_ref, z_ref, xbc_ref, sm_ref, cst_ref, cw_ref, cb_ref, dvec_ref, nw_ref, h_ref,
                     o_ref, ho_ref):
    nb = xbc_ref.shape[0]
    xbc = _silu(_conv_step(xbc_ref, cst_ref, cw_ref) + cb_ref[...])
    sm_t = sm_ref[...].T
    e_mat = _expand_matrix(nb, SSD_N)
    lane_b = lax.broadcasted_iota(jnp.int32, (SSD_P, nb), 1)
    for g in range(SSD_G):
        gs = slice(g * SSD_GW, (g + 1) * SSD_GW)
        bm = xbc[:, SSD_INNER + g * SSD_N:SSD_INNER + (g + 1) * SSD_N]
        cm_t = xbc[:, SSD_INNER + SSD_BC + g * SSD_N:SSD_INNER + SSD_BC + (g + 1) * SSD_N].T
        cb_row = jnp.sum(cm_t * bm.T, axis=0, keepdims=True)
        hg = h_ref[:, pl.ds(g * SSD_HG, SSD_HG)].reshape(nb * SSD_GW, SSD_N)
        ch = _dot(hg.astype(bf16), cm_t.astype(bf16))
        pairs = []
        for jp in range(SSD_HG // 2):
            x_t = xbc[:, g * SSD_GW + jp * LANE:g * SSD_GW + (jp + 1) * LANE].T
            halves = []
            for jj in range(2):
                j = 2 * jp + jj
                hd = g * SSD_HG + j
                dt_row = jax.nn.softplus(sm_t[SM_DT + hd:SM_DT + hd + 1, :] + dtb_ref[hd])
                ea_row = jnp.exp(dt_row * (-jnp.exp(_scalar_vec(alog_ref[hd]))))
                xdt_t = x_t[jj * SSD_P:(jj + 1) * SSD_P] * dt_row
                hi, lo = _split(xdt_t)
                xb = _dot(hi, e_mat) + _dot(lo, e_mat)
                yh = jnp.zeros((SSD_P, nb), f32)
                for b in range(nb):
                    r0 = (b * SSD_HG + j) * SSD_P
                    yh = jnp.where(lane_b == b, ch[r0:r0 + SSD_P], yh)
                    ho_ref[b, hd] = (h_ref[b, hd] * ea_row[:, b:b + 1]
                                     + xb[:, b * SSD_N:(b + 1) * SSD_N] * bm[b:b + 1])
                halves.append(yh * ea_row + cb_row * xdt_t)
            pairs.append(jnp.concatenate(halves, axis=0).T)
        y = jnp.concatenate(pairs, axis=1)
        y = (y + dvec_ref[:, gs] * xbc[:, gs]) * _silu(z_ref[:, gs])
        o_ref[:, gs] = _rms(y, nw_ref[:, gs]).astype(bf16)


def _ssd_step(proj, small, conv_t, states, prev, l, conv_w, conv_b, a_log, dt_bias, dvec, norm_w):
    n = proj.shape[0]
    nb = STEP_B
    smem = pl.BlockSpec(memory_space=pltpu.SMEM)
    const = lambda shape: pl.BlockSpec(shape, lambda i: (0,) * len(shape))
    st = pl.BlockSpec((None, nb, SSD_H, SSD_P, SSD_N), lambda i: (l, i, 0, 0, 0))
    return _state_call(
        _ssd_step_kernel, (n // nb,),
        [smem, smem,
         pl.BlockSpec((nb, SSD_INNER), lambda i: (i, _blk(P_Z_C, SSD_INNER))),
         pl.BlockSpec((nb, SSD_CONV), lambda i: (i, _blk(P_XBC, SSD_CONV))),
         pl.BlockSpec((nb, LANE), lambda i: (i, 0)),
         pl.BlockSpec((CONV_W - 1, nb, SSD_CONV), lambda i: (0, i, 0)),
         const((CONV_W, SSD_CONV)), const((1, SSD_CONV)), const((1, SSD_INNER)), const((1, SSD_INNER)), st],
        [pl.BlockSpec((nb, SSD_INNER), lambda i: (i, 0)), st],
        [jax.ShapeDtypeStruct((n, SSD_INNER), bf16), jax.ShapeDtypeStruct(states.shape, f32)],
        [], ("parallel",), "ssd_step",
        (a_log, dt_bias, proj, proj, small, conv_t, conv_w, conv_b, dvec, norm_w, states),
        prev)


def _mixout_kernel(oa_ref, ob_ref, oc_ref, ga_ref, gb_ref, gc_ref, wa_ref, wb_ref, wc_ref, wo_ref,
                   x_ref, gt_ref, o_ref, acc_ref):
    j = pl.program_id(2)

    @pl.when(j == 0)
    def _():
        acc_ref[...] = jnp.zeros_like(acc_ref)

    m = (_sigmoid(ga_ref[...].astype(f32)) * _dot(oa_ref[...], wa_ref[...])
         + _sigmoid(gb_ref[...].astype(f32)) * _dot(ob_ref[...], wb_ref[...])
         + _sigmoid(gc_ref[...].astype(f32)) * _dot(oc_ref[...], wc_ref[...]))
    acc_ref[...] += _dot(m.astype(bf16), wo_ref[...])

    @pl.when(j == pl.num_programs(2) - 1)
    def _():
        o_ref[...] = x_ref[...] + gt_ref[...] * acc_ref[...]


def _mixout(oa, ob, oc, proj, x, mod, wa, wb, wc, wo, l, per_row, tm):
    bx, tx, dm = x.shape
    tn = 512
    gate = lambda k: pl.BlockSpec((None, tm, tn), lambda b, i, j: (b, i, _blk(P_GATES + k * D_MODEL, tn) + j))
    act = lambda w: pl.BlockSpec((None, tm, w), lambda b, i, j: (b, i, 0))
    wsp = lambda w: pl.BlockSpec((None, w, tn), lambda b, i, j: (l, 0, j))
    return pl.pallas_call(
        _mixout_kernel,
        grid=(bx, tx // tm, dm // tn),
        in_specs=[act(GDN_V), act(GLA_V), act(SSD_INNER), gate(0), gate(1), gate(2),
                  wsp(GDN_V), wsp(GLA_V), wsp(SSD_INNER),
                  pl.BlockSpec((None, tn, dm), lambda b, i, j: (l, j, 0)),
                  pl.BlockSpec((None, tm, dm), lambda b, i, j: (b, i, 0)),
                  _mod_spec(per_row, tm, 5)],
        out_specs=pl.BlockSpec((None, tm, dm), lambda b, i, j: (b, i, 0)),
        out_shape=jax.ShapeDtypeStruct(x.shape, f32),
        scratch_shapes=[pltpu.VMEM((tm, dm), f32)],
        compiler_params=_cp(("parallel", "parallel", "arbitrary")),
        name="mixout",
    )(oa, ob, oc, proj, proj, proj, wa, wb, wc, wo, x, mod)


REGROUP_ROWS = 512


def _regroup_kernel(w_ref, o_ref):
    o_ref[...] = w_ref[0].astype(bf16)


def _regroup_rows(wt):
    nl, _, d = wt.shape
    tr = REGROUP_ROWS
    shifts = []
    for name in MAIN_ORDER:
        a, b = W_IN_SRC[name]
        assert (b - a) % tr == 0 and a % SUB == 0
        shifts.append((P_MAIN[name] // tr, a - P_MAIN[name]))

    def src_row(i):
        off = i * (tr // SUB) + shifts[0][1] // SUB
        for k in range(1, len(shifts)):
            off = off + jnp.where(i >= shifts[k][0], (shifts[k][1] - shifts[k - 1][1]) // SUB, 0)
        return off * SUB

    return pl.pallas_call(
        _regroup_kernel,
        grid=(nl, P_TOTAL // tr),
        in_specs=[pl.BlockSpec((pl.Element(1), pl.Element(tr), pl.Element(d)), lambda l, i: (l, src_row(i), 0))],
        out_specs=pl.BlockSpec((None, tr, d), lambda l, i: (l, i, 0)),
        out_shape=jax.ShapeDtypeStruct((nl, P_TOTAL, d), bf16),
        compiler_params=_cp(("parallel", "parallel")),
        name="regroup_w_in",
    )(wt)


def _permute_w_in(w):
    nl, d, _ = w.shape
    wt = jnp.swapaxes(w, 1, 2)
    rows = lambda name: wt[:, W_IN_SRC[name][0]:W_IN_SRC[name][1]]
    main = _regroup_rows(wt)
    small = jnp.concatenate([rows(name) for name in SMALL_ORDER], axis=1)
    small = jnp.pad(small, ((0, 0), (0, LANE - small.shape[1]), (0, 0))).astype(bf16)
    return main, small


def _ffn_weights(wg, wu, wd):
    f = wg.shape[2]
    cut = f - f % FF_TILE
    w = tuple(a.astype(bf16) for a in (wg, wu, wd))
    if cut == f:
        return w
    return w + (w[0][:, :, cut:], w[1][:, :, cut:], w[2][:, cut:, :])


def _lane_vec(v, lo):
    return jnp.zeros((1, LANE), f32).at[0, lo:lo + v.shape[0]].set(v)


def _layer_params(l, p):
    row = lambda a: a[l].reshape(1, -1)
    wgate = jnp.zeros((LANE, GLA_QK), f32).at[SM_LR:SM_LR + GLA_RANK].set(p["gla_w_gate"][l]).astype(bf16)
    return dict(
        norm1=row(p["norm1"]), norm2=row(p["norm2"]), norm3=row(p["norm3"]),
        gdn_conv_w=p["gdn_conv_w"][l], gdn_a_log=p["gdn_a_log"][l], gdn_dt_bias=p["gdn_dt_bias"][l],
        gdn_alog_l=_lane_vec(p["gdn_a_log"][l], SM_DEC), gdn_dtb_l=_lane_vec(p["gdn_dt_bias"][l], SM_DEC),
        gdn_norm_w=row(p["gdn_norm_w"]),
        gla_wgate=wgate, gla_bgate=row(p["gla_b_gate"]), gla_norm_w=row(p["gla_norm_w"]),
        ssd_conv_w=p["ssd_conv_w"][l], ssd_conv_b=row(p["ssd_conv_b"]), ssd_a_log=p["ssd_a_log"][l],
        ssd_dt_bias=p["ssd_dt_bias"][l],
        ssd_avec_l=_lane_vec(-jnp.exp(p["ssd_a_log"][l]), SM_DT), ssd_dtb_l=_lane_vec(p["ssd_dt_bias"][l], SM_DT),
        ssd_dvec=jnp.repeat(p["ssd_d"][l], SSD_P).reshape(1, -1),
        ssd_norm_w=row(p["ssd_norm_w"]),
    )


def _stacked_weights(p):
    w_in, w_in_small = _permute_w_in(p["w_in"])
    return dict(
        f1=_ffn_weights(p["ffn1_wg"], p["ffn1_wu"], p["ffn1_wd"]),
        f2=_ffn_weights(p["ffn2_wg"], p["ffn2_wu"], p["ffn2_wd"]),
        w_in=w_in, w_in_small=w_in_small,
        wa=p["w_branch_gdn"].astype(bf16), wb=p["w_branch_gla"].astype(bf16),
        wc=p["w_branch_ssd"].astype(bf16), w_out=p["w_out"].astype(bf16),
    )


def _new_conv_state(buf, raw):
    t = raw.shape[1]
    k = CONV_W - 1
    if t >= k:
        return raw[:, t - k:]
    return jnp.concatenate([buf[:, t:], raw], axis=1)


def _mixer_prompt(proj, small, lp, st):
    gdn_conv, s_gdn, s_gla, ssd_conv, s_ssd = st
    b, t, _ = proj.shape
    c = CHUNK
    sm_spec = pl.BlockSpec((None, c, LANE), lambda bi, ti: (bi, ti, 0))
    st_a, st_b, st_c = _per_seq(GDN_H, GDN_DK, GDN_DV), _per_seq(GLA_H, GLA_DK, GLA_DV), _per_seq(SSD_H, SSD_P, SSD_N)
    gdn_specs = [_tok(GDN_CONV, P_QKV_A), _tok(GDN_V, P_Z_A), sm_spec, _per_seq(CONV_W - 1, GDN_CONV),
                 _const(CONV_W, GDN_CONV), _const(1, LANE), _const(1, LANE), _const(1, GDN_DV), st_a]
    gdn_args = (proj, proj, small, gdn_conv, lp["gdn_conv_w"], lp["gdn_alog_l"], lp["gdn_dtb_l"],
                lp["gdn_norm_w"], s_gdn)
    gla_specs = [_tok(GLA_QK, P_Q_B), _tok(GLA_QK, P_K_B), _tok(GLA_V, P_V_B), _tok(GLA_V, P_R_B), sm_spec,
                 _const(LANE, GLA_QK), _const(1, GLA_QK), _const(1, GLA_DV), st_b]
    gla_args = (proj, proj, proj, proj, small, lp["gla_wgate"], lp["gla_bgate"], lp["gla_norm_w"], s_gla)
    ssd_specs = [_tok(SSD_INNER, P_Z_C), _tok(SSD_CONV, P_XBC), sm_spec, _per_seq(CONV_W - 1, SSD_CONV),
                 _const(CONV_W, SSD_CONV), _const(1, SSD_CONV), _const(1, LANE), _const(1, LANE),
                 _const(1, SSD_INNER), _const(1, SSD_INNER), st_c]
    ssd_args = (proj, proj, small, ssd_conv, lp["ssd_conv_w"], lp["ssd_conv_b"], lp["ssd_avec_l"],
                lp["ssd_dtb_l"], lp["ssd_dvec"], lp["ssd_norm_w"], s_ssd)
    assert (len(gdn_specs), len(gla_specs), len(ssd_specs)) == (N_GDN_IN, N_GLA_IN, N_SSD_IN)
    out_tok = lambda width: pl.BlockSpec((None, c, width), lambda bi, ti: (bi, ti, 0))
    oa, s_gdn_n, ob, s_gla_n, oc, s_ssd_n = pl.pallas_call(
        _mixers_kernel,
        grid=(b, t // c),
        in_specs=gdn_specs + gla_specs + ssd_specs,
        out_specs=[out_tok(GDN_V), st_a, out_tok(GLA_V), st_b, out_tok(SSD_INNER), st_c],
        out_shape=[jax.ShapeDtypeStruct((b, t, GDN_V), bf16), jax.ShapeDtypeStruct(s_gdn.shape, f32),
                   jax.ShapeDtypeStruct((b, t, GLA_V), bf16), jax.ShapeDtypeStruct(s_gla.shape, f32),
                   jax.ShapeDtypeStruct((b, t, SSD_INNER), bf16), jax.ShapeDtypeStruct(s_ssd.shape, f32)],
        scratch_shapes=[pltpu.VMEM((GDN_CONV // LANE, 8 + c, LANE), f32), pltpu.VMEM((GLA_H, GLA_DV, GLA_DK), f32),
                        pltpu.VMEM((SSD_CONV // LANE, 8 + c, LANE), f32),
                        pltpu.VMEM((GDN_V // LANE, c, LANE), f32), pltpu.VMEM((SSD_INNER // LANE, c, LANE), f32)],
        compiler_params=_cp(("parallel", "arbitrary")),
        name="mixers_prompt",
    )(*gdn_args, *gla_args, *ssd_args)
    gdn_conv_n = _new_conv_state(gdn_conv, proj[:, :, P_QKV_A:P_QKV_A + GDN_CONV]).astype(f32)
    ssd_conv_n = _new_conv_state(ssd_conv, proj[:, :, P_XBC:P_XBC + SSD_CONV]).astype(f32)
    return (oa, ob, oc), (gdn_conv_n, s_gdn_n, s_gla_n, ssd_conv_n, s_ssd_n)


def _mixer_sample(proj, small, lp, l, states, prev):
    n = proj.shape[1]
    p2 = proj.reshape(n, P_TOTAL)
    s2 = small.reshape(n, LANE)
    gdn_conv, ssd_conv = states[0][l], states[3][l]
    gct = jnp.swapaxes(gdn_conv, 0, 1)
    sct = jnp.swapaxes(ssd_conv, 0, 1)
    pv = (None,) * 5 if prev is None else prev
    oa, s_gdn_n = _gdn_step(p2, s2, gct, states[1], pv[1], l, lp["gdn_conv_w"], lp["gdn_a_log"],
                            lp["gdn_dt_bias"], lp["gdn_norm_w"])
    ob, s_gla_n = _gla_step(p2, s2, states[2], pv[2], l, lp["gla_wgate"], lp["gla_bgate"], lp["gla_norm_w"])
    oc, s_ssd_n = _ssd_step(p2, s2, sct, states[4], pv[4], l, lp["ssd_conv_w"], lp["ssd_conv_b"], lp["ssd_a_log"],
                            lp["ssd_dt_bias"], lp["ssd_dvec"], lp["ssd_norm_w"])
    raw = p2.reshape(n, 1, P_TOTAL)
    gdn_conv_n = _new_conv_state(gdn_conv, raw[:, :, P_QKV_A:P_QKV_A + GDN_CONV])
    ssd_conv_n = _new_conv_state(ssd_conv, raw[:, :, P_XBC:P_XBC + SSD_CONV])
    outs = tuple(o.reshape(1, n, -1) for o in (oa, ob, oc))
    return outs, (gdn_conv_n, s_gdn_n, s_gla_n, ssd_conv_n, s_ssd_n)


def _trunk(x, mods, lps, sw, states, per_row, tm, tm_in, final_w):
    nl = len(lps)
    per_layer = []
    prev = None
    for l in range(nl):
        lp, mod = lps[l], mods[l]
        last = l == nl - 1
        x = _ffn(x, mod, 0, lp["norm1"], sw["f1"], l, lp["norm1"], per_row, tm, False)
        proj, small = _inproj(x, mod, lp["norm2"], sw["w_in"], sw["w_in_small"], l, per_row, tm_in,
                              f32 if per_row else bf16)
        if per_row:
            (oa, ob, oc), st = _mixer_sample(proj, small, lp, l, states, prev)
            prev = st
        else:
            (oa, ob, oc), st = _mixer_prompt(proj, small, lp, tuple(s[l] for s in states))
        per_layer.append(st)
        x = _mixout(oa, ob, oc, proj, x, mod, sw["wa"], sw["wb"], sw["wc"], sw["w_out"], l, per_row, tm)
        x = _ffn(x, mod, 6, lp["norm3"], sw["f2"], l, final_w if last else lp["norm3"], per_row, tm, last)
    stack = lambda i: jnp.stack([st[i] for st in per_layer])
    if per_row:
        new_states = (stack(0), prev[1], prev[2], stack(3), prev[4])
    else:
        new_states = tuple(stack(i) for i in range(5))
    return x, new_states


def kernel(x_prompt, x_sample, state_gdn_conv, state_gdn, state_gla, state_ssd_conv, state_ssd, c_prompt, c_sample, w_ada, b_ada, norm1, norm2, norm3, ffn1_wg, ffn1_wu, ffn1_wd, ffn2_wg, ffn2_wu, ffn2_wd, w_in, gdn_conv_w, gdn_a_log, gdn_dt_bias, gdn_norm_w, gla_w_gate, gla_b_gate, gla_norm_w, ssd_conv_w, ssd_conv_b, ssd_a_log, ssd_dt_bias, ssd_d, ssd_norm_w, w_branch_gdn, w_branch_gla, w_branch_ssd, w_out, final_norm):
    p = dict(norm1=norm1, norm2=norm2, norm3=norm3,
             ffn1_wg=ffn1_wg, ffn1_wu=ffn1_wu, ffn1_wd=ffn1_wd, ffn2_wg=ffn2_wg, ffn2_wu=ffn2_wu, ffn2_wd=ffn2_wd,
             w_in=w_in, gdn_conv_w=gdn_conv_w, gdn_a_log=gdn_a_log, gdn_dt_bias=gdn_dt_bias, gdn_norm_w=gdn_norm_w,
             gla_w_gate=gla_w_gate, gla_b_gate=gla_b_gate, gla_norm_w=gla_norm_w,
             ssd_conv_w=ssd_conv_w, ssd_conv_b=ssd_conv_b, ssd_a_log=ssd_a_log, ssd_dt_bias=ssd_dt_bias,
             ssd_d=ssd_d, ssd_norm_w=ssd_norm_w,
             w_branch_gdn=w_branch_gdn, w_branch_gla=w_branch_gla, w_branch_ssd=w_branch_ssd, w_out=w_out)
    nl = w_ada.shape[0]
    bp, tp, dm = x_prompt.shape
    bs = x_sample.shape[0]
    assert x_sample.shape[1] == 1 and tp % CHUNK == 0 and bs % STEP_B == 0 and dm == D_MODEL
    lps = [_layer_params(l, p) for l in range(nl)]
    sw = _stacked_weights(p)
    fw = final_norm.reshape(1, dm)

    rows = bp + bs
    rpad = -(-rows // 8) * 8
    c_all = jnp.concatenate([c_prompt, c_sample, jnp.zeros((rpad - rows, dm), f32)], axis=0)
    mod = _ada_mod(c_all, w_ada, b_ada)
    mod_p = [mod[l, :bp].reshape(bp, 1, N_MOD * dm) for l in range(nl)]
    mod_s = [mod[l, bp:rows].reshape(1, bs, N_MOD * dm) for l in range(nl)]

    sample_states = (state_gdn_conv, state_gdn, state_gla, state_ssd_conv, state_ssd)
    prompt_states = tuple(jnp.zeros((s.shape[0], bp) + s.shape[2:], x_prompt.dtype) for s in sample_states)
    tm_p = 512 if tp % 512 == 0 else CHUNK
    tm_in = 1024 if tp % 1024 == 0 else tm_p
    y_p, st_p = _trunk(x_prompt, mod_p, lps, sw, prompt_states, False, tm_p, tm_in, fw)
    y_s, st_s = _trunk(x_sample.reshape(1, bs, dm), mod_s, lps, sw, sample_states, True, bs, bs, fw)
    return (y_p, y_s.reshape(bs, 1, dm)) + st_p + st_s
```

```python
import functools

import jax
import jax.numpy as jnp
from jax import lax
from jax.experimental import pallas as pl
from jax.experimental.pallas import tpu as pltpu

f32 = jnp.float32
bf16 = jnp.bfloat16
HI = lax.Precision.HIGHEST

EPS = 1e-6
D_MODEL = 2048
N_MOD = 9
CHUNK = 64
CONV_W = 4
GDN_H, GDN_DK, GDN_DV = 8, 128, 128
GLA_H, GLA_DK, GLA_DV, GLA_RANK, GLA_TAU = 4, 128, 256, 16, 16.0
SSD_H, SSD_P, SSD_G, SSD_N = 32, 64, 4, 128
SSD_HG = SSD_H // SSD_G
GDN_QK = GDN_H * GDN_DK
GDN_V = GDN_H * GDN_DV
GDN_CONV = 2 * GDN_QK + GDN_V
GLA_QK = GLA_H * GLA_DK
GLA_V = GLA_H * GLA_DV
SSD_INNER = SSD_H * SSD_P
SSD_BC = SSD_G * SSD_N
SSD_CONV = SSD_INNER + 2 * SSD_BC
SSD_GW = SSD_HG * SSD_P

LANE = 128
FF_TILE = 512
FF_SUB = 256

IN_SPLITS = (("qkv_a", GDN_CONV), ("z_a", GDN_V), ("beta", GDN_H), ("dec", GDN_H),
             ("q_b", GLA_QK), ("k_b", GLA_QK), ("v_b", GLA_V), ("lr", GLA_RANK), ("r_b", GLA_V),
             ("z_c", SSD_INNER), ("xbc", SSD_CONV), ("dt", SSD_H), ("gates", 3 * D_MODEL))
MAIN_ORDER = ("qkv_a", "xbc", "z_c", "gates", "z_a", "q_b", "k_b", "v_b", "r_b")
SMALL_ORDER = ("beta", "dec", "lr", "dt")


def _layout():
    src, off = {}, 0
    for name, w in IN_SPLITS:
        src[name] = (off, off + w)
        off += w
    main, small, d = {}, {}, 0
    for name in MAIN_ORDER:
        w = src[name][1] - src[name][0]
        main[name] = d
        d += w
    total = d
    d = 0
    for name in SMALL_ORDER:
        small[name] = d
        d += src[name][1] - src[name][0]
    assert d <= LANE
    return src, main, small, total


W_IN_SRC, P_MAIN, P_SM, P_TOTAL = _layout()
P_QKV_A, P_XBC, P_Z_C, P_GATES, P_Z_A = (P_MAIN[k] for k in ("qkv_a", "xbc", "z_c", "gates", "z_a"))
P_Q_B, P_K_B, P_V_B, P_R_B = (P_MAIN[k] for k in ("q_b", "k_b", "v_b", "r_b"))
SM_BETA, SM_DEC, SM_LR, SM_DT = (P_SM[k] for k in SMALL_ORDER)

VMEM_LIMIT = 56 * 1024 * 1024


def _cp(sem):
    return pltpu.CompilerParams(dimension_semantics=sem, vmem_limit_bytes=VMEM_LIMIT)


def _blk(off, width):
    assert off % width == 0, (off, width)
    return off // width


def _sigmoid(x):
    return 0.5 + 0.5 * jnp.tanh(0.5 * x)


def _silu(x):
    h = 0.5 * x
    return h + h * jnp.tanh(h)


def _rms(x, w):
    return x * lax.rsqrt(jnp.mean(x * x, axis=-1, keepdims=True) + EPS) * w


def _dot(a, b):
    return jnp.dot(a, b, preferred_element_type=f32)


def _dot_nt(a, b):
    return lax.dot_general(a, b, (((1,), (1,)), ((), ())), preferred_element_type=f32)


def _dot_tn(a, b):
    return lax.dot_general(a, b, (((0,), (0,)), ((), ())), preferred_element_type=f32)


def _split(x):
    hi = x.astype(bf16)
    return hi, (x - hi.astype(f32)).astype(bf16)


def _dot3(a, b):
    return _dot(a[0], b[0]) + (_dot(a[0], b[1]) + _dot(a[1], b[0]))


def _dot_hi(a, b):
    return jnp.dot(a, b, precision=HI, preferred_element_type=f32)


def _tri_masks(c):
    row = lax.broadcasted_iota(jnp.int32, (c, c), 0)
    col = lax.broadcasted_iota(jnp.int32, (c, c), 1)
    return row >= col, row > col, row == col


SUB = 8


def _row_time(idx, c):
    n = c // SUB
    assert n & (n - 1) == 0, "chunk / SUB must be a power of two"
    sh = n.bit_length() - 1
    return ((idx & (n - 1)) << 3) | (idx >> sh)


def _tri_masks_residue(c, cols=None):
    cols = c if cols is None else cols
    row = _row_time(lax.broadcasted_iota(jnp.int32, (c, cols), 0), c)
    col = _row_time(lax.broadcasted_iota(jnp.int32, (c, cols), 1) & (c - 1), c)
    return row >= col, row > col, row == col


def _load_residue(ref, c, sl=slice(None)):
    n = c // SUB
    return jnp.concatenate([ref[pl.ds(r, n, stride=SUB), sl] for r in range(SUB)], axis=0)


def _store_natural(nat_ref, x, c, lo):
    n = c // SUB
    for j in range(x.shape[1] // LANE):
        for r in range(SUB):
            nat_ref[lo // LANE + j, pl.ds(r, n, stride=SUB), :] = x[r * n:(r + 1) * n, j * LANE:(j + 1) * LANE]


def _read_natural(nat_ref, lo, width):
    return jnp.concatenate([nat_ref[lo // LANE + j] for j in range(width // LANE)], axis=1)


def _lane_col(x, idx):
    lane = lax.broadcasted_iota(jnp.int32, x.shape, 1)
    return jnp.sum(jnp.where(lane == idx, x, 0.0), axis=1, keepdims=True)


def _scalar_vec(s):
    return jnp.full((1, 1), s, f32)


def _ada_kernel(c_ref, w_ref, b_ref, o_ref):
    s = _silu(c_ref[...]).astype(bf16)
    o_ref[...] = _dot(s, w_ref[...].astype(bf16)) + b_ref[...]


def _ada_mod(c_all, w_ada, b_ada):
    nl, dm, n = w_ada.shape
    r = c_all.shape[0]
    tn = 1024
    return pl.pallas_call(
        _ada_kernel,
        grid=(nl, n // tn),
        in_specs=[pl.BlockSpec((r, dm), lambda l, j: (0, 0)),
                  pl.BlockSpec((None, dm, tn), lambda l, j: (l, 0, j)),
                  pl.BlockSpec((None, 1, tn), lambda l, j: (l, 0, j))],
        out_specs=pl.BlockSpec((None, r, tn), lambda l, j: (l, 0, j)),
        out_shape=jax.ShapeDtypeStruct((nl, r, n), f32),
        compiler_params=_cp(("arbitrary", "arbitrary")),
        name="ada_mod",
    )(c_all, w_ada, b_ada.reshape(nl, 1, n))


def _mod_spec(per_row, tm, chunk):
    if per_row:
        return pl.BlockSpec((None, tm, D_MODEL), lambda b, i, j: (b, i, chunk))
    return pl.BlockSpec((None, 1, D_MODEL), lambda b, i, j: (b, 0, chunk))


def _ffn_kernel(x_ref, sh_ref, sc_ref, gt_ref, nw_ref, fw_ref, wg_ref, wu_ref, wd_ref, *rest, final, has_tail):
    if has_tail:
        wgt_ref, wut_ref, wdt_ref, o_ref, h_ref, acc_ref = rest
    else:
        o_ref, h_ref, acc_ref = rest
    f = pl.program_id(2)
    last = pl.num_programs(2) - 1

    @pl.when(f == 0)
    def _():
        y = _rms(x_ref[...], nw_ref[...])
        h_ref[...] = (y * (1.0 + sc_ref[...]) + sh_ref[...]).astype(bf16)
        acc_ref[...] = jnp.zeros_like(acc_ref)

    def accumulate(wg, wu, wd):
        h = h_ref[...]
        width = wg.shape[1]
        sub = FF_SUB if width % FF_SUB == 0 else width
        parts = [(_dot(h, wg[:, s:s + sub]), _dot(h, wu[:, s:s + sub])) for s in range(0, width, sub)]
        acc = acc_ref[...]
        for i, (g, u) in enumerate(parts):
            acc = acc + _dot((_silu(g) * u).astype(bf16), wd[i * sub:(i + 1) * sub, :])
        acc_ref[...] = acc

    if has_tail:
        pl.when(f < last)(lambda: accumulate(wg_ref, wu_ref, wd_ref))
        pl.when(f == last)(lambda: accumulate(wgt_ref, wut_ref, wdt_ref))
    else:
        accumulate(wg_ref, wu_ref, wd_ref)

    @pl.when(f == last)
    def _():
        y = x_ref[...] + 0.5 * gt_ref[...] * acc_ref[...]
        if final:
            y = _rms(y, fw_ref[...])
        o_ref[...] = y


def _ffn(x, mod, k0, nw, w, l, fw, per_row, tm, final):
    bx, tx, dm = x.shape
    tf = FF_TILE
    n_full = w[0].shape[2] // tf
    has_tail = len(w) > 3
    full = lambda f: jnp.minimum(f, n_full - 1)
    w_specs = [pl.BlockSpec((None, dm, tf), lambda b, i, f: (l, 0, full(f))),
               pl.BlockSpec((None, dm, tf), lambda b, i, f: (l, 0, full(f))),
               pl.BlockSpec((None, tf, dm), lambda b, i, f: (l, full(f), 0))]
    if has_tail:
        ft = w[3].shape[2]
        w_specs += [pl.BlockSpec((None, dm, ft), lambda b, i, f: (l, 0, 0)),
                    pl.BlockSpec((None, dm, ft), lambda b, i, f: (l, 0, 0)),
                    pl.BlockSpec((None, ft, dm), lambda b, i, f: (l, 0, 0))]
    return pl.pallas_call(
        functools.partial(_ffn_kernel, final=final, has_tail=has_tail),
        grid=(bx, tx // tm, n_full + int(has_tail)),
        in_specs=[pl.BlockSpec((None, tm, dm), lambda b, i, f: (b, i, 0)),
                  _mod_spec(per_row, tm, k0), _mod_spec(per_row, tm, k0 + 1), _mod_spec(per_row, tm, k0 + 2),
                  pl.BlockSpec((1, dm), lambda b, i, f: (0, 0)),
                  pl.BlockSpec((1, dm), lambda b, i, f: (0, 0))] + w_specs,
        out_specs=pl.BlockSpec((None, tm, dm), lambda b, i, f: (b, i, 0)),
        out_shape=jax.ShapeDtypeStruct(x.shape, f32),
        scratch_shapes=[pltpu.VMEM((tm, dm), bf16), pltpu.VMEM((tm, dm), f32)],
        compiler_params=_cp(("parallel", "parallel", "arbitrary")),
        name="ffn",
    )(x, mod, mod, mod, nw, fw, *w)


def _inproj_kernel(x_ref, sh_ref, sc_ref, nw_ref, w_ref, ws_ref, o_ref, os_ref, h_ref):
    @pl.when(pl.program_id(2) == 0)
    def _():
        y = _rms(x_ref[...], nw_ref[...])
        h = (y * (1.0 + sc_ref[...]) + sh_ref[...]).astype(bf16)
        h_ref[...] = h
        os_ref[...] = _dot_nt(h, ws_ref[...].astype(bf16))

    o_ref[...] = _dot_nt(h_ref[...], w_ref[...]).astype(o_ref.dtype)


def _inproj(x, mod, nw, w, ws, l, per_row, tm, out_dtype):
    bx, tx, dm = x.shape
    n = w.shape[1]
    tn = 1024
    return pl.pallas_call(
        _inproj_kernel,
        grid=(bx, tx // tm, n // tn),
        in_specs=[pl.BlockSpec((None, tm, dm), lambda b, i, j: (b, i, 0)),
                  _mod_spec(per_row, tm, 3), _mod_spec(per_row, tm, 4),
                  pl.BlockSpec((1, dm), lambda b, i, j: (0, 0)),
                  pl.BlockSpec((None, tn, dm), lambda b, i, j: (l, j, 0)),
                  pl.BlockSpec((None, LANE, dm), lambda b, i, j: (l, 0, 0))],
        out_specs=[pl.BlockSpec((None, tm, tn), lambda b, i, j: (b, i, j)),
                   pl.BlockSpec((None, tm, LANE), lambda b, i, j: (b, i, 0))],
        out_shape=[jax.ShapeDtypeStruct((bx, tx, n), out_dtype),
                   jax.ShapeDtypeStruct((bx, tx, LANE), f32)],
        scratch_shapes=[pltpu.VMEM((tm, dm), bf16)],
        compiler_params=_cp(("parallel", "parallel", "arbitrary")),
        name="inproj",
    )(x, mod, mod, nw, w, ws)


def _conv_load(ext_ref, x_ref, cst_ref, first):
    c = x_ref.shape[0]
    nblk = ext_ref.shape[0]

    @pl.when(first)
    def _():
        for j in range(nblk):
            ext_ref[j, pl.ds(5, CONV_W - 1), :] = cst_ref[:, j * LANE:(j + 1) * LANE]

    for j in range(nblk):
        ext_ref[j, pl.ds(8, c), :] = x_ref[:, j * LANE:(j + 1) * LANE].astype(f32)


def _conv_cols(ext_ref, w_ref, b_ref, lo, width, c):
    n = c // SUB
    cols = []
    for blk in range(lo // LANE, (lo + width) // LANE):
        sl = pl.ds(blk * LANE, LANE)
        w = w_ref[:, sl]
        taps = [ext_ref[blk, pl.ds(8 - (CONV_W - 1) + k, n, stride=SUB), :] for k in range(SUB + CONV_W - 1)]
        ys = []
        for r in range(SUB):
            y = taps[r] * w[0:1]
            for j in range(1, CONV_W):
                y = y + taps[r + j] * w[j:j + 1]
            ys.append(y)
        y = jnp.concatenate(ys, axis=0)
        if b_ref is not None:
            y = y + b_ref[:, sl]
        cols.append(_silu(y))
    return cols[0] if len(cols) == 1 else jnp.concatenate(cols, axis=1)


def _conv_carry(ext_ref, c):
    for j in range(ext_ref.shape[0]):
        ext_ref[j, pl.ds(5, CONV_W - 1), :] = ext_ref[j, pl.ds(8 + c - (CONV_W - 1), CONV_W - 1), :]


def _cumsum_rows(x, tri):
    return _dot_hi(jnp.where(tri, 1.0, 0.0).astype(f32), x)


def _gdn_kernel(qkv_ref, z_ref, sm_ref, cst_ref, cw_ref, alog_ref, dtb_ref, nw_ref, s0_ref,
                o_ref, s_ref, ext_ref, nat_ref):
    c = qkv_ref.shape[0]
    first = pl.program_id(1) == 0

    @pl.when(first)
    def _():
        s_ref[...] = s0_ref[...]

    _conv_load(ext_ref, qkv_ref, cst_ref, first)

    sm = _load_residue(sm_ref, c)
    beta_l = _sigmoid(sm)
    g_l = -jnp.exp(alog_ref[...]) * jax.nn.softplus(sm + dtb_ref[...])
    tri, strict, eye = _tri_masks_residue(c)
    gc_all = _cumsum_rows(g_l, tri)
    gc_t = gc_all.T
    eye_f = jnp.where(eye, 1.0, 0.0).astype(f32)
    n_fac = max(1, (c - 1).bit_length() - 1)

    hs = range(GDN_H)
    qb, kb, ks, decay, eg, gcs, nm, rhs = [], [], [], [], [], [], [], []
    for h in hs:
        q = _conv_cols(ext_ref, cw_ref, None, h * GDN_DK, GDN_DK, c)
        k = _conv_cols(ext_ref, cw_ref, None, GDN_QK + h * GDN_DK, GDN_DK, c)
        v = _conv_cols(ext_ref, cw_ref, None, 2 * GDN_QK + h * GDN_DV, GDN_DV, c)
        q = q * lax.rsqrt(jnp.sum(q * q, axis=-1, keepdims=True) + EPS) * (GDN_DK ** -0.5)
        k = k * lax.rsqrt(jnp.sum(k * k, axis=-1, keepdims=True) + EPS)
        beta = beta_l[:, SM_BETA + h:SM_BETA + h + 1]
        gc = gc_all[:, SM_DEC + h:SM_DEC + h + 1]
        d = jnp.exp(jnp.where(tri, gc - gc_t[SM_DEC + h:SM_DEC + h + 1, :], -jnp.inf))
        e = jnp.exp(gc)
        qb.append(q.astype(bf16))
        kb.append(k.astype(bf16))
        ks.append(k)
        decay.append(d)
        eg.append(e)
        gcs.append(gc)
        nm.append(jnp.where(strict, beta * d * _dot_nt(kb[h], kb[h]), 0.0))
        rhs.append(jnp.concatenate([v * beta, k * (beta * e)], axis=1))
        yield

    inv = [eye_f - nm[h] for h in hs]
    pw_s = [_split(nm[h]) for h in hs]
    pw_s = [_split(_dot3(pw_s[h], pw_s[h])) for h in hs]
    yield
    for i in range(n_fac):
        inv = [inv[h] + _dot3(_split(inv[h]), pw_s[h]) for h in hs]
        yield
        if i + 1 < n_fac:
            pw_s = [_split(_dot3(pw_s[h], pw_s[h])) for h in hs]
            yield
    sol = []
    for h in hs:
        inv_hi, inv_lo = _split(inv[h])
        rb = rhs[h].astype(bf16)
        sol.append(_dot(inv_hi, rb) + _dot(inv_lo, rb))
    yield

    s = [s_ref[h] for h in hs]
    sb = [s[h].astype(bf16) for h in hs]
    wb = [(sol[h][:, 0:GDN_DV] - _dot(sol[h][:, GDN_DV:].astype(bf16), sb[h])).astype(bf16) for h in hs]
    yield
    qk = [(_dot_nt(qb[h], kb[h]) * decay[h]).astype(bf16) for h in hs]
    yield
    o = [_dot(qb[h], sb[h]) * eg[h] + _dot(qk[h], wb[h]) for h in hs]
    yield
    for h in hs:
        g_last = gcs[h][c - 1:c]
        kd = (ks[h] * jnp.exp(g_last - gcs[h])).astype(bf16)
        s_ref[h] = s[h] * jnp.exp(g_last) + _dot_tn(kd, wb[h])
    yield
    for h in hs:
        sl = slice(h * GDN_DV, (h + 1) * GDN_DV)
        _store_natural(nat_ref, _rms(o[h], nw_ref[...]), c, h * GDN_DV)
        o_ref[:, sl] = (_read_natural(nat_ref, h * GDN_DV, GDN_DV) * _silu(z_ref[:, sl].astype(f32))).astype(bf16)
        yield

    _conv_carry(ext_ref, c)


def _tok(width, off):
    return pl.BlockSpec((None, CHUNK, width), lambda bi, ti: (bi, ti, _blk(off, width)))


def _per_seq(*shape):
    return pl.BlockSpec((None,) + shape, lambda bi, ti: (bi,) + (0,) * len(shape))


def _const(*shape):
    return pl.BlockSpec(shape, lambda bi, ti: (0,) * len(shape))


N_GDN_IN, N_GLA_IN, N_SSD_IN = 9, 9, 11
_DONE = object()


def _mixers_kernel(*refs):
    i0, i1, i2 = N_GDN_IN, N_GDN_IN + N_GLA_IN, N_GDN_IN + N_GLA_IN + N_SSD_IN
    gdn_in, gla_in, ssd_in = refs[:i0], refs[i0:i1], refs[i1:i2]
    oa_ref, sa_ref, ob_ref, sb_ref, oc_ref, sc_ref, ext_a_ref, st_b_ref, ext_c_ref, nat_a_ref, nat_c_ref = refs[i2:]
    live = [_gdn_kernel(*gdn_in, oa_ref, sa_ref, ext_a_ref, nat_a_ref),
            _gla_kernel(*gla_in, ob_ref, sb_ref, st_b_ref),
            _ssd_kernel(*ssd_in, oc_ref, sc_ref, ext_c_ref, nat_c_ref)]
    while live:
        for gen in list(live):
            if next(gen, _DONE) is _DONE:
                live.remove(gen)


GLA_SUB = 16


def _gla_kernel(q_ref, k_ref, v_ref, r_ref, sm_ref, wgate_ref, bgate_ref, nw_ref, s0_ref,
                o_ref, so_ref, st_ref):
    c = q_ref.shape[0]
    ti = pl.program_id(1)

    @pl.when(ti == 0)
    def _():
        for h in range(GLA_H):
            st_ref[h] = s0_ref[h].T

    sm = sm_ref[...]
    lane = lax.broadcasted_iota(jnp.int32, sm.shape, 1)
    lr = jnp.where((lane >= SM_LR) & (lane < SM_LR + GLA_RANK), sm, 0.0).astype(bf16)
    la = jax.nn.log_sigmoid(_dot(lr, wgate_ref[...]) + bgate_ref[...]) / GLA_TAU
    tri, _, _ = _tri_masks(c)
    b_all = _cumsum_rows(la, tri)

    col = lax.broadcasted_iota(jnp.int32, (GLA_SUB, c), 1)
    row = lax.broadcasted_iota(jnp.int32, (GLA_SUB, c), 0)
    hs = range(GLA_H)
    qs, ks, vbs, bs, atts = [], [], [], [], []
    for h in hs:
        sk = slice(h * GLA_DK, (h + 1) * GLA_DK)
        q = q_ref[:, sk].astype(f32) * (GLA_DK ** -0.5)
        k = k_ref[:, sk].astype(f32)
        b = b_all[:, sk]
        qs.append(q)
        ks.append(k)
        bs.append(b)
        vbs.append(v_ref[:, h * GLA_DV:(h + 1) * GLA_DV].astype(bf16))
        att_rows = []
        for i in range(c // GLA_SUB):
            lo = i * GLA_SUB
            b_i = b[lo:lo + GLA_SUB]
            q_i = q[lo:lo + GLA_SUB]
            b_top = b[lo:lo + 1]
            att = jnp.zeros((GLA_SUB, c), f32)
            if i > 0:
                q_t = (q_i * jnp.exp(b_i - b_top)).astype(bf16)
                k_t = (k * jnp.exp(jnp.minimum(b_top - b, 0.0))).astype(bf16)
                att = jnp.where(col < lo, _dot_nt(q_t, k_t), 0.0)
            for sl in range(GLA_SUB):
                sidx = lo + sl
                e = jnp.exp(jnp.minimum(b_i - b[sidx:sidx + 1], 0.0))
                p = jnp.sum(q_i * k[sidx:sidx + 1] * e, axis=1, keepdims=True)
                att = jnp.where((col == sidx) & (row >= sl), p, att)
            att_rows.append(att)
            yield
        atts.append(jnp.concatenate(att_rows, axis=0).astype(bf16))

    sts = [st_ref[h] for h in hs]
    os_ = [_dot(atts[h], vbs[h]) + _dot_nt((qs[h] * jnp.exp(bs[h])).astype(bf16), sts[h].astype(bf16))
           for h in hs]
    yield
    for h in hs:
        b_last = bs[h][c - 1:c]
        kd = (ks[h] * jnp.exp(b_last - bs[h])).astype(bf16)
        st_ref[h] = sts[h] * jnp.exp(b_last) + _dot_tn(vbs[h], kd)
        yield
    for h in hs:
        sv = slice(h * GLA_DV, (h + 1) * GLA_DV)
        o_ref[:, sv] = (_rms(os_[h], nw_ref[...]) * _silu(r_ref[:, sv].astype(f32))).astype(bf16)
        yield

    @pl.when(ti == pl.num_programs(1) - 1)
    def _():
        for h in hs:
            so_ref[h] = st_ref[h].T


def _ssd_kernel(z_ref, xbc_ref, sm_ref, cst_ref, cw_ref, cb_ref, avec_ref, dtb_ref, dvec_ref, nw_ref, h0_ref,
                o_ref, h_ref, ext_ref, nat_ref):
    c = xbc_ref.shape[0]
    assert c == SSD_P, "head pairs share a lane tile: the chunk must be as wide as a head"
    first = pl.program_id(1) == 0

    @pl.when(first)
    def _():
        h_ref[...] = h0_ref[...]

    _conv_load(ext_ref, xbc_ref, cst_ref, first)

    dt_l = jax.nn.softplus(_load_residue(sm_ref, c) + dtb_ref[...])
    tri, _, _ = _tri_masks_residue(c)
    ac_all = _cumsum_rows(dt_l * avec_ref[...], tri)
    ac_t2 = jnp.concatenate([ac_all, ac_all], axis=0).T

    lane = lax.broadcasted_iota(jnp.int32, (c, 2 * SSD_P), 1)
    lo_half = lane < SSD_P
    tri2, _, _ = _tri_masks_residue(c, 2 * SSD_P)
    lane1 = lane[0:1]
    gs_ = range(SSD_G)
    ps_ = range(SSD_HG // 2)

    xs_, bmb, hg, cb2, ch = [], [], [], [], []
    for g in gs_:
        xs_.append(_conv_cols(ext_ref, cw_ref, cb_ref, g * SSD_GW, SSD_GW, c))
        bmb.append(_conv_cols(ext_ref, cw_ref, cb_ref, SSD_INNER + g * SSD_N, SSD_N, c).astype(bf16))
        cmb = _conv_cols(ext_ref, cw_ref, cb_ref, SSD_INNER + SSD_BC + g * SSD_N, SSD_N, c).astype(bf16)
        hg.append(h_ref[pl.ds(g * SSD_HG, SSD_HG)].reshape(SSD_GW, SSD_N))
        cb2.append(_dot_nt(cmb, jnp.concatenate([bmb[g], bmb[g]], axis=0)))
        ch.append(_dot_nt(cmb, hg[g].astype(bf16)))
        yield

    m2, rhs, e_col, xsc = {}, {}, {}, {}
    for g in gs_:
        for p in ps_:
            l0 = SM_DT + g * SSD_HG + 2 * p
            ps = slice(p * 2 * SSD_P, (p + 1) * 2 * SSD_P)
            ac_col = jnp.where(lo_half, ac_all[:, l0:l0 + 1], ac_all[:, l0 + 1:l0 + 2])
            ac_row = jnp.where(lane1 < SSD_P, ac_t2[l0:l0 + 1, :], ac_t2[l0 + 1:l0 + 2, :])
            decay = jnp.exp(jnp.where(tri2, ac_col - ac_row, -jnp.inf))
            m2[g, p] = (cb2[g] * decay).astype(bf16)
            dt2 = jnp.where(lo_half, dt_l[:, l0:l0 + 1], dt_l[:, l0 + 1:l0 + 2])
            xdt = xs_[g][:, ps] * dt2
            rhs[g, p] = jnp.concatenate([jnp.where(lo_half, xdt, 0.0), jnp.where(lo_half, 0.0, xdt)],
                                        axis=0).astype(bf16)
            e_col[g, p] = jnp.exp(ac_col)
            xsc[g, p] = xdt * jnp.exp(ac_col[c - 1:c] - ac_col)
            yield
    y2 = {gp: _dot(m2[gp], rhs[gp]) for gp in m2}
    yield

    for g in gs_:
        xsg = jnp.concatenate([xsc[g, p] for p in ps_], axis=1).astype(bf16)
        dh = _dot_tn(xsg, bmb[g])
        for j in range(SSD_HG):
            ln = SM_DT + g * SSD_HG + j
            h_ref[g * SSD_HG + j] = (hg[g][j * SSD_P:(j + 1) * SSD_P] * jnp.exp(ac_all[c - 1:c, ln:ln + 1])
                                     + dh[j * SSD_P:(j + 1) * SSD_P])
        yield
    for g in gs_:
        gs = slice(g * SSD_GW, (g + 1) * SSD_GW)
        y = jnp.concatenate([y2[g, p] + ch[g][:, p * 2 * SSD_P:(p + 1) * 2 * SSD_P] * e_col[g, p] for p in ps_],
                            axis=1)
        _store_natural(nat_ref, y + dvec_ref[:, gs] * xs_[g], c, g * SSD_GW)
        y = _read_natural(nat_ref, g * SSD_GW, SSD_GW) * _silu(z_ref[:, gs].astype(f32))
        o_ref[:, gs] = _rms(y, nw_ref[:, gs]).astype(bf16)
        yield

    _conv_carry(ext_ref, c)


STEP_B = 8


def _conv_step(x_ref, c_ref, w_ref):
    w = w_ref[...]
    return c_ref[0] * w[0:1] + c_ref[1] * w[1:2] + c_ref[2] * w[2:3] + x_ref[...] * w[3:4]


def _expand_matrix(nb, width):
    r = lax.broadcasted_iota(jnp.int32, (nb, nb * width), 0)
    c = lax.broadcasted_iota(jnp.int32, (nb, nb * width), 1)
    return jnp.where((c >= r * width) & (c < (r + 1) * width), 1.0, 0.0).astype(bf16)


def _bcast_cols(x, e_mat):
    hi, lo = _split(x)
    return _dot_tn(hi, e_mat) + _dot_tn(lo, e_mat)


def _state_call(kernel, grid, in_specs, out_specs, out_shape, scratch, sem, name, args, prev_state):
    aliases = {}
    if prev_state is not None:
        in_specs = in_specs + [pl.BlockSpec(memory_space=pl.ANY)]
        args = args + (prev_state,)
        aliases = {len(args) - 1: len(out_shape) - 1}
        kernel = functools.partial(_drop_alias_ref, kernel, len(args) - 1)
    return pl.pallas_call(kernel, grid=grid, in_specs=in_specs, out_specs=out_specs, out_shape=out_shape,
                          scratch_shapes=scratch, input_output_aliases=aliases,
                          compiler_params=_cp(sem), name=name)(*args)


def _drop_alias_ref(kernel, idx, *refs):
    return kernel(*refs[:idx], *refs[idx + 1:])


def _gdn_step_kernel(alog_ref, dtb_ref, q_ref, k_ref, v_ref, z_ref, sm_ref, cq_ref, ck_ref, cv_ref,
                     wq_ref, wk_ref, wv_ref, nw_ref, s_ref, o_ref, so_ref):
    nb = q_ref.shape[0]
    qa = _silu(_conv_step(q_ref, cq_ref, wq_ref))
    ka = _silu(_conv_step(k_ref, ck_ref, wk_ref))
    va = _silu(_conv_step(v_ref, cv_ref, wv_ref))
    sm = sm_ref[...]
    z = z_ref[...]
    e_mat = _expand_matrix(nb, GDN_DV)
    for h in range(GDN_H):
        sl = slice(h * GDN_DK, (h + 1) * GDN_DK)
        q = qa[:, sl]
        k = ka[:, sl]
        v = va[:, sl]
        q = q * lax.rsqrt(jnp.sum(q * q, axis=-1, keepdims=True) + EPS) * (GDN_DK ** -0.5)
        k = k * lax.rsqrt(jnp.sum(k * k, axis=-1, keepdims=True) + EPS)
        beta = _sigmoid(sm[:, SM_BETA + h:SM_BETA + h + 1])
        g = -jnp.exp(_scalar_vec(alog_ref[h])) * jax.nn.softplus(sm[:, SM_DEC + h:SM_DEC + h + 1] + dtb_ref[h])
        eg = jnp.exp(g)
        qk = jnp.sum(q * k, axis=-1, keepdims=True)
        kb = _bcast_cols(k, e_mat)
        qb = _bcast_cols(q, e_mat)
        blk = lambda m, b: m[:, b * GDN_DV:(b + 1) * GDN_DV]
        ks = jnp.concatenate([jnp.sum(s_ref[b, h] * blk(kb, b), axis=0, keepdims=True) for b in range(nb)], axis=0)
        qs = jnp.concatenate([jnp.sum(s_ref[b, h] * blk(qb, b), axis=0, keepdims=True) for b in range(nb)], axis=0)
        w = beta * v - (beta * eg) * ks
        o = qs * eg + qk * w
        for b in range(nb):
            so_ref[b, h] = s_ref[b, h] * eg[b:b + 1] + blk(kb, b) * w[b:b + 1]
        o = _rms(o, nw_ref[...]) * _silu(z[:, sl])
        o_ref[:, sl] = o.astype(bf16)


def _gdn_step(proj, small, conv_t, states, prev, l, conv_w, a_log, dt_bias, norm_w):
    n = proj.shape[0]
    nb = STEP_B
    blk = lambda off: pl.BlockSpec((nb, GDN_QK), lambda i: (i, _blk(off, GDN_QK)))
    cst = lambda j: pl.BlockSpec((CONV_W - 1, nb, GDN_QK), lambda i: (0, i, j))
    cw = lambda j: pl.BlockSpec((CONV_W, GDN_QK), lambda i: (0, j))
    smem = pl.BlockSpec(memory_space=pltpu.SMEM)
    st = pl.BlockSpec((None, nb, GDN_H, GDN_DK, GDN_DV), lambda i: (l, i, 0, 0, 0))
    return _state_call(
        _gdn_step_kernel, (n // nb,),
        [smem, smem, blk(P_QKV_A), blk(P_QKV_A + GDN_QK), blk(P_QKV_A + 2 * GDN_QK), blk(P_Z_A),
         pl.BlockSpec((nb, LANE), lambda i: (i, 0)),
         cst(0), cst(1), cst(2), cw(0), cw(1), cw(2),
         pl.BlockSpec((1, GDN_DV), lambda i: (0, 0)), st],
        [pl.BlockSpec((nb, GDN_V), lambda i: (i, 0)), st],
        [jax.ShapeDtypeStruct((n, GDN_V), bf16), jax.ShapeDtypeStruct(states.shape, f32)],
        [], ("parallel",), "gdn_step",
        (a_log, dt_bias, proj, proj, proj, proj, small, conv_t, conv_t, conv_t, conv_w, conv_w, conv_w, norm_w, states),
        prev)


def _gla_step_kernel(q_ref, k_ref, v_ref, r_ref, sm_ref, wgate_ref, bgate_ref, nw_ref, s_ref,
                     o_ref, so_ref):
    nb = q_ref.shape[0]
    sm = sm_ref[...]
    lane = lax.broadcasted_iota(jnp.int32, sm.shape, 1)
    lr = jnp.where((lane >= SM_LR) & (lane < SM_LR + GLA_RANK), sm, 0.0).astype(bf16)
    la_all = jax.nn.log_sigmoid(_dot(lr, wgate_ref[...]) + bgate_ref[...]) / GLA_TAU
    qa = q_ref[...] * (GLA_DK ** -0.5)
    ka = k_ref[...]
    va = v_ref[...]
    r = r_ref[...]
    for h in range(GLA_H):
        sk = slice(h * GLA_DK, (h + 1) * GLA_DK)
        sv = slice(h * GLA_DV, (h + 1) * GLA_DV)
        q = qa[:, sk]
        k = ka[:, sk]
        v = va[:, sv]
        e = jnp.exp(la_all[:, sk])
        qk = jnp.sum(q * k, axis=-1, keepdims=True)
        e_t = e.T
        k_t = k.T
        qe_t = (q * e).T
        rows = []
        for b in range(nb):
            s = s_ref[b, h]
            vrow = v[b:b + 1]
            rows.append(qk[b:b + 1] * vrow + jnp.sum(s * qe_t[:, b:b + 1], axis=0, keepdims=True))
            so_ref[b, h] = s * e_t[:, b:b + 1] + k_t[:, b:b + 1] * vrow
        o = jnp.concatenate(rows, axis=0)
        o = _rms(o, nw_ref[...]) * _silu(r[:, sv])
        o_ref[:, sv] = o.astype(bf16)


def _gla_step(proj, small, states, prev, l, wgate_pad, bgate, norm_w):
    n = proj.shape[0]
    nb = STEP_B
    st = pl.BlockSpec((None, nb, GLA_H, GLA_DK, GLA_DV), lambda i: (l, i, 0, 0, 0))
    return _state_call(
        _gla_step_kernel, (n // nb,),
        [pl.BlockSpec((nb, GLA_QK), lambda i: (i, _blk(P_Q_B, GLA_QK))),
         pl.BlockSpec((nb, GLA_QK), lambda i: (i, _blk(P_K_B, GLA_QK))),
         pl.BlockSpec((nb, GLA_V), lambda i: (i, _blk(P_V_B, GLA_V))),
         pl.BlockSpec((nb, GLA_V), lambda i: (i, _blk(P_R_B, GLA_V))),
         pl.BlockSpec((nb, LANE), lambda i: (i, 0)),
         pl.BlockSpec((LANE, GLA_QK), lambda i: (0, 0)),
         pl.BlockSpec((1, GLA_QK), lambda i: (0, 0)),
         pl.BlockSpec((1, GLA_DV), lambda i: (0, 0)), st],
        [pl.BlockSpec((nb, GLA_V), lambda i: (i, 0)), st],
        [jax.ShapeDtypeStruct((n, GLA_V), bf16), jax.ShapeDtypeStruct(states.shape, f32)],
        [], ("parallel",), "gla_step",
        (proj, proj, proj, proj, small, wgate_pad, bgate, norm_w, states),
        prev)


def _ssd_step_kernel(alog_ref, dtb_ref, z_ref, xbc_ref, sm_ref, cst_ref, cw_ref, cb_ref, dvec_ref, nw_ref, h_ref,
                     o_ref, ho_ref):
    nb = xbc_ref.shape[0]
    xbc = _silu(_conv_step(xbc_ref, cst_ref, cw_ref) + cb_ref[...])
    sm_t = sm_ref[...].T
    e_mat = _expand_matrix(nb, SSD_N)
    lane_b = lax.broadcasted_iota(jnp.int32, (SSD_P, nb), 1)
    for g in range(SSD_G):
        gs = slice(g * SSD_GW, (g + 1) * SSD_GW)
        bm = xbc[:, SSD_INNER + g * SSD_N:SSD_INNER + (g + 1) * SSD_N]
        cm_t = xbc[:, SSD_INNER + SSD_BC + g * SSD_N:SSD_INNER + SSD_BC + (g + 1) * SSD_N].T
        cb_row = jnp.sum(cm_t * bm.T, axis=0, keepdims=True)
        hg = h_ref[:, pl.ds(g * SSD_HG, SSD_HG)].reshape(nb * SSD_GW, SSD_N)
        ch = _dot(hg.astype(bf16), cm_t.astype(bf16))
        pairs = []
        for jp in range(SSD_HG // 2):
            x_t = xbc[:, g * SSD_GW + jp * LANE:g * SSD_GW + (jp + 1) * LANE].T
            halves = []
            for jj in range(2):
                j = 2 * jp + jj
                hd = g * SSD_HG + j
                dt_row = jax.nn.softplus(sm_t[SM_DT + hd:SM_DT + hd + 1, :] + dtb_ref[hd])
                ea_row = jnp.exp(dt_row * (-jnp.exp(_scalar_vec(alog_ref[hd]))))
                xdt_t = x_t[jj * SSD_P:(jj + 1) * SSD_P] * dt_row
                hi, lo = _split(xdt_t)
                xb = _dot(hi, e_mat) + _dot(lo, e_mat)
                yh = jnp.zeros((SSD_P, nb), f32)
                for b in range(nb):
                    r0 = (b * SSD_HG + j) * SSD_P
                    yh = jnp.where(lane_b == b, ch[r0:r0 + SSD_P], yh)
                    ho_ref[b, hd] = (h_ref[b, hd] * ea_row[:, b:b + 1]
                                     + xb[:, b * SSD_N:(b + 1) * SSD_N] * bm[b:b + 1])
                halves.append(yh * ea_row + cb_row * xdt_t)
            pairs.append(jnp.concatenate(halves, axis=0).T)
        y = jnp.concatenate(pairs, axis=1)
        y = (y + dvec_ref[:, gs] * xbc[:, gs]) * _silu(z_ref[:, gs])
        o_ref[:, gs] = _rms(y, nw_ref[:, gs]).astype(bf16)


def _ssd_step(proj, small, conv_t, states, prev, l, conv_w, conv_b, a_log, dt_bias, dvec, norm_w):
    n = proj.shape[0]
    nb = STEP_B
    smem = pl.BlockSpec(memory_space=pltpu.SMEM)
    const = lambda shape: pl.BlockSpec(shape, lambda i: (0,) * len(shape))
    st = pl.BlockSpec((None, nb, SSD_H, SSD_P, SSD_N), lambda i: (l, i, 0, 0, 0))
    return _state_call(
        _ssd_step_kernel, (n // nb,),
        [smem, smem,
         pl.BlockSpec((nb, SSD_INNER), lambda i: (i, _blk(P_Z_C, SSD_INNER))),
         pl.BlockSpec((nb, SSD_CONV), lambda i: (i, _blk(P_XBC, SSD_CONV))),
         pl.BlockSpec((nb, LANE), lambda i: (i, 0)),
         pl.BlockSpec((CONV_W - 1, nb, SSD_CONV), lambda i: (0, i, 0)),
         const((CONV_W, SSD_CONV)), const((1, SSD_CONV)), const((1, SSD_INNER)), const((1, SSD_INNER)), st],
        [pl.BlockSpec((nb, SSD_INNER), lambda i: (i, 0)), st],
        [jax.ShapeDtypeStruct((n, SSD_INNER), bf16), jax.ShapeDtypeStruct(states.shape, f32)],
        [], ("parallel",), "ssd_step",
        (a_log, dt_bias, proj, proj, small, conv_t, conv_w, conv_b, dvec, norm_w, states),
        prev)


def _mixout_kernel(oa_ref, ob_ref, oc_ref, ga_ref, gb_ref, gc_ref, wa_ref, wb_ref, wc_ref, wo_ref,
                   x_ref, gt_ref, o_ref, acc_ref):
    j = pl.program_id(2)

    @pl.when(j == 0)
    def _():
        acc_ref[...] = jnp.zeros_like(acc_ref)

    m = (_sigmoid(ga_ref[...].astype(f32)) * _dot(oa_ref[...], wa_ref[...])
         + _sigmoid(gb_ref[...].astype(f32)) * _dot(ob_ref[...], wb_ref[...])
         + _sigmoid(gc_ref[...].astype(f32)) * _dot(oc_ref[...], wc_ref[...]))
    acc_ref[...] += _dot(m.astype(bf16), wo_ref[...])

    @pl.when(j == pl.num_programs(2) - 1)
    def _():
        o_ref[...] = x_ref[...] + gt_ref[...] * acc_ref[...]


def _mixout(oa, ob, oc, proj, x, mod, wa, wb, wc, wo, l, per_row, tm):
    bx, tx, dm = x.shape
    tn = 512
    gate = lambda k: pl.BlockSpec((None, tm, tn), lambda b, i, j: (b, i, _blk(P_GATES + k * D_MODEL, tn) + j))
    act = lambda w: pl.BlockSpec((None, tm, w), lambda b, i, j: (b, i, 0))
    wsp = lambda w: pl.BlockSpec((None, w, tn), lambda b, i, j: (l, 0, j))
    return pl.pallas_call(
        _mixout_kernel,
        grid=(bx, tx // tm, dm // tn),
        in_specs=[act(GDN_V), act(GLA_V), act(SSD_INNER), gate(0), gate(1), gate(2),
                  wsp(GDN_V), wsp(GLA_V), wsp(SSD_INNER),
                  pl.BlockSpec((None, tn, dm), lambda b, i, j: (l, j, 0)),
                  pl.BlockSpec((None, tm, dm), lambda b, i, j: (b, i, 0)),
                  _mod_spec(per_row, tm, 5)],
        out_specs=pl.BlockSpec((None, tm, dm), lambda b, i, j: (b, i, 0)),
        out_shape=jax.ShapeDtypeStruct(x.shape, f32),
        scratch_shapes=[pltpu.VMEM((tm, dm), f32)],
        compiler_params=_cp(("parallel", "parallel", "arbitrary")),
        name="mixout",
    )(oa, ob, oc, proj, proj, proj, wa, wb, wc, wo, x, mod)


REGROUP_ROWS = 512


def _regroup_kernel(w_ref, o_ref):
    o_ref[...] = w_ref[0].astype(bf16)


def _regroup_rows(wt):
    nl, _, d = wt.shape
    tr = REGROUP_ROWS
    shifts = []
    for name in MAIN_ORDER:
        a, b = W_IN_SRC[name]
        assert (b - a) % tr == 0 and a % SUB == 0
        shifts.append((P_MAIN[name] // tr, a - P_MAIN[name]))

    def src_row(i):
        off = i * (tr // SUB) + shifts[0][1] // SUB
        for k in range(1, len(shifts)):
            off = off + jnp.where(i >= shifts[k][0], (shifts[k][1] - shifts[k - 1][1]) // SUB, 0)
        return off * SUB

    return pl.pallas_call(
        _regroup_kernel,
        grid=(nl, P_TOTAL // tr),
        in_specs=[pl.BlockSpec((pl.Element(1), pl.Element(tr), pl.Element(d)), lambda l, i: (l, src_row(i), 0))],
        out_specs=pl.BlockSpec((None, tr, d), lambda l, i: (l, i, 0)),
        out_shape=jax.ShapeDtypeStruct((nl, P_TOTAL, d), bf16),
        compiler_params=_cp(("parallel", "parallel")),
        name="regroup_w_in",
    )(wt)


def _permute_w_in(w):
    nl, d, _ = w.shape
    main = _regroup_rows(jnp.swapaxes(w, 1, 2))
    small = jnp.concatenate([w[:, :, W_IN_SRC[name][0]:W_IN_SRC[name][1]] for name in SMALL_ORDER], axis=2)
    small = jnp.pad(jnp.swapaxes(small, 1, 2), ((0, 0), (0, LANE - small.shape[2]), (0, 0)))
    return main, small


def _ffn_weights(wg, wu, wd):
    f = wg.shape[2]
    cut = f - f % FF_TILE
    w = tuple(a.astype(bf16) for a in (wg, wu, wd))
    if cut == f:
        return w
    return w + (w[0][:, :, cut:], w[1][:, :, cut:], w[2][:, cut:, :])


def _lane_vec(v, lo):
    return jnp.zeros((1, LANE), f32).at[0, lo:lo + v.shape[0]].set(v)


def _layer_params(l, p):
    row = lambda a: a[l].reshape(1, -1)
    wgate = jnp.zeros((LANE, GLA_QK), f32).at[SM_LR:SM_LR + GLA_RANK].set(p["gla_w_gate"][l]).astype(bf16)
    return dict(
        norm1=row(p["norm1"]), norm2=row(p["norm2"]), norm3=row(p["norm3"]),
        gdn_conv_w=p["gdn_conv_w"][l], gdn_a_log=p["gdn_a_log"][l], gdn_dt_bias=p["gdn_dt_bias"][l],
        gdn_alog_l=_lane_vec(p["gdn_a_log"][l], SM_DEC), gdn_dtb_l=_lane_vec(p["gdn_dt_bias"][l], SM_DEC),
        gdn_norm_w=row(p["gdn_norm_w"]),
        gla_wgate=wgate, gla_bgate=row(p["gla_b_gate"]), gla_norm_w=row(p["gla_norm_w"]),
        ssd_conv_w=p["ssd_conv_w"][l], ssd_conv_b=row(p["ssd_conv_b"]), ssd_a_log=p["ssd_a_log"][l],
        ssd_dt_bias=p["ssd_dt_bias"][l],
        ssd_avec_l=_lane_vec(-jnp.exp(p["ssd_a_log"][l]), SM_DT), ssd_dtb_l=_lane_vec(p["ssd_dt_bias"][l], SM_DT),
        ssd_dvec=jnp.repeat(p["ssd_d"][l], SSD_P).reshape(1, -1),
        ssd_norm_w=row(p["ssd_norm_w"]),
    )


def _stacked_weights(p):
    w_in, w_in_small = _permute_w_in(p["w_in"])
    return dict(
        f1=_ffn_weights(p["ffn1_wg"], p["ffn1_wu"], p["ffn1_wd"]),
        f2=_ffn_weights(p["ffn2_wg"], p["ffn2_wu"], p["ffn2_wd"]),
        w_in=w_in, w_in_small=w_in_small,
        wa=p["w_branch_gdn"].astype(bf16), wb=p["w_branch_gla"].astype(bf16),
        wc=p["w_branch_ssd"].astype(bf16), w_out=p["w_out"].astype(bf16),
    )


def _new_conv_state(buf, raw):
    t = raw.shape[1]
    k = CONV_W - 1
    if t >= k:
        return raw[:, t - k:]
    return jnp.concatenate([buf[:, t:], raw], axis=1)


def _mixer_prompt(proj, small, lp, st):
    gdn_conv, s_gdn, s_gla, ssd_conv, s_ssd = st
    b, t, _ = proj.shape
    c = CHUNK
    sm_spec = pl.BlockSpec((None, c, LANE), lambda bi, ti: (bi, ti, 0))
    st_a, st_b, st_c = _per_seq(GDN_H, GDN_DK, GDN_DV), _per_seq(GLA_H, GLA_DK, GLA_DV), _per_seq(SSD_H, SSD_P, SSD_N)
    gdn_specs = [_tok(GDN_CONV, P_QKV_A), _tok(GDN_V, P_Z_A), sm_spec, _per_seq(CONV_W - 1, GDN_CONV),
                 _const(CONV_W, GDN_CONV), _const(1, LANE), _const(1, LANE), _const(1, GDN_DV), st_a]
    gdn_args = (proj, proj, small, gdn_conv, lp["gdn_conv_w"], lp["gdn_alog_l"], lp["gdn_dtb_l"],
                lp["gdn_norm_w"], s_gdn)
    gla_specs = [_tok(GLA_QK, P_Q_B), _tok(GLA_QK, P_K_B), _tok(GLA_V, P_V_B), _tok(GLA_V, P_R_B), sm_spec,
                 _const(LANE, GLA_QK), _const(1, GLA_QK), _const(1, GLA_DV), st_b]
    gla_args = (proj, proj, proj, proj, small, lp["gla_wgate"], lp["gla_bgate"], lp["gla_norm_w"], s_gla)
    ssd_specs = [_tok(SSD_INNER, P_Z_C), _tok(SSD_CONV, P_XBC), sm_spec, _per_seq(CONV_W - 1, SSD_CONV),
                 _const(CONV_W, SSD_CONV), _const(1, SSD_CONV), _const(1, LANE), _const(1, LANE),
                 _const(1, SSD_INNER), _const(1, SSD_INNER), st_c]
    ssd_args = (proj, proj, small, ssd_conv, lp["ssd_conv_w"], lp["ssd_conv_b"], lp["ssd_avec_l"],
                lp["ssd_dtb_l"], lp["ssd_dvec"], lp["ssd_norm_w"], s_ssd)
    assert (len(gdn_specs), len(gla_specs), len(ssd_specs)) == (N_GDN_IN, N_GLA_IN, N_SSD_IN)
    out_tok = lambda width: pl.BlockSpec((None, c, width), lambda bi, ti: (bi, ti, 0))
    oa, s_gdn_n, ob, s_gla_n, oc, s_ssd_n = pl.pallas_call(
        _mixers_kernel,
        grid=(b, t // c),
        in_specs=gdn_specs + gla_specs + ssd_specs,
        out_specs=[out_tok(GDN_V), st_a, out_tok(GLA_V), st_b, out_tok(SSD_INNER), st_c],
        out_shape=[jax.ShapeDtypeStruct((b, t, GDN_V), bf16), jax.ShapeDtypeStruct(s_gdn.shape, f32),
                   jax.ShapeDtypeStruct((b, t, GLA_V), bf16), jax.ShapeDtypeStruct(s_gla.shape, f32),
                   jax.ShapeDtypeStruct((b, t, SSD_INNER), bf16), jax.ShapeDtypeStruct(s_ssd.shape, f32)],
        scratch_shapes=[pltpu.VMEM((GDN_CONV // LANE, 8 + c, LANE), f32), pltpu.VMEM((GLA_H, GLA_DV, GLA_DK), f32),
                        pltpu.VMEM((SSD_CONV // LANE, 8 + c, LANE), f32),
                        pltpu.VMEM((GDN_V // LANE, c, LANE), f32), pltpu.VMEM((SSD_INNER // LANE, c, LANE), f32)],
        compiler_params=_cp(("parallel", "arbitrary")),
        name="mixers_prompt",
    )(*gdn_args, *gla_args, *ssd_args)
    gdn_conv_n = _new_conv_state(gdn_conv, proj[:, :, P_QKV_A:P_QKV_A + GDN_CONV]).astype(f32)
    ssd_conv_n = _new_conv_state(ssd_conv, proj[:, :, P_XBC:P_XBC + SSD_CONV]).astype(f32)
    return (oa, ob, oc), (gdn_conv_n, s_gdn_n, s_gla_n, ssd_conv_n, s_ssd_n)


def _mixer_sample(proj, small, lp, l, states, prev):
    n = proj.shape[1]
    p2 = proj.reshape(n, P_TOTAL)
    s2 = small.reshape(n, LANE)
    gdn_conv, ssd_conv = states[0][l], states[3][l]
    gct = jnp.swapaxes(gdn_conv, 0, 1)
    sct = jnp.swapaxes(ssd_conv, 0, 1)
    pv = (None,) * 5 if prev is None else prev
    oa, s_gdn_n = _gdn_step(p2, s2, gct, states[1], pv[1], l, lp["gdn_conv_w"], lp["gdn_a_log"],
                            lp["gdn_dt_bias"], lp["gdn_norm_w"])
    ob, s_gla_n = _gla_step(p2, s2, states[2], pv[2], l, lp["gla_wgate"], lp["gla_bgate"], lp["gla_norm_w"])
    oc, s_ssd_n = _ssd_step(p2, s2, sct, states[4], pv[4], l, lp["ssd_conv_w"], lp["ssd_conv_b"], lp["ssd_a_log"],
                            lp["ssd_dt_bias"], lp["ssd_dvec"], lp["ssd_norm_w"])
    raw = p2.reshape(n, 1, P_TOTAL)
    gdn_conv_n = _new_conv_state(gdn_conv, raw[:, :, P_QKV_A:P_QKV_A + GDN_CONV])
    ssd_conv_n = _new_conv_state(ssd_conv, raw[:, :, P_XBC:P_XBC + SSD_CONV])
    outs = tuple(o.reshape(1, n, -1) for o in (oa, ob, oc))
    return outs, (gdn_conv_n, s_gdn_n, s_gla_n, ssd_conv_n, s_ssd_n)


def _trunk(x, mods, lps, sw, states, per_row, tm, tm_in, final_w):
    nl = len(lps)
    per_layer = []
    prev = None
    for l in range(nl):
        lp, mod = lps[l], mods[l]
        last = l == nl - 1
        x = _ffn(x, mod, 0, lp["norm1"], sw["f1"], l, lp["norm1"], per_row, tm, False)
        proj, small = _inproj(x, mod, lp["norm2"], sw["w_in"], sw["w_in_small"], l, per_row, tm_in,
                              f32 if per_row else bf16)
        if per_row:
            (oa, ob, oc), st = _mixer_sample(proj, small, lp, l, states, prev)
            prev = st
        else:
            (oa, ob, oc), st = _mixer_prompt(proj, small, lp, tuple(s[l] for s in states))
        per_layer.append(st)
        x = _mixout(oa, ob, oc, proj, x, mod, sw["wa"], sw["wb"], sw["wc"], sw["w_out"], l, per_row, tm)
        x = _ffn(x, mod, 6, lp["norm3"], sw["f2"], l, final_w if last else lp["norm3"], per_row, tm, last)
    stack = lambda i: jnp.stack([st[i] for st in per_layer])
    if per_row:
        new_states = (stack(0), prev[1], prev[2], stack(3), prev[4])
    else:
        new_states = tuple(stack(i) for i in range(5))
    return x, new_states


def kernel(x_prompt, x_sample, state_gdn_conv, state_gdn, state_gla, state_ssd_conv, state_ssd, c_prompt, c_sample, w_ada, b_ada, norm1, norm2, norm3, ffn1_wg, ffn1_wu, ffn1_wd, ffn2_wg, ffn2_wu, ffn2_wd, w_in, gdn_conv_w, gdn_a_log, gdn_dt_bias, gdn_norm_w, gla_w_gate, gla_b_gate, gla_norm_w, ssd_conv_w, ssd_conv_b, ssd_a_log, ssd_dt_bias, ssd_d, ssd_norm_w, w_branch_gdn, w_branch_gla, w_branch_ssd, w_out, final_norm):
    p = dict(norm1=norm1, norm2=norm2, norm3=norm3,
             ffn1_wg=ffn1_wg, ffn1_wu=ffn1_wu, ffn1_wd=ffn1_wd, ffn2_wg=ffn2_wg, ffn2_wu=ffn2_wu, ffn2_wd=ffn2_wd,
             w_in=w_in, gdn_conv_w=gdn_conv_w, gdn_a_log=gdn_a_log, gdn_dt_bias=gdn_dt_bias, gdn_norm_w=gdn_norm_w,
             gla_w_gate=gla_w_gate, gla_b_gate=gla_b_gate, gla_norm_w=gla_norm_w,
             ssd_conv_w=ssd_conv_w, ssd_conv_b=ssd_conv_b, ssd_a_log=ssd_a_log, ssd_dt_bias=ssd_dt_bias,
             ssd_d=ssd_d, ssd_norm_w=ssd_norm_w,
             w_branch_gdn=w_branch_gdn, w_branch_gla=w_branch_gla, w_branch_ssd=w_branch_ssd, w_out=w_out)
    nl = w_ada.shape[0]
    bp, tp, dm = x_prompt.shape
    bs = x_sample.shape[0]
    assert x_sample.shape[1] == 1 and tp % CHUNK == 0 and bs % STEP_B == 0 and dm == D_MODEL
    lps = [_layer_params(l, p) for l in range(nl)]
    sw = _stacked_weights(p)
    fw = final_norm.reshape(1, dm)

    rows = bp + bs
    rpad = -(-rows // 8) * 8
    c_all = jnp.concatenate([c_prompt, c_sample, jnp.zeros((rpad - rows, dm), f32)], axis=0)
    mod = _ada_mod(c_all, w_ada, b_ada)
    mod_p = [mod[l, :bp].reshape(bp, 1, N_MOD * dm) for l in range(nl)]
    mod_s = [mod[l, bp:rows].reshape(1, bs, N_MOD * dm) for l in range(nl)]

    sample_states = (state_gdn_conv, state_gdn, state_gla, state_ssd_conv, state_ssd)
    prompt_states = tuple(jnp.zeros((s.shape[0], bp) + s.shape[2:], x_prompt.dtype) for s in sample_states)
    tm_p = 512 if tp % 512 == 0 else CHUNK
    tm_in = 1024 if tp % 1024 == 0 else tm_p
    y_p, st_p = _trunk(x_prompt, mod_p, lps, sw, prompt_states, False, tm_p, tm_in, fw)
    y_s, st_s = _trunk(x_sample.reshape(1, bs, dm), mod_s, lps, sw, sample_states, True, bs, bs, fw)
    return (y_p, y_s.reshape(bs, 1, dm)) + st_p + st_s
```

```python
import functools

import jax
import jax.numpy as jnp
from jax import lax
from jax.experimental import pallas as pl
from jax.experimental.pallas import tpu as pltpu

f32 = jnp.float32
bf16 = jnp.bfloat16
HI = lax.Precision.HIGHEST

EPS = 1e-6
D_MODEL = 2048
N_MOD = 9
CHUNK = 64
CONV_W = 4
GDN_H, GDN_DK, GDN_DV = 8, 128, 128
GLA_H, GLA_DK, GLA_DV, GLA_RANK, GLA_TAU = 4, 128, 256, 16, 16.0
SSD_H, SSD_P, SSD_G, SSD_N = 32, 64, 4, 128
SSD_HG = SSD_H // SSD_G
GDN_QK = GDN_H * GDN_DK
GDN_V = GDN_H * GDN_DV
GDN_CONV = 2 * GDN_QK + GDN_V
GLA_QK = GLA_H * GLA_DK
GLA_V = GLA_H * GLA_DV
SSD_INNER = SSD_H * SSD_P
SSD_BC = SSD_G * SSD_N
SSD_CONV = SSD_INNER + 2 * SSD_BC
SSD_GW = SSD_HG * SSD_P

LANE = 128
FF_TILE = 512
FF_SUB = 256

IN_SPLITS = (("qkv_a", GDN_CONV), ("z_a", GDN_V), ("beta", GDN_H), ("dec", GDN_H),
             ("q_b", GLA_QK), ("k_b", GLA_QK), ("v_b", GLA_V), ("lr", GLA_RANK), ("r_b", GLA_V),
             ("z_c", SSD_INNER), ("xbc", SSD_CONV), ("dt", SSD_H), ("gates", 3 * D_MODEL))
MAIN_ORDER = ("qkv_a", "xbc", "z_c", "gates", "z_a", "q_b", "k_b", "v_b", "r_b")
SMALL_ORDER = ("beta", "dec", "lr", "dt")


def _layout():
    src, off = {}, 0
    for name, w in IN_SPLITS:
        src[name] = (off, off + w)
        off += w
    main, small, d = {}, {}, 0
    for name in MAIN_ORDER:
        w = src[name][1] - src[name][0]
        main[name] = d
        d += w
    total = d
    d = 0
    for name in SMALL_ORDER:
        small[name] = d
        d += src[name][1] - src[name][0]
    assert d <= LANE
    return src, main, small, total


W_IN_SRC, P_MAIN, P_SM, P_TOTAL = _layout()
P_QKV_A, P_XBC, P_Z_C, P_GATES, P_Z_A = (P_MAIN[k] for k in ("qkv_a", "xbc", "z_c", "gates", "z_a"))
P_Q_B, P_K_B, P_V_B, P_R_B = (P_MAIN[k] for k in ("q_b", "k_b", "v_b", "r_b"))
SM_BETA, SM_DEC, SM_LR, SM_DT = (P_SM[k] for k in SMALL_ORDER)

VMEM_LIMIT = 56 * 1024 * 1024


def _cp(sem):
    return pltpu.CompilerParams(dimension_semantics=sem, vmem_limit_bytes=VMEM_LIMIT)


def _blk(off, width):
    assert off % width == 0, (off, width)
    return off // width


def _sigmoid(x):
    return 0.5 + 0.5 * jnp.tanh(0.5 * x)


def _silu(x):
    h = 0.5 * x
    return h + h * jnp.tanh(h)


def _rms(x, w):
    return x * lax.rsqrt(jnp.mean(x * x, axis=-1, keepdims=True) + EPS) * w


def _dot(a, b):
    return jnp.dot(a, b, preferred_element_type=f32)


def _dot_nt(a, b):
    return lax.dot_general(a, b, (((1,), (1,)), ((), ())), preferred_element_type=f32)


def _dot_tn(a, b):
    return lax.dot_general(a, b, (((0,), (0,)), ((), ())), preferred_element_type=f32)


def _split(x):
    hi = x.astype(bf16)
    return hi, (x - hi.astype(f32)).astype(bf16)


def _dot3(a, b):
    return _dot(a[0], b[0]) + (_dot(a[0], b[1]) + _dot(a[1], b[0]))


def _dot_hi(a, b):
    return jnp.dot(a, b, precision=HI, preferred_element_type=f32)


def _tri_masks(c):
    row = lax.broadcasted_iota(jnp.int32, (c, c), 0)
    col = lax.broadcasted_iota(jnp.int32, (c, c), 1)
    return row >= col, row > col, row == col


SUB = 8


def _row_time(idx, c):
    n = c // SUB
    assert n & (n - 1) == 0, "chunk / SUB must be a power of two"
    sh = n.bit_length() - 1
    return ((idx & (n - 1)) << 3) | (idx >> sh)


def _tri_masks_residue(c, cols=None):
    cols = c if cols is None else cols
    row = _row_time(lax.broadcasted_iota(jnp.int32, (c, cols), 0), c)
    col = _row_time(lax.broadcasted_iota(jnp.int32, (c, cols), 1) & (c - 1), c)
    return row >= col, row > col, row == col


def _load_residue(ref, c, sl=slice(None)):
    n = c // SUB
    return jnp.concatenate([ref[pl.ds(r, n, stride=SUB), sl] for r in range(SUB)], axis=0)


def _store_natural(nat_ref, x, c, lo):
    n = c // SUB
    for j in range(x.shape[1] // LANE):
        for r in range(SUB):
            nat_ref[lo // LANE + j, pl.ds(r, n, stride=SUB), :] = x[r * n:(r + 1) * n, j * LANE:(j + 1) * LANE]


def _read_natural(nat_ref, lo, width):
    return jnp.concatenate([nat_ref[lo // LANE + j] for j in range(width // LANE)], axis=1)


def _lane_col(x, idx):
    lane = lax.broadcasted_iota(jnp.int32, x.shape, 1)
    return jnp.sum(jnp.where(lane == idx, x, 0.0), axis=1, keepdims=True)


def _scalar_vec(s):
    return jnp.full((1, 1), s, f32)


def _ada_kernel(c_ref, w_ref, b_ref, o_ref):
    s = _silu(c_ref[...]).astype(bf16)
    o_ref[...] = _dot(s, w_ref[...].astype(bf16)) + b_ref[...]


def _ada_mod(c_all, w_ada, b_ada):
    nl, dm, n = w_ada.shape
    r = c_all.shape[0]
    tn = 1024
    return pl.pallas_call(
        _ada_kernel,
        grid=(nl, n // tn),
        in_specs=[pl.BlockSpec((r, dm), lambda l, j: (0, 0)),
                  pl.BlockSpec((None, dm, tn), lambda l, j: (l, 0, j)),
                  pl.BlockSpec((None, 1, tn), lambda l, j: (l, 0, j))],
        out_specs=pl.BlockSpec((None, r, tn), lambda l, j: (l, 0, j)),
        out_shape=jax.ShapeDtypeStruct((nl, r, n), f32),
        compiler_params=_cp(("arbitrary", "arbitrary")),
        name="ada_mod",
    )(c_all, w_ada, b_ada.reshape(nl, 1, n))


def _mod_spec(per_row, tm, chunk):
    if per_row:
        return pl.BlockSpec((None, tm, D_MODEL), lambda b, i, j: (b, i, chunk))
    return pl.BlockSpec((None, 1, D_MODEL), lambda b, i, j: (b, 0, chunk))


def _ffn_kernel(x_ref, sh_ref, sc_ref, gt_ref, nw_ref, fw_ref, wg_ref, wu_ref, wd_ref, *rest, final, has_tail):
    if has_tail:
        wgt_ref, wut_ref, wdt_ref, o_ref, h_ref, acc_ref = rest
    else:
        o_ref, h_ref, acc_ref = rest
    f = pl.program_id(2)
    last = pl.num_programs(2) - 1

    @pl.when(f == 0)
    def _():
        y = _rms(x_ref[...], nw_ref[...])
        h_ref[...] = (y * (1.0 + sc_ref[...]) + sh_ref[...]).astype(bf16)
        acc_ref[...] = jnp.zeros_like(acc_ref)

    def accumulate(wg, wu, wd):
        h = h_ref[...]
        width = wg.shape[1]
        sub = FF_SUB if width % FF_SUB == 0 else width
        parts = [(_dot(h, wg[:, s:s + sub]), _dot(h, wu[:, s:s + sub])) for s in range(0, width, sub)]
        acc = acc_ref[...]
        for i, (g, u) in enumerate(parts):
            acc = acc + _dot((_silu(g) * u).astype(bf16), wd[i * sub:(i + 1) * sub, :])
        acc_ref[...] = acc

    if has_tail:
        pl.when(f < last)(lambda: accumulate(wg_ref, wu_ref, wd_ref))
        pl.when(f == last)(lambda: accumulate(wgt_ref, wut_ref, wdt_ref))
    else:
        accumulate(wg_ref, wu_ref, wd_ref)

    @pl.when(f == last)
    def _():
        y = x_ref[...] + 0.5 * gt_ref[...] * acc_ref[...]
        if final:
            y = _rms(y, fw_ref[...])
        o_ref[...] = y


def _ffn(x, mod, k0, nw, w, l, fw, per_row, tm, final):
    bx, tx, dm = x.shape
    tf = FF_TILE
    n_full = w[0].shape[2] // tf
    has_tail = len(w) > 3
    full = lambda f: jnp.minimum(f, n_full - 1)
    w_specs = [pl.BlockSpec((None, dm, tf), lambda b, i, f: (l, 0, full(f))),
               pl.BlockSpec((None, dm, tf), lambda b, i, f: (l, 0, full(f))),
               pl.BlockSpec((None, tf, dm), lambda b, i, f: (l, full(f), 0))]
    if has_tail:
        ft = w[3].shape[2]
        w_specs += [pl.BlockSpec((None, dm, ft), lambda b, i, f: (l, 0, 0)),
                    pl.BlockSpec((None, dm, ft), lambda b, i, f: (l, 0, 0)),
                    pl.BlockSpec((None, ft, dm), lambda b, i, f: (l, 0, 0))]
    return pl.pallas_call(
        functools.partial(_ffn_kernel, final=final, has_tail=has_tail),
        grid=(bx, tx // tm, n_full + int(has_tail)),
        in_specs=[pl.BlockSpec((None, tm, dm), lambda b, i, f: (b, i, 0)),
                  _mod_spec(per_row, tm, k0), _mod_spec(per_row, tm, k0 + 1), _mod_spec(per_row, tm, k0 + 2),
                  pl.BlockSpec((1, dm), lambda b, i, f: (0, 0)),
                  pl.BlockSpec((1, dm), lambda b, i, f: (0, 0))] + w_specs,
        out_specs=pl.BlockSpec((None, tm, dm), lambda b, i, f: (b, i, 0)),
        out_shape=jax.ShapeDtypeStruct(x.shape, f32),
        scratch_shapes=[pltpu.VMEM((tm, dm), bf16), pltpu.VMEM((tm, dm), f32)],
        compiler_params=_cp(("parallel", "parallel", "arbitrary")),
        name="ffn",
    )(x, mod, mod, mod, nw, fw, *w)


def _inproj_kernel(x_ref, sh_ref, sc_ref, nw_ref, w_ref, ws_ref, o_ref, os_ref, h_ref):
    @pl.when(pl.program_id(2) == 0)
    def _():
        y = _rms(x_ref[...], nw_ref[...])
        h = (y * (1.0 + sc_ref[...]) + sh_ref[...]).astype(bf16)
        h_ref[...] = h
        os_ref[...] = _dot_nt(h, ws_ref[...].astype(bf16))

    o_ref[...] = _dot_nt(h_ref[...], w_ref[...]).astype(o_ref.dtype)


def _inproj(x, mod, nw, w, ws, l, per_row, tm, out_dtype):
    bx, tx, dm = x.shape
    n = w.shape[1]
    tn = 1024
    return pl.pallas_call(
        _inproj_kernel,
        grid=(bx, tx // tm, n // tn),
        in_specs=[pl.BlockSpec((None, tm, dm), lambda b, i, j: (b, i, 0)),
                  _mod_spec(per_row, tm, 3), _mod_spec(per_row, tm, 4),
                  pl.BlockSpec((1, dm), lambda b, i, j: (0, 0)),
                  pl.BlockSpec((None, tn, dm), lambda b, i, j: (l, j, 0)),
                  pl.BlockSpec((None, LANE, dm), lambda b, i, j: (l, 0, 0))],
        out_specs=[pl.BlockSpec((None, tm, tn), lambda b, i, j: (b, i, j)),
                   pl.BlockSpec((None, tm, LANE), lambda b, i, j: (b, i, 0))],
        out_shape=[jax.ShapeDtypeStruct((bx, tx, n), out_dtype),
                   jax.ShapeDtypeStruct((bx, tx, LANE), f32)],
        scratch_shapes=[pltpu.VMEM((tm, dm), bf16)],
        compiler_params=_cp(("parallel", "parallel", "arbitrary")),
        name="inproj",
    )(x, mod, mod, nw, w, ws)


def _conv_load(ext_ref, x_ref, cst_ref, first):
    c = x_ref.shape[0]
    nblk = ext_ref.shape[0]

    @pl.when(first)
    def _():
        for j in range(nblk):
            ext_ref[j, pl.ds(5, CONV_W - 1), :] = cst_ref[:, j * LANE:(j + 1) * LANE]

    for j in range(nblk):
        ext_ref[j, pl.ds(8, c), :] = x_ref[:, j * LANE:(j + 1) * LANE].astype(f32)


def _conv_cols(ext_ref, w_ref, b_ref, lo, width, c):
    n = c // SUB
    cols = []
    for blk in range(lo // LANE, (lo + width) // LANE):
        sl = pl.ds(blk * LANE, LANE)
        w = w_ref[:, sl]
        taps = [ext_ref[blk, pl.ds(8 - (CONV_W - 1) + k, n, stride=SUB), :] for k in range(SUB + CONV_W - 1)]
        ys = []
        for r in range(SUB):
            y = taps[r] * w[0:1]
            for j in range(1, CONV_W):
                y = y + taps[r + j] * w[j:j + 1]
            ys.append(y)
        y = jnp.concatenate(ys, axis=0)
        if b_ref is not None:
            y = y + b_ref[:, sl]
        cols.append(_silu(y))
    return cols[0] if len(cols) == 1 else jnp.concatenate(cols, axis=1)


def _conv_carry(ext_ref, c):
    for j in range(ext_ref.shape[0]):
        ext_ref[j, pl.ds(5, CONV_W - 1), :] = ext_ref[j, pl.ds(8 + c - (CONV_W - 1), CONV_W - 1), :]


def _cumsum_rows(x, tri):
    return _dot_hi(jnp.where(tri, 1.0, 0.0).astype(f32), x)


def _gdn_kernel(qkv_ref, z_ref, sm_ref, cst_ref, cw_ref, alog_ref, dtb_ref, nw_ref, s0_ref,
                o_ref, s_ref, ext_ref, nat_ref):
    c = qkv_ref.shape[0]
    first = pl.program_id(1) == 0

    @pl.when(first)
    def _():
        s_ref[...] = s0_ref[...]

    _conv_load(ext_ref, qkv_ref, cst_ref, first)

    sm = _load_residue(sm_ref, c)
    beta_l = _sigmoid(sm)
    g_l = -jnp.exp(alog_ref[...]) * jax.nn.softplus(sm + dtb_ref[...])
    tri, strict, eye = _tri_masks_residue(c)
    gc_all = _cumsum_rows(g_l, tri)
    gc_t = gc_all.T
    eye_f = jnp.where(eye, 1.0, 0.0).astype(f32)
    n_fac = max(1, (c - 1).bit_length() - 1)

    hs = range(GDN_H)
    qb, kb, ks, decay, eg, gcs, nm, rhs = [], [], [], [], [], [], [], []
    for h in hs:
        q = _conv_cols(ext_ref, cw_ref, None, h * GDN_DK, GDN_DK, c)
        k = _conv_cols(ext_ref, cw_ref, None, GDN_QK + h * GDN_DK, GDN_DK, c)
        v = _conv_cols(ext_ref, cw_ref, None, 2 * GDN_QK + h * GDN_DV, GDN_DV, c)
        q = q * lax.rsqrt(jnp.sum(q * q, axis=-1, keepdims=True) + EPS) * (GDN_DK ** -0.5)
        k = k * lax.rsqrt(jnp.sum(k * k, axis=-1, keepdims=True) + EPS)
        beta = beta_l[:, SM_BETA + h:SM_BETA + h + 1]
        gc = gc_all[:, SM_DEC + h:SM_DEC + h + 1]
        d = jnp.exp(jnp.where(tri, gc - gc_t[SM_DEC + h:SM_DEC + h + 1, :], -jnp.inf))
        e = jnp.exp(gc)
        qb.append(q.astype(bf16))
        kb.append(k.astype(bf16))
        ks.append(k)
        decay.append(d)
        eg.append(e)
        gcs.append(gc)
        nm.append(jnp.where(strict, beta * d * _dot_nt(kb[h], kb[h]), 0.0))
        rhs.append(jnp.concatenate([v * beta, k * (beta * e)], axis=1))
        yield

    inv = [eye_f - nm[h] for h in hs]
    pw_s = [_split(nm[h]) for h in hs]
    pw_s = [_split(_dot3(pw_s[h], pw_s[h])) for h in hs]
    yield
    for i in range(n_fac):
        inv = [inv[h] + _dot3(_split(inv[h]), pw_s[h]) for h in hs]
        yield
        if i + 1 < n_fac:
            pw_s = [_split(_dot3(pw_s[h], pw_s[h])) for h in hs]
            yield
    sol = []
    for h in hs:
        inv_hi, inv_lo = _split(inv[h])
        rb = rhs[h].astype(bf16)
        sol.append(_dot(inv_hi, rb) + _dot(inv_lo, rb))
    yield

    s = [s_ref[h] for h in hs]
    sb = [s[h].astype(bf16) for h in hs]
    wb = [(sol[h][:, 0:GDN_DV] - _dot(sol[h][:, GDN_DV:].astype(bf16), sb[h])).astype(bf16) for h in hs]
    yield
    qk = [(_dot_nt(qb[h], kb[h]) * decay[h]).astype(bf16) for h in hs]
    yield
    o = [_dot(qb[h], sb[h]) * eg[h] + _dot(qk[h], wb[h]) for h in hs]
    yield
    for h in hs:
        g_last = gcs[h][c - 1:c]
        kd = (ks[h] * jnp.exp(g_last - gcs[h])).astype(bf16)
        s_ref[h] = s[h] * jnp.exp(g_last) + _dot_tn(kd, wb[h])
    yield
    for h in hs:
        sl = slice(h * GDN_DV, (h + 1) * GDN_DV)
        _store_natural(nat_ref, _rms(o[h], nw_ref[...]), c, h * GDN_DV)
        o_ref[:, sl] = (_read_natural(nat_ref, h * GDN_DV, GDN_DV) * _silu(z_ref[:, sl].astype(f32))).astype(bf16)
        yield

    _conv_carry(ext_ref, c)


def _tok(width, off):
    return pl.BlockSpec((None, CHUNK, width), lambda bi, ti: (bi, ti, _blk(off, width)))


def _per_seq(*shape):
    return pl.BlockSpec((None,) + shape, lambda bi, ti: (bi,) + (0,) * len(shape))


def _const(*shape):
    return pl.BlockSpec(shape, lambda bi, ti: (0,) * len(shape))


N_GDN_IN, N_GLA_IN, N_SSD_IN = 9, 9, 11
_DONE = object()


def _mixers_kernel(*refs):
    i0, i1, i2 = N_GDN_IN, N_GDN_IN + N_GLA_IN, N_GDN_IN + N_GLA_IN + N_SSD_IN
    gdn_in, gla_in, ssd_in = refs[:i0], refs[i0:i1], refs[i1:i2]
    oa_ref, sa_ref, ob_ref, sb_ref, oc_ref, sc_ref, ext_a_ref, st_b_ref, ext_c_ref, nat_a_ref, nat_c_ref = refs[i2:]
    live = [_gdn_kernel(*gdn_in, oa_ref, sa_ref, ext_a_ref, nat_a_ref),
            _gla_kernel(*gla_in, ob_ref, sb_ref, st_b_ref),
            _ssd_kernel(*ssd_in, oc_ref, sc_ref, ext_c_ref, nat_c_ref)]
    while live:
        for gen in list(live):
            if next(gen, _DONE) is _DONE:
                live.remove(gen)


GLA_SUB = 16


def _gla_kernel(q_ref, k_ref, v_ref, r_ref, sm_ref, wgate_ref, bgate_ref, nw_ref, s0_ref,
                o_ref, so_ref, st_ref):
    c = q_ref.shape[0]
    ti = pl.program_id(1)

    @pl.when(ti == 0)
    def _():
        for h in range(GLA_H):
            st_ref[h] = s0_ref[h].T

    sm = sm_ref[...]
    lane = lax.broadcasted_iota(jnp.int32, sm.shape, 1)
    lr = jnp.where((lane >= SM_LR) & (lane < SM_LR + GLA_RANK), sm, 0.0).astype(bf16)
    la = jax.nn.log_sigmoid(_dot(lr, wgate_ref[...]) + bgate_ref[...]) / GLA_TAU
    tri, _, _ = _tri_masks(c)
    b_all = _cumsum_rows(la, tri)

    col = lax.broadcasted_iota(jnp.int32, (GLA_SUB, c), 1)
    row = lax.broadcasted_iota(jnp.int32, (GLA_SUB, c), 0)
    hs = range(GLA_H)
    qs, ks, vbs, bs, atts = [], [], [], [], []
    for h in hs:
        sk = slice(h * GLA_DK, (h + 1) * GLA_DK)
        q = q_ref[:, sk].astype(f32) * (GLA_DK ** -0.5)
        k = k_ref[:, sk].astype(f32)
        b = b_all[:, sk]
        qs.append(q)
        ks.append(k)
        bs.append(b)
        vbs.append(v_ref[:, h * GLA_DV:(h + 1) * GLA_DV].astype(bf16))
        att_rows = []
        for i in range(c // GLA_SUB):
            lo = i * GLA_SUB
            b_i = b[lo:lo + GLA_SUB]
            q_i = q[lo:lo + GLA_SUB]
            b_top = b[lo:lo + 1]
            att = jnp.zeros((GLA_SUB, c), f32)
            if i > 0:
                q_t = (q_i * jnp.exp(b_i - b_top)).astype(bf16)
                k_t = (k * jnp.exp(jnp.minimum(b_top - b, 0.0))).astype(bf16)
                att = jnp.where(col < lo, _dot_nt(q_t, k_t), 0.0)
            for sl in range(GLA_SUB):
                sidx = lo + sl
                e = jnp.exp(jnp.minimum(b_i - b[sidx:sidx + 1], 0.0))
                p = jnp.sum(q_i * k[sidx:sidx + 1] * e, axis=1, keepdims=True)
                att = jnp.where((col == sidx) & (row >= sl), p, att)
            att_rows.append(att)
            yield
        atts.append(jnp.concatenate(att_rows, axis=0).astype(bf16))

    sts = [st_ref[h] for h in hs]
    os_ = [_dot(atts[h], vbs[h]) + _dot_nt((qs[h] * jnp.exp(bs[h])).astype(bf16), sts[h].astype(bf16))
           for h in hs]
    yield
    for h in hs:
        b_last = bs[h][c - 1:c]
        kd = (ks[h] * jnp.exp(b_last - bs[h])).astype(bf16)
        st_ref[h] = sts[h] * jnp.exp(b_last) + _dot_tn(vbs[h], kd)
        yield
    for h in hs:
        sv = slice(h * GLA_DV, (h + 1) * GLA_DV)
        o_ref[:, sv] = (_rms(os_[h], nw_ref[...]) * _silu(r_ref[:, sv].astype(f32))).astype(bf16)
        yield

    @pl.when(ti == pl.num_programs(1) - 1)
    def _():
        for h in hs:
            so_ref[h] = st_ref[h].T


def _ssd_kernel(z_ref, xbc_ref, sm_ref, cst_ref, cw_ref, cb_ref, avec_ref, dtb_ref, dvec_ref, nw_ref, h0_ref,
                o_ref, h_ref, ext_ref, nat_ref):
    c = xbc_ref.shape[0]
    assert c == SSD_P, "head pairs share a lane tile: the chunk must be as wide as a head"
    first = pl.program_id(1) == 0

    @pl.when(first)
    def _():
        h_ref[...] = h0_ref[...]

    _conv_load(ext_ref, xbc_ref, cst_ref, first)

    dt_l = jax.nn.softplus(_load_residue(sm_ref, c) + dtb_ref[...])
    tri, _, _ = _tri_masks_residue(c)
    ac_all = _cumsum_rows(dt_l * avec_ref[...], tri)
    ac_t2 = jnp.concatenate([ac_all, ac_all], axis=0).T

    lane = lax.broadcasted_iota(jnp.int32, (c, 2 * SSD_P), 1)
    lo_half = lane < SSD_P
    tri2, _, _ = _tri_masks_residue(c, 2 * SSD_P)
    lane1 = lane[0:1]
    gs_ = range(SSD_G)
    ps_ = range(SSD_HG // 2)

    xs_, bmb, hg, cb2, ch = [], [], [], [], []
    for g in gs_:
        xs_.append(_conv_cols(ext_ref, cw_ref, cb_ref, g * SSD_GW, SSD_GW, c))
        bmb.append(_conv_cols(ext_ref, cw_ref, cb_ref, SSD_INNER + g * SSD_N, SSD_N, c).astype(bf16))
        cmb = _conv_cols(ext_ref, cw_ref, cb_ref, SSD_INNER + SSD_BC + g * SSD_N, SSD_N, c).astype(bf16)
        hg.append(h_ref[pl.ds(g * SSD_HG, SSD_HG)].reshape(SSD_GW, SSD_N))
        cb2.append(_dot_nt(cmb, jnp.concatenate([bmb[g], bmb[g]], axis=0)))
        ch.append(_dot_nt(cmb, hg[g].astype(bf16)))
        yield

    m2, rhs, e_col, xsc = {}, {}, {}, {}
    for g in gs_:
        for p in ps_:
            l0 = SM_DT + g * SSD_HG + 2 * p
            ps = slice(p * 2 * SSD_P, (p + 1) * 2 * SSD_P)
            ac_col = jnp.where(lo_half, ac_all[:, l0:l0 + 1], ac_all[:, l0 + 1:l0 + 2])
            ac_row = jnp.where(lane1 < SSD_P, ac_t2[l0:l0 + 1, :], ac_t2[l0 + 1:l0 + 2, :])
            decay = jnp.exp(jnp.where(tri2, ac_col - ac_row, -jnp.inf))
            m2[g, p] = (cb2[g] * decay).astype(bf16)
            dt2 = jnp.where(lo_half, dt_l[:, l0:l0 + 1], dt_l[:, l0 + 1:l0 + 2])
            xdt = xs_[g][:, ps] * dt2
            rhs[g, p] = jnp.concatenate([jnp.where(lo_half, xdt, 0.0), jnp.where(lo_half, 0.0, xdt)],
                                        axis=0).astype(bf16)
            e_col[g, p] = jnp.exp(ac_col)
            xsc[g, p] = xdt * jnp.exp(ac_col[c - 1:c] - ac_col)
            yield
    y2 = {gp: _dot(m2[gp], rhs[gp]) for gp in m2}
    yield

    for g in gs_:
        xsg = jnp.concatenate([xsc[g, p] for p in ps_], axis=1).astype(bf16)
        dh = _dot_tn(xsg, bmb[g])
        for j in range(SSD_HG):
            ln = SM_DT + g * SSD_HG + j
            h_ref[g * SSD_HG + j] = (hg[g][j * SSD_P:(j + 1) * SSD_P] * jnp.exp(ac_all[c - 1:c, ln:ln + 1])
                                     + dh[j * SSD_P:(j + 1) * SSD_P])
        yield
    for g in gs_:
        gs = slice(g * SSD_GW, (g + 1) * SSD_GW)
        y = jnp.concatenate([y2[g, p] + ch[g][:, p * 2 * SSD_P:(p + 1) * 2 * SSD_P] * e_col[g, p] for p in ps_],
                            axis=1)
        _store_natural(nat_ref, y + dvec_ref[:, gs] * xs_[g], c, g * SSD_GW)
        y = _read_natural(nat_ref, g * SSD_GW, SSD_GW) * _silu(z_ref[:, gs].astype(f32))
        o_ref[:, gs] = _rms(y, nw_ref[:, gs]).astype(bf16)
        yield

    _conv_carry(ext_ref, c)


STEP_B = 8


def _conv_step(x_ref, c_ref, w_ref):
    w = w_ref[...]
    return c_ref[0] * w[0:1] + c_ref[1] * w[1:2] + c_ref[2] * w[2:3] + x_ref[...] * w[3:4]


def _expand_matrix(nb, width):
    r = lax.broadcasted_iota(jnp.int32, (nb, nb * width), 0)
    c = lax.broadcasted_iota(jnp.int32, (nb, nb * width), 1)
    return jnp.where((c >= r * width) & (c < (r + 1) * width), 1.0, 0.0).astype(bf16)


def _bcast_cols(x, e_mat):
    hi, lo = _split(x)
    return _dot_tn(hi, e_mat) + _dot_tn(lo, e_mat)


def _state_call(kernel, grid, in_specs, out_specs, out_shape, scratch, sem, name, args, prev_state):
    aliases = {}
    if prev_state is not None:
        in_specs = in_specs + [pl.BlockSpec(memory_space=pl.ANY)]
        args = args + (prev_state,)
        aliases = {len(args) - 1: len(out_shape) - 1}
        kernel = functools.partial(_drop_alias_ref, kernel, len(args) - 1)
    return pl.pallas_call(kernel, grid=grid, in_specs=in_specs, out_specs=out_specs, out_shape=out_shape,
                          scratch_shapes=scratch, input_output_aliases=aliases,
                          compiler_params=_cp(sem), name=name)(*args)


def _drop_alias_ref(kernel, idx, *refs):
    return kernel(*refs[:idx], *refs[idx + 1:])


def _gdn_step_kernel(alog_ref, dtb_ref, q_ref, k_ref, v_ref, z_ref, sm_ref, cq_ref, ck_ref, cv_ref,
                     wq_ref, wk_ref, wv_ref, nw_ref, s_ref, o_ref, so_ref):
    nb = q_ref.shape[0]
    qa = _silu(_conv_step(q_ref, cq_ref, wq_ref))
    ka = _silu(_conv_step(k_ref, ck_ref, wk_ref))
    va = _silu(_conv_step(v_ref, cv_ref, wv_ref))
    sm = sm_ref[...]
    z = z_ref[...]
    e_mat = _expand_matrix(nb, GDN_DV)
    for h in range(GDN_H):
        sl = slice(h * GDN_DK, (h + 1) * GDN_DK)
        q = qa[:, sl]
        k = ka[:, sl]
        v = va[:, sl]
        q = q * lax.rsqrt(jnp.sum(q * q, axis=-1, keepdims=True) + EPS) * (GDN_DK ** -0.5)
        k = k * lax.rsqrt(jnp.sum(k * k, axis=-1, keepdims=True) + EPS)
        beta = _sigmoid(sm[:, SM_BETA + h:SM_BETA + h + 1])
        g = -jnp.exp(_scalar_vec(alog_ref[h])) * jax.nn.softplus(sm[:, SM_DEC + h:SM_DEC + h + 1] + dtb_ref[h])
        eg = jnp.exp(g)
        qk = jnp.sum(q * k, axis=-1, keepdims=True)
        kb = _bcast_cols(k, e_mat)
        qb = _bcast_cols(q, e_mat)
        blk = lambda m, b: m[:, b * GDN_DV:(b + 1) * GDN_DV]
        ks = jnp.concatenate([jnp.sum(s_ref[b, h] * blk(kb, b), axis=0, keepdims=True) for b in range(nb)], axis=0)
        qs = jnp.concatenate([jnp.sum(s_ref[b, h] * blk(qb, b), axis=0, keepdims=True) for b in range(nb)], axis=0)
        w = beta * v - (beta * eg) * ks
        o = qs * eg + qk * w
        for b in range(nb):
            so_ref[b, h] = s_ref[b, h] * eg[b:b + 1] + blk(kb, b) * w[b:b + 1]
        o = _rms(o, nw_ref[...]) * _silu(z[:, sl])
        o_ref[:, sl] = o.astype(bf16)


def _gdn_step(proj, small, conv_t, states, prev, l, conv_w, a_log, dt_bias, norm_w):
    n = proj.shape[0]
    nb = STEP_B
    blk = lambda off: pl.BlockSpec((nb, GDN_QK), lambda i: (i, _blk(off, GDN_QK)))
    cst = lambda j: pl.BlockSpec((CONV_W - 1, nb, GDN_QK), lambda i: (0, i, j))
    cw = lambda j: pl.BlockSpec((CONV_W, GDN_QK), lambda i: (0, j))
    smem = pl.BlockSpec(memory_space=pltpu.SMEM)
    st = pl.BlockSpec((None, nb, GDN_H, GDN_DK, GDN_DV), lambda i: (l, i, 0, 0, 0))
    return _state_call(
        _gdn_step_kernel, (n // nb,),
        [smem, smem, blk(P_QKV_A), blk(P_QKV_A + GDN_QK), blk(P_QKV_A + 2 * GDN_QK), blk(P_Z_A),
         pl.BlockSpec((nb, LANE), lambda i: (i, 0)),
         cst(0), cst(1), cst(2), cw(0), cw(1), cw(2),
         pl.BlockSpec((1, GDN_DV), lambda i: (0, 0)), st],
        [pl.BlockSpec((nb, GDN_V), lambda i: (i, 0)), st],
        [jax.ShapeDtypeStruct((n, GDN_V), bf16), jax.ShapeDtypeStruct(states.shape, f32)],
        [], ("parallel",), "gdn_step",
        (a_log, dt_bias, proj, proj, proj, proj, small, conv_t, conv_t, conv_t, conv_w, conv_w, conv_w, norm_w, states),
        prev)


def _gla_step_kernel(q_ref, k_ref, v_ref, r_ref, sm_ref, wgate_ref, bgate_ref, nw_ref, s_ref,
                     o_ref, so_ref):
    nb = q_ref.shape[0]
    sm = sm_ref[...]
    lane = lax.broadcasted_iota(jnp.int32, sm.shape, 1)
    lr = jnp.where((lane >= SM_LR) & (lane < SM_LR + GLA_RANK), sm, 0.0).astype(bf16)
    la_all = jax.nn.log_sigmoid(_dot(lr, wgate_ref[...]) + bgate_ref[...]) / GLA_TAU
    qa = q_ref[...] * (GLA_DK ** -0.5)
    ka = k_ref[...]
    va = v_ref[...]
    r = r_ref[...]
    for h in range(GLA_H):
        sk = slice(h * GLA_DK, (h + 1) * GLA_DK)
        sv = slice(h * GLA_DV, (h + 1) * GLA_DV)
        q = qa[:, sk]
        k = ka[:, sk]
        v = va[:, sv]
        e = jnp.exp(la_all[:, sk])
        qk = jnp.sum(q * k, axis=-1, keepdims=True)
        e_t = e.T
        k_t = k.T
        qe_t = (q * e).T
        rows = []
        for b in range(nb):
            s = s_ref[b, h]
            vrow = v[b:b + 1]
            rows.append(qk[b:b + 1] * vrow + jnp.sum(s * qe_t[:, b:b + 1], axis=0, keepdims=True))
            so_ref[b, h] = s * e_t[:, b:b + 1] + k_t[:, b:b + 1] * vrow
        o = jnp.concatenate(rows, axis=0)
        o = _rms(o, nw_ref[...]) * _silu(r[:, sv])
        o_ref[:, sv] = o.astype(bf16)


def _gla_step(proj, small, states, prev, l, wgate_pad, bgate, norm_w):
    n = proj.shape[0]
    nb = STEP_B
    st = pl.BlockSpec((None, nb, GLA_H, GLA_DK, GLA_DV), lambda i: (l, i, 0, 0, 0))
    return _state_call(
        _gla_step_kernel, (n // nb,),
        [pl.BlockSpec((nb, GLA_QK), lambda i: (i, _blk(P_Q_B, GLA_QK))),
         pl.BlockSpec((nb, GLA_QK), lambda i: (i, _blk(P_K_B, GLA_QK))),
         pl.BlockSpec((nb, GLA_V), lambda i: (i, _blk(P_V_B, GLA_V))),
         pl.BlockSpec((nb, GLA_V), lambda i: (i, _blk(P_R_B, GLA_V))),
         pl.BlockSpec((nb, LANE), lambda i: (i, 0)),
         pl.BlockSpec((LANE, GLA_QK), lambda i: (0, 0)),
         pl.BlockSpec((1, GLA_QK), lambda i: (0, 0)),
         pl.BlockSpec((1, GLA_DV), lambda i: (0, 0)), st],
        [pl.BlockSpec((nb, GLA_V), lambda i: (i, 0)), st],
        [jax.ShapeDtypeStruct((n, GLA_V), bf16), jax.ShapeDtypeStruct(states.shape, f32)],
        [], ("parallel",), "gla_step",
        (proj, proj, proj, proj, small, wgate_pad, bgate, norm_w, states),
        prev)


def _ssd_step_kernel(alog_ref, dtb_ref, z_ref, xbc_ref, sm_ref, cst_ref, cw_ref, cb_ref, dvec_ref, nw_ref, h_ref,
                     o_ref, ho_ref):
    nb = xbc_ref.shape[0]
    xbc = _silu(_conv_step(xbc_ref, cst_ref, cw_ref) + cb_ref[...])
    sm_t = sm_ref[...].T
    e_mat = _expand_matrix(nb, SSD_N)
    lane_b = lax.broadcasted_iota(jnp.int32, (SSD_P, nb), 1)
    for g in range(SSD_G):
        gs = slice(g * SSD_GW, (g + 1) * SSD_GW)
        bm = xbc[:, SSD_INNER + g * SSD_N:SSD_INNER + (g + 1) * SSD_N]
        cm_t = xbc[:, SSD_INNER + SSD_BC + g * SSD_N:SSD_INNER + SSD_BC + (g + 1) * SSD_N].T
        cb_row = jnp.sum(cm_t * bm.T, axis=0, keepdims=True)
        hg = h_ref[:, pl.ds(g * SSD_HG, SSD_HG)].reshape(nb * SSD_GW, SSD_N)
        ch = _dot(hg.astype(bf16), cm_t.astype(bf16))
        pairs = []
        for jp in range(SSD_HG // 2):
            x_t = xbc[:, g * SSD_GW + jp * LANE:g * SSD_GW + (jp + 1) * LANE].T
            halves = []
            for jj in range(2):
                j = 2 * jp + jj
                hd = g * SSD_HG + j
                dt_row = jax.nn.softplus(sm_t[SM_DT + hd:SM_DT + hd + 1, :] + dtb_ref[hd])
                ea_row = jnp.exp(dt_row * (-jnp.exp(_scalar_vec(alog_ref[hd]))))
                xdt_t = x_t[jj * SSD_P:(jj + 1) * SSD_P] * dt_row
                hi, lo = _split(xdt_t)
                xb = _dot(hi, e_mat) + _dot(lo, e_mat)
                yh = jnp.zeros((SSD_P, nb), f32)
                for b in range(nb):
                    r0 = (b * SSD_HG + j) * SSD_P
                    yh = jnp.where(lane_b == b, ch[r0:r0 + SSD_P], yh)
                    ho_ref[b, hd] = (h_ref[b, hd] * ea_row[:, b:b + 1]
                                     + xb[:, b * SSD_N:(b + 1) * SSD_N] * bm[b:b + 1])
                halves.append(yh * ea_row + cb_row * xdt_t)
            pairs.append(jnp.concatenate(halves, axis=0).T)
        y = jnp.concatenate(pairs, axis=1)
        y = (y + dvec_ref[:, gs] * xbc[:, gs]) * _silu(z_ref[:, gs])
        o_ref[:, gs] = _rms(y, nw_ref[:, gs]).astype(bf16)


def _ssd_step(proj, small, conv_t, states, prev, l, conv_w, conv_b, a_log, dt_bias, dvec, norm_w):
    n = proj.shape[0]
    nb = STEP_B
    smem = pl.BlockSpec(memory_space=pltpu.SMEM)
    const = lambda shape: pl.BlockSpec(shape, lambda i: (0,) * len(shape))
    st = pl.BlockSpec((None, nb, SSD_H, SSD_P, SSD_N), lambda i: (l, i, 0, 0, 0))
    return _state_call(
        _ssd_step_kernel, (n // nb,),
        [smem, smem,
         pl.BlockSpec((nb, SSD_INNER), lambda i: (i, _blk(P_Z_C, SSD_INNER))),
         pl.BlockSpec((nb, SSD_CONV), lambda i: (i, _blk(P_XBC, SSD_CONV))),
         pl.BlockSpec((nb, LANE), lambda i: (i, 0)),
         pl.BlockSpec((CONV_W - 1, nb, SSD_CONV), lambda i: (0, i, 0)),
         const((CONV_W, SSD_CONV)), const((1, SSD_CONV)), const((1, SSD_INNER)), const((1, SSD_INNER)), st],
        [pl.BlockSpec((nb, SSD_INNER), lambda i: (i, 0)), st],
        [jax.ShapeDtypeStruct((n, SSD_INNER), bf16), jax.ShapeDtypeStruct(states.shape, f32)],
        [], ("parallel",), "ssd_step",
        (a_log, dt_bias, proj, proj, small, conv_t, conv_w, conv_b, dvec, norm_w, states),
        prev)


def _merge_kernel(oa_ref, ob_ref, oc_ref, ga_ref, gb_ref, gc_ref, wa_ref, wb_ref, wc_ref, o_ref):
    m = (_sigmoid(ga_ref[...].astype(f32)) * _dot(oa_ref[...], wa_ref[...])
         + _sigmoid(gb_ref[...].astype(f32)) * _dot(ob_ref[...], wb_ref[...])
         + _sigmoid(gc_ref[...].astype(f32)) * _dot(oc_ref[...], wc_ref[...]))
    o_ref[...] = m.astype(bf16)


def _outproj_kernel(m_ref, x_ref, gt_ref, w_ref, o_ref):
    o_ref[...] = x_ref[...] + gt_ref[...] * _dot(m_ref[...], w_ref[...])


def _mixout(oa, ob, oc, proj, x, mod, wa, wb, wc, wo, l, per_row, tm):
    bx, tx, dm = x.shape
    tn = 512
    gate = lambda k: pl.BlockSpec((None, tm, tn), lambda b, i, j: (b, i, _blk(P_GATES + k * D_MODEL, tn) + j))
    act = lambda w: pl.BlockSpec((None, tm, w), lambda b, i, j: (b, i, 0))
    wsp = lambda w: pl.BlockSpec((None, w, tn), lambda b, i, j: (l, 0, j))
    merged = pl.pallas_call(
        _merge_kernel,
        grid=(bx, tx // tm, dm // tn),
        in_specs=[act(GDN_V), act(GLA_V), act(SSD_INNER), gate(0), gate(1), gate(2),
                  wsp(GDN_V), wsp(GLA_V), wsp(SSD_INNER)],
        out_specs=pl.BlockSpec((None, tm, tn), lambda b, i, j: (b, i, j)),
        out_shape=jax.ShapeDtypeStruct((bx, tx, dm), bf16),
        compiler_params=_cp(("parallel", "parallel", "arbitrary")),
        name="merge",
    )(oa, ob, oc, proj, proj, proj, wa, wb, wc)
    to = 1024
    nj = dm // to
    if per_row:
        gspec = pl.BlockSpec((None, tm, to), lambda b, i, j: (b, i, 5 * nj + j))
    else:
        gspec = pl.BlockSpec((None, 1, to), lambda b, i, j: (b, 0, 5 * nj + j))
    return pl.pallas_call(
        _outproj_kernel,
        grid=(bx, tx // tm, nj),
        in_specs=[pl.BlockSpec((None, tm, dm), lambda b, i, j: (b, i, 0)),
                  pl.BlockSpec((None, tm, to), lambda b, i, j: (b, i, j)),
                  gspec,
                  pl.BlockSpec((None, dm, to), lambda b, i, j: (l, 0, j))],
        out_specs=pl.BlockSpec((None, tm, to), lambda b, i, j: (b, i, j)),
        out_shape=jax.ShapeDtypeStruct(x.shape, f32),
        compiler_params=_cp(("parallel", "parallel", "arbitrary")),
        name="outproj",
    )(merged, x, mod, wo)


REGROUP_ROWS = 512


def _regroup_kernel(w_ref, o_ref):
    o_ref[...] = w_ref[0].astype(bf16)


def _regroup_rows(wt):
    nl, _, d = wt.shape
    tr = REGROUP_ROWS
    shifts = []
    for name in MAIN_ORDER:
        a, b = W_IN_SRC[name]
        assert (b - a) % tr == 0 and a % SUB == 0
        shifts.append((P_MAIN[name] // tr, a - P_MAIN[name]))

    def src_row(i):
        off = i * (tr // SUB) + shifts[0][1] // SUB
        for k in range(1, len(shifts)):
            off = off + jnp.where(i >= shifts[k][0], (shifts[k][1] - shifts[k - 1][1]) // SUB, 0)
        return off * SUB

    return pl.pallas_call(
        _regroup_kernel,
        grid=(nl, P_TOTAL // tr),
        in_specs=[pl.BlockSpec((pl.Element(1), pl.Element(tr), pl.Element(d)), lambda l, i: (l, src_row(i), 0))],
        out_specs=pl.BlockSpec((None, tr, d), lambda l, i: (l, i, 0)),
        out_shape=jax.ShapeDtypeStruct((nl, P_TOTAL, d), bf16),
        compiler_params=_cp(("parallel", "parallel")),
        name="regroup_w_in",
    )(wt)


def _permute_w_in(w):
    nl, d, _ = w.shape
    main = _regroup_rows(jnp.swapaxes(w, 1, 2))
    small = jnp.concatenate([w[:, :, W_IN_SRC[name][0]:W_IN_SRC[name][1]] for name in SMALL_ORDER], axis=2)
    small = jnp.pad(jnp.swapaxes(small, 1, 2), ((0, 0), (0, LANE - small.shape[2]), (0, 0)))
    return main, small


def _ffn_weights(wg, wu, wd):
    f = wg.shape[2]
    cut = f - f % FF_TILE
    w = tuple(a.astype(bf16) for a in (wg, wu, wd))
    if cut == f:
        return w
    return w + (w[0][:, :, cut:], w[1][:, :, cut:], w[2][:, cut:, :])


def _lane_vec(v, lo):
    return jnp.zeros((1, LANE), f32).at[0, lo:lo + v.shape[0]].set(v)


def _layer_params(l, p):
    row = lambda a: a[l].reshape(1, -1)
    wgate = jnp.zeros((LANE, GLA_QK), f32).at[SM_LR:SM_LR + GLA_RANK].set(p["gla_w_gate"][l]).astype(bf16)
    return dict(
        norm1=row(p["norm1"]), norm2=row(p["norm2"]), norm3=row(p["norm3"]),
        gdn_conv_w=p["gdn_conv_w"][l], gdn_a_log=p["gdn_a_log"][l], gdn_dt_bias=p["gdn_dt_bias"][l],
        gdn_alog_l=_lane_vec(p["gdn_a_log"][l], SM_DEC), gdn_dtb_l=_lane_vec(p["gdn_dt_bias"][l], SM_DEC),
        gdn_norm_w=row(p["gdn_norm_w"]),
        gla_wgate=wgate, gla_bgate=row(p["gla_b_gate"]), gla_norm_w=row(p["gla_norm_w"]),
        ssd_conv_w=p["ssd_conv_w"][l], ssd_conv_b=row(p["ssd_conv_b"]), ssd_a_log=p["ssd_a_log"][l],
        ssd_dt_bias=p["ssd_dt_bias"][l],
        ssd_avec_l=_lane_vec(-jnp.exp(p["ssd_a_log"][l]), SM_DT), ssd_dtb_l=_lane_vec(p["ssd_dt_bias"][l], SM_DT),
        ssd_dvec=jnp.repeat(p["ssd_d"][l], SSD_P).reshape(1, -1),
        ssd_norm_w=row(p["ssd_norm_w"]),
    )


def _stacked_weights(p):
    w_in, w_in_small = _permute_w_in(p["w_in"])
    return dict(
        f1=_ffn_weights(p["ffn1_wg"], p["ffn1_wu"], p["ffn1_wd"]),
        f2=_ffn_weights(p["ffn2_wg"], p["ffn2_wu"], p["ffn2_wd"]),
        w_in=w_in, w_in_small=w_in_small,
        wa=p["w_branch_gdn"].astype(bf16), wb=p["w_branch_gla"].astype(bf16),
        wc=p["w_branch_ssd"].astype(bf16), w_out=p["w_out"].astype(bf16),
    )


def _new_conv_state(buf, raw):
    t = raw.shape[1]
    k = CONV_W - 1
    if t >= k:
        return raw[:, t - k:]
    return jnp.concatenate([buf[:, t:], raw], axis=1)


def _mixer_prompt(proj, small, lp, st):
    gdn_conv, s_gdn, s_gla, ssd_conv, s_ssd = st
    b, t, _ = proj.shape
    c = CHUNK
    sm_spec = pl.BlockSpec((None, c, LANE), lambda bi, ti: (bi, ti, 0))
    st_a, st_b, st_c = _per_seq(GDN_H, GDN_DK, GDN_DV), _per_seq(GLA_H, GLA_DK, GLA_DV), _per_seq(SSD_H, SSD_P, SSD_N)
    gdn_specs = [_tok(GDN_CONV, P_QKV_A), _tok(GDN_V, P_Z_A), sm_spec, _per_seq(CONV_W - 1, GDN_CONV),
                 _const(CONV_W, GDN_CONV), _const(1, LANE), _const(1, LANE), _const(1, GDN_DV), st_a]
    gdn_args = (proj, proj, small, gdn_conv, lp["gdn_conv_w"], lp["gdn_alog_l"], lp["gdn_dtb_l"],
                lp["gdn_norm_w"], s_gdn)
    gla_specs = [_tok(GLA_QK, P_Q_B), _tok(GLA_QK, P_K_B), _tok(GLA_V, P_V_B), _tok(GLA_V, P_R_B), sm_spec,
                 _const(LANE, GLA_QK), _const(1, GLA_QK), _const(1, GLA_DV), st_b]
    gla_args = (proj, proj, proj, proj, small, lp["gla_wgate"], lp["gla_bgate"], lp["gla_norm_w"], s_gla)
    ssd_specs = [_tok(SSD_INNER, P_Z_C), _tok(SSD_CONV, P_XBC), sm_spec, _per_seq(CONV_W - 1, SSD_CONV),
                 _const(CONV_W, SSD_CONV), _const(1, SSD_CONV), _const(1, LANE), _const(1, LANE),
                 _const(1, SSD_INNER), _const(1, SSD_INNER), st_c]
    ssd_args = (proj, proj, small, ssd_conv, lp["ssd_conv_w"], lp["ssd_conv_b"], lp["ssd_avec_l"],
                lp["ssd_dtb_l"], lp["ssd_dvec"], lp["ssd_norm_w"], s_ssd)
    assert (len(gdn_specs), len(gla_specs), len(ssd_specs)) == (N_GDN_IN, N_GLA_IN, N_SSD_IN)
    out_tok = lambda width: pl.BlockSpec((None, c, width), lambda bi, ti: (bi, ti, 0))
    oa, s_gdn_n, ob, s_gla_n, oc, s_ssd_n = pl.pallas_call(
        _mixers_kernel,
        grid=(b, t // c),
        in_specs=gdn_specs + gla_specs + ssd_specs,
        out_specs=[out_tok(GDN_V), st_a, out_tok(GLA_V), st_b, out_tok(SSD_INNER), st_c],
        out_shape=[jax.ShapeDtypeStruct((b, t, GDN_V), bf16), jax.ShapeDtypeStruct(s_gdn.shape, f32),
                   jax.ShapeDtypeStruct((b, t, GLA_V), bf16), jax.ShapeDtypeStruct(s_gla.shape, f32),
                   jax.ShapeDtypeStruct((b, t, SSD_INNER), bf16), jax.ShapeDtypeStruct(s_ssd.shape, f32)],
        scratch_shapes=[pltpu.VMEM((GDN_CONV // LANE, 8 + c, LANE), f32), pltpu.VMEM((GLA_H, GLA_DV, GLA_DK), f32),
                        pltpu.VMEM((SSD_CONV // LANE, 8 + c, LANE), f32),
                        pltpu.VMEM((GDN_V // LANE, c, LANE), f32), pltpu.VMEM((SSD_INNER // LANE, c, LANE), f32)],
        compiler_params=_cp(("parallel", "arbitrary")),
        name="mixers_prompt",
    )(*gdn_args, *gla_args, *ssd_args)
    gdn_conv_n = _new_conv_state(gdn_conv, proj[:, :, P_QKV_A:P_QKV_A + GDN_CONV]).astype(f32)
    ssd_conv_n = _new_conv_state(ssd_conv, proj[:, :, P_XBC:P_XBC + SSD_CONV]).astype(f32)
    return (oa, ob, oc), (gdn_conv_n, s_gdn_n, s_gla_n, ssd_conv_n, s_ssd_n)


def _mixer_sample(proj, small, lp, l, states, prev):
    n = proj.shape[1]
    p2 = proj.reshape(n, P_TOTAL)
    s2 = small.reshape(n, LANE)
    gdn_conv, ssd_conv = states[0][l], states[3][l]
    gct = jnp.swapaxes(gdn_conv, 0, 1)
    sct = jnp.swapaxes(ssd_conv, 0, 1)
    pv = (None,) * 5 if prev is None else prev
    oa, s_gdn_n = _gdn_step(p2, s2, gct, states[1], pv[1], l, lp["gdn_conv_w"], lp["gdn_a_log"],
                            lp["gdn_dt_bias"], lp["gdn_norm_w"])
    ob, s_gla_n = _gla_step(p2, s2, states[2], pv[2], l, lp["gla_wgate"], lp["gla_bgate"], lp["gla_norm_w"])
    oc, s_ssd_n = _ssd_step(p2, s2, sct, states[4], pv[4], l, lp["ssd_conv_w"], lp["ssd_conv_b"], lp["ssd_a_log"],
                            lp["ssd_dt_bias"], lp["ssd_dvec"], lp["ssd_norm_w"])
    raw = p2.reshape(n, 1, P_TOTAL)
    gdn_conv_n = _new_conv_state(gdn_conv, raw[:, :, P_QKV_A:P_QKV_A + GDN_CONV])
    ssd_conv_n = _new_conv_state(ssd_conv, raw[:, :, P_XBC:P_XBC + SSD_CONV])
    outs = tuple(o.reshape(1, n, -1) for o in (oa, ob, oc))
    return outs, (gdn_conv_n, s_gdn_n, s_gla_n, ssd_conv_n, s_ssd_n)


def _trunk(x, mods, lps, sw, states, per_row, tm, tm_in, final_w):
    nl = len(lps)
    per_layer = []
    prev = None
    for l in range(nl):
        lp, mod = lps[l], mods[l]
        last = l == nl - 1
        x = _ffn(x, mod, 0, lp["norm1"], sw["f1"], l, lp["norm1"], per_row, tm, False)
        proj, small = _inproj(x, mod, lp["norm2"], sw["w_in"], sw["w_in_small"], l, per_row, tm_in,
                              f32 if per_row else bf16)
        if per_row:
            (oa, ob, oc), st = _mixer_sample(proj, small, lp, l, states, prev)
            prev = st
        else:
            (oa, ob, oc), st = _mixer_prompt(proj, small, lp, tuple(s[l] for s in states))
        per_layer.append(st)
        x = _mixout(oa, ob, oc, proj, x, mod, sw["wa"], sw["wb"], sw["wc"], sw["w_out"], l, per_row, tm_in)
        x = _ffn(x, mod, 6, lp["norm3"], sw["f2"], l, final_w if last else lp["norm3"], per_row, tm, last)
    stack = lambda i: jnp.stack([st[i] for st in per_layer])
    if per_row:
        new_states = (stack(0), prev[1], prev[2], stack(3), prev[4])
    else:
        new_states = tuple(stack(i) for i in range(5))
    return x, new_states


def kernel(x_prompt, x_sample, state_gdn_conv, state_gdn, state_gla, state_ssd_conv, state_ssd, c_prompt, c_sample, w_ada, b_ada, norm1, norm2, norm3, ffn1_wg, ffn1_wu, ffn1_wd, ffn2_wg, ffn2_wu, ffn2_wd, w_in, gdn_conv_w, gdn_a_log, gdn_dt_bias, gdn_norm_w, gla_w_gate, gla_b_gate, gla_norm_w, ssd_conv_w, ssd_conv_b, ssd_a_log, ssd_dt_bias, ssd_d, ssd_norm_w, w_branch_gdn, w_branch_gla, w_branch_ssd, w_out, final_norm):
    p = dict(norm1=norm1, norm2=norm2, norm3=norm3,
             ffn1_wg=ffn1_wg, ffn1_wu=ffn1_wu, ffn1_wd=ffn1_wd, ffn2_wg=ffn2_wg, ffn2_wu=ffn2_wu, ffn2_wd=ffn2_wd,
             w_in=w_in, gdn_conv_w=gdn_conv_w, gdn_a_log=gdn_a_log, gdn_dt_bias=gdn_dt_bias, gdn_norm_w=gdn_norm_w,
             gla_w_gate=gla_w_gate, gla_b_gate=gla_b_gate, gla_norm_w=gla_norm_w,
             ssd_conv_w=ssd_conv_w, ssd_conv_b=ssd_conv_b, ssd_a_log=ssd_a_log, ssd_dt_bias=ssd_dt_bias,
             ssd_d=ssd_d, ssd_norm_w=ssd_norm_w,
             w_branch_gdn=w_branch_gdn, w_branch_gla=w_branch_gla, w_branch_ssd=w_branch_ssd, w_out=w_out)
    nl = w_ada.shape[0]
    bp, tp, dm = x_prompt.shape
    bs = x_sample.shape[0]
    assert x_sample.shape[1] == 1 and tp % CHUNK == 0 and bs % STEP_B == 0 and dm == D_MODEL
    lps = [_layer_params(l, p) for l in range(nl)]
    sw = _stacked_weights(p)
    fw = final_norm.reshape(1, dm)

    rows = bp + bs
    rpad = -(-rows // 8) * 8
    c_all = jnp.concatenate([c_prompt, c_sample, jnp.zeros((rpad - rows, dm), f32)], axis=0)
    mod = _ada_mod(c_all, w_ada, b_ada)
    mod_p = [mod[l, :bp].reshape(bp, 1, N_MOD * dm) for l in range(nl)]
    mod_s = [mod[l, bp:rows].reshape(1, bs, N_MOD * dm) for l in range(nl)]

    sample_states = (state_gdn_conv, state_gdn, state_gla, state_ssd_conv, state_ssd)
    prompt_states = tuple(jnp.zeros((s.shape[0], bp) + s.shape[2:], x_prompt.dtype) for s in sample_states)
    tm_p = 512 if tp % 512 == 0 else CHUNK
    tm_in = 1024 if tp % 1024 == 0 else tm_p
    y_p, st_p = _trunk(x_prompt, mod_p, lps, sw, prompt_states, False, tm_p, tm_in, fw)
    y_s, st_s = _trunk(x_sample.reshape(1, bs, dm), mod_s, lps, sw, sample_states, True, bs, bs, fw)
    return (y_p, y_s.reshape(bs, 1, dm)) + st_p + st_s
```

```python
import functools

import jax
import jax.numpy as jnp
from jax import lax
from jax.experimental import pallas as pl
from jax.experimental.pallas import tpu as pltpu

f32 = jnp.float32
bf16 = jnp.bfloat16
HI = lax.Precision.HIGHEST

EPS = 1e-6
D_MODEL = 2048
N_MOD = 9
CHUNK = 64
CONV_W = 4
GDN_H, GDN_DK, GDN_DV = 8, 128, 128
GLA_H, GLA_DK, GLA_DV, GLA_RANK, GLA_TAU = 4, 128, 256, 16, 16.0
SSD_H, SSD_P, SSD_G, SSD_N = 32, 64, 4, 128
SSD_HG = SSD_H // SSD_G
GDN_QK = GDN_H * GDN_DK
GDN_V = GDN_H * GDN_DV
GDN_CONV = 2 * GDN_QK + GDN_V
GLA_QK = GLA_H * GLA_DK
GLA_V = GLA_H * GLA_DV
SSD_INNER = SSD_H * SSD_P
SSD_BC = SSD_G * SSD_N
SSD_CONV = SSD_INNER + 2 * SSD_BC
SSD_GW = SSD_HG * SSD_P

LANE = 128
FF_TILE = 512
FF_SUB = 256

IN_SPLITS = (("qkv_a", GDN_CONV), ("z_a", GDN_V), ("beta", GDN_H), ("dec", GDN_H),
             ("q_b", GLA_QK), ("k_b", GLA_QK), ("v_b", GLA_V), ("lr", GLA_RANK), ("r_b", GLA_V),
             ("z_c", SSD_INNER), ("xbc", SSD_CONV), ("dt", SSD_H), ("gates", 3 * D_MODEL))
MAIN_ORDER = ("qkv_a", "xbc", "z_c", "gates", "z_a", "q_b", "k_b", "v_b", "r_b")
SMALL_ORDER = ("beta", "dec", "lr", "dt")


def _layout():
    src, off = {}, 0
    for name, w in IN_SPLITS:
        src[name] = (off, off + w)
        off += w
    main, small, d = {}, {}, 0
    for name in MAIN_ORDER:
        w = src[name][1] - src[name][0]
        main[name] = d
        d += w
    total = d
    d = 0
    for name in SMALL_ORDER:
        small[name] = d
        d += src[name][1] - src[name][0]
    assert d <= LANE
    return src, main, small, total


W_IN_SRC, P_MAIN, P_SM, P_TOTAL = _layout()
P_QKV_A, P_XBC, P_Z_C, P_GATES, P_Z_A = (P_MAIN[k] for k in ("qkv_a", "xbc", "z_c", "gates", "z_a"))
P_Q_B, P_K_B, P_V_B, P_R_B = (P_MAIN[k] for k in ("q_b", "k_b", "v_b", "r_b"))
SM_BETA, SM_DEC, SM_LR, SM_DT = (P_SM[k] for k in SMALL_ORDER)

VMEM_LIMIT = 56 * 1024 * 1024


def _cp(sem):
    return pltpu.CompilerParams(dimension_semantics=sem, vmem_limit_bytes=VMEM_LIMIT)


def _blk(off, width):
    assert off % width == 0, (off, width)
    return off // width


def _sigmoid(x):
    return 0.5 + 0.5 * jnp.tanh(0.5 * x)


def _silu(x):
    h = 0.5 * x
    return h + h * jnp.tanh(h)


def _rms(x, w):
    return x * lax.rsqrt(jnp.mean(x * x, axis=-1, keepdims=True) + EPS) * w


def _dot(a, b):
    return jnp.dot(a, b, preferred_element_type=f32)


def _dot_nt(a, b):
    return lax.dot_general(a, b, (((1,), (1,)), ((), ())), preferred_element_type=f32)


def _dot_tn(a, b):
    return lax.dot_general(a, b, (((0,), (0,)), ((), ())), preferred_element_type=f32)


def _split(x):
    hi = x.astype(bf16)
    return hi, (x - hi.astype(f32)).astype(bf16)


def _dot3(a, b):
    return _dot(a[0], b[0]) + (_dot(a[0], b[1]) + _dot(a[1], b[0]))


def _dot_hi(a, b):
    return jnp.dot(a, b, precision=HI, preferred_element_type=f32)


def _tri_masks(c):
    row = lax.broadcasted_iota(jnp.int32, (c, c), 0)
    col = lax.broadcasted_iota(jnp.int32, (c, c), 1)
    return row >= col, row > col, row == col


SUB = 8


def _row_time(idx, c):
    n = c // SUB
    assert n & (n - 1) == 0, "chunk / SUB must be a power of two"
    sh = n.bit_length() - 1
    return ((idx & (n - 1)) << 3) | (idx >> sh)


def _tri_masks_residue(c, cols=None):
    cols = c if cols is None else cols
    row = _row_time(lax.broadcasted_iota(jnp.int32, (c, cols), 0), c)
    col = _row_time(lax.broadcasted_iota(jnp.int32, (c, cols), 1) & (c - 1), c)
    return row >= col, row > col, row == col


def _load_residue(ref, c, sl=slice(None)):
    n = c // SUB
    return jnp.concatenate([ref[pl.ds(r, n, stride=SUB), sl] for r in range(SUB)], axis=0)


def _store_natural(nat_ref, x, c, lo):
    n = c // SUB
    for j in range(x.shape[1] // LANE):
        for r in range(SUB):
            nat_ref[lo // LANE + j, pl.ds(r, n, stride=SUB), :] = x[r * n:(r + 1) * n, j * LANE:(j + 1) * LANE]


def _read_natural(nat_ref, lo, width):
    return jnp.concatenate([nat_ref[lo // LANE + j] for j in range(width // LANE)], axis=1)


def _lane_col(x, idx):
    lane = lax.broadcasted_iota(jnp.int32, x.shape, 1)
    return jnp.sum(jnp.where(lane == idx, x, 0.0), axis=1, keepdims=True)


def _scalar_vec(s):
    return jnp.full((1, 1), s, f32)


def _ada_kernel(c_ref, w_ref, b_ref, o_ref):
    s = _silu(c_ref[...]).astype(bf16)
    o_ref[...] = _dot(s, w_ref[...].astype(bf16)) + b_ref[...]


def _ada_mod(c_all, w_ada, b_ada):
    nl, dm, n = w_ada.shape
    r = c_all.shape[0]
    tn = 1024
    return pl.pallas_call(
        _ada_kernel,
        grid=(nl, n // tn),
        in_specs=[pl.BlockSpec((r, dm), lambda l, j: (0, 0)),
                  pl.BlockSpec((None, dm, tn), lambda l, j: (l, 0, j)),
                  pl.BlockSpec((None, 1, tn), lambda l, j: (l, 0, j))],
        out_specs=pl.BlockSpec((None, r, tn), lambda l, j: (l, 0, j)),
        out_shape=jax.ShapeDtypeStruct((nl, r, n), f32),
        compiler_params=_cp(("arbitrary", "arbitrary")),
        name="ada_mod",
    )(c_all, w_ada, b_ada.reshape(nl, 1, n))


def _mod_spec(per_row, tm, chunk):
    if per_row:
        return pl.BlockSpec((None, tm, D_MODEL), lambda b, i, j: (b, i, chunk))
    return pl.BlockSpec((None, 1, D_MODEL), lambda b, i, j: (b, 0, chunk))


def _ffn_prologue(x_ref, sh_ref, sc_ref, nw_ref, h_ref, acc_ref):
    y = _rms(x_ref[...], nw_ref[...])
    h_ref[...] = (y * (1.0 + sc_ref[...]) + sh_ref[...]).astype(bf16)
    acc_ref[...] = jnp.zeros_like(acc_ref)


def _ffn_accumulate(h_ref, acc_ref, wg, wu, wd):
    h = h_ref[...]
    width = wg.shape[1]
    sub = FF_SUB if width % FF_SUB == 0 else width
    parts = [(_dot(h, wg[:, s:s + sub]), _dot(h, wu[:, s:s + sub])) for s in range(0, width, sub)]
    acc = acc_ref[...]
    for i, (g, u) in enumerate(parts):
        acc = acc + _dot((_silu(g) * u).astype(bf16), wd[i * sub:(i + 1) * sub, :])
    acc_ref[...] = acc


def _ffn_epilogue(x_ref, gt_ref, fw_ref, acc_ref, o_ref, final):
    y = x_ref[...] + 0.5 * gt_ref[...] * acc_ref[...]
    if final:
        y = _rms(y, fw_ref[...])
    o_ref[...] = y


def _ffn_kernel(x_ref, sh_ref, sc_ref, gt_ref, nw_ref, fw_ref, wg_ref, wu_ref, wd_ref, *rest, final, has_tail):
    if has_tail:
        wgt_ref, wut_ref, wdt_ref, o_ref, h_ref, acc_ref = rest
    else:
        o_ref, h_ref, acc_ref = rest
    f = pl.program_id(2)
    last = pl.num_programs(2) - 1
    pl.when(f == 0)(lambda: _ffn_prologue(x_ref, sh_ref, sc_ref, nw_ref, h_ref, acc_ref))
    if has_tail:
        pl.when(f < last)(lambda: _ffn_accumulate(h_ref, acc_ref, wg_ref, wu_ref, wd_ref))
        pl.when(f == last)(lambda: _ffn_accumulate(h_ref, acc_ref, wgt_ref, wut_ref, wdt_ref))
    else:
        _ffn_accumulate(h_ref, acc_ref, wg_ref, wu_ref, wd_ref)
    pl.when(f == last)(lambda: _ffn_epilogue(x_ref, gt_ref, fw_ref, acc_ref, o_ref, final))


def _ffn(x, mod, k0, nw, w, fw, per_row, tm, final):
    bx, tx, dm = x.shape
    tf = FF_TILE
    n_full = w[0].shape[1] // tf
    has_tail = len(w) > 3
    full = lambda f: jnp.minimum(f, n_full - 1)
    w_specs = [pl.BlockSpec((dm, tf), lambda b, i, f: (0, full(f))),
               pl.BlockSpec((dm, tf), lambda b, i, f: (0, full(f))),
               pl.BlockSpec((tf, dm), lambda b, i, f: (full(f), 0))]
    if has_tail:
        ft = w[3].shape[1]
        w_specs += [pl.BlockSpec((dm, ft), lambda b, i, f: (0, 0)),
                    pl.BlockSpec((dm, ft), lambda b, i, f: (0, 0)),
                    pl.BlockSpec((ft, dm), lambda b, i, f: (0, 0))]
    return pl.pallas_call(
        functools.partial(_ffn_kernel, final=final, has_tail=has_tail),
        grid=(bx, tx // tm, n_full + int(has_tail)),
        in_specs=[pl.BlockSpec((None, tm, dm), lambda b, i, f: (b, i, 0)),
                  _mod_spec(per_row, tm, k0), _mod_spec(per_row, tm, k0 + 1), _mod_spec(per_row, tm, k0 + 2),
                  pl.BlockSpec((1, dm), lambda b, i, f: (0, 0)),
                  pl.BlockSpec((1, dm), lambda b, i, f: (0, 0))] + w_specs,
        out_specs=pl.BlockSpec((None, tm, dm), lambda b, i, f: (b, i, 0)),
        out_shape=jax.ShapeDtypeStruct(x.shape, f32),
        scratch_shapes=[pltpu.VMEM((tm, dm), bf16), pltpu.VMEM((tm, dm), f32)],
        compiler_params=_cp(("parallel", "parallel", "arbitrary")),
        name="ffn",
    )(x, mod, mod, mod, nw, fw, *w)


FF_CAST_TILE = 256


def _ffn_cast_kernel(x_ref, sh_ref, sc_ref, gt_ref, nw_ref, fw_ref, wg_ref, wu_ref, wd_ref, *rest, final, has_tail):
    if has_tail:
        wgt_ref, wut_ref, wdt_ref, o_ref, wgb_ref, wub_ref, wdb_ref, wgtb_ref, wutb_ref, wdtb_ref, h_ref, acc_ref = rest
    else:
        o_ref, wgb_ref, wub_ref, wdb_ref, h_ref, acc_ref = rest
    f = pl.program_id(2)
    last = pl.num_programs(2) - 1
    pl.when(f == 0)(lambda: _ffn_prologue(x_ref, sh_ref, sc_ref, nw_ref, h_ref, acc_ref))

    def cast_accumulate(src, dst):
        for s_ref, d_ref in zip(src, dst):
            d_ref[...] = s_ref[...].astype(bf16)
        _ffn_accumulate(h_ref, acc_ref, *dst)

    if has_tail:
        pl.when(f < last)(lambda: cast_accumulate((wg_ref, wu_ref, wd_ref), (wgb_ref, wub_ref, wdb_ref)))
        pl.when(f == last)(lambda: cast_accumulate((wgt_ref, wut_ref, wdt_ref), (wgtb_ref, wutb_ref, wdtb_ref)))
    else:
        cast_accumulate((wg_ref, wu_ref, wd_ref), (wgb_ref, wub_ref, wdb_ref))
    pl.when(f == last)(lambda: _ffn_epilogue(x_ref, gt_ref, fw_ref, acc_ref, o_ref, final))


def _ffn_cast(x, mod, k0, nw, wg, wu, wd, l, fw, final):
    bx, tx, dm = x.shape
    assert bx == 1
    f_all = wg.shape[2]
    tf = FF_CAST_TILE
    cut = f_all - f_all % FF_TILE
    n_full = cut // tf
    has_tail = cut < f_all
    full = lambda f: jnp.minimum(f, n_full - 1)
    w_in_specs = [pl.BlockSpec((None, dm, tf), lambda b, i, f: (l, 0, full(f))),
                  pl.BlockSpec((None, dm, tf), lambda b, i, f: (l, 0, full(f))),
                  pl.BlockSpec((None, tf, dm), lambda b, i, f: (l, full(f), 0))]
    w_out_specs = [pl.BlockSpec((dm, tf), lambda b, i, f: (0, full(f))),
                   pl.BlockSpec((dm, tf), lambda b, i, f: (0, full(f))),
                   pl.BlockSpec((tf, dm), lambda b, i, f: (full(f), 0))]
    w_out_shapes = [jax.ShapeDtypeStruct((dm, cut), bf16), jax.ShapeDtypeStruct((dm, cut), bf16),
                    jax.ShapeDtypeStruct((cut, dm), bf16)]
    args = [wg, wu, wd]
    if has_tail:
        ft = f_all - cut
        args += [wg[l, :, cut:], wu[l, :, cut:], wd[l, cut:, :]]
        tails = [pl.BlockSpec((dm, ft), lambda b, i, f: (0, 0)), pl.BlockSpec((dm, ft), lambda b, i, f: (0, 0)),
                 pl.BlockSpec((ft, dm), lambda b, i, f: (0, 0))]
        w_in_specs += tails
        w_out_specs += tails
        w_out_shapes += [jax.ShapeDtypeStruct((dm, ft), bf16), jax.ShapeDtypeStruct((dm, ft), bf16),
                         jax.ShapeDtypeStruct((ft, dm), bf16)]
    out = pl.pallas_call(
        functools.partial(_ffn_cast_kernel, final=final, has_tail=has_tail),
        grid=(1, 1, n_full + int(has_tail)),
        in_specs=[pl.BlockSpec((None, tx, dm), lambda b, i, f: (b, i, 0)),
                  _mod_spec(True, tx, k0), _mod_spec(True, tx, k0 + 1), _mod_spec(True, tx, k0 + 2),
                  pl.BlockSpec((1, dm), lambda b, i, f: (0, 0)),
                  pl.BlockSpec((1, dm), lambda b, i, f: (0, 0))] + w_in_specs,
        out_specs=[pl.BlockSpec((None, tx, dm), lambda b, i, f: (b, i, 0))] + w_out_specs,
        out_shape=[jax.ShapeDtypeStruct(x.shape, f32)] + w_out_shapes,
        scratch_shapes=[pltpu.VMEM((tx, dm), bf16), pltpu.VMEM((tx, dm), f32)],
        compiler_params=_cp(("arbitrary", "arbitrary", "arbitrary")),
        name="ffn_cast",
    )(x, mod, mod, mod, nw, fw, *args)
    return out[0], tuple(out[1:])


def _inproj_kernel(x_ref, sh_ref, sc_ref, nw_ref, w_ref, ws_ref, o_ref, os_ref, h_ref):
    @pl.when(pl.program_id(2) == 0)
    def _():
        y = _rms(x_ref[...], nw_ref[...])
        h = (y * (1.0 + sc_ref[...]) + sh_ref[...]).astype(bf16)
        h_ref[...] = h
        os_ref[...] = _dot_nt(h, ws_ref[...].astype(bf16))

    o_ref[...] = _dot_nt(h_ref[...], w_ref[...]).astype(o_ref.dtype)


def _inproj(x, mod, nw, w, ws, l, per_row, tm, out_dtype):
    bx, tx, dm = x.shape
    n = w.shape[1]
    tn = 1024
    return pl.pallas_call(
        _inproj_kernel,
        grid=(bx, tx // tm, n // tn),
        in_specs=[pl.BlockSpec((None, tm, dm), lambda b, i, j: (b, i, 0)),
                  _mod_spec(per_row, tm, 3), _mod_spec(per_row, tm, 4),
                  pl.BlockSpec((1, dm), lambda b, i, j: (0, 0)),
                  pl.BlockSpec((None, tn, dm), lambda b, i, j: (l, j, 0)),
                  pl.BlockSpec((None, LANE, dm), lambda b, i, j: (l, 0, 0))],
        out_specs=[pl.BlockSpec((None, tm, tn), lambda b, i, j: (b, i, j)),
                   pl.BlockSpec((None, tm, LANE), lambda b, i, j: (b, i, 0))],
        out_shape=[jax.ShapeDtypeStruct((bx, tx, n), out_dtype),
                   jax.ShapeDtypeStruct((bx, tx, LANE), f32)],
        scratch_shapes=[pltpu.VMEM((tm, dm), bf16)],
        compiler_params=_cp(("parallel", "parallel", "arbitrary")),
        name="inproj",
    )(x, mod, mod, nw, w, ws)


def _conv_load(ext_ref, x_ref, cst_ref, first):
    c = x_ref.shape[0]
    nblk = ext_ref.shape[0]

    @pl.when(first)
    def _():
        for j in range(nblk):
            ext_ref[j, pl.ds(5, CONV_W - 1), :] = cst_ref[:, j * LANE:(j + 1) * LANE]

    for j in range(nblk):
        ext_ref[j, pl.ds(8, c), :] = x_ref[:, j * LANE:(j + 1) * LANE].astype(f32)


def _conv_cols(ext_ref, w_ref, b_ref, lo, width, c):
    n = c // SUB
    cols = []
    for blk in range(lo // LANE, (lo + width) // LANE):
        sl = pl.ds(blk * LANE, LANE)
        w = w_ref[:, sl]
        taps = [ext_ref[blk, pl.ds(8 - (CONV_W - 1) + k, n, stride=SUB), :] for k in range(SUB + CONV_W - 1)]
        ys = []
        for r in range(SUB):
            y = taps[r] * w[0:1]
            for j in range(1, CONV_W):
                y = y + taps[r + j] * w[j:j + 1]
            ys.append(y)
        y = jnp.concatenate(ys, axis=0)
        if b_ref is not None:
            y = y + b_ref[:, sl]
        cols.append(_silu(y))
    return cols[0] if len(cols) == 1 else jnp.concatenate(cols, axis=1)


def _conv_carry(ext_ref, c):
    for j in range(ext_ref.shape[0]):
        ext_ref[j, pl.ds(5, CONV_W - 1), :] = ext_ref[j, pl.ds(8 + c - (CONV_W - 1), CONV_W - 1), :]


def _cumsum_rows(x, tri):
    return _dot_hi(jnp.where(tri, 1.0, 0.0).astype(f32), x)


def _gdn_kernel(qkv_ref, z_ref, sm_ref, cst_ref, cw_ref, alog_ref, dtb_ref, nw_ref, s0_ref,
                o_ref, s_ref, ext_ref, nat_ref):
    c = qkv_ref.shape[0]
    first = pl.program_id(1) == 0

    @pl.when(first)
    def _():
        s_ref[...] = s0_ref[...]

    _conv_load(ext_ref, qkv_ref, cst_ref, first)

    sm = _load_residue(sm_ref, c)
    beta_l = _sigmoid(sm)
    g_l = -jnp.exp(alog_ref[...]) * jax.nn.softplus(sm + dtb_ref[...])
    tri, strict, eye = _tri_masks_residue(c)
    gc_all = _cumsum_rows(g_l, tri)
    gc_t = gc_all.T
    eye_f = jnp.where(eye, 1.0, 0.0).astype(f32)
    n_fac = max(1, (c - 1).bit_length() - 1)

    hs = range(GDN_H)
    qb, kb, ks, decay, eg, gcs, nm, rhs = [], [], [], [], [], [], [], []
    for h in hs:
        q = _conv_cols(ext_ref, cw_ref, None, h * GDN_DK, GDN_DK, c)
        k = _conv_cols(ext_ref, cw_ref, None, GDN_QK + h * GDN_DK, GDN_DK, c)
        v = _conv_cols(ext_ref, cw_ref, None, 2 * GDN_QK + h * GDN_DV, GDN_DV, c)
        q = q * lax.rsqrt(jnp.sum(q * q, axis=-1, keepdims=True) + EPS) * (GDN_DK ** -0.5)
        k = k * lax.rsqrt(jnp.sum(k * k, axis=-1, keepdims=True) + EPS)
        beta = beta_l[:, SM_BETA + h:SM_BETA + h + 1]
        gc = gc_all[:, SM_DEC + h:SM_DEC + h + 1]
        d = jnp.exp(jnp.where(tri, gc - gc_t[SM_DEC + h:SM_DEC + h + 1, :], -jnp.inf))
        e = jnp.exp(gc)
        qb.append(q.astype(bf16))
        kb.append(k.astype(bf16))
        ks.append(k)
        decay.append(d)
        eg.append(e)
        gcs.append(gc)
        nm.append(jnp.where(strict, beta * d * _dot_nt(kb[h], kb[h]), 0.0))
        rhs.append(jnp.concatenate([v * beta, k * (beta * e)], axis=1))
        yield

    inv = [eye_f - nm[h] for h in hs]
    pw_s = [_split(nm[h]) for h in hs]
    pw_s = [_split(_dot3(pw_s[h], pw_s[h])) for h in hs]
    yield
    for i in range(n_fac):
        inv = [inv[h] + _dot3(_split(inv[h]), pw_s[h]) for h in hs]
        yield
        if i + 1 < n_fac:
            pw_s = [_split(_dot3(pw_s[h], pw_s[h])) for h in hs]
            yield
    sol = []
    for h in hs:
        inv_hi, inv_lo = _split(inv[h])
        rb = rhs[h].astype(bf16)
        sol.append(_dot(inv_hi, rb) + _dot(inv_lo, rb))
    yield

    s = [s_ref[h] for h in hs]
    sb = [s[h].astype(bf16) for h in hs]
    wb = [(sol[h][:, 0:GDN_DV] - _dot(sol[h][:, GDN_DV:].astype(bf16), sb[h])).astype(bf16) for h in hs]
    yield
    qk = [(_dot_nt(qb[h], kb[h]) * decay[h]).astype(bf16) for h in hs]
    yield
    o = [_dot(qb[h], sb[h]) * eg[h] + _dot(qk[h], wb[h]) for h in hs]
    yield
    for h in hs:
        g_last = gcs[h][c - 1:c]
        kd = (ks[h] * jnp.exp(g_last - gcs[h])).astype(bf16)
        s_ref[h] = s[h] * jnp.exp(g_last) + _dot_tn(kd, wb[h])
    yield
    for h in hs:
        sl = slice(h * GDN_DV, (h + 1) * GDN_DV)
        _store_natural(nat_ref, _rms(o[h], nw_ref[...]), c, h * GDN_DV)
        o_ref[:, sl] = (_read_natural(nat_ref, h * GDN_DV, GDN_DV) * _silu(z_ref[:, sl].astype(f32))).astype(bf16)
        yield

    _conv_carry(ext_ref, c)


def _tok(width, off):
    return pl.BlockSpec((None, CHUNK, width), lambda bi, ti: (bi, ti, _blk(off, width)))


def _per_seq(*shape):
    return pl.BlockSpec((None,) + shape, lambda bi, ti: (bi,) + (0,) * len(shape))


def _const(*shape):
    return pl.BlockSpec(shape, lambda bi, ti: (0,) * len(shape))


N_GDN_IN, N_GLA_IN, N_SSD_IN = 9, 9, 11
_DONE = object()


def _mixers_kernel(*refs):
    i0, i1, i2 = N_GDN_IN, N_GDN_IN + N_GLA_IN, N_GDN_IN + N_GLA_IN + N_SSD_IN
    gdn_in, gla_in, ssd_in = refs[:i0], refs[i0:i1], refs[i1:i2]
    oa_ref, sa_ref, ob_ref, sb_ref, oc_ref, sc_ref, ext_a_ref, st_b_ref, ext_c_ref, nat_a_ref, nat_c_ref = refs[i2:]
    live = [_gdn_kernel(*gdn_in, oa_ref, sa_ref, ext_a_ref, nat_a_ref),
            _gla_kernel(*gla_in, ob_ref, sb_ref, st_b_ref),
            _ssd_kernel(*ssd_in, oc_ref, sc_ref, ext_c_ref, nat_c_ref)]
    while live:
        for gen in list(live):
            if next(gen, _DONE) is _DONE:
                live.remove(gen)


GLA_SUB = 16


def _gla_kernel(q_ref, k_ref, v_ref, r_ref, sm_ref, wgate_ref, bgate_ref, nw_ref, s0_ref,
                o_ref, so_ref, st_ref):
    c = q_ref.shape[0]
    ti = pl.program_id(1)

    @pl.when(ti == 0)
    def _():
        for h in range(GLA_H):
            st_ref[h] = s0_ref[h].T

    sm = sm_ref[...]
    lane = lax.broadcasted_iota(jnp.int32, sm.shape, 1)
    lr = jnp.where((lane >= SM_LR) & (lane < SM_LR + GLA_RANK), sm, 0.0).astype(bf16)
    la = jax.nn.log_sigmoid(_dot(lr, wgate_ref[...]) + bgate_ref[...]) / GLA_TAU
    tri, _, _ = _tri_masks(c)
    b_all = _cumsum_rows(la, tri)

    col = lax.broadcasted_iota(jnp.int32, (GLA_SUB, c), 1)
    row = lax.broadcasted_iota(jnp.int32, (GLA_SUB, c), 0)
    hs = range(GLA_H)
    qs, ks, vbs, bs, atts = [], [], [], [], []
    for h in hs:
        sk = slice(h * GLA_DK, (h + 1) * GLA_DK)
        q = q_ref[:, sk].astype(f32) * (GLA_DK ** -0.5)
        k = k_ref[:, sk].astype(f32)
        b = b_all[:, sk]
        qs.append(q)
        ks.append(k)
        bs.append(b)
        vbs.append(v_ref[:, h * GLA_DV:(h + 1) * GLA_DV].astype(bf16))
        att_rows = []
        for i in range(c // GLA_SUB):
            lo = i * GLA_SUB
            b_i = b[lo:lo + GLA_SUB]
            q_i = q[lo:lo + GLA_SUB]
            b_top = b[lo:lo + 1]
            att = jnp.zeros((GLA_SUB, c), f32)
            if i > 0:
                q_t = (q_i * jnp.exp(b_i - b_top)).astype(bf16)
                k_t = (k * jnp.exp(jnp.minimum(b_top - b, 0.0))).astype(bf16)
                att = jnp.where(col < lo, _dot_nt(q_t, k_t), 0.0)
            for sl in range(GLA_SUB):
                sidx = lo + sl
                e = jnp.exp(jnp.minimum(b_i - b[sidx:sidx + 1], 0.0))
                p = jnp.sum(q_i * k[sidx:sidx + 1] * e, axis=1, keepdims=True)
                att = jnp.where((col == sidx) & (row >= sl), p, att)
            att_rows.append(att)
            yield
        atts.append(jnp.concatenate(att_rows, axis=0).astype(bf16))

    sts = [st_ref[h] for h in hs]
    os_ = [_dot(atts[h], vbs[h]) + _dot_nt((qs[h] * jnp.exp(bs[h])).astype(bf16), sts[h].astype(bf16))
           for h in hs]
    yield
    for h in hs:
        b_last = bs[h][c - 1:c]
        kd = (ks[h] * jnp.exp(b_last - bs[h])).astype(bf16)
        st_ref[h] = sts[h] * jnp.exp(b_last) + _dot_tn(vbs[h], kd)
        yield
    for h in hs:
        sv = slice(h * GLA_DV, (h + 1) * GLA_DV)
        o_ref[:, sv] = (_rms(os_[h], nw_ref[...]) * _silu(r_ref[:, sv].astype(f32))).astype(bf16)
        yield

    @pl.when(ti == pl.num_programs(1) - 1)
    def _():
        for h in hs:
            so_ref[h] = st_ref[h].T


def _ssd_kernel(z_ref, xbc_ref, sm_ref, cst_ref, cw_ref, cb_ref, avec_ref, dtb_ref, dvec_ref, nw_ref, h0_ref,
                o_ref, h_ref, ext_ref, nat_ref):
    c = xbc_ref.shape[0]
    assert c == SSD_P, "head pairs share a lane tile: the chunk must be as wide as a head"
    first = pl.program_id(1) == 0

    @pl.when(first)
    def _():
        h_ref[...] = h0_ref[...]

    _conv_load(ext_ref, xbc_ref, cst_ref, first)

    dt_l = jax.nn.softplus(_load_residue(sm_ref, c) + dtb_ref[...])
    tri, _, _ = _tri_masks_residue(c)
    ac_all = _cumsum_rows(dt_l * avec_ref[...], tri)
    ac_t2 = jnp.concatenate([ac_all, ac_all], axis=0).T

    lane = lax.broadcasted_iota(jnp.int32, (c, 2 * SSD_P), 1)
    lo_half = lane < SSD_P
    tri2, _, _ = _tri_masks_residue(c, 2 * SSD_P)
    lane1 = lane[0:1]
    gs_ = range(SSD_G)
    ps_ = range(SSD_HG // 2)

    xs_, bmb, hg, cb2, ch = [], [], [], [], []
    for g in gs_:
        xs_.append(_conv_cols(ext_ref, cw_ref, cb_ref, g * SSD_GW, SSD_GW, c))
        bmb.append(_conv_cols(ext_ref, cw_ref, cb_ref, SSD_INNER + g * SSD_N, SSD_N, c).astype(bf16))
        cmb = _conv_cols(ext_ref, cw_ref, cb_ref, SSD_INNER + SSD_BC + g * SSD_N, SSD_N, c).astype(bf16)
        hg.append(h_ref[pl.ds(g * SSD_HG, SSD_HG)].reshape(SSD_GW, SSD_N))
        cb2.append(_dot_nt(cmb, jnp.concatenate([bmb[g], bmb[g]], axis=0)))
        ch.append(_dot_nt(cmb, hg[g].astype(bf16)))
        yield

    m2, rhs, e_col, xsc = {}, {}, {}, {}
    for g in gs_:
        for p in ps_:
            l0 = SM_DT + g * SSD_HG + 2 * p
            ps = slice(p * 2 * SSD_P, (p + 1) * 2 * SSD_P)
            ac_col = jnp.where(lo_half, ac_all[:, l0:l0 + 1], ac_all[:, l0 + 1:l0 + 2])
            ac_row = jnp.where(lane1 < SSD_P, ac_t2[l0:l0 + 1, :], ac_t2[l0 + 1:l0 + 2, :])
            decay = jnp.exp(jnp.where(tri2, ac_col - ac_row, -jnp.inf))
            m2[g, p] = (cb2[g] * decay).astype(bf16)
            dt2 = jnp.where(lo_half, dt_l[:, l0:l0 + 1], dt_l[:, l0 + 1:l0 + 2])
            xdt = xs_[g][:, ps] * dt2
            rhs[g, p] = jnp.concatenate([jnp.where(lo_half, xdt, 0.0), jnp.where(lo_half, 0.0, xdt)],
                                        axis=0).astype(bf16)
            e_col[g, p] = jnp.exp(ac_col)
            xsc[g, p] = xdt * jnp.exp(ac_col[c - 1:c] - ac_col)
            yield
    y2 = {gp: _dot(m2[gp], rhs[gp]) for gp in m2}
    yield

    for g in gs_:
        xsg = jnp.concatenate([xsc[g, p] for p in ps_], axis=1).astype(bf16)
        dh = _dot_tn(xsg, bmb[g])
        for j in range(SSD_HG):
            ln = SM_DT + g * SSD_HG + j
            h_ref[g * SSD_HG + j] = (hg[g][j * SSD_P:(j + 1) * SSD_P] * jnp.exp(ac_all[c - 1:c, ln:ln + 1])
                                     + dh[j * SSD_P:(j + 1) * SSD_P])
        yield
    for g in gs_:
        gs = slice(g * SSD_GW, (g + 1) * SSD_GW)
        y = jnp.concatenate([y2[g, p] + ch[g][:, p * 2 * SSD_P:(p + 1) * 2 * SSD_P] * e_col[g, p] for p in ps_],
                            axis=1)
        _store_natural(nat_ref, y + dvec_ref[:, gs] * xs_[g], c, g * SSD_GW)
        y = _read_natural(nat_ref, g * SSD_GW, SSD_GW) * _silu(z_ref[:, gs].astype(f32))
        o_ref[:, gs] = _rms(y, nw_ref[:, gs]).astype(bf16)
        yield

    _conv_carry(ext_ref, c)


STEP_B = 8


def _conv_step(x_ref, c_ref, w_ref):
    w = w_ref[...]
    return c_ref[0] * w[0:1] + c_ref[1] * w[1:2] + c_ref[2] * w[2:3] + x_ref[...] * w[3:4]


def _expand_matrix(nb, width):
    r = lax.broadcasted_iota(jnp.int32, (nb, nb * width), 0)
    c = lax.broadcasted_iota(jnp.int32, (nb, nb * width), 1)
    return jnp.where((c >= r * width) & (c < (r + 1) * width), 1.0, 0.0).astype(bf16)


def _bcast_cols(x, e_mat):
    hi, lo = _split(x)
    return _dot_tn(hi, e_mat) + _dot_tn(lo, e_mat)


def _state_call(kernel, grid, in_specs, out_specs, out_shape, scratch, sem, name, args, prev_state):
    aliases = {}
    if prev_state is not None:
        in_specs = in_specs + [pl.BlockSpec(memory_space=pl.ANY)]
        args = args + (prev_state,)
        aliases = {len(args) - 1: len(out_shape) - 1}
        kernel = functools.partial(_drop_alias_ref, kernel, len(args) - 1)
    return pl.pallas_call(kernel, grid=grid, in_specs=in_specs, out_specs=out_specs, out_shape=out_shape,
                          scratch_shapes=scratch, input_output_aliases=aliases,
                          compiler_params=_cp(sem), name=name)(*args)


def _drop_alias_ref(kernel, idx, *refs):
    return kernel(*refs[:idx], *refs[idx + 1:])


def _gdn_step_kernel(alog_ref, dtb_ref, q_ref, k_ref, v_ref, z_ref, sm_ref, cq_ref, ck_ref, cv_ref,
                     wq_ref, wk_ref, wv_ref, nw_ref, s_ref, o_ref, so_ref):
    nb = q_ref.shape[0]
    qa = _silu(_conv_step(q_ref, cq_ref, wq_ref))
    ka = _silu(_conv_step(k_ref, ck_ref, wk_ref))
    va = _silu(_conv_step(v_ref, cv_ref, wv_ref))
    sm = sm_ref[...]
    z = z_ref[...]
    e_mat = _expand_matrix(nb, GDN_DV)
    for h in range(GDN_H):
        sl = slice(h * GDN_DK, (h + 1) * GDN_DK)
        q = qa[:, sl]
        k = ka[:, sl]
        v = va[:, sl]
        q = q * lax.rsqrt(jnp.sum(q * q, axis=-1, keepdims=True) + EPS) * (GDN_DK ** -0.5)
        k = k * lax.rsqrt(jnp.sum(k * k, axis=-1, keepdims=True) + EPS)
        beta = _sigmoid(sm[:, SM_BETA + h:SM_BETA + h + 1])
        g = -jnp.exp(_scalar_vec(alog_ref[h])) * jax.nn.softplus(sm[:, SM_DEC + h:SM_DEC + h + 1] + dtb_ref[h])
        eg = jnp.exp(g)
        qk = jnp.sum(q * k, axis=-1, keepdims=True)
        kb = _bcast_cols(k, e_mat)
        qb = _bcast_cols(q, e_mat)
        blk = lambda m, b: m[:, b * GDN_DV:(b + 1) * GDN_DV]
        ks = jnp.concatenate([jnp.sum(s_ref[b, h] * blk(kb, b), axis=0, keepdims=True) for b in range(nb)], axis=0)
        qs = jnp.concatenate([jnp.sum(s_ref[b, h] * blk(qb, b), axis=0, keepdims=True) for b in range(nb)], axis=0)
        w = beta * v - (beta * eg) * ks
        o = qs * eg + qk * w
        for b in range(nb):
            so_ref[b, h] = s_ref[b, h] * eg[b:b + 1] + blk(kb, b) * w[b:b + 1]
        o = _rms(o, nw_ref[...]) * _silu(z[:, sl])
        o_ref[:, sl] = o.astype(bf16)


def _gdn_step(proj, small, conv_t, states, prev, l, conv_w, a_log, dt_bias, norm_w):
    n = proj.shape[0]
    nb = STEP_B
    blk = lambda off: pl.BlockSpec((nb, GDN_QK), lambda i: (i, _blk(off, GDN_QK)))
    cst = lambda j: pl.BlockSpec((CONV_W - 1, nb, GDN_QK), lambda i: (0, i, j))
    cw = lambda j: pl.BlockSpec((CONV_W, GDN_QK), lambda i: (0, j))
    smem = pl.BlockSpec(memory_space=pltpu.SMEM)
    st = pl.BlockSpec((None, nb, GDN_H, GDN_DK, GDN_DV), lambda i: (l, i, 0, 0, 0))
    return _state_call(
        _gdn_step_kernel, (n // nb,),
        [smem, smem, blk(P_QKV_A), blk(P_QKV_A + GDN_QK), blk(P_QKV_A + 2 * GDN_QK), blk(P_Z_A),
         pl.BlockSpec((nb, LANE), lambda i: (i, 0)),
         cst(0), cst(1), cst(2), cw(0), cw(1), cw(2),
         pl.BlockSpec((1, GDN_DV), lambda i: (0, 0)), st],
        [pl.BlockSpec((nb, GDN_V), lambda i: (i, 0)), st],
        [jax.ShapeDtypeStruct((n, GDN_V), bf16), jax.ShapeDtypeStruct(states.shape, f32)],
        [], ("parallel",), "gdn_step",
        (a_log, dt_bias, proj, proj, proj, proj, small, conv_t, conv_t, conv_t, conv_w, conv_w, conv_w, norm_w, states),
        prev)


def _gla_step_kernel(q_ref, k_ref, v_ref, r_ref, sm_ref, wgate_ref, bgate_ref, nw_ref, s_ref,
                     o_ref, so_ref):
    nb = q_ref.shape[0]
    sm = sm_ref[...]
    lane = lax.broadcasted_iota(jnp.int32, sm.shape, 1)
    lr = jnp.where((lane >= SM_LR) & (lane < SM_LR + GLA_RANK), sm, 0.0).astype(bf16)
    la_all = jax.nn.log_sigmoid(_dot(lr, wgate_ref[...]) + bgate_ref[...]) / GLA_TAU
    qa = q_ref[...] * (GLA_DK ** -0.5)
    ka = k_ref[...]
    va = v_ref[...]
    r = r_ref[...]
    for h in range(GLA_H):
        sk = slice(h * GLA_DK, (h + 1) * GLA_DK)
        sv = slice(h * GLA_DV, (h + 1) * GLA_DV)
        q = qa[:, sk]
        k = ka[:, sk]
        v = va[:, sv]
        e = jnp.exp(la_all[:, sk])
        qk = jnp.sum(q * k, axis=-1, keepdims=True)
        e_t = e.T
        k_t = k.T
        qe_t = (q * e).T
        rows = []
        for b in range(nb):
            s = s_ref[b, h]
            vrow = v[b:b + 1]
            rows.append(qk[b:b + 1] * vrow + jnp.sum(s * qe_t[:, b:b + 1], axis=0, keepdims=True))
            so_ref[b, h] = s * e_t[:, b:b + 1] + k_t[:, b:b + 1] * vrow
        o = jnp.concatenate(rows, axis=0)
        o = _rms(o, nw_ref[...]) * _silu(r[:, sv])
        o_ref[:, sv] = o.astype(bf16)


def _gla_step(proj, small, states, prev, l, wgate_pad, bgate, norm_w):
    n = proj.shape[0]
    nb = STEP_B
    st = pl.BlockSpec((None, nb, GLA_H, GLA_DK, GLA_DV), lambda i: (l, i, 0, 0, 0))
    return _state_call(
        _gla_step_kernel, (n // nb,),
        [pl.BlockSpec((nb, GLA_QK), lambda i: (i, _blk(P_Q_B, GLA_QK))),
         pl.BlockSpec((nb, GLA_QK), lambda i: (i, _blk(P_K_B, GLA_QK))),
         pl.BlockSpec((nb, GLA_V), lambda i: (i, _blk(P_V_B, GLA_V))),
         pl.BlockSpec((nb, GLA_V), lambda i: (i, _blk(P_R_B, GLA_V))),
         pl.BlockSpec((nb, LANE), lambda i: (i, 0)),
         pl.BlockSpec((LANE, GLA_QK), lambda i: (0, 0)),
         pl.BlockSpec((1, GLA_QK), lambda i: (0, 0)),
         pl.BlockSpec((1, GLA_DV), lambda i: (0, 0)), st],
        [pl.BlockSpec((nb, GLA_V), lambda i: (i, 0)), st],
        [jax.ShapeDtypeStruct((n, GLA_V), bf16), jax.ShapeDtypeStruct(states.shape, f32)],
        [], ("parallel",), "gla_step",
        (proj, proj, proj, proj, small, wgate_pad, bgate, norm_w, states),
        prev)


def _ssd_step_kernel(alog_ref, dtb_ref, z_ref, xbc_ref, sm_ref, cst_ref, cw_ref, cb_ref, dvec_ref, nw_ref, h_ref,
                     o_ref, ho_ref):
    nb = xbc_ref.shape[0]
    xbc = _silu(_conv_step(xbc_ref, cst_ref, cw_ref) + cb_ref[...])
    sm_t = sm_ref[...].T
    e_mat = _expand_matrix(nb, SSD_N)
    lane_b = lax.broadcasted_iota(jnp.int32, (SSD_P, nb), 1)
    for g in range(SSD_G):
        gs = slice(g * SSD_GW, (g + 1) * SSD_GW)
        bm = xbc[:, SSD_INNER + g * SSD_N:SSD_INNER + (g + 1) * SSD_N]
        cm_t = xbc[:, SSD_INNER + SSD_BC + g * SSD_N:SSD_INNER + SSD_BC + (g + 1) * SSD_N].T
        cb_row = jnp.sum(cm_t * bm.T, axis=0, keepdims=True)
        hg = h_ref[:, pl.ds(g * SSD_HG, SSD_HG)].reshape(nb * SSD_GW, SSD_N)
        ch = _dot(hg.astype(bf16), cm_t.astype(bf16))
        pairs = []
        for jp in range(SSD_HG // 2):
            x_t = xbc[:, g * SSD_GW + jp * LANE:g * SSD_GW + (jp + 1) * LANE].T
            halves = []
            for jj in range(2):
                j = 2 * jp + jj
                hd = g * SSD_HG + j
                dt_row = jax.nn.softplus(sm_t[SM_DT + hd:SM_DT + hd + 1, :] + dtb_ref[hd])
                ea_row = jnp.exp(dt_row * (-jnp.exp(_scalar_vec(alog_ref[hd]))))
                xdt_t = x_t[jj * SSD_P:(jj + 1) * SSD_P] * dt_row
                hi, lo = _split(xdt_t)
                xb = _dot(hi, e_mat) + _dot(lo, e_mat)
                yh = jnp.zeros((SSD_P, nb), f32)
                for b in range(nb):
                    r0 = (b * SSD_HG + j) * SSD_P
                    yh = jnp.where(lane_b == b, ch[r0:r0 + SSD_P], yh)
                    ho_ref[b, hd] = (h_ref[b, hd] * ea_row[:, b:b + 1]
                                     + xb[:, b * SSD_N:(b + 1) * SSD_N] * bm[b:b + 1])
                halves.append(yh * ea_row + cb_row * xdt_t)
            pairs.append(jnp.concatenate(halves, axis=0).T)
        y = jnp.concatenate(pairs, axis=1)
        y = (y + dvec_ref[:, gs] * xbc[:, gs]) * _silu(z_ref[:, gs])
        o_ref[:, gs] = _rms(y, nw_ref[:, gs]).astype(bf16)


def _ssd_step(proj, small, conv_t, states, prev, l, conv_w, conv_b, a_log, dt_bias, dvec, norm_w):
    n = proj.shape[0]
    nb = STEP_B
    smem = pl.BlockSpec(memory_space=pltpu.SMEM)
    const = lambda shape: pl.BlockSpec(shape, lambda i: (0,) * len(shape))
    st = pl.BlockSpec((None, nb, SSD_H, SSD_P, SSD_N), lambda i: (l, i, 0, 0, 0))
    return _state_call(
        _ssd_step_kernel, (n // nb,),
        [smem, smem,
         pl.BlockSpec((nb, SSD_INNER), lambda i: (i, _blk(P_Z_C, SSD_INNER))),
         pl.BlockSpec((nb, SSD_CONV), lambda i: (i, _blk(P_XBC, SSD_CONV))),
         pl.BlockSpec((nb, LANE), lambda i: (i, 0)),
         pl.BlockSpec((CONV_W - 1, nb, SSD_CONV), lambda i: (0, i, 0)),
         const((CONV_W, SSD_CONV)), const((1, SSD_CONV)), const((1, SSD_INNER)), const((1, SSD_INNER)), st],
        [pl.BlockSpec((nb, SSD_INNER), lambda i: (i, 0)), st],
        [jax.ShapeDtypeStruct((n, SSD_INNER), bf16), jax.ShapeDtypeStruct(states.shape, f32)],
        [], ("parallel",), "ssd_step",
        (a_log, dt_bias, proj, proj, small, conv_t, conv_w, conv_b, dvec, norm_w, states),
        prev)


def _merge_kernel(oa_ref, ob_ref, oc_ref, ga_ref, gb_ref, gc_ref, wa_ref, wb_ref, wc_ref, o_ref):
    m = (_sigmoid(ga_ref[...].astype(f32)) * _dot(oa_ref[...], wa_ref[...])
         + _sigmoid(gb_ref[...].astype(f32)) * _dot(ob_ref[...], wb_ref[...])
         + _sigmoid(gc_ref[...].astype(f32)) * _dot(oc_ref[...], wc_ref[...]))
    o_ref[...] = m.astype(bf16)


def _outproj_kernel(m_ref, x_ref, gt_ref, w_ref, o_ref):
    o_ref[...] = x_ref[...] + gt_ref[...] * _dot(m_ref[...], w_ref[...])


def _mixout(oa, ob, oc, proj, x, mod, wa, wb, wc, wo, l, per_row, tm):
    bx, tx, dm = x.shape
    tn = 512
    gate = lambda k: pl.BlockSpec((None, tm, tn), lambda b, i, j: (b, i, _blk(P_GATES + k * D_MODEL, tn) + j))
    act = lambda w: pl.BlockSpec((None, tm, w), lambda b, i, j: (b, i, 0))
    wsp = lambda w: pl.BlockSpec((None, w, tn), lambda b, i, j: (l, 0, j))
    merged = pl.pallas_call(
        _merge_kernel,
        grid=(bx, tx // tm, dm // tn),
        in_specs=[act(GDN_V), act(GLA_V), act(SSD_INNER), gate(0), gate(1), gate(2),
                  wsp(GDN_V), wsp(GLA_V), wsp(SSD_INNER)],
        out_specs=pl.BlockSpec((None, tm, tn), lambda b, i, j: (b, i, j)),
        out_shape=jax.ShapeDtypeStruct((bx, tx, dm), bf16),
        compiler_params=_cp(("parallel", "parallel", "arbitrary")),
        name="merge",
    )(oa, ob, oc, proj, proj, proj, wa, wb, wc)
    to = 1024
    nj = dm // to
    if per_row:
        gspec = pl.BlockSpec((None, tm, to), lambda b, i, j: (b, i, 5 * nj + j))
    else:
        gspec = pl.BlockSpec((None, 1, to), lambda b, i, j: (b, 0, 5 * nj + j))
    return pl.pallas_call(
        _outproj_kernel,
        grid=(bx, tx // tm, nj),
        in_specs=[pl.BlockSpec((None, tm, dm), lambda b, i, j: (b, i, 0)),
                  pl.BlockSpec((None, tm, to), lambda b, i, j: (b, i, j)),
                  gspec,
                  pl.BlockSpec((None, dm, to), lambda b, i, j: (l, 0, j))],
        out_specs=pl.BlockSpec((None, tm, to), lambda b, i, j: (b, i, j)),
        out_shape=jax.ShapeDtypeStruct(x.shape, f32),
        compiler_params=_cp(("parallel", "parallel", "arbitrary")),
        name="outproj",
    )(merged, x, mod, wo)


REGROUP_ROWS = 512


def _regroup_kernel(w_ref, o_ref):
    o_ref[...] = w_ref[0].astype(bf16)


def _regroup_rows(wt):
    nl, _, d = wt.shape
    tr = REGROUP_ROWS
    shifts = []
    for name in MAIN_ORDER:
        a, b = W_IN_SRC[name]
        assert (b - a) % tr == 0 and a % SUB == 0
        shifts.append((P_MAIN[name] // tr, a - P_MAIN[name]))

    def src_row(i):
        off = i * (tr // SUB) + shifts[0][1] // SUB
        for k in range(1, len(shifts)):
            off = off + jnp.where(i >= shifts[k][0], (shifts[k][1] - shifts[k - 1][1]) // SUB, 0)
        return off * SUB

    return pl.pallas_call(
        _regroup_kernel,
        grid=(nl, P_TOTAL // tr),
        in_specs=[pl.BlockSpec((pl.Element(1), pl.Element(tr), pl.Element(d)), lambda l, i: (l, src_row(i), 0))],
        out_specs=pl.BlockSpec((None, tr, d), lambda l, i: (l, i, 0)),
        out_shape=jax.ShapeDtypeStruct((nl, P_TOTAL, d), bf16),
        compiler_params=_cp(("parallel", "parallel")),
        name="regroup_w_in",
    )(wt)


def _permute_w_in(w):
    nl, d, _ = w.shape
    main = _regroup_rows(jnp.swapaxes(w, 1, 2))
    small = jnp.concatenate([w[:, :, W_IN_SRC[name][0]:W_IN_SRC[name][1]] for name in SMALL_ORDER], axis=2)
    small = jnp.pad(jnp.swapaxes(small, 1, 2), ((0, 0), (0, LANE - small.shape[2]), (0, 0)))
    return main, small


def _lane_vec(v, lo):
    return jnp.zeros((1, LANE), f32).at[0, lo:lo + v.shape[0]].set(v)


def _layer_params(l, p):
    row = lambda a: a[l].reshape(1, -1)
    wgate = jnp.zeros((LANE, GLA_QK), f32).at[SM_LR:SM_LR + GLA_RANK].set(p["gla_w_gate"][l]).astype(bf16)
    return dict(
        norm1=row(p["norm1"]), norm2=row(p["norm2"]), norm3=row(p["norm3"]),
        gdn_conv_w=p["gdn_conv_w"][l], gdn_a_log=p["gdn_a_log"][l], gdn_dt_bias=p["gdn_dt_bias"][l],
        gdn_alog_l=_lane_vec(p["gdn_a_log"][l], SM_DEC), gdn_dtb_l=_lane_vec(p["gdn_dt_bias"][l], SM_DEC),
        gdn_norm_w=row(p["gdn_norm_w"]),
        gla_wgate=wgate, gla_bgate=row(p["gla_b_gate"]), gla_norm_w=row(p["gla_norm_w"]),
        ssd_conv_w=p["ssd_conv_w"][l], ssd_conv_b=row(p["ssd_conv_b"]), ssd_a_log=p["ssd_a_log"][l],
        ssd_dt_bias=p["ssd_dt_bias"][l],
        ssd_avec_l=_lane_vec(-jnp.exp(p["ssd_a_log"][l]), SM_DT), ssd_dtb_l=_lane_vec(p["ssd_dt_bias"][l], SM_DT),
        ssd_dvec=jnp.repeat(p["ssd_d"][l], SSD_P).reshape(1, -1),
        ssd_norm_w=row(p["ssd_norm_w"]),
    )


def _stacked_weights(p):
    w_in, w_in_small = _permute_w_in(p["w_in"])
    return dict(
        f1=(p["ffn1_wg"], p["ffn1_wu"], p["ffn1_wd"]),
        f2=(p["ffn2_wg"], p["ffn2_wu"], p["ffn2_wd"]),
        w_in=w_in, w_in_small=w_in_small,
        wa=p["w_branch_gdn"].astype(bf16), wb=p["w_branch_gla"].astype(bf16),
        wc=p["w_branch_ssd"].astype(bf16), w_out=p["w_out"].astype(bf16),
    )


def _new_conv_state(buf, raw):
    t = raw.shape[1]
    k = CONV_W - 1
    if t >= k:
        return raw[:, t - k:]
    return jnp.concatenate([buf[:, t:], raw], axis=1)


def _mixer_prompt(proj, small, lp, st):
    gdn_conv, s_gdn, s_gla, ssd_conv, s_ssd = st
    b, t, _ = proj.shape
    c = CHUNK
    sm_spec = pl.BlockSpec((None, c, LANE), lambda bi, ti: (bi, ti, 0))
    st_a, st_b, st_c = _per_seq(GDN_H, GDN_DK, GDN_DV), _per_seq(GLA_H, GLA_DK, GLA_DV), _per_seq(SSD_H, SSD_P, SSD_N)
    gdn_specs = [_tok(GDN_CONV, P_QKV_A), _tok(GDN_V, P_Z_A), sm_spec, _per_seq(CONV_W - 1, GDN_CONV),
                 _const(CONV_W, GDN_CONV), _const(1, LANE), _const(1, LANE), _const(1, GDN_DV), st_a]
    gdn_args = (proj, proj, small, gdn_conv, lp["gdn_conv_w"], lp["gdn_alog_l"], lp["gdn_dtb_l"],
                lp["gdn_norm_w"], s_gdn)
    gla_specs = [_tok(GLA_QK, P_Q_B), _tok(GLA_QK, P_K_B), _tok(GLA_V, P_V_B), _tok(GLA_V, P_R_B), sm_spec,
                 _const(LANE, GLA_QK), _const(1, GLA_QK), _const(1, GLA_DV), st_b]
    gla_args = (proj, proj, proj, proj, small, lp["gla_wgate"], lp["gla_bgate"], lp["gla_norm_w"], s_gla)
    ssd_specs = [_tok(SSD_INNER, P_Z_C), _tok(SSD_CONV, P_XBC), sm_spec, _per_seq(CONV_W - 1, SSD_CONV),
                 _const(CONV_W, SSD_CONV), _const(1, SSD_CONV), _const(1, LANE), _const(1, LANE),
                 _const(1, SSD_INNER), _const(1, SSD_INNER), st_c]
    ssd_args = (proj, proj, small, ssd_conv, lp["ssd_conv_w"], lp["ssd_conv_b"], lp["ssd_avec_l"],
                lp["ssd_dtb_l"], lp["ssd_dvec"], lp["ssd_norm_w"], s_ssd)
    assert (len(gdn_specs), len(gla_specs), len(ssd_specs)) == (N_GDN_IN, N_GLA_IN, N_SSD_IN)
    out_tok = lambda width: pl.BlockSpec((None, c, width), lambda bi, ti: (bi, ti, 0))
    oa, s_gdn_n, ob, s_gla_n, oc, s_ssd_n = pl.pallas_call(
        _mixers_kernel,
        grid=(b, t // c),
        in_specs=gdn_specs + gla_specs + ssd_specs,
        out_specs=[out_tok(GDN_V), st_a, out_tok(GLA_V), st_b, out_tok(SSD_INNER), st_c],
        out_shape=[jax.ShapeDtypeStruct((b, t, GDN_V), bf16), jax.ShapeDtypeStruct(s_gdn.shape, f32),
                   jax.ShapeDtypeStruct((b, t, GLA_V), bf16), jax.ShapeDtypeStruct(s_gla.shape, f32),
                   jax.ShapeDtypeStruct((b, t, SSD_INNER), bf16), jax.ShapeDtypeStruct(s_ssd.shape, f32)],
        scratch_shapes=[pltpu.VMEM((GDN_CONV // LANE, 8 + c, LANE), f32), pltpu.VMEM((GLA_H, GLA_DV, GLA_DK), f32),
                        pltpu.VMEM((SSD_CONV // LANE, 8 + c, LANE), f32),
                        pltpu.VMEM((GDN_V // LANE, c, LANE), f32), pltpu.VMEM((SSD_INNER // LANE, c, LANE), f32)],
        compiler_params=_cp(("parallel", "arbitrary")),
        name="mixers_prompt",
    )(*gdn_args, *gla_args, *ssd_args)
    gdn_conv_n = _new_conv_state(gdn_conv, proj[:, :, P_QKV_A:P_QKV_A + GDN_CONV]).astype(f32)
    ssd_conv_n = _new_conv_state(ssd_conv, proj[:, :, P_XBC:P_XBC + SSD_CONV]).astype(f32)
    return (oa, ob, oc), (gdn_conv_n, s_gdn_n, s_gla_n, ssd_conv_n, s_ssd_n)


def _mixer_sample(proj, small, lp, l, states, prev):
    n = proj.shape[1]
    p2 = proj.reshape(n, P_TOTAL)
    s2 = small.reshape(n, LANE)
    gdn_conv, ssd_conv = states[0][l], states[3][l]
    gct = jnp.swapaxes(gdn_conv, 0, 1)
    sct = jnp.swapaxes(ssd_conv, 0, 1)
    pv = (None,) * 5 if prev is None else prev
    oa, s_gdn_n = _gdn_step(p2, s2, gct, states[1], pv[1], l, lp["gdn_conv_w"], lp["gdn_a_log"],
                            lp["gdn_dt_bias"], lp["gdn_norm_w"])
    ob, s_gla_n = _gla_step(p2, s2, states[2], pv[2], l, lp["gla_wgate"], lp["gla_bgate"], lp["gla_norm_w"])
    oc, s_ssd_n = _ssd_step(p2, s2, sct, states[4], pv[4], l, lp["ssd_conv_w"], lp["ssd_conv_b"], lp["ssd_a_log"],
                            lp["ssd_dt_bias"], lp["ssd_dvec"], lp["ssd_norm_w"])
    raw = p2.reshape(n, 1, P_TOTAL)
    gdn_conv_n = _new_conv_state(gdn_conv, raw[:, :, P_QKV_A:P_QKV_A + GDN_CONV])
    ssd_conv_n = _new_conv_state(ssd_conv, raw[:, :, P_XBC:P_XBC + SSD_CONV])
    outs = tuple(o.reshape(1, n, -1) for o in (oa, ob, oc))
    return outs, (gdn_conv_n, s_gdn_n, s_gla_n, ssd_conv_n, s_ssd_n)


def _trunk(x, mods, lps, sw, ffn_w, states, per_row, tm, tm_in, final_w):
    nl = len(lps)
    per_layer = []
    prev = None

    def ffn(x, mod, k0, nw, name, l, fw, final):
        if per_row:
            x, ffn_w[l, name] = _ffn_cast(x, mod, k0, nw, *sw[name], l, fw, final)
            return x
        return _ffn(x, mod, k0, nw, ffn_w[l, name], fw, per_row, tm, final)

    for l in range(nl):
        lp, mod = lps[l], mods[l]
        last = l == nl - 1
        x = ffn(x, mod, 0, lp["norm1"], "f1", l, lp["norm1"], False)
        proj, small = _inproj(x, mod, lp["norm2"], sw["w_in"], sw["w_in_small"], l, per_row, tm_in,
                              f32 if per_row else bf16)
        if per_row:
            (oa, ob, oc), st = _mixer_sample(proj, small, lp, l, states, prev)
            prev = st
        else:
            (oa, ob, oc), st = _mixer_prompt(proj, small, lp, tuple(s[l] for s in states))
        per_layer.append(st)
        x = _mixout(oa, ob, oc, proj, x, mod, sw["wa"], sw["wb"], sw["wc"], sw["w_out"], l, per_row, tm_in)
        x = ffn(x, mod, 6, lp["norm3"], "f2", l, final_w if last else lp["norm3"], last)
    stack = lambda i: jnp.stack([st[i] for st in per_layer])
    if per_row:
        new_states = (stack(0), prev[1], prev[2], stack(3), prev[4])
    else:
        new_states = tuple(stack(i) for i in range(5))
    return x, new_states


def kernel(x_prompt, x_sample, state_gdn_conv, state_gdn, state_gla, state_ssd_conv, state_ssd, c_prompt, c_sample, w_ada, b_ada, norm1, norm2, norm3, ffn1_wg, ffn1_wu, ffn1_wd, ffn2_wg, ffn2_wu, ffn2_wd, w_in, gdn_conv_w, gdn_a_log, gdn_dt_bias, gdn_norm_w, gla_w_gate, gla_b_gate, gla_norm_w, ssd_conv_w, ssd_conv_b, ssd_a_log, ssd_dt_bias, ssd_d, ssd_norm_w, w_branch_gdn, w_branch_gla, w_branch_ssd, w_out, final_norm):
    p = dict(norm1=norm1, norm2=norm2, norm3=norm3,
             ffn1_wg=ffn1_wg, ffn1_wu=ffn1_wu, ffn1_wd=ffn1_wd, ffn2_wg=ffn2_wg, ffn2_wu=ffn2_wu, ffn2_wd=ffn2_wd,
             w_in=w_in, gdn_conv_w=gdn_conv_w, gdn_a_log=gdn_a_log, gdn_dt_bias=gdn_dt_bias, gdn_norm_w=gdn_norm_w,
             gla_w_gate=gla_w_gate, gla_b_gate=gla_b_gate, gla_norm_w=gla_norm_w,
             ssd_conv_w=ssd_conv_w, ssd_conv_b=ssd_conv_b, ssd_a_log=ssd_a_log, ssd_dt_bias=ssd_dt_bias,
             ssd_d=ssd_d, ssd_norm_w=ssd_norm_w,
             w_branch_gdn=w_branch_gdn, w_branch_gla=w_branch_gla, w_branch_ssd=w_branch_ssd, w_out=w_out)
    nl = w_ada.shape[0]
    bp, tp, dm = x_prompt.shape
    bs = x_sample.shape[0]
    assert x_sample.shape[1] == 1 and tp % CHUNK == 0 and bs % STEP_B == 0 and dm == D_MODEL
    lps = [_layer_params(l, p) for l in range(nl)]
    sw = _stacked_weights(p)
    fw = final_norm.reshape(1, dm)

    rows = bp + bs
    rpad = -(-rows // 8) * 8
    c_all = jnp.concatenate([c_prompt, c_sample, jnp.zeros((rpad - rows, dm), f32)], axis=0)
    mod = _ada_mod(c_all, w_ada, b_ada)
    mod_p = [mod[l, :bp].reshape(bp, 1, N_MOD * dm) for l in range(nl)]
    mod_s = [mod[l, bp:rows].reshape(1, bs, N_MOD * dm) for l in range(nl)]

    sample_states = (state_gdn_conv, state_gdn, state_gla, state_ssd_conv, state_ssd)
    prompt_states = tuple(jnp.zeros((s.shape[0], bp) + s.shape[2:], x_prompt.dtype) for s in sample_states)
    tm_p = 512 if tp % 512 == 0 else CHUNK
    tm_in = 1024 if tp % 1024 == 0 else tm_p
    ffn_w = {}
    y_s, st_s = _trunk(x_sample.reshape(1, bs, dm), mod_s, lps, sw, ffn_w, sample_states, True, bs, bs, fw)
    y_p, st_p = _trunk(x_prompt, mod_p, lps, sw, ffn_w, prompt_states, False, tm_p, tm_in, fw)
    return (y_p, y_s.reshape(bs, 1, dm)) + st_p + st_s
```

```python
import functools

import jax
import jax.numpy as jnp
from jax import lax
from jax.experimental import pallas as pl
from jax.experimental.pallas import tpu as pltpu

f32 = jnp.float32
bf16 = jnp.bfloat16
HI = lax.Precision.HIGHEST

EPS = 1e-6
D_MODEL = 2048
N_MOD = 9
CHUNK = 64
CONV_W = 4
GDN_H, GDN_DK, GDN_DV = 8, 128, 128
GLA_H, GLA_DK, GLA_DV, GLA_RANK, GLA_TAU = 4, 128, 256, 16, 16.0
SSD_H, SSD_P, SSD_G, SSD_N = 32, 64, 4, 128
SSD_HG = SSD_H // SSD_G
GDN_QK = GDN_H * GDN_DK
GDN_V = GDN_H * GDN_DV
GDN_CONV = 2 * GDN_QK + GDN_V
GLA_QK = GLA_H * GLA_DK
GLA_V = GLA_H * GLA_DV
SSD_INNER = SSD_H * SSD_P
SSD_BC = SSD_G * SSD_N
SSD_CONV = SSD_INNER + 2 * SSD_BC
SSD_GW = SSD_HG * SSD_P

LANE = 128
FF_TILE = 512
FF_SUB = 256

IN_SPLITS = (("qkv_a", GDN_CONV), ("z_a", GDN_V), ("beta", GDN_H), ("dec", GDN_H),
             ("q_b", GLA_QK), ("k_b", GLA_QK), ("v_b", GLA_V), ("lr", GLA_RANK), ("r_b", GLA_V),
             ("z_c", SSD_INNER), ("xbc", SSD_CONV), ("dt", SSD_H), ("gates", 3 * D_MODEL))
MAIN_ORDER = ("qkv_a", "xbc", "z_c", "gates", "z_a", "q_b", "k_b", "v_b", "r_b")
SMALL_ORDER = ("beta", "dec", "lr", "dt")


def _layout():
    src, off = {}, 0
    for name, w in IN_SPLITS:
        src[name] = (off, off + w)
        off += w
    main, small, d = {}, {}, 0
    for name in MAIN_ORDER:
        w = src[name][1] - src[name][0]
        main[name] = d
        d += w
    total = d
    d = 0
    for name in SMALL_ORDER:
        small[name] = d
        d += src[name][1] - src[name][0]
    assert d <= LANE
    return src, main, small, total


W_IN_SRC, P_MAIN, P_SM, P_TOTAL = _layout()
P_QKV_A, P_XBC, P_Z_C, P_GATES, P_Z_A = (P_MAIN[k] for k in ("qkv_a", "xbc", "z_c", "gates", "z_a"))
P_Q_B, P_K_B, P_V_B, P_R_B = (P_MAIN[k] for k in ("q_b", "k_b", "v_b", "r_b"))
SM_BETA, SM_DEC, SM_LR, SM_DT = (P_SM[k] for k in SMALL_ORDER)

VMEM_LIMIT = 56 * 1024 * 1024


def _cp(sem):
    return pltpu.CompilerParams(dimension_semantics=sem, vmem_limit_bytes=VMEM_LIMIT)


def _blk(off, width):
    assert off % width == 0, (off, width)
    return off // width


def _sigmoid(x):
    return 0.5 + 0.5 * jnp.tanh(0.5 * x)


def _silu(x):
    h = 0.5 * x
    return h + h * jnp.tanh(h)


def _rms(x, w):
    return x * lax.rsqrt(jnp.mean(x * x, axis=-1, keepdims=True) + EPS) * w


def _dot(a, b):
    return jnp.dot(a, b, preferred_element_type=f32)


def _dot_nt(a, b):
    return lax.dot_general(a, b, (((1,), (1,)), ((), ())), preferred_element_type=f32)


def _dot_tn(a, b):
    return lax.dot_general(a, b, (((0,), (0,)), ((), ())), preferred_element_type=f32)


def _split(x):
    hi = x.astype(bf16)
    return hi, (x - hi.astype(f32)).astype(bf16)


def _dot3(a, b):
    return _dot(a[0], b[0]) + (_dot(a[0], b[1]) + _dot(a[1], b[0]))


def _dot_hi(a, b):
    return jnp.dot(a, b, precision=HI, preferred_element_type=f32)


def _tri_masks(c):
    row = lax.broadcasted_iota(jnp.int32, (c, c), 0)
    col = lax.broadcasted_iota(jnp.int32, (c, c), 1)
    return row >= col, row > col, row == col


SUB = 8


def _row_time(idx, c):
    n = c // SUB
    assert n & (n - 1) == 0, "chunk / SUB must be a power of two"
    sh = n.bit_length() - 1
    return ((idx & (n - 1)) << 3) | (idx >> sh)


def _tri_masks_residue(c, cols=None):
    cols = c if cols is None else cols
    row = _row_time(lax.broadcasted_iota(jnp.int32, (c, cols), 0), c)
    col = _row_time(lax.broadcasted_iota(jnp.int32, (c, cols), 1) & (c - 1), c)
    return row >= col, row > col, row == col


def _load_residue(ref, c, sl=slice(None)):
    n = c // SUB
    return jnp.concatenate([ref[pl.ds(r, n, stride=SUB), sl] for r in range(SUB)], axis=0)


def _store_natural(nat_ref, x, c, lo):
    n = c // SUB
    for j in range(x.shape[1] // LANE):
        for r in range(SUB):
            nat_ref[lo // LANE + j, pl.ds(r, n, stride=SUB), :] = x[r * n:(r + 1) * n, j * LANE:(j + 1) * LANE]


def _read_natural(nat_ref, lo, width):
    return jnp.concatenate([nat_ref[lo // LANE + j] for j in range(width // LANE)], axis=1)


def _lane_col(x, idx):
    lane = lax.broadcasted_iota(jnp.int32, x.shape, 1)
    return jnp.sum(jnp.where(lane == idx, x, 0.0), axis=1, keepdims=True)


def _scalar_vec(s):
    return jnp.full((1, 1), s, f32)


def _ada_kernel(c_ref, w_ref, b_ref, o_ref):
    s = _silu(c_ref[...]).astype(bf16)
    o_ref[...] = _dot(s, w_ref[...].astype(bf16)) + b_ref[...]


def _ada_mod(c_all, w_ada, b_ada):
    nl, dm, n = w_ada.shape
    r = c_all.shape[0]
    tn = 1024
    return pl.pallas_call(
        _ada_kernel,
        grid=(nl, n // tn),
        in_specs=[pl.BlockSpec((r, dm), lambda l, j: (0, 0)),
                  pl.BlockSpec((None, dm, tn), lambda l, j: (l, 0, j)),
                  pl.BlockSpec((None, 1, tn), lambda l, j: (l, 0, j))],
        out_specs=pl.BlockSpec((None, r, tn), lambda l, j: (l, 0, j)),
        out_shape=jax.ShapeDtypeStruct((nl, r, n), f32),
        compiler_params=_cp(("arbitrary", "arbitrary")),
        name="ada_mod",
    )(c_all, w_ada, b_ada.reshape(nl, 1, n))


def _mod_spec(per_row, tm, chunk):
    if per_row:
        return pl.BlockSpec((None, tm, D_MODEL), lambda b, i, j: (b, i, chunk))
    return pl.BlockSpec((None, 1, D_MODEL), lambda b, i, j: (b, 0, chunk))


def _ffn_prologue(x_ref, sh_ref, sc_ref, nw_ref, h_ref, acc_ref):
    y = _rms(x_ref[...], nw_ref[...])
    h_ref[...] = (y * (1.0 + sc_ref[...]) + sh_ref[...]).astype(bf16)
    acc_ref[...] = jnp.zeros_like(acc_ref)


def _ffn_accumulate(h_ref, acc_ref, wg, wu, wd):
    h = h_ref[...]
    width = wg.shape[1]
    sub = FF_SUB if width % FF_SUB == 0 else width
    parts = [(_dot(h, wg[:, s:s + sub]), _dot(h, wu[:, s:s + sub])) for s in range(0, width, sub)]
    acc = acc_ref[...]
    for i, (g, u) in enumerate(parts):
        acc = acc + _dot((_silu(g) * u).astype(bf16), wd[i * sub:(i + 1) * sub, :])
    acc_ref[...] = acc


def _ffn_epilogue(x_ref, gt_ref, fw_ref, acc_ref, o_ref, final):
    y = x_ref[...] + 0.5 * gt_ref[...] * acc_ref[...]
    if final:
        y = _rms(y, fw_ref[...])
    o_ref[...] = y


FF_NORM_STEPS = 8


def _ffn_kernel(x_ref, xn_ref, sh_ref, sc_ref, shn_ref, scn_ref, gt_ref, nw_ref, fw_ref, wg_ref, wu_ref, wd_ref,
                *rest, final, has_tail, norm_steps, tail_norm):
    if has_tail:
        wgt_ref, wut_ref, wdt_ref, o_ref, h_ref, acc_ref = rest
    else:
        o_ref, h_ref, acc_ref = rest
    n = pl.program_id(0)
    f = pl.program_id(1)
    last = pl.num_programs(1) - 1
    slot = n % 2
    rs = x_ref.shape[0] // norm_steps

    def modnorm(x, sh, sc):
        return (_rms(x, nw_ref[...]) * (1.0 + sc[...]) + sh[...]).astype(bf16)

    @pl.when((n == 0) & (f == 0))
    def _():
        h_ref[0] = modnorm(x_ref[...], sh_ref, sc_ref)

    @pl.when(f == 0)
    def _():
        acc_ref[...] = jnp.zeros_like(acc_ref)

    def step(wg, wu, wd, norm_next=True):
        _ffn_accumulate(h_ref.at[slot], acc_ref, wg, wu, wd)
        if norm_next:
            rows = pl.ds(pl.multiple_of(jnp.minimum(f, norm_steps - 1) * rs, rs), rs)
            h_ref[1 - slot, rows, :] = modnorm(xn_ref[rows, :], shn_ref, scn_ref)

    if has_tail:
        pl.when(f < last)(lambda: step(wg_ref, wu_ref, wd_ref))
        pl.when(f == last)(lambda: step(wgt_ref, wut_ref, wdt_ref, norm_next=tail_norm))
    else:
        step(wg_ref, wu_ref, wd_ref)
    pl.when(f == last)(lambda: _ffn_epilogue(x_ref, gt_ref, fw_ref, acc_ref, o_ref, final))


def _ffn(x, mod, k0, nw, w, fw, tm, final):
    bx, tx, dm = x.shape
    tf = FF_TILE
    n_full = w[0].shape[1] // tf
    has_tail = len(w) > 3
    n_steps = n_full + int(has_tail)
    norm_steps = max(1, min(FF_NORM_STEPS, n_steps, tm // 16))
    assert tm % (norm_steps * 16) == 0
    tpb = tx // tm
    n_tiles = bx * tpb
    full = lambda f: jnp.minimum(f, n_full - 1)
    nxt = lambda n: jnp.minimum(n + 1, n_tiles - 1)
    rows = lambda sel: pl.BlockSpec((None, tm, dm), lambda n, f: (sel(n) // tpb, sel(n) % tpb, 0))
    modc = lambda sel, k: pl.BlockSpec((None, 1, dm), lambda n, f: (sel(n) // tpb, 0, k))
    cur = lambda n: n
    w_specs = [pl.BlockSpec((dm, tf), lambda n, f: (0, full(f))),
               pl.BlockSpec((dm, tf), lambda n, f: (0, full(f))),
               pl.BlockSpec((tf, dm), lambda n, f: (full(f), 0))]
    if has_tail:
        ft = w[3].shape[1]
        w_specs += [pl.BlockSpec((dm, ft), lambda n, f: (0, 0)),
                    pl.BlockSpec((dm, ft), lambda n, f: (0, 0)),
                    pl.BlockSpec((ft, dm), lambda n, f: (0, 0))]
    return pl.pallas_call(
        functools.partial(_ffn_kernel, final=final, has_tail=has_tail, norm_steps=norm_steps,
                          tail_norm=norm_steps > n_full),
        grid=(n_tiles, n_steps),
        in_specs=[rows(cur), rows(nxt), modc(cur, k0), modc(cur, k0 + 1), modc(nxt, k0), modc(nxt, k0 + 1),
                  modc(cur, k0 + 2),
                  pl.BlockSpec((1, dm), lambda n, f: (0, 0)),
                  pl.BlockSpec((1, dm), lambda n, f: (0, 0))] + w_specs,
        out_specs=rows(cur),
        out_shape=jax.ShapeDtypeStruct(x.shape, f32),
        scratch_shapes=[pltpu.VMEM((2, tm, dm), bf16), pltpu.VMEM((tm, dm), f32)],
        compiler_params=_cp(("arbitrary", "arbitrary")),
        name="ffn",
    )(x, x, mod, mod, mod, mod, mod, nw, fw, *w)


FF_CAST_TILE = 256


def _ffn_cast_kernel(x_ref, sh_ref, sc_ref, gt_ref, nw_ref, fw_ref, wg_ref, wu_ref, wd_ref, *rest, final, has_tail):
    if has_tail:
        wgt_ref, wut_ref, wdt_ref, o_ref, wgb_ref, wub_ref, wdb_ref, wgtb_ref, wutb_ref, wdtb_ref, h_ref, acc_ref = rest
    else:
        o_ref, wgb_ref, wub_ref, wdb_ref, h_ref, acc_ref = rest
    f = pl.program_id(2)
    last = pl.num_programs(2) - 1
    pl.when(f == 0)(lambda: _ffn_prologue(x_ref, sh_ref, sc_ref, nw_ref, h_ref, acc_ref))

    def cast_accumulate(src, dst):
        for s_ref, d_ref in zip(src, dst):
            d_ref[...] = s_ref[...].astype(bf16)
        _ffn_accumulate(h_ref, acc_ref, *dst)

    if has_tail:
        pl.when(f < last)(lambda: cast_accumulate((wg_ref, wu_ref, wd_ref), (wgb_ref, wub_ref, wdb_ref)))
        pl.when(f == last)(lambda: cast_accumulate((wgt_ref, wut_ref, wdt_ref), (wgtb_ref, wutb_ref, wdtb_ref)))
    else:
        cast_accumulate((wg_ref, wu_ref, wd_ref), (wgb_ref, wub_ref, wdb_ref))
    pl.when(f == last)(lambda: _ffn_epilogue(x_ref, gt_ref, fw_ref, acc_ref, o_ref, final))


def _ffn_cast(x, mod, k0, nw, wg, wu, wd, l, fw, final):
    bx, tx, dm = x.shape
    assert bx == 1
    f_all = wg.shape[2]
    tf = FF_CAST_TILE
    cut = f_all - f_all % FF_TILE
    n_full = cut // tf
    has_tail = cut < f_all
    full = lambda f: jnp.minimum(f, n_full - 1)
    w_in_specs = [pl.BlockSpec((None, dm, tf), lambda b, i, f: (l, 0, full(f))),
                  pl.BlockSpec((None, dm, tf), lambda b, i, f: (l, 0, full(f))),
                  pl.BlockSpec((None, tf, dm), lambda b, i, f: (l, full(f), 0))]
    w_out_specs = [pl.BlockSpec((dm, tf), lambda b, i, f: (0, full(f))),
                   pl.BlockSpec((dm, tf), lambda b, i, f: (0, full(f))),
                   pl.BlockSpec((tf, dm), lambda b, i, f: (full(f), 0))]
    w_out_shapes = [jax.ShapeDtypeStruct((dm, cut), bf16), jax.ShapeDtypeStruct((dm, cut), bf16),
                    jax.ShapeDtypeStruct((cut, dm), bf16)]
    args = [wg, wu, wd]
    if has_tail:
        ft = f_all - cut
        args += [wg[l, :, cut:], wu[l, :, cut:], wd[l, cut:, :]]
        tails = [pl.BlockSpec((dm, ft), lambda b, i, f: (0, 0)), pl.BlockSpec((dm, ft), lambda b, i, f: (0, 0)),
                 pl.BlockSpec((ft, dm), lambda b, i, f: (0, 0))]
        w_in_specs += tails
        w_out_specs += tails
        w_out_shapes += [jax.ShapeDtypeStruct((dm, ft), bf16), jax.ShapeDtypeStruct((dm, ft), bf16),
                         jax.ShapeDtypeStruct((ft, dm), bf16)]
    out = pl.pallas_call(
        functools.partial(_ffn_cast_kernel, final=final, has_tail=has_tail),
        grid=(1, 1, n_full + int(has_tail)),
        in_specs=[pl.BlockSpec((None, tx, dm), lambda b, i, f: (b, i, 0)),
                  _mod_spec(True, tx, k0), _mod_spec(True, tx, k0 + 1), _mod_spec(True, tx, k0 + 2),
                  pl.BlockSpec((1, dm), lambda b, i, f: (0, 0)),
                  pl.BlockSpec((1, dm), lambda b, i, f: (0, 0))] + w_in_specs,
        out_specs=[pl.BlockSpec((None, tx, dm), lambda b, i, f: (b, i, 0))] + w_out_specs,
        out_shape=[jax.ShapeDtypeStruct(x.shape, f32)] + w_out_shapes,
        scratch_shapes=[pltpu.VMEM((tx, dm), bf16), pltpu.VMEM((tx, dm), f32)],
        compiler_params=_cp(("arbitrary", "arbitrary", "arbitrary")),
        name="ffn_cast",
    )(x, mod, mod, mod, nw, fw, *args)
    return out[0], tuple(out[1:])


def _inproj_kernel(x_ref, sh_ref, sc_ref, nw_ref, w_ref, ws_ref, o_ref, os_ref, h_ref):
    @pl.when(pl.program_id(2) == 0)
    def _():
        y = _rms(x_ref[...], nw_ref[...])
        h = (y * (1.0 + sc_ref[...]) + sh_ref[...]).astype(bf16)
        h_ref[...] = h
        os_ref[...] = _dot_nt(h, ws_ref[...].astype(bf16))

    o_ref[...] = _dot_nt(h_ref[...], w_ref[...]).astype(o_ref.dtype)


def _inproj(x, mod, nw, w, ws, l, per_row, tm, out_dtype):
    bx, tx, dm = x.shape
    n = w.shape[1]
    tn = 1024
    return pl.pallas_call(
        _inproj_kernel,
        grid=(bx, tx // tm, n // tn),
        in_specs=[pl.BlockSpec((None, tm, dm), lambda b, i, j: (b, i, 0)),
                  _mod_spec(per_row, tm, 3), _mod_spec(per_row, tm, 4),
                  pl.BlockSpec((1, dm), lambda b, i, j: (0, 0)),
                  pl.BlockSpec((None, tn, dm), lambda b, i, j: (l, j, 0)),
                  pl.BlockSpec((None, LANE, dm), lambda b, i, j: (l, 0, 0))],
        out_specs=[pl.BlockSpec((None, tm, tn), lambda b, i, j: (b, i, j)),
                   pl.BlockSpec((None, tm, LANE), lambda b, i, j: (b, i, 0))],
        out_shape=[jax.ShapeDtypeStruct((bx, tx, n), out_dtype),
                   jax.ShapeDtypeStruct((bx, tx, LANE), f32)],
        scratch_shapes=[pltpu.VMEM((tm, dm), bf16)],
        compiler_params=_cp(("parallel", "parallel", "arbitrary")),
        name="inproj",
    )(x, mod, mod, nw, w, ws)


def _conv_load(ext_ref, x_ref, cst_ref, first):
    c = x_ref.shape[0]
    nblk = ext_ref.shape[0]

    @pl.when(first)
    def _():
        for j in range(nblk):
            ext_ref[j, pl.ds(5, CONV_W - 1), :] = cst_ref[:, j * LANE:(j + 1) * LANE]

    for j in range(nblk):
        ext_ref[j, pl.ds(8, c), :] = x_ref[:, j * LANE:(j + 1) * LANE].astype(f32)


def _conv_cols(ext_ref, w_ref, b_ref, lo, width, c):
    n = c // SUB
    cols = []
    for blk in range(lo // LANE, (lo + width) // LANE):
        sl = pl.ds(blk * LANE, LANE)
        w = w_ref[:, sl]
        taps = [ext_ref[blk, pl.ds(8 - (CONV_W - 1) + k, n, stride=SUB), :] for k in range(SUB + CONV_W - 1)]
        ys = []
        for r in range(SUB):
            y = taps[r] * w[0:1]
            for j in range(1, CONV_W):
                y = y + taps[r + j] * w[j:j + 1]
            ys.append(y)
        y = jnp.concatenate(ys, axis=0)
        if b_ref is not None:
            y = y + b_ref[:, sl]
        cols.append(_silu(y))
    return cols[0] if len(cols) == 1 else jnp.concatenate(cols, axis=1)


def _conv_carry(ext_ref, c):
    for j in range(ext_ref.shape[0]):
        ext_ref[j, pl.ds(5, CONV_W - 1), :] = ext_ref[j, pl.ds(8 + c - (CONV_W - 1), CONV_W - 1), :]


def _cumsum_rows(x, tri):
    return _dot_hi(jnp.where(tri, 1.0, 0.0).astype(f32), x)


def _gdn_kernel(qkv_ref, z_ref, sm_ref, cst_ref, cw_ref, alog_ref, dtb_ref, nw_ref, s0_ref,
                o_ref, s_ref, ext_ref, nat_ref):
    c = qkv_ref.shape[0]
    first = pl.program_id(1) == 0

    @pl.when(first)
    def _():
        s_ref[...] = s0_ref[...]

    _conv_load(ext_ref, qkv_ref, cst_ref, first)

    sm = _load_residue(sm_ref, c)
    beta_l = _sigmoid(sm)
    g_l = -jnp.exp(alog_ref[...]) * jax.nn.softplus(sm + dtb_ref[...])
    tri, strict, eye = _tri_masks_residue(c)
    gc_all = _cumsum_rows(g_l, tri)
    gc_t = gc_all.T
    eye_f = jnp.where(eye, 1.0, 0.0).astype(f32)
    n_fac = max(1, (c - 1).bit_length() - 1)

    hs = range(GDN_H)
    qb, kb, ks, decay, eg, gcs, nm, rhs = [], [], [], [], [], [], [], []
    for h in hs:
        q = _conv_cols(ext_ref, cw_ref, None, h * GDN_DK, GDN_DK, c)
        k = _conv_cols(ext_ref, cw_ref, None, GDN_QK + h * GDN_DK, GDN_DK, c)
        v = _conv_cols(ext_ref, cw_ref, None, 2 * GDN_QK + h * GDN_DV, GDN_DV, c)
        q = q * lax.rsqrt(jnp.sum(q * q, axis=-1, keepdims=True) + EPS) * (GDN_DK ** -0.5)
        k = k * lax.rsqrt(jnp.sum(k * k, axis=-1, keepdims=True) + EPS)
        beta = beta_l[:, SM_BETA + h:SM_BETA + h + 1]
        gc = gc_all[:, SM_DEC + h:SM_DEC + h + 1]
        d = jnp.exp(jnp.where(tri, gc - gc_t[SM_DEC + h:SM_DEC + h + 1, :], -jnp.inf))
        e = jnp.exp(gc)
        qb.append(q.astype(bf16))
        kb.append(k.astype(bf16))
        ks.append(k)
        decay.append(d)
        eg.append(e)
        gcs.append(gc)
        nm.append(jnp.where(strict, beta * d * _dot_nt(kb[h], kb[h]), 0.0))
        rhs.append(jnp.concatenate([v * beta, k * (beta * e)], axis=1))
        yield

    inv = [eye_f - nm[h] for h in hs]
    pw_s = [_split(nm[h]) for h in hs]
    pw_s = [_split(_dot3(pw_s[h], pw_s[h])) for h in hs]
    yield
    for i in range(n_fac):
        inv = [inv[h] + _dot3(_split(inv[h]), pw_s[h]) for h in hs]
        yield
        if i + 1 < n_fac:
            pw_s = [_split(_dot3(pw_s[h], pw_s[h])) for h in hs]
            yield
    sol = []
    for h in hs:
        inv_hi, inv_lo = _split(inv[h])
        rb = rhs[h].astype(bf16)
        sol.append(_dot(inv_hi, rb) + _dot(inv_lo, rb))
    yield

    s = [s_ref[h] for h in hs]
    sb = [s[h].astype(bf16) for h in hs]
    wb = [(sol[h][:, 0:GDN_DV] - _dot(sol[h][:, GDN_DV:].astype(bf16), sb[h])).astype(bf16) for h in hs]
    yield
    qk = [(_dot_nt(qb[h], kb[h]) * decay[h]).astype(bf16) for h in hs]
    yield
    o = [_dot(qb[h], sb[h]) * eg[h] + _dot(qk[h], wb[h]) for h in hs]
    yield
    for h in hs:
        g_last = gcs[h][c - 1:c]
        kd = (ks[h] * jnp.exp(g_last - gcs[h])).astype(bf16)
        s_ref[h] = s[h] * jnp.exp(g_last) + _dot_tn(kd, wb[h])
    yield
    for h in hs:
        sl = slice(h * GDN_DV, (h + 1) * GDN_DV)
        _store_natural(nat_ref, _rms(o[h], nw_ref[...]), c, h * GDN_DV)
        o_ref[:, sl] = (_read_natural(nat_ref, h * GDN_DV, GDN_DV) * _silu(z_ref[:, sl].astype(f32))).astype(bf16)
        yield

    _conv_carry(ext_ref, c)


def _tok(width, off):
    return pl.BlockSpec((None, CHUNK, width), lambda bi, ti: (bi, ti, _blk(off, width)))


def _per_seq(*shape):
    return pl.BlockSpec((None,) + shape, lambda bi, ti: (bi,) + (0,) * len(shape))


def _const(*shape):
    return pl.BlockSpec(shape, lambda bi, ti: (0,) * len(shape))


N_GDN_IN, N_GLA_IN, N_SSD_IN = 9, 9, 11
_DONE = object()


def _mixers_kernel(*refs):
    i0, i1, i2 = N_GDN_IN, N_GDN_IN + N_GLA_IN, N_GDN_IN + N_GLA_IN + N_SSD_IN
    gdn_in, gla_in, ssd_in = refs[:i0], refs[i0:i1], refs[i1:i2]
    oa_ref, sa_ref, ob_ref, sb_ref, oc_ref, sc_ref, ext_a_ref, st_b_ref, ext_c_ref, nat_a_ref, nat_c_ref = refs[i2:]
    live = [_gdn_kernel(*gdn_in, oa_ref, sa_ref, ext_a_ref, nat_a_ref),
            _gla_kernel(*gla_in, ob_ref, sb_ref, st_b_ref),
            _ssd_kernel(*ssd_in, oc_ref, sc_ref, ext_c_ref, nat_c_ref)]
    while live:
        for gen in list(live):
            if next(gen, _DONE) is _DONE:
                live.remove(gen)


GLA_SUB = 16


def _gla_kernel(q_ref, k_ref, v_ref, r_ref, sm_ref, wgate_ref, bgate_ref, nw_ref, s0_ref,
                o_ref, so_ref, st_ref):
    c = q_ref.shape[0]
    ti = pl.program_id(1)

    @pl.when(ti == 0)
    def _():
        for h in range(GLA_H):
            st_ref[h] = s0_ref[h].T

    sm = sm_ref[...]
    lane = lax.broadcasted_iota(jnp.int32, sm.shape, 1)
    lr = jnp.where((lane >= SM_LR) & (lane < SM_LR + GLA_RANK), sm, 0.0).astype(bf16)
    la = jax.nn.log_sigmoid(_dot(lr, wgate_ref[...]) + bgate_ref[...]) / GLA_TAU
    tri, _, _ = _tri_masks(c)
    b_all = _cumsum_rows(la, tri)

    col = lax.broadcasted_iota(jnp.int32, (GLA_SUB, c), 1)
    row = lax.broadcasted_iota(jnp.int32, (GLA_SUB, c), 0)
    hs = range(GLA_H)
    qs, ks, vbs, bs, atts = [], [], [], [], []
    for h in hs:
        sk = slice(h * GLA_DK, (h + 1) * GLA_DK)
        q = q_ref[:, sk].astype(f32) * (GLA_DK ** -0.5)
        k = k_ref[:, sk].astype(f32)
        b = b_all[:, sk]
        qs.append(q)
        ks.append(k)
        bs.append(b)
        vbs.append(v_ref[:, h * GLA_DV:(h + 1) * GLA_DV].astype(bf16))
        att_rows = []
        for i in range(c // GLA_SUB):
            lo = i * GLA_SUB
            b_i = b[lo:lo + GLA_SUB]
            q_i = q[lo:lo + GLA_SUB]
            b_top = b[lo:lo + 1]
            att = jnp.zeros((GLA_SUB, c), f32)
            if i > 0:
                q_t = (q_i * jnp.exp(b_i - b_top)).astype(bf16)
                k_t = (k * jnp.exp(jnp.minimum(b_top - b, 0.0))).astype(bf16)
                att = jnp.where(col < lo, _dot_nt(q_t, k_t), 0.0)
            for sl in range(GLA_SUB):
                sidx = lo + sl
                e = jnp.exp(jnp.minimum(b_i - b[sidx:sidx + 1], 0.0))
                p = jnp.sum(q_i * k[sidx:sidx + 1] * e, axis=1, keepdims=True)
                att = jnp.where((col == sidx) & (row >= sl), p, att)
            att_rows.append(att)
            yield
        atts.append(jnp.concatenate(att_rows, axis=0).astype(bf16))

    sts = [st_ref[h] for h in hs]
    os_ = [_dot(atts[h], vbs[h]) + _dot_nt((qs[h] * jnp.exp(bs[h])).astype(bf16), sts[h].astype(bf16))
           for h in hs]
    yield
    for h in hs:
        b_last = bs[h][c - 1:c]
        kd = (ks[h] * jnp.exp(b_last - bs[h])).astype(bf16)
        st_ref[h] = sts[h] * jnp.exp(b_last) + _dot_tn(vbs[h], kd)
        yield
    for h in hs:
        sv = slice(h * GLA_DV, (h + 1) * GLA_DV)
        o_ref[:, sv] = (_rms(os_[h], nw_ref[...]) * _silu(r_ref[:, sv].astype(f32))).astype(bf16)
        yield

    @pl.when(ti == pl.num_programs(1) - 1)
    def _():
        for h in hs:
            so_ref[h] = st_ref[h].T


def _ssd_kernel(z_ref, xbc_ref, sm_ref, cst_ref, cw_ref, cb_ref, avec_ref, dtb_ref, dvec_ref, nw_ref, h0_ref,
                o_ref, h_ref, ext_ref, nat_ref):
    c = xbc_ref.shape[0]
    assert c == SSD_P, "head pairs share a lane tile: the chunk must be as wide as a head"
    first = pl.program_id(1) == 0

    @pl.when(first)
    def _():
        h_ref[...] = h0_ref[...]

    _conv_load(ext_ref, xbc_ref, cst_ref, first)

    dt_l = jax.nn.softplus(_load_residue(sm_ref, c) + dtb_ref[...])
    tri, _, _ = _tri_masks_residue(c)
    ac_all = _cumsum_rows(dt_l * avec_ref[...], tri)
    ac_t2 = jnp.concatenate([ac_all, ac_all], axis=0).T

    lane = lax.broadcasted_iota(jnp.int32, (c, 2 * SSD_P), 1)
    lo_half = lane < SSD_P
    tri2, _, _ = _tri_masks_residue(c, 2 * SSD_P)
    lane1 = lane[0:1]
    gs_ = range(SSD_G)
    ps_ = range(SSD_HG // 2)

    xs_, bmb, hg, cb2, ch = [], [], [], [], []
    for g in gs_:
        xs_.append(_conv_cols(ext_ref, cw_ref, cb_ref, g * SSD_GW, SSD_GW, c))
        bmb.append(_conv_cols(ext_ref, cw_ref, cb_ref, SSD_INNER + g * SSD_N, SSD_N, c).astype(bf16))
        cmb = _conv_cols(ext_ref, cw_ref, cb_ref, SSD_INNER + SSD_BC + g * SSD_N, SSD_N, c).astype(bf16)
        hg.append(h_ref[pl.ds(g * SSD_HG, SSD_HG)].reshape(SSD_GW, SSD_N))
        cb2.append(_dot_nt(cmb, jnp.concatenate([bmb[g], bmb[g]], axis=0)))
        ch.append(_dot_nt(cmb, hg[g].astype(bf16)))
        yield

    m2, rhs, e_col, xsc = {}, {}, {}, {}
    for g in gs_:
        for p in ps_:
            l0 = SM_DT + g * SSD_HG + 2 * p
            ps = slice(p * 2 * SSD_P, (p + 1) * 2 * SSD_P)
            ac_col = jnp.where(lo_half, ac_all[:, l0:l0 + 1], ac_all[:, l0 + 1:l0 + 2])
            ac_row = jnp.where(lane1 < SSD_P, ac_t2[l0:l0 + 1, :], ac_t2[l0 + 1:l0 + 2, :])
            decay = jnp.exp(jnp.where(tri2, ac_col - ac_row, -jnp.inf))
            m2[g, p] = (cb2[g] * decay).astype(bf16)
            dt2 = jnp.where(lo_half, dt_l[:, l0:l0 + 1], dt_l[:, l0 + 1:l0 + 2])
            xdt = xs_[g][:, ps] * dt2
            rhs[g, p] = jnp.concatenate([jnp.where(lo_half, xdt, 0.0), jnp.where(lo_half, 0.0, xdt)],
                                        axis=0).astype(bf16)
            e_col[g, p] = jnp.exp(ac_col)
            xsc[g, p] = xdt * jnp.exp(ac_col[c - 1:c] - ac_col)
            yield
    y2 = {gp: _dot(m2[gp], rhs[gp]) for gp in m2}
    yield

    for g in gs_:
        xsg = jnp.concatenate([xsc[g, p] for p in ps_], axis=1).astype(bf16)
        dh = _dot_tn(xsg, bmb[g])
        for j in range(SSD_HG):
            ln = SM_DT + g * SSD_HG + j
            h_ref[g * SSD_HG + j] = (hg[g][j * SSD_P:(j + 1) * SSD_P] * jnp.exp(ac_all[c - 1:c, ln:ln + 1])
                                     + dh[j * SSD_P:(j + 1) * SSD_P])
        yield
    for g in gs_:
        gs = slice(g * SSD_GW, (g + 1) * SSD_GW)
        y = jnp.concatenate([y2[g, p] + ch[g][:, p * 2 * SSD_P:(p + 1) * 2 * SSD_P] * e_col[g, p] for p in ps_],
                            axis=1)
        _store_natural(nat_ref, y + dvec_ref[:, gs] * xs_[g], c, g * SSD_GW)
        y = _read_natural(nat_ref, g * SSD_GW, SSD_GW) * _silu(z_ref[:, gs].astype(f32))
        o_ref[:, gs] = _rms(y, nw_ref[:, gs]).astype(bf16)
        yield

    _conv_carry(ext_ref, c)


STEP_B = 8


def _conv_step(x_ref, c_ref, w_ref):
    w = w_ref[...]
    return c_ref[0] * w[0:1] + c_ref[1] * w[1:2] + c_ref[2] * w[2:3] + x_ref[...] * w[3:4]


def _expand_matrix(nb, width):
    r = lax.broadcasted_iota(jnp.int32, (nb, nb * width), 0)
    c = lax.broadcasted_iota(jnp.int32, (nb, nb * width), 1)
    return jnp.where((c >= r * width) & (c < (r + 1) * width), 1.0, 0.0).astype(bf16)


def _bcast_cols(x, e_mat):
    hi, lo = _split(x)
    return _dot_tn(hi, e_mat) + _dot_tn(lo, e_mat)


def _state_call(kernel, grid, in_specs, out_specs, out_shape, scratch, sem, name, args, prev_state):
    aliases = {}
    if prev_state is not None:
        in_specs = in_specs + [pl.BlockSpec(memory_space=pl.ANY)]
        args = args + (prev_state,)
        aliases = {len(args) - 1: len(out_shape) - 1}
        kernel = functools.partial(_drop_alias_ref, kernel, len(args) - 1)
    return pl.pallas_call(kernel, grid=grid, in_specs=in_specs, out_specs=out_specs, out_shape=out_shape,
                          scratch_shapes=scratch, input_output_aliases=aliases,
                          compiler_params=_cp(sem), name=name)(*args)


def _drop_alias_ref(kernel, idx, *refs):
    return kernel(*refs[:idx], *refs[idx + 1:])


def _gdn_step_kernel(alog_ref, dtb_ref, q_ref, k_ref, v_ref, z_ref, sm_ref, cq_ref, ck_ref, cv_ref,
                     wq_ref, wk_ref, wv_ref, nw_ref, s_ref, o_ref, so_ref):
    nb = q_ref.shape[0]
    qa = _silu(_conv_step(q_ref, cq_ref, wq_ref))
    ka = _silu(_conv_step(k_ref, ck_ref, wk_ref))
    va = _silu(_conv_step(v_ref, cv_ref, wv_ref))
    sm = sm_ref[...]
    z = z_ref[...]
    e_mat = _expand_matrix(nb, GDN_DV)
    for h in range(GDN_H):
        sl = slice(h * GDN_DK, (h + 1) * GDN_DK)
        q = qa[:, sl]
        k = ka[:, sl]
        v = va[:, sl]
        q = q * lax.rsqrt(jnp.sum(q * q, axis=-1, keepdims=True) + EPS) * (GDN_DK ** -0.5)
        k = k * lax.rsqrt(jnp.sum(k * k, axis=-1, keepdims=True) + EPS)
        beta = _sigmoid(sm[:, SM_BETA + h:SM_BETA + h + 1])
        g = -jnp.exp(_scalar_vec(alog_ref[h])) * jax.nn.softplus(sm[:, SM_DEC + h:SM_DEC + h + 1] + dtb_ref[h])
        eg = jnp.exp(g)
        qk = jnp.sum(q * k, axis=-1, keepdims=True)
        kb = _bcast_cols(k, e_mat)
        qb = _bcast_cols(q, e_mat)
        blk = lambda m, b: m[:, b * GDN_DV:(b + 1) * GDN_DV]
        ks = jnp.concatenate([jnp.sum(s_ref[b, h] * blk(kb, b), axis=0, keepdims=True) for b in range(nb)], axis=0)
        qs = jnp.concatenate([jnp.sum(s_ref[b, h] * blk(qb, b), axis=0, keepdims=True) for b in range(nb)], axis=0)
        w = beta * v - (beta * eg) * ks
        o = qs * eg + qk * w
        for b in range(nb):
            so_ref[b, h] = s_ref[b, h] * eg[b:b + 1] + blk(kb, b) * w[b:b + 1]
        o = _rms(o, nw_ref[...]) * _silu(z[:, sl])
        o_ref[:, sl] = o.astype(bf16)


def _gdn_step(proj, small, conv_t, states, prev, l, conv_w, a_log, dt_bias, norm_w):
    n = proj.shape[0]
    nb = STEP_B
    blk = lambda off: pl.BlockSpec((nb, GDN_QK), lambda i: (i, _blk(off, GDN_QK)))
    cst = lambda j: pl.BlockSpec((CONV_W - 1, nb, GDN_QK), lambda i: (0, i, j))
    cw = lambda j: pl.BlockSpec((CONV_W, GDN_QK), lambda i: (0, j))
    smem = pl.BlockSpec(memory_space=pltpu.SMEM)
    st = pl.BlockSpec((None, nb, GDN_H, GDN_DK, GDN_DV), lambda i: (l, i, 0, 0, 0))
    return _state_call(
        _gdn_step_kernel, (n // nb,),
        [smem, smem, blk(P_QKV_A), blk(P_QKV_A + GDN_QK), blk(P_QKV_A + 2 * GDN_QK), blk(P_Z_A),
         pl.BlockSpec((nb, LANE), lambda i: (i, 0)),
         cst(0), cst(1), cst(2), cw(0), cw(1), cw(2),
         pl.BlockSpec((1, GDN_DV), lambda i: (0, 0)), st],
        [pl.BlockSpec((nb, GDN_V), lambda i: (i, 0)), st],
        [jax.ShapeDtypeStruct((n, GDN_V), bf16), jax.ShapeDtypeStruct(states.shape, f32)],
        [], ("parallel",), "gdn_step",
        (a_log, dt_bias, proj, proj, proj, proj, small, conv_t, conv_t, conv_t, conv_w, conv_w, conv_w, norm_w, states),
        prev)


def _gla_step_kernel(q_ref, k_ref, v_ref, r_ref, sm_ref, wgate_ref, bgate_ref, nw_ref, s_ref,
                     o_ref, so_ref):
    nb = q_ref.shape[0]
    sm = sm_ref[...]
    lane = lax.broadcasted_iota(jnp.int32, sm.shape, 1)
    lr = jnp.where((lane >= SM_LR) & (lane < SM_LR + GLA_RANK), sm, 0.0).astype(bf16)
    la_all = jax.nn.log_sigmoid(_dot(lr, wgate_ref[...]) + bgate_ref[...]) / GLA_TAU
    qa = q_ref[...] * (GLA_DK ** -0.5)
    ka = k_ref[...]
    va = v_ref[...]
    r = r_ref[...]
    for h in range(GLA_H):
        sk = slice(h * GLA_DK, (h + 1) * GLA_DK)
        sv = slice(h * GLA_DV, (h + 1) * GLA_DV)
        q = qa[:, sk]
        k = ka[:, sk]
        v = va[:, sv]
        e = jnp.exp(la_all[:, sk])
        qk = jnp.sum(q * k, axis=-1, keepdims=True)
        e_t = e.T
        k_t = k.T
        qe_t = (q * e).T
        rows = []
        for b in range(nb):
            s = s_ref[b, h]
            vrow = v[b:b + 1]
            rows.append(qk[b:b + 1] * vrow + jnp.sum(s * qe_t[:, b:b + 1], axis=0, keepdims=True))
            so_ref[b, h] = s * e_t[:, b:b + 1] + k_t[:, b:b + 1] * vrow
        o = jnp.concatenate(rows, axis=0)
        o = _rms(o, nw_ref[...]) * _silu(r[:, sv])
        o_ref[:, sv] = o.astype(bf16)


def _gla_step(proj, small, states, prev, l, wgate_pad, bgate, norm_w):
    n = proj.shape[0]
    nb = STEP_B
    st = pl.BlockSpec((None, nb, GLA_H, GLA_DK, GLA_DV), lambda i: (l, i, 0, 0, 0))
    return _state_call(
        _gla_step_kernel, (n // nb,),
        [pl.BlockSpec((nb, GLA_QK), lambda i: (i, _blk(P_Q_B, GLA_QK))),
         pl.BlockSpec((nb, GLA_QK), lambda i: (i, _blk(P_K_B, GLA_QK))),
         pl.BlockSpec((nb, GLA_V), lambda i: (i, _blk(P_V_B, GLA_V))),
         pl.BlockSpec((nb, GLA_V), lambda i: (i, _blk(P_R_B, GLA_V))),
         pl.BlockSpec((nb, LANE), lambda i: (i, 0)),
         pl.BlockSpec((LANE, GLA_QK), lambda i: (0, 0)),
         pl.BlockSpec((1, GLA_QK), lambda i: (0, 0)),
         pl.BlockSpec((1, GLA_DV), lambda i: (0, 0)), st],
        [pl.BlockSpec((nb, GLA_V), lambda i: (i, 0)), st],
        [jax.ShapeDtypeStruct((n, GLA_V), bf16), jax.ShapeDtypeStruct(states.shape, f32)],
        [], ("parallel",), "gla_step",
        (proj, proj, proj, proj, small, wgate_pad, bgate, norm_w, states),
        prev)


def _ssd_step_kernel(alog_ref, dtb_ref, z_ref, xbc_ref, sm_ref, cst_ref, cw_ref, cb_ref, dvec_ref, nw_ref, h_ref,
                     o_ref, ho_ref):
    nb = xbc_ref.shape[0]
    xbc = _silu(_conv_step(xbc_ref, cst_ref, cw_ref) + cb_ref[...])
    sm_t = sm_ref[...].T
    e_mat = _expand_matrix(nb, SSD_N)
    lane_b = lax.broadcasted_iota(jnp.int32, (SSD_P, nb), 1)
    for g in range(SSD_G):
        gs = slice(g * SSD_GW, (g + 1) * SSD_GW)
        bm = xbc[:, SSD_INNER + g * SSD_N:SSD_INNER + (g + 1) * SSD_N]
        cm_t = xbc[:, SSD_INNER + SSD_BC + g * SSD_N:SSD_INNER + SSD_BC + (g + 1) * SSD_N].T
        cb_row = jnp.sum(cm_t * bm.T, axis=0, keepdims=True)
        hg = h_ref[:, pl.ds(g * SSD_HG, SSD_HG)].reshape(nb * SSD_GW, SSD_N)
        ch = _dot(hg.astype(bf16), cm_t.astype(bf16))
        pairs = []
        for jp in range(SSD_HG // 2):
            x_t = xbc[:, g * SSD_GW + jp * LANE:g * SSD_GW + (jp + 1) * LANE].T
            halves = []
            for jj in range(2):
                j = 2 * jp + jj
                hd = g * SSD_HG + j
                dt_row = jax.nn.softplus(sm_t[SM_DT + hd:SM_DT + hd + 1, :] + dtb_ref[hd])
                ea_row = jnp.exp(dt_row * (-jnp.exp(_scalar_vec(alog_ref[hd]))))
                xdt_t = x_t[jj * SSD_P:(jj + 1) * SSD_P] * dt_row
                hi, lo = _split(xdt_t)
                xb = _dot(hi, e_mat) + _dot(lo, e_mat)
                yh = jnp.zeros((SSD_P, nb), f32)
                for b in range(nb):
                    r0 = (b * SSD_HG + j) * SSD_P
                    yh = jnp.where(lane_b == b, ch[r0:r0 + SSD_P], yh)
                    ho_ref[b, hd] = (h_ref[b, hd] * ea_row[:, b:b + 1]
                                     + xb[:, b * SSD_N:(b + 1) * SSD_N] * bm[b:b + 1])
                halves.append(yh * ea_row + cb_row * xdt_t)
            pairs.append(jnp.concatenate(halves, axis=0).T)
        y = jnp.concatenate(pairs, axis=1)
        y = (y + dvec_ref[:, gs] * xbc[:, gs]) * _silu(z_ref[:, gs])
        o_ref[:, gs] = _rms(y, nw_ref[:, gs]).astype(bf16)


def _ssd_step(proj, small, conv_t, states, prev, l, conv_w, conv_b, a_log, dt_bias, dvec, norm_w):
    n = proj.shape[0]
    nb = STEP_B
    smem = pl.BlockSpec(memory_space=pltpu.SMEM)
    const = lambda shape: pl.BlockSpec(shape, lambda i: (0,) * len(shape))
    st = pl.BlockSpec((None, nb, SSD_H, SSD_P, SSD_N), lambda i: (l, i, 0, 0, 0))
    return _state_call(
        _ssd_step_kernel, (n // nb,),
        [smem, smem,
         pl.BlockSpec((nb, SSD_INNER), lambda i: (i, _blk(P_Z_C, SSD_INNER))),
         pl.BlockSpec((nb, SSD_CONV), lambda i: (i, _blk(P_XBC, SSD_CONV))),
         pl.BlockSpec((nb, LANE), lambda i: (i, 0)),
         pl.BlockSpec((CONV_W - 1, nb, SSD_CONV), lambda i: (0, i, 0)),
         const((CONV_W, SSD_CONV)), const((1, SSD_CONV)), const((1, SSD_INNER)), const((1, SSD_INNER)), st],
        [pl.BlockSpec((nb, SSD_INNER), lambda i: (i, 0)), st],
        [jax.ShapeDtypeStruct((n, SSD_INNER), bf16), jax.ShapeDtypeStruct(states.shape, f32)],
        [], ("parallel",), "ssd_step",
        (a_log, dt_bias, proj, proj, small, conv_t, conv_w, conv_b, dvec, norm_w, states),
        prev)


def _merge_kernel(oa_ref, ob_ref, oc_ref, ga_ref, gb_ref, gc_ref, wa_ref, wb_ref, wc_ref, o_ref):
    m = (_sigmoid(ga_ref[...].astype(f32)) * _dot(oa_ref[...], wa_ref[...])
         + _sigmoid(gb_ref[...].astype(f32)) * _dot(ob_ref[...], wb_ref[...])
         + _sigmoid(gc_ref[...].astype(f32)) * _dot(oc_ref[...], wc_ref[...]))
    o_ref[...] = m.astype(bf16)


def _outproj_kernel(m_ref, x_ref, gt_ref, w_ref, o_ref):
    o_ref[...] = x_ref[...] + gt_ref[...] * _dot(m_ref[...], w_ref[...])


def _mixout(oa, ob, oc, proj, x, mod, wa, wb, wc, wo, l, per_row, tm):
    bx, tx, dm = x.shape
    tn = 512
    gate = lambda k: pl.BlockSpec((None, tm, tn), lambda b, i, j: (b, i, _blk(P_GATES + k * D_MODEL, tn) + j))
    act = lambda w: pl.BlockSpec((None, tm, w), lambda b, i, j: (b, i, 0))
    wsp = lambda w: pl.BlockSpec((None, w, tn), lambda b, i, j: (l, 0, j))
    merged = pl.pallas_call(
        _merge_kernel,
        grid=(bx, tx // tm, dm // tn),
        in_specs=[act(GDN_V), act(GLA_V), act(SSD_INNER), gate(0), gate(1), gate(2),
                  wsp(GDN_V), wsp(GLA_V), wsp(SSD_INNER)],
        out_specs=pl.BlockSpec((None, tm, tn), lambda b, i, j: (b, i, j)),
        out_shape=jax.ShapeDtypeStruct((bx, tx, dm), bf16),
        compiler_params=_cp(("parallel", "parallel", "arbitrary")),
        name="merge",
    )(oa, ob, oc, proj, proj, proj, wa, wb, wc)
    to = 1024
    nj = dm // to
    if per_row:
        gspec = pl.BlockSpec((None, tm, to), lambda b, i, j: (b, i, 5 * nj + j))
    else:
        gspec = pl.BlockSpec((None, 1, to), lambda b, i, j: (b, 0, 5 * nj + j))
    return pl.pallas_call(
        _outproj_kernel,
        grid=(bx, tx // tm, nj),
        in_specs=[pl.BlockSpec((None, tm, dm), lambda b, i, j: (b, i, 0)),
                  pl.BlockSpec((None, tm, to), lambda b, i, j: (b, i, j)),
                  gspec,
                  pl.BlockSpec((None, dm, to), lambda b, i, j: (l, 0, j))],
        out_specs=pl.BlockSpec((None, tm, to), lambda b, i, j: (b, i, j)),
        out_shape=jax.ShapeDtypeStruct(x.shape, f32),
        compiler_params=_cp(("parallel", "parallel", "arbitrary")),
        name="outproj",
    )(merged, x, mod, wo)


REGROUP_ROWS = 512


def _regroup_kernel(w_ref, o_ref):
    o_ref[...] = w_ref[0].astype(bf16)


def _regroup_rows(wt):
    nl, _, d = wt.shape
    tr = REGROUP_ROWS
    shifts = []
    for name in MAIN_ORDER:
        a, b = W_IN_SRC[name]
        assert (b - a) % tr == 0 and a % SUB == 0
        shifts.append((P_MAIN[name] // tr, a - P_MAIN[name]))

    def src_row(i):
        off = i * (tr // SUB) + shifts[0][1] // SUB
        for k in range(1, len(shifts)):
            off = off + jnp.where(i >= shifts[k][0], (shifts[k][1] - shifts[k - 1][1]) // SUB, 0)
        return off * SUB

    return pl.pallas_call(
        _regroup_kernel,
        grid=(nl, P_TOTAL // tr),
        in_specs=[pl.BlockSpec((pl.Element(1), pl.Element(tr), pl.Element(d)), lambda l, i: (l, src_row(i), 0))],
        out_specs=pl.BlockSpec((None, tr, d), lambda l, i: (l, i, 0)),
        out_shape=jax.ShapeDtypeStruct((nl, P_TOTAL, d), bf16),
        compiler_params=_cp(("parallel", "parallel")),
        name="regroup_w_in",
    )(wt)


def _permute_w_in(w):
    nl, d, _ = w.shape
    main = _regroup_rows(jnp.swapaxes(w, 1, 2))
    small = jnp.concatenate([w[:, :, W_IN_SRC[name][0]:W_IN_SRC[name][1]] for name in SMALL_ORDER], axis=2)
    small = jnp.pad(jnp.swapaxes(small, 1, 2), ((0, 0), (0, LANE - small.shape[2]), (0, 0)))
    return main, small


def _lane_vec(v, lo):
    return jnp.zeros((1, LANE), f32).at[0, lo:lo + v.shape[0]].set(v)


def _layer_params(l, p):
    row = lambda a: a[l].reshape(1, -1)
    wgate = jnp.zeros((LANE, GLA_QK), f32).at[SM_LR:SM_LR + GLA_RANK].set(p["gla_w_gate"][l]).astype(bf16)
    return dict(
        norm1=row(p["norm1"]), norm2=row(p["norm2"]), norm3=row(p["norm3"]),
        gdn_conv_w=p["gdn_conv_w"][l], gdn_a_log=p["gdn_a_log"][l], gdn_dt_bias=p["gdn_dt_bias"][l],
        gdn_alog_l=_lane_vec(p["gdn_a_log"][l], SM_DEC), gdn_dtb_l=_lane_vec(p["gdn_dt_bias"][l], SM_DEC),
        gdn_norm_w=row(p["gdn_norm_w"]),
        gla_wgate=wgate, gla_bgate=row(p["gla_b_gate"]), gla_norm_w=row(p["gla_norm_w"]),
        ssd_conv_w=p["ssd_conv_w"][l], ssd_conv_b=row(p["ssd_conv_b"]), ssd_a_log=p["ssd_a_log"][l],
        ssd_dt_bias=p["ssd_dt_bias"][l],
        ssd_avec_l=_lane_vec(-jnp.exp(p["ssd_a_log"][l]), SM_DT), ssd_dtb_l=_lane_vec(p["ssd_dt_bias"][l], SM_DT),
        ssd_dvec=jnp.repeat(p["ssd_d"][l], SSD_P).reshape(1, -1),
        ssd_norm_w=row(p["ssd_norm_w"]),
    )


def _stacked_weights(p):
    w_in, w_in_small = _permute_w_in(p["w_in"])
    return dict(
        f1=(p["ffn1_wg"], p["ffn1_wu"], p["ffn1_wd"]),
        f2=(p["ffn2_wg"], p["ffn2_wu"], p["ffn2_wd"]),
        w_in=w_in, w_in_small=w_in_small,
        wa=p["w_branch_gdn"].astype(bf16), wb=p["w_branch_gla"].astype(bf16),
        wc=p["w_branch_ssd"].astype(bf16), w_out=p["w_out"].astype(bf16),
    )


def _new_conv_state(buf, raw):
    t = raw.shape[1]
    k = CONV_W - 1
    if t >= k:
        return raw[:, t - k:]
    return jnp.concatenate([buf[:, t:], raw], axis=1)


def _mixer_prompt(proj, small, lp, st):
    gdn_conv, s_gdn, s_gla, ssd_conv, s_ssd = st
    b, t, _ = proj.shape
    c = CHUNK
    sm_spec = pl.BlockSpec((None, c, LANE), lambda bi, ti: (bi, ti, 0))
    st_a, st_b, st_c = _per_seq(GDN_H, GDN_DK, GDN_DV), _per_seq(GLA_H, GLA_DK, GLA_DV), _per_seq(SSD_H, SSD_P, SSD_N)
    gdn_specs = [_tok(GDN_CONV, P_QKV_A), _tok(GDN_V, P_Z_A), sm_spec, _per_seq(CONV_W - 1, GDN_CONV),
                 _const(CONV_W, GDN_CONV), _const(1, LANE), _const(1, LANE), _const(1, GDN_DV), st_a]
    gdn_args = (proj, proj, small, gdn_conv, lp["gdn_conv_w"], lp["gdn_alog_l"], lp["gdn_dtb_l"],
                lp["gdn_norm_w"], s_gdn)
    gla_specs = [_tok(GLA_QK, P_Q_B), _tok(GLA_QK, P_K_B), _tok(GLA_V, P_V_B), _tok(GLA_V, P_R_B), sm_spec,
                 _const(LANE, GLA_QK), _const(1, GLA_QK), _const(1, GLA_DV), st_b]
    gla_args = (proj, proj, proj, proj, small, lp["gla_wgate"], lp["gla_bgate"], lp["gla_norm_w"], s_gla)
    ssd_specs = [_tok(SSD_INNER, P_Z_C), _tok(SSD_CONV, P_XBC), sm_spec, _per_seq(CONV_W - 1, SSD_CONV),
                 _const(CONV_W, SSD_CONV), _const(1, SSD_CONV), _const(1, LANE), _const(1, LANE),
                 _const(1, SSD_INNER), _const(1, SSD_INNER), st_c]
    ssd_args = (proj, proj, small, ssd_conv, lp["ssd_conv_w"], lp["ssd_conv_b"], lp["ssd_avec_l"],
                lp["ssd_dtb_l"], lp["ssd_dvec"], lp["ssd_norm_w"], s_ssd)
    assert (len(gdn_specs), len(gla_specs), len(ssd_specs)) == (N_GDN_IN, N_GLA_IN, N_SSD_IN)
    out_tok = lambda width: pl.BlockSpec((None, c, width), lambda bi, ti: (bi, ti, 0))
    oa, s_gdn_n, ob, s_gla_n, oc, s_ssd_n = pl.pallas_call(
        _mixers_kernel,
        grid=(b, t // c),
        in_specs=gdn_specs + gla_specs + ssd_specs,
        out_specs=[out_tok(GDN_V), st_a, out_tok(GLA_V), st_b, out_tok(SSD_INNER), st_c],
        out_shape=[jax.ShapeDtypeStruct((b, t, GDN_V), bf16), jax.ShapeDtypeStruct(s_gdn.shape, f32),
                   jax.ShapeDtypeStruct((b, t, GLA_V), bf16), jax.ShapeDtypeStruct(s_gla.shape, f32),
                   jax.ShapeDtypeStruct((b, t, SSD_INNER), bf16), jax.ShapeDtypeStruct(s_ssd.shape, f32)],
        scratch_shapes=[pltpu.VMEM((GDN_CONV // LANE, 8 + c, LANE), f32), pltpu.VMEM((GLA_H, GLA_DV, GLA_DK), f32),
                        pltpu.VMEM((SSD_CONV // LANE, 8 + c, LANE), f32),
                        pltpu.VMEM((GDN_V // LANE, c, LANE), f32), pltpu.VMEM((SSD_INNER // LANE, c, LANE), f32)],
        compiler_params=_cp(("parallel", "arbitrary")),
        name="mixers_prompt",
    )(*gdn_args, *gla_args, *ssd_args)
    gdn_conv_n = _new_conv_state(gdn_conv, proj[:, :, P_QKV_A:P_QKV_A + GDN_CONV]).astype(f32)
    ssd_conv_n = _new_conv_state(ssd_conv, proj[:, :, P_XBC:P_XBC + SSD_CONV]).astype(f32)
    return (oa, ob, oc), (gdn_conv_n, s_gdn_n, s_gla_n, ssd_conv_n, s_ssd_n)


def _mixer_sample(proj, small, lp, l, states, prev):
    n = proj.shape[1]
    p2 = proj.reshape(n, P_TOTAL)
    s2 = small.reshape(n, LANE)
    gdn_conv, ssd_conv = states[0][l], states[3][l]
    gct = jnp.swapaxes(gdn_conv, 0, 1)
    sct = jnp.swapaxes(ssd_conv, 0, 1)
    pv = (None,) * 5 if prev is None else prev
    oa, s_gdn_n = _gdn_step(p2, s2, gct, states[1], pv[1], l, lp["gdn_conv_w"], lp["gdn_a_log"],
                            lp["gdn_dt_bias"], lp["gdn_norm_w"])
    ob, s_gla_n = _gla_step(p2, s2, states[2], pv[2], l, lp["gla_wgate"], lp["gla_bgate"], lp["gla_norm_w"])
    oc, s_ssd_n = _ssd_step(p2, s2, sct, states[4], pv[4], l, lp["ssd_conv_w"], lp["ssd_conv_b"], lp["ssd_a_log"],
                            lp["ssd_dt_bias"], lp["ssd_dvec"], lp["ssd_norm_w"])
    raw = p2.reshape(n, 1, P_TOTAL)
    gdn_conv_n = _new_conv_state(gdn_conv, raw[:, :, P_QKV_A:P_QKV_A + GDN_CONV])
    ssd_conv_n = _new_conv_state(ssd_conv, raw[:, :, P_XBC:P_XBC + SSD_CONV])
    outs = tuple(o.reshape(1, n, -1) for o in (oa, ob, oc))
    return outs, (gdn_conv_n, s_gdn_n, s_gla_n, ssd_conv_n, s_ssd_n)


def _trunk(x, mods, lps, sw, ffn_w, states, per_row, tm, tm_in, final_w):
    nl = len(lps)
    per_layer = []
    prev = None

    def ffn(x, mod, k0, nw, name, l, fw, final):
        if per_row:
            x, ffn_w[l, name] = _ffn_cast(x, mod, k0, nw, *sw[name], l, fw, final)
            return x
        return _ffn(x, mod, k0, nw, ffn_w[l, name], fw, tm, final)

    for l in range(nl):
        lp, mod = lps[l], mods[l]
        last = l == nl - 1
        x = ffn(x, mod, 0, lp["norm1"], "f1", l, lp["norm1"], False)
        proj, small = _inproj(x, mod, lp["norm2"], sw["w_in"], sw["w_in_small"], l, per_row, tm_in,
                              f32 if per_row else bf16)
        if per_row:
            (oa, ob, oc), st = _mixer_sample(proj, small, lp, l, states, prev)
            prev = st
        else:
            (oa, ob, oc), st = _mixer_prompt(proj, small, lp, tuple(s[l] for s in states))
        per_layer.append(st)
        x = _mixout(oa, ob, oc, proj, x, mod, sw["wa"], sw["wb"], sw["wc"], sw["w_out"], l, per_row, tm_in)
        x = ffn(x, mod, 6, lp["norm3"], "f2", l, final_w if last else lp["norm3"], last)
    stack = lambda i: jnp.stack([st[i] for st in per_layer])
    if per_row:
        new_states = (stack(0), prev[1], prev[2], stack(3), prev[4])
    else:
        new_states = tuple(stack(i) for i in range(5))
    return x, new_states


def kernel(x_prompt, x_sample, state_gdn_conv, state_gdn, state_gla, state_ssd_conv, state_ssd, c_prompt, c_sample, w_ada, b_ada, norm1, norm2, norm3, ffn1_wg, ffn1_wu, ffn1_wd, ffn2_wg, ffn2_wu, ffn2_wd, w_in, gdn_conv_w, gdn_a_log, gdn_dt_bias, gdn_norm_w, gla_w_gate, gla_b_gate, gla_norm_w, ssd_conv_w, ssd_conv_b, ssd_a_log, ssd_dt_bias, ssd_d, ssd_norm_w, w_branch_gdn, w_branch_gla, w_branch_ssd, w_out, final_norm):
    p = dict(norm1=norm1, norm2=norm2, norm3=norm3,
             ffn1_wg=ffn1_wg, ffn1_wu=ffn1_wu, ffn1_wd=ffn1_wd, ffn2_wg=ffn2_wg, ffn2_wu=ffn2_wu, ffn2_wd=ffn2_wd,
             w_in=w_in, gdn_conv_w=gdn_conv_w, gdn_a_log=gdn_a_log, gdn_dt_bias=gdn_dt_bias, gdn_norm_w=gdn_norm_w,
             gla_w_gate=gla_w_gate, gla_b_gate=gla_b_gate, gla_norm_w=gla_norm_w,
             ssd_conv_w=ssd_conv_w, ssd_conv_b=ssd_conv_b, ssd_a_log=ssd_a_log, ssd_dt_bias=ssd_dt_bias,
             ssd_d=ssd_d, ssd_norm_w=ssd_norm_w,
             w_branch_gdn=w_branch_gdn, w_branch_gla=w_branch_gla, w_branch_ssd=w_branch_ssd, w_out=w_out)
    nl = w_ada.shape[0]
    bp, tp, dm = x_prompt.shape
    bs = x_sample.shape[0]
    assert x_sample.shape[1] == 1 and tp % CHUNK == 0 and bs % STEP_B == 0 and dm == D_MODEL
    lps = [_layer_params(l, p) for l in range(nl)]
    sw = _stacked_weights(p)
    fw = final_norm.reshape(1, dm)

    rows = bp + bs
    rpad = -(-rows // 8) * 8
    c_all = jnp.concatenate([c_prompt, c_sample, jnp.zeros((rpad - rows, dm), f32)], axis=0)
    mod = _ada_mod(c_all, w_ada, b_ada)
    mod_p = [mod[l, :bp].reshape(bp, 1, N_MOD * dm) for l in range(nl)]
    mod_s = [mod[l, bp:rows].reshape(1, bs, N_MOD * dm) for l in range(nl)]

    sample_states = (state_gdn_conv, state_gdn, state_gla, state_ssd_conv, state_ssd)
    prompt_states = tuple(jnp.zeros((s.shape[0], bp) + s.shape[2:], x_prompt.dtype) for s in sample_states)
    tm_p = 512 if tp % 512 == 0 else CHUNK
    tm_in = 1024 if tp % 1024 == 0 else tm_p
    ffn_w = {}
    y_s, st_s = _trunk(x_sample.reshape(1, bs, dm), mod_s, lps, sw, ffn_w, sample_states, True, bs, bs, fw)
    y_p, st_p = _trunk(x_prompt, mod_p, lps, sw, ffn_w, prompt_states, False, tm_p, tm_in, fw)
    return (y_p, y_s.reshape(bs, 1, dm)) + st_p + st_s
```

```python
import functools

import jax
import jax.numpy as jnp
from jax import lax
from jax.experimental import pallas as pl
from jax.experimental.pallas import tpu as pltpu

f32 = jnp.float32
bf16 = jnp.bfloat16
HI = lax.Precision.HIGHEST

EPS = 1e-6
D_MODEL = 2048
N_MOD = 9
CHUNK = 64
CONV_W = 4
GDN_H, GDN_DK, GDN_DV = 8, 128, 128
GLA_H, GLA_DK, GLA_DV, GLA_RANK, GLA_TAU = 4, 128, 256, 16, 16.0
SSD_H, SSD_P, SSD_G, SSD_N = 32, 64, 4, 128
SSD_HG = SSD_H // SSD_G
GDN_QK = GDN_H * GDN_DK
GDN_V = GDN_H * GDN_DV
GDN_CONV = 2 * GDN_QK + GDN_V
GLA_QK = GLA_H * GLA_DK
GLA_V = GLA_H * GLA_DV
SSD_INNER = SSD_H * SSD_P
SSD_BC = SSD_G * SSD_N
SSD_CONV = SSD_INNER + 2 * SSD_BC
SSD_GW = SSD_HG * SSD_P

LANE = 128
MXU_WIDTH = 256

ROW_TILE = 512
ROW_TILE_WIDE = 1024
FF_TILE = 512
FF_SUB = MXU_WIDTH
ADA_COLS = 1024
INPROJ_COLS = 1024
MERGE_COLS = 512
OUTPROJ_COLS = 1024

IN_SPLITS = (("qkv_a", GDN_CONV), ("z_a", GDN_V), ("beta", GDN_H), ("dec", GDN_H),
             ("q_b", GLA_QK), ("k_b", GLA_QK), ("v_b", GLA_V), ("lr", GLA_RANK), ("r_b", GLA_V),
             ("z_c", SSD_INNER), ("xbc", SSD_CONV), ("dt", SSD_H), ("gates", 3 * D_MODEL))
MAIN_ORDER = ("qkv_a", "xbc", "z_c", "gates", "z_a", "q_b", "k_b", "v_b", "r_b")
SMALL_ORDER = ("beta", "dec", "lr", "dt")


def _layout():
    src, off = {}, 0
    for name, w in IN_SPLITS:
        src[name] = (off, off + w)
        off += w
    main, small, d = {}, {}, 0
    for name in MAIN_ORDER:
        w = src[name][1] - src[name][0]
        main[name] = d
        d += w
    total = d
    d = 0
    for name in SMALL_ORDER:
        small[name] = d
        d += src[name][1] - src[name][0]
    assert d <= LANE
    return src, main, small, total


W_IN_SRC, P_MAIN, P_SM, P_TOTAL = _layout()
P_QKV_A, P_XBC, P_Z_C, P_GATES, P_Z_A = (P_MAIN[k] for k in ("qkv_a", "xbc", "z_c", "gates", "z_a"))
P_Q_B, P_K_B, P_V_B, P_R_B = (P_MAIN[k] for k in ("q_b", "k_b", "v_b", "r_b"))
SM_BETA, SM_DEC, SM_LR, SM_DT = (P_SM[k] for k in SMALL_ORDER)

VMEM_LIMIT = 56 * 1024 * 1024


def _cp(sem):
    return pltpu.CompilerParams(dimension_semantics=sem, vmem_limit_bytes=VMEM_LIMIT)


def _blk(off, width):
    assert off % width == 0, (off, width)
    return off // width


def _sigmoid(x):
    return 0.5 + 0.5 * jnp.tanh(0.5 * x)


def _silu(x):
    h = 0.5 * x
    return h + h * jnp.tanh(h)


def _rms(x, w):
    return x * lax.rsqrt(jnp.mean(x * x, axis=-1, keepdims=True) + EPS) * w


def _dot(a, b):
    return jnp.dot(a, b, preferred_element_type=f32)


def _dot_nt(a, b):
    return lax.dot_general(a, b, (((1,), (1,)), ((), ())), preferred_element_type=f32)


def _dot_tn(a, b):
    return lax.dot_general(a, b, (((0,), (0,)), ((), ())), preferred_element_type=f32)


def _split(x):
    hi = x.astype(bf16)
    return hi, (x - hi.astype(f32)).astype(bf16)


def _dot3(a, b):
    return _dot(a[0], b[0]) + (_dot(a[0], b[1]) + _dot(a[1], b[0]))


def _dot_hi(a, b):
    return jnp.dot(a, b, precision=HI, preferred_element_type=f32)


def _tri_masks(c):
    row = lax.broadcasted_iota(jnp.int32, (c, c), 0)
    col = lax.broadcasted_iota(jnp.int32, (c, c), 1)
    return row >= col, row > col, row == col


SUB = 8


def _row_time(idx, c):
    n = c // SUB
    assert n & (n - 1) == 0, "chunk / SUB must be a power of two"
    sh = n.bit_length() - 1
    return ((idx & (n - 1)) << 3) | (idx >> sh)


def _tri_masks_residue(c, cols=None):
    cols = c if cols is None else cols
    row = _row_time(lax.broadcasted_iota(jnp.int32, (c, cols), 0), c)
    col = _row_time(lax.broadcasted_iota(jnp.int32, (c, cols), 1) & (c - 1), c)
    return row >= col, row > col, row == col


def _load_residue(ref, c):
    n = c // SUB
    return jnp.concatenate([ref[pl.ds(r, n, stride=SUB), :] for r in range(SUB)], axis=0)


def _store_natural(nat_ref, x, c, lo):
    n = c // SUB
    for j in range(x.shape[1] // LANE):
        for r in range(SUB):
            nat_ref[lo // LANE + j, pl.ds(r, n, stride=SUB), :] = x[r * n:(r + 1) * n, j * LANE:(j + 1) * LANE]


def _read_natural(nat_ref, lo, width):
    return jnp.concatenate([nat_ref[lo // LANE + j] for j in range(width // LANE)], axis=1)


def _scalar_vec(s):
    return jnp.full((1, 1), s, f32)


def _ada_kernel(c_ref, w_ref, b_ref, o_ref):
    s = _silu(c_ref[...]).astype(bf16)
    o_ref[...] = _dot(s, w_ref[...].astype(bf16)) + b_ref[...]


def _ada_mod(c_all, w_ada, b_ada):
    nl, dm, n = w_ada.shape
    r = c_all.shape[0]
    tn = ADA_COLS
    return pl.pallas_call(
        _ada_kernel,
        grid=(nl, n // tn),
        in_specs=[pl.BlockSpec((r, dm), lambda l, j: (0, 0)),
                  pl.BlockSpec((None, dm, tn), lambda l, j: (l, 0, j)),
                  pl.BlockSpec((None, 1, tn), lambda l, j: (l, 0, j))],
        out_specs=pl.BlockSpec((None, r, tn), lambda l, j: (l, 0, j)),
        out_shape=jax.ShapeDtypeStruct((nl, r, n), f32),
        compiler_params=_cp(("arbitrary", "arbitrary")),
        name="ada_mod",
    )(c_all, w_ada, b_ada.reshape(nl, 1, n))


def _mod_spec(per_row, tm, chunk):
    if per_row:
        return pl.BlockSpec((None, tm, D_MODEL), lambda b, i, j: (b, i, chunk))
    return pl.BlockSpec((None, 1, D_MODEL), lambda b, i, j: (b, 0, chunk))


def _ffn_prologue(x_ref, sh_ref, sc_ref, nw_ref, h_ref, acc_ref):
    y = _rms(x_ref[...], nw_ref[...])
    h_ref[...] = (y * (1.0 + sc_ref[...]) + sh_ref[...]).astype(bf16)
    acc_ref[...] = jnp.zeros_like(acc_ref)


def _ffn_accumulate(h_ref, acc_ref, wg, wu, wd):
    h = h_ref[...]
    width = wg.shape[1]
    sub = FF_SUB if width % FF_SUB == 0 else width
    parts = [(_dot(h, wg[:, s:s + sub]), _dot(h, wu[:, s:s + sub])) for s in range(0, width, sub)]
    acc = acc_ref[...]
    for i, (g, u) in enumerate(parts):
        acc = acc + _dot((_silu(g) * u).astype(bf16), wd[i * sub:(i + 1) * sub, :])
    acc_ref[...] = acc


def _ffn_epilogue(x_ref, gt_ref, fw_ref, acc_ref, o_ref, final):
    y = x_ref[...] + 0.5 * gt_ref[...] * acc_ref[...]
    if final:
        y = _rms(y, fw_ref[...])
    o_ref[...] = y


def _ffn_kernel(x_ref, sh_ref, sc_ref, gt_ref, nw_ref, fw_ref, wg_ref, wu_ref, wd_ref, *rest, final, has_tail):
    if has_tail:
        wgt_ref, wut_ref, wdt_ref, o_ref, h_ref, acc_ref = rest
    else:
        o_ref, h_ref, acc_ref = rest
    f = pl.program_id(2)
    last = pl.num_programs(2) - 1
    pl.when(f == 0)(lambda: _ffn_prologue(x_ref, sh_ref, sc_ref, nw_ref, h_ref, acc_ref))
    if has_tail:
        pl.when(f < last)(lambda: _ffn_accumulate(h_ref, acc_ref, wg_ref, wu_ref, wd_ref))
        pl.when(f == last)(lambda: _ffn_accumulate(h_ref, acc_ref, wgt_ref, wut_ref, wdt_ref))
    else:
        _ffn_accumulate(h_ref, acc_ref, wg_ref, wu_ref, wd_ref)
    pl.when(f == last)(lambda: _ffn_epilogue(x_ref, gt_ref, fw_ref, acc_ref, o_ref, final))


def _ffn(x, mod, k0, nw, w, fw, per_row, tm, final):
    bx, tx, dm = x.shape
    tf = FF_TILE
    n_full = w[0].shape[1] // tf
    has_tail = len(w) > 3
    full = lambda f: jnp.minimum(f, n_full - 1)
    w_specs = [pl.BlockSpec((dm, tf), lambda b, i, f: (0, full(f))),
               pl.BlockSpec((dm, tf), lambda b, i, f: (0, full(f))),
               pl.BlockSpec((tf, dm), lambda b, i, f: (full(f), 0))]
    if has_tail:
        ft = w[3].shape[1]
        w_specs += [pl.BlockSpec((dm, ft), lambda b, i, f: (0, 0)),
                    pl.BlockSpec((dm, ft), lambda b, i, f: (0, 0)),
                    pl.BlockSpec((ft, dm), lambda b, i, f: (0, 0))]
    return pl.pallas_call(
        functools.partial(_ffn_kernel, final=final, has_tail=has_tail),
        grid=(bx, tx // tm, n_full + int(has_tail)),
        in_specs=[pl.BlockSpec((None, tm, dm), lambda b, i, f: (b, i, 0)),
                  _mod_spec(per_row, tm, k0), _mod_spec(per_row, tm, k0 + 1), _mod_spec(per_row, tm, k0 + 2),
                  pl.BlockSpec((1, dm), lambda b, i, f: (0, 0)),
                  pl.BlockSpec((1, dm), lambda b, i, f: (0, 0))] + w_specs,
        out_specs=pl.BlockSpec((None, tm, dm), lambda b, i, f: (b, i, 0)),
        out_shape=jax.ShapeDtypeStruct(x.shape, f32),
        scratch_shapes=[pltpu.VMEM((tm, dm), bf16), pltpu.VMEM((tm, dm), f32)],
        compiler_params=_cp(("parallel", "parallel", "arbitrary")),
        name="ffn",
    )(x, mod, mod, mod, nw, fw, *w)


FF_CAST_TILE = 256


def _ffn_cast_kernel(x_ref, sh_ref, sc_ref, gt_ref, nw_ref, fw_ref, wg_ref, wu_ref, wd_ref, *rest, final, has_tail):
    if has_tail:
        wgt_ref, wut_ref, wdt_ref, o_ref, wgb_ref, wub_ref, wdb_ref, wgtb_ref, wutb_ref, wdtb_ref, h_ref, acc_ref = rest
    else:
        o_ref, wgb_ref, wub_ref, wdb_ref, h_ref, acc_ref = rest
    f = pl.program_id(2)
    last = pl.num_programs(2) - 1
    pl.when(f == 0)(lambda: _ffn_prologue(x_ref, sh_ref, sc_ref, nw_ref, h_ref, acc_ref))

    def cast_accumulate(src, dst):
        for s_ref, d_ref in zip(src, dst):
            d_ref[...] = s_ref[...].reshape(d_ref.shape).astype(bf16)
        _ffn_accumulate(h_ref, acc_ref, *dst)

    if has_tail:
        pl.when(f < last)(lambda: cast_accumulate((wg_ref, wu_ref, wd_ref), (wgb_ref, wub_ref, wdb_ref)))
        pl.when(f == last)(lambda: cast_accumulate((wgt_ref, wut_ref, wdt_ref), (wgtb_ref, wutb_ref, wdtb_ref)))
    else:
        cast_accumulate((wg_ref, wu_ref, wd_ref), (wgb_ref, wub_ref, wdb_ref))
    pl.when(f == last)(lambda: _ffn_epilogue(x_ref, gt_ref, fw_ref, acc_ref, o_ref, final))


def _ffn_cast(x, mod, k0, nw, wg, wu, wd, l, fw, final):
    bx, tx, dm = x.shape
    assert bx == 1
    f_all = wg.shape[2]
    tf = FF_CAST_TILE
    cut = f_all - f_all % FF_TILE
    n_full = cut // tf
    has_tail = cut < f_all
    full = lambda f: jnp.minimum(f, n_full - 1)
    w_in_specs = [pl.BlockSpec((None, dm, tf), lambda b, i, f: (l, 0, full(f))),
                  pl.BlockSpec((None, dm, tf), lambda b, i, f: (l, 0, full(f))),
                  pl.BlockSpec((None, tf, dm), lambda b, i, f: (l, full(f), 0))]
    w_out_specs = [pl.BlockSpec((dm, tf), lambda b, i, f: (0, full(f))),
                   pl.BlockSpec((dm, tf), lambda b, i, f: (0, full(f))),
                   pl.BlockSpec((tf, dm), lambda b, i, f: (full(f), 0))]
    w_out_shapes = [jax.ShapeDtypeStruct((dm, cut), bf16), jax.ShapeDtypeStruct((dm, cut), bf16),
                    jax.ShapeDtypeStruct((cut, dm), bf16)]
    args = [wg, wu, wd]
    if has_tail:
        ft = f_all - cut
        args += [wg, wu, wd]
        el = lambda *shape: tuple(pl.Element(s) for s in shape)
        w_in_specs += [pl.BlockSpec(el(1, dm, ft), lambda b, i, f: (l, 0, cut)),
                       pl.BlockSpec(el(1, dm, ft), lambda b, i, f: (l, 0, cut)),
                       pl.BlockSpec(el(1, ft, dm), lambda b, i, f: (l, cut, 0))]
        w_out_specs += [pl.BlockSpec((dm, ft), lambda b, i, f: (0, 0)), pl.BlockSpec((dm, ft), lambda b, i, f: (0, 0)),
                        pl.BlockSpec((ft, dm), lambda b, i, f: (0, 0))]
        w_out_shapes += [jax.ShapeDtypeStruct((dm, ft), bf16), jax.ShapeDtypeStruct((dm, ft), bf16),
                         jax.ShapeDtypeStruct((ft, dm), bf16)]
    out = pl.pallas_call(
        functools.partial(_ffn_cast_kernel, final=final, has_tail=has_tail),
        grid=(1, 1, n_full + int(has_tail)),
        in_specs=[pl.BlockSpec((None, tx, dm), lambda b, i, f: (b, i, 0)),
                  _mod_spec(True, tx, k0), _mod_spec(True, tx, k0 + 1), _mod_spec(True, tx, k0 + 2),
                  pl.BlockSpec((1, dm), lambda b, i, f: (0, 0)),
                  pl.BlockSpec((1, dm), lambda b, i, f: (0, 0))] + w_in_specs,
        out_specs=[pl.BlockSpec((None, tx, dm), lambda b, i, f: (b, i, 0))] + w_out_specs,
        out_shape=[jax.ShapeDtypeStruct(x.shape, f32)] + w_out_shapes,
        scratch_shapes=[pltpu.VMEM((tx, dm), bf16), pltpu.VMEM((tx, dm), f32)],
        compiler_params=_cp(("arbitrary", "arbitrary", "arbitrary")),
        name="ffn_cast",
    )(x, mod, mod, mod, nw, fw, *args)
    return out[0], tuple(out[1:])


def _inproj_kernel(x_ref, sh_ref, sc_ref, nw_ref, w_ref, ws_ref, o_ref, os_ref, h_ref):
    @pl.when(pl.program_id(2) == 0)
    def _():
        y = _rms(x_ref[...], nw_ref[...])
        h = (y * (1.0 + sc_ref[...]) + sh_ref[...]).astype(bf16)
        h_ref[...] = h
        os_ref[...] = _dot_nt(h, ws_ref[...].astype(bf16))

    o_ref[...] = _dot_nt(h_ref[...], w_ref[...]).astype(o_ref.dtype)


def _inproj(x, mod, nw, w, ws, l, per_row, tm, out_dtype):
    bx, tx, dm = x.shape
    n = w.shape[1]
    tn = INPROJ_COLS
    return pl.pallas_call(
        _inproj_kernel,
        grid=(bx, tx // tm, n // tn),
        in_specs=[pl.BlockSpec((None, tm, dm), lambda b, i, j: (b, i, 0)),
                  _mod_spec(per_row, tm, 3), _mod_spec(per_row, tm, 4),
                  pl.BlockSpec((1, dm), lambda b, i, j: (0, 0)),
                  pl.BlockSpec((None, tn, dm), lambda b, i, j: (l, j, 0)),
                  pl.BlockSpec((None, LANE, dm), lambda b, i, j: (l, 0, 0))],
        out_specs=[pl.BlockSpec((None, tm, tn), lambda b, i, j: (b, i, j)),
                   pl.BlockSpec((None, tm, LANE), lambda b, i, j: (b, i, 0))],
        out_shape=[jax.ShapeDtypeStruct((bx, tx, n), out_dtype),
                   jax.ShapeDtypeStruct((bx, tx, LANE), f32)],
        scratch_shapes=[pltpu.VMEM((tm, dm), bf16)],
        compiler_params=_cp(("parallel", "parallel", "arbitrary")),
        name="inproj",
    )(x, mod, mod, nw, w, ws)


def _conv_load(ext_ref, x_ref, cst_ref, first):
    c = x_ref.shape[0]
    nblk = ext_ref.shape[0]

    @pl.when(first)
    def _():
        for j in range(nblk):
            ext_ref[j, pl.ds(5, CONV_W - 1), :] = cst_ref[:, j * LANE:(j + 1) * LANE]

    for j in range(nblk):
        ext_ref[j, pl.ds(8, c), :] = x_ref[:, j * LANE:(j + 1) * LANE].astype(f32)


def _conv_cols(ext_ref, w_ref, b_ref, lo, width, c):
    n = c // SUB
    cols = []
    for blk in range(lo // LANE, (lo + width) // LANE):
        sl = pl.ds(blk * LANE, LANE)
        w = w_ref[:, sl]
        taps = [ext_ref[blk, pl.ds(8 - (CONV_W - 1) + k, n, stride=SUB), :] for k in range(SUB + CONV_W - 1)]
        ys = []
        for r in range(SUB):
            y = taps[r] * w[0:1]
            for j in range(1, CONV_W):
                y = y + taps[r + j] * w[j:j + 1]
            ys.append(y)
        y = jnp.concatenate(ys, axis=0)
        if b_ref is not None:
            y = y + b_ref[:, sl]
        cols.append(_silu(y))
    return cols[0] if len(cols) == 1 else jnp.concatenate(cols, axis=1)


def _conv_carry(ext_ref, c):
    for j in range(ext_ref.shape[0]):
        ext_ref[j, pl.ds(5, CONV_W - 1), :] = ext_ref[j, pl.ds(8 + c - (CONV_W - 1), CONV_W - 1), :]


def _cumsum_rows(x, tri):
    return _dot_hi(jnp.where(tri, 1.0, 0.0).astype(f32), x)


def _gdn_kernel(qkv_ref, z_ref, sm_ref, cst_ref, cw_ref, alog_ref, dtb_ref, nw_ref, s0_ref,
                o_ref, s_ref, ext_ref, nat_ref):
    c = qkv_ref.shape[0]
    first = pl.program_id(1) == 0

    @pl.when(first)
    def _():
        s_ref[...] = s0_ref[...]

    _conv_load(ext_ref, qkv_ref, cst_ref, first)

    sm = _load_residue(sm_ref, c)
    beta_l = _sigmoid(sm)
    g_l = -jnp.exp(alog_ref[...]) * jax.nn.softplus(sm + dtb_ref[...])
    tri, strict, eye = _tri_masks_residue(c)
    gc_all = _cumsum_rows(g_l, tri)
    gc_t = gc_all.T
    eye_f = jnp.where(eye, 1.0, 0.0).astype(f32)
    n_fac = max(1, (c - 1).bit_length() - 1)

    hs = range(GDN_H)
    qb, kb, ks, decay, eg, gcs, nm, rhs = [], [], [], [], [], [], [], []
    for h in hs:
        q = _conv_cols(ext_ref, cw_ref, None, h * GDN_DK, GDN_DK, c)
        k = _conv_cols(ext_ref, cw_ref, None, GDN_QK + h * GDN_DK, GDN_DK, c)
        v = _conv_cols(ext_ref, cw_ref, None, 2 * GDN_QK + h * GDN_DV, GDN_DV, c)
        q = q * lax.rsqrt(jnp.sum(q * q, axis=-1, keepdims=True) + EPS) * (GDN_DK ** -0.5)
        k = k * lax.rsqrt(jnp.sum(k * k, axis=-1, keepdims=True) + EPS)
        beta = beta_l[:, SM_BETA + h:SM_BETA + h + 1]
        gc = gc_all[:, SM_DEC + h:SM_DEC + h + 1]
        d = jnp.exp(jnp.where(tri, gc - gc_t[SM_DEC + h:SM_DEC + h + 1, :], -jnp.inf))
        e = jnp.exp(gc)
        qb.append(q.astype(bf16))
        kb.append(k.astype(bf16))
        ks.append(k)
        decay.append(d)
        eg.append(e)
        gcs.append(gc)
        nm.append(jnp.where(strict, beta * d * _dot_nt(kb[h], kb[h]), 0.0))
        rhs.append(jnp.concatenate([v * beta, k * (beta * e)], axis=1))
        yield

    inv = [eye_f - nm[h] for h in hs]
    pw_s = [_split(nm[h]) for h in hs]
    pw_s = [_split(_dot3(pw_s[h], pw_s[h])) for h in hs]
    yield
    for i in range(n_fac):
        inv = [inv[h] + _dot3(_split(inv[h]), pw_s[h]) for h in hs]
        yield
        if i + 1 < n_fac:
            pw_s = [_split(_dot3(pw_s[h], pw_s[h])) for h in hs]
            yield
    sol = []
    for h in hs:
        inv_hi, inv_lo = _split(inv[h])
        rb = rhs[h].astype(bf16)
        sol.append(_dot(inv_hi, rb) + _dot(inv_lo, rb))
    yield

    s = [s_ref[h] for h in hs]
    sb = [s[h].astype(bf16) for h in hs]
    wb = [(sol[h][:, 0:GDN_DV] - _dot(sol[h][:, GDN_DV:].astype(bf16), sb[h])).astype(bf16) for h in hs]
    yield
    qk = [(_dot_nt(qb[h], kb[h]) * decay[h]).astype(bf16) for h in hs]
    yield
    o = [_dot(qb[h], sb[h]) * eg[h] + _dot(qk[h], wb[h]) for h in hs]
    yield
    for h in hs:
        g_last = gcs[h][c - 1:c]
        kd = (ks[h] * jnp.exp(g_last - gcs[h])).astype(bf16)
        s_ref[h] = s[h] * jnp.exp(g_last) + _dot_tn(kd, wb[h])
    yield
    for h in hs:
        sl = slice(h * GDN_DV, (h + 1) * GDN_DV)
        _store_natural(nat_ref, _rms(o[h], nw_ref[...]), c, h * GDN_DV)
        o_ref[:, sl] = (_read_natural(nat_ref, h * GDN_DV, GDN_DV) * _silu(z_ref[:, sl].astype(f32))).astype(bf16)
        yield

    _conv_carry(ext_ref, c)


def _tok(width, off):
    return pl.BlockSpec((None, CHUNK, width), lambda bi, ti: (bi, ti, _blk(off, width)))


def _per_seq(*shape):
    return pl.BlockSpec((None,) + shape, lambda bi, ti: (bi,) + (0,) * len(shape))


def _const(*shape):
    return pl.BlockSpec(shape, lambda bi, ti: (0,) * len(shape))


N_GDN_IN, N_GLA_IN, N_SSD_IN = 9, 9, 11
_DONE = object()


def _mixers_kernel(*refs):
    i0, i1, i2 = N_GDN_IN, N_GDN_IN + N_GLA_IN, N_GDN_IN + N_GLA_IN + N_SSD_IN
    gdn_in, gla_in, ssd_in = refs[:i0], refs[i0:i1], refs[i1:i2]
    oa_ref, sa_ref, ob_ref, sb_ref, oc_ref, sc_ref, ext_a_ref, st_b_ref, ext_c_ref, nat_a_ref, nat_c_ref = refs[i2:]
    live = [_gdn_kernel(*gdn_in, oa_ref, sa_ref, ext_a_ref, nat_a_ref),
            _gla_kernel(*gla_in, ob_ref, sb_ref, st_b_ref),
            _ssd_kernel(*ssd_in, oc_ref, sc_ref, ext_c_ref, nat_c_ref)]
    while live:
        for gen in list(live):
            if next(gen, _DONE) is _DONE:
                live.remove(gen)


GLA_SUB = 16


def _gla_kernel(q_ref, k_ref, v_ref, r_ref, sm_ref, wgate_ref, bgate_ref, nw_ref, s0_ref,
                o_ref, so_ref, st_ref):
    c = q_ref.shape[0]
    ti = pl.program_id(1)

    @pl.when(ti == 0)
    def _():
        for h in range(GLA_H):
            st_ref[h] = s0_ref[h].T

    sm = sm_ref[...]
    lane = lax.broadcasted_iota(jnp.int32, sm.shape, 1)
    lr = jnp.where((lane >= SM_LR) & (lane < SM_LR + GLA_RANK), sm, 0.0).astype(bf16)
    la = jax.nn.log_sigmoid(_dot(lr, wgate_ref[...]) + bgate_ref[...]) / GLA_TAU
    tri, _, _ = _tri_masks(c)
    b_all = _cumsum_rows(la, tri)

    col = lax.broadcasted_iota(jnp.int32, (GLA_SUB, c), 1)
    row = lax.broadcasted_iota(jnp.int32, (GLA_SUB, c), 0)
    hs = range(GLA_H)
    qs, ks, vbs, bs, atts = [], [], [], [], []
    for h in hs:
        sk = slice(h * GLA_DK, (h + 1) * GLA_DK)
        q = q_ref[:, sk].astype(f32) * (GLA_DK ** -0.5)
        k = k_ref[:, sk].astype(f32)
        b = b_all[:, sk]
        qs.append(q)
        ks.append(k)
        bs.append(b)
        vbs.append(v_ref[:, h * GLA_DV:(h + 1) * GLA_DV].astype(bf16))
        att_rows = []
        for i in range(c // GLA_SUB):
            lo = i * GLA_SUB
            b_i = b[lo:lo + GLA_SUB]
            q_i = q[lo:lo + GLA_SUB]
            b_top = b[lo:lo + 1]
            att = jnp.zeros((GLA_SUB, c), f32)
            if i > 0:
                q_t = (q_i * jnp.exp(b_i - b_top)).astype(bf16)
                k_t = (k * jnp.exp(jnp.minimum(b_top - b, 0.0))).astype(bf16)
                att = jnp.where(col < lo, _dot_nt(q_t, k_t), 0.0)
            for sl in range(GLA_SUB):
                sidx = lo + sl
                e = jnp.exp(jnp.minimum(b_i - b[sidx:sidx + 1], 0.0))
                p = jnp.sum(q_i * k[sidx:sidx + 1] * e, axis=1, keepdims=True)
                att = jnp.where((col == sidx) & (row >= sl), p, att)
            att_rows.append(att)
            yield
        atts.append(jnp.concatenate(att_rows, axis=0).astype(bf16))

    sts = [st_ref[h] for h in hs]
    os_ = [_dot(atts[h], vbs[h]) + _dot_nt((qs[h] * jnp.exp(bs[h])).astype(bf16), sts[h].astype(bf16))
           for h in hs]
    yield
    for h in hs:
        b_last = bs[h][c - 1:c]
        kd = (ks[h] * jnp.exp(b_last - bs[h])).astype(bf16)
        st_ref[h] = sts[h] * jnp.exp(b_last) + _dot_tn(vbs[h], kd)
        yield
    for h in hs:
        sv = slice(h * GLA_DV, (h + 1) * GLA_DV)
        o_ref[:, sv] = (_rms(os_[h], nw_ref[...]) * _silu(r_ref[:, sv].astype(f32))).astype(bf16)
        yield

    @pl.when(ti == pl.num_programs(1) - 1)
    def _():
        for h in hs:
            so_ref[h] = st_ref[h].T


def _ssd_kernel(z_ref, xbc_ref, sm_ref, cst_ref, cw_ref, cb_ref, avec_ref, dtb_ref, dvec_ref, nw_ref, h0_ref,
                o_ref, h_ref, ext_ref, nat_ref):
    c = xbc_ref.shape[0]
    assert c == SSD_P, "head pairs share a lane tile: the chunk must be as wide as a head"
    first = pl.program_id(1) == 0

    @pl.when(first)
    def _():
        h_ref[...] = h0_ref[...]

    _conv_load(ext_ref, xbc_ref, cst_ref, first)

    dt_l = jax.nn.softplus(_load_residue(sm_ref, c) + dtb_ref[...])
    tri, _, _ = _tri_masks_residue(c)
    ac_all = _cumsum_rows(dt_l * avec_ref[...], tri)
    ac_t2 = jnp.concatenate([ac_all, ac_all], axis=0).T

    lane = lax.broadcasted_iota(jnp.int32, (c, 2 * SSD_P), 1)
    lo_half = lane < SSD_P
    tri2, _, _ = _tri_masks_residue(c, 2 * SSD_P)
    lane1 = lane[0:1]
    gs_ = range(SSD_G)
    ps_ = range(SSD_HG // 2)

    xs_, bmb, hg, cb2, ch = [], [], [], [], []
    for g in gs_:
        xs_.append(_conv_cols(ext_ref, cw_ref, cb_ref, g * SSD_GW, SSD_GW, c))
        bmb.append(_conv_cols(ext_ref, cw_ref, cb_ref, SSD_INNER + g * SSD_N, SSD_N, c).astype(bf16))
        cmb = _conv_cols(ext_ref, cw_ref, cb_ref, SSD_INNER + SSD_BC + g * SSD_N, SSD_N, c).astype(bf16)
        hg.append(h_ref[pl.ds(g * SSD_HG, SSD_HG)].reshape(SSD_GW, SSD_N))
        cb2.append(_dot_nt(cmb, jnp.concatenate([bmb[g], bmb[g]], axis=0)))
        ch.append(_dot_nt(cmb, hg[g].astype(bf16)))
        yield

    m2, rhs, e_col, xsc = {}, {}, {}, {}
    for g in gs_:
        for p in ps_:
            l0 = SM_DT + g * SSD_HG + 2 * p
            ps = slice(p * 2 * SSD_P, (p + 1) * 2 * SSD_P)
            ac_col = jnp.where(lo_half, ac_all[:, l0:l0 + 1], ac_all[:, l0 + 1:l0 + 2])
            ac_row = jnp.where(lane1 < SSD_P, ac_t2[l0:l0 + 1, :], ac_t2[l0 + 1:l0 + 2, :])
            decay = jnp.exp(jnp.where(tri2, ac_col - ac_row, -jnp.inf))
            m2[g, p] = (cb2[g] * decay).astype(bf16)
            dt2 = jnp.where(lo_half, dt_l[:, l0:l0 + 1], dt_l[:, l0 + 1:l0 + 2])
            xdt = xs_[g][:, ps] * dt2
            rhs[g, p] = jnp.concatenate([jnp.where(lo_half, xdt, 0.0), jnp.where(lo_half, 0.0, xdt)],
                                        axis=0).astype(bf16)
            e_col[g, p] = jnp.exp(ac_col)
            xsc[g, p] = xdt * jnp.exp(ac_col[c - 1:c] - ac_col)
            yield
    y2 = {gp: _dot(m2[gp], rhs[gp]) for gp in m2}
    yield

    for g in gs_:
        xsg = jnp.concatenate([xsc[g, p] for p in ps_], axis=1).astype(bf16)
        dh = _dot_tn(xsg, bmb[g])
        for j in range(SSD_HG):
            ln = SM_DT + g * SSD_HG + j
            h_ref[g * SSD_HG + j] = (hg[g][j * SSD_P:(j + 1) * SSD_P] * jnp.exp(ac_all[c - 1:c, ln:ln + 1])
                                     + dh[j * SSD_P:(j + 1) * SSD_P])
        yield
    for g in gs_:
        gs = slice(g * SSD_GW, (g + 1) * SSD_GW)
        y = jnp.concatenate([y2[g, p] + ch[g][:, p * 2 * SSD_P:(p + 1) * 2 * SSD_P] * e_col[g, p] for p in ps_],
                            axis=1)
        _store_natural(nat_ref, y + dvec_ref[:, gs] * xs_[g], c, g * SSD_GW)
        y = _read_natural(nat_ref, g * SSD_GW, SSD_GW) * _silu(z_ref[:, gs].astype(f32))
        o_ref[:, gs] = _rms(y, nw_ref[:, gs]).astype(bf16)
        yield

    _conv_carry(ext_ref, c)


STEP_B = 8


def _conv_step(x_ref, c_ref, w_ref):
    w = w_ref[...]
    return c_ref[0] * w[0:1] + c_ref[1] * w[1:2] + c_ref[2] * w[2:3] + x_ref[...] * w[3:4]


def _expand_matrix(nb, width):
    r = lax.broadcasted_iota(jnp.int32, (nb, nb * width), 0)
    c = lax.broadcasted_iota(jnp.int32, (nb, nb * width), 1)
    return jnp.where((c >= r * width) & (c < (r + 1) * width), 1.0, 0.0).astype(bf16)


def _bcast_cols(x, e_mat):
    hi, lo = _split(x)
    return _dot_tn(hi, e_mat) + _dot_tn(lo, e_mat)


def _state_call(kernel, grid, in_specs, out_specs, out_shape, scratch, sem, name, args, prev_state):
    aliases = {}
    if prev_state is not None:
        in_specs = in_specs + [pl.BlockSpec(memory_space=pl.ANY)]
        args = args + (prev_state,)
        aliases = {len(args) - 1: len(out_shape) - 1}
        kernel = functools.partial(_drop_alias_ref, kernel, len(args) - 1)
    return pl.pallas_call(kernel, grid=grid, in_specs=in_specs, out_specs=out_specs, out_shape=out_shape,
                          scratch_shapes=scratch, input_output_aliases=aliases,
                          compiler_params=_cp(sem), name=name)(*args)


def _drop_alias_ref(kernel, idx, *refs):
    return kernel(*refs[:idx], *refs[idx + 1:])


def _gdn_step_kernel(alog_ref, dtb_ref, q_ref, k_ref, v_ref, z_ref, sm_ref, cq_ref, ck_ref, cv_ref,
                     wq_ref, wk_ref, wv_ref, nw_ref, s_ref, o_ref, so_ref):
    nb = q_ref.shape[0]
    qa = _silu(_conv_step(q_ref, cq_ref, wq_ref))
    ka = _silu(_conv_step(k_ref, ck_ref, wk_ref))
    va = _silu(_conv_step(v_ref, cv_ref, wv_ref))
    sm = sm_ref[...]
    z = z_ref[...]
    e_mat = _expand_matrix(nb, GDN_DV)
    for h in range(GDN_H):
        sl = slice(h * GDN_DK, (h + 1) * GDN_DK)
        q = qa[:, sl]
        k = ka[:, sl]
        v = va[:, sl]
        q = q * lax.rsqrt(jnp.sum(q * q, axis=-1, keepdims=True) + EPS) * (GDN_DK ** -0.5)
        k = k * lax.rsqrt(jnp.sum(k * k, axis=-1, keepdims=True) + EPS)
        beta = _sigmoid(sm[:, SM_BETA + h:SM_BETA + h + 1])
        g = -jnp.exp(_scalar_vec(alog_ref[h])) * jax.nn.softplus(sm[:, SM_DEC + h:SM_DEC + h + 1] + dtb_ref[h])
        eg = jnp.exp(g)
        qk = jnp.sum(q * k, axis=-1, keepdims=True)
        kb = _bcast_cols(k, e_mat)
        qb = _bcast_cols(q, e_mat)
        blk = lambda m, b: m[:, b * GDN_DV:(b + 1) * GDN_DV]
        ks = jnp.concatenate([jnp.sum(s_ref[b, h] * blk(kb, b), axis=0, keepdims=True) for b in range(nb)], axis=0)
        qs = jnp.concatenate([jnp.sum(s_ref[b, h] * blk(qb, b), axis=0, keepdims=True) for b in range(nb)], axis=0)
        w = beta * v - (beta * eg) * ks
        o = qs * eg + qk * w
        for b in range(nb):
            so_ref[b, h] = s_ref[b, h] * eg[b:b + 1] + blk(kb, b) * w[b:b + 1]
        o = _rms(o, nw_ref[...]) * _silu(z[:, sl])
        o_ref[:, sl] = o.astype(bf16)


def _gdn_step(proj, small, conv_t, states, prev, l, conv_w, a_log, dt_bias, norm_w):
    n = proj.shape[0]
    nb = STEP_B
    blk = lambda off: pl.BlockSpec((nb, GDN_QK), lambda i: (i, _blk(off, GDN_QK)))
    cst = lambda j: pl.BlockSpec((CONV_W - 1, nb, GDN_QK), lambda i: (0, i, j))
    cw = lambda j: pl.BlockSpec((CONV_W, GDN_QK), lambda i: (0, j))
    smem = pl.BlockSpec(memory_space=pltpu.SMEM)
    st = pl.BlockSpec((None, nb, GDN_H, GDN_DK, GDN_DV), lambda i: (l, i, 0, 0, 0))
    return _state_call(
        _gdn_step_kernel, (n // nb,),
        [smem, smem, blk(P_QKV_A), blk(P_QKV_A + GDN_QK), blk(P_QKV_A + 2 * GDN_QK), blk(P_Z_A),
         pl.BlockSpec((nb, LANE), lambda i: (i, 0)),
         cst(0), cst(1), cst(2), cw(0), cw(1), cw(2),
         pl.BlockSpec((1, GDN_DV), lambda i: (0, 0)), st],
        [pl.BlockSpec((nb, GDN_V), lambda i: (i, 0)), st],
        [jax.ShapeDtypeStruct((n, GDN_V), bf16), jax.ShapeDtypeStruct(states.shape, f32)],
        [], ("parallel",), "gdn_step",
        (a_log, dt_bias, proj, proj, proj, proj, small, conv_t, conv_t, conv_t, conv_w, conv_w, conv_w, norm_w, states),
        prev)


def _gla_step_kernel(q_ref, k_ref, v_ref, r_ref, sm_ref, wgate_ref, bgate_ref, nw_ref, s_ref,
                     o_ref, so_ref):
    nb = q_ref.shape[0]
    sm = sm_ref[...]
    lane = lax.broadcasted_iota(jnp.int32, sm.shape, 1)
    lr = jnp.where((lane >= SM_LR) & (lane < SM_LR + GLA_RANK), sm, 0.0).astype(bf16)
    la_all = jax.nn.log_sigmoid(_dot(lr, wgate_ref[...]) + bgate_ref[...]) / GLA_TAU
    qa = q_ref[...] * (GLA_DK ** -0.5)
    ka = k_ref[...]
    va = v_ref[...]
    r = r_ref[...]
    for h in range(GLA_H):
        sk = slice(h * GLA_DK, (h + 1) * GLA_DK)
        sv = slice(h * GLA_DV, (h + 1) * GLA_DV)
        q = qa[:, sk]
        k = ka[:, sk]
        v = va[:, sv]
        e = jnp.exp(la_all[:, sk])
        qk = jnp.sum(q * k, axis=-1, keepdims=True)
        e_t = e.T
        k_t = k.T
        qe_t = (q * e).T
        rows = []
        for b in range(nb):
            s = s_ref[b, h]
            vrow = v[b:b + 1]
            rows.append(qk[b:b + 1] * vrow + jnp.sum(s * qe_t[:, b:b + 1], axis=0, keepdims=True))
            so_ref[b, h] = s * e_t[:, b:b + 1] + k_t[:, b:b + 1] * vrow
        o = jnp.concatenate(rows, axis=0)
        o = _rms(o, nw_ref[...]) * _silu(r[:, sv])
        o_ref[:, sv] = o.astype(bf16)


def _gla_step(proj, small, states, prev, l, wgate_pad, bgate, norm_w):
    n = proj.shape[0]
    nb = STEP_B
    st = pl.BlockSpec((None, nb, GLA_H, GLA_DK, GLA_DV), lambda i: (l, i, 0, 0, 0))
    return _state_call(
        _gla_step_kernel, (n // nb,),
        [pl.BlockSpec((nb, GLA_QK), lambda i: (i, _blk(P_Q_B, GLA_QK))),
         pl.BlockSpec((nb, GLA_QK), lambda i: (i, _blk(P_K_B, GLA_QK))),
         pl.BlockSpec((nb, GLA_V), lambda i: (i, _blk(P_V_B, GLA_V))),
         pl.BlockSpec((nb, GLA_V), lambda i: (i, _blk(P_R_B, GLA_V))),
         pl.BlockSpec((nb, LANE), lambda i: (i, 0)),
         pl.BlockSpec((LANE, GLA_QK), lambda i: (0, 0)),
         pl.BlockSpec((1, GLA_QK), lambda i: (0, 0)),
         pl.BlockSpec((1, GLA_DV), lambda i: (0, 0)), st],
        [pl.BlockSpec((nb, GLA_V), lambda i: (i, 0)), st],
        [jax.ShapeDtypeStruct((n, GLA_V), bf16), jax.ShapeDtypeStruct(states.shape, f32)],
        [], ("parallel",), "gla_step",
        (proj, proj, proj, proj, small, wgate_pad, bgate, norm_w, states),
        prev)


def _ssd_step_kernel(alog_ref, dtb_ref, z_ref, xbc_ref, sm_ref, cst_ref, cw_ref, cb_ref, dvec_ref, nw_ref, h_ref,
                     o_ref, ho_ref):
    nb = xbc_ref.shape[0]
    xbc = _silu(_conv_step(xbc_ref, cst_ref, cw_ref) + cb_ref[...])
    sm_t = sm_ref[...].T
    e_mat = _expand_matrix(nb, SSD_N)
    lane_b = lax.broadcasted_iota(jnp.int32, (SSD_P, nb), 1)
    for g in range(SSD_G):
        gs = slice(g * SSD_GW, (g + 1) * SSD_GW)
        bm = xbc[:, SSD_INNER + g * SSD_N:SSD_INNER + (g + 1) * SSD_N]
        cm_t = xbc[:, SSD_INNER + SSD_BC + g * SSD_N:SSD_INNER + SSD_BC + (g + 1) * SSD_N].T
        cb_row = jnp.sum(cm_t * bm.T, axis=0, keepdims=True)
        hg = h_ref[:, pl.ds(g * SSD_HG, SSD_HG)].reshape(nb * SSD_GW, SSD_N)
        ch = _dot(hg.astype(bf16), cm_t.astype(bf16))
        pairs = []
        for jp in range(SSD_HG // 2):
            x_t = xbc[:, g * SSD_GW + jp * LANE:g * SSD_GW + (jp + 1) * LANE].T
            halves = []
            for jj in range(2):
                j = 2 * jp + jj
                hd = g * SSD_HG + j
                dt_row = jax.nn.softplus(sm_t[SM_DT + hd:SM_DT + hd + 1, :] + dtb_ref[hd])
                ea_row = jnp.exp(dt_row * (-jnp.exp(_scalar_vec(alog_ref[hd]))))
                xdt_t = x_t[jj * SSD_P:(jj + 1) * SSD_P] * dt_row
                hi, lo = _split(xdt_t)
                xb = _dot(hi, e_mat) + _dot(lo, e_mat)
                yh = jnp.zeros((SSD_P, nb), f32)
                for b in range(nb):
                    r0 = (b * SSD_HG + j) * SSD_P
                    yh = jnp.where(lane_b == b, ch[r0:r0 + SSD_P], yh)
                    ho_ref[b, hd] = (h_ref[b, hd] * ea_row[:, b:b + 1]
                                     + xb[:, b * SSD_N:(b + 1) * SSD_N] * bm[b:b + 1])
                halves.append(yh * ea_row + cb_row * xdt_t)
            pairs.append(jnp.concatenate(halves, axis=0).T)
        y = jnp.concatenate(pairs, axis=1)
        y = (y + dvec_ref[:, gs] * xbc[:, gs]) * _silu(z_ref[:, gs])
        o_ref[:, gs] = _rms(y, nw_ref[:, gs]).astype(bf16)


def _ssd_step(proj, small, conv_t, states, prev, l, conv_w, conv_b, a_log, dt_bias, dvec, norm_w):
    n = proj.shape[0]
    nb = STEP_B
    smem = pl.BlockSpec(memory_space=pltpu.SMEM)
    const = lambda shape: pl.BlockSpec(shape, lambda i: (0,) * len(shape))
    st = pl.BlockSpec((None, nb, SSD_H, SSD_P, SSD_N), lambda i: (l, i, 0, 0, 0))
    return _state_call(
        _ssd_step_kernel, (n // nb,),
        [smem, smem,
         pl.BlockSpec((nb, SSD_INNER), lambda i: (i, _blk(P_Z_C, SSD_INNER))),
         pl.BlockSpec((nb, SSD_CONV), lambda i: (i, _blk(P_XBC, SSD_CONV))),
         pl.BlockSpec((nb, LANE), lambda i: (i, 0)),
         pl.BlockSpec((CONV_W - 1, nb, SSD_CONV), lambda i: (0, i, 0)),
         const((CONV_W, SSD_CONV)), const((1, SSD_CONV)), const((1, SSD_INNER)), const((1, SSD_INNER)), st],
        [pl.BlockSpec((nb, SSD_INNER), lambda i: (i, 0)), st],
        [jax.ShapeDtypeStruct((n, SSD_INNER), bf16), jax.ShapeDtypeStruct(states.shape, f32)],
        [], ("parallel",), "ssd_step",
        (a_log, dt_bias, proj, proj, small, conv_t, conv_w, conv_b, dvec, norm_w, states),
        prev)


def _merge_kernel(oa_ref, ob_ref, oc_ref, ga_ref, gb_ref, gc_ref, wa_ref, wb_ref, wc_ref, o_ref):
    m = (_sigmoid(ga_ref[...].astype(f32)) * _dot(oa_ref[...], wa_ref[...])
         + _sigmoid(gb_ref[...].astype(f32)) * _dot(ob_ref[...], wb_ref[...])
         + _sigmoid(gc_ref[...].astype(f32)) * _dot(oc_ref[...], wc_ref[...]))
    o_ref[...] = m.astype(bf16)


def _outproj_kernel(m_ref, x_ref, gt_ref, w_ref, o_ref):
    o_ref[...] = x_ref[...] + gt_ref[...] * _dot(m_ref[...], w_ref[...])


def _mixout(oa, ob, oc, proj, x, mod, wa, wb, wc, wo, l, per_row, tm):
    bx, tx, dm = x.shape
    tn = MERGE_COLS
    gate = lambda k: pl.BlockSpec((None, tm, tn), lambda b, i, j: (b, i, _blk(P_GATES + k * D_MODEL, tn) + j))
    act = lambda w: pl.BlockSpec((None, tm, w), lambda b, i, j: (b, i, 0))
    wsp = lambda w: pl.BlockSpec((None, w, tn), lambda b, i, j: (l, 0, j))
    merged = pl.pallas_call(
        _merge_kernel,
        grid=(bx, tx // tm, dm // tn),
        in_specs=[act(GDN_V), act(GLA_V), act(SSD_INNER), gate(0), gate(1), gate(2),
                  wsp(GDN_V), wsp(GLA_V), wsp(SSD_INNER)],
        out_specs=pl.BlockSpec((None, tm, tn), lambda b, i, j: (b, i, j)),
        out_shape=jax.ShapeDtypeStruct((bx, tx, dm), bf16),
        compiler_params=_cp(("parallel", "parallel", "arbitrary")),
        name="merge",
    )(oa, ob, oc, proj, proj, proj, wa, wb, wc)
    to = OUTPROJ_COLS
    nj = dm // to
    if per_row:
        gspec = pl.BlockSpec((None, tm, to), lambda b, i, j: (b, i, 5 * nj + j))
    else:
        gspec = pl.BlockSpec((None, 1, to), lambda b, i, j: (b, 0, 5 * nj + j))
    return pl.pallas_call(
        _outproj_kernel,
        grid=(bx, tx // tm, nj),
        in_specs=[pl.BlockSpec((None, tm, dm), lambda b, i, j: (b, i, 0)),
                  pl.BlockSpec((None, tm, to), lambda b, i, j: (b, i, j)),
                  gspec,
                  pl.BlockSpec((None, dm, to), lambda b, i, j: (l, 0, j))],
        out_specs=pl.BlockSpec((None, tm, to), lambda b, i, j: (b, i, j)),
        out_shape=jax.ShapeDtypeStruct(x.shape, f32),
        compiler_params=_cp(("parallel", "parallel", "arbitrary")),
        name="outproj",
    )(merged, x, mod, wo)


REGROUP_ROWS = 512


def _regroup_kernel(w_ref, o_ref):
    o_ref[...] = w_ref[0].astype(bf16)


def _regroup_rows(wt):
    nl, _, d = wt.shape
    tr = REGROUP_ROWS
    shifts = []
    for name in MAIN_ORDER:
        a, b = W_IN_SRC[name]
        assert (b - a) % tr == 0 and a % SUB == 0
        shifts.append((P_MAIN[name] // tr, a - P_MAIN[name]))

    def src_row(i):
        off = i * (tr // SUB) + shifts[0][1] // SUB
        for k in range(1, len(shifts)):
            off = off + jnp.where(i >= shifts[k][0], (shifts[k][1] - shifts[k - 1][1]) // SUB, 0)
        return off * SUB

    return pl.pallas_call(
        _regroup_kernel,
        grid=(nl, P_TOTAL // tr),
        in_specs=[pl.BlockSpec((pl.Element(1), pl.Element(tr), pl.Element(d)), lambda l, i: (l, src_row(i), 0))],
        out_specs=pl.BlockSpec((None, tr, d), lambda l, i: (l, i, 0)),
        out_shape=jax.ShapeDtypeStruct((nl, P_TOTAL, d), bf16),
        compiler_params=_cp(("parallel", "parallel")),
        name="regroup_w_in",
    )(wt)


def _permute_w_in(w):
    nl, d, _ = w.shape
    main = _regroup_rows(jnp.swapaxes(w, 1, 2))
    small = jnp.concatenate([w[:, :, W_IN_SRC[name][0]:W_IN_SRC[name][1]] for name in SMALL_ORDER], axis=2)
    small = jnp.pad(jnp.swapaxes(small, 1, 2), ((0, 0), (0, LANE - small.shape[2]), (0, 0)))
    return main, small


def _lane_vec(v, lo):
    return jnp.zeros((1, LANE), f32).at[0, lo:lo + v.shape[0]].set(v)


def _layer_params(l, p):
    row = lambda a: a[l].reshape(1, -1)
    wgate = jnp.zeros((LANE, GLA_QK), f32).at[SM_LR:SM_LR + GLA_RANK].set(p["gla_w_gate"][l]).astype(bf16)
    return dict(
        norm1=row(p["norm1"]), norm2=row(p["norm2"]), norm3=row(p["norm3"]),
        gdn_conv_w=p["gdn_conv_w"][l], gdn_a_log=p["gdn_a_log"][l], gdn_dt_bias=p["gdn_dt_bias"][l],
        gdn_alog_l=_lane_vec(p["gdn_a_log"][l], SM_DEC), gdn_dtb_l=_lane_vec(p["gdn_dt_bias"][l], SM_DEC),
        gdn_norm_w=row(p["gdn_norm_w"]),
        gla_wgate=wgate, gla_bgate=row(p["gla_b_gate"]), gla_norm_w=row(p["gla_norm_w"]),
        ssd_conv_w=p["ssd_conv_w"][l], ssd_conv_b=row(p["ssd_conv_b"]), ssd_a_log=p["ssd_a_log"][l],
        ssd_dt_bias=p["ssd_dt_bias"][l],
        ssd_avec_l=_lane_vec(-jnp.exp(p["ssd_a_log"][l]), SM_DT), ssd_dtb_l=_lane_vec(p["ssd_dt_bias"][l], SM_DT),
        ssd_dvec=jnp.repeat(p["ssd_d"][l], SSD_P).reshape(1, -1),
        ssd_norm_w=row(p["ssd_norm_w"]),
    )


def _stacked_weights(p):
    w_in, w_in_small = _permute_w_in(p["w_in"])
    return dict(
        f1=(p["ffn1_wg"], p["ffn1_wu"], p["ffn1_wd"]),
        f2=(p["ffn2_wg"], p["ffn2_wu"], p["ffn2_wd"]),
        w_in=w_in, w_in_small=w_in_small,
        wa=p["w_branch_gdn"].astype(bf16), wb=p["w_branch_gla"].astype(bf16),
        wc=p["w_branch_ssd"].astype(bf16), w_out=p["w_out"].astype(bf16),
    )


def _new_conv_state(buf, raw):
    t = raw.shape[1]
    k = CONV_W - 1
    if t >= k:
        return raw[:, t - k:]
    return jnp.concatenate([buf[:, t:], raw], axis=1)


def _mixer_prompt(proj, small, lp, st):
    gdn_conv, s_gdn, s_gla, ssd_conv, s_ssd = st
    b, t, _ = proj.shape
    c = CHUNK
    sm_spec = pl.BlockSpec((None, c, LANE), lambda bi, ti: (bi, ti, 0))
    st_a, st_b, st_c = _per_seq(GDN_H, GDN_DK, GDN_DV), _per_seq(GLA_H, GLA_DK, GLA_DV), _per_seq(SSD_H, SSD_P, SSD_N)
    gdn_specs = [_tok(GDN_CONV, P_QKV_A), _tok(GDN_V, P_Z_A), sm_spec, _per_seq(CONV_W - 1, GDN_CONV),
                 _const(CONV_W, GDN_CONV), _const(1, LANE), _const(1, LANE), _const(1, GDN_DV), st_a]
    gdn_args = (proj, proj, small, gdn_conv, lp["gdn_conv_w"], lp["gdn_alog_l"], lp["gdn_dtb_l"],
                lp["gdn_norm_w"], s_gdn)
    gla_specs = [_tok(GLA_QK, P_Q_B), _tok(GLA_QK, P_K_B), _tok(GLA_V, P_V_B), _tok(GLA_V, P_R_B), sm_spec,
                 _const(LANE, GLA_QK), _const(1, GLA_QK), _const(1, GLA_DV), st_b]
    gla_args = (proj, proj, proj, proj, small, lp["gla_wgate"], lp["gla_bgate"], lp["gla_norm_w"], s_gla)
    ssd_specs = [_tok(SSD_INNER, P_Z_C), _tok(SSD_CONV, P_XBC), sm_spec, _per_seq(CONV_W - 1, SSD_CONV),
                 _const(CONV_W, SSD_CONV), _const(1, SSD_CONV), _const(1, LANE), _const(1, LANE),
                 _const(1, SSD_INNER), _const(1, SSD_INNER), st_c]
    ssd_args = (proj, proj, small, ssd_conv, lp["ssd_conv_w"], lp["ssd_conv_b"], lp["ssd_avec_l"],
                lp["ssd_dtb_l"], lp["ssd_dvec"], lp["ssd_norm_w"], s_ssd)
    assert (len(gdn_specs), len(gla_specs), len(ssd_specs)) == (N_GDN_IN, N_GLA_IN, N_SSD_IN)
    out_tok = lambda width: pl.BlockSpec((None, c, width), lambda bi, ti: (bi, ti, 0))
    oa, s_gdn_n, ob, s_gla_n, oc, s_ssd_n = pl.pallas_call(
        _mixers_kernel,
        grid=(b, t // c),
        in_specs=gdn_specs + gla_specs + ssd_specs,
        out_specs=[out_tok(GDN_V), st_a, out_tok(GLA_V), st_b, out_tok(SSD_INNER), st_c],
        out_shape=[jax.ShapeDtypeStruct((b, t, GDN_V), bf16), jax.ShapeDtypeStruct(s_gdn.shape, f32),
                   jax.ShapeDtypeStruct((b, t, GLA_V), bf16), jax.ShapeDtypeStruct(s_gla.shape, f32),
                   jax.ShapeDtypeStruct((b, t, SSD_INNER), bf16), jax.ShapeDtypeStruct(s_ssd.shape, f32)],
        scratch_shapes=[pltpu.VMEM((GDN_CONV // LANE, 8 + c, LANE), f32), pltpu.VMEM((GLA_H, GLA_DV, GLA_DK), f32),
                        pltpu.VMEM((SSD_CONV // LANE, 8 + c, LANE), f32),
                        pltpu.VMEM((GDN_V // LANE, c, LANE), f32), pltpu.VMEM((SSD_INNER // LANE, c, LANE), f32)],
        compiler_params=_cp(("parallel", "arbitrary")),
        name="mixers_prompt",
    )(*gdn_args, *gla_args, *ssd_args)
    gdn_conv_n = _new_conv_state(gdn_conv, proj[:, :, P_QKV_A:P_QKV_A + GDN_CONV]).astype(f32)
    ssd_conv_n = _new_conv_state(ssd_conv, proj[:, :, P_XBC:P_XBC + SSD_CONV]).astype(f32)
    return (oa, ob, oc), (gdn_conv_n, s_gdn_n, s_gla_n, ssd_conv_n, s_ssd_n)


def _mixer_sample(proj, small, lp, l, states, prev):
    n = proj.shape[1]
    p2 = proj.reshape(n, P_TOTAL)
    s2 = small.reshape(n, LANE)
    gdn_conv, ssd_conv = states[0][l], states[3][l]
    gct = jnp.swapaxes(gdn_conv, 0, 1)
    sct = jnp.swapaxes(ssd_conv, 0, 1)
    pv = (None,) * 5 if prev is None else prev
    oa, s_gdn_n = _gdn_step(p2, s2, gct, states[1], pv[1], l, lp["gdn_conv_w"], lp["gdn_a_log"],
                            lp["gdn_dt_bias"], lp["gdn_norm_w"])
    ob, s_gla_n = _gla_step(p2, s2, states[2], pv[2], l, lp["gla_wgate"], lp["gla_bgate"], lp["gla_norm_w"])
    oc, s_ssd_n = _ssd_step(p2, s2, sct, states[4], pv[4], l, lp["ssd_conv_w"], lp["ssd_conv_b"], lp["ssd_a_log"],
                            lp["ssd_dt_bias"], lp["ssd_dvec"], lp["ssd_norm_w"])
    raw = p2.reshape(n, 1, P_TOTAL)
    gdn_conv_n = _new_conv_state(gdn_conv, raw[:, :, P_QKV_A:P_QKV_A + GDN_CONV])
    ssd_conv_n = _new_conv_state(ssd_conv, raw[:, :, P_XBC:P_XBC + SSD_CONV])
    outs = tuple(o.reshape(1, n, -1) for o in (oa, ob, oc))
    return outs, (gdn_conv_n, s_gdn_n, s_gla_n, ssd_conv_n, s_ssd_n)


def _trunk(x, mods, lps, sw, ffn_w, states, per_row, tm, tm_in, final_w):
    nl = len(lps)
    per_layer = []
    prev = None

    def ffn(x, mod, k0, nw, name, l, fw, final):
        if per_row:
            x, ffn_w[l, name] = _ffn_cast(x, mod, k0, nw, *sw[name], l, fw, final)
            return x
        return _ffn(x, mod, k0, nw, ffn_w[l, name], fw, per_row, tm, final)

    for l in range(nl):
        lp, mod = lps[l], mods[l]
        last = l == nl - 1
        x = ffn(x, mod, 0, lp["norm1"], "f1", l, lp["norm1"], False)
        proj, small = _inproj(x, mod, lp["norm2"], sw["w_in"], sw["w_in_small"], l, per_row, tm_in,
                              f32 if per_row else bf16)
        if per_row:
            (oa, ob, oc), st = _mixer_sample(proj, small, lp, l, states, prev)
            prev = st
        else:
            (oa, ob, oc), st = _mixer_prompt(proj, small, lp, tuple(s[l] for s in states))
        per_layer.append(st)
        x = _mixout(oa, ob, oc, proj, x, mod, sw["wa"], sw["wb"], sw["wc"], sw["w_out"], l, per_row, tm_in)
        x = ffn(x, mod, 6, lp["norm3"], "f2", l, final_w if last else lp["norm3"], last)
    stack = lambda i: jnp.stack([st[i] for st in per_layer])
    if per_row:
        new_states = (stack(0), prev[1], prev[2], stack(3), prev[4])
    else:
        new_states = tuple(stack(i) for i in range(5))
    return x, new_states


def kernel(x_prompt, x_sample, state_gdn_conv, state_gdn, state_gla, state_ssd_conv, state_ssd, c_prompt, c_sample, w_ada, b_ada, norm1, norm2, norm3, ffn1_wg, ffn1_wu, ffn1_wd, ffn2_wg, ffn2_wu, ffn2_wd, w_in, gdn_conv_w, gdn_a_log, gdn_dt_bias, gdn_norm_w, gla_w_gate, gla_b_gate, gla_norm_w, ssd_conv_w, ssd_conv_b, ssd_a_log, ssd_dt_bias, ssd_d, ssd_norm_w, w_branch_gdn, w_branch_gla, w_branch_ssd, w_out, final_norm):
    p = dict(norm1=norm1, norm2=norm2, norm3=norm3,
             ffn1_wg=ffn1_wg, ffn1_wu=ffn1_wu, ffn1_wd=ffn1_wd, ffn2_wg=ffn2_wg, ffn2_wu=ffn2_wu, ffn2_wd=ffn2_wd,
             w_in=w_in, gdn_conv_w=gdn_conv_w, gdn_a_log=gdn_a_log, gdn_dt_bias=gdn_dt_bias, gdn_norm_w=gdn_norm_w,
             gla_w_gate=gla_w_gate, gla_b_gate=gla_b_gate, gla_norm_w=gla_norm_w,
             ssd_conv_w=ssd_conv_w, ssd_conv_b=ssd_conv_b, ssd_a_log=ssd_a_log, ssd_dt_bias=ssd_dt_bias,
             ssd_d=ssd_d, ssd_norm_w=ssd_norm_w,
             w_branch_gdn=w_branch_gdn, w_branch_gla=w_branch_gla, w_branch_ssd=w_branch_ssd, w_out=w_out)
    nl = w_ada.shape[0]
    bp, tp, dm = x_prompt.shape
    bs = x_sample.shape[0]
    assert x_sample.shape[1] == 1 and tp % CHUNK == 0 and bs % STEP_B == 0 and dm == D_MODEL
    lps = [_layer_params(l, p) for l in range(nl)]
    sw = _stacked_weights(p)
    fw = final_norm.reshape(1, dm)

    rows = bp + bs
    rpad = -(-rows // 8) * 8
    c_all = jnp.concatenate([c_prompt, c_sample, jnp.zeros((rpad - rows, dm), f32)], axis=0)
    mod = _ada_mod(c_all, w_ada, b_ada)
    mod_p = [mod[l, :bp].reshape(bp, 1, N_MOD * dm) for l in range(nl)]
    mod_s = [mod[l, bp:rows].reshape(1, bs, N_MOD * dm) for l in range(nl)]

    sample_states = (state_gdn_conv, state_gdn, state_gla, state_ssd_conv, state_ssd)
    prompt_states = tuple(jnp.zeros((s.shape[0], bp) + s.shape[2:], x_prompt.dtype) for s in sample_states)
    tm_p = ROW_TILE if tp % ROW_TILE == 0 else CHUNK
    tm_in = ROW_TILE_WIDE if tp % ROW_TILE_WIDE == 0 else tm_p
    ffn_w = {}
    y_s, st_s = _trunk(x_sample.reshape(1, bs, dm), mod_s, lps, sw, ffn_w, sample_states, True, bs, bs, fw)
    y_p, st_p = _trunk(x_prompt, mod_p, lps, sw, ffn_w, prompt_states, False, tm_p, tm_in, fw)
    return (y_p, y_s.reshape(bs, 1, dm)) + st_p + st_s
```

```python
import functools

import jax
import jax.numpy as jnp
from jax import lax
from jax.experimental import pallas as pl
from jax.experimental.pallas import tpu as pltpu

f32 = jnp.float32
bf16 = jnp.bfloat16
HI = lax.Precision.HIGHEST

EPS = 1e-6
D_MODEL = 2048
N_MOD = 9
CHUNK = 64
CONV_W = 4
GDN_H, GDN_DK, GDN_DV = 8, 128, 128
GLA_H, GLA_DK, GLA_DV, GLA_RANK, GLA_TAU = 4, 128, 256, 16, 16.0
SSD_H, SSD_P, SSD_G, SSD_N = 32, 64, 4, 128
SSD_HG = SSD_H // SSD_G
GDN_QK = GDN_H * GDN_DK
GDN_V = GDN_H * GDN_DV
GDN_CONV = 2 * GDN_QK + GDN_V
GLA_QK = GLA_H * GLA_DK
GLA_V = GLA_H * GLA_DV
SSD_INNER = SSD_H * SSD_P
SSD_BC = SSD_G * SSD_N
SSD_CONV = SSD_INNER + 2 * SSD_BC
SSD_GW = SSD_HG * SSD_P

LANE = 128
MXU_WIDTH = 256

ROW_TILE = 512
ROW_TILE_WIDE = 1024
FF_TILE = 512
FF_SUB = MXU_WIDTH
ADA_COLS = 1024
INPROJ_COLS = 1024
MERGE_COLS = 512
OUTPROJ_COLS = 1024

IN_SPLITS = (("qkv_a", GDN_CONV), ("z_a", GDN_V), ("beta", GDN_H), ("dec", GDN_H),
             ("q_b", GLA_QK), ("k_b", GLA_QK), ("v_b", GLA_V), ("lr", GLA_RANK), ("r_b", GLA_V),
             ("z_c", SSD_INNER), ("xbc", SSD_CONV), ("dt", SSD_H), ("gates", 3 * D_MODEL))
MAIN_ORDER = ("qkv_a", "xbc", "z_c", "gates", "z_a", "q_b", "k_b", "v_b", "r_b")
SMALL_ORDER = ("beta", "dec", "lr", "dt")


def _layout():
    src, off = {}, 0
    for name, w in IN_SPLITS:
        src[name] = (off, off + w)
        off += w
    main, small, d = {}, {}, 0
    for name in MAIN_ORDER:
        w = src[name][1] - src[name][0]
        main[name] = d
        d += w
    total = d
    d = 0
    for name in SMALL_ORDER:
        small[name] = d
        d += src[name][1] - src[name][0]
    assert d <= LANE
    return src, main, small, total


W_IN_SRC, P_MAIN, P_SM, P_TOTAL = _layout()
P_QKV_A, P_XBC, P_Z_C, P_GATES, P_Z_A = (P_MAIN[k] for k in ("qkv_a", "xbc", "z_c", "gates", "z_a"))
P_Q_B, P_K_B, P_V_B, P_R_B = (P_MAIN[k] for k in ("q_b", "k_b", "v_b", "r_b"))
SM_BETA, SM_DEC, SM_LR, SM_DT = (P_SM[k] for k in SMALL_ORDER)

VMEM_LIMIT = 56 * 1024 * 1024


def _cp(sem):
    return pltpu.CompilerParams(dimension_semantics=sem, vmem_limit_bytes=VMEM_LIMIT)


def _blk(off, width):
    assert off % width == 0, (off, width)
    return off // width


def _sigmoid(x):
    return 0.5 + 0.5 * jnp.tanh(0.5 * x)


def _silu(x):
    h = 0.5 * x
    return h + h * jnp.tanh(h)


def _rms(x, w):
    return x * lax.rsqrt(jnp.mean(x * x, axis=-1, keepdims=True) + EPS) * w


def _dot(a, b):
    return jnp.dot(a, b, preferred_element_type=f32)


def _dot_nt(a, b):
    return lax.dot_general(a, b, (((1,), (1,)), ((), ())), preferred_element_type=f32)


def _dot_tn(a, b):
    return lax.dot_general(a, b, (((0,), (0,)), ((), ())), preferred_element_type=f32)


def _split(x):
    hi = x.astype(bf16)
    return hi, (x - hi.astype(f32)).astype(bf16)


def _dot3(a, b):
    return _dot(a[0], b[0]) + (_dot(a[0], b[1]) + _dot(a[1], b[0]))


def _dot_hi(a, b):
    return jnp.dot(a, b, precision=HI, preferred_element_type=f32)


def _tri_masks(c):
    row = lax.broadcasted_iota(jnp.int32, (c, c), 0)
    col = lax.broadcasted_iota(jnp.int32, (c, c), 1)
    return row >= col, row > col, row == col


SUB = 8


def _row_time(idx, c):
    n = c // SUB
    assert n & (n - 1) == 0, "chunk / SUB must be a power of two"
    sh = n.bit_length() - 1
    return ((idx & (n - 1)) << 3) | (idx >> sh)


def _tri_masks_residue(c, cols=None):
    cols = c if cols is None else cols
    row = _row_time(lax.broadcasted_iota(jnp.int32, (c, cols), 0), c)
    col = _row_time(lax.broadcasted_iota(jnp.int32, (c, cols), 1) & (c - 1), c)
    return row >= col, row > col, row == col


def _load_residue(ref, c):
    n = c // SUB
    return jnp.concatenate([ref[pl.ds(r, n, stride=SUB), :] for r in range(SUB)], axis=0)


def _store_natural(nat_ref, x, c, lo):
    n = c // SUB
    for j in range(x.shape[1] // LANE):
        for r in range(SUB):
            nat_ref[lo // LANE + j, pl.ds(r, n, stride=SUB), :] = x[r * n:(r + 1) * n, j * LANE:(j + 1) * LANE]


def _read_natural(nat_ref, lo, width):
    return jnp.concatenate([nat_ref[lo // LANE + j] for j in range(width // LANE)], axis=1)


def _scalar_vec(s):
    return jnp.full((1, 1), s, f32)


def _ada_kernel(c_ref, w_ref, b_ref, o_ref):
    s = _silu(c_ref[...]).astype(bf16)
    o_ref[...] = _dot(s, w_ref[...].astype(bf16)) + b_ref[...]


def _ada_mod(c_all, w_ada, b_ada):
    nl, dm, n = w_ada.shape
    r = c_all.shape[0]
    tn = ADA_COLS
    return pl.pallas_call(
        _ada_kernel,
        grid=(nl, n // tn),
        in_specs=[pl.BlockSpec((r, dm), lambda l, j: (0, 0)),
                  pl.BlockSpec((None, dm, tn), lambda l, j: (l, 0, j)),
                  pl.BlockSpec((None, 1, tn), lambda l, j: (l, 0, j))],
        out_specs=pl.BlockSpec((None, r, tn), lambda l, j: (l, 0, j)),
        out_shape=jax.ShapeDtypeStruct((nl, r, n), f32),
        compiler_params=_cp(("arbitrary", "arbitrary")),
        name="ada_mod",
    )(c_all, w_ada, b_ada.reshape(nl, 1, n))


def _mod_spec(per_row, tm, chunk):
    if per_row:
        return pl.BlockSpec((None, tm, D_MODEL), lambda b, i, j: (b, i, chunk))
    return pl.BlockSpec((None, 1, D_MODEL), lambda b, i, j: (b, 0, chunk))


def _ffn_prologue(x_ref, sh_ref, sc_ref, nw_ref, h_ref, acc_ref):
    y = _rms(x_ref[...], nw_ref[...])
    h_ref[...] = (y * (1.0 + sc_ref[...]) + sh_ref[...]).astype(bf16)
    acc_ref[...] = jnp.zeros_like(acc_ref)


def _ffn_accumulate(h_ref, acc_ref, wg, wu, wd):
    h = h_ref[...]
    width = wg.shape[1]
    sub = FF_SUB if width % FF_SUB == 0 else width
    parts = [(_dot(h, wg[:, s:s + sub]), _dot(h, wu[:, s:s + sub])) for s in range(0, width, sub)]
    acc = acc_ref[...]
    for i, (g, u) in enumerate(parts):
        acc = acc + _dot((_silu(g) * u).astype(bf16), wd[i * sub:(i + 1) * sub, :])
    acc_ref[...] = acc


def _ffn_epilogue(x_ref, gt_ref, fw_ref, acc_ref, o_ref, final):
    y = x_ref[...] + 0.5 * gt_ref[...] * acc_ref[...]
    if final:
        y = _rms(y, fw_ref[...])
    o_ref[...] = y


def _ffn_kernel(x_ref, sh_ref, sc_ref, gt_ref, nw_ref, fw_ref, wg_ref, wu_ref, wd_ref, *rest, final, has_tail):
    if has_tail:
        wgt_ref, wut_ref, wdt_ref, o_ref, h_ref, acc_ref = rest
    else:
        o_ref, h_ref, acc_ref = rest
    f = pl.program_id(2)
    last = pl.num_programs(2) - 1
    pl.when(f == 0)(lambda: _ffn_prologue(x_ref, sh_ref, sc_ref, nw_ref, h_ref, acc_ref))
    if has_tail:
        pl.when(f < last)(lambda: _ffn_accumulate(h_ref, acc_ref, wg_ref, wu_ref, wd_ref))
        pl.when(f == last)(lambda: _ffn_accumulate(h_ref, acc_ref, wgt_ref, wut_ref, wdt_ref))
    else:
        _ffn_accumulate(h_ref, acc_ref, wg_ref, wu_ref, wd_ref)
    pl.when(f == last)(lambda: _ffn_epilogue(x_ref, gt_ref, fw_ref, acc_ref, o_ref, final))


def _ffn(x, mod, k0, nw, w, fw, per_row, tm, final):
    bx, tx, dm = x.shape
    tf = FF_TILE
    n_full = w[0].shape[1] // tf
    has_tail = len(w) > 3
    full = lambda f: jnp.minimum(f, n_full - 1)
    w_specs = [pl.BlockSpec((dm, tf), lambda b, i, f: (0, full(f))),
               pl.BlockSpec((dm, tf), lambda b, i, f: (0, full(f))),
               pl.BlockSpec((tf, dm), lambda b, i, f: (full(f), 0))]
    if has_tail:
        ft = w[3].shape[1]
        w_specs += [pl.BlockSpec((dm, ft), lambda b, i, f: (0, 0)),
                    pl.BlockSpec((dm, ft), lambda b, i, f: (0, 0)),
                    pl.BlockSpec((ft, dm), lambda b, i, f: (0, 0))]
    return pl.pallas_call(
        functools.partial(_ffn_kernel, final=final, has_tail=has_tail),
        grid=(bx, tx // tm, n_full + int(has_tail)),
        in_specs=[pl.BlockSpec((None, tm, dm), lambda b, i, f: (b, i, 0)),
                  _mod_spec(per_row, tm, k0), _mod_spec(per_row, tm, k0 + 1), _mod_spec(per_row, tm, k0 + 2),
                  pl.BlockSpec((1, dm), lambda b, i, f: (0, 0)),
                  pl.BlockSpec((1, dm), lambda b, i, f: (0, 0))] + w_specs,
        out_specs=pl.BlockSpec((None, tm, dm), lambda b, i, f: (b, i, 0)),
        out_shape=jax.ShapeDtypeStruct(x.shape, f32),
        scratch_shapes=[pltpu.VMEM((tm, dm), bf16), pltpu.VMEM((tm, dm), f32)],
        compiler_params=_cp(("parallel", "parallel", "arbitrary")),
        name="ffn",
    )(x, mod, mod, mod, nw, fw, *w)


FF_CAST_TILE = 256


def _ffn_cast_kernel(x_ref, sh_ref, sc_ref, gt_ref, nw_ref, fw_ref, wg_ref, wu_ref, wd_ref, *rest, final, has_tail):
    if has_tail:
        wgt_ref, wut_ref, wdt_ref, o_ref, wgb_ref, wub_ref, wdb_ref, wgtb_ref, wutb_ref, wdtb_ref, h_ref, acc_ref = rest
    else:
        o_ref, wgb_ref, wub_ref, wdb_ref, h_ref, acc_ref = rest
    f = pl.program_id(2)
    last = pl.num_programs(2) - 1
    pl.when(f == 0)(lambda: _ffn_prologue(x_ref, sh_ref, sc_ref, nw_ref, h_ref, acc_ref))

    def cast_accumulate(src, dst):
        for s_ref, d_ref in zip(src, dst):
            d_ref[...] = s_ref[...].reshape(d_ref.shape).astype(bf16)
        _ffn_accumulate(h_ref, acc_ref, *dst)

    if has_tail:
        pl.when(f < last)(lambda: cast_accumulate((wg_ref, wu_ref, wd_ref), (wgb_ref, wub_ref, wdb_ref)))
        pl.when(f == last)(lambda: cast_accumulate((wgt_ref, wut_ref, wdt_ref), (wgtb_ref, wutb_ref, wdtb_ref)))
    else:
        cast_accumulate((wg_ref, wu_ref, wd_ref), (wgb_ref, wub_ref, wdb_ref))
    pl.when(f == last)(lambda: _ffn_epilogue(x_ref, gt_ref, fw_ref, acc_ref, o_ref, final))


def _ffn_cast(x, mod, k0, nw, wg, wu, wd, l, fw, final):
    bx, tx, dm = x.shape
    assert bx == 1
    f_all = wg.shape[2]
    tf = FF_CAST_TILE
    cut = f_all - f_all % FF_TILE
    n_full = cut // tf
    has_tail = cut < f_all
    full = lambda f: jnp.minimum(f, n_full - 1)
    w_in_specs = [pl.BlockSpec((None, dm, tf), lambda b, i, f: (l, 0, full(f))),
                  pl.BlockSpec((None, dm, tf), lambda b, i, f: (l, 0, full(f))),
                  pl.BlockSpec((None, tf, dm), lambda b, i, f: (l, full(f), 0))]
    w_out_specs = [pl.BlockSpec((dm, tf), lambda b, i, f: (0, full(f))),
                   pl.BlockSpec((dm, tf), lambda b, i, f: (0, full(f))),
                   pl.BlockSpec((tf, dm), lambda b, i, f: (full(f), 0))]
    w_out_shapes = [jax.ShapeDtypeStruct((dm, cut), bf16), jax.ShapeDtypeStruct((dm, cut), bf16),
                    jax.ShapeDtypeStruct((cut, dm), bf16)]
    args = [wg, wu, wd]
    if has_tail:
        ft = f_all - cut
        args += [wg, wu, wd]
        el = lambda *shape: tuple(pl.Element(s) for s in shape)
        w_in_specs += [pl.BlockSpec(el(1, dm, ft), lambda b, i, f: (l, 0, cut)),
                       pl.BlockSpec(el(1, dm, ft), lambda b, i, f: (l, 0, cut)),
                       pl.BlockSpec(el(1, ft, dm), lambda b, i, f: (l, cut, 0))]
        w_out_specs += [pl.BlockSpec((dm, ft), lambda b, i, f: (0, 0)), pl.BlockSpec((dm, ft), lambda b, i, f: (0, 0)),
                        pl.BlockSpec((ft, dm), lambda b, i, f: (0, 0))]
        w_out_shapes += [jax.ShapeDtypeStruct((dm, ft), bf16), jax.ShapeDtypeStruct((dm, ft), bf16),
                         jax.ShapeDtypeStruct((ft, dm), bf16)]
    out = pl.pallas_call(
        functools.partial(_ffn_cast_kernel, final=final, has_tail=has_tail),
        grid=(1, 1, n_full + int(has_tail)),
        in_specs=[pl.BlockSpec((None, tx, dm), lambda b, i, f: (b, i, 0)),
                  _mod_spec(True, tx, k0), _mod_spec(True, tx, k0 + 1), _mod_spec(True, tx, k0 + 2),
                  pl.BlockSpec((1, dm), lambda b, i, f: (0, 0)),
                  pl.BlockSpec((1, dm), lambda b, i, f: (0, 0))] + w_in_specs,
        out_specs=[pl.BlockSpec((None, tx, dm), lambda b, i, f: (b, i, 0))] + w_out_specs,
        out_shape=[jax.ShapeDtypeStruct(x.shape, f32)] + w_out_shapes,
        scratch_shapes=[pltpu.VMEM((tx, dm), bf16), pltpu.VMEM((tx, dm), f32)],
        compiler_params=_cp(("arbitrary", "arbitrary", "arbitrary")),
        name="ffn_cast",
    )(x, mod, mod, mod, nw, fw, *args)
    return out[0], tuple(out[1:])


def _inproj_kernel(x_ref, sh_ref, sc_ref, nw_ref, w_ref, ws_ref, o_ref, os_ref, h_ref):
    @pl.when(pl.program_id(2) == 0)
    def _():
        y = _rms(x_ref[...], nw_ref[...])
        h = (y * (1.0 + sc_ref[...]) + sh_ref[...]).astype(bf16)
        h_ref[...] = h
        os_ref[...] = _dot_nt(h, ws_ref[...].astype(bf16))

    o_ref[...] = _dot_nt(h_ref[...], w_ref[...]).astype(o_ref.dtype)


def _inproj(x, mod, nw, w, ws, l, per_row, tm, out_dtype):
    bx, tx, dm = x.shape
    n = w.shape[1]
    tn = INPROJ_COLS
    return pl.pallas_call(
        _inproj_kernel,
        grid=(bx, tx // tm, n // tn),
        in_specs=[pl.BlockSpec((None, tm, dm), lambda b, i, j: (b, i, 0)),
                  _mod_spec(per_row, tm, 3), _mod_spec(per_row, tm, 4),
                  pl.BlockSpec((1, dm), lambda b, i, j: (0, 0)),
                  pl.BlockSpec((None, tn, dm), lambda b, i, j: (l, j, 0)),
                  pl.BlockSpec((None, LANE, dm), lambda b, i, j: (l, 0, 0))],
        out_specs=[pl.BlockSpec((None, tm, tn), lambda b, i, j: (b, i, j)),
                   pl.BlockSpec((None, tm, LANE), lambda b, i, j: (b, i, 0))],
        out_shape=[jax.ShapeDtypeStruct((bx, tx, n), out_dtype),
                   jax.ShapeDtypeStruct((bx, tx, LANE), f32)],
        scratch_shapes=[pltpu.VMEM((tm, dm), bf16)],
        compiler_params=_cp(("parallel", "parallel", "arbitrary")),
        name="inproj",
    )(x, mod, mod, nw, w, ws)


def _conv_load(ext_ref, x_ref, cst_ref, first):
    c = x_ref.shape[0]
    nblk = ext_ref.shape[0]

    @pl.when(first)
    def _():
        for j in range(nblk):
            ext_ref[j, pl.ds(5, CONV_W - 1), :] = cst_ref[:, j * LANE:(j + 1) * LANE]

    for j in range(nblk):
        ext_ref[j, pl.ds(8, c), :] = x_ref[:, j * LANE:(j + 1) * LANE].astype(f32)


def _conv_cols(ext_ref, w_ref, b_ref, lo, width, c):
    n = c // SUB
    cols = []
    for blk in range(lo // LANE, (lo + width) // LANE):
        sl = pl.ds(blk * LANE, LANE)
        w = w_ref[:, sl]
        taps = [ext_ref[blk, pl.ds(8 - (CONV_W - 1) + k, n, stride=SUB), :] for k in range(SUB + CONV_W - 1)]
        ys = []
        for r in range(SUB):
            y = taps[r] * w[0:1]
            for j in range(1, CONV_W):
                y = y + taps[r + j] * w[j:j + 1]
            ys.append(y)
        y = jnp.concatenate(ys, axis=0)
        if b_ref is not None:
            y = y + b_ref[:, sl]
        cols.append(_silu(y))
    return cols[0] if len(cols) == 1 else jnp.concatenate(cols, axis=1)


def _conv_carry(ext_ref, c):
    for j in range(ext_ref.shape[0]):
        ext_ref[j, pl.ds(5, CONV_W - 1), :] = ext_ref[j, pl.ds(8 + c - (CONV_W - 1), CONV_W - 1), :]


def _cumsum_rows(x, tri):
    return _dot_hi(jnp.where(tri, 1.0, 0.0).astype(f32), x)


def _gdn_kernel(qkv_ref, z_ref, sm_ref, cst_ref, cw_ref, alog_ref, dtb_ref, nw_ref, s0_ref,
                o_ref, s_ref, ext_ref, nat_ref):
    c = qkv_ref.shape[0]
    first = pl.program_id(1) == 0

    @pl.when(first)
    def _():
        s_ref[...] = s0_ref[...]

    _conv_load(ext_ref, qkv_ref, cst_ref, first)

    sm = _load_residue(sm_ref, c)
    beta_l = _sigmoid(sm)
    g_l = -jnp.exp(alog_ref[...]) * jax.nn.softplus(sm + dtb_ref[...])
    tri, strict, eye = _tri_masks_residue(c)
    gc_all = _cumsum_rows(g_l, tri)
    gc_t = gc_all.T
    eye_f = jnp.where(eye, 1.0, 0.0).astype(f32)

    hs = range(GDN_H)
    qb, kb, ks, decay, eg, gcs, nm, rhs = [], [], [], [], [], [], [], []
    for h in hs:
        q = _conv_cols(ext_ref, cw_ref, None, h * GDN_DK, GDN_DK, c)
        k = _conv_cols(ext_ref, cw_ref, None, GDN_QK + h * GDN_DK, GDN_DK, c)
        v = _conv_cols(ext_ref, cw_ref, None, 2 * GDN_QK + h * GDN_DV, GDN_DV, c)
        q = q * lax.rsqrt(jnp.sum(q * q, axis=-1, keepdims=True) + EPS) * (GDN_DK ** -0.5)
        k = k * lax.rsqrt(jnp.sum(k * k, axis=-1, keepdims=True) + EPS)
        beta = beta_l[:, SM_BETA + h:SM_BETA + h + 1]
        gc = gc_all[:, SM_DEC + h:SM_DEC + h + 1]
        d = jnp.exp(jnp.where(tri, gc - gc_t[SM_DEC + h:SM_DEC + h + 1, :], -jnp.inf))
        e = jnp.exp(gc)
        qb.append(q.astype(bf16))
        kb.append(k.astype(bf16))
        ks.append(k)
        decay.append(d)
        eg.append(e)
        gcs.append(gc)
        nm.append(jnp.where(strict, beta * d * _dot_nt(kb[h], kb[h]), 0.0))
        rhs.append(jnp.concatenate([v * beta, k * (beta * e)], axis=1))
        yield

    assert c // GDN_BLK <= 4 and GDN_BLK & (GDN_BLK - 1) == 0
    blk_sh = GDN_BLK.bit_length() - 1
    same_blk = (_row_time(lax.broadcasted_iota(jnp.int32, (c, c), 0), c) >> blk_sh
                == _row_time(lax.broadcasted_iota(jnp.int32, (c, c), 1), c) >> blk_sh)
    nd = [jnp.where(same_blk, nm[h], 0.0) for h in hs]
    lo_s = [_split(nm[h] - nd[h]) for h in hs]
    dinv = [eye_f - nd[h] for h in hs]
    pw_s = [_split(nd[h]) for h in hs]
    for i in range(GDN_BLK.bit_length() - 2):
        pw_s = [_split(_dot3(pw_s[h], pw_s[h])) for h in hs]
        yield
        dinv = [dinv[h] + _dot3(_split(dinv[h]), pw_s[h]) for h in hs]
        yield
    dinv_s = [_split(dinv[h]) for h in hs]
    m = [_dot3(dinv_s[h], lo_s[h]) for h in hs]
    m_s = [_split(m[h]) for h in hs]
    yield
    m2 = [_dot3(m_s[h], m_s[h]) for h in hs]
    yield
    t1 = [eye_f - m[h] + m2[h] - _dot3(m_s[h], _split(m2[h])) for h in hs]
    yield
    inv = [_dot3(_split(t1[h]), dinv_s[h]) for h in hs]
    yield
    sol = []
    for h in hs:
        inv_hi, inv_lo = _split(inv[h])
        rb = rhs[h].astype(bf16)
        sol.append(_dot(inv_hi, rb) + _dot(inv_lo, rb))
    yield

    s = [s_ref[h] for h in hs]
    sb = [s[h].astype(bf16) for h in hs]
    wb = [(sol[h][:, 0:GDN_DV] - _dot(sol[h][:, GDN_DV:].astype(bf16), sb[h])).astype(bf16) for h in hs]
    yield
    qk = [(_dot_nt(qb[h], kb[h]) * decay[h]).astype(bf16) for h in hs]
    yield
    o = [_dot(qb[h], sb[h]) * eg[h] + _dot(qk[h], wb[h]) for h in hs]
    yield
    for h in hs:
        g_last = gcs[h][c - 1:c]
        kd = (ks[h] * jnp.exp(g_last - gcs[h])).astype(bf16)
        s_ref[h] = s[h] * jnp.exp(g_last) + _dot_tn(kd, wb[h])
    yield
    for h in hs:
        sl = slice(h * GDN_DV, (h + 1) * GDN_DV)
        _store_natural(nat_ref, _rms(o[h], nw_ref[...]), c, h * GDN_DV)
        o_ref[:, sl] = (_read_natural(nat_ref, h * GDN_DV, GDN_DV) * _silu(z_ref[:, sl].astype(f32))).astype(bf16)
        yield

    _conv_carry(ext_ref, c)


def _tok(width, off):
    return pl.BlockSpec((None, CHUNK, width), lambda bi, ti: (bi, ti, _blk(off, width)))


def _per_seq(*shape):
    return pl.BlockSpec((None,) + shape, lambda bi, ti: (bi,) + (0,) * len(shape))


def _const(*shape):
    return pl.BlockSpec(shape, lambda bi, ti: (0,) * len(shape))


N_GDN_IN, N_GLA_IN, N_SSD_IN = 9, 9, 11
_DONE = object()


def _mixers_kernel(*refs):
    i0, i1, i2 = N_GDN_IN, N_GDN_IN + N_GLA_IN, N_GDN_IN + N_GLA_IN + N_SSD_IN
    gdn_in, gla_in, ssd_in = refs[:i0], refs[i0:i1], refs[i1:i2]
    oa_ref, sa_ref, ob_ref, sb_ref, oc_ref, sc_ref, ext_a_ref, st_b_ref, ext_c_ref, nat_a_ref, nat_c_ref = refs[i2:]
    live = [_gdn_kernel(*gdn_in, oa_ref, sa_ref, ext_a_ref, nat_a_ref),
            _gla_kernel(*gla_in, ob_ref, sb_ref, st_b_ref),
            _ssd_kernel(*ssd_in, oc_ref, sc_ref, ext_c_ref, nat_c_ref)]
    while live:
        for gen in list(live):
            if next(gen, _DONE) is _DONE:
                live.remove(gen)


GLA_SUB = 16
GDN_BLK = 16


def _gla_kernel(q_ref, k_ref, v_ref, r_ref, sm_ref, wgate_ref, bgate_ref, nw_ref, s0_ref,
                o_ref, so_ref, st_ref):
    c = q_ref.shape[0]
    ti = pl.program_id(1)

    @pl.when(ti == 0)
    def _():
        for h in range(GLA_H):
            st_ref[h] = s0_ref[h].T

    sm = sm_ref[...]
    lane = lax.broadcasted_iota(jnp.int32, sm.shape, 1)
    lr = jnp.where((lane >= SM_LR) & (lane < SM_LR + GLA_RANK), sm, 0.0).astype(bf16)
    la = jax.nn.log_sigmoid(_dot(lr, wgate_ref[...]) + bgate_ref[...]) / GLA_TAU
    tri, _, _ = _tri_masks(c)
    b_all = _cumsum_rows(la, tri)

    col = lax.broadcasted_iota(jnp.int32, (GLA_SUB, c), 1)
    row = lax.broadcasted_iota(jnp.int32, (GLA_SUB, c), 0)
    hs = range(GLA_H)
    qs, ks, vbs, bs, atts = [], [], [], [], []
    for h in hs:
        sk = slice(h * GLA_DK, (h + 1) * GLA_DK)
        q = q_ref[:, sk].astype(f32) * (GLA_DK ** -0.5)
        k = k_ref[:, sk].astype(f32)
        b = b_all[:, sk]
        qs.append(q)
        ks.append(k)
        bs.append(b)
        vbs.append(v_ref[:, h * GLA_DV:(h + 1) * GLA_DV].astype(bf16))
        att_rows = []
        for i in range(c // GLA_SUB):
            lo = i * GLA_SUB
            b_i = b[lo:lo + GLA_SUB]
            q_i = q[lo:lo + GLA_SUB]
            b_top = b[lo:lo + 1]
            att = jnp.zeros((GLA_SUB, c), f32)
            if i > 0:
                q_t = (q_i * jnp.exp(b_i - b_top)).astype(bf16)
                k_t = (k * jnp.exp(jnp.minimum(b_top - b, 0.0))).astype(bf16)
                att = jnp.where(col < lo, _dot_nt(q_t, k_t), 0.0)
            for sl in range(GLA_SUB):
                sidx = lo + sl
                e = jnp.exp(jnp.minimum(b_i - b[sidx:sidx + 1], 0.0))
                p = jnp.sum(q_i * k[sidx:sidx + 1] * e, axis=1, keepdims=True)
                att = jnp.where((col == sidx) & (row >= sl), p, att)
            att_rows.append(att)
            yield
        atts.append(jnp.concatenate(att_rows, axis=0).astype(bf16))

    sts = [st_ref[h] for h in hs]
    os_ = [_dot(atts[h], vbs[h]) + _dot_nt((qs[h] * jnp.exp(bs[h])).astype(bf16), sts[h].astype(bf16))
           for h in hs]
    yield
    for h in hs:
        b_last = bs[h][c - 1:c]
        kd = (ks[h] * jnp.exp(b_last - bs[h])).astype(bf16)
        st_ref[h] = sts[h] * jnp.exp(b_last) + _dot_tn(vbs[h], kd)
        yield
    for h in hs:
        sv = slice(h * GLA_DV, (h + 1) * GLA_DV)
        o_ref[:, sv] = (_rms(os_[h], nw_ref[...]) * _silu(r_ref[:, sv].astype(f32))).astype(bf16)
        yield

    @pl.when(ti == pl.num_programs(1) - 1)
    def _():
        for h in hs:
            so_ref[h] = st_ref[h].T


def _ssd_kernel(z_ref, xbc_ref, sm_ref, cst_ref, cw_ref, cb_ref, avec_ref, dtb_ref, dvec_ref, nw_ref, h0_ref,
                o_ref, h_ref, ext_ref, nat_ref):
    c = xbc_ref.shape[0]
    assert c == SSD_P, "head pairs share a lane tile: the chunk must be as wide as a head"
    first = pl.program_id(1) == 0

    @pl.when(first)
    def _():
        h_ref[...] = h0_ref[...]

    _conv_load(ext_ref, xbc_ref, cst_ref, first)

    dt_l = jax.nn.softplus(_load_residue(sm_ref, c) + dtb_ref[...])
    tri, _, _ = _tri_masks_residue(c)
    ac_all = _cumsum_rows(dt_l * avec_ref[...], tri)
    ac_t2 = jnp.concatenate([ac_all, ac_all], axis=0).T

    lane = lax.broadcasted_iota(jnp.int32, (c, 2 * SSD_P), 1)
    lo_half = lane < SSD_P
    tri2, _, _ = _tri_masks_residue(c, 2 * SSD_P)
    lane1 = lane[0:1]
    gs_ = range(SSD_G)
    ps_ = range(SSD_HG // 2)

    xs_, bmb, hg, cb2, ch = [], [], [], [], []
    for g in gs_:
        xs_.append(_conv_cols(ext_ref, cw_ref, cb_ref, g * SSD_GW, SSD_GW, c))
        bmb.append(_conv_cols(ext_ref, cw_ref, cb_ref, SSD_INNER + g * SSD_N, SSD_N, c).astype(bf16))
        cmb = _conv_cols(ext_ref, cw_ref, cb_ref, SSD_INNER + SSD_BC + g * SSD_N, SSD_N, c).astype(bf16)
        hg.append(h_ref[pl.ds(g * SSD_HG, SSD_HG)].reshape(SSD_GW, SSD_N))
        cb2.append(_dot_nt(cmb, jnp.concatenate([bmb[g], bmb[g]], axis=0)))
        ch.append(_dot_nt(cmb, hg[g].astype(bf16)))
        yield

    m2, rhs, e_col, xsc = {}, {}, {}, {}
    for g in gs_:
        for p in ps_:
            l0 = SM_DT + g * SSD_HG + 2 * p
            ps = slice(p * 2 * SSD_P, (p + 1) * 2 * SSD_P)
            ac_col = jnp.where(lo_half, ac_all[:, l0:l0 + 1], ac_all[:, l0 + 1:l0 + 2])
            ac_row = jnp.where(lane1 < SSD_P, ac_t2[l0:l0 + 1, :], ac_t2[l0 + 1:l0 + 2, :])
            decay = jnp.exp(jnp.where(tri2, ac_col - ac_row, -jnp.inf))
            m2[g, p] = (cb2[g] * decay).astype(bf16)
            dt2 = jnp.where(lo_half, dt_l[:, l0:l0 + 1], dt_l[:, l0 + 1:l0 + 2])
            xdt = xs_[g][:, ps] * dt2
            rhs[g, p] = jnp.concatenate([jnp.where(lo_half, xdt, 0.0), jnp.where(lo_half, 0.0, xdt)],
                                        axis=0).astype(bf16)
            e_col[g, p] = jnp.exp(ac_col)
            xsc[g, p] = xdt * jnp.exp(ac_col[c - 1:c] - ac_col)
            yield
    y2 = {gp: _dot(m2[gp], rhs[gp]) for gp in m2}
    yield

    for g in gs_:
        xsg = jnp.concatenate([xsc[g, p] for p in ps_], axis=1).astype(bf16)
        dh = _dot_tn(xsg, bmb[g])
        for j in range(SSD_HG):
            ln = SM_DT + g * SSD_HG + j
            h_ref[g * SSD_HG + j] = (hg[g][j * SSD_P:(j + 1) * SSD_P] * jnp.exp(ac_all[c - 1:c, ln:ln + 1])
                                     + dh[j * SSD_P:(j + 1) * SSD_P])
        yield
    for g in gs_:
        gs = slice(g * SSD_GW, (g + 1) * SSD_GW)
        y = jnp.concatenate([y2[g, p] + ch[g][:, p * 2 * SSD_P:(p + 1) * 2 * SSD_P] * e_col[g, p] for p in ps_],
                            axis=1)
        _store_natural(nat_ref, y + dvec_ref[:, gs] * xs_[g], c, g * SSD_GW)
        y = _read_natural(nat_ref, g * SSD_GW, SSD_GW) * _silu(z_ref[:, gs].astype(f32))
        o_ref[:, gs] = _rms(y, nw_ref[:, gs]).astype(bf16)
        yield

    _conv_carry(ext_ref, c)


STEP_B = 8


def _conv_step(x_ref, c_ref, w_ref):
    w = w_ref[...]
    return c_ref[0] * w[0:1] + c_ref[1] * w[1:2] + c_ref[2] * w[2:3] + x_ref[...] * w[3:4]


def _expand_matrix(nb, width):
    r = lax.broadcasted_iota(jnp.int32, (nb, nb * width), 0)
    c = lax.broadcasted_iota(jnp.int32, (nb, nb * width), 1)
    return jnp.where((c >= r * width) & (c < (r + 1) * width), 1.0, 0.0).astype(bf16)


def _bcast_cols(x, e_mat):
    hi, lo = _split(x)
    return _dot_tn(hi, e_mat) + _dot_tn(lo, e_mat)


def _state_call(kernel, grid, in_specs, out_specs, out_shape, scratch, sem, name, args, prev_state):
    aliases = {}
    if prev_state is not None:
        in_specs = in_specs + [pl.BlockSpec(memory_space=pl.ANY)]
        args = args + (prev_state,)
        aliases = {len(args) - 1: len(out_shape) - 1}
        kernel = functools.partial(_drop_alias_ref, kernel, len(args) - 1)
    return pl.pallas_call(kernel, grid=grid, in_specs=in_specs, out_specs=out_specs, out_shape=out_shape,
                          scratch_shapes=scratch, input_output_aliases=aliases,
                          compiler_params=_cp(sem), name=name)(*args)


def _drop_alias_ref(kernel, idx, *refs):
    return kernel(*refs[:idx], *refs[idx + 1:])


def _gdn_step_kernel(alog_ref, dtb_ref, q_ref, k_ref, v_ref, z_ref, sm_ref, cq_ref, ck_ref, cv_ref,
                     wq_ref, wk_ref, wv_ref, nw_ref, s_ref, o_ref, so_ref):
    nb = q_ref.shape[0]
    qa = _silu(_conv_step(q_ref, cq_ref, wq_ref))
    ka = _silu(_conv_step(k_ref, ck_ref, wk_ref))
    va = _silu(_conv_step(v_ref, cv_ref, wv_ref))
    sm = sm_ref[...]
    z = z_ref[...]
    e_mat = _expand_matrix(nb, GDN_DV)
    for h in range(GDN_H):
        sl = slice(h * GDN_DK, (h + 1) * GDN_DK)
        q = qa[:, sl]
        k = ka[:, sl]
        v = va[:, sl]
        q = q * lax.rsqrt(jnp.sum(q * q, axis=-1, keepdims=True) + EPS) * (GDN_DK ** -0.5)
        k = k * lax.rsqrt(jnp.sum(k * k, axis=-1, keepdims=True) + EPS)
        beta = _sigmoid(sm[:, SM_BETA + h:SM_BETA + h + 1])
        g = -jnp.exp(_scalar_vec(alog_ref[h])) * jax.nn.softplus(sm[:, SM_DEC + h:SM_DEC + h + 1] + dtb_ref[h])
        eg = jnp.exp(g)
        qk = jnp.sum(q * k, axis=-1, keepdims=True)
        kb = _bcast_cols(k, e_mat)
        qb = _bcast_cols(q, e_mat)
        blk = lambda m, b: m[:, b * GDN_DV:(b + 1) * GDN_DV]
        ks = jnp.concatenate([jnp.sum(s_ref[b, h] * blk(kb, b), axis=0, keepdims=True) for b in range(nb)], axis=0)
        qs = jnp.concatenate([jnp.sum(s_ref[b, h] * blk(qb, b), axis=0, keepdims=True) for b in range(nb)], axis=0)
        w = beta * v - (beta * eg) * ks
        o = qs * eg + qk * w
        for b in range(nb):
            so_ref[b, h] = s_ref[b, h] * eg[b:b + 1] + blk(kb, b) * w[b:b + 1]
        o = _rms(o, nw_ref[...]) * _silu(z[:, sl])
        o_ref[:, sl] = o.astype(bf16)


def _gdn_step(proj, small, conv_t, states, prev, l, conv_w, a_log, dt_bias, norm_w):
    n = proj.shape[0]
    nb = STEP_B
    blk = lambda off: pl.BlockSpec((nb, GDN_QK), lambda i: (i, _blk(off, GDN_QK)))
    cst = lambda j: pl.BlockSpec((CONV_W - 1, nb, GDN_QK), lambda i: (0, i, j))
    cw = lambda j: pl.BlockSpec((CONV_W, GDN_QK), lambda i: (0, j))
    smem = pl.BlockSpec(memory_space=pltpu.SMEM)
    st = pl.BlockSpec((None, nb, GDN_H, GDN_DK, GDN_DV), lambda i: (l, i, 0, 0, 0))
    return _state_call(
        _gdn_step_kernel, (n // nb,),
        [smem, smem, blk(P_QKV_A), blk(P_QKV_A + GDN_QK), blk(P_QKV_A + 2 * GDN_QK), blk(P_Z_A),
         pl.BlockSpec((nb, LANE), lambda i: (i, 0)),
         cst(0), cst(1), cst(2), cw(0), cw(1), cw(2),
         pl.BlockSpec((1, GDN_DV), lambda i: (0, 0)), st],
        [pl.BlockSpec((nb, GDN_V), lambda i: (i, 0)), st],
        [jax.ShapeDtypeStruct((n, GDN_V), bf16), jax.ShapeDtypeStruct(states.shape, f32)],
        [], ("parallel",), "gdn_step",
        (a_log, dt_bias, proj, proj, proj, proj, small, conv_t, conv_t, conv_t, conv_w, conv_w, conv_w, norm_w, states),
        prev)


def _gla_step_kernel(q_ref, k_ref, v_ref, r_ref, sm_ref, wgate_ref, bgate_ref, nw_ref, s_ref,
                     o_ref, so_ref):
    nb = q_ref.shape[0]
    sm = sm_ref[...]
    lane = lax.broadcasted_iota(jnp.int32, sm.shape, 1)
    lr = jnp.where((lane >= SM_LR) & (lane < SM_LR + GLA_RANK), sm, 0.0).astype(bf16)
    la_all = jax.nn.log_sigmoid(_dot(lr, wgate_ref[...]) + bgate_ref[...]) / GLA_TAU
    qa = q_ref[...] * (GLA_DK ** -0.5)
    ka = k_ref[...]
    va = v_ref[...]
    r = r_ref[...]
    for h in range(GLA_H):
        sk = slice(h * GLA_DK, (h + 1) * GLA_DK)
        sv = slice(h * GLA_DV, (h + 1) * GLA_DV)
        q = qa[:, sk]
        k = ka[:, sk]
        v = va[:, sv]
        e = jnp.exp(la_all[:, sk])
        qk = jnp.sum(q * k, axis=-1, keepdims=True)
        e_t = e.T
        k_t = k.T
        qe_t = (q * e).T
        rows = []
        for b in range(nb):
            s = s_ref[b, h]
            vrow = v[b:b + 1]
            rows.append(qk[b:b + 1] * vrow + jnp.sum(s * qe_t[:, b:b + 1], axis=0, keepdims=True))
            so_ref[b, h] = s * e_t[:, b:b + 1] + k_t[:, b:b + 1] * vrow
        o = jnp.concatenate(rows, axis=0)
        o = _rms(o, nw_ref[...]) * _silu(r[:, sv])
        o_ref[:, sv] = o.astype(bf16)


def _gla_step(proj, small, states, prev, l, wgate_pad, bgate, norm_w):
    n = proj.shape[0]
    nb = STEP_B
    st = pl.BlockSpec((None, nb, GLA_H, GLA_DK, GLA_DV), lambda i: (l, i, 0, 0, 0))
    return _state_call(
        _gla_step_kernel, (n // nb,),
        [pl.BlockSpec((nb, GLA_QK), lambda i: (i, _blk(P_Q_B, GLA_QK))),
         pl.BlockSpec((nb, GLA_QK), lambda i: (i, _blk(P_K_B, GLA_QK))),
         pl.BlockSpec((nb, GLA_V), lambda i: (i, _blk(P_V_B, GLA_V))),
         pl.BlockSpec((nb, GLA_V), lambda i: (i, _blk(P_R_B, GLA_V))),
         pl.BlockSpec((nb, LANE), lambda i: (i, 0)),
         pl.BlockSpec((LANE, GLA_QK), lambda i: (0, 0)),
         pl.BlockSpec((1, GLA_QK), lambda i: (0, 0)),
         pl.BlockSpec((1, GLA_DV), lambda i: (0, 0)), st],
        [pl.BlockSpec((nb, GLA_V), lambda i: (i, 0)), st],
        [jax.ShapeDtypeStruct((n, GLA_V), bf16), jax.ShapeDtypeStruct(states.shape, f32)],
        [], ("parallel",), "gla_step",
        (proj, proj, proj, proj, small, wgate_pad, bgate, norm_w, states),
        prev)


def _ssd_step_kernel(alog_ref, dtb_ref, z_ref, xbc_ref, sm_ref, cst_ref, cw_ref, cb_ref, dvec_ref, nw_ref, h_ref,
                     o_ref, ho_ref):
    nb = xbc_ref.shape[0]
    xbc = _silu(_conv_step(xbc_ref, cst_ref, cw_ref) + cb_ref[...])
    sm_t = sm_ref[...].T
    e_mat = _expand_matrix(nb, SSD_N)
    lane_b = lax.broadcasted_iota(jnp.int32, (SSD_P, nb), 1)
    for g in range(SSD_G):
        gs = slice(g * SSD_GW, (g + 1) * SSD_GW)
        bm = xbc[:, SSD_INNER + g * SSD_N:SSD_INNER + (g + 1) * SSD_N]
        cm_t = xbc[:, SSD_INNER + SSD_BC + g * SSD_N:SSD_INNER + SSD_BC + (g + 1) * SSD_N].T
        cb_row = jnp.sum(cm_t * bm.T, axis=0, keepdims=True)
        hg = h_ref[:, pl.ds(g * SSD_HG, SSD_HG)].reshape(nb * SSD_GW, SSD_N)
        ch = _dot(hg.astype(bf16), cm_t.astype(bf16))
        pairs = []
        for jp in range(SSD_HG // 2):
            x_t = xbc[:, g * SSD_GW + jp * LANE:g * SSD_GW + (jp + 1) * LANE].T
            halves = []
            for jj in range(2):
                j = 2 * jp + jj
                hd = g * SSD_HG + j
                dt_row = jax.nn.softplus(sm_t[SM_DT + hd:SM_DT + hd + 1, :] + dtb_ref[hd])
                ea_row = jnp.exp(dt_row * (-jnp.exp(_scalar_vec(alog_ref[hd]))))
                xdt_t = x_t[jj * SSD_P:(jj + 1) * SSD_P] * dt_row
                hi, lo = _split(xdt_t)
                xb = _dot(hi, e_mat) + _dot(lo, e_mat)
                yh = jnp.zeros((SSD_P, nb), f32)
                for b in range(nb):
                    r0 = (b * SSD_HG + j) * SSD_P
                    yh = jnp.where(lane_b == b, ch[r0:r0 + SSD_P], yh)
                    ho_ref[b, hd] = (h_ref[b, hd] * ea_row[:, b:b + 1]
                                     + xb[:, b * SSD_N:(b + 1) * SSD_N] * bm[b:b + 1])
                halves.append(yh * ea_row + cb_row * xdt_t)
            pairs.append(jnp.concatenate(halves, axis=0).T)
        y = jnp.concatenate(pairs, axis=1)
        y = (y + dvec_ref[:, gs] * xbc[:, gs]) * _silu(z_ref[:, gs])
        o_ref[:, gs] = _rms(y, nw_ref[:, gs]).astype(bf16)


def _ssd_step(proj, small, conv_t, states, prev, l, conv_w, conv_b, a_log, dt_bias, dvec, norm_w):
    n = proj.shape[0]
    nb = STEP_B
    smem = pl.BlockSpec(memory_space=pltpu.SMEM)
    const = lambda shape: pl.BlockSpec(shape, lambda i: (0,) * len(shape))
    st = pl.BlockSpec((None, nb, SSD_H, SSD_P, SSD_N), lambda i: (l, i, 0, 0, 0))
    return _state_call(
        _ssd_step_kernel, (n // nb,),
        [smem, smem,
         pl.BlockSpec((nb, SSD_INNER), lambda i: (i, _blk(P_Z_C, SSD_INNER))),
         pl.BlockSpec((nb, SSD_CONV), lambda i: (i, _blk(P_XBC, SSD_CONV))),
         pl.BlockSpec((nb, LANE), lambda i: (i, 0)),
         pl.BlockSpec((CONV_W - 1, nb, SSD_CONV), lambda i: (0, i, 0)),
         const((CONV_W, SSD_CONV)), const((1, SSD_CONV)), const((1, SSD_INNER)), const((1, SSD_INNER)), st],
        [pl.BlockSpec((nb, SSD_INNER), lambda i: (i, 0)), st],
        [jax.ShapeDtypeStruct((n, SSD_INNER), bf16), jax.ShapeDtypeStruct(states.shape, f32)],
        [], ("parallel",), "ssd_step",
        (a_log, dt_bias, proj, proj, small, conv_t, conv_w, conv_b, dvec, norm_w, states),
        prev)


def _merge_kernel(oa_ref, ob_ref, oc_ref, ga_ref, gb_ref, gc_ref, wa_ref, wb_ref, wc_ref, o_ref):
    m = (_sigmoid(ga_ref[...].astype(f32)) * _dot(oa_ref[...], wa_ref[...])
         + _sigmoid(gb_ref[...].astype(f32)) * _dot(ob_ref[...], wb_ref[...])
         + _sigmoid(gc_ref[...].astype(f32)) * _dot(oc_ref[...], wc_ref[...]))
    o_ref[...] = m.astype(bf16)


def _outproj_kernel(m_ref, x_ref, gt_ref, w_ref, o_ref):
    o_ref[...] = x_ref[...] + gt_ref[...] * _dot(m_ref[...], w_ref[...])


def _mixout(oa, ob, oc, proj, x, mod, wa, wb, wc, wo, l, per_row, tm):
    bx, tx, dm = x.shape
    tn = MERGE_COLS
    gate = lambda k: pl.BlockSpec((None, tm, tn), lambda b, i, j: (b, i, _blk(P_GATES + k * D_MODEL, tn) + j))
    act = lambda w: pl.BlockSpec((None, tm, w), lambda b, i, j: (b, i, 0))
    wsp = lambda w: pl.BlockSpec((None, w, tn), lambda b, i, j: (l, 0, j))
    merged = pl.pallas_call(
        _merge_kernel,
        grid=(bx, tx // tm, dm // tn),
        in_specs=[act(GDN_V), act(GLA_V), act(SSD_INNER), gate(0), gate(1), gate(2),
                  wsp(GDN_V), wsp(GLA_V), wsp(SSD_INNER)],
        out_specs=pl.BlockSpec((None, tm, tn), lambda b, i, j: (b, i, j)),
        out_shape=jax.ShapeDtypeStruct((bx, tx, dm), bf16),
        compiler_params=_cp(("parallel", "parallel", "arbitrary")),
        name="merge",
    )(oa, ob, oc, proj, proj, proj, wa, wb, wc)
    to = OUTPROJ_COLS
    nj = dm // to
    if per_row:
        gspec = pl.BlockSpec((None, tm, to), lambda b, i, j: (b, i, 5 * nj + j))
    else:
        gspec = pl.BlockSpec((None, 1, to), lambda b, i, j: (b, 0, 5 * nj + j))
    return pl.pallas_call(
        _outproj_kernel,
        grid=(bx, tx // tm, nj),
        in_specs=[pl.BlockSpec((None, tm, dm), lambda b, i, j: (b, i, 0)),
                  pl.BlockSpec((None, tm, to), lambda b, i, j: (b, i, j)),
                  gspec,
                  pl.BlockSpec((None, dm, to), lambda b, i, j: (l, 0, j))],
        out_specs=pl.BlockSpec((None, tm, to), lambda b, i, j: (b, i, j)),
        out_shape=jax.ShapeDtypeStruct(x.shape, f32),
        compiler_params=_cp(("parallel", "parallel", "arbitrary")),
        name="outproj",
    )(merged, x, mod, wo)


REGROUP_ROWS = 512


def _regroup_kernel(w_ref, o_ref):
    o_ref[...] = w_ref[0].astype(bf16)


def _regroup_rows(wt):
    nl, _, d = wt.shape
    tr = REGROUP_ROWS
    shifts = []
    for name in MAIN_ORDER:
        a, b = W_IN_SRC[name]
        assert (b - a) % tr == 0 and a % SUB == 0
        shifts.append((P_MAIN[name] // tr, a - P_MAIN[name]))

    def src_row(i):
        off = i * (tr // SUB) + shifts[0][1] // SUB
        for k in range(1, len(shifts)):
            off = off + jnp.where(i >= shifts[k][0], (shifts[k][1] - shifts[k - 1][1]) // SUB, 0)
        return off * SUB

    return pl.pallas_call(
        _regroup_kernel,
        grid=(nl, P_TOTAL // tr),
        in_specs=[pl.BlockSpec((pl.Element(1), pl.Element(tr), pl.Element(d)), lambda l, i: (l, src_row(i), 0))],
        out_specs=pl.BlockSpec((None, tr, d), lambda l, i: (l, i, 0)),
        out_shape=jax.ShapeDtypeStruct((nl, P_TOTAL, d), bf16),
        compiler_params=_cp(("parallel", "parallel")),
        name="regroup_w_in",
    )(wt)


def _permute_w_in(w):
    nl, d, _ = w.shape
    main = _regroup_rows(jnp.swapaxes(w, 1, 2))
    small = jnp.concatenate([w[:, :, W_IN_SRC[name][0]:W_IN_SRC[name][1]] for name in SMALL_ORDER], axis=2)
    small = jnp.pad(jnp.swapaxes(small, 1, 2), ((0, 0), (0, LANE - small.shape[2]), (0, 0)))
    return main, small


def _lane_vec(v, lo):
    return jnp.zeros((1, LANE), f32).at[0, lo:lo + v.shape[0]].set(v)


def _layer_params(l, p):
    row = lambda a: a[l].reshape(1, -1)
    wgate = jnp.zeros((LANE, GLA_QK), f32).at[SM_LR:SM_LR + GLA_RANK].set(p["gla_w_gate"][l]).astype(bf16)
    return dict(
        norm1=row(p["norm1"]), norm2=row(p["norm2"]), norm3=row(p["norm3"]),
        gdn_conv_w=p["gdn_conv_w"][l], gdn_a_log=p["gdn_a_log"][l], gdn_dt_bias=p["gdn_dt_bias"][l],
        gdn_alog_l=_lane_vec(p["gdn_a_log"][l], SM_DEC), gdn_dtb_l=_lane_vec(p["gdn_dt_bias"][l], SM_DEC),
        gdn_norm_w=row(p["gdn_norm_w"]),
        gla_wgate=wgate, gla_bgate=row(p["gla_b_gate"]), gla_norm_w=row(p["gla_norm_w"]),
        ssd_conv_w=p["ssd_conv_w"][l], ssd_conv_b=row(p["ssd_conv_b"]), ssd_a_log=p["ssd_a_log"][l],
        ssd_dt_bias=p["ssd_dt_bias"][l],
        ssd_avec_l=_lane_vec(-jnp.exp(p["ssd_a_log"][l]), SM_DT), ssd_dtb_l=_lane_vec(p["ssd_dt_bias"][l], SM_DT),
        ssd_dvec=jnp.repeat(p["ssd_d"][l], SSD_P).reshape(1, -1),
        ssd_norm_w=row(p["ssd_norm_w"]),
    )


def _stacked_weights(p):
    w_in, w_in_small = _permute_w_in(p["w_in"])
    return dict(
        f1=(p["ffn1_wg"], p["ffn1_wu"], p["ffn1_wd"]),
        f2=(p["ffn2_wg"], p["ffn2_wu"], p["ffn2_wd"]),
        w_in=w_in, w_in_small=w_in_small,
        wa=p["w_branch_gdn"].astype(bf16), wb=p["w_branch_gla"].astype(bf16),
        wc=p["w_branch_ssd"].astype(bf16), w_out=p["w_out"].astype(bf16),
    )


def _new_conv_state(buf, raw):
    t = raw.shape[1]
    k = CONV_W - 1
    if t >= k:
        return raw[:, t - k:]
    return jnp.concatenate([buf[:, t:], raw], axis=1)


def _mixer_prompt(proj, small, lp, st):
    gdn_conv, s_gdn, s_gla, ssd_conv, s_ssd = st
    b, t, _ = proj.shape
    c = CHUNK
    sm_spec = pl.BlockSpec((None, c, LANE), lambda bi, ti: (bi, ti, 0))
    st_a, st_b, st_c = _per_seq(GDN_H, GDN_DK, GDN_DV), _per_seq(GLA_H, GLA_DK, GLA_DV), _per_seq(SSD_H, SSD_P, SSD_N)
    gdn_specs = [_tok(GDN_CONV, P_QKV_A), _tok(GDN_V, P_Z_A), sm_spec, _per_seq(CONV_W - 1, GDN_CONV),
                 _const(CONV_W, GDN_CONV), _const(1, LANE), _const(1, LANE), _const(1, GDN_DV), st_a]
    gdn_args = (proj, proj, small, gdn_conv, lp["gdn_conv_w"], lp["gdn_alog_l"], lp["gdn_dtb_l"],
                lp["gdn_norm_w"], s_gdn)
    gla_specs = [_tok(GLA_QK, P_Q_B), _tok(GLA_QK, P_K_B), _tok(GLA_V, P_V_B), _tok(GLA_V, P_R_B), sm_spec,
                 _const(LANE, GLA_QK), _const(1, GLA_QK), _const(1, GLA_DV), st_b]
    gla_args = (proj, proj, proj, proj, small, lp["gla_wgate"], lp["gla_bgate"], lp["gla_norm_w"], s_gla)
    ssd_specs = [_tok(SSD_INNER, P_Z_C), _tok(SSD_CONV, P_XBC), sm_spec, _per_seq(CONV_W - 1, SSD_CONV),
                 _const(CONV_W, SSD_CONV), _const(1, SSD_CONV), _const(1, LANE), _const(1, LANE),
                 _const(1, SSD_INNER), _const(1, SSD_INNER), st_c]
    ssd_args = (proj, proj, small, ssd_conv, lp["ssd_conv_w"], lp["ssd_conv_b"], lp["ssd_avec_l"],
                lp["ssd_dtb_l"], lp["ssd_dvec"], lp["ssd_norm_w"], s_ssd)
    assert (len(gdn_specs), len(gla_specs), len(ssd_specs)) == (N_GDN_IN, N_GLA_IN, N_SSD_IN)
    out_tok = lambda width: pl.BlockSpec((None, c, width), lambda bi, ti: (bi, ti, 0))
    oa, s_gdn_n, ob, s_gla_n, oc, s_ssd_n = pl.pallas_call(
        _mixers_kernel,
        grid=(b, t // c),
        in_specs=gdn_specs + gla_specs + ssd_specs,
        out_specs=[out_tok(GDN_V), st_a, out_tok(GLA_V), st_b, out_tok(SSD_INNER), st_c],
        out_shape=[jax.ShapeDtypeStruct((b, t, GDN_V), bf16), jax.ShapeDtypeStruct(s_gdn.shape, f32),
                   jax.ShapeDtypeStruct((b, t, GLA_V), bf16), jax.ShapeDtypeStruct(s_gla.shape, f32),
                   jax.ShapeDtypeStruct((b, t, SSD_INNER), bf16), jax.ShapeDtypeStruct(s_ssd.shape, f32)],
        scratch_shapes=[pltpu.VMEM((GDN_CONV // LANE, 8 + c, LANE), f32), pltpu.VMEM((GLA_H, GLA_DV, GLA_DK), f32),
                        pltpu.VMEM((SSD_CONV // LANE, 8 + c, LANE), f32),
                        pltpu.VMEM((GDN_V // LANE, c, LANE), f32), pltpu.VMEM((SSD_INNER // LANE, c, LANE), f32)],
        compiler_params=_cp(("parallel", "arbitrary")),
        name="mixers_prompt",
    )(*gdn_args, *gla_args, *ssd_args)
    gdn_conv_n = _new_conv_state(gdn_conv, proj[:, :, P_QKV_A:P_QKV_A + GDN_CONV]).astype(f32)
    ssd_conv_n = _new_conv_state(ssd_conv, proj[:, :, P_XBC:P_XBC + SSD_CONV]).astype(f32)
    return (oa, ob, oc), (gdn_conv_n, s_gdn_n, s_gla_n, ssd_conv_n, s_ssd_n)


def _mixer_sample(proj, small, lp, l, states, prev):
    n = proj.shape[1]
    p2 = proj.reshape(n, P_TOTAL)
    s2 = small.reshape(n, LANE)
    gdn_conv, ssd_conv = states[0][l], states[3][l]
    gct = jnp.swapaxes(gdn_conv, 0, 1)
    sct = jnp.swapaxes(ssd_conv, 0, 1)
    pv = (None,) * 5 if prev is None else prev
    oa, s_gdn_n = _gdn_step(p2, s2, gct, states[1], pv[1], l, lp["gdn_conv_w"], lp["gdn_a_log"],
                            lp["gdn_dt_bias"], lp["gdn_norm_w"])
    ob, s_gla_n = _gla_step(p2, s2, states[2], pv[2], l, lp["gla_wgate"], lp["gla_bgate"], lp["gla_norm_w"])
    oc, s_ssd_n = _ssd_step(p2, s2, sct, states[4], pv[4], l, lp["ssd_conv_w"], lp["ssd_conv_b"], lp["ssd_a_log"],
                            lp["ssd_dt_bias"], lp["ssd_dvec"], lp["ssd_norm_w"])
    raw = p2.reshape(n, 1, P_TOTAL)
    gdn_conv_n = _new_conv_state(gdn_conv, raw[:, :, P_QKV_A:P_QKV_A + GDN_CONV])
    ssd_conv_n = _new_conv_state(ssd_conv, raw[:, :, P_XBC:P_XBC + SSD_CONV])
    outs = tuple(o.reshape(1, n, -1) for o in (oa, ob, oc))
    return outs, (gdn_conv_n, s_gdn_n, s_gla_n, ssd_conv_n, s_ssd_n)


def _trunk(x, mods, lps, sw, ffn_w, states, per_row, tm, tm_in, final_w):
    nl = len(lps)
    per_layer = []
    prev = None

    def ffn(x, mod, k0, nw, name, l, fw, final):
        if per_row:
            x, ffn_w[l, name] = _ffn_cast(x, mod, k0, nw, *sw[name], l, fw, final)
            return x
        return _ffn(x, mod, k0, nw, ffn_w[l, name], fw, per_row, tm, final)

    for l in range(nl):
        lp, mod = lps[l], mods[l]
        last = l == nl - 1
        x = ffn(x, mod, 0, lp["norm1"], "f1", l, lp["norm1"], False)
        proj, small = _inproj(x, mod, lp["norm2"], sw["w_in"], sw["w_in_small"], l, per_row, tm_in,
                              f32 if per_row else bf16)
        if per_row:
            (oa, ob, oc), st = _mixer_sample(proj, small, lp, l, states, prev)
            prev = st
        else:
            (oa, ob, oc), st = _mixer_prompt(proj, small, lp, tuple(s[l] for s in states))
        per_layer.append(st)
        x = _mixout(oa, ob, oc, proj, x, mod, sw["wa"], sw["wb"], sw["wc"], sw["w_out"], l, per_row, tm_in)
        x = ffn(x, mod, 6, lp["norm3"], "f2", l, final_w if last else lp["norm3"], last)
    stack = lambda i: jnp.stack([st[i] for st in per_layer])
    if per_row:
        new_states = (stack(0), prev[1], prev[2], stack(3), prev[4])
    else:
        new_states = tuple(stack(i) for i in range(5))
    return x, new_states


def kernel(x_prompt, x_sample, state_gdn_conv, state_gdn, state_gla, state_ssd_conv, state_ssd, c_prompt, c_sample, w_ada, b_ada, norm1, norm2, norm3, ffn1_wg, ffn1_wu, ffn1_wd, ffn2_wg, ffn2_wu, ffn2_wd, w_in, gdn_conv_w, gdn_a_log, gdn_dt_bias, gdn_norm_w, gla_w_gate, gla_b_gate, gla_norm_w, ssd_conv_w, ssd_conv_b, ssd_a_log, ssd_dt_bias, ssd_d, ssd_norm_w, w_branch_gdn, w_branch_gla, w_branch_ssd, w_out, final_norm):
    p = dict(norm1=norm1, norm2=norm2, norm3=norm3,
             ffn1_wg=ffn1_wg, ffn1_wu=ffn1_wu, ffn1_wd=ffn1_wd, ffn2_wg=ffn2_wg, ffn2_wu=ffn2_wu, ffn2_wd=ffn2_wd,
             w_in=w_in, gdn_conv_w=gdn_conv_w, gdn_a_log=gdn_a_log, gdn_dt_bias=gdn_dt_bias, gdn_norm_w=gdn_norm_w,
             gla_w_gate=gla_w_gate, gla_b_gate=gla_b_gate, gla_norm_w=gla_norm_w,
             ssd_conv_w=ssd_conv_w, ssd_conv_b=ssd_conv_b, ssd_a_log=ssd_a_log, ssd_dt_bias=ssd_dt_bias,
             ssd_d=ssd_d, ssd_norm_w=ssd_norm_w,
             w_branch_gdn=w_branch_gdn, w_branch_gla=w_branch_gla, w_branch_ssd=w_branch_ssd, w_out=w_out)
    nl = w_ada.shape[0]
    bp, tp, dm = x_prompt.shape
    bs = x_sample.shape[0]
    assert x_sample.shape[1] == 1 and tp % CHUNK == 0 and bs % STEP_B == 0 and dm == D_MODEL
    lps = [_layer_params(l, p) for l in range(nl)]
    sw = _stacked_weights(p)
    fw = final_norm.reshape(1, dm)

    rows = bp + bs
    rpad = -(-rows // 8) * 8
    c_all = jnp.concatenate([c_prompt, c_sample, jnp.zeros((rpad - rows, dm), f32)], axis=0)
    mod = _ada_mod(c_all, w_ada, b_ada)
    mod_p = [mod[l, :bp].reshape(bp, 1, N_MOD * dm) for l in range(nl)]
    mod_s = [mod[l, bp:rows].reshape(1, bs, N_MOD * dm) for l in range(nl)]

    sample_states = (state_gdn_conv, state_gdn, state_gla, state_ssd_conv, state_ssd)
    prompt_states = tuple(jnp.zeros((s.shape[0], bp) + s.shape[2:], x_prompt.dtype) for s in sample_states)
    tm_p = ROW_TILE if tp % ROW_TILE == 0 else CHUNK
    tm_in = ROW_TILE_WIDE if tp % ROW_TILE_WIDE == 0 else tm_p
    ffn_w = {}
    y_s, st_s = _trunk(x_sample.reshape(1, bs, dm), mod_s, lps, sw, ffn_w, sample_states, True, bs, bs, fw)
    y_p, st_p = _trunk(x_prompt, mod_p, lps, sw, ffn_w, prompt_states, False, tm_p, tm_in, fw)
    return (y_p, y_s.reshape(bs, 1, dm)) + st_p + st_s
```

```python
import functools

import jax
import jax.numpy as jnp
from jax import lax
from jax.experimental import pallas as pl
from jax.experimental.pallas import tpu as pltpu

f32 = jnp.float32
bf16 = jnp.bfloat16
HI = lax.Precision.HIGHEST

EPS = 1e-6
D_MODEL = 2048
N_MOD = 9
CHUNK = 64
CONV_W = 4
GDN_H, GDN_DK, GDN_DV = 8, 128, 128
GLA_H, GLA_DK, GLA_DV, GLA_RANK, GLA_TAU = 4, 128, 256, 16, 16.0
SSD_H, SSD_P, SSD_G, SSD_N = 32, 64, 4, 128
SSD_HG = SSD_H // SSD_G
GDN_QK = GDN_H * GDN_DK
GDN_V = GDN_H * GDN_DV
GDN_CONV = 2 * GDN_QK + GDN_V
GLA_QK = GLA_H * GLA_DK
GLA_V = GLA_H * GLA_DV
SSD_INNER = SSD_H * SSD_P
SSD_BC = SSD_G * SSD_N
SSD_CONV = SSD_INNER + 2 * SSD_BC
SSD_GW = SSD_HG * SSD_P

LANE = 128
MXU_WIDTH = 256

ROW_TILE = 512
ROW_TILE_WIDE = 1024
FF_TILE = 1024
FF_SUB = MXU_WIDTH
ADA_COLS = 1024
INPROJ_COLS = 1024
MERGE_COLS = 512
OUTPROJ_COLS = 1024

IN_SPLITS = (("qkv_a", GDN_CONV), ("z_a", GDN_V), ("beta", GDN_H), ("dec", GDN_H),
             ("q_b", GLA_QK), ("k_b", GLA_QK), ("v_b", GLA_V), ("lr", GLA_RANK), ("r_b", GLA_V),
             ("z_c", SSD_INNER), ("xbc", SSD_CONV), ("dt", SSD_H), ("gates", 3 * D_MODEL))
MAIN_ORDER = ("qkv_a", "xbc", "z_c", "gates", "z_a", "q_b", "k_b", "v_b", "r_b")
SMALL_ORDER = ("beta", "dec", "lr", "dt")


def _layout():
    src, off = {}, 0
    for name, w in IN_SPLITS:
        src[name] = (off, off + w)
        off += w
    main, small, d = {}, {}, 0
    for name in MAIN_ORDER:
        w = src[name][1] - src[name][0]
        main[name] = d
        d += w
    total = d
    d = 0
    for name in SMALL_ORDER:
        small[name] = d
        d += src[name][1] - src[name][0]
    assert d <= LANE
    return src, main, small, total


W_IN_SRC, P_MAIN, P_SM, P_TOTAL = _layout()
P_QKV_A, P_XBC, P_Z_C, P_GATES, P_Z_A = (P_MAIN[k] for k in ("qkv_a", "xbc", "z_c", "gates", "z_a"))
P_Q_B, P_K_B, P_V_B, P_R_B = (P_MAIN[k] for k in ("q_b", "k_b", "v_b", "r_b"))
SM_BETA, SM_DEC, SM_LR, SM_DT = (P_SM[k] for k in SMALL_ORDER)

VMEM_LIMIT = 56 * 1024 * 1024


def _cp(sem):
    return pltpu.CompilerParams(dimension_semantics=sem, vmem_limit_bytes=VMEM_LIMIT)


def _blk(off, width):
    assert off % width == 0, (off, width)
    return off // width


def _sigmoid(x):
    return 0.5 + 0.5 * jnp.tanh(0.5 * x)


def _silu(x):
    h = 0.5 * x
    return h + h * jnp.tanh(h)


def _rms(x, w):
    return x * lax.rsqrt(jnp.mean(x * x, axis=-1, keepdims=True) + EPS) * w


def _dot(a, b):
    return jnp.dot(a, b, preferred_element_type=f32)


def _dot_nt(a, b):
    return lax.dot_general(a, b, (((1,), (1,)), ((), ())), preferred_element_type=f32)


def _dot_tn(a, b):
    return lax.dot_general(a, b, (((0,), (0,)), ((), ())), preferred_element_type=f32)


def _split(x):
    hi = x.astype(bf16)
    return hi, (x - hi.astype(f32)).astype(bf16)


def _dot3(a, b):
    return _dot(a[0], b[0]) + (_dot(a[0], b[1]) + _dot(a[1], b[0]))


def _dot_hi(a, b):
    return jnp.dot(a, b, precision=HI, preferred_element_type=f32)


def _tri_masks(c):
    row = lax.broadcasted_iota(jnp.int32, (c, c), 0)
    col = lax.broadcasted_iota(jnp.int32, (c, c), 1)
    return row >= col, row > col, row == col


SUB = 8


def _row_time(idx, c):
    n = c // SUB
    assert n & (n - 1) == 0, "chunk / SUB must be a power of two"
    sh = n.bit_length() - 1
    return ((idx & (n - 1)) << 3) | (idx >> sh)


def _tri_masks_residue(c, cols=None):
    cols = c if cols is None else cols
    row = _row_time(lax.broadcasted_iota(jnp.int32, (c, cols), 0), c)
    col = _row_time(lax.broadcasted_iota(jnp.int32, (c, cols), 1) & (c - 1), c)
    return row >= col, row > col, row == col


def _load_residue(ref, c):
    n = c // SUB
    return jnp.concatenate([ref[pl.ds(r, n, stride=SUB), :] for r in range(SUB)], axis=0)


def _store_natural(nat_ref, x, c, lo):
    n = c // SUB
    for j in range(x.shape[1] // LANE):
        for r in range(SUB):
            nat_ref[lo // LANE + j, pl.ds(r, n, stride=SUB), :] = x[r * n:(r + 1) * n, j * LANE:(j + 1) * LANE]


def _read_natural(nat_ref, lo, width):
    return jnp.concatenate([nat_ref[lo // LANE + j] for j in range(width // LANE)], axis=1)


def _scalar_vec(s):
    return jnp.full((1, 1), s, f32)


def _ada_kernel(c_ref, w_ref, b_ref, o_ref):
    s = _silu(c_ref[...]).astype(bf16)
    o_ref[...] = _dot(s, w_ref[...].astype(bf16)) + b_ref[...]


def _ada_mod(c_all, w_ada, b_ada):
    nl, dm, n = w_ada.shape
    r = c_all.shape[0]
    tn = ADA_COLS
    return pl.pallas_call(
        _ada_kernel,
        grid=(nl, n // tn),
        in_specs=[pl.BlockSpec((r, dm), lambda l, j: (0, 0)),
                  pl.BlockSpec((None, dm, tn), lambda l, j: (l, 0, j)),
                  pl.BlockSpec((None, 1, tn), lambda l, j: (l, 0, j))],
        out_specs=pl.BlockSpec((None, r, tn), lambda l, j: (l, 0, j)),
        out_shape=jax.ShapeDtypeStruct((nl, r, n), f32),
        compiler_params=_cp(("arbitrary", "arbitrary")),
        name="ada_mod",
    )(c_all, w_ada, b_ada.reshape(nl, 1, n))


def _mod_spec(per_row, tm, chunk):
    if per_row:
        return pl.BlockSpec((None, tm, D_MODEL), lambda b, i, j: (b, i, chunk))
    return pl.BlockSpec((None, 1, D_MODEL), lambda b, i, j: (b, 0, chunk))


def _ffn_prologue(x_ref, sh_ref, sc_ref, nw_ref, h_ref, acc_ref):
    y = _rms(x_ref[...], nw_ref[...])
    h_ref[...] = (y * (1.0 + sc_ref[...]) + sh_ref[...]).astype(bf16)
    acc_ref[...] = jnp.zeros_like(acc_ref)


def _ffn_accumulate(h_ref, acc_ref, wg, wu, wd):
    h = h_ref[...]
    width = wg.shape[1]
    sub = FF_SUB if width % FF_SUB == 0 else width
    parts = [(_dot(h, wg[:, s:s + sub]), _dot(h, wu[:, s:s + sub])) for s in range(0, width, sub)]
    acc = acc_ref[...]
    for i, (g, u) in enumerate(parts):
        acc = acc + _dot((_silu(g) * u).astype(bf16), wd[i * sub:(i + 1) * sub, :])
    acc_ref[...] = acc


def _ffn_epilogue(x_ref, gt_ref, fw_ref, acc_ref, o_ref, final):
    y = x_ref[...] + 0.5 * gt_ref[...] * acc_ref[...]
    if final:
        y = _rms(y, fw_ref[...])
    o_ref[...] = y


def _ffn_kernel(x_ref, sh_ref, sc_ref, gt_ref, nw_ref, fw_ref, wg_ref, wu_ref, wd_ref, *rest, final, has_tail):
    if has_tail:
        wgt_ref, wut_ref, wdt_ref, o_ref, h_ref, acc_ref = rest
    else:
        o_ref, h_ref, acc_ref = rest
    f = pl.program_id(2)
    last = pl.num_programs(2) - 1
    pl.when(f == 0)(lambda: _ffn_prologue(x_ref, sh_ref, sc_ref, nw_ref, h_ref, acc_ref))
    if has_tail:
        pl.when(f < last)(lambda: _ffn_accumulate(h_ref, acc_ref, wg_ref, wu_ref, wd_ref))
        pl.when(f == last)(lambda: _ffn_accumulate(h_ref, acc_ref, wgt_ref, wut_ref, wdt_ref))
    else:
        _ffn_accumulate(h_ref, acc_ref, wg_ref, wu_ref, wd_ref)
    pl.when(f == last)(lambda: _ffn_epilogue(x_ref, gt_ref, fw_ref, acc_ref, o_ref, final))


def _ffn(x, mod, k0, nw, w, fw, per_row, tm, final):
    bx, tx, dm = x.shape
    tf = FF_TILE
    n_full = w[0].shape[1] // tf
    has_tail = len(w) > 3
    full = lambda f: jnp.minimum(f, n_full - 1)
    w_specs = [pl.BlockSpec((dm, tf), lambda b, i, f: (0, full(f))),
               pl.BlockSpec((dm, tf), lambda b, i, f: (0, full(f))),
               pl.BlockSpec((tf, dm), lambda b, i, f: (full(f), 0))]
    if has_tail:
        ft = w[3].shape[1]
        once = pl.Buffered(1)
        w_specs += [pl.BlockSpec((dm, ft), lambda b, i, f: (0, 0), pipeline_mode=once),
                    pl.BlockSpec((dm, ft), lambda b, i, f: (0, 0), pipeline_mode=once),
                    pl.BlockSpec((ft, dm), lambda b, i, f: (0, 0), pipeline_mode=once)]
    return pl.pallas_call(
        functools.partial(_ffn_kernel, final=final, has_tail=has_tail),
        grid=(bx, tx // tm, n_full + int(has_tail)),
        in_specs=[pl.BlockSpec((None, tm, dm), lambda b, i, f: (b, i, 0)),
                  _mod_spec(per_row, tm, k0), _mod_spec(per_row, tm, k0 + 1), _mod_spec(per_row, tm, k0 + 2),
                  pl.BlockSpec((1, dm), lambda b, i, f: (0, 0)),
                  pl.BlockSpec((1, dm), lambda b, i, f: (0, 0))] + w_specs,
        out_specs=pl.BlockSpec((None, tm, dm), lambda b, i, f: (b, i, 0)),
        out_shape=jax.ShapeDtypeStruct(x.shape, f32),
        scratch_shapes=[pltpu.VMEM((tm, dm), bf16), pltpu.VMEM((tm, dm), f32)],
        compiler_params=_cp(("parallel", "parallel", "arbitrary")),
        name="ffn",
    )(x, mod, mod, mod, nw, fw, *w)


FF_CAST_TILE = 256


def _ffn_cast_kernel(x_ref, sh_ref, sc_ref, gt_ref, nw_ref, fw_ref, wg_ref, wu_ref, wd_ref, *rest, final, has_tail):
    if has_tail:
        wgt_ref, wut_ref, wdt_ref, o_ref, wgb_ref, wub_ref, wdb_ref, wgtb_ref, wutb_ref, wdtb_ref, h_ref, acc_ref = rest
    else:
        o_ref, wgb_ref, wub_ref, wdb_ref, h_ref, acc_ref = rest
    f = pl.program_id(2)
    last = pl.num_programs(2) - 1
    pl.when(f == 0)(lambda: _ffn_prologue(x_ref, sh_ref, sc_ref, nw_ref, h_ref, acc_ref))

    def cast_accumulate(src, dst):
        for s_ref, d_ref in zip(src, dst):
            d_ref[...] = s_ref[...].reshape(d_ref.shape).astype(bf16)
        _ffn_accumulate(h_ref, acc_ref, *dst)

    if has_tail:
        pl.when(f < last)(lambda: cast_accumulate((wg_ref, wu_ref, wd_ref), (wgb_ref, wub_ref, wdb_ref)))
        pl.when(f == last)(lambda: cast_accumulate((wgt_ref, wut_ref, wdt_ref), (wgtb_ref, wutb_ref, wdtb_ref)))
    else:
        cast_accumulate((wg_ref, wu_ref, wd_ref), (wgb_ref, wub_ref, wdb_ref))
    pl.when(f == last)(lambda: _ffn_epilogue(x_ref, gt_ref, fw_ref, acc_ref, o_ref, final))


def _ffn_cast(x, mod, k0, nw, wg, wu, wd, l, fw, final):
    bx, tx, dm = x.shape
    assert bx == 1
    f_all = wg.shape[2]
    tf = FF_CAST_TILE
    cut = f_all - f_all % FF_TILE
    n_full = cut // tf
    has_tail = cut < f_all
    full = lambda f: jnp.minimum(f, n_full - 1)
    w_in_specs = [pl.BlockSpec((None, dm, tf), lambda b, i, f: (l, 0, full(f))),
                  pl.BlockSpec((None, dm, tf), lambda b, i, f: (l, 0, full(f))),
                  pl.BlockSpec((None, tf, dm), lambda b, i, f: (l, full(f), 0))]
    w_out_specs = [pl.BlockSpec((dm, tf), lambda b, i, f: (0, full(f))),
                   pl.BlockSpec((dm, tf), lambda b, i, f: (0, full(f))),
                   pl.BlockSpec((tf, dm), lambda b, i, f: (full(f), 0))]
    w_out_shapes = [jax.ShapeDtypeStruct((dm, cut), bf16), jax.ShapeDtypeStruct((dm, cut), bf16),
                    jax.ShapeDtypeStruct((cut, dm), bf16)]
    args = [wg, wu, wd]
    if has_tail:
        ft = f_all - cut
        args += [wg, wu, wd]
        el = lambda *shape: tuple(pl.Element(s) for s in shape)
        w_in_specs += [pl.BlockSpec(el(1, dm, ft), lambda b, i, f: (l, 0, cut)),
                       pl.BlockSpec(el(1, dm, ft), lambda b, i, f: (l, 0, cut)),
                       pl.BlockSpec(el(1, ft, dm), lambda b, i, f: (l, cut, 0))]
        w_out_specs += [pl.BlockSpec((dm, ft), lambda b, i, f: (0, 0)), pl.BlockSpec((dm, ft), lambda b, i, f: (0, 0)),
                        pl.BlockSpec((ft, dm), lambda b, i, f: (0, 0))]
        w_out_shapes += [jax.ShapeDtypeStruct((dm, ft), bf16), jax.ShapeDtypeStruct((dm, ft), bf16),
                         jax.ShapeDtypeStruct((ft, dm), bf16)]
    out = pl.pallas_call(
        functools.partial(_ffn_cast_kernel, final=final, has_tail=has_tail),
        grid=(1, 1, n_full + int(has_tail)),
        in_specs=[pl.BlockSpec((None, tx, dm), lambda b, i, f: (b, i, 0)),
                  _mod_spec(True, tx, k0), _mod_spec(True, tx, k0 + 1), _mod_spec(True, tx, k0 + 2),
                  pl.BlockSpec((1, dm), lambda b, i, f: (0, 0)),
                  pl.BlockSpec((1, dm), lambda b, i, f: (0, 0))] + w_in_specs,
        out_specs=[pl.BlockSpec((None, tx, dm), lambda b, i, f: (b, i, 0))] + w_out_specs,
        out_shape=[jax.ShapeDtypeStruct(x.shape, f32)] + w_out_shapes,
        scratch_shapes=[pltpu.VMEM((tx, dm), bf16), pltpu.VMEM((tx, dm), f32)],
        compiler_params=_cp(("arbitrary", "arbitrary", "arbitrary")),
        name="ffn_cast",
    )(x, mod, mod, mod, nw, fw, *args)
    return out[0], tuple(out[1:])


def _inproj_kernel(x_ref, sh_ref, sc_ref, nw_ref, w_ref, ws_ref, o_ref, os_ref, h_ref):
    @pl.when(pl.program_id(2) == 0)
    def _():
        y = _rms(x_ref[...], nw_ref[...])
        h = (y * (1.0 + sc_ref[...]) + sh_ref[...]).astype(bf16)
        h_ref[...] = h
        os_ref[...] = _dot_nt(h, ws_ref[...].astype(bf16))

    o_ref[...] = _dot_nt(h_ref[...], w_ref[...]).astype(o_ref.dtype)


def _inproj(x, mod, nw, w, ws, l, per_row, tm, out_dtype):
    bx, tx, dm = x.shape
    n = w.shape[1]
    tn = INPROJ_COLS
    return pl.pallas_call(
        _inproj_kernel,
        grid=(bx, tx // tm, n // tn),
        in_specs=[pl.BlockSpec((None, tm, dm), lambda b, i, j: (b, i, 0)),
                  _mod_spec(per_row, tm, 3), _mod_spec(per_row, tm, 4),
                  pl.BlockSpec((1, dm), lambda b, i, j: (0, 0)),
                  pl.BlockSpec((None, tn, dm), lambda b, i, j: (l, j, 0)),
                  pl.BlockSpec((None, LANE, dm), lambda b, i, j: (l, 0, 0))],
        out_specs=[pl.BlockSpec((None, tm, tn), lambda b, i, j: (b, i, j)),
                   pl.BlockSpec((None, tm, LANE), lambda b, i, j: (b, i, 0))],
        out_shape=[jax.ShapeDtypeStruct((bx, tx, n), out_dtype),
                   jax.ShapeDtypeStruct((bx, tx, LANE), f32)],
        scratch_shapes=[pltpu.VMEM((tm, dm), bf16)],
        compiler_params=_cp(("parallel", "parallel", "arbitrary")),
        name="inproj",
    )(x, mod, mod, nw, w, ws)


def _conv_load(ext_ref, x_ref, cst_ref, first):
    c = x_ref.shape[0]
    nblk = ext_ref.shape[0]

    @pl.when(first)
    def _():
        for j in range(nblk):
            ext_ref[j, pl.ds(5, CONV_W - 1), :] = cst_ref[:, j * LANE:(j + 1) * LANE]

    for j in range(nblk):
        ext_ref[j, pl.ds(8, c), :] = x_ref[:, j * LANE:(j + 1) * LANE].astype(f32)


def _conv_cols(ext_ref, w_ref, b_ref, lo, width, c):
    n = c // SUB
    cols = []
    for blk in range(lo // LANE, (lo + width) // LANE):
        sl = pl.ds(blk * LANE, LANE)
        w = w_ref[:, sl]
        taps = [ext_ref[blk, pl.ds(8 - (CONV_W - 1) + k, n, stride=SUB), :] for k in range(SUB + CONV_W - 1)]
        ys = []
        for r in range(SUB):
            y = taps[r] * w[0:1]
            for j in range(1, CONV_W):
                y = y + taps[r + j] * w[j:j + 1]
            ys.append(y)
        y = jnp.concatenate(ys, axis=0)
        if b_ref is not None:
            y = y + b_ref[:, sl]
        cols.append(_silu(y))
    return cols[0] if len(cols) == 1 else jnp.concatenate(cols, axis=1)


def _conv_carry(ext_ref, c):
    for j in range(ext_ref.shape[0]):
        ext_ref[j, pl.ds(5, CONV_W - 1), :] = ext_ref[j, pl.ds(8 + c - (CONV_W - 1), CONV_W - 1), :]


def _cumsum_rows(x, tri):
    return _dot_hi(jnp.where(tri, 1.0, 0.0).astype(f32), x)


def _gdn_kernel(qkv_ref, z_ref, sm_ref, cst_ref, cw_ref, alog_ref, dtb_ref, nw_ref, s0_ref,
                o_ref, s_ref, ext_ref, nat_ref):
    c = qkv_ref.shape[0]
    first = pl.program_id(1) == 0

    @pl.when(first)
    def _():
        s_ref[...] = s0_ref[...]

    _conv_load(ext_ref, qkv_ref, cst_ref, first)

    sm = _load_residue(sm_ref, c)
    beta_l = _sigmoid(sm)
    g_l = -jnp.exp(alog_ref[...]) * jax.nn.softplus(sm + dtb_ref[...])
    tri, strict, eye = _tri_masks_residue(c)
    gc_all = _cumsum_rows(g_l, tri)
    gc_t = gc_all.T
    eye_f = jnp.where(eye, 1.0, 0.0).astype(f32)

    hs = range(GDN_H)
    qb, kb, ks, decay, eg, gcs, nm, rhs = [], [], [], [], [], [], [], []
    for h in hs:
        q = _conv_cols(ext_ref, cw_ref, None, h * GDN_DK, GDN_DK, c)
        k = _conv_cols(ext_ref, cw_ref, None, GDN_QK + h * GDN_DK, GDN_DK, c)
        v = _conv_cols(ext_ref, cw_ref, None, 2 * GDN_QK + h * GDN_DV, GDN_DV, c)
        q = q * lax.rsqrt(jnp.sum(q * q, axis=-1, keepdims=True) + EPS) * (GDN_DK ** -0.5)
        k = k * lax.rsqrt(jnp.sum(k * k, axis=-1, keepdims=True) + EPS)
        beta = beta_l[:, SM_BETA + h:SM_BETA + h + 1]
        gc = gc_all[:, SM_DEC + h:SM_DEC + h + 1]
        d = jnp.exp(jnp.where(tri, gc - gc_t[SM_DEC + h:SM_DEC + h + 1, :], -jnp.inf))
        e = jnp.exp(gc)
        qb.append(q.astype(bf16))
        kb.append(k.astype(bf16))
        ks.append(k)
        decay.append(d)
        eg.append(e)
        gcs.append(gc)
        nm.append(jnp.where(strict, beta * d * _dot_nt(kb[h], kb[h]), 0.0))
        rhs.append(jnp.concatenate([v * beta, k * (beta * e)], axis=1))
        yield

    assert c // GDN_BLK <= 4 and GDN_BLK & (GDN_BLK - 1) == 0
    blk_sh = GDN_BLK.bit_length() - 1
    same_blk = (_row_time(lax.broadcasted_iota(jnp.int32, (c, c), 0), c) >> blk_sh
                == _row_time(lax.broadcasted_iota(jnp.int32, (c, c), 1), c) >> blk_sh)
    nd = [jnp.where(same_blk, nm[h], 0.0) for h in hs]
    lo_s = [_split(nm[h] - nd[h]) for h in hs]
    dinv = [eye_f - nd[h] for h in hs]
    pw_s = [_split(nd[h]) for h in hs]
    for i in range(GDN_BLK.bit_length() - 2):
        pw_s = [_split(_dot3(pw_s[h], pw_s[h])) for h in hs]
        yield
        dinv = [dinv[h] + _dot3(_split(dinv[h]), pw_s[h]) for h in hs]
        yield
    dinv_s = [_split(dinv[h]) for h in hs]
    m = [_dot3(dinv_s[h], lo_s[h]) for h in hs]
    m_s = [_split(m[h]) for h in hs]
    yield
    m2 = [_dot3(m_s[h], m_s[h]) for h in hs]
    yield
    t1 = [eye_f - m[h] + m2[h] - _dot3(m_s[h], _split(m2[h])) for h in hs]
    yield
    inv = [_dot3(_split(t1[h]), dinv_s[h]) for h in hs]
    yield
    sol = []
    for h in hs:
        inv_hi, inv_lo = _split(inv[h])
        rb = rhs[h].astype(bf16)
        sol.append(_dot(inv_hi, rb) + _dot(inv_lo, rb))
    yield

    s = [s_ref[h] for h in hs]
    sb = [s[h].astype(bf16) for h in hs]
    wb = [(sol[h][:, 0:GDN_DV] - _dot(sol[h][:, GDN_DV:].astype(bf16), sb[h])).astype(bf16) for h in hs]
    yield
    qk = [(_dot_nt(qb[h], kb[h]) * decay[h]).astype(bf16) for h in hs]
    yield
    o = [_dot(qb[h], sb[h]) * eg[h] + _dot(qk[h], wb[h]) for h in hs]
    yield
    for h in hs:
        g_last = gcs[h][c - 1:c]
        kd = (ks[h] * jnp.exp(g_last - gcs[h])).astype(bf16)
        s_ref[h] = s[h] * jnp.exp(g_last) + _dot_tn(kd, wb[h])
    yield
    for h in hs:
        sl = slice(h * GDN_DV, (h + 1) * GDN_DV)
        _store_natural(nat_ref, _rms(o[h], nw_ref[...]), c, h * GDN_DV)
        o_ref[:, sl] = (_read_natural(nat_ref, h * GDN_DV, GDN_DV) * _silu(z_ref[:, sl].astype(f32))).astype(bf16)
        yield

    _conv_carry(ext_ref, c)


def _tok(width, off):
    return pl.BlockSpec((None, CHUNK, width), lambda bi, ti: (bi, ti, _blk(off, width)))


def _per_seq(*shape):
    return pl.BlockSpec((None,) + shape, lambda bi, ti: (bi,) + (0,) * len(shape))


def _const(*shape):
    return pl.BlockSpec(shape, lambda bi, ti: (0,) * len(shape))


N_GDN_IN, N_GLA_IN, N_SSD_IN = 9, 9, 11
_DONE = object()


def _mixers_kernel(*refs):
    i0, i1, i2 = N_GDN_IN, N_GDN_IN + N_GLA_IN, N_GDN_IN + N_GLA_IN + N_SSD_IN
    gdn_in, gla_in, ssd_in = refs[:i0], refs[i0:i1], refs[i1:i2]
    oa_ref, sa_ref, ob_ref, sb_ref, oc_ref, sc_ref, ext_a_ref, st_b_ref, ext_c_ref, nat_a_ref, nat_c_ref = refs[i2:]
    live = [_gdn_kernel(*gdn_in, oa_ref, sa_ref, ext_a_ref, nat_a_ref),
            _gla_kernel(*gla_in, ob_ref, sb_ref, st_b_ref),
            _ssd_kernel(*ssd_in, oc_ref, sc_ref, ext_c_ref, nat_c_ref)]
    while live:
        for gen in list(live):
            if next(gen, _DONE) is _DONE:
                live.remove(gen)


GLA_SUB = 16
GDN_BLK = 16


def _gla_kernel(q_ref, k_ref, v_ref, r_ref, sm_ref, wgate_ref, bgate_ref, nw_ref, s0_ref,
                o_ref, so_ref, st_ref):
    c = q_ref.shape[0]
    ti = pl.program_id(1)

    @pl.when(ti == 0)
    def _():
        for h in range(GLA_H):
            st_ref[h] = s0_ref[h].T

    sm = sm_ref[...]
    lane = lax.broadcasted_iota(jnp.int32, sm.shape, 1)
    lr = jnp.where((lane >= SM_LR) & (lane < SM_LR + GLA_RANK), sm, 0.0).astype(bf16)
    la = jax.nn.log_sigmoid(_dot(lr, wgate_ref[...]) + bgate_ref[...]) / GLA_TAU
    tri, _, _ = _tri_masks(c)
    b_all = _cumsum_rows(la, tri)

    col = lax.broadcasted_iota(jnp.int32, (GLA_SUB, c), 1)
    row = lax.broadcasted_iota(jnp.int32, (GLA_SUB, c), 0)
    hs = range(GLA_H)
    qs, ks, vbs, bs, atts = [], [], [], [], []
    for h in hs:
        sk = slice(h * GLA_DK, (h + 1) * GLA_DK)
        q = q_ref[:, sk].astype(f32) * (GLA_DK ** -0.5)
        k = k_ref[:, sk].astype(f32)
        b = b_all[:, sk]
        qs.append(q)
        ks.append(k)
        bs.append(b)
        vbs.append(v_ref[:, h * GLA_DV:(h + 1) * GLA_DV].astype(bf16))
        att_rows = []
        for i in range(c // GLA_SUB):
            lo = i * GLA_SUB
            b_i = b[lo:lo + GLA_SUB]
            q_i = q[lo:lo + GLA_SUB]
            b_top = b[lo:lo + 1]
            att = jnp.zeros((GLA_SUB, c), f32)
            if i > 0:
                q_t = (q_i * jnp.exp(b_i - b_top)).astype(bf16)
                k_t = (k * jnp.exp(jnp.minimum(b_top - b, 0.0))).astype(bf16)
                att = jnp.where(col < lo, _dot_nt(q_t, k_t), 0.0)
            for sl in range(GLA_SUB):
                sidx = lo + sl
                e = jnp.exp(jnp.minimum(b_i - b[sidx:sidx + 1], 0.0))
                p = jnp.sum(q_i * k[sidx:sidx + 1] * e, axis=1, keepdims=True)
                att = jnp.where((col == sidx) & (row >= sl), p, att)
            att_rows.append(att)
            yield
        atts.append(jnp.concatenate(att_rows, axis=0).astype(bf16))

    sts = [st_ref[h] for h in hs]
    os_ = [_dot(atts[h], vbs[h]) + _dot_nt((qs[h] * jnp.exp(bs[h])).astype(bf16), sts[h].astype(bf16))
           for h in hs]
    yield
    for h in hs:
        b_last = bs[h][c - 1:c]
        kd = (ks[h] * jnp.exp(b_last - bs[h])).astype(bf16)
        st_ref[h] = sts[h] * jnp.exp(b_last) + _dot_tn(vbs[h], kd)
        yield
    for h in hs:
        sv = slice(h * GLA_DV, (h + 1) * GLA_DV)
        o_ref[:, sv] = (_rms(os_[h], nw_ref[...]) * _silu(r_ref[:, sv].astype(f32))).astype(bf16)
        yield

    @pl.when(ti == pl.num_programs(1) - 1)
    def _():
        for h in hs:
            so_ref[h] = st_ref[h].T


def _ssd_kernel(z_ref, xbc_ref, sm_ref, cst_ref, cw_ref, cb_ref, avec_ref, dtb_ref, dvec_ref, nw_ref, h0_ref,
                o_ref, h_ref, ext_ref, nat_ref):
    c = xbc_ref.shape[0]
    assert c == SSD_P, "head pairs share a lane tile: the chunk must be as wide as a head"
    first = pl.program_id(1) == 0

    @pl.when(first)
    def _():
        h_ref[...] = h0_ref[...]

    _conv_load(ext_ref, xbc_ref, cst_ref, first)

    dt_l = jax.nn.softplus(_load_residue(sm_ref, c) + dtb_ref[...])
    tri, _, _ = _tri_masks_residue(c)
    ac_all = _cumsum_rows(dt_l * avec_ref[...], tri)
    ac_t2 = jnp.concatenate([ac_all, ac_all], axis=0).T

    lane = lax.broadcasted_iota(jnp.int32, (c, 2 * SSD_P), 1)
    lo_half = lane < SSD_P
    tri2, _, _ = _tri_masks_residue(c, 2 * SSD_P)
    lane1 = lane[0:1]
    gs_ = range(SSD_G)
    ps_ = range(SSD_HG // 2)

    xs_, bmb, hg, cb2, ch = [], [], [], [], []
    for g in gs_:
        xs_.append(_conv_cols(ext_ref, cw_ref, cb_ref, g * SSD_GW, SSD_GW, c))
        bmb.append(_conv_cols(ext_ref, cw_ref, cb_ref, SSD_INNER + g * SSD_N, SSD_N, c).astype(bf16))
        cmb = _conv_cols(ext_ref, cw_ref, cb_ref, SSD_INNER + SSD_BC + g * SSD_N, SSD_N, c).astype(bf16)
        hg.append(h_ref[pl.ds(g * SSD_HG, SSD_HG)].reshape(SSD_GW, SSD_N))
        cb2.append(_dot_nt(cmb, jnp.concatenate([bmb[g], bmb[g]], axis=0)))
        ch.append(_dot_nt(cmb, hg[g].astype(bf16)))
        yield

    m2, rhs, e_col, xsc = {}, {}, {}, {}
    for g in gs_:
        for p in ps_:
            l0 = SM_DT + g * SSD_HG + 2 * p
            ps = slice(p * 2 * SSD_P, (p + 1) * 2 * SSD_P)
            ac_col = jnp.where(lo_half, ac_all[:, l0:l0 + 1], ac_all[:, l0 + 1:l0 + 2])
            ac_row = jnp.where(lane1 < SSD_P, ac_t2[l0:l0 + 1, :], ac_t2[l0 + 1:l0 + 2, :])
            decay = jnp.exp(jnp.where(tri2, ac_col - ac_row, -jnp.inf))
            m2[g, p] = (cb2[g] * decay).astype(bf16)
            dt2 = jnp.where(lo_half, dt_l[:, l0:l0 + 1], dt_l[:, l0 + 1:l0 + 2])
            xdt = xs_[g][:, ps] * dt2
            rhs[g, p] = jnp.concatenate([jnp.where(lo_half, xdt, 0.0), jnp.where(lo_half, 0.0, xdt)],
                                        axis=0).astype(bf16)
            e_col[g, p] = jnp.exp(ac_col)
            xsc[g, p] = xdt * jnp.exp(ac_col[c - 1:c] - ac_col)
            yield
    y2 = {gp: _dot(m2[gp], rhs[gp]) for gp in m2}
    yield

    for g in gs_:
        xsg = jnp.concatenate([xsc[g, p] for p in ps_], axis=1).astype(bf16)
        dh = _dot_tn(xsg, bmb[g])
        for j in range(SSD_HG):
            ln = SM_DT + g * SSD_HG + j
            h_ref[g * SSD_HG + j] = (hg[g][j * SSD_P:(j + 1) * SSD_P] * jnp.exp(ac_all[c - 1:c, ln:ln + 1])
                                     + dh[j * SSD_P:(j + 1) * SSD_P])
        yield
    for g in gs_:
        gs = slice(g * SSD_GW, (g + 1) * SSD_GW)
        y = jnp.concatenate([y2[g, p] + ch[g][:, p * 2 * SSD_P:(p + 1) * 2 * SSD_P] * e_col[g, p] for p in ps_],
                            axis=1)
        _store_natural(nat_ref, y + dvec_ref[:, gs] * xs_[g], c, g * SSD_GW)
        y = _read_natural(nat_ref, g * SSD_GW, SSD_GW) * _silu(z_ref[:, gs].astype(f32))
        o_ref[:, gs] = _rms(y, nw_ref[:, gs]).astype(bf16)
        yield

    _conv_carry(ext_ref, c)


STEP_B = 8


def _conv_step(x_ref, c_ref, w_ref):
    w = w_ref[...]
    return c_ref[0] * w[0:1] + c_ref[1] * w[1:2] + c_ref[2] * w[2:3] + x_ref[...] * w[3:4]


def _expand_matrix(nb, width):
    r = lax.broadcasted_iota(jnp.int32, (nb, nb * width), 0)
    c = lax.broadcasted_iota(jnp.int32, (nb, nb * width), 1)
    return jnp.where((c >= r * width) & (c < (r + 1) * width), 1.0, 0.0).astype(bf16)


def _bcast_cols(x, e_mat):
    hi, lo = _split(x)
    return _dot_tn(hi, e_mat) + _dot_tn(lo, e_mat)


def _state_call(kernel, grid, in_specs, out_specs, out_shape, scratch, sem, name, args, prev_state):
    aliases = {}
    if prev_state is not None:
        in_specs = in_specs + [pl.BlockSpec(memory_space=pl.ANY)]
        args = args + (prev_state,)
        aliases = {len(args) - 1: len(out_shape) - 1}
        kernel = functools.partial(_drop_alias_ref, kernel, len(args) - 1)
    return pl.pallas_call(kernel, grid=grid, in_specs=in_specs, out_specs=out_specs, out_shape=out_shape,
                          scratch_shapes=scratch, input_output_aliases=aliases,
                          compiler_params=_cp(sem), name=name)(*args)


def _drop_alias_ref(kernel, idx, *refs):
    return kernel(*refs[:idx], *refs[idx + 1:])


def _gdn_step_kernel(alog_ref, dtb_ref, q_ref, k_ref, v_ref, z_ref, sm_ref, cq_ref, ck_ref, cv_ref,
                     wq_ref, wk_ref, wv_ref, nw_ref, s_ref, o_ref, so_ref):
    nb = q_ref.shape[0]
    qa = _silu(_conv_step(q_ref, cq_ref, wq_ref))
    ka = _silu(_conv_step(k_ref, ck_ref, wk_ref))
    va = _silu(_conv_step(v_ref, cv_ref, wv_ref))
    sm = sm_ref[...]
    z = z_ref[...]
    e_mat = _expand_matrix(nb, GDN_DV)
    for h in range(GDN_H):
        sl = slice(h * GDN_DK, (h + 1) * GDN_DK)
        q = qa[:, sl]
        k = ka[:, sl]
        v = va[:, sl]
        q = q * lax.rsqrt(jnp.sum(q * q, axis=-1, keepdims=True) + EPS) * (GDN_DK ** -0.5)
        k = k * lax.rsqrt(jnp.sum(k * k, axis=-1, keepdims=True) + EPS)
        beta = _sigmoid(sm[:, SM_BETA + h:SM_BETA + h + 1])
        g = -jnp.exp(_scalar_vec(alog_ref[h])) * jax.nn.softplus(sm[:, SM_DEC + h:SM_DEC + h + 1] + dtb_ref[h])
        eg = jnp.exp(g)
        qk = jnp.sum(q * k, axis=-1, keepdims=True)
        kb = _bcast_cols(k, e_mat)
        qb = _bcast_cols(q, e_mat)
        blk = lambda m, b: m[:, b * GDN_DV:(b + 1) * GDN_DV]
        ks = jnp.concatenate([jnp.sum(s_ref[b, h] * blk(kb, b), axis=0, keepdims=True) for b in range(nb)], axis=0)
        qs = jnp.concatenate([jnp.sum(s_ref[b, h] * blk(qb, b), axis=0, keepdims=True) for b in range(nb)], axis=0)
        w = beta * v - (beta * eg) * ks
        o = qs * eg + qk * w
        for b in range(nb):
            so_ref[b, h] = s_ref[b, h] * eg[b:b + 1] + blk(kb, b) * w[b:b + 1]
        o = _rms(o, nw_ref[...]) * _silu(z[:, sl])
        o_ref[:, sl] = o.astype(bf16)


def _gdn_step(proj, small, conv_t, states, prev, l, conv_w, a_log, dt_bias, norm_w):
    n = proj.shape[0]
    nb = STEP_B
    blk = lambda off: pl.BlockSpec((nb, GDN_QK), lambda i: (i, _blk(off, GDN_QK)))
    cst = lambda j: pl.BlockSpec((CONV_W - 1, nb, GDN_QK), lambda i: (0, i, j))
    cw = lambda j: pl.BlockSpec((CONV_W, GDN_QK), lambda i: (0, j))
    smem = pl.BlockSpec(memory_space=pltpu.SMEM)
    st = pl.BlockSpec((None, nb, GDN_H, GDN_DK, GDN_DV), lambda i: (l, i, 0, 0, 0))
    return _state_call(
        _gdn_step_kernel, (n // nb,),
        [smem, smem, blk(P_QKV_A), blk(P_QKV_A + GDN_QK), blk(P_QKV_A + 2 * GDN_QK), blk(P_Z_A),
         pl.BlockSpec((nb, LANE), lambda i: (i, 0)),
         cst(0), cst(1), cst(2), cw(0), cw(1), cw(2),
         pl.BlockSpec((1, GDN_DV), lambda i: (0, 0)), st],
        [pl.BlockSpec((nb, GDN_V), lambda i: (i, 0)), st],
        [jax.ShapeDtypeStruct((n, GDN_V), bf16), jax.ShapeDtypeStruct(states.shape, f32)],
        [], ("parallel",), "gdn_step",
        (a_log, dt_bias, proj, proj, proj, proj, small, conv_t, conv_t, conv_t, conv_w, conv_w, conv_w, norm_w, states),
        prev)


def _gla_step_kernel(q_ref, k_ref, v_ref, r_ref, sm_ref, wgate_ref, bgate_ref, nw_ref, s_ref,
                     o_ref, so_ref):
    nb = q_ref.shape[0]
    sm = sm_ref[...]
    lane = lax.broadcasted_iota(jnp.int32, sm.shape, 1)
    lr = jnp.where((lane >= SM_LR) & (lane < SM_LR + GLA_RANK), sm, 0.0).astype(bf16)
    la_all = jax.nn.log_sigmoid(_dot(lr, wgate_ref[...]) + bgate_ref[...]) / GLA_TAU
    qa = q_ref[...] * (GLA_DK ** -0.5)
    ka = k_ref[...]
    va = v_ref[...]
    r = r_ref[...]
    for h in range(GLA_H):
        sk = slice(h * GLA_DK, (h + 1) * GLA_DK)
        sv = slice(h * GLA_DV, (h + 1) * GLA_DV)
        q = qa[:, sk]
        k = ka[:, sk]
        v = va[:, sv]
        e = jnp.exp(la_all[:, sk])
        qk = jnp.sum(q * k, axis=-1, keepdims=True)
        e_t = e.T
        k_t = k.T
        qe_t = (q * e).T
        rows = []
        for b in range(nb):
            s = s_ref[b, h]
            vrow = v[b:b + 1]
            rows.append(qk[b:b + 1] * vrow + jnp.sum(s * qe_t[:, b:b + 1], axis=0, keepdims=True))
            so_ref[b, h] = s * e_t[:, b:b + 1] + k_t[:, b:b + 1] * vrow
        o = jnp.concatenate(rows, axis=0)
        o = _rms(o, nw_ref[...]) * _silu(r[:, sv])
        o_ref[:, sv] = o.astype(bf16)


def _gla_step(proj, small, states, prev, l, wgate_pad, bgate, norm_w):
    n = proj.shape[0]
    nb = STEP_B
    st = pl.BlockSpec((None, nb, GLA_H, GLA_DK, GLA_DV), lambda i: (l, i, 0, 0, 0))
    return _state_call(
        _gla_step_kernel, (n // nb,),
        [pl.BlockSpec((nb, GLA_QK), lambda i: (i, _blk(P_Q_B, GLA_QK))),
         pl.BlockSpec((nb, GLA_QK), lambda i: (i, _blk(P_K_B, GLA_QK))),
         pl.BlockSpec((nb, GLA_V), lambda i: (i, _blk(P_V_B, GLA_V))),
         pl.BlockSpec((nb, GLA_V), lambda i: (i, _blk(P_R_B, GLA_V))),
         pl.BlockSpec((nb, LANE), lambda i: (i, 0)),
         pl.BlockSpec((LANE, GLA_QK), lambda i: (0, 0)),
         pl.BlockSpec((1, GLA_QK), lambda i: (0, 0)),
         pl.BlockSpec((1, GLA_DV), lambda i: (0, 0)), st],
        [pl.BlockSpec((nb, GLA_V), lambda i: (i, 0)), st],
        [jax.ShapeDtypeStruct((n, GLA_V), bf16), jax.ShapeDtypeStruct(states.shape, f32)],
        [], ("parallel",), "gla_step",
        (proj, proj, proj, proj, small, wgate_pad, bgate, norm_w, states),
        prev)


def _ssd_step_kernel(alog_ref, dtb_ref, z_ref, xbc_ref, sm_ref, cst_ref, cw_ref, cb_ref, dvec_ref, nw_ref, h_ref,
                     o_ref, ho_ref):
    nb = xbc_ref.shape[0]
    xbc = _silu(_conv_step(xbc_ref, cst_ref, cw_ref) + cb_ref[...])
    sm_t = sm_ref[...].T
    e_mat = _expand_matrix(nb, SSD_N)
    lane_b = lax.broadcasted_iota(jnp.int32, (SSD_P, nb), 1)
    for g in range(SSD_G):
        gs = slice(g * SSD_GW, (g + 1) * SSD_GW)
        bm = xbc[:, SSD_INNER + g * SSD_N:SSD_INNER + (g + 1) * SSD_N]
        cm_t = xbc[:, SSD_INNER + SSD_BC + g * SSD_N:SSD_INNER + SSD_BC + (g + 1) * SSD_N].T
        cb_row = jnp.sum(cm_t * bm.T, axis=0, keepdims=True)
        hg = h_ref[:, pl.ds(g * SSD_HG, SSD_HG)].reshape(nb * SSD_GW, SSD_N)
        ch = _dot(hg.astype(bf16), cm_t.astype(bf16))
        pairs = []
        for jp in range(SSD_HG // 2):
            x_t = xbc[:, g * SSD_GW + jp * LANE:g * SSD_GW + (jp + 1) * LANE].T
            halves = []
            for jj in range(2):
                j = 2 * jp + jj
                hd = g * SSD_HG + j
                dt_row = jax.nn.softplus(sm_t[SM_DT + hd:SM_DT + hd + 1, :] + dtb_ref[hd])
                ea_row = jnp.exp(dt_row * (-jnp.exp(_scalar_vec(alog_ref[hd]))))
                xdt_t = x_t[jj * SSD_P:(jj + 1) * SSD_P] * dt_row
                hi, lo = _split(xdt_t)
                xb = _dot(hi, e_mat) + _dot(lo, e_mat)
                yh = jnp.zeros((SSD_P, nb), f32)
                for b in range(nb):
                    r0 = (b * SSD_HG + j) * SSD_P
                    yh = jnp.where(lane_b == b, ch[r0:r0 + SSD_P], yh)
                    ho_ref[b, hd] = (h_ref[b, hd] * ea_row[:, b:b + 1]
                                     + xb[:, b * SSD_N:(b + 1) * SSD_N] * bm[b:b + 1])
                halves.append(yh * ea_row + cb_row * xdt_t)
            pairs.append(jnp.concatenate(halves, axis=0).T)
        y = jnp.concatenate(pairs, axis=1)
        y = (y + dvec_ref[:, gs] * xbc[:, gs]) * _silu(z_ref[:, gs])
        o_ref[:, gs] = _rms(y, nw_ref[:, gs]).astype(bf16)


def _ssd_step(proj, small, conv_t, states, prev, l, conv_w, conv_b, a_log, dt_bias, dvec, norm_w):
    n = proj.shape[0]
    nb = STEP_B
    smem = pl.BlockSpec(memory_space=pltpu.SMEM)
    const = lambda shape: pl.BlockSpec(shape, lambda i: (0,) * len(shape))
    st = pl.BlockSpec((None, nb, SSD_H, SSD_P, SSD_N), lambda i: (l, i, 0, 0, 0))
    return _state_call(
        _ssd_step_kernel, (n // nb,),
        [smem, smem,
         pl.BlockSpec((nb, SSD_INNER), lambda i: (i, _blk(P_Z_C, SSD_INNER))),
         pl.BlockSpec((nb, SSD_CONV), lambda i: (i, _blk(P_XBC, SSD_CONV))),
         pl.BlockSpec((nb, LANE), lambda i: (i, 0)),
         pl.BlockSpec((CONV_W - 1, nb, SSD_CONV), lambda i: (0, i, 0)),
         const((CONV_W, SSD_CONV)), const((1, SSD_CONV)), const((1, SSD_INNER)), const((1, SSD_INNER)), st],
        [pl.BlockSpec((nb, SSD_INNER), lambda i: (i, 0)), st],
        [jax.ShapeDtypeStruct((n, SSD_INNER), bf16), jax.ShapeDtypeStruct(states.shape, f32)],
        [], ("parallel",), "ssd_step",
        (a_log, dt_bias, proj, proj, small, conv_t, conv_w, conv_b, dvec, norm_w, states),
        prev)


def _merge_kernel(oa_ref, ob_ref, oc_ref, ga_ref, gb_ref, gc_ref, wa_ref, wb_ref, wc_ref, o_ref):
    m = (_sigmoid(ga_ref[...].astype(f32)) * _dot(oa_ref[...], wa_ref[...])
         + _sigmoid(gb_ref[...].astype(f32)) * _dot(ob_ref[...], wb_ref[...])
         + _sigmoid(gc_ref[...].astype(f32)) * _dot(oc_ref[...], wc_ref[...]))
    o_ref[...] = m.astype(bf16)


def _outproj_kernel(m_ref, x_ref, gt_ref, w_ref, o_ref):
    o_ref[...] = x_ref[...] + gt_ref[...] * _dot(m_ref[...], w_ref[...])


def _mixout(oa, ob, oc, proj, x, mod, wa, wb, wc, wo, l, per_row, tm):
    bx, tx, dm = x.shape
    tn = MERGE_COLS
    gate = lambda k: pl.BlockSpec((None, tm, tn), lambda b, i, j: (b, i, _blk(P_GATES + k * D_MODEL, tn) + j))
    act = lambda w: pl.BlockSpec((None, tm, w), lambda b, i, j: (b, i, 0))
    wsp = lambda w: pl.BlockSpec((None, w, tn), lambda b, i, j: (l, 0, j))
    merged = pl.pallas_call(
        _merge_kernel,
        grid=(bx, tx // tm, dm // tn),
        in_specs=[act(GDN_V), act(GLA_V), act(SSD_INNER), gate(0), gate(1), gate(2),
                  wsp(GDN_V), wsp(GLA_V), wsp(SSD_INNER)],
        out_specs=pl.BlockSpec((None, tm, tn), lambda b, i, j: (b, i, j)),
        out_shape=jax.ShapeDtypeStruct((bx, tx, dm), bf16),
        compiler_params=_cp(("parallel", "parallel", "arbitrary")),
        name="merge",
    )(oa, ob, oc, proj, proj, proj, wa, wb, wc)
    to = OUTPROJ_COLS
    nj = dm // to
    if per_row:
        gspec = pl.BlockSpec((None, tm, to), lambda b, i, j: (b, i, 5 * nj + j))
    else:
        gspec = pl.BlockSpec((None, 1, to), lambda b, i, j: (b, 0, 5 * nj + j))
    return pl.pallas_call(
        _outproj_kernel,
        grid=(bx, tx // tm, nj),
        in_specs=[pl.BlockSpec((None, tm, dm), lambda b, i, j: (b, i, 0)),
                  pl.BlockSpec((None, tm, to), lambda b, i, j: (b, i, j)),
                  gspec,
                  pl.BlockSpec((None, dm, to), lambda b, i, j: (l, 0, j))],
        out_specs=pl.BlockSpec((None, tm, to), lambda b, i, j: (b, i, j)),
        out_shape=jax.ShapeDtypeStruct(x.shape, f32),
        compiler_params=_cp(("parallel", "parallel", "arbitrary")),
        name="outproj",
    )(merged, x, mod, wo)


REGROUP_ROWS = 512


def _regroup_kernel(w_ref, o_ref):
    o_ref[...] = w_ref[0].astype(bf16)


def _regroup_rows(wt):
    nl, _, d = wt.shape
    tr = REGROUP_ROWS
    shifts = []
    for name in MAIN_ORDER:
        a, b = W_IN_SRC[name]
        assert (b - a) % tr == 0 and a % SUB == 0
        shifts.append((P_MAIN[name] // tr, a - P_MAIN[name]))

    def src_row(i):
        off = i * (tr // SUB) + shifts[0][1] // SUB
        for k in range(1, len(shifts)):
            off = off + jnp.where(i >= shifts[k][0], (shifts[k][1] - shifts[k - 1][1]) // SUB, 0)
        return off * SUB

    return pl.pallas_call(
        _regroup_kernel,
        grid=(nl, P_TOTAL // tr),
        in_specs=[pl.BlockSpec((pl.Element(1), pl.Element(tr), pl.Element(d)), lambda l, i: (l, src_row(i), 0))],
        out_specs=pl.BlockSpec((None, tr, d), lambda l, i: (l, i, 0)),
        out_shape=jax.ShapeDtypeStruct((nl, P_TOTAL, d), bf16),
        compiler_params=_cp(("parallel", "parallel")),
        name="regroup_w_in",
    )(wt)


def _permute_w_in(w):
    nl, d, _ = w.shape
    main = _regroup_rows(jnp.swapaxes(w, 1, 2))
    small = jnp.concatenate([w[:, :, W_IN_SRC[name][0]:W_IN_SRC[name][1]] for name in SMALL_ORDER], axis=2)
    small = jnp.pad(jnp.swapaxes(small, 1, 2), ((0, 0), (0, LANE - small.shape[2]), (0, 0)))
    return main, small


def _lane_vec(v, lo):
    return jnp.zeros((1, LANE), f32).at[0, lo:lo + v.shape[0]].set(v)


def _layer_params(l, p):
    row = lambda a: a[l].reshape(1, -1)
    wgate = jnp.zeros((LANE, GLA_QK), f32).at[SM_LR:SM_LR + GLA_RANK].set(p["gla_w_gate"][l]).astype(bf16)
    return dict(
        norm1=row(p["norm1"]), norm2=row(p["norm2"]), norm3=row(p["norm3"]),
        gdn_conv_w=p["gdn_conv_w"][l], gdn_a_log=p["gdn_a_log"][l], gdn_dt_bias=p["gdn_dt_bias"][l],
        gdn_alog_l=_lane_vec(p["gdn_a_log"][l], SM_DEC), gdn_dtb_l=_lane_vec(p["gdn_dt_bias"][l], SM_DEC),
        gdn_norm_w=row(p["gdn_norm_w"]),
        gla_wgate=wgate, gla_bgate=row(p["gla_b_gate"]), gla_norm_w=row(p["gla_norm_w"]),
        ssd_conv_w=p["ssd_conv_w"][l], ssd_conv_b=row(p["ssd_conv_b"]), ssd_a_log=p["ssd_a_log"][l],
        ssd_dt_bias=p["ssd_dt_bias"][l],
        ssd_avec_l=_lane_vec(-jnp.exp(p["ssd_a_log"][l]), SM_DT), ssd_dtb_l=_lane_vec(p["ssd_dt_bias"][l], SM_DT),
        ssd_dvec=jnp.repeat(p["ssd_d"][l], SSD_P).reshape(1, -1),
        ssd_norm_w=row(p["ssd_norm_w"]),
    )


def _stacked_weights(p):
    w_in, w_in_small = _permute_w_in(p["w_in"])
    return dict(
        f1=(p["ffn1_wg"], p["ffn1_wu"], p["ffn1_wd"]),
        f2=(p["ffn2_wg"], p["ffn2_wu"], p["ffn2_wd"]),
        w_in=w_in, w_in_small=w_in_small,
        wa=p["w_branch_gdn"].astype(bf16), wb=p["w_branch_gla"].astype(bf16),
        wc=p["w_branch_ssd"].astype(bf16), w_out=p["w_out"].astype(bf16),
    )


def _new_conv_state(buf, raw):
    t = raw.shape[1]
    k = CONV_W - 1
    if t >= k:
        return raw[:, t - k:]
    return jnp.concatenate([buf[:, t:], raw], axis=1)


def _mixer_prompt(proj, small, lp, st):
    gdn_conv, s_gdn, s_gla, ssd_conv, s_ssd = st
    b, t, _ = proj.shape
    c = CHUNK
    sm_spec = pl.BlockSpec((None, c, LANE), lambda bi, ti: (bi, ti, 0))
    st_a, st_b, st_c = _per_seq(GDN_H, GDN_DK, GDN_DV), _per_seq(GLA_H, GLA_DK, GLA_DV), _per_seq(SSD_H, SSD_P, SSD_N)
    gdn_specs = [_tok(GDN_CONV, P_QKV_A), _tok(GDN_V, P_Z_A), sm_spec, _per_seq(CONV_W - 1, GDN_CONV),
                 _const(CONV_W, GDN_CONV), _const(1, LANE), _const(1, LANE), _const(1, GDN_DV), st_a]
    gdn_args = (proj, proj, small, gdn_conv, lp["gdn_conv_w"], lp["gdn_alog_l"], lp["gdn_dtb_l"],
                lp["gdn_norm_w"], s_gdn)
    gla_specs = [_tok(GLA_QK, P_Q_B), _tok(GLA_QK, P_K_B), _tok(GLA_V, P_V_B), _tok(GLA_V, P_R_B), sm_spec,
                 _const(LANE, GLA_QK), _const(1, GLA_QK), _const(1, GLA_DV), st_b]
    gla_args = (proj, proj, proj, proj, small, lp["gla_wgate"], lp["gla_bgate"], lp["gla_norm_w"], s_gla)
    ssd_specs = [_tok(SSD_INNER, P_Z_C), _tok(SSD_CONV, P_XBC), sm_spec, _per_seq(CONV_W - 1, SSD_CONV),
                 _const(CONV_W, SSD_CONV), _const(1, SSD_CONV), _const(1, LANE), _const(1, LANE),
                 _const(1, SSD_INNER), _const(1, SSD_INNER), st_c]
    ssd_args = (proj, proj, small, ssd_conv, lp["ssd_conv_w"], lp["ssd_conv_b"], lp["ssd_avec_l"],
                lp["ssd_dtb_l"], lp["ssd_dvec"], lp["ssd_norm_w"], s_ssd)
    assert (len(gdn_specs), len(gla_specs), len(ssd_specs)) == (N_GDN_IN, N_GLA_IN, N_SSD_IN)
    out_tok = lambda width: pl.BlockSpec((None, c, width), lambda bi, ti: (bi, ti, 0))
    oa, s_gdn_n, ob, s_gla_n, oc, s_ssd_n = pl.pallas_call(
        _mixers_kernel,
        grid=(b, t // c),
        in_specs=gdn_specs + gla_specs + ssd_specs,
        out_specs=[out_tok(GDN_V), st_a, out_tok(GLA_V), st_b, out_tok(SSD_INNER), st_c],
        out_shape=[jax.ShapeDtypeStruct((b, t, GDN_V), bf16), jax.ShapeDtypeStruct(s_gdn.shape, f32),
                   jax.ShapeDtypeStruct((b, t, GLA_V), bf16), jax.ShapeDtypeStruct(s_gla.shape, f32),
                   jax.ShapeDtypeStruct((b, t, SSD_INNER), bf16), jax.ShapeDtypeStruct(s_ssd.shape, f32)],
        scratch_shapes=[pltpu.VMEM((GDN_CONV // LANE, 8 + c, LANE), f32), pltpu.VMEM((GLA_H, GLA_DV, GLA_DK), f32),
                        pltpu.VMEM((SSD_CONV // LANE, 8 + c, LANE), f32),
                        pltpu.VMEM((GDN_V // LANE, c, LANE), f32), pltpu.VMEM((SSD_INNER // LANE, c, LANE), f32)],
        compiler_params=_cp(("parallel", "arbitrary")),
        name="mixers_prompt",
    )(*gdn_args, *gla_args, *ssd_args)
    gdn_conv_n = _new_conv_state(gdn_conv, proj[:, :, P_QKV_A:P_QKV_A + GDN_CONV]).astype(f32)
    ssd_conv_n = _new_conv_state(ssd_conv, proj[:, :, P_XBC:P_XBC + SSD_CONV]).astype(f32)
    return (oa, ob, oc), (gdn_conv_n, s_gdn_n, s_gla_n, ssd_conv_n, s_ssd_n)


def _mixer_sample(proj, small, lp, l, states, prev):
    n = proj.shape[1]
    p2 = proj.reshape(n, P_TOTAL)
    s2 = small.reshape(n, LANE)
    gdn_conv, ssd_conv = states[0][l], states[3][l]
    gct = jnp.swapaxes(gdn_conv, 0, 1)
    sct = jnp.swapaxes(ssd_conv, 0, 1)
    pv = (None,) * 5 if prev is None else prev
    oa, s_gdn_n = _gdn_step(p2, s2, gct, states[1], pv[1], l, lp["gdn_conv_w"], lp["gdn_a_log"],
                            lp["gdn_dt_bias"], lp["gdn_norm_w"])
    ob, s_gla_n = _gla_step(p2, s2, states[2], pv[2], l, lp["gla_wgate"], lp["gla_bgate"], lp["gla_norm_w"])
    oc, s_ssd_n = _ssd_step(p2, s2, sct, states[4], pv[4], l, lp["ssd_conv_w"], lp["ssd_conv_b"], lp["ssd_a_log"],
                            lp["ssd_dt_bias"], lp["ssd_dvec"], lp["ssd_norm_w"])
    raw = p2.reshape(n, 1, P_TOTAL)
    gdn_conv_n = _new_conv_state(gdn_conv, raw[:, :, P_QKV_A:P_QKV_A + GDN_CONV])
    ssd_conv_n = _new_conv_state(ssd_conv, raw[:, :, P_XBC:P_XBC + SSD_CONV])
    outs = tuple(o.reshape(1, n, -1) for o in (oa, ob, oc))
    return outs, (gdn_conv_n, s_gdn_n, s_gla_n, ssd_conv_n, s_ssd_n)


def _trunk(x, mods, lps, sw, ffn_w, states, per_row, tm, tm_in, final_w):
    nl = len(lps)
    per_layer = []
    prev = None

    def ffn(x, mod, k0, nw, name, l, fw, final):
        if per_row:
            x, ffn_w[l, name] = _ffn_cast(x, mod, k0, nw, *sw[name], l, fw, final)
            return x
        return _ffn(x, mod, k0, nw, ffn_w[l, name], fw, per_row, tm, final)

    for l in range(nl):
        lp, mod = lps[l], mods[l]
        last = l == nl - 1
        x = ffn(x, mod, 0, lp["norm1"], "f1", l, lp["norm1"], False)
        proj, small = _inproj(x, mod, lp["norm2"], sw["w_in"], sw["w_in_small"], l, per_row, tm_in,
                              f32 if per_row else bf16)
        if per_row:
            (oa, ob, oc), st = _mixer_sample(proj, small, lp, l, states, prev)
            prev = st
        else:
            (oa, ob, oc), st = _mixer_prompt(proj, small, lp, tuple(s[l] for s in states))
        per_layer.append(st)
        x = _mixout(oa, ob, oc, proj, x, mod, sw["wa"], sw["wb"], sw["wc"], sw["w_out"], l, per_row, tm_in)
        x = ffn(x, mod, 6, lp["norm3"], "f2", l, final_w if last else lp["norm3"], last)
    stack = lambda i: jnp.stack([st[i] for st in per_layer])
    if per_row:
        new_states = (stack(0), prev[1], prev[2], stack(3), prev[4])
    else:
        new_states = tuple(stack(i) for i in range(5))
    return x, new_states


def kernel(x_prompt, x_sample, state_gdn_conv, state_gdn, state_gla, state_ssd_conv, state_ssd, c_prompt, c_sample, w_ada, b_ada, norm1, norm2, norm3, ffn1_wg, ffn1_wu, ffn1_wd, ffn2_wg, ffn2_wu, ffn2_wd, w_in, gdn_conv_w, gdn_a_log, gdn_dt_bias, gdn_norm_w, gla_w_gate, gla_b_gate, gla_norm_w, ssd_conv_w, ssd_conv_b, ssd_a_log, ssd_dt_bias, ssd_d, ssd_norm_w, w_branch_gdn, w_branch_gla, w_branch_ssd, w_out, final_norm):
    p = dict(norm1=norm1, norm2=norm2, norm3=norm3,
             ffn1_wg=ffn1_wg, ffn1_wu=ffn1_wu, ffn1_wd=ffn1_wd, ffn2_wg=ffn2_wg, ffn2_wu=ffn2_wu, ffn2_wd=ffn2_wd,
             w_in=w_in, gdn_conv_w=gdn_conv_w, gdn_a_log=gdn_a_log, gdn_dt_bias=gdn_dt_bias, gdn_norm_w=gdn_norm_w,
             gla_w_gate=gla_w_gate, gla_b_gate=gla_b_gate, gla_norm_w=gla_norm_w,
             ssd_conv_w=ssd_conv_w, ssd_conv_b=ssd_conv_b, ssd_a_log=ssd_a_log, ssd_dt_bias=ssd_dt_bias,
             ssd_d=ssd_d, ssd_norm_w=ssd_norm_w,
             w_branch_gdn=w_branch_gdn, w_branch_gla=w_branch_gla, w_branch_ssd=w_branch_ssd, w_out=w_out)
    nl = w_ada.shape[0]
    bp, tp, dm = x_prompt.shape
    bs = x_sample.shape[0]
    assert x_sample.shape[1] == 1 and tp % CHUNK == 0 and bs % STEP_B == 0 and dm == D_MODEL
    lps = [_layer_params(l, p) for l in range(nl)]
    sw = _stacked_weights(p)
    fw = final_norm.reshape(1, dm)

    rows = bp + bs
    rpad = -(-rows // 8) * 8
    c_all = jnp.concatenate([c_prompt, c_sample, jnp.zeros((rpad - rows, dm), f32)], axis=0)
    mod = _ada_mod(c_all, w_ada, b_ada)
    mod_p = [mod[l, :bp].reshape(bp, 1, N_MOD * dm) for l in range(nl)]
    mod_s = [mod[l, bp:rows].reshape(1, bs, N_MOD * dm) for l in range(nl)]

    sample_states = (state_gdn_conv, state_gdn, state_gla, state_ssd_conv, state_ssd)
    prompt_states = tuple(jnp.zeros((s.shape[0], bp) + s.shape[2:], x_prompt.dtype) for s in sample_states)
    tm_p = ROW_TILE if tp % ROW_TILE == 0 else CHUNK
    tm_in = ROW_TILE_WIDE if tp % ROW_TILE_WIDE == 0 else tm_p
    ffn_w = {}
    y_s, st_s = _trunk(x_sample.reshape(1, bs, dm), mod_s, lps, sw, ffn_w, sample_states, True, bs, bs, fw)
    y_p, st_p = _trunk(x_prompt, mod_p, lps, sw, ffn_w, prompt_states, False, tm_p, tm_in, fw)
    return (y_p, y_s.reshape(bs, 1, dm)) + st_p + st_s
```

```python
import functools

import jax
import jax.numpy as jnp
from jax import lax
from jax.experimental import pallas as pl
from jax.experimental.pallas import tpu as pltpu

f32 = jnp.float32
bf16 = jnp.bfloat16
HI = lax.Precision.HIGHEST

EPS = 1e-6
D_MODEL = 2048
N_MOD = 9
CHUNK = 64
CONV_W = 4
GDN_H, GDN_DK, GDN_DV = 8, 128, 128
GLA_H, GLA_DK, GLA_DV, GLA_RANK, GLA_TAU = 4, 128, 256, 16, 16.0
SSD_H, SSD_P, SSD_G, SSD_N = 32, 64, 4, 128
SSD_HG = SSD_H // SSD_G
GDN_QK = GDN_H * GDN_DK
GDN_V = GDN_H * GDN_DV
GDN_CONV = 2 * GDN_QK + GDN_V
GLA_QK = GLA_H * GLA_DK
GLA_V = GLA_H * GLA_DV
SSD_INNER = SSD_H * SSD_P
SSD_BC = SSD_G * SSD_N
SSD_CONV = SSD_INNER + 2 * SSD_BC
SSD_GW = SSD_HG * SSD_P

LANE = 128
MXU_WIDTH = 256

ROW_TILE = 512
ROW_TILE_WIDE = 1024
FF_TILE = 512
FF_SUB = MXU_WIDTH
ADA_COLS = 1024
INPROJ_COLS = 512
MERGE_COLS = 512
OUTPROJ_COLS = 1024

IN_SPLITS = (("qkv_a", GDN_CONV), ("z_a", GDN_V), ("beta", GDN_H), ("dec", GDN_H),
             ("q_b", GLA_QK), ("k_b", GLA_QK), ("v_b", GLA_V), ("lr", GLA_RANK), ("r_b", GLA_V),
             ("z_c", SSD_INNER), ("xbc", SSD_CONV), ("dt", SSD_H), ("gates", 3 * D_MODEL))
MAIN_ORDER = ("qkv_a", "xbc", "z_c", "gates", "z_a", "q_b", "k_b", "v_b", "r_b")
SMALL_ORDER = ("beta", "dec", "lr", "dt")


def _layout():
    src, off = {}, 0
    for name, w in IN_SPLITS:
        src[name] = (off, off + w)
        off += w
    main, small, d = {}, {}, 0
    for name in MAIN_ORDER:
        w = src[name][1] - src[name][0]
        main[name] = d
        d += w
    total = d
    d = 0
    for name in SMALL_ORDER:
        small[name] = d
        d += src[name][1] - src[name][0]
    assert d <= LANE
    return src, main, small, total


W_IN_SRC, P_MAIN, P_SM, P_TOTAL = _layout()
P_QKV_A, P_XBC, P_Z_C, P_GATES, P_Z_A = (P_MAIN[k] for k in ("qkv_a", "xbc", "z_c", "gates", "z_a"))
P_Q_B, P_K_B, P_V_B, P_R_B = (P_MAIN[k] for k in ("q_b", "k_b", "v_b", "r_b"))
SM_BETA, SM_DEC, SM_LR, SM_DT = (P_SM[k] for k in SMALL_ORDER)

VMEM_LIMIT = 56 * 1024 * 1024


def _cp(sem):
    return pltpu.CompilerParams(dimension_semantics=sem, vmem_limit_bytes=VMEM_LIMIT)


def _blk(off, width):
    assert off % width == 0, (off, width)
    return off // width


def _sigmoid(x):
    return 0.5 + 0.5 * jnp.tanh(0.5 * x)


def _silu(x):
    h = 0.5 * x
    return h + h * jnp.tanh(h)


def _rms(x, w):
    return x * lax.rsqrt(jnp.mean(x * x, axis=-1, keepdims=True) + EPS) * w


def _dot(a, b):
    return jnp.dot(a, b, preferred_element_type=f32)


def _dot_nt(a, b):
    return lax.dot_general(a, b, (((1,), (1,)), ((), ())), preferred_element_type=f32)


def _dot_tn(a, b):
    return lax.dot_general(a, b, (((0,), (0,)), ((), ())), preferred_element_type=f32)


def _split(x):
    hi = x.astype(bf16)
    return hi, (x - hi.astype(f32)).astype(bf16)


def _dot3(a, b):
    return _dot(a[0], b[0]) + (_dot(a[0], b[1]) + _dot(a[1], b[0]))


def _dot_hi(a, b):
    return jnp.dot(a, b, precision=HI, preferred_element_type=f32)


def _tri_masks(c):
    row = lax.broadcasted_iota(jnp.int32, (c, c), 0)
    col = lax.broadcasted_iota(jnp.int32, (c, c), 1)
    return row >= col, row > col, row == col


SUB = 8


def _row_time(idx, c):
    n = c // SUB
    assert n & (n - 1) == 0, "chunk / SUB must be a power of two"
    sh = n.bit_length() - 1
    return ((idx & (n - 1)) << 3) | (idx >> sh)


def _tri_masks_residue(c, cols=None):
    cols = c if cols is None else cols
    row = _row_time(lax.broadcasted_iota(jnp.int32, (c, cols), 0), c)
    col = _row_time(lax.broadcasted_iota(jnp.int32, (c, cols), 1) & (c - 1), c)
    return row >= col, row > col, row == col


def _load_residue(ref, c):
    n = c // SUB
    return jnp.concatenate([ref[pl.ds(r, n, stride=SUB), :] for r in range(SUB)], axis=0)


def _store_natural(nat_ref, x, c, lo):
    n = c // SUB
    for j in range(x.shape[1] // LANE):
        for r in range(SUB):
            nat_ref[lo // LANE + j, pl.ds(r, n, stride=SUB), :] = x[r * n:(r + 1) * n, j * LANE:(j + 1) * LANE]


def _read_natural(nat_ref, lo, width):
    return jnp.concatenate([nat_ref[lo // LANE + j] for j in range(width // LANE)], axis=1)


def _scalar_vec(s):
    return jnp.full((1, 1), s, f32)


def _ada_kernel(c_ref, w_ref, b_ref, o_ref):
    s = _silu(c_ref[...]).astype(bf16)
    o_ref[...] = _dot(s, w_ref[...].astype(bf16)) + b_ref[...]


def _ada_mod(c_all, w_ada, b_ada):
    nl, dm, n = w_ada.shape
    r = c_all.shape[0]
    tn = ADA_COLS
    return pl.pallas_call(
        _ada_kernel,
        grid=(nl, n // tn),
        in_specs=[pl.BlockSpec((r, dm), lambda l, j: (0, 0)),
                  pl.BlockSpec((None, dm, tn), lambda l, j: (l, 0, j)),
                  pl.BlockSpec((None, 1, tn), lambda l, j: (l, 0, j))],
        out_specs=pl.BlockSpec((None, r, tn), lambda l, j: (l, 0, j)),
        out_shape=jax.ShapeDtypeStruct((nl, r, n), f32),
        compiler_params=_cp(("arbitrary", "arbitrary")),
        name="ada_mod",
    )(c_all, w_ada, b_ada.reshape(nl, 1, n))


def _mod_spec(per_row, tm, chunk):
    if per_row:
        return pl.BlockSpec((None, tm, D_MODEL), lambda b, i, j: (b, i, chunk))
    return pl.BlockSpec((None, 1, D_MODEL), lambda b, i, j: (b, 0, chunk))


def _ffn_prologue(x_ref, sh_ref, sc_ref, nw_ref, h_ref, acc_ref):
    y = _rms(x_ref[...], nw_ref[...])
    h_ref[...] = (y * (1.0 + sc_ref[...]) + sh_ref[...]).astype(bf16)
    acc_ref[...] = jnp.zeros_like(acc_ref)


def _ffn_accumulate(h_ref, acc_ref, wg, wu, wd):
    h = h_ref[...]
    width = wg.shape[1]
    sub = FF_SUB if width % FF_SUB == 0 else width
    parts = [(_dot(h, wg[:, s:s + sub]), _dot(h, wu[:, s:s + sub])) for s in range(0, width, sub)]
    acc = acc_ref[...]
    for i, (g, u) in enumerate(parts):
        acc = acc + _dot((_silu(g) * u).astype(bf16), wd[i * sub:(i + 1) * sub, :])
    acc_ref[...] = acc


def _ffn_epilogue(x_ref, gt_ref, fw_ref, acc_ref, o_ref, final):
    y = x_ref[...] + 0.5 * gt_ref[...] * acc_ref[...]
    if final:
        y = _rms(y, fw_ref[...])
    o_ref[...] = y


def _ffn_kernel(x_ref, sh_ref, sc_ref, gt_ref, nw_ref, fw_ref, wg_ref, wu_ref, wd_ref, *rest, final, has_tail):
    if has_tail:
        wgt_ref, wut_ref, wdt_ref, o_ref, h_ref, acc_ref = rest
    else:
        o_ref, h_ref, acc_ref = rest
    f = pl.program_id(2)
    last = pl.num_programs(2) - 1
    pl.when(f == 0)(lambda: _ffn_prologue(x_ref, sh_ref, sc_ref, nw_ref, h_ref, acc_ref))
    if has_tail:
        pl.when(f < last)(lambda: _ffn_accumulate(h_ref, acc_ref, wg_ref, wu_ref, wd_ref))
        pl.when(f == last)(lambda: _ffn_accumulate(h_ref, acc_ref, wgt_ref, wut_ref, wdt_ref))
    else:
        _ffn_accumulate(h_ref, acc_ref, wg_ref, wu_ref, wd_ref)
    pl.when(f == last)(lambda: _ffn_epilogue(x_ref, gt_ref, fw_ref, acc_ref, o_ref, final))


def _ffn(x, mod, k0, nw, w, fw, per_row, tm, final):
    bx, tx, dm = x.shape
    tf = FF_TILE
    n_full = w[0].shape[1] // tf
    has_tail = len(w) > 3
    full = lambda f: jnp.minimum(f, n_full - 1)
    w_specs = [pl.BlockSpec((dm, tf), lambda b, i, f: (0, full(f))),
               pl.BlockSpec((dm, tf), lambda b, i, f: (0, full(f))),
               pl.BlockSpec((tf, dm), lambda b, i, f: (full(f), 0))]
    if has_tail:
        ft = w[3].shape[1]
        w_specs += [pl.BlockSpec((dm, ft), lambda b, i, f: (0, 0)),
                    pl.BlockSpec((dm, ft), lambda b, i, f: (0, 0)),
                    pl.BlockSpec((ft, dm), lambda b, i, f: (0, 0))]
    return pl.pallas_call(
        functools.partial(_ffn_kernel, final=final, has_tail=has_tail),
        grid=(bx, tx // tm, n_full + int(has_tail)),
        in_specs=[pl.BlockSpec((None, tm, dm), lambda b, i, f: (b, i, 0)),
                  _mod_spec(per_row, tm, k0), _mod_spec(per_row, tm, k0 + 1), _mod_spec(per_row, tm, k0 + 2),
                  pl.BlockSpec((1, dm), lambda b, i, f: (0, 0)),
                  pl.BlockSpec((1, dm), lambda b, i, f: (0, 0))] + w_specs,
        out_specs=pl.BlockSpec((None, tm, dm), lambda b, i, f: (b, i, 0)),
        out_shape=jax.ShapeDtypeStruct(x.shape, f32),
        scratch_shapes=[pltpu.VMEM((tm, dm), bf16), pltpu.VMEM((tm, dm), f32)],
        compiler_params=_cp(("parallel", "parallel", "arbitrary")),
        name="ffn",
    )(x, mod, mod, mod, nw, fw, *w)


FF_CAST_TILE = 256


def _ffn_cast_kernel(x_ref, sh_ref, sc_ref, gt_ref, nw_ref, fw_ref, wg_ref, wu_ref, wd_ref, *rest, final, has_tail):
    if has_tail:
        wgt_ref, wut_ref, wdt_ref, o_ref, wgb_ref, wub_ref, wdb_ref, wgtb_ref, wutb_ref, wdtb_ref, h_ref, acc_ref = rest
    else:
        o_ref, wgb_ref, wub_ref, wdb_ref, h_ref, acc_ref = rest
    f = pl.program_id(2)
    last = pl.num_programs(2) - 1
    pl.when(f == 0)(lambda: _ffn_prologue(x_ref, sh_ref, sc_ref, nw_ref, h_ref, acc_ref))

    def cast_accumulate(src, dst):
        for s_ref, d_ref in zip(src, dst):
            d_ref[...] = s_ref[...].reshape(d_ref.shape).astype(bf16)
        _ffn_accumulate(h_ref, acc_ref, *dst)

    if has_tail:
        pl.when(f < last)(lambda: cast_accumulate((wg_ref, wu_ref, wd_ref), (wgb_ref, wub_ref, wdb_ref)))
        pl.when(f == last)(lambda: cast_accumulate((wgt_ref, wut_ref, wdt_ref), (wgtb_ref, wutb_ref, wdtb_ref)))
    else:
        cast_accumulate((wg_ref, wu_ref, wd_ref), (wgb_ref, wub_ref, wdb_ref))
    pl.when(f == last)(lambda: _ffn_epilogue(x_ref, gt_ref, fw_ref, acc_ref, o_ref, final))


def _ffn_cast(x, mod, k0, nw, wg, wu, wd, l, fw, final):
    bx, tx, dm = x.shape
    assert bx == 1
    f_all = wg.shape[2]
    tf = FF_CAST_TILE
    cut = f_all - f_all % FF_TILE
    n_full = cut // tf
    has_tail = cut < f_all
    full = lambda f: jnp.minimum(f, n_full - 1)
    w_in_specs = [pl.BlockSpec((None, dm, tf), lambda b, i, f: (l, 0, full(f))),
                  pl.BlockSpec((None, dm, tf), lambda b, i, f: (l, 0, full(f))),
                  pl.BlockSpec((None, tf, dm), lambda b, i, f: (l, full(f), 0))]
    w_out_specs = [pl.BlockSpec((dm, tf), lambda b, i, f: (0, full(f))),
                   pl.BlockSpec((dm, tf), lambda b, i, f: (0, full(f))),
                   pl.BlockSpec((tf, dm), lambda b, i, f: (full(f), 0))]
    w_out_shapes = [jax.ShapeDtypeStruct((dm, cut), bf16), jax.ShapeDtypeStruct((dm, cut), bf16),
                    jax.ShapeDtypeStruct((cut, dm), bf16)]
    args = [wg, wu, wd]
    if has_tail:
        ft = f_all - cut
        args += [wg, wu, wd]
        el = lambda *shape: tuple(pl.Element(s) for s in shape)
        w_in_specs += [pl.BlockSpec(el(1, dm, ft), lambda b, i, f: (l, 0, cut)),
                       pl.BlockSpec(el(1, dm, ft), lambda b, i, f: (l, 0, cut)),
                       pl.BlockSpec(el(1, ft, dm), lambda b, i, f: (l, cut, 0))]
        w_out_specs += [pl.BlockSpec((dm, ft), lambda b, i, f: (0, 0)), pl.BlockSpec((dm, ft), lambda b, i, f: (0, 0)),
                        pl.BlockSpec((ft, dm), lambda b, i, f: (0, 0))]
        w_out_shapes += [jax.ShapeDtypeStruct((dm, ft), bf16), jax.ShapeDtypeStruct((dm, ft), bf16),
                         jax.ShapeDtypeStruct((ft, dm), bf16)]
    out = pl.pallas_call(
        functools.partial(_ffn_cast_kernel, final=final, has_tail=has_tail),
        grid=(1, 1, n_full + int(has_tail)),
        in_specs=[pl.BlockSpec((None, tx, dm), lambda b, i, f: (b, i, 0)),
                  _mod_spec(True, tx, k0), _mod_spec(True, tx, k0 + 1), _mod_spec(True, tx, k0 + 2),
                  pl.BlockSpec((1, dm), lambda b, i, f: (0, 0)),
                  pl.BlockSpec((1, dm), lambda b, i, f: (0, 0))] + w_in_specs,
        out_specs=[pl.BlockSpec((None, tx, dm), lambda b, i, f: (b, i, 0))] + w_out_specs,
        out_shape=[jax.ShapeDtypeStruct(x.shape, f32)] + w_out_shapes,
        scratch_shapes=[pltpu.VMEM((tx, dm), bf16), pltpu.VMEM((tx, dm), f32)],
        compiler_params=_cp(("arbitrary", "arbitrary", "arbitrary")),
        name="ffn_cast",
    )(x, mod, mod, mod, nw, fw, *args)
    return out[0], tuple(out[1:])


def _inproj_kernel(x_ref, sh_ref, sc_ref, nw_ref, w_ref, ws_ref, o_ref, os_ref, h_ref):
    @pl.when(pl.program_id(2) == 0)
    def _():
        y = _rms(x_ref[...], nw_ref[...])
        h = (y * (1.0 + sc_ref[...]) + sh_ref[...]).astype(bf16)
        h_ref[...] = h
        os_ref[...] = _dot_nt(h, ws_ref[...].astype(bf16))

    o_ref[...] = _dot_nt(h_ref[...], w_ref[...]).astype(o_ref.dtype)


def _inproj(x, mod, nw, w, ws, l, per_row, tm, out_dtype):
    bx, tx, dm = x.shape
    n = w.shape[1]
    tn = INPROJ_COLS
    return pl.pallas_call(
        _inproj_kernel,
        grid=(bx, tx // tm, n // tn),
        in_specs=[pl.BlockSpec((None, tm, dm), lambda b, i, j: (b, i, 0)),
                  _mod_spec(per_row, tm, 3), _mod_spec(per_row, tm, 4),
                  pl.BlockSpec((1, dm), lambda b, i, j: (0, 0)),
                  pl.BlockSpec((None, tn, dm), lambda b, i, j: (l, j, 0)),
                  pl.BlockSpec((None, LANE, dm), lambda b, i, j: (l, 0, 0))],
        out_specs=[pl.BlockSpec((None, tm, tn), lambda b, i, j: (b, i, j)),
                   pl.BlockSpec((None, tm, LANE), lambda b, i, j: (b, i, 0))],
        out_shape=[jax.ShapeDtypeStruct((bx, tx, n), out_dtype),
                   jax.ShapeDtypeStruct((bx, tx, LANE), f32)],
        scratch_shapes=[pltpu.VMEM((tm, dm), bf16)],
        compiler_params=_cp(("parallel", "parallel", "arbitrary")),
        name="inproj",
    )(x, mod, mod, nw, w, ws)


CONV_AT = SUB
CONV_HIST = CONV_AT - (CONV_W - 1)


def _conv_load(ext_ref, x_ref, cst_ref, first):
    c = x_ref.shape[0]
    nblk = ext_ref.shape[0]

    @pl.when(first)
    def _():
        for j in range(nblk):
            ext_ref[j, pl.ds(CONV_HIST, CONV_W - 1), :] = cst_ref[:, j * LANE:(j + 1) * LANE]

    for j in range(nblk):
        ext_ref[j, pl.ds(CONV_AT, c), :] = x_ref[:, j * LANE:(j + 1) * LANE].astype(f32)


def _conv_cols(ext_ref, w_ref, b_ref, lo, width, c):
    n = c // SUB
    cols = []
    for blk in range(lo // LANE, (lo + width) // LANE):
        sl = pl.ds(blk * LANE, LANE)
        w = w_ref[:, sl]
        taps = [ext_ref[blk, pl.ds(CONV_HIST + k, n, stride=SUB), :] for k in range(SUB + CONV_W - 1)]
        ys = []
        for r in range(SUB):
            y = taps[r] * w[0:1]
            for j in range(1, CONV_W):
                y = y + taps[r + j] * w[j:j + 1]
            ys.append(y)
        y = jnp.concatenate(ys, axis=0)
        if b_ref is not None:
            y = y + b_ref[:, sl]
        cols.append(_silu(y))
    return cols[0] if len(cols) == 1 else jnp.concatenate(cols, axis=1)


def _conv_carry(ext_ref, c):
    for j in range(ext_ref.shape[0]):
        ext_ref[j, pl.ds(CONV_HIST, CONV_W - 1), :] = ext_ref[j, pl.ds(CONV_HIST + c, CONV_W - 1), :]


def _cumsum_rows(x, tri):
    return _dot_hi(jnp.where(tri, 1.0, 0.0).astype(f32), x)


def _gdn_kernel(qkv_ref, z_ref, sm_ref, cst_ref, cw_ref, alog_ref, dtb_ref, nw_ref, s0_ref,
                o_ref, s_ref, ext_ref, nat_ref):
    c = qkv_ref.shape[0]
    first = pl.program_id(1) == 0

    @pl.when(first)
    def _():
        s_ref[...] = s0_ref[...]

    _conv_load(ext_ref, qkv_ref, cst_ref, first)

    sm = _load_residue(sm_ref, c)
    beta_l = _sigmoid(sm)
    g_l = -jnp.exp(alog_ref[...]) * jax.nn.softplus(sm + dtb_ref[...])
    tri, strict, eye = _tri_masks_residue(c)
    gc_all = _cumsum_rows(g_l, tri)
    gc_t = gc_all.T
    eye_f = jnp.where(eye, 1.0, 0.0).astype(f32)

    hs = range(GDN_H)
    qb, kb, ks, decay, eg, gcs, nm, rhs = [], [], [], [], [], [], [], []
    for h in hs:
        q = _conv_cols(ext_ref, cw_ref, None, h * GDN_DK, GDN_DK, c)
        k = _conv_cols(ext_ref, cw_ref, None, GDN_QK + h * GDN_DK, GDN_DK, c)
        v = _conv_cols(ext_ref, cw_ref, None, 2 * GDN_QK + h * GDN_DV, GDN_DV, c)
        q = q * lax.rsqrt(jnp.sum(q * q, axis=-1, keepdims=True) + EPS) * (GDN_DK ** -0.5)
        k = k * lax.rsqrt(jnp.sum(k * k, axis=-1, keepdims=True) + EPS)
        beta = beta_l[:, SM_BETA + h:SM_BETA + h + 1]
        gc = gc_all[:, SM_DEC + h:SM_DEC + h + 1]
        d = jnp.exp(jnp.where(tri, gc - gc_t[SM_DEC + h:SM_DEC + h + 1, :], -jnp.inf))
        e = jnp.exp(gc)
        qb.append(q.astype(bf16))
        kb.append(k.astype(bf16))
        ks.append(k)
        decay.append(d)
        eg.append(e)
        gcs.append(gc)
        nm.append(jnp.where(strict, beta * d * _dot_nt(kb[h], kb[h]), 0.0))
        rhs.append(jnp.concatenate([v * beta, k * (beta * e)], axis=1))
        yield

    assert c // GDN_BLK <= 4 and GDN_BLK & (GDN_BLK - 1) == 0
    blk_sh = GDN_BLK.bit_length() - 1
    same_blk = (_row_time(lax.broadcasted_iota(jnp.int32, (c, c), 0), c) >> blk_sh
                == _row_time(lax.broadcasted_iota(jnp.int32, (c, c), 1), c) >> blk_sh)
    nd = [jnp.where(same_blk, nm[h], 0.0) for h in hs]
    lo_s = [_split(nm[h] - nd[h]) for h in hs]
    dinv = [eye_f - nd[h] for h in hs]
    pw_s = [_split(nd[h]) for h in hs]
    for i in range(GDN_BLK.bit_length() - 2):
        pw_s = [_split(_dot3(pw_s[h], pw_s[h])) for h in hs]
        yield
        dinv = [dinv[h] + _dot3(_split(dinv[h]), pw_s[h]) for h in hs]
        yield
    dinv_s = [_split(dinv[h]) for h in hs]
    m = [_dot3(dinv_s[h], lo_s[h]) for h in hs]
    m_s = [_split(m[h]) for h in hs]
    yield
    m2 = [_dot3(m_s[h], m_s[h]) for h in hs]
    yield
    t1 = [eye_f - m[h] + m2[h] - _dot3(m_s[h], _split(m2[h])) for h in hs]
    yield
    inv = [_dot3(_split(t1[h]), dinv_s[h]) for h in hs]
    yield
    sol = []
    for h in hs:
        inv_hi, inv_lo = _split(inv[h])
        rb = rhs[h].astype(bf16)
        sol.append(_dot(inv_hi, rb) + _dot(inv_lo, rb))
    yield

    s = [s_ref[h] for h in hs]
    sb = [s[h].astype(bf16) for h in hs]
    wb = [(sol[h][:, 0:GDN_DV] - _dot(sol[h][:, GDN_DV:].astype(bf16), sb[h])).astype(bf16) for h in hs]
    yield
    qk = [(_dot_nt(qb[h], kb[h]) * decay[h]).astype(bf16) for h in hs]
    yield
    o = [_dot(qb[h], sb[h]) * eg[h] + _dot(qk[h], wb[h]) for h in hs]
    yield
    for h in hs:
        g_last = gcs[h][c - 1:c]
        kd = (ks[h] * jnp.exp(g_last - gcs[h])).astype(bf16)
        s_ref[h] = s[h] * jnp.exp(g_last) + _dot_tn(kd, wb[h])
    yield
    for h in hs:
        sl = slice(h * GDN_DV, (h + 1) * GDN_DV)
        _store_natural(nat_ref, _rms(o[h], nw_ref[...]), c, h * GDN_DV)
        o_ref[:, sl] = (_read_natural(nat_ref, h * GDN_DV, GDN_DV) * _silu(z_ref[:, sl].astype(f32))).astype(bf16)
        yield

    _conv_carry(ext_ref, c)


def _tok(width, off):
    return pl.BlockSpec((None, CHUNK, width), lambda bi, ti: (bi, ti, _blk(off, width)))


def _per_seq(*shape):
    return pl.BlockSpec((None,) + shape, lambda bi, ti: (bi,) + (0,) * len(shape))


def _const(*shape):
    return pl.BlockSpec(shape, lambda bi, ti: (0,) * len(shape))


N_GDN_IN, N_GLA_IN, N_SSD_IN = 9, 9, 11
_DONE = object()


def _mixers_kernel(*refs):
    i0, i1, i2 = N_GDN_IN, N_GDN_IN + N_GLA_IN, N_GDN_IN + N_GLA_IN + N_SSD_IN
    gdn_in, gla_in, ssd_in = refs[:i0], refs[i0:i1], refs[i1:i2]
    oa_ref, sa_ref, ob_ref, sb_ref, oc_ref, sc_ref, ext_a_ref, st_b_ref, ext_c_ref, nat_a_ref, nat_c_ref = refs[i2:]
    live = [_gdn_kernel(*gdn_in, oa_ref, sa_ref, ext_a_ref, nat_a_ref),
            _gla_kernel(*gla_in, ob_ref, sb_ref, st_b_ref),
            _ssd_kernel(*ssd_in, oc_ref, sc_ref, ext_c_ref, nat_c_ref)]
    while live:
        for gen in list(live):
            if next(gen, _DONE) is _DONE:
                live.remove(gen)


GLA_SUB = 16
GDN_BLK = 16


def _gla_kernel(q_ref, k_ref, v_ref, r_ref, sm_ref, wgate_ref, bgate_ref, nw_ref, s0_ref,
                o_ref, so_ref, st_ref):
    c = q_ref.shape[0]
    ti = pl.program_id(1)

    @pl.when(ti == 0)
    def _():
        for h in range(GLA_H):
            st_ref[h] = s0_ref[h].T

    sm = sm_ref[...]
    lane = lax.broadcasted_iota(jnp.int32, sm.shape, 1)
    lr = jnp.where((lane >= SM_LR) & (lane < SM_LR + GLA_RANK), sm, 0.0).astype(bf16)
    la = jax.nn.log_sigmoid(_dot(lr, wgate_ref[...]) + bgate_ref[...]) / GLA_TAU
    tri, _, _ = _tri_masks(c)
    b_all = _cumsum_rows(la, tri)

    col = lax.broadcasted_iota(jnp.int32, (GLA_SUB, c), 1)
    row = lax.broadcasted_iota(jnp.int32, (GLA_SUB, c), 0)
    hs = range(GLA_H)
    qs, ks, vbs, bs, atts = [], [], [], [], []
    for h in hs:
        sk = slice(h * GLA_DK, (h + 1) * GLA_DK)
        q = q_ref[:, sk].astype(f32) * (GLA_DK ** -0.5)
        k = k_ref[:, sk].astype(f32)
        b = b_all[:, sk]
        qs.append(q)
        ks.append(k)
        bs.append(b)
        vbs.append(v_ref[:, h * GLA_DV:(h + 1) * GLA_DV])
        att_rows = []
        for i in range(c // GLA_SUB):
            lo = i * GLA_SUB
            b_i = b[lo:lo + GLA_SUB]
            q_i = q[lo:lo + GLA_SUB]
            b_top = b[lo:lo + 1]
            att = jnp.zeros((GLA_SUB, c), f32)
            if i > 0:
                q_t = (q_i * jnp.exp(b_i - b_top)).astype(bf16)
                k_t = (k * jnp.exp(jnp.minimum(b_top - b, 0.0))).astype(bf16)
                att = jnp.where(col < lo, _dot_nt(q_t, k_t), 0.0)
            for sl in range(GLA_SUB):
                sidx = lo + sl
                e = jnp.exp(jnp.minimum(b_i - b[sidx:sidx + 1], 0.0))
                p = jnp.sum(q_i * k[sidx:sidx + 1] * e, axis=1, keepdims=True)
                att = jnp.where((col == sidx) & (row >= sl), p, att)
            att_rows.append(att)
            yield
        atts.append(jnp.concatenate(att_rows, axis=0).astype(bf16))

    sts = [st_ref[h] for h in hs]
    os_ = [_dot(atts[h], vbs[h]) + _dot_nt((qs[h] * jnp.exp(bs[h])).astype(bf16), sts[h].astype(bf16))
           for h in hs]
    yield
    for h in hs:
        b_last = bs[h][c - 1:c]
        kd = (ks[h] * jnp.exp(b_last - bs[h])).astype(bf16)
        st_ref[h] = sts[h] * jnp.exp(b_last) + _dot_tn(vbs[h], kd)
        yield
    for h in hs:
        sv = slice(h * GLA_DV, (h + 1) * GLA_DV)
        o_ref[:, sv] = (_rms(os_[h], nw_ref[...]) * _silu(r_ref[:, sv].astype(f32))).astype(bf16)
        yield

    @pl.when(ti == pl.num_programs(1) - 1)
    def _():
        for h in hs:
            so_ref[h] = st_ref[h].T


def _ssd_kernel(z_ref, xbc_ref, sm_ref, cst_ref, cw_ref, cb_ref, avec_ref, dtb_ref, dvec_ref, nw_ref, h0_ref,
                o_ref, h_ref, ext_ref, nat_ref):
    c = xbc_ref.shape[0]
    assert c == SSD_P, "head pairs share a lane tile: the chunk must be as wide as a head"
    first = pl.program_id(1) == 0

    @pl.when(first)
    def _():
        h_ref[...] = h0_ref[...]

    _conv_load(ext_ref, xbc_ref, cst_ref, first)

    dt_l = jax.nn.softplus(_load_residue(sm_ref, c) + dtb_ref[...])
    tri, _, _ = _tri_masks_residue(c)
    ac_all = _cumsum_rows(dt_l * avec_ref[...], tri)
    ac_t2 = jnp.concatenate([ac_all, ac_all], axis=0).T

    lane = lax.broadcasted_iota(jnp.int32, (c, 2 * SSD_P), 1)
    lo_half = lane < SSD_P
    tri2, _, _ = _tri_masks_residue(c, 2 * SSD_P)
    lane1 = lane[0:1]
    gs_ = range(SSD_G)
    ps_ = range(SSD_HG // 2)

    xs_, bmb, hg, cb2, ch = [], [], [], [], []
    for g in gs_:
        xs_.append(_conv_cols(ext_ref, cw_ref, cb_ref, g * SSD_GW, SSD_GW, c))
        bmb.append(_conv_cols(ext_ref, cw_ref, cb_ref, SSD_INNER + g * SSD_N, SSD_N, c).astype(bf16))
        cmb = _conv_cols(ext_ref, cw_ref, cb_ref, SSD_INNER + SSD_BC + g * SSD_N, SSD_N, c).astype(bf16)
        hg.append(h_ref[pl.ds(g * SSD_HG, SSD_HG)].reshape(SSD_GW, SSD_N))
        cb2.append(_dot_nt(cmb, jnp.concatenate([bmb[g], bmb[g]], axis=0)))
        ch.append(_dot_nt(cmb, hg[g].astype(bf16)))
        yield

    m2, rhs, e_col, xsc = {}, {}, {}, {}
    for g in gs_:
        for p in ps_:
            l0 = SM_DT + g * SSD_HG + 2 * p
            ps = slice(p * 2 * SSD_P, (p + 1) * 2 * SSD_P)
            ac_col = jnp.where(lo_half, ac_all[:, l0:l0 + 1], ac_all[:, l0 + 1:l0 + 2])
            ac_row = jnp.where(lane1 < SSD_P, ac_t2[l0:l0 + 1, :], ac_t2[l0 + 1:l0 + 2, :])
            decay = jnp.exp(jnp.where(tri2, ac_col - ac_row, -jnp.inf))
            m2[g, p] = (cb2[g] * decay).astype(bf16)
            dt2 = jnp.where(lo_half, dt_l[:, l0:l0 + 1], dt_l[:, l0 + 1:l0 + 2])
            xdt = xs_[g][:, ps] * dt2
            rhs[g, p] = jnp.concatenate([jnp.where(lo_half, xdt, 0.0), jnp.where(lo_half, 0.0, xdt)],
                                        axis=0).astype(bf16)
            e_col[g, p] = jnp.exp(ac_col)
            xsc[g, p] = xdt * jnp.exp(ac_col[c - 1:c] - ac_col)
            yield
    y2 = {gp: _dot(m2[gp], rhs[gp]) for gp in m2}
    yield

    for g in gs_:
        xsg = jnp.concatenate([xsc[g, p] for p in ps_], axis=1).astype(bf16)
        dh = _dot_tn(xsg, bmb[g])
        for j in range(SSD_HG):
            ln = SM_DT + g * SSD_HG + j
            h_ref[g * SSD_HG + j] = (hg[g][j * SSD_P:(j + 1) * SSD_P] * jnp.exp(ac_all[c - 1:c, ln:ln + 1])
                                     + dh[j * SSD_P:(j + 1) * SSD_P])
        yield
    for g in gs_:
        gs = slice(g * SSD_GW, (g + 1) * SSD_GW)
        y = jnp.concatenate([y2[g, p] + ch[g][:, p * 2 * SSD_P:(p + 1) * 2 * SSD_P] * e_col[g, p] for p in ps_],
                            axis=1)
        _store_natural(nat_ref, y + dvec_ref[:, gs] * xs_[g], c, g * SSD_GW)
        y = _read_natural(nat_ref, g * SSD_GW, SSD_GW) * _silu(z_ref[:, gs].astype(f32))
        o_ref[:, gs] = _rms(y, nw_ref[:, gs]).astype(bf16)
        yield

    _conv_carry(ext_ref, c)


STEP_B = 8


def _conv_step(x_ref, c_ref, w_ref):
    w = w_ref[...]
    return c_ref[0] * w[0:1] + c_ref[1] * w[1:2] + c_ref[2] * w[2:3] + x_ref[...] * w[3:4]


def _expand_matrix(nb, width):
    r = lax.broadcasted_iota(jnp.int32, (nb, nb * width), 0)
    c = lax.broadcasted_iota(jnp.int32, (nb, nb * width), 1)
    return jnp.where((c >= r * width) & (c < (r + 1) * width), 1.0, 0.0).astype(bf16)


def _bcast_cols(x, e_mat):
    hi, lo = _split(x)
    return _dot_tn(hi, e_mat) + _dot_tn(lo, e_mat)


def _state_call(kernel, grid, in_specs, out_specs, out_shape, scratch, sem, name, args, prev_state):
    aliases = {}
    if prev_state is not None:
        in_specs = in_specs + [pl.BlockSpec(memory_space=pl.ANY)]
        args = args + (prev_state,)
        aliases = {len(args) - 1: len(out_shape) - 1}
        kernel = functools.partial(_drop_alias_ref, kernel, len(args) - 1)
    return pl.pallas_call(kernel, grid=grid, in_specs=in_specs, out_specs=out_specs, out_shape=out_shape,
                          scratch_shapes=scratch, input_output_aliases=aliases,
                          compiler_params=_cp(sem), name=name)(*args)


def _drop_alias_ref(kernel, idx, *refs):
    return kernel(*refs[:idx], *refs[idx + 1:])


def _gdn_step_kernel(alog_ref, dtb_ref, q_ref, k_ref, v_ref, z_ref, sm_ref, cq_ref, ck_ref, cv_ref,
                     wq_ref, wk_ref, wv_ref, nw_ref, s_ref, o_ref, so_ref):
    nb = q_ref.shape[0]
    qa = _silu(_conv_step(q_ref, cq_ref, wq_ref))
    ka = _silu(_conv_step(k_ref, ck_ref, wk_ref))
    va = _silu(_conv_step(v_ref, cv_ref, wv_ref))
    sm = sm_ref[...]
    z = z_ref[...]
    e_mat = _expand_matrix(nb, GDN_DV)
    for h in range(GDN_H):
        sl = slice(h * GDN_DK, (h + 1) * GDN_DK)
        q = qa[:, sl]
        k = ka[:, sl]
        v = va[:, sl]
        q = q * lax.rsqrt(jnp.sum(q * q, axis=-1, keepdims=True) + EPS) * (GDN_DK ** -0.5)
        k = k * lax.rsqrt(jnp.sum(k * k, axis=-1, keepdims=True) + EPS)
        beta = _sigmoid(sm[:, SM_BETA + h:SM_BETA + h + 1])
        g = -jnp.exp(_scalar_vec(alog_ref[h])) * jax.nn.softplus(sm[:, SM_DEC + h:SM_DEC + h + 1] + dtb_ref[h])
        eg = jnp.exp(g)
        qk = jnp.sum(q * k, axis=-1, keepdims=True)
        kb = _bcast_cols(k, e_mat)
        qb = _bcast_cols(q, e_mat)
        blk = lambda m, b: m[:, b * GDN_DV:(b + 1) * GDN_DV]
        ks = jnp.concatenate([jnp.sum(s_ref[b, h] * blk(kb, b), axis=0, keepdims=True) for b in range(nb)], axis=0)
        qs = jnp.concatenate([jnp.sum(s_ref[b, h] * blk(qb, b), axis=0, keepdims=True) for b in range(nb)], axis=0)
        w = beta * v - (beta * eg) * ks
        o = qs * eg + qk * w
        for b in range(nb):
            so_ref[b, h] = s_ref[b, h] * eg[b:b + 1] + blk(kb, b) * w[b:b + 1]
        o = _rms(o, nw_ref[...]) * _silu(z[:, sl])
        o_ref[:, sl] = o.astype(bf16)


def _gdn_step(proj, small, conv_t, states, prev, l, conv_w, a_log, dt_bias, norm_w):
    n = proj.shape[0]
    nb = STEP_B
    blk = lambda off: pl.BlockSpec((nb, GDN_QK), lambda i: (i, _blk(off, GDN_QK)))
    cst = lambda j: pl.BlockSpec((CONV_W - 1, nb, GDN_QK), lambda i: (0, i, j))
    cw = lambda j: pl.BlockSpec((CONV_W, GDN_QK), lambda i: (0, j))
    smem = pl.BlockSpec(memory_space=pltpu.SMEM)
    st = pl.BlockSpec((None, nb, GDN_H, GDN_DK, GDN_DV), lambda i: (l, i, 0, 0, 0))
    return _state_call(
        _gdn_step_kernel, (n // nb,),
        [smem, smem, blk(P_QKV_A), blk(P_QKV_A + GDN_QK), blk(P_QKV_A + 2 * GDN_QK), blk(P_Z_A),
         pl.BlockSpec((nb, LANE), lambda i: (i, 0)),
         cst(0), cst(1), cst(2), cw(0), cw(1), cw(2),
         pl.BlockSpec((1, GDN_DV), lambda i: (0, 0)), st],
        [pl.BlockSpec((nb, GDN_V), lambda i: (i, 0)), st],
        [jax.ShapeDtypeStruct((n, GDN_V), bf16), jax.ShapeDtypeStruct(states.shape, f32)],
        [], ("parallel",), "gdn_step",
        (a_log, dt_bias, proj, proj, proj, proj, small, conv_t, conv_t, conv_t, conv_w, conv_w, conv_w, norm_w, states),
        prev)


def _gla_step_kernel(q_ref, k_ref, v_ref, r_ref, sm_ref, wgate_ref, bgate_ref, nw_ref, s_ref,
                     o_ref, so_ref):
    nb = q_ref.shape[0]
    sm = sm_ref[...]
    lane = lax.broadcasted_iota(jnp.int32, sm.shape, 1)
    lr = jnp.where((lane >= SM_LR) & (lane < SM_LR + GLA_RANK), sm, 0.0).astype(bf16)
    la_all = jax.nn.log_sigmoid(_dot(lr, wgate_ref[...]) + bgate_ref[...]) / GLA_TAU
    qa = q_ref[...] * (GLA_DK ** -0.5)
    ka = k_ref[...]
    va = v_ref[...]
    r = r_ref[...]
    for h in range(GLA_H):
        sk = slice(h * GLA_DK, (h + 1) * GLA_DK)
        sv = slice(h * GLA_DV, (h + 1) * GLA_DV)
        q = qa[:, sk]
        k = ka[:, sk]
        v = va[:, sv]
        e = jnp.exp(la_all[:, sk])
        qk = jnp.sum(q * k, axis=-1, keepdims=True)
        e_t = e.T
        k_t = k.T
        qe_t = (q * e).T
        rows = []
        for b in range(nb):
            s = s_ref[b, h]
            vrow = v[b:b + 1]
            rows.append(qk[b:b + 1] * vrow + jnp.sum(s * qe_t[:, b:b + 1], axis=0, keepdims=True))
            so_ref[b, h] = s * e_t[:, b:b + 1] + k_t[:, b:b + 1] * vrow
        o = jnp.concatenate(rows, axis=0)
        o = _rms(o, nw_ref[...]) * _silu(r[:, sv])
        o_ref[:, sv] = o.astype(bf16)


def _gla_step(proj, small, states, prev, l, wgate_pad, bgate, norm_w):
    n = proj.shape[0]
    nb = STEP_B
    st = pl.BlockSpec((None, nb, GLA_H, GLA_DK, GLA_DV), lambda i: (l, i, 0, 0, 0))
    return _state_call(
        _gla_step_kernel, (n // nb,),
        [pl.BlockSpec((nb, GLA_QK), lambda i: (i, _blk(P_Q_B, GLA_QK))),
         pl.BlockSpec((nb, GLA_QK), lambda i: (i, _blk(P_K_B, GLA_QK))),
         pl.BlockSpec((nb, GLA_V), lambda i: (i, _blk(P_V_B, GLA_V))),
         pl.BlockSpec((nb, GLA_V), lambda i: (i, _blk(P_R_B, GLA_V))),
         pl.BlockSpec((nb, LANE), lambda i: (i, 0)),
         pl.BlockSpec((LANE, GLA_QK), lambda i: (0, 0)),
         pl.BlockSpec((1, GLA_QK), lambda i: (0, 0)),
         pl.BlockSpec((1, GLA_DV), lambda i: (0, 0)), st],
        [pl.BlockSpec((nb, GLA_V), lambda i: (i, 0)), st],
        [jax.ShapeDtypeStruct((n, GLA_V), bf16), jax.ShapeDtypeStruct(states.shape, f32)],
        [], ("parallel",), "gla_step",
        (proj, proj, proj, proj, small, wgate_pad, bgate, norm_w, states),
        prev)


def _ssd_step_kernel(alog_ref, dtb_ref, z_ref, xbc_ref, sm_ref, cst_ref, cw_ref, cb_ref, dvec_ref, nw_ref, h_ref,
                     o_ref, ho_ref):
    nb = xbc_ref.shape[0]
    xbc = _silu(_conv_step(xbc_ref, cst_ref, cw_ref) + cb_ref[...])
    sm_t = sm_ref[...].T
    e_mat = _expand_matrix(nb, SSD_N)
    lane_b = lax.broadcasted_iota(jnp.int32, (SSD_P, nb), 1)
    for g in range(SSD_G):
        gs = slice(g * SSD_GW, (g + 1) * SSD_GW)
        bm = xbc[:, SSD_INNER + g * SSD_N:SSD_INNER + (g + 1) * SSD_N]
        cm_t = xbc[:, SSD_INNER + SSD_BC + g * SSD_N:SSD_INNER + SSD_BC + (g + 1) * SSD_N].T
        cb_row = jnp.sum(cm_t * bm.T, axis=0, keepdims=True)
        hg = h_ref[:, pl.ds(g * SSD_HG, SSD_HG)].reshape(nb * SSD_GW, SSD_N)
        ch = _dot(hg.astype(bf16), cm_t.astype(bf16))
        pairs = []
        for jp in range(SSD_HG // 2):
            x_t = xbc[:, g * SSD_GW + jp * LANE:g * SSD_GW + (jp + 1) * LANE].T
            halves = []
            for jj in range(2):
                j = 2 * jp + jj
                hd = g * SSD_HG + j
                dt_row = jax.nn.softplus(sm_t[SM_DT + hd:SM_DT + hd + 1, :] + dtb_ref[hd])
                ea_row = jnp.exp(dt_row * (-jnp.exp(_scalar_vec(alog_ref[hd]))))
                xdt_t = x_t[jj * SSD_P:(jj + 1) * SSD_P] * dt_row
                hi, lo = _split(xdt_t)
                xb = _dot(hi, e_mat) + _dot(lo, e_mat)
                yh = jnp.zeros((SSD_P, nb), f32)
                for b in range(nb):
                    r0 = (b * SSD_HG + j) * SSD_P
                    yh = jnp.where(lane_b == b, ch[r0:r0 + SSD_P], yh)
                    ho_ref[b, hd] = (h_ref[b, hd] * ea_row[:, b:b + 1]
                                     + xb[:, b * SSD_N:(b + 1) * SSD_N] * bm[b:b + 1])
                halves.append(yh * ea_row + cb_row * xdt_t)
            pairs.append(jnp.concatenate(halves, axis=0).T)
        y = jnp.concatenate(pairs, axis=1)
        y = (y + dvec_ref[:, gs] * xbc[:, gs]) * _silu(z_ref[:, gs])
        o_ref[:, gs] = _rms(y, nw_ref[:, gs]).astype(bf16)


def _ssd_step(proj, small, conv_t, states, prev, l, conv_w, conv_b, a_log, dt_bias, dvec, norm_w):
    n = proj.shape[0]
    nb = STEP_B
    smem = pl.BlockSpec(memory_space=pltpu.SMEM)
    const = lambda shape: pl.BlockSpec(shape, lambda i: (0,) * len(shape))
    st = pl.BlockSpec((None, nb, SSD_H, SSD_P, SSD_N), lambda i: (l, i, 0, 0, 0))
    return _state_call(
        _ssd_step_kernel, (n // nb,),
        [smem, smem,
         pl.BlockSpec((nb, SSD_INNER), lambda i: (i, _blk(P_Z_C, SSD_INNER))),
         pl.BlockSpec((nb, SSD_CONV), lambda i: (i, _blk(P_XBC, SSD_CONV))),
         pl.BlockSpec((nb, LANE), lambda i: (i, 0)),
         pl.BlockSpec((CONV_W - 1, nb, SSD_CONV), lambda i: (0, i, 0)),
         const((CONV_W, SSD_CONV)), const((1, SSD_CONV)), const((1, SSD_INNER)), const((1, SSD_INNER)), st],
        [pl.BlockSpec((nb, SSD_INNER), lambda i: (i, 0)), st],
        [jax.ShapeDtypeStruct((n, SSD_INNER), bf16), jax.ShapeDtypeStruct(states.shape, f32)],
        [], ("parallel",), "ssd_step",
        (a_log, dt_bias, proj, proj, small, conv_t, conv_w, conv_b, dvec, norm_w, states),
        prev)


def _merge_kernel(oa_ref, ob_ref, oc_ref, ga_ref, gb_ref, gc_ref, wa_ref, wb_ref, wc_ref, o_ref):
    m = (_sigmoid(ga_ref[...].astype(f32)) * _dot(oa_ref[...], wa_ref[...])
         + _sigmoid(gb_ref[...].astype(f32)) * _dot(ob_ref[...], wb_ref[...])
         + _sigmoid(gc_ref[...].astype(f32)) * _dot(oc_ref[...], wc_ref[...]))
    o_ref[...] = m.astype(bf16)


def _outproj_kernel(m_ref, x_ref, gt_ref, w_ref, o_ref):
    o_ref[...] = x_ref[...] + gt_ref[...] * _dot(m_ref[...], w_ref[...])


def _mixout(oa, ob, oc, proj, x, mod, wa, wb, wc, wo, l, per_row, tm):
    bx, tx, dm = x.shape
    tn = MERGE_COLS
    gate = lambda k: pl.BlockSpec((None, tm, tn), lambda b, i, j: (b, i, _blk(P_GATES + k * D_MODEL, tn) + j))
    act = lambda w: pl.BlockSpec((None, tm, w), lambda b, i, j: (b, i, 0))
    wsp = lambda w: pl.BlockSpec((None, w, tn), lambda b, i, j: (l, 0, j))
    merged = pl.pallas_call(
        _merge_kernel,
        grid=(bx, tx // tm, dm // tn),
        in_specs=[act(GDN_V), act(GLA_V), act(SSD_INNER), gate(0), gate(1), gate(2),
                  wsp(GDN_V), wsp(GLA_V), wsp(SSD_INNER)],
        out_specs=pl.BlockSpec((None, tm, tn), lambda b, i, j: (b, i, j)),
        out_shape=jax.ShapeDtypeStruct((bx, tx, dm), bf16),
        compiler_params=_cp(("parallel", "parallel", "arbitrary")),
        name="merge",
    )(oa, ob, oc, proj, proj, proj, wa, wb, wc)
    to = OUTPROJ_COLS
    nj = dm // to
    if per_row:
        gspec = pl.BlockSpec((None, tm, to), lambda b, i, j: (b, i, 5 * nj + j))
    else:
        gspec = pl.BlockSpec((None, 1, to), lambda b, i, j: (b, 0, 5 * nj + j))
    return pl.pallas_call(
        _outproj_kernel,
        grid=(bx, tx // tm, nj),
        in_specs=[pl.BlockSpec((None, tm, dm), lambda b, i, j: (b, i, 0)),
                  pl.BlockSpec((None, tm, to), lambda b, i, j: (b, i, j)),
                  gspec,
                  pl.BlockSpec((None, dm, to), lambda b, i, j: (l, 0, j))],
        out_specs=pl.BlockSpec((None, tm, to), lambda b, i, j: (b, i, j)),
        out_shape=jax.ShapeDtypeStruct(x.shape, f32),
        compiler_params=_cp(("parallel", "parallel", "arbitrary")),
        name="outproj",
    )(merged, x, mod, wo)


REGROUP_ROWS = 512


def _regroup_kernel(w_ref, o_ref):
    o_ref[...] = w_ref[0].astype(bf16)


def _regroup_rows(wt):
    nl, _, d = wt.shape
    tr = REGROUP_ROWS
    shifts = []
    for name in MAIN_ORDER:
        a, b = W_IN_SRC[name]
        assert (b - a) % tr == 0 and a % SUB == 0
        shifts.append((P_MAIN[name] // tr, a - P_MAIN[name]))

    def src_row(i):
        off = i * (tr // SUB) + shifts[0][1] // SUB
        for k in range(1, len(shifts)):
            off = off + jnp.where(i >= shifts[k][0], (shifts[k][1] - shifts[k - 1][1]) // SUB, 0)
        return off * SUB

    return pl.pallas_call(
        _regroup_kernel,
        grid=(nl, P_TOTAL // tr),
        in_specs=[pl.BlockSpec((pl.Element(1), pl.Element(tr), pl.Element(d)), lambda l, i: (l, src_row(i), 0))],
        out_specs=pl.BlockSpec((None, tr, d), lambda l, i: (l, i, 0)),
        out_shape=jax.ShapeDtypeStruct((nl, P_TOTAL, d), bf16),
        compiler_params=_cp(("parallel", "parallel")),
        name="regroup_w_in",
    )(wt)


def _permute_w_in(w):
    nl, d, _ = w.shape
    main = _regroup_rows(jnp.swapaxes(w, 1, 2))
    small = jnp.concatenate([w[:, :, W_IN_SRC[name][0]:W_IN_SRC[name][1]] for name in SMALL_ORDER], axis=2)
    small = jnp.pad(jnp.swapaxes(small, 1, 2), ((0, 0), (0, LANE - small.shape[2]), (0, 0)))
    return main, small


def _lane_vec(v, lo):
    return jnp.zeros((1, LANE), f32).at[0, lo:lo + v.shape[0]].set(v)


def _layer_params(l, p):
    row = lambda a: a[l].reshape(1, -1)
    wgate = jnp.zeros((LANE, GLA_QK), f32).at[SM_LR:SM_LR + GLA_RANK].set(p["gla_w_gate"][l]).astype(bf16)
    return dict(
        norm1=row(p["norm1"]), norm2=row(p["norm2"]), norm3=row(p["norm3"]),
        gdn_conv_w=p["gdn_conv_w"][l], gdn_a_log=p["gdn_a_log"][l], gdn_dt_bias=p["gdn_dt_bias"][l],
        gdn_alog_l=_lane_vec(p["gdn_a_log"][l], SM_DEC), gdn_dtb_l=_lane_vec(p["gdn_dt_bias"][l], SM_DEC),
        gdn_norm_w=row(p["gdn_norm_w"]),
        gla_wgate=wgate, gla_bgate=row(p["gla_b_gate"]), gla_norm_w=row(p["gla_norm_w"]),
        ssd_conv_w=p["ssd_conv_w"][l], ssd_conv_b=row(p["ssd_conv_b"]), ssd_a_log=p["ssd_a_log"][l],
        ssd_dt_bias=p["ssd_dt_bias"][l],
        ssd_avec_l=_lane_vec(-jnp.exp(p["ssd_a_log"][l]), SM_DT), ssd_dtb_l=_lane_vec(p["ssd_dt_bias"][l], SM_DT),
        ssd_dvec=jnp.repeat(p["ssd_d"][l], SSD_P).reshape(1, -1),
        ssd_norm_w=row(p["ssd_norm_w"]),
    )


def _stacked_weights(p):
    w_in, w_in_small = _permute_w_in(p["w_in"])
    return dict(
        f1=(p["ffn1_wg"], p["ffn1_wu"], p["ffn1_wd"]),
        f2=(p["ffn2_wg"], p["ffn2_wu"], p["ffn2_wd"]),
        w_in=w_in, w_in_small=w_in_small,
        wa=p["w_branch_gdn"].astype(bf16), wb=p["w_branch_gla"].astype(bf16),
        wc=p["w_branch_ssd"].astype(bf16), w_out=p["w_out"].astype(bf16),
    )


def _new_conv_state(buf, raw):
    t = raw.shape[1]
    k = CONV_W - 1
    if t >= k:
        return raw[:, t - k:]
    return jnp.concatenate([buf[:, t:], raw], axis=1)


def _mixer_prompt(proj, small, lp, st):
    gdn_conv, s_gdn, s_gla, ssd_conv, s_ssd = st
    b, t, _ = proj.shape
    c = CHUNK
    sm_spec = pl.BlockSpec((None, c, LANE), lambda bi, ti: (bi, ti, 0))
    st_a, st_b, st_c = _per_seq(GDN_H, GDN_DK, GDN_DV), _per_seq(GLA_H, GLA_DK, GLA_DV), _per_seq(SSD_H, SSD_P, SSD_N)
    gdn_specs = [_tok(GDN_CONV, P_QKV_A), _tok(GDN_V, P_Z_A), sm_spec, _per_seq(CONV_W - 1, GDN_CONV),
                 _const(CONV_W, GDN_CONV), _const(1, LANE), _const(1, LANE), _const(1, GDN_DV), st_a]
    gdn_args = (proj, proj, small, gdn_conv, lp["gdn_conv_w"], lp["gdn_alog_l"], lp["gdn_dtb_l"],
                lp["gdn_norm_w"], s_gdn)
    gla_specs = [_tok(GLA_QK, P_Q_B), _tok(GLA_QK, P_K_B), _tok(GLA_V, P_V_B), _tok(GLA_V, P_R_B), sm_spec,
                 _const(LANE, GLA_QK), _const(1, GLA_QK), _const(1, GLA_DV), st_b]
    gla_args = (proj, proj, proj, proj, small, lp["gla_wgate"], lp["gla_bgate"], lp["gla_norm_w"], s_gla)
    ssd_specs = [_tok(SSD_INNER, P_Z_C), _tok(SSD_CONV, P_XBC), sm_spec, _per_seq(CONV_W - 1, SSD_CONV),
                 _const(CONV_W, SSD_CONV), _const(1, SSD_CONV), _const(1, LANE), _const(1, LANE),
                 _const(1, SSD_INNER), _const(1, SSD_INNER), st_c]
    ssd_args = (proj, proj, small, ssd_conv, lp["ssd_conv_w"], lp["ssd_conv_b"], lp["ssd_avec_l"],
                lp["ssd_dtb_l"], lp["ssd_dvec"], lp["ssd_norm_w"], s_ssd)
    assert (len(gdn_specs), len(gla_specs), len(ssd_specs)) == (N_GDN_IN, N_GLA_IN, N_SSD_IN)
    out_tok = lambda width: pl.BlockSpec((None, c, width), lambda bi, ti: (bi, ti, 0))
    oa, s_gdn_n, ob, s_gla_n, oc, s_ssd_n = pl.pallas_call(
        _mixers_kernel,
        grid=(b, t // c),
        in_specs=gdn_specs + gla_specs + ssd_specs,
        out_specs=[out_tok(GDN_V), st_a, out_tok(GLA_V), st_b, out_tok(SSD_INNER), st_c],
        out_shape=[jax.ShapeDtypeStruct((b, t, GDN_V), bf16), jax.ShapeDtypeStruct(s_gdn.shape, f32),
                   jax.ShapeDtypeStruct((b, t, GLA_V), bf16), jax.ShapeDtypeStruct(s_gla.shape, f32),
                   jax.ShapeDtypeStruct((b, t, SSD_INNER), bf16), jax.ShapeDtypeStruct(s_ssd.shape, f32)],
        scratch_shapes=[pltpu.VMEM((GDN_CONV // LANE, CONV_AT + c, LANE), f32),
                        pltpu.VMEM((GLA_H, GLA_DV, GLA_DK), f32),
                        pltpu.VMEM((SSD_CONV // LANE, CONV_AT + c, LANE), f32),
                        pltpu.VMEM((GDN_V // LANE, c, LANE), f32), pltpu.VMEM((SSD_INNER // LANE, c, LANE), f32)],
        compiler_params=_cp(("parallel", "arbitrary")),
        name="mixers_prompt",
    )(*gdn_args, *gla_args, *ssd_args)
    gdn_conv_n = _new_conv_state(gdn_conv, proj[:, :, P_QKV_A:P_QKV_A + GDN_CONV]).astype(f32)
    ssd_conv_n = _new_conv_state(ssd_conv, proj[:, :, P_XBC:P_XBC + SSD_CONV]).astype(f32)
    return (oa, ob, oc), (gdn_conv_n, s_gdn_n, s_gla_n, ssd_conv_n, s_ssd_n)


def _mixer_sample(proj, small, lp, l, states, prev):
    n = proj.shape[1]
    p2 = proj.reshape(n, P_TOTAL)
    s2 = small.reshape(n, LANE)
    gdn_conv, ssd_conv = states[0][l], states[3][l]
    gct = jnp.swapaxes(gdn_conv, 0, 1)
    sct = jnp.swapaxes(ssd_conv, 0, 1)
    pv = (None,) * 5 if prev is None else prev
    oa, s_gdn_n = _gdn_step(p2, s2, gct, states[1], pv[1], l, lp["gdn_conv_w"], lp["gdn_a_log"],
                            lp["gdn_dt_bias"], lp["gdn_norm_w"])
    ob, s_gla_n = _gla_step(p2, s2, states[2], pv[2], l, lp["gla_wgate"], lp["gla_bgate"], lp["gla_norm_w"])
    oc, s_ssd_n = _ssd_step(p2, s2, sct, states[4], pv[4], l, lp["ssd_conv_w"], lp["ssd_conv_b"], lp["ssd_a_log"],
                            lp["ssd_dt_bias"], lp["ssd_dvec"], lp["ssd_norm_w"])
    raw = p2.reshape(n, 1, P_TOTAL)
    gdn_conv_n = _new_conv_state(gdn_conv, raw[:, :, P_QKV_A:P_QKV_A + GDN_CONV])
    ssd_conv_n = _new_conv_state(ssd_conv, raw[:, :, P_XBC:P_XBC + SSD_CONV])
    outs = tuple(o.reshape(1, n, -1) for o in (oa, ob, oc))
    return outs, (gdn_conv_n, s_gdn_n, s_gla_n, ssd_conv_n, s_ssd_n)


def _trunk(x, mods, lps, sw, ffn_w, states, per_row, tm, tm_in, final_w):
    nl = len(lps)
    per_layer = []
    prev = None

    def ffn(x, mod, k0, nw, name, l, fw, final):
        if per_row:
            x, ffn_w[l, name] = _ffn_cast(x, mod, k0, nw, *sw[name], l, fw, final)
            return x
        return _ffn(x, mod, k0, nw, ffn_w[l, name], fw, per_row, tm, final)

    for l in range(nl):
        lp, mod = lps[l], mods[l]
        last = l == nl - 1
        x = ffn(x, mod, 0, lp["norm1"], "f1", l, lp["norm1"], False)
        proj, small = _inproj(x, mod, lp["norm2"], sw["w_in"], sw["w_in_small"], l, per_row, tm_in,
                              f32 if per_row else bf16)
        if per_row:
            (oa, ob, oc), st = _mixer_sample(proj, small, lp, l, states, prev)
            prev = st
        else:
            (oa, ob, oc), st = _mixer_prompt(proj, small, lp, tuple(s[l] for s in states))
        per_layer.append(st)
        x = _mixout(oa, ob, oc, proj, x, mod, sw["wa"], sw["wb"], sw["wc"], sw["w_out"], l, per_row, tm_in)
        x = ffn(x, mod, 6, lp["norm3"], "f2", l, final_w if last else lp["norm3"], last)
    stack = lambda i: jnp.stack([st[i] for st in per_layer])
    if per_row:
        new_states = (stack(0), prev[1], prev[2], stack(3), prev[4])
    else:
        new_states = tuple(stack(i) for i in range(5))
    return x, new_states


def kernel(x_prompt, x_sample, state_gdn_conv, state_gdn, state_gla, state_ssd_conv, state_ssd, c_prompt, c_sample, w_ada, b_ada, norm1, norm2, norm3, ffn1_wg, ffn1_wu, ffn1_wd, ffn2_wg, ffn2_wu, ffn2_wd, w_in, gdn_conv_w, gdn_a_log, gdn_dt_bias, gdn_norm_w, gla_w_gate, gla_b_gate, gla_norm_w, ssd_conv_w, ssd_conv_b, ssd_a_log, ssd_dt_bias, ssd_d, ssd_norm_w, w_branch_gdn, w_branch_gla, w_branch_ssd, w_out, final_norm):
    p = dict(norm1=norm1, norm2=norm2, norm3=norm3,
             ffn1_wg=ffn1_wg, ffn1_wu=ffn1_wu, ffn1_wd=ffn1_wd, ffn2_wg=ffn2_wg, ffn2_wu=ffn2_wu, ffn2_wd=ffn2_wd,
             w_in=w_in, gdn_conv_w=gdn_conv_w, gdn_a_log=gdn_a_log, gdn_dt_bias=gdn_dt_bias, gdn_norm_w=gdn_norm_w,
             gla_w_gate=gla_w_gate, gla_b_gate=gla_b_gate, gla_norm_w=gla_norm_w,
             ssd_conv_w=ssd_conv_w, ssd_conv_b=ssd_conv_b, ssd_a_log=ssd_a_log, ssd_dt_bias=ssd_dt_bias,
             ssd_d=ssd_d, ssd_norm_w=ssd_norm_w,
             w_branch_gdn=w_branch_gdn, w_branch_gla=w_branch_gla, w_branch_ssd=w_branch_ssd, w_out=w_out)
    nl = w_ada.shape[0]
    bp, tp, dm = x_prompt.shape
    bs = x_sample.shape[0]
    assert x_sample.shape[1] == 1 and tp % CHUNK == 0 and bs % STEP_B == 0 and dm == D_MODEL
    lps = [_layer_params(l, p) for l in range(nl)]
    sw = _stacked_weights(p)
    fw = final_norm.reshape(1, dm)

    rows = bp + bs
    rpad = -(-rows // 8) * 8
    c_all = jnp.concatenate([c_prompt, c_sample, jnp.zeros((rpad - rows, dm), f32)], axis=0)
    mod = _ada_mod(c_all, w_ada, b_ada)
    mod_p = [mod[l, :bp].reshape(bp, 1, N_MOD * dm) for l in range(nl)]
    mod_s = [mod[l, bp:rows].reshape(1, bs, N_MOD * dm) for l in range(nl)]

    sample_states = (state_gdn_conv, state_gdn, state_gla, state_ssd_conv, state_ssd)
    prompt_states = tuple(jnp.zeros((s.shape[0], bp) + s.shape[2:], x_prompt.dtype) for s in sample_states)
    tm_p = ROW_TILE if tp % ROW_TILE == 0 else CHUNK
    tm_in = ROW_TILE_WIDE if tp % ROW_TILE_WIDE == 0 else tm_p
    ffn_w = {}
    y_s, st_s = _trunk(x_sample.reshape(1, bs, dm), mod_s, lps, sw, ffn_w, sample_states, True, bs, bs, fw)
    y_p, st_p = _trunk(x_prompt, mod_p, lps, sw, ffn_w, prompt_states, False, tm_p, tm_in, fw)
    return (y_p, y_s.reshape(bs, 1, dm)) + st_p + st_s
```

```python
import functools

import jax
import jax.numpy as jnp
from jax import lax
from jax.experimental import pallas as pl
from jax.experimental.pallas import tpu as pltpu

f32 = jnp.float32
bf16 = jnp.bfloat16
HI = lax.Precision.HIGHEST

EPS = 1e-6
D_MODEL = 2048
N_MOD = 9
CHUNK = 64
CONV_W = 4
GDN_H, GDN_DK, GDN_DV = 8, 128, 128
GLA_H, GLA_DK, GLA_DV, GLA_RANK, GLA_TAU = 4, 128, 256, 16, 16.0
SSD_H, SSD_P, SSD_G, SSD_N = 32, 64, 4, 128
SSD_HG = SSD_H // SSD_G
GDN_QK = GDN_H * GDN_DK
GDN_V = GDN_H * GDN_DV
GDN_CONV = 2 * GDN_QK + GDN_V
GLA_QK = GLA_H * GLA_DK
GLA_V = GLA_H * GLA_DV
SSD_INNER = SSD_H * SSD_P
SSD_BC = SSD_G * SSD_N
SSD_CONV = SSD_INNER + 2 * SSD_BC
SSD_GW = SSD_HG * SSD_P

LANE = 128
MXU_WIDTH = 256

ROW_TILE = 512
ROW_TILE_WIDE = 1024
FF_TILE = 512
FF_SUB = MXU_WIDTH
ADA_COLS = 1024
INPROJ_COLS = 1024
MERGE_COLS = 512
OUTPROJ_COLS = 1024

IN_SPLITS = (("qkv_a", GDN_CONV), ("z_a", GDN_V), ("beta", GDN_H), ("dec", GDN_H),
             ("q_b", GLA_QK), ("k_b", GLA_QK), ("v_b", GLA_V), ("lr", GLA_RANK), ("r_b", GLA_V),
             ("z_c", SSD_INNER), ("xbc", SSD_CONV), ("dt", SSD_H), ("gates", 3 * D_MODEL))
MAIN_ORDER = ("qkv_a", "xbc", "z_c", "gates", "z_a", "q_b", "k_b", "v_b", "r_b")
SMALL_ORDER = ("beta", "dec", "lr", "dt")


def _layout():
    src, off = {}, 0
    for name, w in IN_SPLITS:
        src[name] = (off, off + w)
        off += w
    main, small, d = {}, {}, 0
    for name in MAIN_ORDER:
        w = src[name][1] - src[name][0]
        main[name] = d
        d += w
    total = d
    d = 0
    for name in SMALL_ORDER:
        small[name] = d
        d += src[name][1] - src[name][0]
    assert d <= LANE
    return src, main, small, total


W_IN_SRC, P_MAIN, P_SM, P_TOTAL = _layout()
P_QKV_A, P_XBC, P_Z_C, P_GATES, P_Z_A = (P_MAIN[k] for k in ("qkv_a", "xbc", "z_c", "gates", "z_a"))
P_Q_B, P_K_B, P_V_B, P_R_B = (P_MAIN[k] for k in ("q_b", "k_b", "v_b", "r_b"))
SM_BETA, SM_DEC, SM_LR, SM_DT = (P_SM[k] for k in SMALL_ORDER)

VMEM_LIMIT = 56 * 1024 * 1024


def _cp(sem):
    return pltpu.CompilerParams(dimension_semantics=sem, vmem_limit_bytes=VMEM_LIMIT)


def _blk(off, width):
    assert off % width == 0, (off, width)
    return off // width


def _sigmoid(x):
    return 0.5 + 0.5 * jnp.tanh(0.5 * x)


def _silu(x):
    h = 0.5 * x
    return h + h * jnp.tanh(h)


def _rms(x, w):
    return x * lax.rsqrt(jnp.mean(x * x, axis=-1, keepdims=True) + EPS) * w


def _dot(a, b):
    return jnp.dot(a, b, preferred_element_type=f32)


def _dot_nt(a, b):
    return lax.dot_general(a, b, (((1,), (1,)), ((), ())), preferred_element_type=f32)


def _dot_tn(a, b):
    return lax.dot_general(a, b, (((0,), (0,)), ((), ())), preferred_element_type=f32)


def _split(x):
    hi = x.astype(bf16)
    return hi, (x - hi.astype(f32)).astype(bf16)


def _dot3(a, b):
    return _dot(a[0], b[0]) + (_dot(a[0], b[1]) + _dot(a[1], b[0]))


def _dot_hi(a, b):
    return jnp.dot(a, b, precision=HI, preferred_element_type=f32)


def _tri_masks(c):
    row = lax.broadcasted_iota(jnp.int32, (c, c), 0)
    col = lax.broadcasted_iota(jnp.int32, (c, c), 1)
    return row >= col, row > col, row == col


SUB = 8


def _row_time(idx, c):
    n = c // SUB
    assert n & (n - 1) == 0, "chunk / SUB must be a power of two"
    sh = n.bit_length() - 1
    return ((idx & (n - 1)) << 3) | (idx >> sh)


def _tri_masks_residue(c, cols=None):
    cols = c if cols is None else cols
    row = _row_time(lax.broadcasted_iota(jnp.int32, (c, cols), 0), c)
    col = _row_time(lax.broadcasted_iota(jnp.int32, (c, cols), 1) & (c - 1), c)
    return row >= col, row > col, row == col


def _load_residue(ref, c):
    n = c // SUB
    return jnp.concatenate([ref[pl.ds(r, n, stride=SUB), :] for r in range(SUB)], axis=0)


def _store_natural(nat_ref, x, c, lo):
    n = c // SUB
    for j in range(x.shape[1] // LANE):
        for r in range(SUB):
            nat_ref[lo // LANE + j, pl.ds(r, n, stride=SUB), :] = x[r * n:(r + 1) * n, j * LANE:(j + 1) * LANE]


def _read_natural(nat_ref, lo, width):
    return jnp.concatenate([nat_ref[lo // LANE + j] for j in range(width // LANE)], axis=1)


def _scalar_vec(s):
    return jnp.full((1, 1), s, f32)


def _ada_kernel(c_ref, w_ref, b_ref, o_ref):
    s = _silu(c_ref[...]).astype(bf16)
    o_ref[...] = _dot(s, w_ref[...].astype(bf16)) + b_ref[...]


def _ada_mod(c_all, w_ada, b_ada):
    nl, dm, n = w_ada.shape
    r = c_all.shape[0]
    tn = ADA_COLS
    return pl.pallas_call(
        _ada_kernel,
        grid=(nl, n // tn),
        in_specs=[pl.BlockSpec((r, dm), lambda l, j: (0, 0)),
                  pl.BlockSpec((None, dm, tn), lambda l, j: (l, 0, j)),
                  pl.BlockSpec((None, 1, tn), lambda l, j: (l, 0, j))],
        out_specs=pl.BlockSpec((None, r, tn), lambda l, j: (l, 0, j)),
        out_shape=jax.ShapeDtypeStruct((nl, r, n), f32),
        compiler_params=_cp(("arbitrary", "arbitrary")),
        name="ada_mod",
    )(c_all, w_ada, b_ada.reshape(nl, 1, n))


def _mod_spec(per_row, tm, chunk):
    if per_row:
        return pl.BlockSpec((None, tm, D_MODEL), lambda b, i, j: (b, i, chunk))
    return pl.BlockSpec((None, 1, D_MODEL), lambda b, i, j: (b, 0, chunk))


def _ffn_prologue(x_ref, sh_ref, sc_ref, nw_ref, h_ref, acc_ref):
    y = _rms(x_ref[...], nw_ref[...])
    h_ref[...] = (y * (1.0 + sc_ref[...]) + sh_ref[...]).astype(bf16)
    acc_ref[...] = jnp.zeros_like(acc_ref)


def _ffn_accumulate(h_ref, acc_ref, wg, wu, wd):
    h = h_ref[...]
    width = wg.shape[1]
    sub = FF_SUB if width % FF_SUB == 0 else width
    parts = [(_dot(h, wg[:, s:s + sub]), _dot(h, wu[:, s:s + sub])) for s in range(0, width, sub)]
    acc = acc_ref[...]
    for i, (g, u) in enumerate(parts):
        acc = acc + _dot((_silu(g) * u).astype(bf16), wd[i * sub:(i + 1) * sub, :])
    acc_ref[...] = acc


def _ffn_epilogue(x_ref, gt_ref, fw_ref, acc_ref, o_ref, final):
    y = x_ref[...] + 0.5 * gt_ref[...] * acc_ref[...]
    if final:
        y = _rms(y, fw_ref[...])
    o_ref[...] = y


def _ffn_kernel(x_ref, sh_ref, sc_ref, gt_ref, nw_ref, fw_ref, wg_ref, wu_ref, wd_ref, *rest, final, has_tail):
    if has_tail:
        wgt_ref, wut_ref, wdt_ref, o_ref, h_ref, acc_ref = rest
    else:
        o_ref, h_ref, acc_ref = rest
    f = pl.program_id(2)
    last = pl.num_programs(2) - 1
    pl.when(f == 0)(lambda: _ffn_prologue(x_ref, sh_ref, sc_ref, nw_ref, h_ref, acc_ref))
    if has_tail:
        pl.when(f < last)(lambda: _ffn_accumulate(h_ref, acc_ref, wg_ref, wu_ref, wd_ref))
        pl.when(f == last)(lambda: _ffn_accumulate(h_ref, acc_ref, wgt_ref, wut_ref, wdt_ref))
    else:
        _ffn_accumulate(h_ref, acc_ref, wg_ref, wu_ref, wd_ref)
    pl.when(f == last)(lambda: _ffn_epilogue(x_ref, gt_ref, fw_ref, acc_ref, o_ref, final))


def _ffn(x, mod, k0, nw, w, fw, per_row, tm, final):
    bx, tx, dm = x.shape
    tf = FF_TILE
    n_full = w[0].shape[1] // tf
    has_tail = len(w) > 3
    full = lambda f: jnp.minimum(f, n_full - 1)
    w_specs = [pl.BlockSpec((dm, tf), lambda b, i, f: (0, full(f))),
               pl.BlockSpec((dm, tf), lambda b, i, f: (0, full(f))),
               pl.BlockSpec((tf, dm), lambda b, i, f: (full(f), 0))]
    if has_tail:
        ft = w[3].shape[1]
        w_specs += [pl.BlockSpec((dm, ft), lambda b, i, f: (0, 0)),
                    pl.BlockSpec((dm, ft), lambda b, i, f: (0, 0)),
                    pl.BlockSpec((ft, dm), lambda b, i, f: (0, 0))]
    return pl.pallas_call(
        functools.partial(_ffn_kernel, final=final, has_tail=has_tail),
        grid=(bx, tx // tm, n_full + int(has_tail)),
        in_specs=[pl.BlockSpec((None, tm, dm), lambda b, i, f: (b, i, 0)),
                  _mod_spec(per_row, tm, k0), _mod_spec(per_row, tm, k0 + 1), _mod_spec(per_row, tm, k0 + 2),
                  pl.BlockSpec((1, dm), lambda b, i, f: (0, 0)),
                  pl.BlockSpec((1, dm), lambda b, i, f: (0, 0))] + w_specs,
        out_specs=pl.BlockSpec((None, tm, dm), lambda b, i, f: (b, i, 0)),
        out_shape=jax.ShapeDtypeStruct(x.shape, f32),
        scratch_shapes=[pltpu.VMEM((tm, dm), bf16), pltpu.VMEM((tm, dm), f32)],
        compiler_params=_cp(("parallel", "parallel", "arbitrary")),
        name="ffn",
    )(x, mod, mod, mod, nw, fw, *w)


FF_CAST_TILE = 256


def _ffn_cast_kernel(x_ref, sh_ref, sc_ref, gt_ref, nw_ref, fw_ref, wg_ref, wu_ref, wd_ref, *rest, final, has_tail):
    if has_tail:
        wgt_ref, wut_ref, wdt_ref, o_ref, wgb_ref, wub_ref, wdb_ref, wgtb_ref, wutb_ref, wdtb_ref, h_ref, acc_ref = rest
    else:
        o_ref, wgb_ref, wub_ref, wdb_ref, h_ref, acc_ref = rest
    f = pl.program_id(2)
    last = pl.num_programs(2) - 1
    pl.when(f == 0)(lambda: _ffn_prologue(x_ref, sh_ref, sc_ref, nw_ref, h_ref, acc_ref))

    def cast_accumulate(src, dst):
        for s_ref, d_ref in zip(src, dst):
            d_ref[...] = s_ref[...].reshape(d_ref.shape).astype(bf16)
        _ffn_accumulate(h_ref, acc_ref, *dst)

    if has_tail:
        pl.when(f < last)(lambda: cast_accumulate((wg_ref, wu_ref, wd_ref), (wgb_ref, wub_ref, wdb_ref)))
        pl.when(f == last)(lambda: cast_accumulate((wgt_ref, wut_ref, wdt_ref), (wgtb_ref, wutb_ref, wdtb_ref)))
    else:
        cast_accumulate((wg_ref, wu_ref, wd_ref), (wgb_ref, wub_ref, wdb_ref))
    pl.when(f == last)(lambda: _ffn_epilogue(x_ref, gt_ref, fw_ref, acc_ref, o_ref, final))


def _ffn_cast(x, mod, k0, nw, wg, wu, wd, l, fw, final):
    bx, tx, dm = x.shape
    assert bx == 1
    f_all = wg.shape[2]
    tf = FF_CAST_TILE
    cut = f_all - f_all % FF_TILE
    n_full = cut // tf
    has_tail = cut < f_all
    full = lambda f: jnp.minimum(f, n_full - 1)
    w_in_specs = [pl.BlockSpec((None, dm, tf), lambda b, i, f: (l, 0, full(f))),
                  pl.BlockSpec((None, dm, tf), lambda b, i, f: (l, 0, full(f))),
                  pl.BlockSpec((None, tf, dm), lambda b, i, f: (l, full(f), 0))]
    w_out_specs = [pl.BlockSpec((dm, tf), lambda b, i, f: (0, full(f))),
                   pl.BlockSpec((dm, tf), lambda b, i, f: (0, full(f))),
                   pl.BlockSpec((tf, dm), lambda b, i, f: (full(f), 0))]
    w_out_shapes = [jax.ShapeDtypeStruct((dm, cut), bf16), jax.ShapeDtypeStruct((dm, cut), bf16),
                    jax.ShapeDtypeStruct((cut, dm), bf16)]
    args = [wg, wu, wd]
    if has_tail:
        ft = f_all - cut
        args += [wg, wu, wd]
        el = lambda *shape: tuple(pl.Element(s) for s in shape)
        w_in_specs += [pl.BlockSpec(el(1, dm, ft), lambda b, i, f: (l, 0, cut)),
                       pl.BlockSpec(el(1, dm, ft), lambda b, i, f: (l, 0, cut)),
                       pl.BlockSpec(el(1, ft, dm), lambda b, i, f: (l, cut, 0))]
        w_out_specs += [pl.BlockSpec((dm, ft), lambda b, i, f: (0, 0)), pl.BlockSpec((dm, ft), lambda b, i, f: (0, 0)),
                        pl.BlockSpec((ft, dm), lambda b, i, f: (0, 0))]
        w_out_shapes += [jax.ShapeDtypeStruct((dm, ft), bf16), jax.ShapeDtypeStruct((dm, ft), bf16),
                         jax.ShapeDtypeStruct((ft, dm), bf16)]
    out = pl.pallas_call(
        functools.partial(_ffn_cast_kernel, final=final, has_tail=has_tail),
        grid=(1, 1, n_full + int(has_tail)),
        in_specs=[pl.BlockSpec((None, tx, dm), lambda b, i, f: (b, i, 0)),
                  _mod_spec(True, tx, k0), _mod_spec(True, tx, k0 + 1), _mod_spec(True, tx, k0 + 2),
                  pl.BlockSpec((1, dm), lambda b, i, f: (0, 0)),
                  pl.BlockSpec((1, dm), lambda b, i, f: (0, 0))] + w_in_specs,
        out_specs=[pl.BlockSpec((None, tx, dm), lambda b, i, f: (b, i, 0))] + w_out_specs,
        out_shape=[jax.ShapeDtypeStruct(x.shape, f32)] + w_out_shapes,
        scratch_shapes=[pltpu.VMEM((tx, dm), bf16), pltpu.VMEM((tx, dm), f32)],
        compiler_params=_cp(("arbitrary", "arbitrary", "arbitrary")),
        name="ffn_cast",
    )(x, mod, mod, mod, nw, fw, *args)
    return out[0], tuple(out[1:])


def _inproj_kernel(x_ref, sh_ref, sc_ref, nw_ref, w_ref, ws_ref, o_ref, os_ref, h_ref):
    @pl.when(pl.program_id(2) == 0)
    def _():
        y = _rms(x_ref[...], nw_ref[...])
        h = (y * (1.0 + sc_ref[...]) + sh_ref[...]).astype(bf16)
        h_ref[...] = h
        os_ref[...] = _dot_nt(h, ws_ref[...].astype(bf16))

    o_ref[...] = _dot_nt(h_ref[...], w_ref[...]).astype(o_ref.dtype)


def _inproj(x, mod, nw, w, ws, l, per_row, tm, out_dtype):
    bx, tx, dm = x.shape
    n = w.shape[1]
    tn = INPROJ_COLS
    return pl.pallas_call(
        _inproj_kernel,
        grid=(bx, tx // tm, n // tn),
        in_specs=[pl.BlockSpec((None, tm, dm), lambda b, i, j: (b, i, 0)),
                  _mod_spec(per_row, tm, 3), _mod_spec(per_row, tm, 4),
                  pl.BlockSpec((1, dm), lambda b, i, j: (0, 0)),
                  pl.BlockSpec((None, tn, dm), lambda b, i, j: (l, j, 0)),
                  pl.BlockSpec((None, LANE, dm), lambda b, i, j: (l, 0, 0))],
        out_specs=[pl.BlockSpec((None, tm, tn), lambda b, i, j: (b, i, j)),
                   pl.BlockSpec((None, tm, LANE), lambda b, i, j: (b, i, 0))],
        out_shape=[jax.ShapeDtypeStruct((bx, tx, n), out_dtype),
                   jax.ShapeDtypeStruct((bx, tx, LANE), f32)],
        scratch_shapes=[pltpu.VMEM((tm, dm), bf16)],
        compiler_params=_cp(("parallel", "parallel", "arbitrary")),
        name="inproj",
    )(x, mod, mod, nw, w, ws)


CONV_AT = SUB
CONV_HIST = CONV_AT - (CONV_W - 1)


def _conv_load(ext_ref, x_ref, cst_ref, first):
    c = x_ref.shape[0]
    nblk = ext_ref.shape[0]

    @pl.when(first)
    def _():
        for j in range(nblk):
            ext_ref[j, pl.ds(CONV_HIST, CONV_W - 1), :] = cst_ref[:, j * LANE:(j + 1) * LANE]

    for j in range(nblk):
        ext_ref[j, pl.ds(CONV_AT, c), :] = x_ref[:, j * LANE:(j + 1) * LANE].astype(f32)


def _conv_cols(ext_ref, w_ref, b_ref, lo, width, c):
    n = c // SUB
    cols = []
    for blk in range(lo // LANE, (lo + width) // LANE):
        sl = pl.ds(blk * LANE, LANE)
        w = w_ref[:, sl]
        taps = [ext_ref[blk, pl.ds(CONV_HIST + k, n, stride=SUB), :] for k in range(SUB + CONV_W - 1)]
        ys = []
        for r in range(SUB):
            y = taps[r] * w[0:1]
            for j in range(1, CONV_W):
                y = y + taps[r + j] * w[j:j + 1]
            ys.append(y)
        y = jnp.concatenate(ys, axis=0)
        if b_ref is not None:
            y = y + b_ref[:, sl]
        cols.append(_silu(y))
    return cols[0] if len(cols) == 1 else jnp.concatenate(cols, axis=1)


def _conv_carry(ext_ref, c):
    for j in range(ext_ref.shape[0]):
        ext_ref[j, pl.ds(CONV_HIST, CONV_W - 1), :] = ext_ref[j, pl.ds(CONV_HIST + c, CONV_W - 1), :]


def _cumsum_rows(x, tri):
    return _dot_hi(jnp.where(tri, 1.0, 0.0).astype(f32), x)


def _gdn_kernel(qkv_ref, z_ref, sm_ref, cst_ref, cw_ref, alog_ref, dtb_ref, nw_ref, s0_ref,
                o_ref, s_ref, ext_ref, nat_ref):
    c = qkv_ref.shape[0]
    first = pl.program_id(1) == 0

    @pl.when(first)
    def _():
        s_ref[...] = s0_ref[...]

    _conv_load(ext_ref, qkv_ref, cst_ref, first)

    sm = _load_residue(sm_ref, c)
    beta_l = _sigmoid(sm)
    g_l = -jnp.exp(alog_ref[...]) * jax.nn.softplus(sm + dtb_ref[...])
    tri, strict, eye = _tri_masks_residue(c)
    gc_all = _cumsum_rows(g_l, tri)
    gc_t = gc_all.T
    eye_f = jnp.where(eye, 1.0, 0.0).astype(f32)

    hs = range(GDN_H)
    qb, kb, ks, decay, eg, gcs, nm, rhs = [], [], [], [], [], [], [], []
    for h in hs:
        q = _conv_cols(ext_ref, cw_ref, None, h * GDN_DK, GDN_DK, c)
        k = _conv_cols(ext_ref, cw_ref, None, GDN_QK + h * GDN_DK, GDN_DK, c)
        v = _conv_cols(ext_ref, cw_ref, None, 2 * GDN_QK + h * GDN_DV, GDN_DV, c)
        q = q * lax.rsqrt(jnp.sum(q * q, axis=-1, keepdims=True) + EPS) * (GDN_DK ** -0.5)
        k = k * lax.rsqrt(jnp.sum(k * k, axis=-1, keepdims=True) + EPS)
        beta = beta_l[:, SM_BETA + h:SM_BETA + h + 1]
        gc = gc_all[:, SM_DEC + h:SM_DEC + h + 1]
        d = jnp.exp(jnp.where(tri, gc - gc_t[SM_DEC + h:SM_DEC + h + 1, :], -jnp.inf))
        e = jnp.exp(gc)
        qb.append(q.astype(bf16))
        kb.append(k.astype(bf16))
        ks.append(k)
        decay.append(d)
        eg.append(e)
        gcs.append(gc)
        nm.append(jnp.where(strict, beta * d * _dot_nt(kb[h], kb[h]), 0.0))
        rhs.append(jnp.concatenate([v * beta, k * (beta * e)], axis=1))
        yield

    assert c // GDN_BLK <= 4 and GDN_BLK & (GDN_BLK - 1) == 0
    blk_sh = GDN_BLK.bit_length() - 1
    same_blk = (_row_time(lax.broadcasted_iota(jnp.int32, (c, c), 0), c) >> blk_sh
                == _row_time(lax.broadcasted_iota(jnp.int32, (c, c), 1), c) >> blk_sh)
    nd = [jnp.where(same_blk, nm[h], 0.0) for h in hs]
    lo_s = [_split(nm[h] - nd[h]) for h in hs]
    dinv = [eye_f - nd[h] for h in hs]
    pw_s = [_split(nd[h]) for h in hs]
    for i in range(GDN_BLK.bit_length() - 2):
        pw_s = [_split(_dot3(pw_s[h], pw_s[h])) for h in hs]
        yield
        dinv = [dinv[h] + _dot3(_split(dinv[h]), pw_s[h]) for h in hs]
        yield
    dinv_s = [_split(dinv[h]) for h in hs]
    m = [_dot3(dinv_s[h], lo_s[h]) for h in hs]
    m_s = [_split(m[h]) for h in hs]
    yield
    m2 = [_dot3(m_s[h], m_s[h]) for h in hs]
    yield
    t1 = [eye_f - m[h] + m2[h] - _dot3(m_s[h], _split(m2[h])) for h in hs]
    yield
    inv = [_dot3(_split(t1[h]), dinv_s[h]) for h in hs]
    yield
    sol = []
    for h in hs:
        inv_hi, inv_lo = _split(inv[h])
        rb = rhs[h].astype(bf16)
        sol.append(_dot(inv_hi, rb) + _dot(inv_lo, rb))
    yield

    s = [s_ref[h] for h in hs]
    sb = [s[h].astype(bf16) for h in hs]
    wb = [(sol[h][:, 0:GDN_DV] - _dot(sol[h][:, GDN_DV:].astype(bf16), sb[h])).astype(bf16) for h in hs]
    yield
    qk = [(_dot_nt(qb[h], kb[h]) * decay[h]).astype(bf16) for h in hs]
    yield
    o = [_dot(qb[h], sb[h]) * eg[h] + _dot(qk[h], wb[h]) for h in hs]
    yield
    for h in hs:
        g_last = gcs[h][c - 1:c]
        kd = (ks[h] * jnp.exp(g_last - gcs[h])).astype(bf16)
        s_ref[h] = s[h] * jnp.exp(g_last) + _dot_tn(kd, wb[h])
    yield
    for h in hs:
        sl = slice(h * GDN_DV, (h + 1) * GDN_DV)
        _store_natural(nat_ref, _rms(o[h], nw_ref[...]), c, h * GDN_DV)
        o_ref[:, sl] = (_read_natural(nat_ref, h * GDN_DV, GDN_DV) * _silu(z_ref[:, sl].astype(f32))).astype(bf16)
        yield

    _conv_carry(ext_ref, c)


def _tok(width, off):
    return pl.BlockSpec((None, CHUNK, width), lambda bi, ti: (bi, ti, _blk(off, width)))


def _per_seq(*shape):
    return pl.BlockSpec((None,) + shape, lambda bi, ti: (bi,) + (0,) * len(shape))


def _const(*shape):
    return pl.BlockSpec(shape, lambda bi, ti: (0,) * len(shape))


N_GDN_IN, N_GLA_IN, N_SSD_IN = 9, 9, 11
_DONE = object()


def _mixers_kernel(*refs):
    i0, i1, i2 = N_GDN_IN, N_GDN_IN + N_GLA_IN, N_GDN_IN + N_GLA_IN + N_SSD_IN
    gdn_in, gla_in, ssd_in = refs[:i0], refs[i0:i1], refs[i1:i2]
    oa_ref, sa_ref, ob_ref, sb_ref, oc_ref, sc_ref, ext_a_ref, st_b_ref, ext_c_ref, nat_a_ref, nat_c_ref = refs[i2:]
    live = [_gdn_kernel(*gdn_in, oa_ref, sa_ref, ext_a_ref, nat_a_ref),
            _gla_kernel(*gla_in, ob_ref, sb_ref, st_b_ref),
            _ssd_kernel(*ssd_in, oc_ref, sc_ref, ext_c_ref, nat_c_ref)]
    while live:
        for gen in list(live):
            if next(gen, _DONE) is _DONE:
                live.remove(gen)


GLA_SUB = 16
GDN_BLK = 16


def _gla_kernel(q_ref, k_ref, v_ref, r_ref, sm_ref, wgate_ref, bgate_ref, nw_ref, s0_ref,
                o_ref, so_ref, st_ref):
    c = q_ref.shape[0]
    ti = pl.program_id(1)

    @pl.when(ti == 0)
    def _():
        for h in range(GLA_H):
            st_ref[h] = s0_ref[h].T

    sm = sm_ref[...]
    lane = lax.broadcasted_iota(jnp.int32, sm.shape, 1)
    lr = jnp.where((lane >= SM_LR) & (lane < SM_LR + GLA_RANK), sm, 0.0).astype(bf16)
    la = jax.nn.log_sigmoid(_dot(lr, wgate_ref[...]) + bgate_ref[...]) / GLA_TAU
    tri, _, _ = _tri_masks(c)
    b_all = _cumsum_rows(la, tri)

    col = lax.broadcasted_iota(jnp.int32, (GLA_SUB, c), 1)
    row = lax.broadcasted_iota(jnp.int32, (GLA_SUB, c), 0)
    hs = range(GLA_H)
    qs, ks, vbs, bs, atts = [], [], [], [], []
    for h in hs:
        sk = slice(h * GLA_DK, (h + 1) * GLA_DK)
        q = q_ref[:, sk].astype(f32) * (GLA_DK ** -0.5)
        k = k_ref[:, sk].astype(f32)
        b = b_all[:, sk]
        qs.append(q)
        ks.append(k)
        bs.append(b)
        vbs.append(v_ref[:, h * GLA_DV:(h + 1) * GLA_DV])
        att_rows = []
        for i in range(c // GLA_SUB):
            lo = i * GLA_SUB
            b_i = b[lo:lo + GLA_SUB]
            q_i = q[lo:lo + GLA_SUB]
            b_top = b[lo:lo + 1]
            att = jnp.zeros((GLA_SUB, c), f32)
            if i > 0:
                q_t = (q_i * jnp.exp(b_i - b_top)).astype(bf16)
                k_t = (k * jnp.exp(jnp.minimum(b_top - b, 0.0))).astype(bf16)
                att = jnp.where(col < lo, _dot_nt(q_t, k_t), 0.0)
            for sl in range(GLA_SUB):
                sidx = lo + sl
                e = jnp.exp(jnp.minimum(b_i - b[sidx:sidx + 1], 0.0))
                p = jnp.sum(q_i * k[sidx:sidx + 1] * e, axis=1, keepdims=True)
                att = jnp.where((col == sidx) & (row >= sl), p, att)
            att_rows.append(att)
            yield
        atts.append(jnp.concatenate(att_rows, axis=0).astype(bf16))

    sts = [st_ref[h] for h in hs]
    os_ = [_dot(atts[h], vbs[h]) + _dot_nt((qs[h] * jnp.exp(bs[h])).astype(bf16), sts[h].astype(bf16))
           for h in hs]
    yield
    for h in hs:
        b_last = bs[h][c - 1:c]
        kd = (ks[h] * jnp.exp(b_last - bs[h])).astype(bf16)
        st_ref[h] = sts[h] * jnp.exp(b_last) + _dot_tn(vbs[h], kd)
        yield
    for h in hs:
        sv = slice(h * GLA_DV, (h + 1) * GLA_DV)
        o_ref[:, sv] = (_rms(os_[h], nw_ref[...]) * _silu(r_ref[:, sv].astype(f32))).astype(bf16)
        yield

    @pl.when(ti == pl.num_programs(1) - 1)
    def _():
        for h in hs:
            so_ref[h] = st_ref[h].T


def _ssd_kernel(z_ref, xbc_ref, sm_ref, cst_ref, cw_ref, cb_ref, avec_ref, dtb_ref, dvec_ref, nw_ref, h0_ref,
                o_ref, h_ref, ext_ref, nat_ref):
    c = xbc_ref.shape[0]
    assert c == SSD_P, "head pairs share a lane tile: the chunk must be as wide as a head"
    first = pl.program_id(1) == 0

    @pl.when(first)
    def _():
        h_ref[...] = h0_ref[...]

    _conv_load(ext_ref, xbc_ref, cst_ref, first)

    dt_l = jax.nn.softplus(_load_residue(sm_ref, c) + dtb_ref[...])
    tri, _, _ = _tri_masks_residue(c)
    ac_all = _cumsum_rows(dt_l * avec_ref[...], tri)
    ac_t2 = jnp.concatenate([ac_all, ac_all], axis=0).T

    lane = lax.broadcasted_iota(jnp.int32, (c, 2 * SSD_P), 1)
    lo_half = lane < SSD_P
    tri2, _, _ = _tri_masks_residue(c, 2 * SSD_P)
    lane1 = lane[0:1]
    gs_ = range(SSD_G)
    ps_ = range(SSD_HG // 2)

    xs_, bmb, hg, cb2, ch = [], [], [], [], []
    for g in gs_:
        xs_.append(_conv_cols(ext_ref, cw_ref, cb_ref, g * SSD_GW, SSD_GW, c))
        bmb.append(_conv_cols(ext_ref, cw_ref, cb_ref, SSD_INNER + g * SSD_N, SSD_N, c).astype(bf16))
        cmb = _conv_cols(ext_ref, cw_ref, cb_ref, SSD_INNER + SSD_BC + g * SSD_N, SSD_N, c).astype(bf16)
        hg.append(h_ref[pl.ds(g * SSD_HG, SSD_HG)].reshape(SSD_GW, SSD_N))
        cb2.append(_dot_nt(cmb, jnp.concatenate([bmb[g], bmb[g]], axis=0)))
        ch.append(_dot_nt(cmb, hg[g].astype(bf16)))
        yield

    m2, rhs, e_col, xsc = {}, {}, {}, {}
    for g in gs_:
        for p in ps_:
            l0 = SM_DT + g * SSD_HG + 2 * p
            ps = slice(p * 2 * SSD_P, (p + 1) * 2 * SSD_P)
            ac_col = jnp.where(lo_half, ac_all[:, l0:l0 + 1], ac_all[:, l0 + 1:l0 + 2])
            ac_row = jnp.where(lane1 < SSD_P, ac_t2[l0:l0 + 1, :], ac_t2[l0 + 1:l0 + 2, :])
            decay = jnp.exp(jnp.where(tri2, ac_col - ac_row, -jnp.inf))
            m2[g, p] = (cb2[g] * decay).astype(bf16)
            dt2 = jnp.where(lo_half, dt_l[:, l0:l0 + 1], dt_l[:, l0 + 1:l0 + 2])
            xdt = xs_[g][:, ps] * dt2
            rhs[g, p] = jnp.concatenate([jnp.where(lo_half, xdt, 0.0), jnp.where(lo_half, 0.0, xdt)],
                                        axis=0).astype(bf16)
            e_col[g, p] = jnp.exp(ac_col)
            xsc[g, p] = xdt * jnp.exp(ac_col[c - 1:c] - ac_col)
            yield
    y2 = {gp: _dot(m2[gp], rhs[gp]) for gp in m2}
    yield

    for g in gs_:
        xsg = jnp.concatenate([xsc[g, p] for p in ps_], axis=1).astype(bf16)
        dh = _dot_tn(xsg, bmb[g])
        for j in range(SSD_HG):
            ln = SM_DT + g * SSD_HG + j
            h_ref[g * SSD_HG + j] = (hg[g][j * SSD_P:(j + 1) * SSD_P] * jnp.exp(ac_all[c - 1:c, ln:ln + 1])
                                     + dh[j * SSD_P:(j + 1) * SSD_P])
        yield
    for g in gs_:
        gs = slice(g * SSD_GW, (g + 1) * SSD_GW)
        y = jnp.concatenate([y2[g, p] + ch[g][:, p * 2 * SSD_P:(p + 1) * 2 * SSD_P] * e_col[g, p] for p in ps_],
                            axis=1)
        _store_natural(nat_ref, y + dvec_ref[:, gs] * xs_[g], c, g * SSD_GW)
        y = _read_natural(nat_ref, g * SSD_GW, SSD_GW) * _silu(z_ref[:, gs].astype(f32))
        o_ref[:, gs] = _rms(y, nw_ref[:, gs]).astype(bf16)
        yield

    _conv_carry(ext_ref, c)


STEP_B = 8


def _conv_step(x_ref, c_ref, w_ref):
    w = w_ref[...]
    return c_ref[0] * w[0:1] + c_ref[1] * w[1:2] + c_ref[2] * w[2:3] + x_ref[...] * w[3:4]


def _expand_matrix(nb, width):
    r = lax.broadcasted_iota(jnp.int32, (nb, nb * width), 0)
    c = lax.broadcasted_iota(jnp.int32, (nb, nb * width), 1)
    return jnp.where((c >= r * width) & (c < (r + 1) * width), 1.0, 0.0).astype(bf16)


def _bcast_cols(x, e_mat):
    hi, lo = _split(x)
    return _dot_tn(hi, e_mat) + _dot_tn(lo, e_mat)


def _state_call(kernel, grid, in_specs, out_specs, out_shape, scratch, sem, name, args, prev_state):
    aliases = {}
    if prev_state is not None:
        in_specs = in_specs + [pl.BlockSpec(memory_space=pl.ANY)]
        args = args + (prev_state,)
        aliases = {len(args) - 1: len(out_shape) - 1}
        kernel = functools.partial(_drop_alias_ref, kernel, len(args) - 1)
    return pl.pallas_call(kernel, grid=grid, in_specs=in_specs, out_specs=out_specs, out_shape=out_shape,
                          scratch_shapes=scratch, input_output_aliases=aliases,
                          compiler_params=_cp(sem), name=name)(*args)


def _drop_alias_ref(kernel, idx, *refs):
    return kernel(*refs[:idx], *refs[idx + 1:])


def _gdn_step_kernel(alog_ref, dtb_ref, q_ref, k_ref, v_ref, z_ref, sm_ref, cq_ref, ck_ref, cv_ref,
                     wq_ref, wk_ref, wv_ref, nw_ref, s_ref, o_ref, so_ref):
    nb = q_ref.shape[0]
    qa = _silu(_conv_step(q_ref, cq_ref, wq_ref))
    ka = _silu(_conv_step(k_ref, ck_ref, wk_ref))
    va = _silu(_conv_step(v_ref, cv_ref, wv_ref))
    sm = sm_ref[...]
    z = z_ref[...]
    e_mat = _expand_matrix(nb, GDN_DV)
    for h in range(GDN_H):
        sl = slice(h * GDN_DK, (h + 1) * GDN_DK)
        q = qa[:, sl]
        k = ka[:, sl]
        v = va[:, sl]
        q = q * lax.rsqrt(jnp.sum(q * q, axis=-1, keepdims=True) + EPS) * (GDN_DK ** -0.5)
        k = k * lax.rsqrt(jnp.sum(k * k, axis=-1, keepdims=True) + EPS)
        beta = _sigmoid(sm[:, SM_BETA + h:SM_BETA + h + 1])
        g = -jnp.exp(_scalar_vec(alog_ref[h])) * jax.nn.softplus(sm[:, SM_DEC + h:SM_DEC + h + 1] + dtb_ref[h])
        eg = jnp.exp(g)
        qk = jnp.sum(q * k, axis=-1, keepdims=True)
        kb = _bcast_cols(k, e_mat)
        qb = _bcast_cols(q, e_mat)
        blk = lambda m, b: m[:, b * GDN_DV:(b + 1) * GDN_DV]
        ks = jnp.concatenate([jnp.sum(s_ref[b, h] * blk(kb, b), axis=0, keepdims=True) for b in range(nb)], axis=0)
        qs = jnp.concatenate([jnp.sum(s_ref[b, h] * blk(qb, b), axis=0, keepdims=True) for b in range(nb)], axis=0)
        w = beta * v - (beta * eg) * ks
        o = qs * eg + qk * w
        for b in range(nb):
            so_ref[b, h] = s_ref[b, h] * eg[b:b + 1] + blk(kb, b) * w[b:b + 1]
        o = _rms(o, nw_ref[...]) * _silu(z[:, sl])
        o_ref[:, sl] = o.astype(bf16)


def _gdn_step(proj, small, conv_t, states, prev, l, conv_w, a_log, dt_bias, norm_w):
    n = proj.shape[0]
    nb = STEP_B
    blk = lambda off: pl.BlockSpec((nb, GDN_QK), lambda i: (i, _blk(off, GDN_QK)))
    cst = lambda j: pl.BlockSpec((CONV_W - 1, nb, GDN_QK), lambda i: (0, i, j))
    cw = lambda j: pl.BlockSpec((CONV_W, GDN_QK), lambda i: (0, j))
    smem = pl.BlockSpec(memory_space=pltpu.SMEM)
    st = pl.BlockSpec((None, nb, GDN_H, GDN_DK, GDN_DV), lambda i: (l, i, 0, 0, 0))
    return _state_call(
        _gdn_step_kernel, (n // nb,),
        [smem, smem, blk(P_QKV_A), blk(P_QKV_A + GDN_QK), blk(P_QKV_A + 2 * GDN_QK), blk(P_Z_A),
         pl.BlockSpec((nb, LANE), lambda i: (i, 0)),
         cst(0), cst(1), cst(2), cw(0), cw(1), cw(2),
         pl.BlockSpec((1, GDN_DV), lambda i: (0, 0)), st],
        [pl.BlockSpec((nb, GDN_V), lambda i: (i, 0)), st],
        [jax.ShapeDtypeStruct((n, GDN_V), bf16), jax.ShapeDtypeStruct(states.shape, f32)],
        [], ("parallel",), "gdn_step",
        (a_log, dt_bias, proj, proj, proj, proj, small, conv_t, conv_t, conv_t, conv_w, conv_w, conv_w, norm_w, states),
        prev)


def _gla_step_kernel(q_ref, k_ref, v_ref, r_ref, sm_ref, wgate_ref, bgate_ref, nw_ref, s_ref,
                     o_ref, so_ref):
    nb = q_ref.shape[0]
    sm = sm_ref[...]
    lane = lax.broadcasted_iota(jnp.int32, sm.shape, 1)
    lr = jnp.where((lane >= SM_LR) & (lane < SM_LR + GLA_RANK), sm, 0.0).astype(bf16)
    la_all = jax.nn.log_sigmoid(_dot(lr, wgate_ref[...]) + bgate_ref[...]) / GLA_TAU
    qa = q_ref[...] * (GLA_DK ** -0.5)
    ka = k_ref[...]
    va = v_ref[...]
    r = r_ref[...]
    for h in range(GLA_H):
        sk = slice(h * GLA_DK, (h + 1) * GLA_DK)
        sv = slice(h * GLA_DV, (h + 1) * GLA_DV)
        q = qa[:, sk]
        k = ka[:, sk]
        v = va[:, sv]
        e = jnp.exp(la_all[:, sk])
        qk = jnp.sum(q * k, axis=-1, keepdims=True)
        e_t = e.T
        k_t = k.T
        qe_t = (q * e).T
        rows = []
        for b in range(nb):
            s = s_ref[b, h]
            vrow = v[b:b + 1]
            rows.append(qk[b:b + 1] * vrow + jnp.sum(s * qe_t[:, b:b + 1], axis=0, keepdims=True))
            so_ref[b, h] = s * e_t[:, b:b + 1] + k_t[:, b:b + 1] * vrow
        o = jnp.concatenate(rows, axis=0)
        o = _rms(o, nw_ref[...]) * _silu(r[:, sv])
        o_ref[:, sv] = o.astype(bf16)


def _gla_step(proj, small, states, prev, l, wgate_pad, bgate, norm_w):
    n = proj.shape[0]
    nb = STEP_B
    st = pl.BlockSpec((None, nb, GLA_H, GLA_DK, GLA_DV), lambda i: (l, i, 0, 0, 0))
    return _state_call(
        _gla_step_kernel, (n // nb,),
        [pl.BlockSpec((nb, GLA_QK), lambda i: (i, _blk(P_Q_B, GLA_QK))),
         pl.BlockSpec((nb, GLA_QK), lambda i: (i, _blk(P_K_B, GLA_QK))),
         pl.BlockSpec((nb, GLA_V), lambda i: (i, _blk(P_V_B, GLA_V))),
         pl.BlockSpec((nb, GLA_V), lambda i: (i, _blk(P_R_B, GLA_V))),
         pl.BlockSpec((nb, LANE), lambda i: (i, 0)),
         pl.BlockSpec((LANE, GLA_QK), lambda i: (0, 0)),
         pl.BlockSpec((1, GLA_QK), lambda i: (0, 0)),
         pl.BlockSpec((1, GLA_DV), lambda i: (0, 0)), st],
        [pl.BlockSpec((nb, GLA_V), lambda i: (i, 0)), st],
        [jax.ShapeDtypeStruct((n, GLA_V), bf16), jax.ShapeDtypeStruct(states.shape, f32)],
        [], ("parallel",), "gla_step",
        (proj, proj, proj, proj, small, wgate_pad, bgate, norm_w, states),
        prev)


def _ssd_step_kernel(alog_ref, dtb_ref, z_ref, xbc_ref, sm_ref, cst_ref, cw_ref, cb_ref, dvec_ref, nw_ref, h_ref,
                     o_ref, ho_ref):
    nb = xbc_ref.shape[0]
    xbc = _silu(_conv_step(xbc_ref, cst_ref, cw_ref) + cb_ref[...])
    sm_t = sm_ref[...].T
    e_mat = _expand_matrix(nb, SSD_N)
    lane_b = lax.broadcasted_iota(jnp.int32, (SSD_P, nb), 1)
    for g in range(SSD_G):
        gs = slice(g * SSD_GW, (g + 1) * SSD_GW)
        bm = xbc[:, SSD_INNER + g * SSD_N:SSD_INNER + (g + 1) * SSD_N]
        cm_t = xbc[:, SSD_INNER + SSD_BC + g * SSD_N:SSD_INNER + SSD_BC + (g + 1) * SSD_N].T
        cb_row = jnp.sum(cm_t * bm.T, axis=0, keepdims=True)
        hg = h_ref[:, pl.ds(g * SSD_HG, SSD_HG)].reshape(nb * SSD_GW, SSD_N)
        ch = _dot(hg.astype(bf16), cm_t.astype(bf16))
        pairs = []
        for jp in range(SSD_HG // 2):
            x_t = xbc[:, g * SSD_GW + jp * LANE:g * SSD_GW + (jp + 1) * LANE].T
            halves = []
            for jj in range(2):
                j = 2 * jp + jj
                hd = g * SSD_HG + j
                dt_row = jax.nn.softplus(sm_t[SM_DT + hd:SM_DT + hd + 1, :] + dtb_ref[hd])
                ea_row = jnp.exp(dt_row * (-jnp.exp(_scalar_vec(alog_ref[hd]))))
                xdt_t = x_t[jj * SSD_P:(jj + 1) * SSD_P] * dt_row
                hi, lo = _split(xdt_t)
                xb = _dot(hi, e_mat) + _dot(lo, e_mat)
                yh = jnp.zeros((SSD_P, nb), f32)
                for b in range(nb):
                    r0 = (b * SSD_HG + j) * SSD_P
                    yh = jnp.where(lane_b == b, ch[r0:r0 + SSD_P], yh)
                    ho_ref[b, hd] = (h_ref[b, hd] * ea_row[:, b:b + 1]
                                     + xb[:, b * SSD_N:(b + 1) * SSD_N] * bm[b:b + 1])
                halves.append(yh * ea_row + cb_row * xdt_t)
            pairs.append(jnp.concatenate(halves, axis=0).T)
        y = jnp.concatenate(pairs, axis=1)
        y = (y + dvec_ref[:, gs] * xbc[:, gs]) * _silu(z_ref[:, gs])
        o_ref[:, gs] = _rms(y, nw_ref[:, gs]).astype(bf16)


def _ssd_step(proj, small, conv_t, states, prev, l, conv_w, conv_b, a_log, dt_bias, dvec, norm_w):
    n = proj.shape[0]
    nb = STEP_B
    smem = pl.BlockSpec(memory_space=pltpu.SMEM)
    const = lambda shape: pl.BlockSpec(shape, lambda i: (0,) * len(shape))
    st = pl.BlockSpec((None, nb, SSD_H, SSD_P, SSD_N), lambda i: (l, i, 0, 0, 0))
    return _state_call(
        _ssd_step_kernel, (n // nb,),
        [smem, smem,
         pl.BlockSpec((nb, SSD_INNER), lambda i: (i, _blk(P_Z_C, SSD_INNER))),
         pl.BlockSpec((nb, SSD_CONV), lambda i: (i, _blk(P_XBC, SSD_CONV))),
         pl.BlockSpec((nb, LANE), lambda i: (i, 0)),
         pl.BlockSpec((CONV_W - 1, nb, SSD_CONV), lambda i: (0, i, 0)),
         const((CONV_W, SSD_CONV)), const((1, SSD_CONV)), const((1, SSD_INNER)), const((1, SSD_INNER)), st],
        [pl.BlockSpec((nb, SSD_INNER), lambda i: (i, 0)), st],
        [jax.ShapeDtypeStruct((n, SSD_INNER), bf16), jax.ShapeDtypeStruct(states.shape, f32)],
        [], ("parallel",), "ssd_step",
        (a_log, dt_bias, proj, proj, small, conv_t, conv_w, conv_b, dvec, norm_w, states),
        prev)


def _merge_kernel(oa_ref, ob_ref, oc_ref, ga_ref, gb_ref, gc_ref, wa_ref, wb_ref, wc_ref, o_ref):
    m = (_sigmoid(ga_ref[...].astype(f32)) * _dot(oa_ref[...], wa_ref[...])
         + _sigmoid(gb_ref[...].astype(f32)) * _dot(ob_ref[...], wb_ref[...])
         + _sigmoid(gc_ref[...].astype(f32)) * _dot(oc_ref[...], wc_ref[...]))
    o_ref[...] = m.astype(bf16)


def _outproj_kernel(m_ref, x_ref, gt_ref, w_ref, o_ref):
    o_ref[...] = x_ref[...] + gt_ref[...] * _dot(m_ref[...], w_ref[...])


def _mixout(oa, ob, oc, proj, x, mod, wa, wb, wc, wo, l, per_row, tm):
    bx, tx, dm = x.shape
    tn = MERGE_COLS
    gate = lambda k: pl.BlockSpec((None, tm, tn), lambda b, i, j: (b, i, _blk(P_GATES + k * D_MODEL, tn) + j))
    act = lambda w: pl.BlockSpec((None, tm, w), lambda b, i, j: (b, i, 0))
    wsp = lambda w: pl.BlockSpec((None, w, tn), lambda b, i, j: (l, 0, j))
    merged = pl.pallas_call(
        _merge_kernel,
        grid=(bx, tx // tm, dm // tn),
        in_specs=[act(GDN_V), act(GLA_V), act(SSD_INNER), gate(0), gate(1), gate(2),
                  wsp(GDN_V), wsp(GLA_V), wsp(SSD_INNER)],
        out_specs=pl.BlockSpec((None, tm, tn), lambda b, i, j: (b, i, j)),
        out_shape=jax.ShapeDtypeStruct((bx, tx, dm), bf16),
        compiler_params=_cp(("parallel", "parallel", "arbitrary")),
        name="merge",
    )(oa, ob, oc, proj, proj, proj, wa, wb, wc)
    to = OUTPROJ_COLS
    nj = dm // to
    if per_row:
        gspec = pl.BlockSpec((None, tm, to), lambda b, i, j: (b, i, 5 * nj + j))
    else:
        gspec = pl.BlockSpec((None, 1, to), lambda b, i, j: (b, 0, 5 * nj + j))
    return pl.pallas_call(
        _outproj_kernel,
        grid=(bx, tx // tm, nj),
        in_specs=[pl.BlockSpec((None, tm, dm), lambda b, i, j: (b, i, 0)),
                  pl.BlockSpec((None, tm, to), lambda b, i, j: (b, i, j)),
                  gspec,
                  pl.BlockSpec((None, dm, to), lambda b, i, j: (l, 0, j))],
        out_specs=pl.BlockSpec((None, tm, to), lambda b, i, j: (b, i, j)),
        out_shape=jax.ShapeDtypeStruct(x.shape, f32),
        compiler_params=_cp(("parallel", "parallel", "arbitrary")),
        name="outproj",
    )(merged, x, mod, wo)


REGROUP_ROWS = 512


def _regroup_kernel(w_ref, o_ref):
    o_ref[...] = w_ref[0].astype(bf16)


def _regroup_rows(wt):
    nl, _, d = wt.shape
    tr = REGROUP_ROWS
    shifts = []
    for name in MAIN_ORDER:
        a, b = W_IN_SRC[name]
        assert (b - a) % tr == 0 and a % SUB == 0
        shifts.append((P_MAIN[name] // tr, a - P_MAIN[name]))

    def src_row(i):
        off = i * (tr // SUB) + shifts[0][1] // SUB
        for k in range(1, len(shifts)):
            off = off + jnp.where(i >= shifts[k][0], (shifts[k][1] - shifts[k - 1][1]) // SUB, 0)
        return off * SUB

    return pl.pallas_call(
        _regroup_kernel,
        grid=(nl, P_TOTAL // tr),
        in_specs=[pl.BlockSpec((pl.Element(1), pl.Element(tr), pl.Element(d)), lambda l, i: (l, src_row(i), 0))],
        out_specs=pl.BlockSpec((None, tr, d), lambda l, i: (l, i, 0)),
        out_shape=jax.ShapeDtypeStruct((nl, P_TOTAL, d), bf16),
        compiler_params=_cp(("parallel", "parallel")),
        name="regroup_w_in",
    )(wt)


def _permute_w_in(w):
    nl, d, _ = w.shape
    main = _regroup_rows(jnp.swapaxes(w, 1, 2))
    small = jnp.concatenate([w[:, :, W_IN_SRC[name][0]:W_IN_SRC[name][1]] for name in SMALL_ORDER], axis=2)
    small = jnp.pad(jnp.swapaxes(small, 1, 2), ((0, 0), (0, LANE - small.shape[2]), (0, 0)))
    return main, small


def _lane_vec(v, lo):
    return jnp.zeros((1, LANE), f32).at[0, lo:lo + v.shape[0]].set(v)


def _layer_params(l, p):
    row = lambda a: a[l].reshape(1, -1)
    wgate = jnp.zeros((LANE, GLA_QK), f32).at[SM_LR:SM_LR + GLA_RANK].set(p["gla_w_gate"][l]).astype(bf16)
    return dict(
        norm1=row(p["norm1"]), norm2=row(p["norm2"]), norm3=row(p["norm3"]),
        gdn_conv_w=p["gdn_conv_w"][l], gdn_a_log=p["gdn_a_log"][l], gdn_dt_bias=p["gdn_dt_bias"][l],
        gdn_alog_l=_lane_vec(p["gdn_a_log"][l], SM_DEC), gdn_dtb_l=_lane_vec(p["gdn_dt_bias"][l], SM_DEC),
        gdn_norm_w=row(p["gdn_norm_w"]),
        gla_wgate=wgate, gla_bgate=row(p["gla_b_gate"]), gla_norm_w=row(p["gla_norm_w"]),
        ssd_conv_w=p["ssd_conv_w"][l], ssd_conv_b=row(p["ssd_conv_b"]), ssd_a_log=p["ssd_a_log"][l],
        ssd_dt_bias=p["ssd_dt_bias"][l],
        ssd_avec_l=_lane_vec(-jnp.exp(p["ssd_a_log"][l]), SM_DT), ssd_dtb_l=_lane_vec(p["ssd_dt_bias"][l], SM_DT),
        ssd_dvec=jnp.repeat(p["ssd_d"][l], SSD_P).reshape(1, -1),
        ssd_norm_w=row(p["ssd_norm_w"]),
    )


def _stacked_weights(p):
    w_in, w_in_small = _permute_w_in(p["w_in"])
    return dict(
        f1=(p["ffn1_wg"], p["ffn1_wu"], p["ffn1_wd"]),
        f2=(p["ffn2_wg"], p["ffn2_wu"], p["ffn2_wd"]),
        w_in=w_in, w_in_small=w_in_small,
        wa=p["w_branch_gdn"].astype(bf16), wb=p["w_branch_gla"].astype(bf16),
        wc=p["w_branch_ssd"].astype(bf16), w_out=p["w_out"].astype(bf16),
    )


def _new_conv_state(buf, raw):
    t = raw.shape[1]
    k = CONV_W - 1
    if t >= k:
        return raw[:, t - k:]
    return jnp.concatenate([buf[:, t:], raw], axis=1)


def _mixer_prompt(proj, small, lp, st):
    gdn_conv, s_gdn, s_gla, ssd_conv, s_ssd = st
    b, t, _ = proj.shape
    c = CHUNK
    sm_spec = pl.BlockSpec((None, c, LANE), lambda bi, ti: (bi, ti, 0))
    st_a, st_b, st_c = _per_seq(GDN_H, GDN_DK, GDN_DV), _per_seq(GLA_H, GLA_DK, GLA_DV), _per_seq(SSD_H, SSD_P, SSD_N)
    gdn_specs = [_tok(GDN_CONV, P_QKV_A), _tok(GDN_V, P_Z_A), sm_spec, _per_seq(CONV_W - 1, GDN_CONV),
                 _const(CONV_W, GDN_CONV), _const(1, LANE), _const(1, LANE), _const(1, GDN_DV), st_a]
    gdn_args = (proj, proj, small, gdn_conv, lp["gdn_conv_w"], lp["gdn_alog_l"], lp["gdn_dtb_l"],
                lp["gdn_norm_w"], s_gdn)
    gla_specs = [_tok(GLA_QK, P_Q_B), _tok(GLA_QK, P_K_B), _tok(GLA_V, P_V_B), _tok(GLA_V, P_R_B), sm_spec,
                 _const(LANE, GLA_QK), _const(1, GLA_QK), _const(1, GLA_DV), st_b]
    gla_args = (proj, proj, proj, proj, small, lp["gla_wgate"], lp["gla_bgate"], lp["gla_norm_w"], s_gla)
    ssd_specs = [_tok(SSD_INNER, P_Z_C), _tok(SSD_CONV, P_XBC), sm_spec, _per_seq(CONV_W - 1, SSD_CONV),
                 _const(CONV_W, SSD_CONV), _const(1, SSD_CONV), _const(1, LANE), _const(1, LANE),
                 _const(1, SSD_INNER), _const(1, SSD_INNER), st_c]
    ssd_args = (proj, proj, small, ssd_conv, lp["ssd_conv_w"], lp["ssd_conv_b"], lp["ssd_avec_l"],
                lp["ssd_dtb_l"], lp["ssd_dvec"], lp["ssd_norm_w"], s_ssd)
    assert (len(gdn_specs), len(gla_specs), len(ssd_specs)) == (N_GDN_IN, N_GLA_IN, N_SSD_IN)
    out_tok = lambda width: pl.BlockSpec((None, c, width), lambda bi, ti: (bi, ti, 0))
    oa, s_gdn_n, ob, s_gla_n, oc, s_ssd_n = pl.pallas_call(
        _mixers_kernel,
        grid=(b, t // c),
        in_specs=gdn_specs + gla_specs + ssd_specs,
        out_specs=[out_tok(GDN_V), st_a, out_tok(GLA_V), st_b, out_tok(SSD_INNER), st_c],
        out_shape=[jax.ShapeDtypeStruct((b, t, GDN_V), bf16), jax.ShapeDtypeStruct(s_gdn.shape, f32),
                   jax.ShapeDtypeStruct((b, t, GLA_V), bf16), jax.ShapeDtypeStruct(s_gla.shape, f32),
                   jax.ShapeDtypeStruct((b, t, SSD_INNER), bf16), jax.ShapeDtypeStruct(s_ssd.shape, f32)],
        scratch_shapes=[pltpu.VMEM((GDN_CONV // LANE, CONV_AT + c, LANE), f32),
                        pltpu.VMEM((GLA_H, GLA_DV, GLA_DK), f32),
                        pltpu.VMEM((SSD_CONV // LANE, CONV_AT + c, LANE), f32),
                        pltpu.VMEM((GDN_V // LANE, c, LANE), f32), pltpu.VMEM((SSD_INNER // LANE, c, LANE), f32)],
        compiler_params=_cp(("parallel", "arbitrary")),
        name="mixers_prompt",
    )(*gdn_args, *gla_args, *ssd_args)
    gdn_conv_n = _new_conv_state(gdn_conv, proj[:, :, P_QKV_A:P_QKV_A + GDN_CONV]).astype(f32)
    ssd_conv_n = _new_conv_state(ssd_conv, proj[:, :, P_XBC:P_XBC + SSD_CONV]).astype(f32)
    return (oa, ob, oc), (gdn_conv_n, s_gdn_n, s_gla_n, ssd_conv_n, s_ssd_n)


def _mixer_sample(proj, small, lp, l, states, prev):
    n = proj.shape[1]
    p2 = proj.reshape(n, P_TOTAL)
    s2 = small.reshape(n, LANE)
    gdn_conv, ssd_conv = states[0][l], states[3][l]
    gct = jnp.swapaxes(gdn_conv, 0, 1)
    sct = jnp.swapaxes(ssd_conv, 0, 1)
    pv = (None,) * 5 if prev is None else prev
    oa, s_gdn_n = _gdn_step(p2, s2, gct, states[1], pv[1], l, lp["gdn_conv_w"], lp["gdn_a_log"],
                            lp["gdn_dt_bias"], lp["gdn_norm_w"])
    ob, s_gla_n = _gla_step(p2, s2, states[2], pv[2], l, lp["gla_wgate"], lp["gla_bgate"], lp["gla_norm_w"])
    oc, s_ssd_n = _ssd_step(p2, s2, sct, states[4], pv[4], l, lp["ssd_conv_w"], lp["ssd_conv_b"], lp["ssd_a_log"],
                            lp["ssd_dt_bias"], lp["ssd_dvec"], lp["ssd_norm_w"])
    raw = p2.reshape(n, 1, P_TOTAL)
    gdn_conv_n = _new_conv_state(gdn_conv, raw[:, :, P_QKV_A:P_QKV_A + GDN_CONV])
    ssd_conv_n = _new_conv_state(ssd_conv, raw[:, :, P_XBC:P_XBC + SSD_CONV])
    outs = tuple(o.reshape(1, n, -1) for o in (oa, ob, oc))
    return outs, (gdn_conv_n, s_gdn_n, s_gla_n, ssd_conv_n, s_ssd_n)


def _trunk(x, mods, lps, sw, ffn_w, states, per_row, tm, tm_in, final_w):
    nl = len(lps)
    per_layer = []
    prev = None

    def ffn(x, mod, k0, nw, name, l, fw, final):
        if per_row:
            x, ffn_w[l, name] = _ffn_cast(x, mod, k0, nw, *sw[name], l, fw, final)
            return x
        return _ffn(x, mod, k0, nw, ffn_w[l, name], fw, per_row, tm, final)

    for l in range(nl):
        lp, mod = lps[l], mods[l]
        last = l == nl - 1
        x = ffn(x, mod, 0, lp["norm1"], "f1", l, lp["norm1"], False)
        proj, small = _inproj(x, mod, lp["norm2"], sw["w_in"], sw["w_in_small"], l, per_row, tm_in,
                              f32 if per_row else bf16)
        if per_row:
            (oa, ob, oc), st = _mixer_sample(proj, small, lp, l, states, prev)
            prev = st
        else:
            (oa, ob, oc), st = _mixer_prompt(proj, small, lp, tuple(s[l] for s in states))
        per_layer.append(st)
        x = _mixout(oa, ob, oc, proj, x, mod, sw["wa"], sw["wb"], sw["wc"], sw["w_out"], l, per_row, tm_in)
        x = ffn(x, mod, 6, lp["norm3"], "f2", l, final_w if last else lp["norm3"], last)
    stack = lambda i: jnp.stack([st[i] for st in per_layer])
    if per_row:
        new_states = (stack(0), prev[1], prev[2], stack(3), prev[4])
    else:
        new_states = tuple(stack(i) for i in range(5))
    return x, new_states


def kernel(x_prompt, x_sample, state_gdn_conv, state_gdn, state_gla, state_ssd_conv, state_ssd, c_prompt, c_sample, w_ada, b_ada, norm1, norm2, norm3, ffn1_wg, ffn1_wu, ffn1_wd, ffn2_wg, ffn2_wu, ffn2_wd, w_in, gdn_conv_w, gdn_a_log, gdn_dt_bias, gdn_norm_w, gla_w_gate, gla_b_gate, gla_norm_w, ssd_conv_w, ssd_conv_b, ssd_a_log, ssd_dt_bias, ssd_d, ssd_norm_w, w_branch_gdn, w_branch_gla, w_branch_ssd, w_out, final_norm):
    p = dict(norm1=norm1, norm2=norm2, norm3=norm3,
             ffn1_wg=ffn1_wg, ffn1_wu=ffn1_wu, ffn1_wd=ffn1_wd, ffn2_wg=ffn2_wg, ffn2_wu=ffn2_wu, ffn2_wd=ffn2_wd,
             w_in=w_in, gdn_conv_w=gdn_conv_w, gdn_a_log=gdn_a_log, gdn_dt_bias=gdn_dt_bias, gdn_norm_w=gdn_norm_w,
             gla_w_gate=gla_w_gate, gla_b_gate=gla_b_gate, gla_norm_w=gla_norm_w,
             ssd_conv_w=ssd_conv_w, ssd_conv_b=ssd_conv_b, ssd_a_log=ssd_a_log, ssd_dt_bias=ssd_dt_bias,
             ssd_d=ssd_d, ssd_norm_w=ssd_norm_w,
             w_branch_gdn=w_branch_gdn, w_branch_gla=w_branch_gla, w_branch_ssd=w_branch_ssd, w_out=w_out)
    nl = w_ada.shape[0]
    bp, tp, dm = x_prompt.shape
    bs = x_sample.shape[0]
    assert x_sample.shape[1] == 1 and tp % CHUNK == 0 and bs % STEP_B == 0 and dm == D_MODEL
    lps = [_layer_params(l, p) for l in range(nl)]
    sw = _stacked_weights(p)
    fw = final_norm.reshape(1, dm)

    rows = bp + bs
    rpad = -(-rows // 8) * 8
    c_all = jnp.concatenate([c_prompt, c_sample, jnp.zeros((rpad - rows, dm), f32)], axis=0)
    mod = _ada_mod(c_all, w_ada, b_ada)
    mod_p = [mod[l, :bp].reshape(bp, 1, N_MOD * dm) for l in range(nl)]
    mod_s = [mod[l, bp:rows].reshape(1, bs, N_MOD * dm) for l in range(nl)]

    sample_states = (state_gdn_conv, state_gdn, state_gla, state_ssd_conv, state_ssd)
    prompt_states = tuple(jnp.zeros((s.shape[0], bp) + s.shape[2:], x_prompt.dtype) for s in sample_states)
    tm_p = ROW_TILE if tp % ROW_TILE == 0 else CHUNK
    tm_in = ROW_TILE_WIDE if tp % ROW_TILE_WIDE == 0 else tm_p
    ffn_w = {}
    y_s, st_s = _trunk(x_sample.reshape(1, bs, dm), mod_s, lps, sw, ffn_w, sample_states, True, bs, bs, fw)
    y_p, st_p = _trunk(x_prompt, mod_p, lps, sw, ffn_w, prompt_states, False, tm_p, tm_in, fw)
    return (y_p, y_s.reshape(bs, 1, dm)) + st_p + st_s
```

```python
import functools

import jax
import jax.numpy as jnp
from jax import lax
from jax.experimental import pallas as pl
from jax.experimental.pallas import tpu as pltpu

f32 = jnp.float32
bf16 = jnp.bfloat16
HI = lax.Precision.HIGHEST

EPS = 1e-6
D_MODEL = 2048
N_MOD = 9
CHUNK = 64
CONV_W = 4
GDN_H, GDN_DK, GDN_DV = 8, 128, 128
GLA_H, GLA_DK, GLA_DV, GLA_RANK, GLA_TAU = 4, 128, 256, 16, 16.0
SSD_H, SSD_P, SSD_G, SSD_N = 32, 64, 4, 128
SSD_HG = SSD_H // SSD_G
GDN_QK = GDN_H * GDN_DK
GDN_V = GDN_H * GDN_DV
GDN_CONV = 2 * GDN_QK + GDN_V
GLA_QK = GLA_H * GLA_DK
GLA_V = GLA_H * GLA_DV
SSD_INNER = SSD_H * SSD_P
SSD_BC = SSD_G * SSD_N
SSD_CONV = SSD_INNER + 2 * SSD_BC
SSD_GW = SSD_HG * SSD_P

LANE = 128
MXU_WIDTH = 256

ROW_TILE = 512
ROW_TILE_WIDE = 1024
FF_TILE = 512
FF_SUB = MXU_WIDTH
ADA_COLS = 1024
INPROJ_COLS = 1024
MERGE_COLS = 1024
OUTPROJ_COLS = 1024

IN_SPLITS = (("qkv_a", GDN_CONV), ("z_a", GDN_V), ("beta", GDN_H), ("dec", GDN_H),
             ("q_b", GLA_QK), ("k_b", GLA_QK), ("v_b", GLA_V), ("lr", GLA_RANK), ("r_b", GLA_V),
             ("z_c", SSD_INNER), ("xbc", SSD_CONV), ("dt", SSD_H), ("gates", 3 * D_MODEL))
MAIN_ORDER = ("qkv_a", "xbc", "z_c", "gates", "z_a", "q_b", "k_b", "v_b", "r_b")
SMALL_ORDER = ("beta", "dec", "lr", "dt")


def _layout():
    src, off = {}, 0
    for name, w in IN_SPLITS:
        src[name] = (off, off + w)
        off += w
    main, small, d = {}, {}, 0
    for name in MAIN_ORDER:
        w = src[name][1] - src[name][0]
        main[name] = d
        d += w
    total = d
    d = 0
    for name in SMALL_ORDER:
        small[name] = d
        d += src[name][1] - src[name][0]
    assert d <= LANE
    return src, main, small, total


W_IN_SRC, P_MAIN, P_SM, P_TOTAL = _layout()
P_QKV_A, P_XBC, P_Z_C, P_GATES, P_Z_A = (P_MAIN[k] for k in ("qkv_a", "xbc", "z_c", "gates", "z_a"))
P_Q_B, P_K_B, P_V_B, P_R_B = (P_MAIN[k] for k in ("q_b", "k_b", "v_b", "r_b"))
SM_BETA, SM_DEC, SM_LR, SM_DT = (P_SM[k] for k in SMALL_ORDER)

VMEM_LIMIT = 56 * 1024 * 1024


def _cp(sem):
    return pltpu.CompilerParams(dimension_semantics=sem, vmem_limit_bytes=VMEM_LIMIT)


def _blk(off, width):
    assert off % width == 0, (off, width)
    return off // width


def _sigmoid(x):
    return 0.5 + 0.5 * jnp.tanh(0.5 * x)


def _silu(x):
    h = 0.5 * x
    return h + h * jnp.tanh(h)


def _rms(x, w):
    return x * lax.rsqrt(jnp.mean(x * x, axis=-1, keepdims=True) + EPS) * w


def _dot(a, b):
    return jnp.dot(a, b, preferred_element_type=f32)


def _dot_nt(a, b):
    return lax.dot_general(a, b, (((1,), (1,)), ((), ())), preferred_element_type=f32)


def _dot_tn(a, b):
    return lax.dot_general(a, b, (((0,), (0,)), ((), ())), preferred_element_type=f32)


def _split(x):
    hi = x.astype(bf16)
    return hi, (x - hi.astype(f32)).astype(bf16)


def _dot3(a, b):
    return _dot(a[0], b[0]) + (_dot(a[0], b[1]) + _dot(a[1], b[0]))


def _dot_hi(a, b):
    return jnp.dot(a, b, precision=HI, preferred_element_type=f32)


def _tri_masks(c):
    row = lax.broadcasted_iota(jnp.int32, (c, c), 0)
    col = lax.broadcasted_iota(jnp.int32, (c, c), 1)
    return row >= col, row > col, row == col


SUB = 8


def _row_time(idx, c):
    n = c // SUB
    assert n & (n - 1) == 0, "chunk / SUB must be a power of two"
    sh = n.bit_length() - 1
    return ((idx & (n - 1)) << 3) | (idx >> sh)


def _tri_masks_residue(c, cols=None):
    cols = c if cols is None else cols
    row = _row_time(lax.broadcasted_iota(jnp.int32, (c, cols), 0), c)
    col = _row_time(lax.broadcasted_iota(jnp.int32, (c, cols), 1) & (c - 1), c)
    return row >= col, row > col, row == col


def _load_residue(ref, c):
    n = c // SUB
    return jnp.concatenate([ref[pl.ds(r, n, stride=SUB), :] for r in range(SUB)], axis=0)


def _store_natural(nat_ref, x, c, lo):
    n = c // SUB
    for j in range(x.shape[1] // LANE):
        for r in range(SUB):
            nat_ref[lo // LANE + j, pl.ds(r, n, stride=SUB), :] = x[r * n:(r + 1) * n, j * LANE:(j + 1) * LANE]


def _read_natural(nat_ref, lo, width):
    return jnp.concatenate([nat_ref[lo // LANE + j] for j in range(width // LANE)], axis=1)


def _scalar_vec(s):
    return jnp.full((1, 1), s, f32)


def _ada_kernel(c_ref, w_ref, b_ref, o_ref):
    s = _silu(c_ref[...]).astype(bf16)
    o_ref[...] = _dot(s, w_ref[...].astype(bf16)) + b_ref[...]


def _ada_mod(c_all, w_ada, b_ada):
    nl, dm, n = w_ada.shape
    r = c_all.shape[0]
    tn = ADA_COLS
    return pl.pallas_call(
        _ada_kernel,
        grid=(nl, n // tn),
        in_specs=[pl.BlockSpec((r, dm), lambda l, j: (0, 0)),
                  pl.BlockSpec((None, dm, tn), lambda l, j: (l, 0, j)),
                  pl.BlockSpec((None, 1, tn), lambda l, j: (l, 0, j))],
        out_specs=pl.BlockSpec((None, r, tn), lambda l, j: (l, 0, j)),
        out_shape=jax.ShapeDtypeStruct((nl, r, n), f32),
        compiler_params=_cp(("arbitrary", "arbitrary")),
        name="ada_mod",
    )(c_all, w_ada, b_ada.reshape(nl, 1, n))


def _mod_spec(per_row, tm, chunk):
    if per_row:
        return pl.BlockSpec((None, tm, D_MODEL), lambda b, i, j: (b, i, chunk))
    return pl.BlockSpec((None, 1, D_MODEL), lambda b, i, j: (b, 0, chunk))


def _ffn_prologue(x_ref, sh_ref, sc_ref, nw_ref, h_ref, acc_ref):
    y = _rms(x_ref[...], nw_ref[...])
    h_ref[...] = (y * (1.0 + sc_ref[...]) + sh_ref[...]).astype(bf16)
    acc_ref[...] = jnp.zeros_like(acc_ref)


def _ffn_accumulate(h_ref, acc_ref, wg, wu, wd):
    h = h_ref[...]
    width = wg.shape[1]
    sub = FF_SUB if width % FF_SUB == 0 else width
    parts = [(_dot(h, wg[:, s:s + sub]), _dot(h, wu[:, s:s + sub])) for s in range(0, width, sub)]
    acc = acc_ref[...]
    for i, (g, u) in enumerate(parts):
        acc = acc + _dot((_silu(g) * u).astype(bf16), wd[i * sub:(i + 1) * sub, :])
    acc_ref[...] = acc


def _ffn_epilogue(x_ref, gt_ref, fw_ref, acc_ref, o_ref, final):
    y = x_ref[...] + 0.5 * gt_ref[...] * acc_ref[...]
    if final:
        y = _rms(y, fw_ref[...])
    o_ref[...] = y


def _ffn_kernel(x_ref, sh_ref, sc_ref, gt_ref, nw_ref, fw_ref, wg_ref, wu_ref, wd_ref, *rest, final, has_tail):
    if has_tail:
        wgt_ref, wut_ref, wdt_ref, o_ref, h_ref, acc_ref = rest
    else:
        o_ref, h_ref, acc_ref = rest
    f = pl.program_id(2)
    last = pl.num_programs(2) - 1
    pl.when(f == 0)(lambda: _ffn_prologue(x_ref, sh_ref, sc_ref, nw_ref, h_ref, acc_ref))
    if has_tail:
        pl.when(f < last)(lambda: _ffn_accumulate(h_ref, acc_ref, wg_ref, wu_ref, wd_ref))
        pl.when(f == last)(lambda: _ffn_accumulate(h_ref, acc_ref, wgt_ref, wut_ref, wdt_ref))
    else:
        _ffn_accumulate(h_ref, acc_ref, wg_ref, wu_ref, wd_ref)
    pl.when(f == last)(lambda: _ffn_epilogue(x_ref, gt_ref, fw_ref, acc_ref, o_ref, final))


def _ffn(x, mod, k0, nw, w, fw, per_row, tm, final):
    bx, tx, dm = x.shape
    tf = FF_TILE
    n_full = w[0].shape[1] // tf
    has_tail = len(w) > 3
    full = lambda f: jnp.minimum(f, n_full - 1)
    w_specs = [pl.BlockSpec((dm, tf), lambda b, i, f: (0, full(f))),
               pl.BlockSpec((dm, tf), lambda b, i, f: (0, full(f))),
               pl.BlockSpec((tf, dm), lambda b, i, f: (full(f), 0))]
    if has_tail:
        ft = w[3].shape[1]
        w_specs += [pl.BlockSpec((dm, ft), lambda b, i, f: (0, 0)),
                    pl.BlockSpec((dm, ft), lambda b, i, f: (0, 0)),
                    pl.BlockSpec((ft, dm), lambda b, i, f: (0, 0))]
    return pl.pallas_call(
        functools.partial(_ffn_kernel, final=final, has_tail=has_tail),
        grid=(bx, tx // tm, n_full + int(has_tail)),
        in_specs=[pl.BlockSpec((None, tm, dm), lambda b, i, f: (b, i, 0)),
                  _mod_spec(per_row, tm, k0), _mod_spec(per_row, tm, k0 + 1), _mod_spec(per_row, tm, k0 + 2),
                  pl.BlockSpec((1, dm), lambda b, i, f: (0, 0)),
                  pl.BlockSpec((1, dm), lambda b, i, f: (0, 0))] + w_specs,
        out_specs=pl.BlockSpec((None, tm, dm), lambda b, i, f: (b, i, 0)),
        out_shape=jax.ShapeDtypeStruct(x.shape, f32),
        scratch_shapes=[pltpu.VMEM((tm, dm), bf16), pltpu.VMEM((tm, dm), f32)],
        compiler_params=_cp(("parallel", "parallel", "arbitrary")),
        name="ffn",
    )(x, mod, mod, mod, nw, fw, *w)


FF_CAST_TILE = 256


def _ffn_cast_kernel(x_ref, sh_ref, sc_ref, gt_ref, nw_ref, fw_ref, wg_ref, wu_ref, wd_ref, *rest, final, has_tail):
    if has_tail:
        wgt_ref, wut_ref, wdt_ref, o_ref, wgb_ref, wub_ref, wdb_ref, wgtb_ref, wutb_ref, wdtb_ref, h_ref, acc_ref = rest
    else:
        o_ref, wgb_ref, wub_ref, wdb_ref, h_ref, acc_ref = rest
    f = pl.program_id(2)
    last = pl.num_programs(2) - 1
    pl.when(f == 0)(lambda: _ffn_prologue(x_ref, sh_ref, sc_ref, nw_ref, h_ref, acc_ref))

    def cast_accumulate(src, dst):
        for s_ref, d_ref in zip(src, dst):
            d_ref[...] = s_ref[...].reshape(d_ref.shape).astype(bf16)
        _ffn_accumulate(h_ref, acc_ref, *dst)

    if has_tail:
        pl.when(f < last)(lambda: cast_accumulate((wg_ref, wu_ref, wd_ref), (wgb_ref, wub_ref, wdb_ref)))
        pl.when(f == last)(lambda: cast_accumulate((wgt_ref, wut_ref, wdt_ref), (wgtb_ref, wutb_ref, wdtb_ref)))
    else:
        cast_accumulate((wg_ref, wu_ref, wd_ref), (wgb_ref, wub_ref, wdb_ref))
    pl.when(f == last)(lambda: _ffn_epilogue(x_ref, gt_ref, fw_ref, acc_ref, o_ref, final))


def _ffn_cast(x, mod, k0, nw, wg, wu, wd, l, fw, final):
    bx, tx, dm = x.shape
    assert bx == 1
    f_all = wg.shape[2]
    tf = FF_CAST_TILE
    cut = f_all - f_all % FF_TILE
    n_full = cut // tf
    has_tail = cut < f_all
    full = lambda f: jnp.minimum(f, n_full - 1)
    w_in_specs = [pl.BlockSpec((None, dm, tf), lambda b, i, f: (l, 0, full(f))),
                  pl.BlockSpec((None, dm, tf), lambda b, i, f: (l, 0, full(f))),
                  pl.BlockSpec((None, tf, dm), lambda b, i, f: (l, full(f), 0))]
    w_out_specs = [pl.BlockSpec((dm, tf), lambda b, i, f: (0, full(f))),
                   pl.BlockSpec((dm, tf), lambda b, i, f: (0, full(f))),
                   pl.BlockSpec((tf, dm), lambda b, i, f: (full(f), 0))]
    w_out_shapes = [jax.ShapeDtypeStruct((dm, cut), bf16), jax.ShapeDtypeStruct((dm, cut), bf16),
                    jax.ShapeDtypeStruct((cut, dm), bf16)]
    args = [wg, wu, wd]
    if has_tail:
        ft = f_all - cut
        args += [wg, wu, wd]
        el = lambda *shape: tuple(pl.Element(s) for s in shape)
        w_in_specs += [pl.BlockSpec(el(1, dm, ft), lambda b, i, f: (l, 0, cut)),
                       pl.BlockSpec(el(1, dm, ft), lambda b, i, f: (l, 0, cut)),
                       pl.BlockSpec(el(1, ft, dm), lambda b, i, f: (l, cut, 0))]
        w_out_specs += [pl.BlockSpec((dm, ft), lambda b, i, f: (0, 0)), pl.BlockSpec((dm, ft), lambda b, i, f: (0, 0)),
                        pl.BlockSpec((ft, dm), lambda b, i, f: (0, 0))]
        w_out_shapes += [jax.ShapeDtypeStruct((dm, ft), bf16), jax.ShapeDtypeStruct((dm, ft), bf16),
                         jax.ShapeDtypeStruct((ft, dm), bf16)]
    out = pl.pallas_call(
        functools.partial(_ffn_cast_kernel, final=final, has_tail=has_tail),
        grid=(1, 1, n_full + int(has_tail)),
        in_specs=[pl.BlockSpec((None, tx, dm), lambda b, i, f: (b, i, 0)),
                  _mod_spec(True, tx, k0), _mod_spec(True, tx, k0 + 1), _mod_spec(True, tx, k0 + 2),
                  pl.BlockSpec((1, dm), lambda b, i, f: (0, 0)),
                  pl.BlockSpec((1, dm), lambda b, i, f: (0, 0))] + w_in_specs,
        out_specs=[pl.BlockSpec((None, tx, dm), lambda b, i, f: (b, i, 0))] + w_out_specs,
        out_shape=[jax.ShapeDtypeStruct(x.shape, f32)] + w_out_shapes,
        scratch_shapes=[pltpu.VMEM((tx, dm), bf16), pltpu.VMEM((tx, dm), f32)],
        compiler_params=_cp(("arbitrary", "arbitrary", "arbitrary")),
        name="ffn_cast",
    )(x, mod, mod, mod, nw, fw, *args)
    return out[0], tuple(out[1:])


def _inproj_kernel(x_ref, sh_ref, sc_ref, nw_ref, w_ref, ws_ref, o_ref, os_ref, h_ref):
    @pl.when(pl.program_id(2) == 0)
    def _():
        y = _rms(x_ref[...], nw_ref[...])
        h = (y * (1.0 + sc_ref[...]) + sh_ref[...]).astype(bf16)
        h_ref[...] = h
        os_ref[...] = _dot_nt(h, ws_ref[...].astype(bf16))

    o_ref[...] = _dot_nt(h_ref[...], w_ref[...]).astype(o_ref.dtype)


def _inproj(x, mod, nw, w, ws, l, per_row, tm, out_dtype):
    bx, tx, dm = x.shape
    n = w.shape[1]
    tn = INPROJ_COLS
    return pl.pallas_call(
        _inproj_kernel,
        grid=(bx, tx // tm, n // tn),
        in_specs=[pl.BlockSpec((None, tm, dm), lambda b, i, j: (b, i, 0)),
                  _mod_spec(per_row, tm, 3), _mod_spec(per_row, tm, 4),
                  pl.BlockSpec((1, dm), lambda b, i, j: (0, 0)),
                  pl.BlockSpec((None, tn, dm), lambda b, i, j: (l, j, 0)),
                  pl.BlockSpec((None, LANE, dm), lambda b, i, j: (l, 0, 0))],
        out_specs=[pl.BlockSpec((None, tm, tn), lambda b, i, j: (b, i, j)),
                   pl.BlockSpec((None, tm, LANE), lambda b, i, j: (b, i, 0))],
        out_shape=[jax.ShapeDtypeStruct((bx, tx, n), out_dtype),
                   jax.ShapeDtypeStruct((bx, tx, LANE), f32)],
        scratch_shapes=[pltpu.VMEM((tm, dm), bf16)],
        compiler_params=_cp(("parallel", "parallel", "arbitrary")),
        name="inproj",
    )(x, mod, mod, nw, w, ws)


def _conv_load(ext_ref, x_ref, cst_ref, first):
    c = x_ref.shape[0]
    nblk = ext_ref.shape[0]

    @pl.when(first)
    def _():
        for j in range(nblk):
            ext_ref[j, pl.ds(5, CONV_W - 1), :] = cst_ref[:, j * LANE:(j + 1) * LANE]

    for j in range(nblk):
        ext_ref[j, pl.ds(8, c), :] = x_ref[:, j * LANE:(j + 1) * LANE].astype(f32)


def _conv_cols(ext_ref, w_ref, b_ref, lo, width, c):
    n = c // SUB
    cols = []
    for blk in range(lo // LANE, (lo + width) // LANE):
        sl = pl.ds(blk * LANE, LANE)
        w = w_ref[:, sl]
        taps = [ext_ref[blk, pl.ds(8 - (CONV_W - 1) + k, n, stride=SUB), :] for k in range(SUB + CONV_W - 1)]
        ys = []
        for r in range(SUB):
            y = taps[r] * w[0:1]
            for j in range(1, CONV_W):
                y = y + taps[r + j] * w[j:j + 1]
            ys.append(y)
        y = jnp.concatenate(ys, axis=0)
        if b_ref is not None:
            y = y + b_ref[:, sl]
        cols.append(_silu(y))
    return cols[0] if len(cols) == 1 else jnp.concatenate(cols, axis=1)


def _conv_carry(ext_ref, c):
    for j in range(ext_ref.shape[0]):
        ext_ref[j, pl.ds(5, CONV_W - 1), :] = ext_ref[j, pl.ds(8 + c - (CONV_W - 1), CONV_W - 1), :]


def _cumsum_rows(x, tri):
    return _dot_hi(jnp.where(tri, 1.0, 0.0).astype(f32), x)


def _gdn_kernel(qkv_ref, z_ref, sm_ref, cst_ref, cw_ref, alog_ref, dtb_ref, nw_ref, s0_ref,
                o_ref, s_ref, ext_ref, nat_ref):
    c = qkv_ref.shape[0]
    first = pl.program_id(1) == 0

    @pl.when(first)
    def _():
        s_ref[...] = s0_ref[...]

    _conv_load(ext_ref, qkv_ref, cst_ref, first)

    sm = _load_residue(sm_ref, c)
    beta_l = _sigmoid(sm)
    g_l = -jnp.exp(alog_ref[...]) * jax.nn.softplus(sm + dtb_ref[...])
    tri, strict, eye = _tri_masks_residue(c)
    gc_all = _cumsum_rows(g_l, tri)
    gc_t = gc_all.T
    eye_f = jnp.where(eye, 1.0, 0.0).astype(f32)

    hs = range(GDN_H)
    qb, kb, ks, decay, eg, gcs, nm, rhs = [], [], [], [], [], [], [], []
    for h in hs:
        q = _conv_cols(ext_ref, cw_ref, None, h * GDN_DK, GDN_DK, c)
        k = _conv_cols(ext_ref, cw_ref, None, GDN_QK + h * GDN_DK, GDN_DK, c)
        v = _conv_cols(ext_ref, cw_ref, None, 2 * GDN_QK + h * GDN_DV, GDN_DV, c)
        q = q * lax.rsqrt(jnp.sum(q * q, axis=-1, keepdims=True) + EPS) * (GDN_DK ** -0.5)
        k = k * lax.rsqrt(jnp.sum(k * k, axis=-1, keepdims=True) + EPS)
        beta = beta_l[:, SM_BETA + h:SM_BETA + h + 1]
        gc = gc_all[:, SM_DEC + h:SM_DEC + h + 1]
        d = jnp.exp(jnp.where(tri, gc - gc_t[SM_DEC + h:SM_DEC + h + 1, :], -jnp.inf))
        e = jnp.exp(gc)
        qb.append(q.astype(bf16))
        kb.append(k.astype(bf16))
        ks.append(k)
        decay.append(d)
        eg.append(e)
        gcs.append(gc)
        nm.append(jnp.where(strict, beta * d * _dot_nt(kb[h], kb[h]), 0.0))
        rhs.append(jnp.concatenate([v * beta, k * (beta * e)], axis=1))
        yield

    assert c // GDN_BLK <= 4 and GDN_BLK & (GDN_BLK - 1) == 0
    blk_sh = GDN_BLK.bit_length() - 1
    same_blk = (_row_time(lax.broadcasted_iota(jnp.int32, (c, c), 0), c) >> blk_sh
                == _row_time(lax.broadcasted_iota(jnp.int32, (c, c), 1), c) >> blk_sh)
    nd = [jnp.where(same_blk, nm[h], 0.0) for h in hs]
    lo_s = [_split(nm[h] - nd[h]) for h in hs]
    dinv = [eye_f - nd[h] for h in hs]
    pw_s = [_split(nd[h]) for h in hs]
    for i in range(GDN_BLK.bit_length() - 2):
        pw_s = [_split(_dot3(pw_s[h], pw_s[h])) for h in hs]
        yield
        dinv = [dinv[h] + _dot3(_split(dinv[h]), pw_s[h]) for h in hs]
        yield
    dinv_s = [_split(dinv[h]) for h in hs]
    m = [_dot3(dinv_s[h], lo_s[h]) for h in hs]
    m_s = [_split(m[h]) for h in hs]
    yield
    m2 = [_dot3(m_s[h], m_s[h]) for h in hs]
    yield
    t1 = [eye_f - m[h] + m2[h] - _dot3(m_s[h], _split(m2[h])) for h in hs]
    yield
    inv = [_dot3(_split(t1[h]), dinv_s[h]) for h in hs]
    yield
    sol = []
    for h in hs:
        inv_hi, inv_lo = _split(inv[h])
        rb = rhs[h].astype(bf16)
        sol.append(_dot(inv_hi, rb) + _dot(inv_lo, rb))
    yield

    s = [s_ref[h] for h in hs]
    sb = [s[h].astype(bf16) for h in hs]
    wb = [(sol[h][:, 0:GDN_DV] - _dot(sol[h][:, GDN_DV:].astype(bf16), sb[h])).astype(bf16) for h in hs]
    yield
    qk = [(_dot_nt(qb[h], kb[h]) * decay[h]).astype(bf16) for h in hs]
    yield
    o = [_dot(qb[h], sb[h]) * eg[h] + _dot(qk[h], wb[h]) for h in hs]
    yield
    for h in hs:
        g_last = gcs[h][c - 1:c]
        kd = (ks[h] * jnp.exp(g_last - gcs[h])).astype(bf16)
        s_ref[h] = s[h] * jnp.exp(g_last) + _dot_tn(kd, wb[h])
    yield
    for h in hs:
        sl = slice(h * GDN_DV, (h + 1) * GDN_DV)
        _store_natural(nat_ref, _rms(o[h], nw_ref[...]), c, h * GDN_DV)
        o_ref[:, sl] = (_read_natural(nat_ref, h * GDN_DV, GDN_DV) * _silu(z_ref[:, sl].astype(f32))).astype(bf16)
        yield

    _conv_carry(ext_ref, c)


def _tok(width, off):
    return pl.BlockSpec((None, CHUNK, width), lambda bi, ti: (bi, ti, _blk(off, width)))


def _per_seq(*shape):
    return pl.BlockSpec((None,) + shape, lambda bi, ti: (bi,) + (0,) * len(shape))


def _const(*shape):
    return pl.BlockSpec(shape, lambda bi, ti: (0,) * len(shape))


N_GDN_IN, N_GLA_IN, N_SSD_IN = 9, 9, 11
_DONE = object()


def _mixers_kernel(*refs):
    i0, i1, i2 = N_GDN_IN, N_GDN_IN + N_GLA_IN, N_GDN_IN + N_GLA_IN + N_SSD_IN
    gdn_in, gla_in, ssd_in = refs[:i0], refs[i0:i1], refs[i1:i2]
    oa_ref, sa_ref, ob_ref, sb_ref, oc_ref, sc_ref, ext_a_ref, st_b_ref, ext_c_ref, nat_a_ref, nat_c_ref = refs[i2:]
    live = [_gdn_kernel(*gdn_in, oa_ref, sa_ref, ext_a_ref, nat_a_ref),
            _gla_kernel(*gla_in, ob_ref, sb_ref, st_b_ref),
            _ssd_kernel(*ssd_in, oc_ref, sc_ref, ext_c_ref, nat_c_ref)]
    while live:
        for gen in list(live):
            if next(gen, _DONE) is _DONE:
                live.remove(gen)


GLA_SUB = 16
GDN_BLK = 16


def _gla_kernel(q_ref, k_ref, v_ref, r_ref, sm_ref, wgate_ref, bgate_ref, nw_ref, s0_ref,
                o_ref, so_ref, st_ref):
    c = q_ref.shape[0]
    ti = pl.program_id(1)

    @pl.when(ti == 0)
    def _():
        for h in range(GLA_H):
            st_ref[h] = s0_ref[h].T

    sm = sm_ref[...]
    lane = lax.broadcasted_iota(jnp.int32, sm.shape, 1)
    lr = jnp.where((lane >= SM_LR) & (lane < SM_LR + GLA_RANK), sm, 0.0).astype(bf16)
    la = jax.nn.log_sigmoid(_dot(lr, wgate_ref[...]) + bgate_ref[...]) / GLA_TAU
    tri, _, _ = _tri_masks(c)
    b_all = _cumsum_rows(la, tri)

    col = lax.broadcasted_iota(jnp.int32, (GLA_SUB, c), 1)
    row = lax.broadcasted_iota(jnp.int32, (GLA_SUB, c), 0)
    hs = range(GLA_H)
    qs, ks, vbs, bs, atts = [], [], [], [], []
    for h in hs:
        sk = slice(h * GLA_DK, (h + 1) * GLA_DK)
        q = q_ref[:, sk].astype(f32) * (GLA_DK ** -0.5)
        k = k_ref[:, sk].astype(f32)
        b = b_all[:, sk]
        qs.append(q)
        ks.append(k)
        bs.append(b)
        vbs.append(v_ref[:, h * GLA_DV:(h + 1) * GLA_DV].astype(bf16))
        att_rows = []
        for i in range(c // GLA_SUB):
            lo = i * GLA_SUB
            b_i = b[lo:lo + GLA_SUB]
            q_i = q[lo:lo + GLA_SUB]
            b_top = b[lo:lo + 1]
            att = jnp.zeros((GLA_SUB, c), f32)
            if i > 0:
                q_t = (q_i * jnp.exp(b_i - b_top)).astype(bf16)
                k_t = (k * jnp.exp(jnp.minimum(b_top - b, 0.0))).astype(bf16)
                att = jnp.where(col < lo, _dot_nt(q_t, k_t), 0.0)
            for sl in range(GLA_SUB):
                sidx = lo + sl
                e = jnp.exp(jnp.minimum(b_i - b[sidx:sidx + 1], 0.0))
                p = jnp.sum(q_i * k[sidx:sidx + 1] * e, axis=1, keepdims=True)
                att = jnp.where((col == sidx) & (row >= sl), p, att)
            att_rows.append(att)
            yield
        atts.append(jnp.concatenate(att_rows, axis=0).astype(bf16))

    sts = [st_ref[h] for h in hs]
    os_ = [_dot(atts[h], vbs[h]) + _dot_nt((qs[h] * jnp.exp(bs[h])).astype(bf16), sts[h].astype(bf16))
           for h in hs]
    yield
    for h in hs:
        b_last = bs[h][c - 1:c]
        kd = (ks[h] * jnp.exp(b_last - bs[h])).astype(bf16)
        st_ref[h] = sts[h] * jnp.exp(b_last) + _dot_tn(vbs[h], kd)
        yield
    for h in hs:
        sv = slice(h * GLA_DV, (h + 1) * GLA_DV)
        o_ref[:, sv] = (_rms(os_[h], nw_ref[...]) * _silu(r_ref[:, sv].astype(f32))).astype(bf16)
        yield

    @pl.when(ti == pl.num_programs(1) - 1)
    def _():
        for h in hs:
            so_ref[h] = st_ref[h].T


def _ssd_kernel(z_ref, xbc_ref, sm_ref, cst_ref, cw_ref, cb_ref, avec_ref, dtb_ref, dvec_ref, nw_ref, h0_ref,
                o_ref, h_ref, ext_ref, nat_ref):
    c = xbc_ref.shape[0]
    assert c == SSD_P, "head pairs share a lane tile: the chunk must be as wide as a head"
    first = pl.program_id(1) == 0

    @pl.when(first)
    def _():
        h_ref[...] = h0_ref[...]

    _conv_load(ext_ref, xbc_ref, cst_ref, first)

    dt_l = jax.nn.softplus(_load_residue(sm_ref, c) + dtb_ref[...])
    tri, _, _ = _tri_masks_residue(c)
    ac_all = _cumsum_rows(dt_l * avec_ref[...], tri)
    ac_t2 = jnp.concatenate([ac_all, ac_all], axis=0).T

    lane = lax.broadcasted_iota(jnp.int32, (c, 2 * SSD_P), 1)
    lo_half = lane < SSD_P
    tri2, _, _ = _tri_masks_residue(c, 2 * SSD_P)
    lane1 = lane[0:1]
    gs_ = range(SSD_G)
    ps_ = range(SSD_HG // 2)

    xs_, bmb, hg, cb2, ch = [], [], [], [], []
    for g in gs_:
        xs_.append(_conv_cols(ext_ref, cw_ref, cb_ref, g * SSD_GW, SSD_GW, c))
        bmb.append(_conv_cols(ext_ref, cw_ref, cb_ref, SSD_INNER + g * SSD_N, SSD_N, c).astype(bf16))
        cmb = _conv_cols(ext_ref, cw_ref, cb_ref, SSD_INNER + SSD_BC + g * SSD_N, SSD_N, c).astype(bf16)
        hg.append(h_ref[pl.ds(g * SSD_HG, SSD_HG)].reshape(SSD_GW, SSD_N))
        cb2.append(_dot_nt(cmb, jnp.concatenate([bmb[g], bmb[g]], axis=0)))
        ch.append(_dot_nt(cmb, hg[g].astype(bf16)))
        yield

    m2, rhs, e_col, xsc = {}, {}, {}, {}
    for g in gs_:
        for p in ps_:
            l0 = SM_DT + g * SSD_HG + 2 * p
            ps = slice(p * 2 * SSD_P, (p + 1) * 2 * SSD_P)
            ac_col = jnp.where(lo_half, ac_all[:, l0:l0 + 1], ac_all[:, l0 + 1:l0 + 2])
            ac_row = jnp.where(lane1 < SSD_P, ac_t2[l0:l0 + 1, :], ac_t2[l0 + 1:l0 + 2, :])
            decay = jnp.exp(jnp.where(tri2, ac_col - ac_row, -jnp.inf))
            m2[g, p] = (cb2[g] * decay).astype(bf16)
            dt2 = jnp.where(lo_half, dt_l[:, l0:l0 + 1], dt_l[:, l0 + 1:l0 + 2])
            xdt = xs_[g][:, ps] * dt2
            rhs[g, p] = jnp.concatenate([jnp.where(lo_half, xdt, 0.0), jnp.where(lo_half, 0.0, xdt)],
                                        axis=0).astype(bf16)
            e_col[g, p] = jnp.exp(ac_col)
            xsc[g, p] = xdt * jnp.exp(ac_col[c - 1:c] - ac_col)
            yield
    y2 = {gp: _dot(m2[gp], rhs[gp]) for gp in m2}
    yield

    for g in gs_:
        xsg = jnp.concatenate([xsc[g, p] for p in ps_], axis=1).astype(bf16)
        dh = _dot_tn(xsg, bmb[g])
        for j in range(SSD_HG):
            ln = SM_DT + g * SSD_HG + j
            h_ref[g * SSD_HG + j] = (hg[g][j * SSD_P:(j + 1) * SSD_P] * jnp.exp(ac_all[c - 1:c, ln:ln + 1])
                                     + dh[j * SSD_P:(j + 1) * SSD_P])
        yield
    for g in gs_:
        gs = slice(g * SSD_GW, (g + 1) * SSD_GW)
        y = jnp.concatenate([y2[g, p] + ch[g][:, p * 2 * SSD_P:(p + 1) * 2 * SSD_P] * e_col[g, p] for p in ps_],
                            axis=1)
        _store_natural(nat_ref, y + dvec_ref[:, gs] * xs_[g], c, g * SSD_GW)
        y = _read_natural(nat_ref, g * SSD_GW, SSD_GW) * _silu(z_ref[:, gs].astype(f32))
        o_ref[:, gs] = _rms(y, nw_ref[:, gs]).astype(bf16)
        yield

    _conv_carry(ext_ref, c)


STEP_B = 8


def _conv_step(x_ref, c_ref, w_ref):
    w = w_ref[...]
    return c_ref[0] * w[0:1] + c_ref[1] * w[1:2] + c_ref[2] * w[2:3] + x_ref[...] * w[3:4]


def _expand_matrix(nb, width):
    r = lax.broadcasted_iota(jnp.int32, (nb, nb * width), 0)
    c = lax.broadcasted_iota(jnp.int32, (nb, nb * width), 1)
    return jnp.where((c >= r * width) & (c < (r + 1) * width), 1.0, 0.0).astype(bf16)


def _bcast_cols(x, e_mat):
    hi, lo = _split(x)
    return _dot_tn(hi, e_mat) + _dot_tn(lo, e_mat)


def _state_call(kernel, grid, in_specs, out_specs, out_shape, scratch, sem, name, args, prev_state):
    aliases = {}
    if prev_state is not None:
        in_specs = in_specs + [pl.BlockSpec(memory_space=pl.ANY)]
        args = args + (prev_state,)
        aliases = {len(args) - 1: len(out_shape) - 1}
        kernel = functools.partial(_drop_alias_ref, kernel, len(args) - 1)
    return pl.pallas_call(kernel, grid=grid, in_specs=in_specs, out_specs=out_specs, out_shape=out_shape,
                          scratch_shapes=scratch, input_output_aliases=aliases,
                          compiler_params=_cp(sem), name=name)(*args)


def _drop_alias_ref(kernel, idx, *refs):
    return kernel(*refs[:idx], *refs[idx + 1:])


def _gdn_step_kernel(alog_ref, dtb_ref, q_ref, k_ref, v_ref, z_ref, sm_ref, cq_ref, ck_ref, cv_ref,
                     wq_ref, wk_ref, wv_ref, nw_ref, s_ref, o_ref, so_ref):
    nb = q_ref.shape[0]
    qa = _silu(_conv_step(q_ref, cq_ref, wq_ref))
    ka = _silu(_conv_step(k_ref, ck_ref, wk_ref))
    va = _silu(_conv_step(v_ref, cv_ref, wv_ref))
    sm = sm_ref[...]
    z = z_ref[...]
    e_mat = _expand_matrix(nb, GDN_DV)
    for h in range(GDN_H):
        sl = slice(h * GDN_DK, (h + 1) * GDN_DK)
        q = qa[:, sl]
        k = ka[:, sl]
        v = va[:, sl]
        q = q * lax.rsqrt(jnp.sum(q * q, axis=-1, keepdims=True) + EPS) * (GDN_DK ** -0.5)
        k = k * lax.rsqrt(jnp.sum(k * k, axis=-1, keepdims=True) + EPS)
        beta = _sigmoid(sm[:, SM_BETA + h:SM_BETA + h + 1])
        g = -jnp.exp(_scalar_vec(alog_ref[h])) * jax.nn.softplus(sm[:, SM_DEC + h:SM_DEC + h + 1] + dtb_ref[h])
        eg = jnp.exp(g)
        qk = jnp.sum(q * k, axis=-1, keepdims=True)
        kb = _bcast_cols(k, e_mat)
        qb = _bcast_cols(q, e_mat)
        blk = lambda m, b: m[:, b * GDN_DV:(b + 1) * GDN_DV]
        ks = jnp.concatenate([jnp.sum(s_ref[b, h] * blk(kb, b), axis=0, keepdims=True) for b in range(nb)], axis=0)
        qs = jnp.concatenate([jnp.sum(s_ref[b, h] * blk(qb, b), axis=0, keepdims=True) for b in range(nb)], axis=0)
        w = beta * v - (beta * eg) * ks
        o = qs * eg + qk * w
        for b in range(nb):
            so_ref[b, h] = s_ref[b, h] * eg[b:b + 1] + blk(kb, b) * w[b:b + 1]
        o = _rms(o, nw_ref[...]) * _silu(z[:, sl])
        o_ref[:, sl] = o.astype(bf16)


def _gdn_step(proj, small, conv_t, states, prev, l, conv_w, a_log, dt_bias, norm_w):
    n = proj.shape[0]
    nb = STEP_B
    blk = lambda off: pl.BlockSpec((nb, GDN_QK), lambda i: (i, _blk(off, GDN_QK)))
    cst = lambda j: pl.BlockSpec((CONV_W - 1, nb, GDN_QK), lambda i: (0, i, j))
    cw = lambda j: pl.BlockSpec((CONV_W, GDN_QK), lambda i: (0, j))
    smem = pl.BlockSpec(memory_space=pltpu.SMEM)
    st = pl.BlockSpec((None, nb, GDN_H, GDN_DK, GDN_DV), lambda i: (l, i, 0, 0, 0))
    return _state_call(
        _gdn_step_kernel, (n // nb,),
        [smem, smem, blk(P_QKV_A), blk(P_QKV_A + GDN_QK), blk(P_QKV_A + 2 * GDN_QK), blk(P_Z_A),
         pl.BlockSpec((nb, LANE), lambda i: (i, 0)),
         cst(0), cst(1), cst(2), cw(0), cw(1), cw(2),
         pl.BlockSpec((1, GDN_DV), lambda i: (0, 0)), st],
        [pl.BlockSpec((nb, GDN_V), lambda i: (i, 0)), st],
        [jax.ShapeDtypeStruct((n, GDN_V), bf16), jax.ShapeDtypeStruct(states.shape, f32)],
        [], ("parallel",), "gdn_step",
        (a_log, dt_bias, proj, proj, proj, proj, small, conv_t, conv_t, conv_t, conv_w, conv_w, conv_w, norm_w, states),
        prev)


def _gla_step_kernel(q_ref, k_ref, v_ref, r_ref, sm_ref, wgate_ref, bgate_ref, nw_ref, s_ref,
                     o_ref, so_ref):
    nb = q_ref.shape[0]
    sm = sm_ref[...]
    lane = lax.broadcasted_iota(jnp.int32, sm.shape, 1)
    lr = jnp.where((lane >= SM_LR) & (lane < SM_LR + GLA_RANK), sm, 0.0).astype(bf16)
    la_all = jax.nn.log_sigmoid(_dot(lr, wgate_ref[...]) + bgate_ref[...]) / GLA_TAU
    qa = q_ref[...] * (GLA_DK ** -0.5)
    ka = k_ref[...]
    va = v_ref[...]
    r = r_ref[...]
    for h in range(GLA_H):
        sk = slice(h * GLA_DK, (h + 1) * GLA_DK)
        sv = slice(h * GLA_DV, (h + 1) * GLA_DV)
        q = qa[:, sk]
        k = ka[:, sk]
        v = va[:, sv]
        e = jnp.exp(la_all[:, sk])
        qk = jnp.sum(q * k, axis=-1, keepdims=True)
        e_t = e.T
        k_t = k.T
        qe_t = (q * e).T
        rows = []
        for b in range(nb):
            s = s_ref[b, h]
            vrow = v[b:b + 1]
            rows.append(qk[b:b + 1] * vrow + jnp.sum(s * qe_t[:, b:b + 1], axis=0, keepdims=True))
            so_ref[b, h] = s * e_t[:, b:b + 1] + k_t[:, b:b + 1] * vrow
        o = jnp.concatenate(rows, axis=0)
        o = _rms(o, nw_ref[...]) * _silu(r[:, sv])
        o_ref[:, sv] = o.astype(bf16)


def _gla_step(proj, small, states, prev, l, wgate_pad, bgate, norm_w):
    n = proj.shape[0]
    nb = STEP_B
    st = pl.BlockSpec((None, nb, GLA_H, GLA_DK, GLA_DV), lambda i: (l, i, 0, 0, 0))
    return _state_call(
        _gla_step_kernel, (n // nb,),
        [pl.BlockSpec((nb, GLA_QK), lambda i: (i, _blk(P_Q_B, GLA_QK))),
         pl.BlockSpec((nb, GLA_QK), lambda i: (i, _blk(P_K_B, GLA_QK))),
         pl.BlockSpec((nb, GLA_V), lambda i: (i, _blk(P_V_B, GLA_V))),
         pl.BlockSpec((nb, GLA_V), lambda i: (i, _blk(P_R_B, GLA_V))),
         pl.BlockSpec((nb, LANE), lambda i: (i, 0)),
         pl.BlockSpec((LANE, GLA_QK), lambda i: (0, 0)),
         pl.BlockSpec((1, GLA_QK), lambda i: (0, 0)),
         pl.BlockSpec((1, GLA_DV), lambda i: (0, 0)), st],
        [pl.BlockSpec((nb, GLA_V), lambda i: (i, 0)), st],
        [jax.ShapeDtypeStruct((n, GLA_V), bf16), jax.ShapeDtypeStruct(states.shape, f32)],
        [], ("parallel",), "gla_step",
        (proj, proj, proj, proj, small, wgate_pad, bgate, norm_w, states),
        prev)


def _ssd_step_kernel(alog_ref, dtb_ref, z_ref, xbc_ref, sm_ref, cst_ref, cw_ref, cb_ref, dvec_ref, nw_ref, h_ref,
                     o_ref, ho_ref):
    nb = xbc_ref.shape[0]
    xbc = _silu(_conv_step(xbc_ref, cst_ref, cw_ref) + cb_ref[...])
    sm_t = sm_ref[...].T
    e_mat = _expand_matrix(nb, SSD_N)
    lane_b = lax.broadcasted_iota(jnp.int32, (SSD_P, nb), 1)
    for g in range(SSD_G):
        gs = slice(g * SSD_GW, (g + 1) * SSD_GW)
        bm = xbc[:, SSD_INNER + g * SSD_N:SSD_INNER + (g + 1) * SSD_N]
        cm_t = xbc[:, SSD_INNER + SSD_BC + g * SSD_N:SSD_INNER + SSD_BC + (g + 1) * SSD_N].T
        cb_row = jnp.sum(cm_t * bm.T, axis=0, keepdims=True)
        hg = h_ref[:, pl.ds(g * SSD_HG, SSD_HG)].reshape(nb * SSD_GW, SSD_N)
        ch = _dot(hg.astype(bf16), cm_t.astype(bf16))
        pairs = []
        for jp in range(SSD_HG // 2):
            x_t = xbc[:, g * SSD_GW + jp * LANE:g * SSD_GW + (jp + 1) * LANE].T
            halves = []
            for jj in range(2):
                j = 2 * jp + jj
                hd = g * SSD_HG + j
                dt_row = jax.nn.softplus(sm_t[SM_DT + hd:SM_DT + hd + 1, :] + dtb_ref[hd])
                ea_row = jnp.exp(dt_row * (-jnp.exp(_scalar_vec(alog_ref[hd]))))
                xdt_t = x_t[jj * SSD_P:(jj + 1) * SSD_P] * dt_row
                hi, lo = _split(xdt_t)
                xb = _dot(hi, e_mat) + _dot(lo, e_mat)
                yh = jnp.zeros((SSD_P, nb), f32)
                for b in range(nb):
                    r0 = (b * SSD_HG + j) * SSD_P
                    yh = jnp.where(lane_b == b, ch[r0:r0 + SSD_P], yh)
                    ho_ref[b, hd] = (h_ref[b, hd] * ea_row[:, b:b + 1]
                                     + xb[:, b * SSD_N:(b + 1) * SSD_N] * bm[b:b + 1])
                halves.append(yh * ea_row + cb_row * xdt_t)
            pairs.append(jnp.concatenate(halves, axis=0).T)
        y = jnp.concatenate(pairs, axis=1)
        y = (y + dvec_ref[:, gs] * xbc[:, gs]) * _silu(z_ref[:, gs])
        o_ref[:, gs] = _rms(y, nw_ref[:, gs]).astype(bf16)


def _ssd_step(proj, small, conv_t, states, prev, l, conv_w, conv_b, a_log, dt_bias, dvec, norm_w):
    n = proj.shape[0]
    nb = STEP_B
    smem = pl.BlockSpec(memory_space=pltpu.SMEM)
    const = lambda shape: pl.BlockSpec(shape, lambda i: (0,) * len(shape))
    st = pl.BlockSpec((None, nb, SSD_H, SSD_P, SSD_N), lambda i: (l, i, 0, 0, 0))
    return _state_call(
        _ssd_step_kernel, (n // nb,),
        [smem, smem,
         pl.BlockSpec((nb, SSD_INNER), lambda i: (i, _blk(P_Z_C, SSD_INNER))),
         pl.BlockSpec((nb, SSD_CONV), lambda i: (i, _blk(P_XBC, SSD_CONV))),
         pl.BlockSpec((nb, LANE), lambda i: (i, 0)),
         pl.BlockSpec((CONV_W - 1, nb, SSD_CONV), lambda i: (0, i, 0)),
         const((CONV_W, SSD_CONV)), const((1, SSD_CONV)), const((1, SSD_INNER)), const((1, SSD_INNER)), st],
        [pl.BlockSpec((nb, SSD_INNER), lambda i: (i, 0)), st],
        [jax.ShapeDtypeStruct((n, SSD_INNER), bf16), jax.ShapeDtypeStruct(states.shape, f32)],
        [], ("parallel",), "ssd_step",
        (a_log, dt_bias, proj, proj, small, conv_t, conv_w, conv_b, dvec, norm_w, states),
        prev)


def _merge_kernel(oa_ref, ob_ref, oc_ref, ga_ref, gb_ref, gc_ref, wa_ref, wb_ref, wc_ref, o_ref):
    m = (_sigmoid(ga_ref[...].astype(f32)) * _dot(oa_ref[...], wa_ref[...])
         + _sigmoid(gb_ref[...].astype(f32)) * _dot(ob_ref[...], wb_ref[...])
         + _sigmoid(gc_ref[...].astype(f32)) * _dot(oc_ref[...], wc_ref[...]))
    o_ref[...] = m.astype(bf16)


def _outproj_kernel(m_ref, x_ref, gt_ref, w_ref, o_ref):
    o_ref[...] = x_ref[...] + gt_ref[...] * _dot(m_ref[...], w_ref[...])


def _mixout(oa, ob, oc, proj, x, mod, wa, wb, wc, wo, l, per_row, tm):
    bx, tx, dm = x.shape
    tn = MERGE_COLS
    gate = lambda k: pl.BlockSpec((None, tm, tn), lambda b, i, j: (b, i, _blk(P_GATES + k * D_MODEL, tn) + j))
    act = lambda w: pl.BlockSpec((None, tm, w), lambda b, i, j: (b, i, 0))
    wsp = lambda w: pl.BlockSpec((None, w, tn), lambda b, i, j: (l, 0, j))
    merged = pl.pallas_call(
        _merge_kernel,
        grid=(bx, tx // tm, dm // tn),
        in_specs=[act(GDN_V), act(GLA_V), act(SSD_INNER), gate(0), gate(1), gate(2),
                  wsp(GDN_V), wsp(GLA_V), wsp(SSD_INNER)],
        out_specs=pl.BlockSpec((None, tm, tn), lambda b, i, j: (b, i, j)),
        out_shape=jax.ShapeDtypeStruct((bx, tx, dm), bf16),
        compiler_params=_cp(("parallel", "parallel", "arbitrary")),
        name="merge",
    )(oa, ob, oc, proj, proj, proj, wa, wb, wc)
    to = OUTPROJ_COLS
    nj = dm // to
    if per_row:
        gspec = pl.BlockSpec((None, tm, to), lambda b, i, j: (b, i, 5 * nj + j))
    else:
        gspec = pl.BlockSpec((None, 1, to), lambda b, i, j: (b, 0, 5 * nj + j))
    return pl.pallas_call(
        _outproj_kernel,
        grid=(bx, tx // tm, nj),
        in_specs=[pl.BlockSpec((None, tm, dm), lambda b, i, j: (b, i, 0)),
                  pl.BlockSpec((None, tm, to), lambda b, i, j: (b, i, j)),
                  gspec,
                  pl.BlockSpec((None, dm, to), lambda b, i, j: (l, 0, j))],
        out_specs=pl.BlockSpec((None, tm, to), lambda b, i, j: (b, i, j)),
        out_shape=jax.ShapeDtypeStruct(x.shape, f32),
        compiler_params=_cp(("parallel", "parallel", "arbitrary")),
        name="outproj",
    )(merged, x, mod, wo)


REGROUP_ROWS = 512


def _regroup_kernel(w_ref, o_ref):
    o_ref[...] = w_ref[0].astype(bf16)


def _regroup_rows(wt):
    nl, _, d = wt.shape
    tr = REGROUP_ROWS
    shifts = []
    for name in MAIN_ORDER:
        a, b = W_IN_SRC[name]
        assert (b - a) % tr == 0 and a % SUB == 0
        shifts.append((P_MAIN[name] // tr, a - P_MAIN[name]))

    def src_row(i):
        off = i * (tr // SUB) + shifts[0][1] // SUB
        for k in range(1, len(shifts)):
            off = off + jnp.where(i >= shifts[k][0], (shifts[k][1] - shifts[k - 1][1]) // SUB, 0)
        return off * SUB

    return pl.pallas_call(
        _regroup_kernel,
        grid=(nl, P_TOTAL // tr),
        in_specs=[pl.BlockSpec((pl.Element(1), pl.Element(tr), pl.Element(d)), lambda l, i: (l, src_row(i), 0))],
        out_specs=pl.BlockSpec((None, tr, d), lambda l, i: (l, i, 0)),
        out_shape=jax.ShapeDtypeStruct((nl, P_TOTAL, d), bf16),
        compiler_params=_cp(("parallel", "parallel")),
        name="regroup_w_in",
    )(wt)


def _permute_w_in(w):
    nl, d, _ = w.shape
    main = _regroup_rows(jnp.swapaxes(w, 1, 2))
    small = jnp.concatenate([w[:, :, W_IN_SRC[name][0]:W_IN_SRC[name][1]] for name in SMALL_ORDER], axis=2)
    small = jnp.pad(jnp.swapaxes(small, 1, 2), ((0, 0), (0, LANE - small.shape[2]), (0, 0)))
    return main, small


def _lane_vec(v, lo):
    return jnp.zeros((1, LANE), f32).at[0, lo:lo + v.shape[0]].set(v)


def _layer_params(l, p):
    row = lambda a: a[l].reshape(1, -1)
    wgate = jnp.zeros((LANE, GLA_QK), f32).at[SM_LR:SM_LR + GLA_RANK].set(p["gla_w_gate"][l]).astype(bf16)
    return dict(
        norm1=row(p["norm1"]), norm2=row(p["norm2"]), norm3=row(p["norm3"]),
        gdn_conv_w=p["gdn_conv_w"][l], gdn_a_log=p["gdn_a_log"][l], gdn_dt_bias=p["gdn_dt_bias"][l],
        gdn_alog_l=_lane_vec(p["gdn_a_log"][l], SM_DEC), gdn_dtb_l=_lane_vec(p["gdn_dt_bias"][l], SM_DEC),
        gdn_norm_w=row(p["gdn_norm_w"]),
        gla_wgate=wgate, gla_bgate=row(p["gla_b_gate"]), gla_norm_w=row(p["gla_norm_w"]),
        ssd_conv_w=p["ssd_conv_w"][l], ssd_conv_b=row(p["ssd_conv_b"]), ssd_a_log=p["ssd_a_log"][l],
        ssd_dt_bias=p["ssd_dt_bias"][l],
        ssd_avec_l=_lane_vec(-jnp.exp(p["ssd_a_log"][l]), SM_DT), ssd_dtb_l=_lane_vec(p["ssd_dt_bias"][l], SM_DT),
        ssd_dvec=jnp.repeat(p["ssd_d"][l], SSD_P).reshape(1, -1),
        ssd_norm_w=row(p["ssd_norm_w"]),
    )


def _stacked_weights(p):
    w_in, w_in_small = _permute_w_in(p["w_in"])
    return dict(
        f1=(p["ffn1_wg"], p["ffn1_wu"], p["ffn1_wd"]),
        f2=(p["ffn2_wg"], p["ffn2_wu"], p["ffn2_wd"]),
        w_in=w_in, w_in_small=w_in_small,
        wa=p["w_branch_gdn"].astype(bf16), wb=p["w_branch_gla"].astype(bf16),
        wc=p["w_branch_ssd"].astype(bf16), w_out=p["w_out"].astype(bf16),
    )


def _new_conv_state(buf, raw):
    t = raw.shape[1]
    k = CONV_W - 1
    if t >= k:
        return raw[:, t - k:]
    return jnp.concatenate([buf[:, t:], raw], axis=1)


def _mixer_prompt(proj, small, lp, st):
    gdn_conv, s_gdn, s_gla, ssd_conv, s_ssd = st
    b, t, _ = proj.shape
    c = CHUNK
    sm_spec = pl.BlockSpec((None, c, LANE), lambda bi, ti: (bi, ti, 0))
    st_a, st_b, st_c = _per_seq(GDN_H, GDN_DK, GDN_DV), _per_seq(GLA_H, GLA_DK, GLA_DV), _per_seq(SSD_H, SSD_P, SSD_N)
    gdn_specs = [_tok(GDN_CONV, P_QKV_A), _tok(GDN_V, P_Z_A), sm_spec, _per_seq(CONV_W - 1, GDN_CONV),
                 _const(CONV_W, GDN_CONV), _const(1, LANE), _const(1, LANE), _const(1, GDN_DV), st_a]
    gdn_args = (proj, proj, small, gdn_conv, lp["gdn_conv_w"], lp["gdn_alog_l"], lp["gdn_dtb_l"],
                lp["gdn_norm_w"], s_gdn)
    gla_specs = [_tok(GLA_QK, P_Q_B), _tok(GLA_QK, P_K_B), _tok(GLA_V, P_V_B), _tok(GLA_V, P_R_B), sm_spec,
                 _const(LANE, GLA_QK), _const(1, GLA_QK), _const(1, GLA_DV), st_b]
    gla_args = (proj, proj, proj, proj, small, lp["gla_wgate"], lp["gla_bgate"], lp["gla_norm_w"], s_gla)
    ssd_specs = [_tok(SSD_INNER, P_Z_C), _tok(SSD_CONV, P_XBC), sm_spec, _per_seq(CONV_W - 1, SSD_CONV),
                 _const(CONV_W, SSD_CONV), _const(1, SSD_CONV), _const(1, LANE), _const(1, LANE),
                 _const(1, SSD_INNER), _const(1, SSD_INNER), st_c]
    ssd_args = (proj, proj, small, ssd_conv, lp["ssd_conv_w"], lp["ssd_conv_b"], lp["ssd_avec_l"],
                lp["ssd_dtb_l"], lp["ssd_dvec"], lp["ssd_norm_w"], s_ssd)
    assert (len(gdn_specs), len(gla_specs), len(ssd_specs)) == (N_GDN_IN, N_GLA_IN, N_SSD_IN)
    out_tok = lambda width: pl.BlockSpec((None, c, width), lambda bi, ti: (bi, ti, 0))
    oa, s_gdn_n, ob, s_gla_n, oc, s_ssd_n = pl.pallas_call(
        _mixers_kernel,
        grid=(b, t // c),
        in_specs=gdn_specs + gla_specs + ssd_specs,
        out_specs=[out_tok(GDN_V), st_a, out_tok(GLA_V), st_b, out_tok(SSD_INNER), st_c],
        out_shape=[jax.ShapeDtypeStruct((b, t, GDN_V), bf16), jax.ShapeDtypeStruct(s_gdn.shape, f32),
                   jax.ShapeDtypeStruct((b, t, GLA_V), bf16), jax.ShapeDtypeStruct(s_gla.shape, f32),
                   jax.ShapeDtypeStruct((b, t, SSD_INNER), bf16), jax.ShapeDtypeStruct(s_ssd.shape, f32)],
        scratch_shapes=[pltpu.VMEM((GDN_CONV // LANE, 8 + c, LANE), f32), pltpu.VMEM((GLA_H, GLA_DV, GLA_DK), f32),
                        pltpu.VMEM((SSD_CONV // LANE, 8 + c, LANE), f32),
                        pltpu.VMEM((GDN_V // LANE, c, LANE), f32), pltpu.VMEM((SSD_INNER // LANE, c, LANE), f32)],
        compiler_params=_cp(("parallel", "arbitrary")),
        name="mixers_prompt",
    )(*gdn_args, *gla_args, *ssd_args)
    gdn_conv_n = _new_conv_state(gdn_conv, proj[:, :, P_QKV_A:P_QKV_A + GDN_CONV]).astype(f32)
    ssd_conv_n = _new_conv_state(ssd_conv, proj[:, :, P_XBC:P_XBC + SSD_CONV]).astype(f32)
    return (oa, ob, oc), (gdn_conv_n, s_gdn_n, s_gla_n, ssd_conv_n, s_ssd_n)


def _mixer_sample(proj, small, lp, l, states, prev):
    n = proj.shape[1]
    p2 = proj.reshape(n, P_TOTAL)
    s2 = small.reshape(n, LANE)
    gdn_conv, ssd_conv = states[0][l], states[3][l]
    gct = jnp.swapaxes(gdn_conv, 0, 1)
    sct = jnp.swapaxes(ssd_conv, 0, 1)
    pv = (None,) * 5 if prev is None else prev
    oa, s_gdn_n = _gdn_step(p2, s2, gct, states[1], pv[1], l, lp["gdn_conv_w"], lp["gdn_a_log"],
                            lp["gdn_dt_bias"], lp["gdn_norm_w"])
    ob, s_gla_n = _gla_step(p2, s2, states[2], pv[2], l, lp["gla_wgate"], lp["gla_bgate"], lp["gla_norm_w"])
    oc, s_ssd_n = _ssd_step(p2, s2, sct, states[4], pv[4], l, lp["ssd_conv_w"], lp["ssd_conv_b"], lp["ssd_a_log"],
                            lp["ssd_dt_bias"], lp["ssd_dvec"], lp["ssd_norm_w"])
    raw = p2.reshape(n, 1, P_TOTAL)
    gdn_conv_n = _new_conv_state(gdn_conv, raw[:, :, P_QKV_A:P_QKV_A + GDN_CONV])
    ssd_conv_n = _new_conv_state(ssd_conv, raw[:, :, P_XBC:P_XBC + SSD_CONV])
    outs = tuple(o.reshape(1, n, -1) for o in (oa, ob, oc))
    return outs, (gdn_conv_n, s_gdn_n, s_gla_n, ssd_conv_n, s_ssd_n)


def _trunk(x, mods, lps, sw, ffn_w, states, per_row, tm, tm_in, final_w):
    nl = len(lps)
    per_layer = []
    prev = None

    def ffn(x, mod, k0, nw, name, l, fw, final):
        if per_row:
            x, ffn_w[l, name] = _ffn_cast(x, mod, k0, nw, *sw[name], l, fw, final)
            return x
        return _ffn(x, mod, k0, nw, ffn_w[l, name], fw, per_row, tm, final)

    for l in range(nl):
        lp, mod = lps[l], mods[l]
        last = l == nl - 1
        x = ffn(x, mod, 0, lp["norm1"], "f1", l, lp["norm1"], False)
        proj, small = _inproj(x, mod, lp["norm2"], sw["w_in"], sw["w_in_small"], l, per_row, tm_in,
                              f32 if per_row else bf16)
        if per_row:
            (oa, ob, oc), st = _mixer_sample(proj, small, lp, l, states, prev)
            prev = st
        else:
            (oa, ob, oc), st = _mixer_prompt(proj, small, lp, tuple(s[l] for s in states))
        per_layer.append(st)
        x = _mixout(oa, ob, oc, proj, x, mod, sw["wa"], sw["wb"], sw["wc"], sw["w_out"], l, per_row, tm_in)
        x = ffn(x, mod, 6, lp["norm3"], "f2", l, final_w if last else lp["norm3"], last)
    stack = lambda i: jnp.stack([st[i] for st in per_layer])
    if per_row:
        new_states = (stack(0), prev[1], prev[2], stack(3), prev[4])
    else:
        new_states = tuple(stack(i) for i in range(5))
    return x, new_states


def kernel(x_prompt, x_sample, state_gdn_conv, state_gdn, state_gla, state_ssd_conv, state_ssd, c_prompt, c_sample, w_ada, b_ada, norm1, norm2, norm3, ffn1_wg, ffn1_wu, ffn1_wd, ffn2_wg, ffn2_wu, ffn2_wd, w_in, gdn_conv_w, gdn_a_log, gdn_dt_bias, gdn_norm_w, gla_w_gate, gla_b_gate, gla_norm_w, ssd_conv_w, ssd_conv_b, ssd_a_log, ssd_dt_bias, ssd_d, ssd_norm_w, w_branch_gdn, w_branch_gla, w_branch_ssd, w_out, final_norm):
    p = dict(norm1=norm1, norm2=norm2, norm3=norm3,
             ffn1_wg=ffn1_wg, ffn1_wu=ffn1_wu, ffn1_wd=ffn1_wd, ffn2_wg=ffn2_wg, ffn2_wu=ffn2_wu, ffn2_wd=ffn2_wd,
             w_in=w_in, gdn_conv_w=gdn_conv_w, gdn_a_log=gdn_a_log, gdn_dt_bias=gdn_dt_bias, gdn_norm_w=gdn_norm_w,
             gla_w_gate=gla_w_gate, gla_b_gate=gla_b_gate, gla_norm_w=gla_norm_w,
             ssd_conv_w=ssd_conv_w, ssd_conv_b=ssd_conv_b, ssd_a_log=ssd_a_log, ssd_dt_bias=ssd_dt_bias,
             ssd_d=ssd_d, ssd_norm_w=ssd_norm_w,
             w_branch_gdn=w_branch_gdn, w_branch_gla=w_branch_gla, w_branch_ssd=w_branch_ssd, w_out=w_out)
    nl = w_ada.shape[0]
    bp, tp, dm = x_prompt.shape
    bs = x_sample.shape[0]
    assert x_sample.shape[1] == 1 and tp % CHUNK == 0 and bs % STEP_B == 0 and dm == D_MODEL
    lps = [_layer_params(l, p) for l in range(nl)]
    sw = _stacked_weights(p)
    fw = final_norm.reshape(1, dm)

    rows = bp + bs
    rpad = -(-rows // 8) * 8
    c_all = jnp.concatenate([c_prompt, c_sample, jnp.zeros((rpad - rows, dm), f32)], axis=0)
    mod = _ada_mod(c_all, w_ada, b_ada)
    mod_p = [mod[l, :bp].reshape(bp, 1, N_MOD * dm) for l in range(nl)]
    mod_s = [mod[l, bp:rows].reshape(1, bs, N_MOD * dm) for l in range(nl)]

    sample_states = (state_gdn_conv, state_gdn, state_gla, state_ssd_conv, state_ssd)
    prompt_states = tuple(jnp.zeros((s.shape[0], bp) + s.shape[2:], x_prompt.dtype) for s in sample_states)
    tm_p = ROW_TILE if tp % ROW_TILE == 0 else CHUNK
    tm_in = ROW_TILE_WIDE if tp % ROW_TILE_WIDE == 0 else tm_p
    ffn_w = {}
    y_s, st_s = _trunk(x_sample.reshape(1, bs, dm), mod_s, lps, sw, ffn_w, sample_states, True, bs, bs, fw)
    y_p, st_p = _trunk(x_prompt, mod_p, lps, sw, ffn_w, prompt_states, False, tm_p, tm_in, fw)
    return (y_p, y_s.reshape(bs, 1, dm)) + st_p + st_s
```
